```python
import math
import jax, jax.numpy as jnp
from jax import lax
import numpy as np

D_MODEL = 1024
BATCH = 8
SEQ = 4096
DEPTH = 1

ATT_HEADS = 8
ATT_HEAD_DIM = 64
ATT_WIDTH = ATT_HEADS * ATT_HEAD_DIM
MOBA_BLOCK = 256
MOBA_TOPK = 3
MOBA_QBLOCK = 16
ROPE_THETA = 10000.0
HGRN_HEADS = 4
HGRN_KEY_DIM = 128
HGRN_VAL_DIM = 128
HGRN_FDIM = HGRN_HEADS * HGRN_KEY_DIM
HGRN_VDIM = HGRN_HEADS * HGRN_VAL_DIM
HGRN_CHUNK = 64
N_EXPERTS = 32
TOP_K = 4
D_EXPERT = 1024
SWIGLU_LIMIT = 7.0
SWIGLU_ALPHA = 1.702
MOE_BLOCK = 128
RMS_EPS = 1e-6
NEG = -1e30

IN_WIDTHS = [ATT_WIDTH, ATT_WIDTH, ATT_WIDTH,
             HGRN_FDIM, HGRN_FDIM, HGRN_VDIM, HGRN_VDIM,
             D_MODEL, D_MODEL]
D_IN = sum(IN_WIDTHS)
IN_SPLITS = list(np.cumsum(IN_WIDTHS)[:-1].tolist())

kernel_name = 'hybrid_moba_hgrn2_moe_block'


def rmsnorm(x, w):
    xf = x.astype(jnp.float32)
    y = xf * lax.rsqrt(jnp.mean(xf * xf, axis=-1, keepdims=True) + RMS_EPS)
    return (y * w.astype(jnp.float32)).astype(x.dtype)


def apply_rope(t, pos):
    dh = t.shape[-1]
    half = dh // 2
    inv = ROPE_THETA ** (-(jnp.arange(half, dtype=jnp.float32) * 2.0 / dh))
    ang = pos.astype(jnp.float32)[:, None] * inv[None, :]
    cos, sin = jnp.cos(ang), jnp.sin(ang)
    tf = t.astype(jnp.float32)
    t1, t2 = tf[..., :half], tf[..., half:]
    out = jnp.concatenate([t1 * cos - t2 * sin, t2 * cos + t1 * sin], axis=-1)
    return out.astype(t.dtype)


def moba_attention(q, k, v):
    B, H, S, Dh = q.shape
    nb = -(-S // MOBA_BLOCK)
    pad = nb * MOBA_BLOCK - S
    kp = jnp.pad(k, ((0, 0), (0, 0), (0, pad), (0, 0)))
    vp = jnp.pad(v, ((0, 0), (0, 0), (0, pad), (0, 0)))
    kb = kp.reshape(B, H, nb, MOBA_BLOCK, Dh)
    vb = vp.reshape(B, H, nb, MOBA_BLOCK, Dh)
    cnt = jnp.minimum(S - jnp.arange(nb) * MOBA_BLOCK, MOBA_BLOCK).astype(jnp.float32)
    kmean = kb.astype(jnp.float32).sum(axis=3) / cnt[None, None, :, None]
    k_sel = min(MOBA_TOPK, nb)
    scale = Dh ** -0.5
    bi = jnp.arange(B)[:, None, None, None]
    hi = jnp.arange(H)[None, :, None, None]

    def one_qblock(qi):
        q0 = qi * MOBA_QBLOCK
        qblk = lax.dynamic_slice_in_dim(q, q0, MOBA_QBLOCK, axis=2).astype(jnp.float32)
        cur = q0 // MOBA_BLOCK
        qpos = q0 + jnp.arange(MOBA_QBLOCK)
        gate = jnp.einsum('bhqd,bhnd->bhqn', qblk, kmean)
        gate = jnp.where(jnp.arange(nb)[None, None, None, :] < cur, gate, NEG)
        _, sel = lax.top_k(gate, k_sel)
        sel_ok = sel < cur
        ks = kb[bi, hi, sel]
        vs = vb[bi, hi, sel]
        s_sel = jnp.einsum('bhqd,bhqjkd->bhqjk', qblk, ks) * scale
        s_sel = jnp.where(sel_ok[..., None], s_sel, NEG).reshape(B, H, MOBA_QBLOCK, k_sel * MOBA_BLOCK)
        ko = lax.dynamic_slice_in_dim(kp, cur * MOBA_BLOCK, MOBA_BLOCK, axis=2)
        vo = lax.dynamic_slice_in_dim(vp, cur * MOBA_BLOCK, MOBA_BLOCK, axis=2)
        kpos = cur * MOBA_BLOCK + jnp.arange(MOBA_BLOCK)
        s_own = jnp.einsum('bhqd,bhkd->bhqk', qblk, ko) * scale
        s_own = jnp.where(kpos[None, :] <= qpos[:, None], s_own, NEG)
        p = jax.nn.softmax(jnp.concatenate([s_sel, s_own], axis=-1), axis=-1)
        p_sel = p[..., :k_sel * MOBA_BLOCK].reshape(B, H, MOBA_QBLOCK, k_sel, MOBA_BLOCK)
        p_own = p[..., k_sel * MOBA_BLOCK:]
        out = (jnp.einsum('bhqjk,bhqjkd->bhqd', p_sel, vs)
               + jnp.einsum('bhqk,bhkd->bhqd', p_own, vo))
        return out.astype(q.dtype)

    outs = lax.map(one_qblock, jnp.arange(S // MOBA_QBLOCK))
    return outs.transpose(1, 2, 0, 3, 4).reshape(B, H, S, Dh)


def hgrn2(q_in, f_in, i_in, g_in, lb, norm_w):
    B, S, _ = q_in.shape
    H, dk, dv, C = HGRN_HEADS, HGRN_KEY_DIM, HGRN_VAL_DIM, HGRN_CHUNK
    nc = S // C
    qf = jax.nn.silu(q_in.astype(jnp.float32))
    f = lb.astype(jnp.float32) + (1.0 - lb.astype(jnp.float32)) * jax.nn.sigmoid(f_in.astype(jnp.float32))
    logf = jnp.log(f)
    kf = 1.0 - f
    iv = i_in.astype(jnp.float32)

    def to_chunks(t, d):
        return t.reshape(B, nc, C, H, d).transpose(1, 0, 3, 2, 4)

    causal = jnp.tril(jnp.ones((C, C), dtype=bool))

    def step(state, xs):
        qc, kc, ic, lc = xs
        b = jnp.cumsum(lc, axis=2)
        diff = b[:, :, :, None, :] - b[:, :, None, :, :]
        decay = jnp.exp(jnp.where(causal[None, None, :, :, None], diff, -jnp.inf))
        attn = jnp.einsum('bhtd,bhsd,bhtsd->bhts', qc, kc, decay)
        o = (jnp.einsum('bhts,bhsv->bhtv', attn, ic)
             + jnp.einsum('bhtd,bhdv->bhtv', qc * jnp.exp(b), state))
        b_last = b[:, :, -1:, :]
        state = (jnp.exp(b_last[:, :, 0, :])[..., None] * state
                 + jnp.einsum('bhsd,bhsv->bhdv', kc * jnp.exp(b_last - b), ic))
        return state, o

    s0 = jnp.zeros((B, H, dk, dv), jnp.float32)
    _, o = lax.scan(step, s0, (to_chunks(qf, dk), to_chunks(kf, dk), to_chunks(iv, dv), to_chunks(logf, dk)))
    o = o.transpose(1, 0, 3, 2, 4).reshape(B, S, H, dv)
    o = o * lax.rsqrt(jnp.mean(o * o, axis=-1, keepdims=True) + RMS_EPS)
    o = o * norm_w.astype(jnp.float32).reshape(H, dv)
    o = o.reshape(B, S, H * dv) * jax.nn.silu(g_in.astype(jnp.float32))
    return o.astype(q_in.dtype)


def moe_ffn(h, router_w, router_b, w_gate_up, b_gate_up, w_down, b_down):
    B, S, D = h.shape
    T = B * S
    A = T * TOP_K
    hf = h.reshape(T, D)
    logits = (hf @ router_w).astype(jnp.float32) + router_b.astype(jnp.float32)
    top_v, top_e = lax.top_k(logits, TOP_K)
    top_w = jax.nn.softmax(top_v, axis=-1)
    e_flat = top_e.reshape(A)
    w_flat = top_w.reshape(A)
    tok_flat = jnp.repeat(jnp.arange(T, dtype=jnp.int32), TOP_K)
    order = jnp.argsort(e_flat)
    e_sorted = e_flat[order]
    counts = jnp.bincount(e_flat, length=N_EXPERTS)
    start = jnp.cumsum(counts) - counts
    padded = (counts + MOE_BLOCK - 1) // MOE_BLOCK * MOE_BLOCK
    pend = jnp.cumsum(padded)
    pstart = pend - padded
    dest = pstart[e_sorted] + (jnp.arange(A) - start[e_sorted])
    P = A + N_EXPERTS * MOE_BLOCK
    nblk = P // MOE_BLOCK
    row_tok = jnp.zeros((P,), jnp.int32).at[dest].set(tok_flat[order])
    row_w = jnp.zeros((P,), jnp.float32).at[dest].set(w_flat[order])
    blk_e = jnp.minimum(jnp.sum((jnp.arange(nblk) * MOE_BLOCK)[:, None] >= pend[None, :], axis=1), N_EXPERTS - 1)
    xg = hf[row_tok].reshape(nblk, MOE_BLOCK, D)

    def expert_block(args):
        xb, e = args
        gu = xb @ w_gate_up[e] + b_gate_up[e]
        gate = jnp.minimum(gu[:, 0::2], SWIGLU_LIMIT)
        up = jnp.clip(gu[:, 1::2], -SWIGLU_LIMIT, SWIGLU_LIMIT)
        glu = gate * jax.nn.sigmoid(gate * SWIGLU_ALPHA)
        return ((up + 1.0) * glu) @ w_down[e] + b_down[e]

    yg = lax.map(expert_block, (xg, blk_e)).reshape(P, D)
    out = jnp.zeros((T, D), jnp.float32).at[row_tok].add(yg.astype(jnp.float32) * row_w[:, None])
    return out.astype(h.dtype).reshape(B, S, D)


def setup_inputs(seed: int = 0) -> dict:
    key = jax.random.key(seed)
    ks = jax.random.split(key, 16)
    f32 = jnp.float32
    nrm = lambda k, shape, s: jax.random.normal(k, shape, f32) * s
    return {
        'x': jax.random.normal(ks[0], (BATCH, SEQ, D_MODEL), f32),
        'norm1_w': 1.0 + nrm(ks[1], (DEPTH, D_MODEL), 0.02),
        'w_in': nrm(ks[2], (DEPTH, D_MODEL, D_IN), D_MODEL ** -0.5),
        'moba_up': nrm(ks[3], (DEPTH, ATT_WIDTH, D_MODEL), ATT_WIDTH ** -0.5),
        'hgrn_lb_logits': nrm(ks[4], (DEPTH + 1, HGRN_FDIM), 0.1),
        'hgrn_norm_w': 1.0 + nrm(ks[5], (DEPTH, HGRN_VDIM), 0.02),
        'hgrn_up': nrm(ks[6], (DEPTH, HGRN_VDIM, D_MODEL), HGRN_VDIM ** -0.5),
        'w_out': nrm(ks[7], (DEPTH, D_MODEL, D_MODEL), D_MODEL ** -0.5),
        'norm2_w': 1.0 + nrm(ks[8], (DEPTH, D_MODEL), 0.02),
        'router_w': nrm(ks[9], (DEPTH, D_MODEL, N_EXPERTS), D_MODEL ** -0.5),
        'router_b': nrm(ks[10], (DEPTH, N_EXPERTS), 0.01),
        'w_gate_up': nrm(ks[11], (DEPTH, N_EXPERTS, D_MODEL, 2 * D_EXPERT), D_MODEL ** -0.5),
        'b_gate_up': nrm(ks[12], (DEPTH, N_EXPERTS, 2 * D_EXPERT), 0.01),
        'w_down': nrm(ks[13], (DEPTH, N_EXPERTS, D_EXPERT, D_MODEL), D_EXPERT ** -0.5),
        'b_down': nrm(ks[14], (DEPTH, N_EXPERTS, D_MODEL), 0.01),
        'final_norm_w': 1.0 + nrm(ks[15], (D_MODEL,), 0.02),
    }


def reference(x, norm1_w, w_in, moba_up, hgrn_lb_logits, hgrn_norm_w, hgrn_up, w_out,
              norm2_w, router_w, router_b, w_gate_up, b_gate_up, w_down, b_down, final_norm_w):
    B, S, D = x.shape
    pos = jnp.arange(S, dtype=jnp.int32)
    lb_table = jnp.cumsum(jax.nn.softmax(hgrn_lb_logits.astype(jnp.float32), axis=0), axis=0)

    def heads(t):
        return t.reshape(B, S, ATT_HEADS, ATT_HEAD_DIM).transpose(0, 2, 1, 3)

    for l in range(DEPTH):
        h = rmsnorm(x, norm1_w[l])
        proj = h @ w_in[l]
        q_a, k_a, v_a, q_b, f_b, i_b, g_b, gate_a, gate_b = jnp.split(proj, IN_SPLITS, axis=-1)
        q_a = apply_rope(heads(q_a), pos)
        k_a = apply_rope(heads(k_a), pos)
        y_a = moba_attention(q_a, k_a, heads(v_a)).transpose(0, 2, 1, 3).reshape(B, S, ATT_WIDTH)
        y_b = hgrn2(q_b, f_b, i_b, g_b, lb_table[l], hgrn_norm_w[l])
        mixed = (jax.nn.sigmoid(gate_a) * (y_a @ moba_up[l])
                 + jax.nn.sigmoid(gate_b) * (y_b @ hgrn_up[l]))
        x = x + mixed @ w_out[l]
        h2 = rmsnorm(x, norm2_w[l])
        x = x + moe_ffn(h2, router_w[l], router_b[l], w_gate_up[l], b_gate_up[l], w_down[l], b_down[l])
    return rmsnorm(x, final_norm_w)
```

```python
import functools
import math

import numpy as np
import jax
import jax.numpy as jnp
from jax import lax
from jax.experimental import pallas as pl
from jax.experimental.pallas import tpu as pltpu

ATT_HEADS = 8
ATT_HEAD_DIM = 64
ATT_WIDTH = ATT_HEADS * ATT_HEAD_DIM
MOBA_BLOCK = 256
MOBA_TOPK = 3
ROPE_THETA = 10000.0
HGRN_HEADS = 4
HGRN_DIM = 128
HGRN_WIDTH = HGRN_HEADS * HGRN_DIM
HGRN_CHUNK = 64
N_EXPERTS = 32
TOP_K = 4
SWIGLU_LIMIT = 7.0
SWIGLU_ALPHA = 1.702
RMS_EPS = 1e-6
NEG = -1e30

V7X_LANES = 128
V7X_VMEM_LIMIT_BYTES = 56 * 1024 * 1024

PROJ_ROWS = 512
PROJ_COLS = 512
HGRN_ROWS = 256
MIX_ROWS = 512
EXPERT_ROWS = 256
DMA_TOKENS = 512
COMBINE_ROWS = 256

F32 = jnp.float32
BF16 = jnp.bfloat16


def _nt_dot(a, b, precision=None):
    return lax.dot_general(a, b, (((1,), (1,)), ((), ())), precision=precision,
                           preferred_element_type=F32)


def _tn_dot(a, b, precision=None):
    return lax.dot_general(a, b, (((0,), (0,)), ((), ())), precision=precision,
                           preferred_element_type=F32)


def _cparams(*sem):
    return pltpu.CompilerParams(dimension_semantics=sem, vmem_limit_bytes=V7X_VMEM_LIMIT_BYTES)


def _in_proj_kernel(x_ref, nw_ref, w_ref, cos_ref, sin_ref,
                    qa_ref, ka_ref, va_ref, qb_ref, fb_ref, ib_ref, gb_ref, ga_ref, gtb_ref):
    x = x_ref[...]
    h = x * lax.rsqrt(jnp.mean(x * x, axis=-1, keepdims=True) + RMS_EPS) * nw_ref[...]
    h = h.astype(BF16)
    cos = cos_ref[...]
    sin = sin_ref[...]

    def proj(c):
        return jnp.dot(h, w_ref[:, c * PROJ_COLS:(c + 1) * PROJ_COLS], preferred_element_type=F32)

    def rope(t):
        out = []
        for j in range(PROJ_COLS // V7X_LANES):
            tj = t[:, j * V7X_LANES:(j + 1) * V7X_LANES]
            out.append(tj * cos + pltpu.roll(tj, V7X_LANES // 2, 1) * sin)
        return jnp.concatenate(out, axis=1)

    qa_ref[...] = (rope(proj(0)) * (ATT_HEAD_DIM ** -0.5)).astype(BF16)
    ka_ref[...] = rope(proj(1)).astype(BF16)
    va_ref[...] = proj(2).astype(BF16)
    qb_ref[...] = proj(3).astype(BF16)
    fb_ref[...] = proj(4)
    ib_ref[...] = proj(5).astype(BF16)
    gb_ref[...] = proj(6).astype(BF16)
    ga_ref[:, :PROJ_COLS] = proj(7).astype(BF16)
    ga_ref[:, PROJ_COLS:] = proj(8).astype(BF16)
    gtb_ref[:, :PROJ_COLS] = proj(9).astype(BF16)
    gtb_ref[:, PROJ_COLS:] = proj(10).astype(BF16)


def _in_proj(x2, norm_w, w_in_bf16, cos_t, sin_t, seq):
    T, D = x2.shape
    tm = min(PROJ_ROWS, seq)
    n_seq_tiles = seq // tm
    row = lambda w: pl.BlockSpec((tm, w), lambda i: (i, 0))
    tab = pl.BlockSpec((tm, V7X_LANES), lambda i: (i % n_seq_tiles, 0))
    widths = [ATT_WIDTH] * 3 + [HGRN_WIDTH] * 4 + [D, D]
    dtypes = [BF16, BF16, BF16, BF16, F32, BF16, BF16, BF16, BF16]
    return pl.pallas_call(
        _in_proj_kernel,
        grid=(T // tm,),
        in_specs=[row(D), pl.BlockSpec((1, D), lambda i: (0, 0)),
                  pl.BlockSpec(memory_space=pltpu.VMEM), tab, tab],
        out_specs=[row(w) for w in widths],
        out_shape=[jax.ShapeDtypeStruct((T, w), dt) for w, dt in zip(widths, dtypes)],
        compiler_params=_cparams("parallel"),
        name="in_proj",
    )(x2, norm_w, w_in_bf16, cos_t, sin_t)


def _moba_kernel(q_ref, k_ref, v_ref, o_ref, kaug_ref, kmean_ref, *, nb):
    qi = pl.program_id(2)
    blk = MOBA_BLOCK
    lanes = V7X_LANES

    @pl.when(qi == 0)
    def _():
        kaug_ref[:, :lanes] = k_ref[0]
        rowb = lax.broadcasted_iota(jnp.int32, (nb * blk, lanes), 0) // blk
        col = lax.broadcasted_iota(jnp.int32, (nb * blk, lanes), 1)
        kaug_ref[:, lanes:] = jnp.where(rowb == col, 1.0, 0.0).astype(BF16)
        kmean_ref[...] = jnp.zeros_like(kmean_ref)
        for n in range(nb):
            kb = k_ref[0, n * blk:(n + 1) * blk, :].astype(F32)
            kmean_ref[n:n + 1, :] = jnp.sum(kb, axis=0, keepdims=True) * (1.0 / blk)

    q = q_ref[0]
    lane = lax.broadcasted_iota(jnp.int32, (blk, lanes), 1)
    quarter = lane // (ATT_HEAD_DIM // 2)
    row_i = lax.broadcasted_iota(jnp.int32, (blk, blk), 0)
    col_i = lax.broadcasted_iota(jnp.int32, (blk, blk), 1)
    k_own = k_ref[0, pl.ds(pl.multiple_of(qi * blk, blk), blk), :]
    v_own = v_ref[0, pl.ds(pl.multiple_of(qi * blk, blk), blk), :]
    nbp = kmean_ref.shape[0]
    blk_id = lax.broadcasted_iota(jnp.int32, (nbp, blk), 0)

    outs = []
    for hh in range(2):
        qh = jnp.where((quarter % 2) == hh, q, jnp.zeros_like(q))
        gate = _nt_dot(kmean_ref[...], qh.astype(F32), precision=lax.Precision.HIGHEST)
        gate = jnp.where(blk_id < qi, gate, NEG)
        beaten = jnp.zeros((nbp, blk), F32)
        for n in range(nb):
            gn = gate[n:n + 1, :]
            wins = (gn > gate) | ((gn == gate) & (blk_id > n))
            beaten = beaten + jnp.where(wins, 1.0, 0.0)
        sel = (beaten < MOBA_TOPK) & (blk_id < qi)
        bias_t = jnp.where(sel, 0.0, NEG)
        bias_t = jnp.concatenate([bias_t, jnp.zeros((lanes - nbp, blk), F32)], axis=0)
        q_aug = jnp.concatenate([qh, bias_t.T.astype(BF16)], axis=1)

        s = _nt_dot(qh, k_own)
        s = jnp.where(col_i <= row_i, s, NEG)
        m0 = jnp.max(s, axis=1, keepdims=True)
        p = jnp.exp(s - m0)
        l0 = jnp.sum(p, axis=1, keepdims=True)
        acc0 = jnp.dot(p.astype(BF16), v_own, preferred_element_type=F32)

        def body(j, carry):
            m, l, acc = carry
            off = pl.multiple_of(j * blk, blk)
            s = _nt_dot(q_aug, kaug_ref[pl.ds(off, blk), :])
            m_new = jnp.maximum(m, jnp.max(s, axis=1, keepdims=True))
            alpha = jnp.exp(m - m_new)
            p = jnp.exp(s - m_new)
            l = alpha * l + jnp.sum(p, axis=1, keepdims=True)
            acc = alpha * acc + jnp.dot(p.astype(BF16), v_ref[0, pl.ds(off, blk), :],
                                        preferred_element_type=F32)
            return m_new, l, acc

        _, l, acc = lax.fori_loop(0, qi, body, (m0, l0, acc0))
        outs.append(acc / l)

    o_ref[0] = jnp.where(lane < ATT_HEAD_DIM, outs[0], outs[1]).astype(BF16)


def _moba(q, k, v):
    B, S, _ = q.shape
    nb = S // MOBA_BLOCK
    nbp = -(-nb // 8) * 8
    n_pairs = ATT_WIDTH // V7X_LANES
    qspec = pl.BlockSpec((1, MOBA_BLOCK, V7X_LANES), lambda b, p, i: (b, i, p))
    kvspec = pl.BlockSpec((1, S, V7X_LANES), lambda b, p, i: (b, 0, p))
    return pl.pallas_call(
        functools.partial(_moba_kernel, nb=nb),
        grid=(B, n_pairs, nb),
        in_specs=[qspec, kvspec, kvspec],
        out_specs=qspec,
        out_shape=jax.ShapeDtypeStruct((B, S, ATT_WIDTH), BF16),
        scratch_shapes=[pltpu.VMEM((S, 2 * V7X_LANES), BF16), pltpu.VMEM((nbp, V7X_LANES), F32)],
        compiler_params=_cparams("parallel", "parallel", "arbitrary"),
        name="moba",
    )(q, k, v)


def _hgrn_level_sizes(chunk):
    return [chunk >> (i + 1) for i in range(int(math.log2(chunk)))]


def _hgrn_constants(chunk):
    t = np.arange(chunk)
    mats = [(t[None, :] <= t[:, None]),
            (t[None, :] > t[:, None])]
    qrows, pmasks = [], []
    for bs in _hgrn_level_sizes(chunk):
        blk = t // bs
        odd = (blk % 2) == 1
        lo, hi = blk * bs, (blk + 1) * bs
        u = t[None, :]
        m_odd = (u >= lo[:, None]) & (u <= t[:, None])
        m_even = (u > t[:, None]) & (u < hi[:, None])
        mats.append(np.where(odd[:, None], m_odd, m_even))
        qrows.append(odd)
        pmasks.append(odd[:, None] & (blk[None, :] == blk[:, None] - 1))
    pmasks.append(t[None, :] == t[:, None])
    summat = np.concatenate(mats, axis=0).astype(np.float32)
    qrow = np.stack(qrows, axis=0).astype(np.float32)
    pmask = np.stack(pmasks, axis=0).astype(np.float32)
    return summat, qrow, pmask


def _hgrn_kernel(q_ref, f_ref, i_ref, g_ref, lb_ref, nw_ref, sm_ref, qrow_ref, pm_ref,
                 o_ref, state_ref, *, rows):
    C = HGRN_CHUNK
    n_levels = qrow_ref.shape[0]

    @pl.when(pl.program_id(1) == 0)
    def _():
        state_ref[...] = jnp.zeros_like(state_ref)

    lb = lb_ref[...]
    summat = sm_ref[...]
    for c in range(rows // C):
        r0 = c * C
        fg = lb + (1.0 - lb) * jax.nn.sigmoid(f_ref[0, r0:r0 + C, :])
        logf = jnp.log(fg)
        sums = jnp.dot(summat, logf, precision=lax.Precision.HIGHEST,
                       preferred_element_type=F32)
        for h in range(HGRN_HEADS):
            ls = slice(h * HGRN_DIM, (h + 1) * HGRN_DIM)
            qf = jax.nn.silu(q_ref[0, r0:r0 + C, ls].astype(F32))
            kf = 1.0 - fg[:, ls]
            iv = i_ref[0, r0:r0 + C, ls]
            bcum = sums[0:C, ls]
            bsuf = sums[C:2 * C, ls]
            att = _nt_dot(qf.astype(BF16), kf.astype(BF16)) * pm_ref[n_levels]
            for lv in range(n_levels):
                w = jnp.exp(sums[(2 + lv) * C:(3 + lv) * C, ls])
                qrow = qrow_ref[lv]
                z = (jnp.where(qrow > 0.5, qf, kf) * w).astype(BF16)
                att = att + _nt_dot(z, z) * pm_ref[lv]
            o = jnp.dot(att.astype(BF16), iv, preferred_element_type=F32)
            st = state_ref[h]
            o = o + _nt_dot((qf * jnp.exp(bcum)).astype(BF16), st.astype(BF16))
            kdec = (kf * jnp.exp(bsuf)).astype(BF16)
            state_ref[h] = st * jnp.exp(bcum[C - 1:C, :]) + _tn_dot(iv, kdec)
            o = o * lax.rsqrt(jnp.mean(o * o, axis=-1, keepdims=True) + RMS_EPS)
            o = o * nw_ref[:, ls] * jax.nn.silu(g_ref[0, r0:r0 + C, ls].astype(F32))
            o_ref[0, r0:r0 + C, ls] = o.astype(BF16)


def _hgrn(qb, fb, ib, gb, lb, norm_w):
    B, S, W = qb.shape
    rows = min(HGRN_ROWS, S)
    summat, qrow, pmask = _hgrn_constants(HGRN_CHUNK)
    n_levels = qrow.shape[0]
    blk = pl.BlockSpec((1, rows, W), lambda b, s: (b, s, 0))
    vec = pl.BlockSpec((1, W), lambda b, s: (0, 0))
    const = lambda a: pl.BlockSpec(a.shape, lambda b, s: (0,) * a.ndim)
    qrow3 = qrow.reshape(n_levels, HGRN_CHUNK, 1)
    return pl.pallas_call(
        functools.partial(_hgrn_kernel, rows=rows),
        grid=(B, S // rows),
        in_specs=[blk, blk, blk, blk, vec, vec, const(summat), const(qrow3), const(pmask)],
        out_specs=blk,
        out_shape=jax.ShapeDtypeStruct((B, S, W), BF16),
        scratch_shapes=[pltpu.VMEM((HGRN_HEADS, HGRN_DIM, HGRN_DIM), F32)],
        compiler_params=_cparams("parallel", "arbitrary"),
        name="hgrn",
    )(qb, fb, ib, gb, lb, norm_w, jnp.asarray(summat), jnp.asarray(qrow3), jnp.asarray(pmask))


def _mix_route_kernel(ya_ref, yb_ref, ga_ref, gb_ref, x_ref, wa_ref, wb_ref, wo_ref, nw_ref,
                      rw_ref, rb_ref, tri_ref,
                      x1_ref, h2_ref, e_ref, w_ref, rank_ref, cnt_ref, carry_ref):
    @pl.when(pl.program_id(0) == 0)
    def _():
        carry_ref[...] = jnp.zeros_like(carry_ref)

    ua = jnp.dot(ya_ref[...], wa_ref[...], preferred_element_type=F32)
    ub = jnp.dot(yb_ref[...], wb_ref[...], preferred_element_type=F32)
    mixed = (jax.nn.sigmoid(ga_ref[...].astype(F32)) * ua
             + jax.nn.sigmoid(gb_ref[...].astype(F32)) * ub)
    x1 = x_ref[...] + jnp.dot(mixed.astype(BF16), wo_ref[...], preferred_element_type=F32)
    x1_ref[...] = x1
    h2 = x1 * lax.rsqrt(jnp.mean(x1 * x1, axis=-1, keepdims=True) + RMS_EPS) * nw_ref[...]
    h2_ref[...] = h2

    tm = x1.shape[0]
    logits = _nt_dot(rw_ref[...], h2, precision=lax.Precision.HIGHEST) + rb_ref[...]
    eid = lax.broadcasted_iota(jnp.int32, (N_EXPERTS, tm), 0)
    work = logits
    es, vs = [], []
    for _ in range(TOP_K):
        mx = jnp.max(work, axis=0, keepdims=True)
        idx = jnp.min(jnp.where(work == mx, eid, N_EXPERTS), axis=0, keepdims=True)
        es.append(idx)
        vs.append(mx)
        work = jnp.where(eid == idx, -jnp.inf, work)
    ex = [jnp.exp(v - vs[0]) for v in vs]
    den = ex[0] + ex[1] + ex[2] + ex[3]
    multi = jnp.zeros((N_EXPERTS, tm), F32)
    for k in range(TOP_K):
        multi = multi + jnp.where(eid == es[k], 1.0, 0.0)
    before = jnp.dot(multi.astype(BF16), tri_ref[...], preferred_element_type=F32) + carry_ref[...]
    for k in range(TOP_K):
        e_ref[k:k + 1, :] = es[k]
        w_ref[k:k + 1, :] = ex[k] / den
        rank_ref[k:k + 1, :] = jnp.sum(jnp.where(eid == es[k], before, 0.0), axis=0,
                                       keepdims=True).astype(jnp.int32)
    carry_ref[...] = carry_ref[...] + jnp.sum(multi, axis=1, keepdims=True)
    cnt_ref[...] = jnp.broadcast_to(carry_ref[...], cnt_ref.shape)


def _mix_route(ya, yb, ga, gb, x2, wa, wb, wo, norm_w, rw_t, rb):
    T, D = x2.shape
    tm = min(MIX_ROWS, T)
    row = lambda w: pl.BlockSpec((tm, w), lambda i: (i, 0))
    whole = pl.BlockSpec(memory_space=pltpu.VMEM)
    kt = pl.BlockSpec((TOP_K, tm), lambda i: (0, i))
    tri = jnp.asarray(np.triu(np.ones((tm, tm), np.float32), 1), BF16)
    return pl.pallas_call(
        _mix_route_kernel,
        grid=(T // tm,),
        in_specs=[row(ATT_WIDTH), row(HGRN_WIDTH), row(D), row(D), row(D),
                  whole, whole, whole, pl.BlockSpec((1, D), lambda i: (0, 0)),
                  whole, whole, whole],
        out_specs=[row(D), row(D), kt, kt, kt,
                   pl.BlockSpec((N_EXPERTS, V7X_LANES), lambda i: (0, 0))],
        out_shape=[jax.ShapeDtypeStruct((T, D), F32), jax.ShapeDtypeStruct((T, D), F32),
                   jax.ShapeDtypeStruct((TOP_K, T), jnp.int32),
                   jax.ShapeDtypeStruct((TOP_K, T), F32),
                   jax.ShapeDtypeStruct((TOP_K, T), jnp.int32),
                   jax.ShapeDtypeStruct((N_EXPERTS, V7X_LANES), F32)],
        scratch_shapes=[pltpu.VMEM((N_EXPERTS, 1), F32)],
        compiler_params=_cparams("arbitrary"),
        name="mix_route",
    )(ya, yb, ga, gb, x2, wa, wb, wo, norm_w, rw_t, rb, tri)


def _row_copy(src, si, dst, di, sem):
    return pltpu.make_async_copy(src.at[pl.ds(si, 1)], dst.at[pl.ds(di, 1)], sem)


def _scatter_rows_kernel(dest_ref, src_ref, dst_ref, sem, *, tokens):
    base = pl.program_id(0) * tokens

    def issue(t, c):
        for k in range(TOP_K):
            _row_copy(src_ref, base + t, dst_ref, dest_ref[k * tokens + t], sem).start()
        return c

    lax.fori_loop(0, tokens, issue, 0)

    def drain(t, c):
        for k in range(TOP_K):
            _row_copy(src_ref, 0, dst_ref, 0, sem).wait()
        return c

    lax.fori_loop(0, tokens, drain, 0)


def _gather_rows_kernel(dest_ref, src_ref, dst_ref, sem, *, tokens, total):
    base = pl.program_id(0) * tokens

    def issue(t, c):
        for k in range(TOP_K):
            _row_copy(src_ref, dest_ref[k * tokens + t], dst_ref, k * total + base + t, sem).start()
        return c

    lax.fori_loop(0, tokens, issue, 0)

    def drain(t, c):
        for k in range(TOP_K):
            _row_copy(src_ref, 0, dst_ref, 0, sem).wait()
        return c

    lax.fori_loop(0, tokens, drain, 0)


def _tile_dest(dest, tokens):
    T = dest.shape[1]
    return dest.reshape(TOP_K, T // tokens, tokens).transpose(1, 0, 2).reshape(-1)


def _row_dma_call(kernel, dest, src, out_rows, tokens, name):
    T = dest.shape[1]
    D = src.shape[1]
    return pl.pallas_call(
        kernel,
        grid=(T // tokens,),
        in_specs=[pl.BlockSpec((TOP_K * tokens,), lambda i: (i,), memory_space=pltpu.SMEM),
                  pl.BlockSpec(memory_space=pl.ANY)],
        out_specs=pl.BlockSpec(memory_space=pl.ANY),
        out_shape=jax.ShapeDtypeStruct((out_rows, D), src.dtype),
        scratch_shapes=[pltpu.SemaphoreType.DMA],
        compiler_params=_cparams("arbitrary"),
        name=name,
    )(_tile_dest(dest, tokens), src)


def _scatter_rows(dest, h2):
    T = dest.shape[1]
    tokens = min(DMA_TOKENS, T)
    return _row_dma_call(functools.partial(_scatter_rows_kernel, tokens=tokens),
                         dest, h2, TOP_K * T, tokens, "scatter_rows")


def _gather_rows(dest, yg):
    T = dest.shape[1]
    tokens = min(DMA_TOKENS, T)
    return _row_dma_call(functools.partial(_gather_rows_kernel, tokens=tokens, total=T),
                         dest, yg, TOP_K * T, tokens, "gather_rows")


def _experts_kernel(blk_ref, exp_ref, lo_ref, hi_ref, first_ref,
                    x_ref, wg_ref, wu_ref, bg_ref, bu_ref, wd_ref, bd_ref, y_ref):
    w = pl.program_id(0)
    lo = lo_ref[w]
    hi = hi_ref[w]

    @pl.when(lo < hi)
    def _():
        x = x_ref[...].astype(BF16)
        g = jnp.dot(x, wg_ref[0], preferred_element_type=F32) + bg_ref[0]
        u = jnp.dot(x, wu_ref[0], preferred_element_type=F32) + bu_ref[0]
        g = jnp.minimum(g, SWIGLU_LIMIT)
        u = jnp.clip(u, -SWIGLU_LIMIT, SWIGLU_LIMIT)
        act = (u + 1.0) * (g * jax.nn.sigmoid(g * SWIGLU_ALPHA))
        y = jnp.dot(act.astype(BF16), wd_ref[0], preferred_element_type=F32) + bd_ref[0]
        r = lax.broadcasted_iota(jnp.int32, y.shape, 0)
        mine = (r >= lo) & (r < hi)

        @pl.when(first_ref[w] == 1)
        def _():
            y_ref[...] = jnp.where(mine, y, 0.0)

        @pl.when(first_ref[w] == 0)
        def _():
            y_ref[...] = jnp.where(mine, y, y_ref[...])


def _work_items(counts, n_rows, tm):
    nblk = n_rows // tm
    n_items = nblk + N_EXPERTS - 1
    end = jnp.cumsum(counts)
    start = end - counts
    fb = start // tm
    nitems = jnp.where(counts > 0, (end - 1) // tm - fb + 1, 0)
    item_end = jnp.cumsum(nitems)
    item_start = item_end - nitems
    w = jnp.arange(n_items, dtype=jnp.int32)
    valid = w < item_end[-1]
    wc = jnp.minimum(w, item_end[-1] - 1)
    e = jnp.sum(wc[:, None] >= item_end[None, :], axis=1).astype(jnp.int32)
    e = jnp.minimum(e, N_EXPERTS - 1)
    blk = (fb[e] + (wc - item_start[e])).astype(jnp.int32)
    lo = jnp.maximum(start[e], blk * tm) - blk * tm
    hi = jnp.minimum(end[e], (blk + 1) * tm) - blk * tm
    lo = jnp.where(valid, lo, 0).astype(jnp.int32)
    hi = jnp.where(valid, hi, 0).astype(jnp.int32)
    first = jnp.concatenate([jnp.ones((1,), jnp.int32),
                             (blk[1:] != blk[:-1]).astype(jnp.int32)])
    return blk, e, lo, hi, first


def _experts(xg, counts, wg, wu, bg, bu, wd, bd):
    A, D = xg.shape
    tm = min(EXPERT_ROWS, A)
    F = wg.shape[2]
    items = _work_items(counts, A, tm)
    n_items = A // tm + N_EXPERTS - 1
    xs = pl.BlockSpec((tm, D), lambda w, blk, e, lo, hi, fi: (blk[w], 0))
    wsp = lambda r, c: pl.BlockSpec((1, r, c), lambda w, blk, e, lo, hi, fi: (e[w], 0, 0))
    return pl.pallas_call(
        _experts_kernel,
        grid_spec=pltpu.PrefetchScalarGridSpec(
            num_scalar_prefetch=5,
            grid=(n_items,),
            in_specs=[xs, wsp(D, F), wsp(D, F), wsp(1, F), wsp(1, F), wsp(F, D), wsp(1, D)],
            out_specs=xs,
        ),
        out_shape=jax.ShapeDtypeStruct((A, D), F32),
        compiler_params=_cparams("arbitrary"),
        name="experts",
    )(*items, xg, wg, wu, bg, bu, wd, bd)


def _combine_kernel(x1_ref, yk_ref, w_ref, nw_ref, o_ref):
    w = w_ref[...]
    x = x1_ref[...]
    for k in range(TOP_K):
        x = x + yk_ref[k] * w[:, k:k + 1]
    o_ref[...] = x * lax.rsqrt(jnp.mean(x * x, axis=-1, keepdims=True) + RMS_EPS) * nw_ref[...]


def _combine(x1, yk, w_tk, norm_w):
    T, D = x1.shape
    tm = min(COMBINE_ROWS, T)
    row = pl.BlockSpec((tm, D), lambda i: (i, 0))
    return pl.pallas_call(
        _combine_kernel,
        grid=(T // tm,),
        in_specs=[row, pl.BlockSpec((TOP_K, tm, D), lambda i: (0, i, 0)),
                  pl.BlockSpec((tm, TOP_K), lambda i: (i, 0)),
                  pl.BlockSpec((1, D), lambda i: (0, 0))],
        out_specs=row,
        out_shape=jax.ShapeDtypeStruct((T, D), F32),
        compiler_params=_cparams("parallel"),
        name="combine",
    )(x1, yk, w_tk, norm_w)


def _qk_column_order():
    half = ATT_HEAD_DIM // 2
    order = []
    for p in range(ATT_HEADS // 2):
        for part in range(2):
            for h in (2 * p, 2 * p + 1):
                order.extend(range(h * ATT_HEAD_DIM + part * half, h * ATT_HEAD_DIM + (part + 1) * half))
    return np.asarray(order, np.int32)


def _rope_tables(seq):
    half = ATT_HEAD_DIM // 2
    inv = ROPE_THETA ** (-(jnp.arange(half, dtype=F32) * 2.0 / ATT_HEAD_DIM))
    ang = jnp.arange(seq, dtype=F32)[:, None] * inv[None, :]
    cos, sin = jnp.cos(ang), jnp.sin(ang)
    return (jnp.concatenate([cos, cos, cos, cos], axis=1),
            jnp.concatenate([-sin, -sin, sin, sin], axis=1))


def kernel(x, norm1_w, w_in, moba_up, hgrn_lb_logits, hgrn_norm_w, hgrn_up, w_out, norm2_w,
           router_w, router_b, w_gate_up, b_gate_up, w_down, b_down, final_norm_w):
    B, S, D = x.shape
    T = B * S
    assert S % MOBA_BLOCK == 0 and w_in.shape[0] == 1
    x2 = x.reshape(T, D)

    perm = _qk_column_order()
    w0 = w_in[0]
    w_in_p = jnp.concatenate([w0[:, :ATT_WIDTH][:, perm], w0[:, ATT_WIDTH:2 * ATT_WIDTH][:, perm],
                              w0[:, 2 * ATT_WIDTH:]], axis=1).astype(BF16)
    cos_t, sin_t = _rope_tables(S)
    lb = jnp.cumsum(jax.nn.softmax(hgrn_lb_logits.astype(F32), axis=0), axis=0)[0:1]

    qa, ka, va, qb, fb, ib, gb, ga, gtb = _in_proj(x2, norm1_w, w_in_p, cos_t, sin_t, S)
    r3 = lambda a: a.reshape(B, S, a.shape[1])
    ya = _moba(r3(qa), r3(ka), r3(va)).reshape(T, ATT_WIDTH)
    yb = _hgrn(r3(qb), r3(fb), r3(ib), r3(gb), lb, hgrn_norm_w).reshape(T, HGRN_WIDTH)

    x1, h2, top_e, top_w, rank, cnt = _mix_route(
        ya, yb, ga, gtb, x2, moba_up[0].astype(BF16), hgrn_up[0].astype(BF16),
        w_out[0].astype(BF16), norm2_w, router_w[0].T, router_b[0][:, None])

    counts = cnt[:, 0].astype(jnp.int32)
    start = jnp.cumsum(counts) - counts
    dest = rank + jnp.sum(jnp.where(top_e[:, :, None] == jnp.arange(N_EXPERTS)[None, None, :],
                                    start[None, None, :], 0), axis=-1)

    xg = _scatter_rows(dest, h2)
    wgu = w_gate_up[0]
    yg = _experts(xg, counts,
                  wgu[:, :, 0::2].astype(BF16), wgu[:, :, 1::2].astype(BF16),
                  b_gate_up[0][:, None, 0::2], b_gate_up[0][:, None, 1::2],
                  w_down[0].astype(BF16), b_down[0][:, None, :])
    yk = _gather_rows(dest, yg).reshape(TOP_K, T, D)
    out = _combine(x1, yk, top_w.T, final_norm_w[None, :])
    return out.reshape(B, S, D)
```

```python
import functools
import math

import numpy as np
import jax
import jax.numpy as jnp
from jax import lax
from jax.experimental import pallas as pl
from jax.experimental.pallas import tpu as pltpu
from jax.experimental.pallas import tpu_sc as plsc

ATT_HEADS = 8
ATT_HEAD_DIM = 64
ATT_WIDTH = ATT_HEADS * ATT_HEAD_DIM
MOBA_BLOCK = 256
MOBA_TOPK = 3
ROPE_THETA = 10000.0
HGRN_HEADS = 4
HGRN_DIM = 128
HGRN_WIDTH = HGRN_HEADS * HGRN_DIM
HGRN_CHUNK = 64
N_EXPERTS = 32
TOP_K = 4
SWIGLU_LIMIT = 7.0
SWIGLU_ALPHA = 1.702
RMS_EPS = 1e-6
NEG = -1e30

V7X_LANES = 128
V7X_SUBLANES = 8
V7X_MXU_DIM = 256
V7X_VMEM_LIMIT_BYTES = 56 * 1024 * 1024
V7X_SC_CORES = 2
V7X_SC_SUBCORES = 16

PROJ_ROWS = 512
PROJ_COLS = 512
HGRN_ROWS = 256
MIX_ROWS = 512
EXPERT_ROWS = 256
SC_ROWS = 32
COMBINE_ROWS = 256
PREP_COLS = 512

F32 = jnp.float32
BF16 = jnp.bfloat16


def _nt_dot(a, b, precision=None):
    return lax.dot_general(a, b, (((1,), (1,)), ((), ())), precision=precision,
                           preferred_element_type=F32)


def _tn_dot(a, b, precision=None):
    return lax.dot_general(a, b, (((0,), (0,)), ((), ())), precision=precision,
                           preferred_element_type=F32)


def _cparams(*sem):
    return pltpu.CompilerParams(dimension_semantics=sem, vmem_limit_bytes=V7X_VMEM_LIMIT_BYTES)


def _row_tiles_spec(tm, d, index_map):
    return pl.BlockSpec((tm, d // V7X_LANES, V7X_LANES), index_map)


def _load_row_tiles(ref, lead=()):
    n = ref.shape[-2]
    return jnp.concatenate([ref[lead + (slice(None), j, slice(None))] for j in range(n)], axis=1)


def _store_row_tiles(ref, val):
    for j in range(ref.shape[-2]):
        ref[:, j, :] = val[:, j * V7X_LANES:(j + 1) * V7X_LANES]


def _in_proj_kernel(x_ref, nw_ref, w_ref, cos_ref, sin_ref,
                    qa_ref, ka_ref, va_ref, qb_ref, fb_ref, ib_ref, gb_ref, ga_ref, gtb_ref):
    x = x_ref[...]
    h = x * lax.rsqrt(jnp.mean(x * x, axis=-1, keepdims=True) + RMS_EPS) * nw_ref[...]
    h = h.astype(BF16)
    cos = cos_ref[...]
    sin = sin_ref[...]

    def proj(c):
        return jnp.dot(h, w_ref[:, c * PROJ_COLS:(c + 1) * PROJ_COLS], preferred_element_type=F32)

    def rope(t):
        out = []
        for j in range(PROJ_COLS // V7X_LANES):
            tj = t[:, j * V7X_LANES:(j + 1) * V7X_LANES]
            out.append(tj * cos + pltpu.roll(tj, V7X_LANES // 2, 1) * sin)
        return jnp.concatenate(out, axis=1)

    qa_ref[...] = (rope(proj(0)) * (ATT_HEAD_DIM ** -0.5)).astype(BF16)
    ka_ref[...] = rope(proj(1)).astype(BF16)
    va_ref[...] = proj(2).astype(BF16)
    qb_ref[...] = proj(3).astype(BF16)
    fb_ref[...] = proj(4)
    ib_ref[...] = proj(5).astype(BF16)
    gb_ref[...] = proj(6).astype(BF16)
    ga_ref[:, :PROJ_COLS] = proj(7).astype(BF16)
    ga_ref[:, PROJ_COLS:] = proj(8).astype(BF16)
    gtb_ref[:, :PROJ_COLS] = proj(9).astype(BF16)
    gtb_ref[:, PROJ_COLS:] = proj(10).astype(BF16)


def _in_proj(x2, norm_w, w_in_bf16, cos_t, sin_t, seq):
    T, D = x2.shape
    tm = min(PROJ_ROWS, seq)
    n_seq_tiles = seq // tm
    row = lambda w: pl.BlockSpec((tm, w), lambda i: (i, 0))
    tab = pl.BlockSpec((tm, V7X_LANES), lambda i: (i % n_seq_tiles, 0))
    widths = [ATT_WIDTH] * 3 + [HGRN_WIDTH] * 4 + [D, D]
    dtypes = [BF16, BF16, BF16, BF16, F32, BF16, BF16, BF16, BF16]
    return pl.pallas_call(
        _in_proj_kernel,
        grid=(T // tm,),
        in_specs=[row(D), pl.BlockSpec((1, D), lambda i: (0, 0)),
                  pl.BlockSpec(memory_space=pltpu.VMEM), tab, tab],
        out_specs=[row(w) for w in widths],
        out_shape=[jax.ShapeDtypeStruct((T, w), dt) for w, dt in zip(widths, dtypes)],
        compiler_params=_cparams("parallel"),
        name="in_proj",
    )(x2, norm_w, w_in_bf16, cos_t, sin_t)


def _moba_kernel(q_ref, k_ref, v_ref, o_ref, kaug_ref, kmean_ref, *, nb):
    qi = pl.program_id(2)
    blk = MOBA_BLOCK
    lanes = V7X_LANES

    @pl.when(qi == 0)
    def _():
        kaug_ref[:, :lanes] = k_ref[0]
        rowb = lax.broadcasted_iota(jnp.int32, (nb * blk, lanes), 0) // blk
        col = lax.broadcasted_iota(jnp.int32, (nb * blk, lanes), 1)
        kaug_ref[:, lanes:] = jnp.where(rowb == col, 1.0, 0.0).astype(BF16)
        kmean_ref[...] = jnp.zeros_like(kmean_ref)
        for n in range(nb):
            kb = k_ref[0, n * blk:(n + 1) * blk, :].astype(F32)
            kmean_ref[n:n + 1, :] = jnp.sum(kb, axis=0, keepdims=True) * (1.0 / blk)

    q = q_ref[0]
    lane = lax.broadcasted_iota(jnp.int32, (blk, lanes), 1)
    quarter = lane // (ATT_HEAD_DIM // 2)
    row_i = lax.broadcasted_iota(jnp.int32, (blk, blk), 0)
    col_i = lax.broadcasted_iota(jnp.int32, (blk, blk), 1)
    k_own = k_ref[0, pl.ds(pl.multiple_of(qi * blk, blk), blk), :]
    v_own = v_ref[0, pl.ds(pl.multiple_of(qi * blk, blk), blk), :]
    nbp = kmean_ref.shape[0]
    blk_id = lax.broadcasted_iota(jnp.int32, (nbp, blk), 0)

    outs = []
    for hh in range(2):
        qh = jnp.where((quarter % 2) == hh, q, jnp.zeros_like(q))
        gate = _nt_dot(kmean_ref[...], qh.astype(F32), precision=lax.Precision.HIGHEST)
        gate = jnp.where(blk_id < qi, gate, NEG)
        beaten = jnp.zeros((nbp, blk), F32)
        for n in range(nb):
            gn = gate[n:n + 1, :]
            wins = (gn > gate) | ((gn == gate) & (blk_id > n))
            beaten = beaten + jnp.where(wins, 1.0, 0.0)
        sel = (beaten < MOBA_TOPK) & (blk_id < qi)
        bias_t = jnp.where(sel, 0.0, NEG)
        bias_t = jnp.concatenate([bias_t, jnp.zeros((lanes - nbp, blk), F32)], axis=0)
        q_aug = jnp.concatenate([qh, bias_t.T.astype(BF16)], axis=1)

        s = _nt_dot(qh, k_own)
        s = jnp.where(col_i <= row_i, s, NEG)
        m0 = jnp.max(s, axis=1, keepdims=True)
        p = jnp.exp(s - m0)
        l0 = jnp.sum(p, axis=1, keepdims=True)
        acc0 = jnp.dot(p.astype(BF16), v_own, preferred_element_type=F32)

        def body(j, carry):
            m, l, acc = carry
            off = pl.multiple_of(j * blk, blk)
            s = _nt_dot(q_aug, kaug_ref[pl.ds(off, blk), :])
            m_new = jnp.maximum(m, jnp.max(s, axis=1, keepdims=True))
            alpha = jnp.exp(m - m_new)
            p = jnp.exp(s - m_new)
            l = alpha * l + jnp.sum(p, axis=1, keepdims=True)
            acc = alpha * acc + jnp.dot(p.astype(BF16), v_ref[0, pl.ds(off, blk), :],
                                        preferred_element_type=F32)
            return m_new, l, acc

        _, l, acc = lax.fori_loop(0, qi, body, (m0, l0, acc0))
        outs.append(acc / l)

    o_ref[0] = jnp.where(lane < ATT_HEAD_DIM, outs[0], outs[1]).astype(BF16)


def _moba(q, k, v):
    B, S, _ = q.shape
    nb = S // MOBA_BLOCK
    nbp = -(-nb // 8) * 8
    n_pairs = ATT_WIDTH // V7X_LANES
    qspec = pl.BlockSpec((1, MOBA_BLOCK, V7X_LANES), lambda b, p, i: (b, i, p))
    kvspec = pl.BlockSpec((1, S, V7X_LANES), lambda b, p, i: (b, 0, p))
    return pl.pallas_call(
        functools.partial(_moba_kernel, nb=nb),
        grid=(B, n_pairs, nb),
        in_specs=[qspec, kvspec, kvspec],
        out_specs=qspec,
        out_shape=jax.ShapeDtypeStruct((B, S, ATT_WIDTH), BF16),
        scratch_shapes=[pltpu.VMEM((S, 2 * V7X_LANES), BF16), pltpu.VMEM((nbp, V7X_LANES), F32)],
        compiler_params=_cparams("parallel", "parallel", "arbitrary"),
        name="moba",
    )(q, k, v)


def _hgrn_level_sizes(chunk):
    return [chunk >> (i + 1) for i in range(int(math.log2(chunk)))]


def _hgrn_constants(chunk):
    t = np.arange(chunk)
    mats = [(t[None, :] <= t[:, None]),
            (t[None, :] > t[:, None])]
    qrows, pmasks = [], []
    for bs in _hgrn_level_sizes(chunk):
        blk = t // bs
        odd = (blk % 2) == 1
        lo, hi = blk * bs, (blk + 1) * bs
        u = t[None, :]
        m_odd = (u >= lo[:, None]) & (u <= t[:, None])
        m_even = (u > t[:, None]) & (u < hi[:, None])
        mats.append(np.where(odd[:, None], m_odd, m_even))
        qrows.append(odd)
        pmasks.append(odd[:, None] & (blk[None, :] == blk[:, None] - 1))
    pmasks.append(t[None, :] == t[:, None])
    summat = np.concatenate(mats, axis=0).astype(np.float32)
    qrow = np.stack(qrows, axis=0).astype(np.float32)
    pmask = np.stack(pmasks, axis=0).astype(np.float32)
    return summat, qrow, pmask


def _hgrn_kernel(q_ref, f_ref, i_ref, g_ref, lb_ref, nw_ref, sm_ref, qrow_ref, pm_ref,
                 o_ref, state_ref, *, rows):
    C = HGRN_CHUNK
    n_levels = qrow_ref.shape[0]

    @pl.when(pl.program_id(1) == 0)
    def _():
        state_ref[...] = jnp.zeros_like(state_ref)

    lb = lb_ref[...]
    summat = sm_ref[...]
    for c in range(rows // C):
        r0 = c * C
        fg = lb + (1.0 - lb) * jax.nn.sigmoid(f_ref[0, r0:r0 + C, :])
        logf = jnp.log(fg)
        sums = jnp.dot(summat, logf, precision=lax.Precision.HIGHEST,
                       preferred_element_type=F32)
        for h in range(HGRN_HEADS):
            ls = slice(h * HGRN_DIM, (h + 1) * HGRN_DIM)
            qf = jax.nn.silu(q_ref[0, r0:r0 + C, ls].astype(F32))
            kf = 1.0 - fg[:, ls]
            iv = i_ref[0, r0:r0 + C, ls]
            bcum = sums[0:C, ls]
            bsuf = sums[C:2 * C, ls]
            att = _nt_dot(qf.astype(BF16), kf.astype(BF16)) * pm_ref[n_levels]
            for lv in range(n_levels):
                w = jnp.exp(sums[(2 + lv) * C:(3 + lv) * C, ls])
                qrow = qrow_ref[lv]
                z = (jnp.where(qrow > 0.5, qf, kf) * w).astype(BF16)
                att = att + _nt_dot(z, z) * pm_ref[lv]
            o = jnp.dot(att.astype(BF16), iv, preferred_element_type=F32)
            st = state_ref[h]
            o = o + _nt_dot((qf * jnp.exp(bcum)).astype(BF16), st.astype(BF16))
            kdec = (kf * jnp.exp(bsuf)).astype(BF16)
            state_ref[h] = st * jnp.exp(bcum[C - 1:C, :]) + _tn_dot(iv, kdec)
            o = o * lax.rsqrt(jnp.mean(o * o, axis=-1, keepdims=True) + RMS_EPS)
            o = o * nw_ref[:, ls] * jax.nn.silu(g_ref[0, r0:r0 + C, ls].astype(F32))
            o_ref[0, r0:r0 + C, ls] = o.astype(BF16)


def _hgrn(qb, fb, ib, gb, lb, norm_w):
    B, S, W = qb.shape
    rows = min(HGRN_ROWS, S)
    summat, qrow, pmask = _hgrn_constants(HGRN_CHUNK)
    n_levels = qrow.shape[0]
    blk = pl.BlockSpec((1, rows, W), lambda b, s: (b, s, 0))
    vec = pl.BlockSpec((1, W), lambda b, s: (0, 0))
    const = lambda a: pl.BlockSpec(a.shape, lambda b, s: (0,) * a.ndim)
    qrow3 = qrow.reshape(n_levels, HGRN_CHUNK, 1)
    return pl.pallas_call(
        functools.partial(_hgrn_kernel, rows=rows),
        grid=(B, S // rows),
        in_specs=[blk, blk, blk, blk, vec, vec, const(summat), const(qrow3), const(pmask)],
        out_specs=blk,
        out_shape=jax.ShapeDtypeStruct((B, S, W), BF16),
        scratch_shapes=[pltpu.VMEM((HGRN_HEADS, HGRN_DIM, HGRN_DIM), F32)],
        compiler_params=_cparams("parallel", "arbitrary"),
        name="hgrn",
    )(qb, fb, ib, gb, lb, norm_w, jnp.asarray(summat), jnp.asarray(qrow3), jnp.asarray(pmask))


def _mix_route_kernel(ya_ref, yb_ref, ga_ref, gb_ref, x_ref, wa_ref, wb_ref, wo_ref, nw_ref,
                      rw_ref, rb_ref, tri_ref,
                      x1_ref, h2_ref, e_ref, w_ref, rank_ref, cnt_ref, carry_ref):
    @pl.when(pl.program_id(0) == 0)
    def _():
        carry_ref[...] = jnp.zeros_like(carry_ref)

    ua = jnp.dot(ya_ref[...], wa_ref[...], preferred_element_type=F32)
    ub = jnp.dot(yb_ref[...], wb_ref[...], preferred_element_type=F32)
    mixed = (jax.nn.sigmoid(ga_ref[...].astype(F32)) * ua
             + jax.nn.sigmoid(gb_ref[...].astype(F32)) * ub)
    x1 = x_ref[...] + jnp.dot(mixed.astype(BF16), wo_ref[...], preferred_element_type=F32)
    x1_ref[...] = x1
    h2 = x1 * lax.rsqrt(jnp.mean(x1 * x1, axis=-1, keepdims=True) + RMS_EPS) * nw_ref[...]
    _store_row_tiles(h2_ref, h2)

    tm = x1.shape[0]
    logits = _nt_dot(rw_ref[...], h2, precision=lax.Precision.HIGHEST) + rb_ref[...]
    eid = lax.broadcasted_iota(jnp.int32, (N_EXPERTS, tm), 0)
    work = logits
    es, vs = [], []
    for _ in range(TOP_K):
        mx = jnp.max(work, axis=0, keepdims=True)
        idx = jnp.min(jnp.where(work == mx, eid, N_EXPERTS), axis=0, keepdims=True)
        es.append(idx)
        vs.append(mx)
        work = jnp.where(eid == idx, -jnp.inf, work)
    ex = [jnp.exp(v - vs[0]) for v in vs]
    den = ex[0] + ex[1] + ex[2] + ex[3]
    multi = jnp.zeros((N_EXPERTS, tm), F32)
    for k in range(TOP_K):
        multi = multi + jnp.where(eid == es[k], 1.0, 0.0)
    before = jnp.dot(multi.astype(BF16), tri_ref[...], preferred_element_type=F32) + carry_ref[...]
    for k in range(TOP_K):
        e_ref[k:k + 1, :] = es[k]
        w_ref[k:k + 1, :] = ex[k] / den
        rank_ref[k:k + 1, :] = jnp.sum(jnp.where(eid == es[k], before, 0.0), axis=0,
                                       keepdims=True).astype(jnp.int32)
    carry_ref[...] = carry_ref[...] + jnp.sum(multi, axis=1, keepdims=True)
    cnt_ref[...] = jnp.broadcast_to(carry_ref[...], cnt_ref.shape)


def _mix_route(ya, yb, ga, gb, x2, wa, wb, wo, norm_w, rw_t, rb):
    T, D = x2.shape
    tm = min(MIX_ROWS, T)
    row = lambda w: pl.BlockSpec((tm, w), lambda i: (i, 0))
    whole = pl.BlockSpec(memory_space=pltpu.VMEM)
    kt = pl.BlockSpec((TOP_K, tm), lambda i: (0, i))
    tri = jnp.asarray(np.triu(np.ones((tm, tm), np.float32), 1), BF16)
    return pl.pallas_call(
        _mix_route_kernel,
        grid=(T // tm,),
        in_specs=[row(ATT_WIDTH), row(HGRN_WIDTH), row(D), row(D), row(D),
                  whole, whole, whole, pl.BlockSpec((1, D), lambda i: (0, 0)),
                  whole, whole, whole],
        out_specs=[row(D), _row_tiles_spec(tm, D, lambda i: (i, 0, 0)), kt, kt, kt,
                   pl.BlockSpec((N_EXPERTS, V7X_LANES), lambda i: (0, 0))],
        out_shape=[jax.ShapeDtypeStruct((T, D), F32),
                   jax.ShapeDtypeStruct((T, D // V7X_LANES, V7X_LANES), F32),
                   jax.ShapeDtypeStruct((TOP_K, T), jnp.int32),
                   jax.ShapeDtypeStruct((TOP_K, T), F32),
                   jax.ShapeDtypeStruct((TOP_K, T), jnp.int32),
                   jax.ShapeDtypeStruct((N_EXPERTS, V7X_LANES), F32)],
        scratch_shapes=[pltpu.VMEM((N_EXPERTS, 1), F32)],
        compiler_params=_cparams("arbitrary"),
        name="mix_route",
    )(ya, yb, ga, gb, x2, wa, wb, wo, norm_w, rw_t, rb, tri)


def _sc_worker_id():
    return lax.axis_index("s") * V7X_SC_CORES + lax.axis_index("c")


def _sc_kernel(body, out_rows, like, window, name):
    mesh = plsc.VectorSubcoreMesh(core_axis_name="c", subcore_axis_name="s")
    return pl.kernel(
        body, mesh=mesh,
        out_type=jax.ShapeDtypeStruct((out_rows,) + like.shape[1:], like.dtype),
        scratch_types=[pltpu.VMEM((window,), jnp.int32),
                       pltpu.VMEM((window,) + like.shape[1:], like.dtype)],
        name=name)


def _scatter_rows(dest, h2):
    T = dest.shape[1]
    n_workers = V7X_SC_CORES * V7X_SC_SUBCORES
    per_worker = T // n_workers
    window = min(SC_ROWS, per_worker)

    def body(dest_hbm, src_hbm, dst_hbm, idx_v, rows_v):
        base = _sc_worker_id() * per_worker

        @pl.loop(0, per_worker // window)
        def _(c):
            t0 = base + c * window
            pltpu.sync_copy(src_hbm.at[pl.ds(t0, window)], rows_v)
            for k in range(TOP_K):
                pltpu.sync_copy(dest_hbm.at[pl.ds(k * T + t0, window)], idx_v)
                pltpu.sync_copy(rows_v, dst_hbm.at[idx_v])

    return _sc_kernel(body, TOP_K * T, h2, window, "scatter_rows")(dest.reshape(-1), h2)


def _gather_rows(dest, yg):
    T = dest.shape[1]
    n_workers = V7X_SC_CORES * V7X_SC_SUBCORES
    per_worker = TOP_K * T // n_workers
    window = min(SC_ROWS, per_worker)

    def body(dest_hbm, src_hbm, dst_hbm, idx_v, rows_v):
        base = _sc_worker_id() * per_worker

        @pl.loop(0, per_worker // window)
        def _(c):
            r0 = base + c * window
            pltpu.sync_copy(dest_hbm.at[pl.ds(r0, window)], idx_v)
            pltpu.sync_copy(src_hbm.at[idx_v], rows_v)
            pltpu.sync_copy(rows_v, dst_hbm.at[pl.ds(r0, window)])

    return _sc_kernel(body, TOP_K * T, yg, window, "gather_rows")(dest.reshape(-1), yg)


def _prep_gate_up_kernel(w_ref, perm_ref, wg_ref, wu_ref):
    half = V7X_MXU_DIM // 2
    for g in range(w_ref.shape[2] // V7X_MXU_DIM):
        wb = w_ref[0, :, g * V7X_MXU_DIM:(g + 1) * V7X_MXU_DIM].astype(BF16)
        d = jnp.dot(wb, perm_ref[...], preferred_element_type=F32).astype(BF16)
        wg_ref[0, :, g * half:(g + 1) * half] = d[:, :half]
        wu_ref[0, :, g * half:(g + 1) * half] = d[:, half:]


def _prep_gate_up(w_gate_up):
    E, D, F2 = w_gate_up.shape
    cols = min(PREP_COLS, F2)
    i = np.arange(V7X_MXU_DIM)
    src = np.where(i < V7X_MXU_DIM // 2, 2 * i, 2 * (i - V7X_MXU_DIM // 2) + 1)
    perm = np.zeros((V7X_MXU_DIM, V7X_MXU_DIM), np.float32)
    perm[src, i] = 1.0
    out = pl.BlockSpec((1, D, cols // 2), lambda e, c: (e, 0, c))
    return pl.pallas_call(
        _prep_gate_up_kernel,
        grid=(E, F2 // cols),
        in_specs=[pl.BlockSpec((1, D, cols), lambda e, c: (e, 0, c)),
                  pl.BlockSpec((V7X_MXU_DIM, V7X_MXU_DIM), lambda e, c: (0, 0))],
        out_specs=[out, out],
        out_shape=[jax.ShapeDtypeStruct((E, D, F2 // 2), BF16)] * 2,
        compiler_params=_cparams("parallel", "parallel"),
        name="prep_gate_up",
    )(w_gate_up, jnp.asarray(perm, BF16))


def _experts_kernel(blk_ref, exp_ref, lo_ref, hi_ref, first_ref,
                    x_ref, wg_ref, wu_ref, bg_ref, bu_ref, wd_ref, bd_ref, y_ref):
    w = pl.program_id(0)
    lo = lo_ref[w]
    hi = hi_ref[w]

    @pl.when(lo < hi)
    def _():
        x = _load_row_tiles(x_ref).astype(BF16)
        g = jnp.dot(x, wg_ref[0], preferred_element_type=F32) + bg_ref[0]
        u = jnp.dot(x, wu_ref[0], preferred_element_type=F32) + bu_ref[0]
        g = jnp.minimum(g, SWIGLU_LIMIT)
        u = jnp.clip(u, -SWIGLU_LIMIT, SWIGLU_LIMIT)
        act = (u + 1.0) * (g * jax.nn.sigmoid(g * SWIGLU_ALPHA))
        y = jnp.dot(act.astype(BF16), wd_ref[0], preferred_element_type=F32) + bd_ref[0]
        r = lax.broadcasted_iota(jnp.int32, y.shape, 0)
        mine = (r >= lo) & (r < hi)

        @pl.when(first_ref[w] == 1)
        def _():
            _store_row_tiles(y_ref, jnp.where(mine, y, 0.0))

        @pl.when(first_ref[w] == 0)
        def _():
            _store_row_tiles(y_ref, jnp.where(mine, y, _load_row_tiles(y_ref)))


def _work_items(counts, n_rows, tm):
    nblk = n_rows // tm
    n_items = nblk + N_EXPERTS - 1
    end = jnp.cumsum(counts)
    start = end - counts
    fb = start // tm
    nitems = jnp.where(counts > 0, (end - 1) // tm - fb + 1, 0)
    item_end = jnp.cumsum(nitems)
    item_start = item_end - nitems
    w = jnp.arange(n_items, dtype=jnp.int32)
    valid = w < item_end[-1]
    wc = jnp.minimum(w, item_end[-1] - 1)
    e = jnp.sum(wc[:, None] >= item_end[None, :], axis=1).astype(jnp.int32)
    e = jnp.minimum(e, N_EXPERTS - 1)
    blk = (fb[e] + (wc - item_start[e])).astype(jnp.int32)
    lo = jnp.maximum(start[e], blk * tm) - blk * tm
    hi = jnp.minimum(end[e], (blk + 1) * tm) - blk * tm
    lo = jnp.where(valid, lo, 0).astype(jnp.int32)
    hi = jnp.where(valid, hi, 0).astype(jnp.int32)
    first = jnp.concatenate([jnp.ones((1,), jnp.int32),
                             (blk[1:] != blk[:-1]).astype(jnp.int32)])
    return blk, e, lo, hi, first


def _experts(xg, counts, wg, wu, bg, bu, wd, bd):
    A = xg.shape[0]
    D = xg.shape[1] * xg.shape[2]
    tm = min(EXPERT_ROWS, A)
    F = wg.shape[2]
    items = _work_items(counts, A, tm)
    n_items = A // tm + N_EXPERTS - 1
    xs = _row_tiles_spec(tm, D, lambda w, blk, e, lo, hi, fi: (blk[w], 0, 0))
    wsp = lambda r, c: pl.BlockSpec((1, r, c), lambda w, blk, e, lo, hi, fi: (e[w], 0, 0))
    return pl.pallas_call(
        _experts_kernel,
        grid_spec=pltpu.PrefetchScalarGridSpec(
            num_scalar_prefetch=5,
            grid=(n_items,),
            in_specs=[xs, wsp(D, F), wsp(D, F), wsp(1, F), wsp(1, F), wsp(F, D), wsp(1, D)],
            out_specs=xs,
        ),
        out_shape=jax.ShapeDtypeStruct(xg.shape, F32),
        compiler_params=_cparams("arbitrary"),
        name="experts",
    )(*items, xg, wg, wu, bg, bu, wd, bd)


def _combine_kernel(x1_ref, yk_ref, w_ref, nw_ref, o_ref):
    w = w_ref[...]
    x = x1_ref[...]
    for k in range(TOP_K):
        x = x + _load_row_tiles(yk_ref, (k,)) * w[:, k:k + 1]
    o_ref[...] = x * lax.rsqrt(jnp.mean(x * x, axis=-1, keepdims=True) + RMS_EPS) * nw_ref[...]


def _combine(x1, yk, w_tk, norm_w):
    T, D = x1.shape
    tm = min(COMBINE_ROWS, T)
    row = pl.BlockSpec((tm, D), lambda i: (i, 0))
    return pl.pallas_call(
        _combine_kernel,
        grid=(T // tm,),
        in_specs=[row, pl.BlockSpec((TOP_K, tm, D // V7X_LANES, V7X_LANES), lambda i: (0, i, 0, 0)),
                  pl.BlockSpec((tm, TOP_K), lambda i: (i, 0)),
                  pl.BlockSpec((1, D), lambda i: (0, 0))],
        out_specs=row,
        out_shape=jax.ShapeDtypeStruct((T, D), F32),
        compiler_params=_cparams("parallel"),
        name="combine",
    )(x1, yk, w_tk, norm_w)


def _qk_column_order():
    half = ATT_HEAD_DIM // 2
    order = []
    for p in range(ATT_HEADS // 2):
        for part in range(2):
            for h in (2 * p, 2 * p + 1):
                order.extend(range(h * ATT_HEAD_DIM + part * half, h * ATT_HEAD_DIM + (part + 1) * half))
    return np.asarray(order, np.int32)


def _rope_tables(seq):
    half = ATT_HEAD_DIM // 2
    inv = ROPE_THETA ** (-(jnp.arange(half, dtype=F32) * 2.0 / ATT_HEAD_DIM))
    ang = jnp.arange(seq, dtype=F32)[:, None] * inv[None, :]
    cos, sin = jnp.cos(ang), jnp.sin(ang)
    return (jnp.concatenate([cos, cos, cos, cos], axis=1),
            jnp.concatenate([-sin, -sin, sin, sin], axis=1))


def kernel(x, norm1_w, w_in, moba_up, hgrn_lb_logits, hgrn_norm_w, hgrn_up, w_out, norm2_w,
           router_w, router_b, w_gate_up, b_gate_up, w_down, b_down, final_norm_w):
    B, S, D = x.shape
    T = B * S
    assert S % MOBA_BLOCK == 0 and w_in.shape[0] == 1
    x2 = x.reshape(T, D)

    perm = _qk_column_order()
    w0 = w_in[0]
    w_in_p = jnp.concatenate([w0[:, :ATT_WIDTH][:, perm], w0[:, ATT_WIDTH:2 * ATT_WIDTH][:, perm],
                              w0[:, 2 * ATT_WIDTH:]], axis=1).astype(BF16)
    cos_t, sin_t = _rope_tables(S)
    lb = jnp.cumsum(jax.nn.softmax(hgrn_lb_logits.astype(F32), axis=0), axis=0)[0:1]

    qa, ka, va, qb, fb, ib, gb, ga, gtb = _in_proj(x2, norm1_w, w_in_p, cos_t, sin_t, S)
    r3 = lambda a: a.reshape(B, S, a.shape[1])
    ya = _moba(r3(qa), r3(ka), r3(va)).reshape(T, ATT_WIDTH)
    yb = _hgrn(r3(qb), r3(fb), r3(ib), r3(gb), lb, hgrn_norm_w).reshape(T, HGRN_WIDTH)

    x1, h2, top_e, top_w, rank, cnt = _mix_route(
        ya, yb, ga, gtb, x2, moba_up[0].astype(BF16), hgrn_up[0].astype(BF16),
        w_out[0].astype(BF16), norm2_w, router_w[0].T, router_b[0][:, None])

    counts = cnt[:, 0].astype(jnp.int32)
    start = jnp.cumsum(counts) - counts
    dest = rank + jnp.sum(jnp.where(top_e[:, :, None] == jnp.arange(N_EXPERTS)[None, None, :],
                                    start[None, None, :], 0), axis=-1)

    xg = _scatter_rows(dest, h2)
    wg, wu = _prep_gate_up(w_gate_up[0])
    yg = _experts(xg, counts, wg, wu,
                  b_gate_up[0][:, None, 0::2], b_gate_up[0][:, None, 1::2],
                  w_down[0].astype(BF16), b_down[0][:, None, :])
    yk = _gather_rows(dest, yg).reshape(TOP_K, T, D // V7X_LANES, V7X_LANES)
    out = _combine(x1, yk, top_w.T, final_norm_w[None, :])
    return out.reshape(B, S, D)
```

```python
import functools
import math

import numpy as np
import jax
import jax.numpy as jnp
from jax import lax
from jax.experimental import pallas as pl
from jax.experimental.pallas import tpu as pltpu
from jax.experimental.pallas import tpu_sc as plsc

ATT_HEADS = 8
ATT_HEAD_DIM = 64
ATT_WIDTH = ATT_HEADS * ATT_HEAD_DIM
MOBA_BLOCK = 256
MOBA_TOPK = 3
ROPE_THETA = 10000.0
HGRN_HEADS = 4
HGRN_DIM = 128
HGRN_WIDTH = HGRN_HEADS * HGRN_DIM
HGRN_CHUNK = 64
N_EXPERTS = 32
TOP_K = 4
SWIGLU_LIMIT = 7.0
SWIGLU_ALPHA = 1.702
RMS_EPS = 1e-6
NEG = -1e30

V7X_LANES = 128
V7X_SUBLANES = 8
V7X_MXU_DIM = 256
V7X_VMEM_LIMIT_BYTES = 56 * 1024 * 1024
V7X_SC_CORES = 2
V7X_SC_SUBCORES = 16

MOBA_GROUP = 4
PROJ_ROWS = 512
PROJ_COLS = 512
HGRN_ROWS = 256
MIX_ROWS = 512
EXPERT_ROWS = 256
SC_ROWS = 32
COMBINE_ROWS = 256
PREP_COLS = 512

F32 = jnp.float32
BF16 = jnp.bfloat16


def _nt_dot(a, b, precision=None):
    return lax.dot_general(a, b, (((1,), (1,)), ((), ())), precision=precision,
                           preferred_element_type=F32)


def _tn_dot(a, b, precision=None):
    return lax.dot_general(a, b, (((0,), (0,)), ((), ())), precision=precision,
                           preferred_element_type=F32)


def _cparams(*sem):
    return pltpu.CompilerParams(dimension_semantics=sem, vmem_limit_bytes=V7X_VMEM_LIMIT_BYTES)


def _row_tiles_spec(tm, d, index_map):
    return pl.BlockSpec((tm, d // V7X_LANES, V7X_LANES), index_map)


def _load_row_tiles(ref, lead=()):
    n = ref.shape[-2]
    return jnp.concatenate([ref[lead + (slice(None), j, slice(None))] for j in range(n)], axis=1)


def _store_row_tiles(ref, val):
    for j in range(ref.shape[-2]):
        ref[:, j, :] = val[:, j * V7X_LANES:(j + 1) * V7X_LANES]


def _in_proj_kernel(x_ref, nw_ref, w_ref, cos_ref, sin_ref,
                    qa_ref, ka_ref, va_ref, qb_ref, fb_ref, ib_ref, gb_ref, ga_ref, gtb_ref):
    x = x_ref[...]
    h = x * lax.rsqrt(jnp.mean(x * x, axis=-1, keepdims=True) + RMS_EPS) * nw_ref[...]
    h = h.astype(BF16)
    cos = cos_ref[...]
    sin = sin_ref[...]

    def proj(c):
        return jnp.dot(h, w_ref[:, c * PROJ_COLS:(c + 1) * PROJ_COLS], preferred_element_type=F32)

    def rope(t):
        out = []
        for j in range(PROJ_COLS // V7X_LANES):
            tj = t[:, j * V7X_LANES:(j + 1) * V7X_LANES]
            out.append(tj * cos + pltpu.roll(tj, V7X_LANES // 2, 1) * sin)
        return jnp.concatenate(out, axis=1)

    qa_ref[...] = (rope(proj(0)) * (ATT_HEAD_DIM ** -0.5)).astype(BF16)
    ka_ref[...] = rope(proj(1)).astype(BF16)
    va_ref[...] = proj(2).astype(BF16)
    qb_ref[...] = proj(3).astype(BF16)
    fb_ref[...] = proj(4)
    ib_ref[...] = proj(5).astype(BF16)
    gb_ref[...] = proj(6).astype(BF16)
    ga_ref[:, :PROJ_COLS] = proj(7).astype(BF16)
    ga_ref[:, PROJ_COLS:] = proj(8).astype(BF16)
    gtb_ref[:, :PROJ_COLS] = proj(9).astype(BF16)
    gtb_ref[:, PROJ_COLS:] = proj(10).astype(BF16)


def _in_proj(x2, norm_w, w_in_bf16, cos_t, sin_t, seq):
    T, D = x2.shape
    tm = min(PROJ_ROWS, seq)
    n_seq_tiles = seq // tm
    row = lambda w: pl.BlockSpec((tm, w), lambda i: (i, 0))
    tab = pl.BlockSpec((tm, V7X_LANES), lambda i: (i % n_seq_tiles, 0))
    widths = [ATT_WIDTH] * 3 + [HGRN_WIDTH] * 4 + [D, D]
    dtypes = [BF16, BF16, BF16, BF16, F32, BF16, BF16, BF16, BF16]
    return pl.pallas_call(
        _in_proj_kernel,
        grid=(T // tm,),
        in_specs=[row(D), pl.BlockSpec((1, D), lambda i: (0, 0)),
                  pl.BlockSpec(memory_space=pltpu.VMEM), tab, tab],
        out_specs=[row(w) for w in widths],
        out_shape=[jax.ShapeDtypeStruct((T, w), dt) for w, dt in zip(widths, dtypes)],
        compiler_params=_cparams("parallel"),
        name="in_proj",
    )(x2, norm_w, w_in_bf16, cos_t, sin_t)


def _moba_kernel(q_ref, k_ref, v_ref, o_ref, kaug_ref, kmean_ref, vt_ref, acc_ref, s_ref, *, nb):
    qi = pl.program_id(2)
    blk = MOBA_BLOCK
    lanes = V7X_LANES

    @pl.when(qi == 0)
    def _():
        kaug_ref[:, :lanes] = k_ref[0]
        rowb = lax.broadcasted_iota(jnp.int32, (nb * blk, lanes), 0) // blk
        col = lax.broadcasted_iota(jnp.int32, (nb * blk, lanes), 1)
        kaug_ref[:, lanes:] = jnp.where(rowb == col, 1.0, 0.0).astype(BF16)
        kmean_ref[...] = jnp.zeros_like(kmean_ref)
        for n in range(nb):
            kb = k_ref[0, n * blk:(n + 1) * blk, :].astype(F32)
            kmean_ref[n:n + 1, :] = jnp.sum(kb, axis=0, keepdims=True) * (1.0 / blk)

        for c in range(nb):
            vt = v_ref[0, c * blk:(c + 1) * blk, :].astype(F32).T
            vt_ref[:lanes, c * blk:(c + 1) * blk] = vt.astype(BF16)
        vt_ref[lanes:, :] = jnp.ones((vt_ref.shape[0] - lanes, nb * blk), BF16)

    qt = q_ref[0].astype(F32).T
    feat = lax.broadcasted_iota(jnp.int32, (lanes, blk), 0)
    key_i = lax.broadcasted_iota(jnp.int32, (blk, blk), 0)
    qry_i = lax.broadcasted_iota(jnp.int32, (blk, blk), 1)
    own = pl.multiple_of(qi * blk, blk)
    k_own = kaug_ref[pl.ds(own, blk), :lanes]
    nbp = kmean_ref.shape[0]
    blk_id = lax.broadcasted_iota(jnp.int32, (nbp, blk), 0)
    slab = 2 * V7X_SUBLANES

    def slab_max(s):
        return jnp.max(s.reshape(s.shape[0] // slab, slab, blk), axis=0)

    q_aug, m_init = [], []
    for hh in range(2):
        qh = jnp.where((feat // (ATT_HEAD_DIM // 2)) % 2 == hh, qt, 0.0)
        gate = jnp.dot(kmean_ref[...], qh, precision=lax.Precision.HIGHEST,
                       preferred_element_type=F32)
        gate = jnp.where(blk_id < qi, gate, NEG)
        beaten = jnp.zeros((nbp, blk), F32)
        for n in range(nb):
            gn = gate[n:n + 1, :]
            wins = (gn > gate) | ((gn == gate) & (blk_id > n))
            beaten = beaten + jnp.where(wins, 1.0, 0.0)
        sel = (beaten < MOBA_TOPK) & (blk_id < qi)
        bias = jnp.where(sel, 0.0, NEG)
        bias = jnp.concatenate([bias, jnp.zeros((lanes - nbp, blk), F32)], axis=0)
        qh = qh.astype(BF16)
        q_aug.append(jnp.concatenate([qh, bias.astype(BF16)], axis=0))

        m_init.append(jnp.full((slab, blk), NEG, F32))
        acc_ref[hh] = jnp.zeros(acc_ref.shape[1:], F32)

    group = MOBA_GROUP * blk
    n_groups = (qi + MOBA_GROUP) // MOBA_GROUP

    def score_group(g, ms):
        off = pl.multiple_of(g * group, group)
        kb = kaug_ref[pl.ds(off, group), :]
        out = []
        for hh in range(2):
            s = jnp.dot(kb, q_aug[hh], preferred_element_type=F32)
            s_ref[hh, pl.ds(off, group), :] = s
            out.append(jnp.maximum(ms[hh], slab_max(s)))
        return tuple(out)

    ms = lax.fori_loop(0, n_groups, score_group, tuple(m_init))

    m_fin = []
    for hh in range(2):
        s = jnp.dot(k_own, q_aug[hh][:lanes], preferred_element_type=F32)
        s = jnp.where(key_i <= qry_i, s, NEG)
        s_ref[hh, pl.ds(own, blk), :] = s
        m = jnp.maximum(ms[hh], slab_max(s))
        m_fin.append(jnp.max(m, axis=0, keepdims=True))

    def value_group(g, c):
        off = pl.multiple_of(g * group, group)
        vb = vt_ref[:, pl.ds(off, group)]
        for hh in range(2):
            p = jnp.exp(s_ref[hh, pl.ds(off, group), :] - m_fin[hh]).astype(BF16)
            acc_ref[hh] += jnp.dot(vb, p, preferred_element_type=F32)
        return c

    lax.fori_loop(0, n_groups, value_group, 0)

    half = lanes // 2
    o0 = acc_ref[0, :half, :] / acc_ref[0, lanes:lanes + 1, :]
    o1 = acc_ref[1, half:lanes, :] / acc_ref[1, lanes:lanes + 1, :]
    o_ref[0] = jnp.concatenate([o0, o1], axis=0).T.astype(BF16)


def _moba(q, k, v):
    B, S, _ = q.shape
    nb = S // MOBA_BLOCK
    assert nb % MOBA_GROUP == 0
    nbp = -(-nb // V7X_SUBLANES) * V7X_SUBLANES
    n_pairs = ATT_WIDTH // V7X_LANES
    vt_rows = V7X_LANES + 2 * V7X_SUBLANES
    qspec = pl.BlockSpec((1, MOBA_BLOCK, V7X_LANES), lambda b, p, i: (b, i, p))
    kvspec = pl.BlockSpec((1, S, V7X_LANES), lambda b, p, i: (b, 0, p))
    return pl.pallas_call(
        functools.partial(_moba_kernel, nb=nb),
        grid=(B, n_pairs, nb),
        in_specs=[qspec, kvspec, kvspec],
        out_specs=qspec,
        out_shape=jax.ShapeDtypeStruct((B, S, ATT_WIDTH), BF16),
        scratch_shapes=[pltpu.VMEM((S, 2 * V7X_LANES), BF16), pltpu.VMEM((nbp, V7X_LANES), F32),
                        pltpu.VMEM((vt_rows, S), BF16), pltpu.VMEM((2, vt_rows, MOBA_BLOCK), F32),
                        pltpu.VMEM((2, S, MOBA_BLOCK), F32)],
        compiler_params=_cparams("parallel", "parallel", "arbitrary"),
        name="moba",
    )(q, k, v)


def _hgrn_level_sizes(chunk):
    return [chunk >> (i + 1) for i in range(int(math.log2(chunk)))]


def _hgrn_constants(chunk):
    t = np.arange(chunk)
    mats = [(t[None, :] <= t[:, None]),
            (t[None, :] > t[:, None])]
    qrows, pmasks = [], []
    for bs in _hgrn_level_sizes(chunk):
        blk = t // bs
        odd = (blk % 2) == 1
        lo, hi = blk * bs, (blk + 1) * bs
        u = t[None, :]
        m_odd = (u >= lo[:, None]) & (u <= t[:, None])
        m_even = (u > t[:, None]) & (u < hi[:, None])
        mats.append(np.where(odd[:, None], m_odd, m_even))
        qrows.append(odd)
        pmasks.append(odd[:, None] & (blk[None, :] == blk[:, None] - 1))
    pmasks.append(t[None, :] == t[:, None])
    summat = np.concatenate(mats, axis=0).astype(np.float32)
    qrow = np.stack(qrows, axis=0).astype(np.float32)
    pmask = np.stack(pmasks, axis=0).astype(np.float32)
    return summat, qrow, pmask


def _hgrn_kernel(q_ref, f_ref, i_ref, g_ref, lb_ref, nw_ref, sm_ref, qrow_ref, pm_ref,
                 o_ref, state_ref, *, rows):
    C = HGRN_CHUNK
    n_levels = qrow_ref.shape[0]

    @pl.when(pl.program_id(1) == 0)
    def _():
        state_ref[...] = jnp.zeros_like(state_ref)

    lb = lb_ref[...]
    summat = sm_ref[...]
    for c in range(rows // C):
        r0 = c * C
        fg = lb + (1.0 - lb) * jax.nn.sigmoid(f_ref[0, r0:r0 + C, :])
        logf = jnp.log(fg)
        sums = jnp.dot(summat, logf, precision=lax.Precision.HIGHEST,
                       preferred_element_type=F32)
        for h in range(HGRN_HEADS):
            ls = slice(h * HGRN_DIM, (h + 1) * HGRN_DIM)
            qf = jax.nn.silu(q_ref[0, r0:r0 + C, ls].astype(F32))
            kf = 1.0 - fg[:, ls]
            iv = i_ref[0, r0:r0 + C, ls]
            bcum = sums[0:C, ls]
            bsuf = sums[C:2 * C, ls]
            att = _nt_dot(qf.astype(BF16), kf.astype(BF16)) * pm_ref[n_levels]
            for lv in range(n_levels):
                w = jnp.exp(sums[(2 + lv) * C:(3 + lv) * C, ls])
                qrow = qrow_ref[lv]
                z = (jnp.where(qrow > 0.5, qf, kf) * w).astype(BF16)
                att = att + _nt_dot(z, z) * pm_ref[lv]
            o = jnp.dot(att.astype(BF16), iv, preferred_element_type=F32)
            st = state_ref[h]
            o = o + _nt_dot((qf * jnp.exp(bcum)).astype(BF16), st.astype(BF16))
            kdec = (kf * jnp.exp(bsuf)).astype(BF16)
            state_ref[h] = st * jnp.exp(bcum[C - 1:C, :]) + _tn_dot(iv, kdec)
            o = o * lax.rsqrt(jnp.mean(o * o, axis=-1, keepdims=True) + RMS_EPS)
            o = o * nw_ref[:, ls] * jax.nn.silu(g_ref[0, r0:r0 + C, ls].astype(F32))
            o_ref[0, r0:r0 + C, ls] = o.astype(BF16)


def _hgrn(qb, fb, ib, gb, lb, norm_w):
    B, S, W = qb.shape
    rows = min(HGRN_ROWS, S)
    summat, qrow, pmask = _hgrn_constants(HGRN_CHUNK)
    n_levels = qrow.shape[0]
    blk = pl.BlockSpec((1, rows, W), lambda b, s: (b, s, 0))
    vec = pl.BlockSpec((1, W), lambda b, s: (0, 0))
    const = lambda a: pl.BlockSpec(a.shape, lambda b, s: (0,) * a.ndim)
    qrow3 = qrow.reshape(n_levels, HGRN_CHUNK, 1)
    return pl.pallas_call(
        functools.partial(_hgrn_kernel, rows=rows),
        grid=(B, S // rows),
        in_specs=[blk, blk, blk, blk, vec, vec, const(summat), const(qrow3), const(pmask)],
        out_specs=blk,
        out_shape=jax.ShapeDtypeStruct((B, S, W), BF16),
        scratch_shapes=[pltpu.VMEM((HGRN_HEADS, HGRN_DIM, HGRN_DIM), F32)],
        compiler_params=_cparams("parallel", "arbitrary"),
        name="hgrn",
    )(qb, fb, ib, gb, lb, norm_w, jnp.asarray(summat), jnp.asarray(qrow3), jnp.asarray(pmask))


def _mix_route_kernel(ya_ref, yb_ref, ga_ref, gb_ref, x_ref, wa_ref, wb_ref, wo_ref, nw_ref,
                      rw_ref, rb_ref, tri_ref,
                      x1_ref, h2_ref, e_ref, w_ref, rank_ref, cnt_ref, carry_ref):
    @pl.when(pl.program_id(0) == 0)
    def _():
        carry_ref[...] = jnp.zeros_like(carry_ref)

    ua = jnp.dot(ya_ref[...], wa_ref[...], preferred_element_type=F32)
    ub = jnp.dot(yb_ref[...], wb_ref[...], preferred_element_type=F32)
    mixed = (jax.nn.sigmoid(ga_ref[...].astype(F32)) * ua
             + jax.nn.sigmoid(gb_ref[...].astype(F32)) * ub)
    x1 = x_ref[...] + jnp.dot(mixed.astype(BF16), wo_ref[...], preferred_element_type=F32)
    x1_ref[...] = x1
    h2 = x1 * lax.rsqrt(jnp.mean(x1 * x1, axis=-1, keepdims=True) + RMS_EPS) * nw_ref[...]
    _store_row_tiles(h2_ref, h2)

    tm = x1.shape[0]
    logits = _nt_dot(rw_ref[...], h2, precision=lax.Precision.HIGHEST) + rb_ref[...]
    eid = lax.broadcasted_iota(jnp.int32, (N_EXPERTS, tm), 0)
    work = logits
    es, vs = [], []
    for _ in range(TOP_K):
        mx = jnp.max(work, axis=0, keepdims=True)
        idx = jnp.min(jnp.where(work == mx, eid, N_EXPERTS), axis=0, keepdims=True)
        es.append(idx)
        vs.append(mx)
        work = jnp.where(eid == idx, -jnp.inf, work)
    ex = [jnp.exp(v - vs[0]) for v in vs]
    den = ex[0] + ex[1] + ex[2] + ex[3]
    multi = jnp.zeros((N_EXPERTS, tm), F32)
    for k in range(TOP_K):
        multi = multi + jnp.where(eid == es[k], 1.0, 0.0)
    before = jnp.dot(multi.astype(BF16), tri_ref[...], preferred_element_type=F32) + carry_ref[...]
    for k in range(TOP_K):
        e_ref[k:k + 1, :] = es[k]
        w_ref[k:k + 1, :] = ex[k] / den
        rank_ref[k:k + 1, :] = jnp.sum(jnp.where(eid == es[k], before, 0.0), axis=0,
                                       keepdims=True).astype(jnp.int32)
    carry_ref[...] = carry_ref[...] + jnp.sum(multi, axis=1, keepdims=True)
    cnt_ref[...] = jnp.broadcast_to(carry_ref[...], cnt_ref.shape)


def _mix_route(ya, yb, ga, gb, x2, wa, wb, wo, norm_w, rw_t, rb):
    T, D = x2.shape
    tm = min(MIX_ROWS, T)
    row = lambda w: pl.BlockSpec((tm, w), lambda i: (i, 0))
    whole = pl.BlockSpec(memory_space=pltpu.VMEM)
    kt = pl.BlockSpec((TOP_K, tm), lambda i: (0, i))
    tri = jnp.asarray(np.triu(np.ones((tm, tm), np.float32), 1), BF16)
    return pl.pallas_call(
        _mix_route_kernel,
        grid=(T // tm,),
        in_specs=[row(ATT_WIDTH), row(HGRN_WIDTH), row(D), row(D), row(D),
                  whole, whole, whole, pl.BlockSpec((1, D), lambda i: (0, 0)),
                  whole, whole, whole],
        out_specs=[row(D), _row_tiles_spec(tm, D, lambda i: (i, 0, 0)), kt, kt, kt,
                   pl.BlockSpec((N_EXPERTS, V7X_LANES), lambda i: (0, 0))],
        out_shape=[jax.ShapeDtypeStruct((T, D), F32),
                   jax.ShapeDtypeStruct((T, D // V7X_LANES, V7X_LANES), F32),
                   jax.ShapeDtypeStruct((TOP_K, T), jnp.int32),
                   jax.ShapeDtypeStruct((TOP_K, T), F32),
                   jax.ShapeDtypeStruct((TOP_K, T), jnp.int32),
                   jax.ShapeDtypeStruct((N_EXPERTS, V7X_LANES), F32)],
        scratch_shapes=[pltpu.VMEM((N_EXPERTS, 1), F32)],
        compiler_params=_cparams("arbitrary"),
        name="mix_route",
    )(ya, yb, ga, gb, x2, wa, wb, wo, norm_w, rw_t, rb, tri)


def _sc_worker_id():
    return lax.axis_index("s") * V7X_SC_CORES + lax.axis_index("c")


def _sc_kernel(body, out_rows, like, window, name):
    mesh = plsc.VectorSubcoreMesh(core_axis_name="c", subcore_axis_name="s")
    return pl.kernel(
        body, mesh=mesh,
        out_type=jax.ShapeDtypeStruct((out_rows,) + like.shape[1:], like.dtype),
        scratch_types=[pltpu.VMEM((window,), jnp.int32),
                       pltpu.VMEM((window,) + like.shape[1:], like.dtype)],
        name=name)


def _scatter_rows(dest, h2):
    T = dest.shape[1]
    n_workers = V7X_SC_CORES * V7X_SC_SUBCORES
    per_worker = T // n_workers
    window = min(SC_ROWS, per_worker)

    def body(dest_hbm, src_hbm, dst_hbm, idx_v, rows_v):
        base = _sc_worker_id() * per_worker

        @pl.loop(0, per_worker // window)
        def _(c):
            t0 = base + c * window
            pltpu.sync_copy(src_hbm.at[pl.ds(t0, window)], rows_v)
            for k in range(TOP_K):
                pltpu.sync_copy(dest_hbm.at[pl.ds(k * T + t0, window)], idx_v)
                pltpu.sync_copy(rows_v, dst_hbm.at[idx_v])

    return _sc_kernel(body, TOP_K * T, h2, window, "scatter_rows")(dest.reshape(-1), h2)


def _gather_rows(dest, yg):
    T = dest.shape[1]
    n_workers = V7X_SC_CORES * V7X_SC_SUBCORES
    per_worker = TOP_K * T // n_workers
    window = min(SC_ROWS, per_worker)

    def body(dest_hbm, src_hbm, dst_hbm, idx_v, rows_v):
        base = _sc_worker_id() * per_worker

        @pl.loop(0, per_worker // window)
        def _(c):
            r0 = base + c * window
            pltpu.sync_copy(dest_hbm.at[pl.ds(r0, window)], idx_v)
            pltpu.sync_copy(src_hbm.at[idx_v], rows_v)
            pltpu.sync_copy(rows_v, dst_hbm.at[pl.ds(r0, window)])

    return _sc_kernel(body, TOP_K * T, yg, window, "gather_rows")(dest.reshape(-1), yg)


def _prep_gate_up_kernel(w_ref, perm_ref, wg_ref, wu_ref):
    half = V7X_MXU_DIM // 2
    for g in range(w_ref.shape[2] // V7X_MXU_DIM):
        wb = w_ref[0, :, g * V7X_MXU_DIM:(g + 1) * V7X_MXU_DIM].astype(BF16)
        d = jnp.dot(wb, perm_ref[...], preferred_element_type=F32).astype(BF16)
        wg_ref[0, :, g * half:(g + 1) * half] = d[:, :half]
        wu_ref[0, :, g * half:(g + 1) * half] = d[:, half:]


def _prep_gate_up(w_gate_up):
    E, D, F2 = w_gate_up.shape
    cols = min(PREP_COLS, F2)
    i = np.arange(V7X_MXU_DIM)
    src = np.where(i < V7X_MXU_DIM // 2, 2 * i, 2 * (i - V7X_MXU_DIM // 2) + 1)
    perm = np.zeros((V7X_MXU_DIM, V7X_MXU_DIM), np.float32)
    perm[src, i] = 1.0
    out = pl.BlockSpec((1, D, cols // 2), lambda e, c: (e, 0, c))
    return pl.pallas_call(
        _prep_gate_up_kernel,
        grid=(E, F2 // cols),
        in_specs=[pl.BlockSpec((1, D, cols), lambda e, c: (e, 0, c)),
                  pl.BlockSpec((V7X_MXU_DIM, V7X_MXU_DIM), lambda e, c: (0, 0))],
        out_specs=[out, out],
        out_shape=[jax.ShapeDtypeStruct((E, D, F2 // 2), BF16)] * 2,
        compiler_params=_cparams("parallel", "parallel"),
        name="prep_gate_up",
    )(w_gate_up, jnp.asarray(perm, BF16))


def _experts_kernel(blk_ref, exp_ref, lo_ref, hi_ref, first_ref,
                    x_ref, wg_ref, wu_ref, bg_ref, bu_ref, wd_ref, bd_ref, y_ref):
    w = pl.program_id(0)
    lo = lo_ref[w]
    hi = hi_ref[w]

    @pl.when(lo < hi)
    def _():
        x = _load_row_tiles(x_ref).astype(BF16)
        g = jnp.dot(x, wg_ref[0], preferred_element_type=F32) + bg_ref[0]
        u = jnp.dot(x, wu_ref[0], preferred_element_type=F32) + bu_ref[0]
        g = jnp.minimum(g, SWIGLU_LIMIT)
        u = jnp.clip(u, -SWIGLU_LIMIT, SWIGLU_LIMIT)
        act = (u + 1.0) * (g * jax.nn.sigmoid(g * SWIGLU_ALPHA))
        y = jnp.dot(act.astype(BF16), wd_ref[0], preferred_element_type=F32) + bd_ref[0]
        r = lax.broadcasted_iota(jnp.int32, y.shape, 0)
        mine = (r >= lo) & (r < hi)

        @pl.when(first_ref[w] == 1)
        def _():
            _store_row_tiles(y_ref, jnp.where(mine, y, 0.0))

        @pl.when(first_ref[w] == 0)
        def _():
            _store_row_tiles(y_ref, jnp.where(mine, y, _load_row_tiles(y_ref)))


def _work_items(counts, n_rows, tm):
    nblk = n_rows // tm
    n_items = nblk + N_EXPERTS - 1
    end = jnp.cumsum(counts)
    start = end - counts
    fb = start // tm
    nitems = jnp.where(counts > 0, (end - 1) // tm - fb + 1, 0)
    item_end = jnp.cumsum(nitems)
    item_start = item_end - nitems
    w = jnp.arange(n_items, dtype=jnp.int32)
    valid = w < item_end[-1]
    wc = jnp.minimum(w, item_end[-1] - 1)
    e = jnp.sum(wc[:, None] >= item_end[None, :], axis=1).astype(jnp.int32)
    e = jnp.minimum(e, N_EXPERTS - 1)
    blk = (fb[e] + (wc - item_start[e])).astype(jnp.int32)
    lo = jnp.maximum(start[e], blk * tm) - blk * tm
    hi = jnp.minimum(end[e], (blk + 1) * tm) - blk * tm
    lo = jnp.where(valid, lo, 0).astype(jnp.int32)
    hi = jnp.where(valid, hi, 0).astype(jnp.int32)
    first = jnp.concatenate([jnp.ones((1,), jnp.int32),
                             (blk[1:] != blk[:-1]).astype(jnp.int32)])
    return blk, e, lo, hi, first


def _experts(xg, counts, wg, wu, bg, bu, wd, bd):
    A = xg.shape[0]
    D = xg.shape[1] * xg.shape[2]
    tm = min(EXPERT_ROWS, A)
    F = wg.shape[2]
    items = _work_items(counts, A, tm)
    n_items = A // tm + N_EXPERTS - 1
    xs = _row_tiles_spec(tm, D, lambda w, blk, e, lo, hi, fi: (blk[w], 0, 0))
    wsp = lambda r, c: pl.BlockSpec((1, r, c), lambda w, blk, e, lo, hi, fi: (e[w], 0, 0))
    return pl.pallas_call(
        _experts_kernel,
        grid_spec=pltpu.PrefetchScalarGridSpec(
            num_scalar_prefetch=5,
            grid=(n_items,),
            in_specs=[xs, wsp(D, F), wsp(D, F), wsp(1, F), wsp(1, F), wsp(F, D), wsp(1, D)],
            out_specs=xs,
        ),
        out_shape=jax.ShapeDtypeStruct(xg.shape, F32),
        compiler_params=_cparams("arbitrary"),
        name="experts",
    )(*items, xg, wg, wu, bg, bu, wd, bd)


def _combine_kernel(x1_ref, yk_ref, w_ref, nw_ref, o_ref):
    w = w_ref[...]
    x = x1_ref[...]
    for k in range(TOP_K):
        x = x + _load_row_tiles(yk_ref, (k,)) * w[:, k:k + 1]
    o_ref[...] = x * lax.rsqrt(jnp.mean(x * x, axis=-1, keepdims=True) + RMS_EPS) * nw_ref[...]


def _combine(x1, yk, w_tk, norm_w):
    T, D = x1.shape
    tm = min(COMBINE_ROWS, T)
    row = pl.BlockSpec((tm, D), lambda i: (i, 0))
    return pl.pallas_call(
        _combine_kernel,
        grid=(T // tm,),
        in_specs=[row, pl.BlockSpec((TOP_K, tm, D // V7X_LANES, V7X_LANES), lambda i: (0, i, 0, 0)),
                  pl.BlockSpec((tm, TOP_K), lambda i: (i, 0)),
                  pl.BlockSpec((1, D), lambda i: (0, 0))],
        out_specs=row,
        out_shape=jax.ShapeDtypeStruct((T, D), F32),
        compiler_params=_cparams("parallel"),
        name="combine",
    )(x1, yk, w_tk, norm_w)


def _qk_column_order():
    half = ATT_HEAD_DIM // 2
    order = []
    for p in range(ATT_HEADS // 2):
        for part in range(2):
            for h in (2 * p, 2 * p + 1):
                order.extend(range(h * ATT_HEAD_DIM + part * half, h * ATT_HEAD_DIM + (part + 1) * half))
    return np.asarray(order, np.int32)


def _rope_tables(seq):
    half = ATT_HEAD_DIM // 2
    inv = ROPE_THETA ** (-(jnp.arange(half, dtype=F32) * 2.0 / ATT_HEAD_DIM))
    ang = jnp.arange(seq, dtype=F32)[:, None] * inv[None, :]
    cos, sin = jnp.cos(ang), jnp.sin(ang)
    return (jnp.concatenate([cos, cos, cos, cos], axis=1),
            jnp.concatenate([-sin, -sin, sin, sin], axis=1))


def kernel(x, norm1_w, w_in, moba_up, hgrn_lb_logits, hgrn_norm_w, hgrn_up, w_out, norm2_w,
           router_w, router_b, w_gate_up, b_gate_up, w_down, b_down, final_norm_w):
    B, S, D = x.shape
    T = B * S
    assert S % MOBA_BLOCK == 0 and w_in.shape[0] == 1
    x2 = x.reshape(T, D)

    perm = _qk_column_order()
    w0 = w_in[0]
    w_in_p = jnp.concatenate([w0[:, :ATT_WIDTH][:, perm], w0[:, ATT_WIDTH:2 * ATT_WIDTH][:, perm],
                              w0[:, 2 * ATT_WIDTH:]], axis=1).astype(BF16)
    cos_t, sin_t = _rope_tables(S)
    lb = jnp.cumsum(jax.nn.softmax(hgrn_lb_logits.astype(F32), axis=0), axis=0)[0:1]

    qa, ka, va, qb, fb, ib, gb, ga, gtb = _in_proj(x2, norm1_w, w_in_p, cos_t, sin_t, S)
    r3 = lambda a: a.reshape(B, S, a.shape[1])
    ya = _moba(r3(qa), r3(ka), r3(va)).reshape(T, ATT_WIDTH)
    yb = _hgrn(r3(qb), r3(fb), r3(ib), r3(gb), lb, hgrn_norm_w).reshape(T, HGRN_WIDTH)

    x1, h2, top_e, top_w, rank, cnt = _mix_route(
        ya, yb, ga, gtb, x2, moba_up[0].astype(BF16), hgrn_up[0].astype(BF16),
        w_out[0].astype(BF16), norm2_w, router_w[0].T, router_b[0][:, None])

    counts = cnt[:, 0].astype(jnp.int32)
    start = jnp.cumsum(counts) - counts
    dest = rank + jnp.sum(jnp.where(top_e[:, :, None] == jnp.arange(N_EXPERTS)[None, None, :],
                                    start[None, None, :], 0), axis=-1)

    xg = _scatter_rows(dest, h2)
    wg, wu = _prep_gate_up(w_gate_up[0])
    yg = _experts(xg, counts, wg, wu,
                  b_gate_up[0][:, None, 0::2], b_gate_up[0][:, None, 1::2],
                  w_down[0].astype(BF16), b_down[0][:, None, :])
    yk = _gather_rows(dest, yg).reshape(TOP_K, T, D // V7X_LANES, V7X_LANES)
    out = _combine(x1, yk, top_w.T, final_norm_w[None, :])
    return out.reshape(B, S, D)
```

```python
import functools
import math

import numpy as np
import jax
import jax.numpy as jnp
from jax import lax
from jax.experimental import pallas as pl
from jax.experimental.pallas import tpu as pltpu
from jax.experimental.pallas import tpu_sc as plsc

ATT_HEADS = 8
ATT_HEAD_DIM = 64
ATT_WIDTH = ATT_HEADS * ATT_HEAD_DIM
MOBA_BLOCK = 256
MOBA_TOPK = 3
ROPE_THETA = 10000.0
HGRN_HEADS = 4
HGRN_DIM = 128
HGRN_WIDTH = HGRN_HEADS * HGRN_DIM
HGRN_CHUNK = 64
N_EXPERTS = 32
TOP_K = 4
SWIGLU_LIMIT = 7.0
SWIGLU_ALPHA = 1.702
RMS_EPS = 1e-6
NEG = -1e30

V7X_LANES = 128
V7X_SUBLANES = 8
V7X_MXU_DIM = 256
V7X_VMEM_LIMIT_BYTES = 56 * 1024 * 1024
V7X_SC_CORES = 2
V7X_SC_SUBCORES = 16

MOBA_GROUP = 4
PROJ_ROWS = 512
PROJ_COLS = 512
HGRN_ROWS = 256
MIX_ROWS = 512
EXPERT_ROWS = 256
SC_ROWS = 32
COMBINE_ROWS = 256
PREP_COLS = 512

F32 = jnp.float32
BF16 = jnp.bfloat16


def _nt_dot(a, b, precision=None):
    return lax.dot_general(a, b, (((1,), (1,)), ((), ())), precision=precision,
                           preferred_element_type=F32)


def _tn_dot(a, b, precision=None):
    return lax.dot_general(a, b, (((0,), (0,)), ((), ())), precision=precision,
                           preferred_element_type=F32)


def _cparams(*sem):
    return pltpu.CompilerParams(dimension_semantics=sem, vmem_limit_bytes=V7X_VMEM_LIMIT_BYTES)


def _in_proj_kernel(x_ref, nw_ref, w_ref, cos_ref, sin_ref,
                    qa_ref, ka_ref, va_ref, qb_ref, fb_ref, ib_ref, gb_ref, ga_ref, gtb_ref):
    x = x_ref[...]
    h = x * lax.rsqrt(jnp.mean(x * x, axis=-1, keepdims=True) + RMS_EPS) * nw_ref[...]
    h = h.astype(BF16)
    cos = cos_ref[...]
    sin = sin_ref[...]

    def proj(c):
        return jnp.dot(h, w_ref[:, c * PROJ_COLS:(c + 1) * PROJ_COLS], preferred_element_type=F32)

    def rope(t):
        out = []
        for j in range(PROJ_COLS // V7X_LANES):
            tj = t[:, j * V7X_LANES:(j + 1) * V7X_LANES]
            out.append(tj * cos + pltpu.roll(tj, V7X_LANES // 2, 1) * sin)
        return jnp.concatenate(out, axis=1)

    qa_ref[...] = (rope(proj(0)) * (ATT_HEAD_DIM ** -0.5)).astype(BF16)
    ka_ref[...] = rope(proj(1)).astype(BF16)
    va_ref[...] = proj(2).astype(BF16)
    qb_ref[...] = proj(3).astype(BF16)
    fb_ref[...] = proj(4)
    ib_ref[...] = proj(5).astype(BF16)
    gb_ref[...] = proj(6).astype(BF16)
    ga_ref[:, :PROJ_COLS] = proj(7).astype(BF16)
    ga_ref[:, PROJ_COLS:] = proj(8).astype(BF16)
    gtb_ref[:, :PROJ_COLS] = proj(9).astype(BF16)
    gtb_ref[:, PROJ_COLS:] = proj(10).astype(BF16)


def _in_proj(x2, norm_w, w_in_bf16, cos_t, sin_t, seq):
    T, D = x2.shape
    tm = min(PROJ_ROWS, seq)
    n_seq_tiles = seq // tm
    row = lambda w: pl.BlockSpec((tm, w), lambda i: (i, 0))
    tab = pl.BlockSpec((tm, V7X_LANES), lambda i: (i % n_seq_tiles, 0))
    widths = [ATT_WIDTH] * 3 + [HGRN_WIDTH] * 4 + [D, D]
    dtypes = [BF16, BF16, BF16, BF16, F32, BF16, BF16, BF16, BF16]
    return pl.pallas_call(
        _in_proj_kernel,
        grid=(T // tm,),
        in_specs=[row(D), pl.BlockSpec((1, D), lambda i: (0, 0)),
                  pl.BlockSpec(memory_space=pltpu.VMEM), tab, tab],
        out_specs=[row(w) for w in widths],
        out_shape=[jax.ShapeDtypeStruct((T, w), dt) for w, dt in zip(widths, dtypes)],
        compiler_params=_cparams("parallel"),
        name="in_proj",
    )(x2, norm_w, w_in_bf16, cos_t, sin_t)


def _moba_kernel(q_ref, k_ref, v_ref, o_ref, kaug_ref, kmean_ref, vt_ref, acc_ref, s_ref, *, nb):
    qi = pl.program_id(2)
    blk = MOBA_BLOCK
    lanes = V7X_LANES

    @pl.when(qi == 0)
    def _():
        kaug_ref[:, :lanes] = k_ref[0]
        rowb = lax.broadcasted_iota(jnp.int32, (nb * blk, lanes), 0) // blk
        col = lax.broadcasted_iota(jnp.int32, (nb * blk, lanes), 1)
        kaug_ref[:, lanes:] = jnp.where(rowb == col, 1.0, 0.0).astype(BF16)
        kmean_ref[...] = jnp.zeros_like(kmean_ref)
        for n in range(nb):
            kb = k_ref[0, n * blk:(n + 1) * blk, :].astype(F32)
            kmean_ref[n:n + 1, :] = jnp.sum(kb, axis=0, keepdims=True) * (1.0 / blk)

        for c in range(nb):
            vt = v_ref[0, c * blk:(c + 1) * blk, :].astype(F32).T
            vt_ref[:lanes, c * blk:(c + 1) * blk] = vt.astype(BF16)
        vt_ref[lanes:, :] = jnp.ones((vt_ref.shape[0] - lanes, nb * blk), BF16)

    qt = q_ref[0].astype(F32).T
    feat = lax.broadcasted_iota(jnp.int32, (lanes, blk), 0)
    key_i = lax.broadcasted_iota(jnp.int32, (blk, blk), 0)
    qry_i = lax.broadcasted_iota(jnp.int32, (blk, blk), 1)
    own = pl.multiple_of(qi * blk, blk)
    k_own = kaug_ref[pl.ds(own, blk), :lanes]
    nbp = kmean_ref.shape[0]
    blk_id = lax.broadcasted_iota(jnp.int32, (nbp, blk), 0)
    slab = 2 * V7X_SUBLANES

    def slab_max(s):
        return jnp.max(s.reshape(s.shape[0] // slab, slab, blk), axis=0)

    q_aug, m_init = [], []
    for hh in range(2):
        qh = jnp.where((feat // (ATT_HEAD_DIM // 2)) % 2 == hh, qt, 0.0)
        gate = jnp.dot(kmean_ref[...], qh, precision=lax.Precision.HIGHEST,
                       preferred_element_type=F32)
        gate = jnp.where(blk_id < qi, gate, NEG)
        beaten = jnp.zeros((nbp, blk), F32)
        for n in range(nb):
            gn = gate[n:n + 1, :]
            wins = (gn > gate) | ((gn == gate) & (blk_id > n))
            beaten = beaten + jnp.where(wins, 1.0, 0.0)
        sel = (beaten < MOBA_TOPK) & (blk_id < qi)
        bias = jnp.where(sel, 0.0, NEG)
        bias = jnp.concatenate([bias, jnp.zeros((lanes - nbp, blk), F32)], axis=0)
        qh = qh.astype(BF16)
        q_aug.append(jnp.concatenate([qh, bias.astype(BF16)], axis=0))

        m_init.append(jnp.full((slab, blk), NEG, F32))
        acc_ref[hh] = jnp.zeros(acc_ref.shape[1:], F32)

    group = MOBA_GROUP * blk
    n_groups = (qi + MOBA_GROUP) // MOBA_GROUP

    def score_group(g, ms):
        off = pl.multiple_of(g * group, group)
        kb = kaug_ref[pl.ds(off, group), :]
        out = []
        for hh in range(2):
            s = jnp.dot(kb, q_aug[hh], preferred_element_type=F32)
            s_ref[hh, pl.ds(off, group), :] = s
            out.append(jnp.maximum(ms[hh], slab_max(s)))
        return tuple(out)

    ms = lax.fori_loop(0, n_groups, score_group, tuple(m_init))

    m_fin = []
    for hh in range(2):
        s = jnp.dot(k_own, q_aug[hh][:lanes], preferred_element_type=F32)
        s = jnp.where(key_i <= qry_i, s, NEG)
        s_ref[hh, pl.ds(own, blk), :] = s
        m = jnp.maximum(ms[hh], slab_max(s))
        m_fin.append(jnp.max(m, axis=0, keepdims=True))

    def value_group(g, c):
        off = pl.multiple_of(g * group, group)
        vb = vt_ref[:, pl.ds(off, group)]
        for hh in range(2):
            p = jnp.exp(s_ref[hh, pl.ds(off, group), :] - m_fin[hh]).astype(BF16)
            acc_ref[hh] += jnp.dot(vb, p, preferred_element_type=F32)
        return c

    lax.fori_loop(0, n_groups, value_group, 0)

    half = lanes // 2
    o0 = acc_ref[0, :half, :] / acc_ref[0, lanes:lanes + 1, :]
    o1 = acc_ref[1, half:lanes, :] / acc_ref[1, lanes:lanes + 1, :]
    o_ref[0] = jnp.concatenate([o0, o1], axis=0).T.astype(BF16)


def _moba(q, k, v):
    B, S, _ = q.shape
    nb = S // MOBA_BLOCK
    assert nb % MOBA_GROUP == 0
    nbp = -(-nb // V7X_SUBLANES) * V7X_SUBLANES
    n_pairs = ATT_WIDTH // V7X_LANES
    vt_rows = V7X_LANES + 2 * V7X_SUBLANES
    qspec = pl.BlockSpec((1, MOBA_BLOCK, V7X_LANES), lambda b, p, i: (b, i, p))
    kvspec = pl.BlockSpec((1, S, V7X_LANES), lambda b, p, i: (b, 0, p))
    return pl.pallas_call(
        functools.partial(_moba_kernel, nb=nb),
        grid=(B, n_pairs, nb),
        in_specs=[qspec, kvspec, kvspec],
        out_specs=qspec,
        out_shape=jax.ShapeDtypeStruct((B, S, ATT_WIDTH), BF16),
        scratch_shapes=[pltpu.VMEM((S, 2 * V7X_LANES), BF16), pltpu.VMEM((nbp, V7X_LANES), F32),
                        pltpu.VMEM((vt_rows, S), BF16), pltpu.VMEM((2, vt_rows, MOBA_BLOCK), F32),
                        pltpu.VMEM((2, S, MOBA_BLOCK), F32)],
        compiler_params=_cparams("parallel", "parallel", "arbitrary"),
        name="moba",
    )(q, k, v)


def _hgrn_level_sizes(chunk):
    return [chunk >> (i + 1) for i in range(int(math.log2(chunk)))]


def _hgrn_constants(chunk):
    t = np.arange(chunk)
    mats = [(t[None, :] <= t[:, None]),
            (t[None, :] > t[:, None])]
    qrows, pmasks = [], []
    for bs in _hgrn_level_sizes(chunk):
        blk = t // bs
        odd = (blk % 2) == 1
        lo, hi = blk * bs, (blk + 1) * bs
        u = t[None, :]
        m_odd = (u >= lo[:, None]) & (u <= t[:, None])
        m_even = (u > t[:, None]) & (u < hi[:, None])
        mats.append(np.where(odd[:, None], m_odd, m_even))
        qrows.append(odd)
        pmasks.append(odd[:, None] & (blk[None, :] == blk[:, None] - 1))
    pmasks.append(t[None, :] == t[:, None])
    summat = np.concatenate(mats, axis=0).astype(np.float32)
    qrow = np.stack(qrows, axis=0).astype(np.float32)
    pmask = np.stack(pmasks, axis=0).astype(np.float32)
    return summat, qrow, pmask


def _hgrn_kernel(q_ref, f_ref, i_ref, g_ref, lb_ref, nw_ref, sm_ref, qrow_ref, pm_ref,
                 o_ref, state_ref, *, rows):
    C = HGRN_CHUNK
    n_levels = qrow_ref.shape[0]

    @pl.when(pl.program_id(1) == 0)
    def _():
        state_ref[...] = jnp.zeros_like(state_ref)

    lb = lb_ref[...]
    summat = sm_ref[...]
    for c in range(rows // C):
        r0 = c * C
        fg = lb + (1.0 - lb) * jax.nn.sigmoid(f_ref[0, r0:r0 + C, :])
        logf = jnp.log(fg)
        sums = jnp.dot(summat, logf, precision=lax.Precision.HIGHEST,
                       preferred_element_type=F32)
        for h in range(HGRN_HEADS):
            ls = slice(h * HGRN_DIM, (h + 1) * HGRN_DIM)
            qf = jax.nn.silu(q_ref[0, r0:r0 + C, ls].astype(F32))
            kf = 1.0 - fg[:, ls]
            iv = i_ref[0, r0:r0 + C, ls]
            bcum = sums[0:C, ls]
            bsuf = sums[C:2 * C, ls]
            att = _nt_dot(qf.astype(BF16), kf.astype(BF16)) * pm_ref[n_levels]
            for lv in range(n_levels):
                w = jnp.exp(sums[(2 + lv) * C:(3 + lv) * C, ls])
                qrow = qrow_ref[lv]
                z = (jnp.where(qrow > 0.5, qf, kf) * w).astype(BF16)
                att = att + _nt_dot(z, z) * pm_ref[lv]
            o = jnp.dot(att.astype(BF16), iv, preferred_element_type=F32)
            st = state_ref[h]
            o = o + _nt_dot((qf * jnp.exp(bcum)).astype(BF16), st.astype(BF16))
            kdec = (kf * jnp.exp(bsuf)).astype(BF16)
            state_ref[h] = st * jnp.exp(bcum[C - 1:C, :]) + _tn_dot(iv, kdec)
            o = o * lax.rsqrt(jnp.mean(o * o, axis=-1, keepdims=True) + RMS_EPS)
            o = o * nw_ref[:, ls] * jax.nn.silu(g_ref[0, r0:r0 + C, ls].astype(F32))
            o_ref[0, r0:r0 + C, ls] = o.astype(BF16)


def _hgrn(qb, fb, ib, gb, lb, norm_w):
    B, S, W = qb.shape
    rows = min(HGRN_ROWS, S)
    summat, qrow, pmask = _hgrn_constants(HGRN_CHUNK)
    n_levels = qrow.shape[0]
    blk = pl.BlockSpec((1, rows, W), lambda b, s: (b, s, 0))
    vec = pl.BlockSpec((1, W), lambda b, s: (0, 0))
    const = lambda a: pl.BlockSpec(a.shape, lambda b, s: (0,) * a.ndim)
    qrow3 = qrow.reshape(n_levels, HGRN_CHUNK, 1)
    return pl.pallas_call(
        functools.partial(_hgrn_kernel, rows=rows),
        grid=(B, S // rows),
        in_specs=[blk, blk, blk, blk, vec, vec, const(summat), const(qrow3), const(pmask)],
        out_specs=blk,
        out_shape=jax.ShapeDtypeStruct((B, S, W), BF16),
        scratch_shapes=[pltpu.VMEM((HGRN_HEADS, HGRN_DIM, HGRN_DIM), F32)],
        compiler_params=_cparams("parallel", "arbitrary"),
        name="hgrn",
    )(qb, fb, ib, gb, lb, norm_w, jnp.asarray(summat), jnp.asarray(qrow3), jnp.asarray(pmask))


def _mix_route_kernel(ya_ref, yb_ref, ga_ref, gb_ref, x_ref, wa_ref, wb_ref, wo_ref, nw_ref,
                      rw_ref, rb_ref, tri_ref,
                      x1_ref, h2_ref, e_ref, w_ref, rank_ref, cnt_ref, carry_ref):
    @pl.when(pl.program_id(0) == 0)
    def _():
        carry_ref[...] = jnp.zeros_like(carry_ref)

    ua = jnp.dot(ya_ref[...], wa_ref[...], preferred_element_type=F32)
    ub = jnp.dot(yb_ref[...], wb_ref[...], preferred_element_type=F32)
    mixed = (jax.nn.sigmoid(ga_ref[...].astype(F32)) * ua
             + jax.nn.sigmoid(gb_ref[...].astype(F32)) * ub)
    x1 = x_ref[...] + jnp.dot(mixed.astype(BF16), wo_ref[...], preferred_element_type=F32)
    x1_ref[...] = x1
    h2 = x1 * lax.rsqrt(jnp.mean(x1 * x1, axis=-1, keepdims=True) + RMS_EPS) * nw_ref[...]
    h2_ref[...] = h2

    tm = x1.shape[0]
    logits = _nt_dot(rw_ref[...], h2, precision=lax.Precision.HIGHEST) + rb_ref[...]
    eid = lax.broadcasted_iota(jnp.int32, (N_EXPERTS, tm), 0)
    work = logits
    es, vs = [], []
    for _ in range(TOP_K):
        mx = jnp.max(work, axis=0, keepdims=True)
        idx = jnp.min(jnp.where(work == mx, eid, N_EXPERTS), axis=0, keepdims=True)
        es.append(idx)
        vs.append(mx)
        work = jnp.where(eid == idx, -jnp.inf, work)
    ex = [jnp.exp(v - vs[0]) for v in vs]
    den = ex[0] + ex[1] + ex[2] + ex[3]
    multi = jnp.zeros((N_EXPERTS, tm), F32)
    for k in range(TOP_K):
        multi = multi + jnp.where(eid == es[k], 1.0, 0.0)
    before = jnp.dot(multi.astype(BF16), tri_ref[...], preferred_element_type=F32) + carry_ref[...]
    for k in range(TOP_K):
        e_ref[k:k + 1, :] = es[k]
        w_ref[k:k + 1, :] = ex[k] / den
        rank_ref[k:k + 1, :] = jnp.sum(jnp.where(eid == es[k], before, 0.0), axis=0,
                                       keepdims=True).astype(jnp.int32)
    carry_ref[...] = carry_ref[...] + jnp.sum(multi, axis=1, keepdims=True)
    cnt_ref[...] = jnp.broadcast_to(carry_ref[...], cnt_ref.shape)


def _mix_route(ya, yb, ga, gb, x2, wa, wb, wo, norm_w, rw_t, rb):
    T, D = x2.shape
    tm = min(MIX_ROWS, T)
    row = lambda w: pl.BlockSpec((tm, w), lambda i: (i, 0))
    whole = pl.BlockSpec(memory_space=pltpu.VMEM)
    kt = pl.BlockSpec((TOP_K, tm), lambda i: (0, i))
    tri = jnp.asarray(np.triu(np.ones((tm, tm), np.float32), 1), BF16)
    return pl.pallas_call(
        _mix_route_kernel,
        grid=(T // tm,),
        in_specs=[row(ATT_WIDTH), row(HGRN_WIDTH), row(D), row(D), row(D),
                  whole, whole, whole, pl.BlockSpec((1, D), lambda i: (0, 0)),
                  whole, whole, whole],
        out_specs=[row(D), row(D), kt, kt, kt,
                   pl.BlockSpec((N_EXPERTS, V7X_LANES), lambda i: (0, 0))],
        out_shape=[jax.ShapeDtypeStruct((T, D), F32), jax.ShapeDtypeStruct((T, D), F32),
                   jax.ShapeDtypeStruct((TOP_K, T), jnp.int32),
                   jax.ShapeDtypeStruct((TOP_K, T), F32),
                   jax.ShapeDtypeStruct((TOP_K, T), jnp.int32),
                   jax.ShapeDtypeStruct((N_EXPERTS, V7X_LANES), F32)],
        scratch_shapes=[pltpu.VMEM((N_EXPERTS, 1), F32)],
        compiler_params=_cparams("arbitrary"),
        name="mix_route",
    )(ya, yb, ga, gb, x2, wa, wb, wo, norm_w, rw_t, rb, tri)


def _sc_worker_id():
    return lax.axis_index("s") * V7X_SC_CORES + lax.axis_index("c")


def _sc_kernel(body, out_rows, like, window, name):
    mesh = plsc.VectorSubcoreMesh(core_axis_name="c", subcore_axis_name="s")
    return pl.kernel(
        body, mesh=mesh,
        out_type=jax.ShapeDtypeStruct((out_rows,) + like.shape[1:], like.dtype),
        scratch_types=[pltpu.VMEM((window,), jnp.int32),
                       pltpu.VMEM((window,) + like.shape[1:], like.dtype)],
        name=name)


def _scatter_rows(dest, h2):
    T = dest.shape[1]
    n_workers = V7X_SC_CORES * V7X_SC_SUBCORES
    per_worker = T // n_workers
    window = min(SC_ROWS, per_worker)

    def body(dest_hbm, src_hbm, dst_hbm, idx_v, rows_v):
        base = _sc_worker_id() * per_worker

        @pl.loop(0, per_worker // window)
        def _(c):
            t0 = base + c * window
            pltpu.sync_copy(src_hbm.at[pl.ds(t0, window)], rows_v)
            for k in range(TOP_K):
                pltpu.sync_copy(dest_hbm.at[pl.ds(k * T + t0, window)], idx_v)
                pltpu.sync_copy(rows_v, dst_hbm.at[idx_v])

    return _sc_kernel(body, TOP_K * T, h2, window, "scatter_rows")(dest.reshape(-1), h2)


def _gather_rows(dest, yg):
    T = dest.shape[1]
    n_workers = V7X_SC_CORES * V7X_SC_SUBCORES
    per_worker = TOP_K * T // n_workers
    window = min(SC_ROWS, per_worker)

    def body(dest_hbm, src_hbm, dst_hbm, idx_v, rows_v):
        base = _sc_worker_id() * per_worker

        @pl.loop(0, per_worker // window)
        def _(c):
            r0 = base + c * window
            pltpu.sync_copy(dest_hbm.at[pl.ds(r0, window)], idx_v)
            pltpu.sync_copy(src_hbm.at[idx_v], rows_v)
            pltpu.sync_copy(rows_v, dst_hbm.at[pl.ds(r0, window)])

    return _sc_kernel(body, TOP_K * T, yg, window, "gather_rows")(dest.reshape(-1), yg)


def _prep_gate_up_kernel(w_ref, perm_ref, wg_ref, wu_ref):
    half = V7X_MXU_DIM // 2
    for g in range(w_ref.shape[2] // V7X_MXU_DIM):
        wb = w_ref[0, :, g * V7X_MXU_DIM:(g + 1) * V7X_MXU_DIM].astype(BF16)
        d = jnp.dot(wb, perm_ref[...], preferred_element_type=F32).astype(BF16)
        wg_ref[0, :, g * half:(g + 1) * half] = d[:, :half]
        wu_ref[0, :, g * half:(g + 1) * half] = d[:, half:]


def _prep_gate_up(w_gate_up):
    E, D, F2 = w_gate_up.shape
    cols = min(PREP_COLS, F2)
    i = np.arange(V7X_MXU_DIM)
    src = np.where(i < V7X_MXU_DIM // 2, 2 * i, 2 * (i - V7X_MXU_DIM // 2) + 1)
    perm = np.zeros((V7X_MXU_DIM, V7X_MXU_DIM), np.float32)
    perm[src, i] = 1.0
    out = pl.BlockSpec((1, D, cols // 2), lambda e, c: (e, 0, c))
    return pl.pallas_call(
        _prep_gate_up_kernel,
        grid=(E, F2 // cols),
        in_specs=[pl.BlockSpec((1, D, cols), lambda e, c: (e, 0, c)),
                  pl.BlockSpec((V7X_MXU_DIM, V7X_MXU_DIM), lambda e, c: (0, 0))],
        out_specs=[out, out],
        out_shape=[jax.ShapeDtypeStruct((E, D, F2 // 2), BF16)] * 2,
        compiler_params=_cparams("parallel", "parallel"),
        name="prep_gate_up",
    )(w_gate_up, jnp.asarray(perm, BF16))


def _experts_kernel(blk_ref, exp_ref, lo_ref, hi_ref, first_ref,
                    x_ref, wg_ref, wu_ref, bg_ref, bu_ref, wd_ref, bd_ref, y_ref):
    w = pl.program_id(0)
    lo = lo_ref[w]
    hi = hi_ref[w]

    @pl.when(lo < hi)
    def _():
        x = x_ref[...].astype(BF16)
        g = jnp.dot(x, wg_ref[0], preferred_element_type=F32) + bg_ref[0]
        u = jnp.dot(x, wu_ref[0], preferred_element_type=F32) + bu_ref[0]
        g = jnp.minimum(g, SWIGLU_LIMIT)
        u = jnp.clip(u, -SWIGLU_LIMIT, SWIGLU_LIMIT)
        act = (u + 1.0) * (g * jax.nn.sigmoid(g * SWIGLU_ALPHA))
        y = jnp.dot(act.astype(BF16), wd_ref[0], preferred_element_type=F32) + bd_ref[0]
        r = lax.broadcasted_iota(jnp.int32, y.shape, 0)
        mine = (r >= lo) & (r < hi)

        @pl.when(first_ref[w] == 1)
        def _():
            y_ref[...] = jnp.where(mine, y, 0.0)

        @pl.when(first_ref[w] == 0)
        def _():
            y_ref[...] = jnp.where(mine, y, y_ref[...])


def _work_items(counts, n_rows, tm):
    nblk = n_rows // tm
    n_items = nblk + N_EXPERTS - 1
    end = jnp.cumsum(counts)
    start = end - counts
    fb = start // tm
    nitems = jnp.where(counts > 0, (end - 1) // tm - fb + 1, 0)
    item_end = jnp.cumsum(nitems)
    item_start = item_end - nitems
    w = jnp.arange(n_items, dtype=jnp.int32)
    valid = w < item_end[-1]
    wc = jnp.minimum(w, item_end[-1] - 1)
    e = jnp.sum(wc[:, None] >= item_end[None, :], axis=1).astype(jnp.int32)
    e = jnp.minimum(e, N_EXPERTS - 1)
    blk = (fb[e] + (wc - item_start[e])).astype(jnp.int32)
    lo = jnp.maximum(start[e], blk * tm) - blk * tm
    hi = jnp.minimum(end[e], (blk + 1) * tm) - blk * tm
    lo = jnp.where(valid, lo, 0).astype(jnp.int32)
    hi = jnp.where(valid, hi, 0).astype(jnp.int32)
    first = jnp.concatenate([jnp.ones((1,), jnp.int32),
                             (blk[1:] != blk[:-1]).astype(jnp.int32)])
    return blk, e, lo, hi, first


def _experts(xg, counts, wg, wu, bg, bu, wd, bd):
    A, D = xg.shape
    tm = min(EXPERT_ROWS, A)
    F = wg.shape[2]
    items = _work_items(counts, A, tm)
    n_items = A // tm + N_EXPERTS - 1
    xs = pl.BlockSpec((tm, D), lambda w, blk, e, lo, hi, fi: (blk[w], 0))
    wsp = lambda r, c: pl.BlockSpec((1, r, c), lambda w, blk, e, lo, hi, fi: (e[w], 0, 0))
    return pl.pallas_call(
        _experts_kernel,
        grid_spec=pltpu.PrefetchScalarGridSpec(
            num_scalar_prefetch=5,
            grid=(n_items,),
            in_specs=[xs, wsp(D, F), wsp(D, F), wsp(1, F), wsp(1, F), wsp(F, D), wsp(1, D)],
            out_specs=xs,
        ),
        out_shape=jax.ShapeDtypeStruct(xg.shape, F32),
        compiler_params=_cparams("arbitrary"),
        name="experts",
    )(*items, xg, wg, wu, bg, bu, wd, bd)


def _combine_kernel(x1_ref, yk_ref, w_ref, nw_ref, o_ref):
    w = w_ref[...]
    x = x1_ref[...]
    for k in range(TOP_K):
        x = x + yk_ref[k] * w[:, k:k + 1]
    o_ref[...] = x * lax.rsqrt(jnp.mean(x * x, axis=-1, keepdims=True) + RMS_EPS) * nw_ref[...]


def _combine(x1, yk, w_tk, norm_w):
    T, D = x1.shape
    tm = min(COMBINE_ROWS, T)
    row = pl.BlockSpec((tm, D), lambda i: (i, 0))
    return pl.pallas_call(
        _combine_kernel,
        grid=(T // tm,),
        in_specs=[row, pl.BlockSpec((TOP_K, tm, D), lambda i: (0, i, 0)),
                  pl.BlockSpec((tm, TOP_K), lambda i: (i, 0)),
                  pl.BlockSpec((1, D), lambda i: (0, 0))],
        out_specs=row,
        out_shape=jax.ShapeDtypeStruct((T, D), F32),
        compiler_params=_cparams("parallel"),
        name="combine",
    )(x1, yk, w_tk, norm_w)


def _qk_column_order():
    half = ATT_HEAD_DIM // 2
    order = []
    for p in range(ATT_HEADS // 2):
        for part in range(2):
            for h in (2 * p, 2 * p + 1):
                order.extend(range(h * ATT_HEAD_DIM + part * half, h * ATT_HEAD_DIM + (part + 1) * half))
    return np.asarray(order, np.int32)


def _rope_tables(seq):
    half = ATT_HEAD_DIM // 2
    inv = ROPE_THETA ** (-(jnp.arange(half, dtype=F32) * 2.0 / ATT_HEAD_DIM))
    ang = jnp.arange(seq, dtype=F32)[:, None] * inv[None, :]
    cos, sin = jnp.cos(ang), jnp.sin(ang)
    return (jnp.concatenate([cos, cos, cos, cos], axis=1),
            jnp.concatenate([-sin, -sin, sin, sin], axis=1))


def kernel(x, norm1_w, w_in, moba_up, hgrn_lb_logits, hgrn_norm_w, hgrn_up, w_out, norm2_w,
           router_w, router_b, w_gate_up, b_gate_up, w_down, b_down, final_norm_w):
    B, S, D = x.shape
    T = B * S
    assert S % MOBA_BLOCK == 0 and w_in.shape[0] == 1
    x2 = x.reshape(T, D)

    perm = _qk_column_order()
    w0 = w_in[0]
    w_in_p = jnp.concatenate([w0[:, :ATT_WIDTH][:, perm], w0[:, ATT_WIDTH:2 * ATT_WIDTH][:, perm],
                              w0[:, 2 * ATT_WIDTH:]], axis=1).astype(BF16)
    cos_t, sin_t = _rope_tables(S)
    lb = jnp.cumsum(jax.nn.softmax(hgrn_lb_logits.astype(F32), axis=0), axis=0)[0:1]

    qa, ka, va, qb, fb, ib, gb, ga, gtb = _in_proj(x2, norm1_w, w_in_p, cos_t, sin_t, S)
    r3 = lambda a: a.reshape(B, S, a.shape[1])
    ya = _moba(r3(qa), r3(ka), r3(va)).reshape(T, ATT_WIDTH)
    yb = _hgrn(r3(qb), r3(fb), r3(ib), r3(gb), lb, hgrn_norm_w).reshape(T, HGRN_WIDTH)

    x1, h2, top_e, top_w, rank, cnt = _mix_route(
        ya, yb, ga, gtb, x2, moba_up[0].astype(BF16), hgrn_up[0].astype(BF16),
        w_out[0].astype(BF16), norm2_w, router_w[0].T, router_b[0][:, None])

    counts = cnt[:, 0].astype(jnp.int32)
    start = jnp.cumsum(counts) - counts
    dest = rank + jnp.sum(jnp.where(top_e[:, :, None] == jnp.arange(N_EXPERTS)[None, None, :],
                                    start[None, None, :], 0), axis=-1)

    xg = _scatter_rows(dest, h2)
    wg, wu = _prep_gate_up(w_gate_up[0])
    yg = _experts(xg, counts, wg, wu,
                  b_gate_up[0][:, None, 0::2], b_gate_up[0][:, None, 1::2],
                  w_down[0].astype(BF16), b_down[0][:, None, :])
    yk = _gather_rows(dest, yg).reshape(TOP_K, T, D)
    out = _combine(x1, yk, top_w.T, final_norm_w[None, :])
    return out.reshape(B, S, D)
```

```python
import functools
import math

import numpy as np
import jax
import jax.numpy as jnp
from jax import lax
from jax.experimental import pallas as pl
from jax.experimental.pallas import tpu as pltpu
from jax.experimental.pallas import tpu_sc as plsc

ATT_HEADS = 8
ATT_HEAD_DIM = 64
ATT_WIDTH = ATT_HEADS * ATT_HEAD_DIM
MOBA_BLOCK = 256
MOBA_TOPK = 3
ROPE_THETA = 10000.0
HGRN_HEADS = 4
HGRN_DIM = 128
HGRN_WIDTH = HGRN_HEADS * HGRN_DIM
HGRN_CHUNK = 64
N_EXPERTS = 32
TOP_K = 4
SWIGLU_LIMIT = 7.0
SWIGLU_ALPHA = 1.702
RMS_EPS = 1e-6
NEG = -1e30

V7X_LANES = 128
V7X_SUBLANES = 8
V7X_MXU_DIM = 256
V7X_VMEM_LIMIT_BYTES = 56 * 1024 * 1024
V7X_SC_CORES = 2
V7X_SC_SUBCORES = 16

MOBA_GROUP = 4
PROJ_ROWS = 512
PROJ_COLS = 512
HGRN_ROWS = 256
MIX_ROWS = 512
EXPERT_ROWS = 256
SC_ROWS = 32
COMBINE_ROWS = 256
PREP_COLS = 512

F32 = jnp.float32
BF16 = jnp.bfloat16


def _nt_dot(a, b, precision=None):
    return lax.dot_general(a, b, (((1,), (1,)), ((), ())), precision=precision,
                           preferred_element_type=F32)


def _tn_dot(a, b, precision=None):
    return lax.dot_general(a, b, (((0,), (0,)), ((), ())), precision=precision,
                           preferred_element_type=F32)


def _cparams(*sem):
    return pltpu.CompilerParams(dimension_semantics=sem, vmem_limit_bytes=V7X_VMEM_LIMIT_BYTES)


def _in_proj_kernel(x_ref, nw_ref, w_ref, cos_ref, sin_ref,
                    qa_ref, ka_ref, va_ref, qb_ref, fb_ref, ib_ref, gb_ref, ga_ref, gtb_ref):
    x = x_ref[...]
    h = x * lax.rsqrt(jnp.mean(x * x, axis=-1, keepdims=True) + RMS_EPS) * nw_ref[...]
    h = h.astype(BF16)
    cos = cos_ref[...]
    sin = sin_ref[...]

    def proj(c):
        return jnp.dot(h, w_ref[:, c * PROJ_COLS:(c + 1) * PROJ_COLS], preferred_element_type=F32)

    def rope(t):
        out = []
        for j in range(PROJ_COLS // V7X_LANES):
            tj = t[:, j * V7X_LANES:(j + 1) * V7X_LANES]
            out.append(tj * cos + pltpu.roll(tj, V7X_LANES // 2, 1) * sin)
        return jnp.concatenate(out, axis=1)

    qa_ref[...] = (rope(proj(0)) * (ATT_HEAD_DIM ** -0.5)).astype(BF16)
    ka_ref[...] = rope(proj(1)).astype(BF16)
    va_ref[...] = proj(2).astype(BF16)
    qb_ref[...] = proj(3).astype(BF16)
    fb_ref[...] = proj(4)
    ib_ref[...] = proj(5).astype(BF16)
    gb_ref[...] = proj(6).astype(BF16)
    ga_ref[:, :PROJ_COLS] = proj(7).astype(BF16)
    ga_ref[:, PROJ_COLS:] = proj(8).astype(BF16)
    gtb_ref[:, :PROJ_COLS] = proj(9).astype(BF16)
    gtb_ref[:, PROJ_COLS:] = proj(10).astype(BF16)


def _in_proj(x2, norm_w, w_in_bf16, cos_t, sin_t, seq):
    T, D = x2.shape
    tm = min(PROJ_ROWS, seq)
    n_seq_tiles = seq // tm
    row = lambda w: pl.BlockSpec((tm, w), lambda i: (i, 0))
    tab = pl.BlockSpec((tm, V7X_LANES), lambda i: (i % n_seq_tiles, 0))
    widths = [ATT_WIDTH] * 3 + [HGRN_WIDTH] * 4 + [D, D]
    dtypes = [BF16, BF16, BF16, BF16, F32, BF16, BF16, BF16, BF16]
    return pl.pallas_call(
        _in_proj_kernel,
        grid=(T // tm,),
        in_specs=[row(D), pl.BlockSpec((1, D), lambda i: (0, 0)),
                  pl.BlockSpec(memory_space=pltpu.VMEM), tab, tab],
        out_specs=[row(w) for w in widths],
        out_shape=[jax.ShapeDtypeStruct((T, w), dt) for w, dt in zip(widths, dtypes)],
        compiler_params=_cparams("parallel"),
        name="in_proj",
    )(x2, norm_w, w_in_bf16, cos_t, sin_t)


def _moba_kernel(q_ref, k_ref, v_ref, o_ref, kaug_ref, kmean_ref, vt_ref, acc_ref, s_ref, *, nb):
    qi = pl.program_id(2)
    blk = MOBA_BLOCK
    lanes = V7X_LANES

    @pl.when(qi == 0)
    def _():
        kaug_ref[:, :lanes] = k_ref[0]
        rowb = lax.broadcasted_iota(jnp.int32, (nb * blk, lanes), 0) // blk
        col = lax.broadcasted_iota(jnp.int32, (nb * blk, lanes), 1)
        kaug_ref[:, lanes:] = jnp.where(rowb == col, 1.0, 0.0).astype(BF16)
        kmean_ref[...] = jnp.zeros_like(kmean_ref)
        for n in range(nb):
            kb = k_ref[0, n * blk:(n + 1) * blk, :].astype(F32)
            kmean_ref[n:n + 1, :] = jnp.sum(kb, axis=0, keepdims=True) * (1.0 / blk)

        for c in range(nb):
            vt = v_ref[0, c * blk:(c + 1) * blk, :].astype(F32).T
            vt_ref[:lanes, c * blk:(c + 1) * blk] = vt.astype(BF16)
        vt_ref[lanes:, :] = jnp.ones((vt_ref.shape[0] - lanes, nb * blk), BF16)

    qt = q_ref[0].astype(F32).T
    feat = lax.broadcasted_iota(jnp.int32, (lanes, blk), 0)
    key_i = lax.broadcasted_iota(jnp.int32, (blk, blk), 0)
    qry_i = lax.broadcasted_iota(jnp.int32, (blk, blk), 1)
    own = pl.multiple_of(qi * blk, blk)
    k_own = kaug_ref[pl.ds(own, blk), :lanes]
    nbp = kmean_ref.shape[0]
    blk_id = lax.broadcasted_iota(jnp.int32, (nbp, blk), 0)
    slab = 2 * V7X_SUBLANES

    def slab_max(s):
        return jnp.max(s.reshape(s.shape[0] // slab, slab, blk), axis=0)

    q_aug, m_init = [], []
    for hh in range(2):
        qh = jnp.where((feat // (ATT_HEAD_DIM // 2)) % 2 == hh, qt, 0.0)
        gate = jnp.dot(kmean_ref[...], qh, precision=lax.Precision.HIGHEST,
                       preferred_element_type=F32)
        gate = jnp.where(blk_id < qi, gate, NEG)
        beaten = jnp.zeros((nbp, blk), F32)
        for n in range(nb):
            gn = gate[n:n + 1, :]
            wins = (gn > gate) | ((gn == gate) & (blk_id > n))
            beaten = beaten + jnp.where(wins, 1.0, 0.0)
        sel = (beaten < MOBA_TOPK) & (blk_id < qi)
        bias = jnp.where(sel, 0.0, NEG)
        bias = jnp.concatenate([bias, jnp.zeros((lanes - nbp, blk), F32)], axis=0)
        qh = qh.astype(BF16)
        q_aug.append(jnp.concatenate([qh, bias.astype(BF16)], axis=0))

        m_init.append(jnp.full((slab, blk), NEG, F32))
        acc_ref[hh] = jnp.zeros(acc_ref.shape[1:], F32)

    group = MOBA_GROUP * blk
    n_groups = (qi + MOBA_GROUP) // MOBA_GROUP

    def score_group(g, ms):
        off = pl.multiple_of(g * group, group)
        kb = kaug_ref[pl.ds(off, group), :]
        out = []
        for hh in range(2):
            s = jnp.dot(kb, q_aug[hh], preferred_element_type=F32)
            s_ref[hh, pl.ds(off, group), :] = s
            out.append(jnp.maximum(ms[hh], slab_max(s)))
        return tuple(out)

    ms = lax.fori_loop(0, n_groups, score_group, tuple(m_init))

    m_fin = []
    for hh in range(2):
        s = jnp.dot(k_own, q_aug[hh][:lanes], preferred_element_type=F32)
        s = jnp.where(key_i <= qry_i, s, NEG)
        s_ref[hh, pl.ds(own, blk), :] = s
        m = jnp.maximum(ms[hh], slab_max(s))
        m_fin.append(jnp.max(m, axis=0, keepdims=True))

    def value_group(g, c):
        off = pl.multiple_of(g * group, group)
        vb = vt_ref[:, pl.ds(off, group)]
        for hh in range(2):
            p = jnp.exp(s_ref[hh, pl.ds(off, group), :] - m_fin[hh]).astype(BF16)
            acc_ref[hh] += jnp.dot(vb, p, preferred_element_type=F32)
        return c

    lax.fori_loop(0, n_groups, value_group, 0)

    half = lanes // 2
    o0 = acc_ref[0, :half, :] / acc_ref[0, lanes:lanes + 1, :]
    o1 = acc_ref[1, half:lanes, :] / acc_ref[1, lanes:lanes + 1, :]
    o_ref[0] = jnp.concatenate([o0, o1], axis=0).T.astype(BF16)


def _moba(q, k, v):
    B, S, _ = q.shape
    nb = S // MOBA_BLOCK
    assert nb % MOBA_GROUP == 0
    nbp = -(-nb // V7X_SUBLANES) * V7X_SUBLANES
    n_pairs = ATT_WIDTH // V7X_LANES
    vt_rows = V7X_LANES + 2 * V7X_SUBLANES
    qspec = pl.BlockSpec((1, MOBA_BLOCK, V7X_LANES), lambda b, p, i: (b, i, p))
    kvspec = pl.BlockSpec((1, S, V7X_LANES), lambda b, p, i: (b, 0, p))
    return pl.pallas_call(
        functools.partial(_moba_kernel, nb=nb),
        grid=(B, n_pairs, nb),
        in_specs=[qspec, kvspec, kvspec],
        out_specs=qspec,
        out_shape=jax.ShapeDtypeStruct((B, S, ATT_WIDTH), BF16),
        scratch_shapes=[pltpu.VMEM((S, 2 * V7X_LANES), BF16), pltpu.VMEM((nbp, V7X_LANES), F32),
                        pltpu.VMEM((vt_rows, S), BF16), pltpu.VMEM((2, vt_rows, MOBA_BLOCK), F32),
                        pltpu.VMEM((2, S, MOBA_BLOCK), F32)],
        compiler_params=_cparams("parallel", "parallel", "arbitrary"),
        name="moba",
    )(q, k, v)


def _hgrn_level_sizes(chunk):
    return [chunk >> (i + 1) for i in range(int(math.log2(chunk)))]


def _hgrn_constants(chunk):
    t = np.arange(chunk)
    mats = [(t[None, :] <= t[:, None]),
            (t[None, :] > t[:, None])]
    qrows, pmasks = [], []
    for bs in _hgrn_level_sizes(chunk):
        blk = t // bs
        odd = (blk % 2) == 1
        lo, hi = blk * bs, (blk + 1) * bs
        u = t[None, :]
        m_odd = (u >= lo[:, None]) & (u <= t[:, None])
        m_even = (u > t[:, None]) & (u < hi[:, None])
        mats.append(np.where(odd[:, None], m_odd, m_even))
        qrows.append(odd)
        pmasks.append(odd[:, None] & (blk[None, :] == blk[:, None] - 1))
    pmasks.append(t[None, :] == t[:, None])
    summat = np.concatenate(mats, axis=0).astype(np.float32)
    qrow = np.stack(qrows, axis=0).astype(np.float32)
    pmask = np.stack(pmasks, axis=0).astype(np.float32)
    return summat, qrow, pmask


def _hgrn_kernel(q_ref, f_ref, i_ref, g_ref, lb_ref, nw_ref, sm_ref, qrow_ref, pm_ref,
                 o_ref, state_ref, *, rows):
    C = HGRN_CHUNK
    n_levels = qrow_ref.shape[0]

    @pl.when(pl.program_id(1) == 0)
    def _():
        state_ref[...] = jnp.zeros_like(state_ref)

    lb = lb_ref[...]
    summat = sm_ref[...]
    for c in range(rows // C):
        r0 = c * C
        fg = lb + (1.0 - lb) * jax.nn.sigmoid(f_ref[0, r0:r0 + C, :])
        logf = jnp.log(fg)
        hi = logf.astype(BF16)
        rem = logf - hi.astype(F32)
        mid = rem.astype(BF16)
        lo = (rem - mid.astype(F32)).astype(BF16)
        sums = jnp.dot(summat, jnp.concatenate([hi, mid, lo], axis=0),
                       preferred_element_type=F32)
        for h in range(HGRN_HEADS):
            ls = slice(h * HGRN_DIM, (h + 1) * HGRN_DIM)
            qf = jax.nn.silu(q_ref[0, r0:r0 + C, ls].astype(F32))
            kf = 1.0 - fg[:, ls]
            iv = i_ref[0, r0:r0 + C, ls]
            bcum = sums[0:C, ls]
            bsuf = sums[C:2 * C, ls]
            att = _nt_dot(qf.astype(BF16), kf.astype(BF16)) * pm_ref[n_levels]
            for lv in range(n_levels):
                w = jnp.exp(sums[(2 + lv) * C:(3 + lv) * C, ls])
                qrow = qrow_ref[lv]
                z = (jnp.where(qrow > 0.5, qf, kf) * w).astype(BF16)
                att = att + _nt_dot(z, z) * pm_ref[lv]
            o = jnp.dot(att.astype(BF16), iv, preferred_element_type=F32)
            st = state_ref[h]
            o = o + _nt_dot((qf * jnp.exp(bcum)).astype(BF16), st.astype(BF16))
            kdec = (kf * jnp.exp(bsuf)).astype(BF16)
            state_ref[h] = st * jnp.exp(bcum[C - 1:C, :]) + _tn_dot(iv, kdec)
            o = o * lax.rsqrt(jnp.mean(o * o, axis=-1, keepdims=True) + RMS_EPS)
            o = o * nw_ref[:, ls] * jax.nn.silu(g_ref[0, r0:r0 + C, ls].astype(F32))
            o_ref[0, r0:r0 + C, ls] = o.astype(BF16)


def _hgrn(qb, fb, ib, gb, lb, norm_w):
    B, S, W = qb.shape
    rows = min(HGRN_ROWS, S)
    summat, qrow, pmask = _hgrn_constants(HGRN_CHUNK)
    summat = jnp.asarray(np.concatenate([summat] * 3, axis=1), BF16)
    n_levels = qrow.shape[0]
    blk = pl.BlockSpec((1, rows, W), lambda b, s: (b, s, 0))
    vec = pl.BlockSpec((1, W), lambda b, s: (0, 0))
    const = lambda a: pl.BlockSpec(a.shape, lambda b, s: (0,) * a.ndim)
    qrow3 = qrow.reshape(n_levels, HGRN_CHUNK, 1)
    return pl.pallas_call(
        functools.partial(_hgrn_kernel, rows=rows),
        grid=(B, S // rows),
        in_specs=[blk, blk, blk, blk, vec, vec, const(summat), const(qrow3), const(pmask)],
        out_specs=blk,
        out_shape=jax.ShapeDtypeStruct((B, S, W), BF16),
        scratch_shapes=[pltpu.VMEM((HGRN_HEADS, HGRN_DIM, HGRN_DIM), F32)],
        compiler_params=_cparams("parallel", "arbitrary"),
        name="hgrn",
    )(qb, fb, ib, gb, lb, norm_w, jnp.asarray(summat), jnp.asarray(qrow3), jnp.asarray(pmask))


def _mix_route_kernel(ya_ref, yb_ref, ga_ref, gb_ref, x_ref, wa_ref, wb_ref, wo_ref, nw_ref,
                      rw_ref, rb_ref, tri_ref,
                      x1_ref, h2_ref, e_ref, w_ref, rank_ref, cnt_ref, carry_ref):
    @pl.when(pl.program_id(0) == 0)
    def _():
        carry_ref[...] = jnp.zeros_like(carry_ref)

    ua = jnp.dot(ya_ref[...], wa_ref[...], preferred_element_type=F32)
    ub = jnp.dot(yb_ref[...], wb_ref[...], preferred_element_type=F32)
    mixed = (jax.nn.sigmoid(ga_ref[...].astype(F32)) * ua
             + jax.nn.sigmoid(gb_ref[...].astype(F32)) * ub)
    x1 = x_ref[...] + jnp.dot(mixed.astype(BF16), wo_ref[...], preferred_element_type=F32)
    x1_ref[...] = x1
    h2 = x1 * lax.rsqrt(jnp.mean(x1 * x1, axis=-1, keepdims=True) + RMS_EPS) * nw_ref[...]
    h2_ref[...] = h2

    tm = x1.shape[0]
    logits = _nt_dot(rw_ref[...], h2, precision=lax.Precision.HIGHEST) + rb_ref[...]
    eid = lax.broadcasted_iota(jnp.int32, (N_EXPERTS, tm), 0)
    work = logits
    es, vs = [], []
    for _ in range(TOP_K):
        mx = jnp.max(work, axis=0, keepdims=True)
        idx = jnp.min(jnp.where(work == mx, eid, N_EXPERTS), axis=0, keepdims=True)
        es.append(idx)
        vs.append(mx)
        work = jnp.where(eid == idx, -jnp.inf, work)
    ex = [jnp.exp(v - vs[0]) for v in vs]
    den = ex[0] + ex[1] + ex[2] + ex[3]
    multi = jnp.zeros((N_EXPERTS, tm), F32)
    for k in range(TOP_K):
        multi = multi + jnp.where(eid == es[k], 1.0, 0.0)
    before = jnp.dot(multi.astype(BF16), tri_ref[...], preferred_element_type=F32) + carry_ref[...]
    for k in range(TOP_K):
        e_ref[k:k + 1, :] = es[k]
        w_ref[k:k + 1, :] = ex[k] / den
        rank_ref[k:k + 1, :] = jnp.sum(jnp.where(eid == es[k], before, 0.0), axis=0,
                                       keepdims=True).astype(jnp.int32)
    carry_ref[...] = carry_ref[...] + jnp.sum(multi, axis=1, keepdims=True)
    cnt_ref[...] = jnp.broadcast_to(carry_ref[...], cnt_ref.shape)


def _mix_route(ya, yb, ga, gb, x2, wa, wb, wo, norm_w, rw_t, rb):
    T, D = x2.shape
    tm = min(MIX_ROWS, T)
    row = lambda w: pl.BlockSpec((tm, w), lambda i: (i, 0))
    whole = pl.BlockSpec(memory_space=pltpu.VMEM)
    kt = pl.BlockSpec((TOP_K, tm), lambda i: (0, i))
    tri = jnp.asarray(np.triu(np.ones((tm, tm), np.float32), 1), BF16)
    return pl.pallas_call(
        _mix_route_kernel,
        grid=(T // tm,),
        in_specs=[row(ATT_WIDTH), row(HGRN_WIDTH), row(D), row(D), row(D),
                  whole, whole, whole, pl.BlockSpec((1, D), lambda i: (0, 0)),
                  whole, whole, whole],
        out_specs=[row(D), row(D), kt, kt, kt,
                   pl.BlockSpec((N_EXPERTS, V7X_LANES), lambda i: (0, 0))],
        out_shape=[jax.ShapeDtypeStruct((T, D), F32), jax.ShapeDtypeStruct((T, D), F32),
                   jax.ShapeDtypeStruct((TOP_K, T), jnp.int32),
                   jax.ShapeDtypeStruct((TOP_K, T), F32),
                   jax.ShapeDtypeStruct((TOP_K, T), jnp.int32),
                   jax.ShapeDtypeStruct((N_EXPERTS, V7X_LANES), F32)],
        scratch_shapes=[pltpu.VMEM((N_EXPERTS, 1), F32)],
        compiler_params=_cparams("arbitrary"),
        name="mix_route",
    )(ya, yb, ga, gb, x2, wa, wb, wo, norm_w, rw_t, rb, tri)


def _sc_worker_id():
    return lax.axis_index("s") * V7X_SC_CORES + lax.axis_index("c")


def _sc_kernel(body, out_rows, like, window, name):
    mesh = plsc.VectorSubcoreMesh(core_axis_name="c", subcore_axis_name="s")
    return pl.kernel(
        body, mesh=mesh,
        out_type=jax.ShapeDtypeStruct((out_rows,) + like.shape[1:], like.dtype),
        scratch_types=[pltpu.VMEM((window,), jnp.int32),
                       pltpu.VMEM((window,) + like.shape[1:], like.dtype)],
        name=name)


def _scatter_rows(dest, h2):
    T = dest.shape[1]
    n_workers = V7X_SC_CORES * V7X_SC_SUBCORES
    per_worker = T // n_workers
    window = min(SC_ROWS, per_worker)

    def body(dest_hbm, src_hbm, dst_hbm, idx_v, rows_v):
        base = _sc_worker_id() * per_worker

        @pl.loop(0, per_worker // window)
        def _(c):
            t0 = base + c * window
            pltpu.sync_copy(src_hbm.at[pl.ds(t0, window)], rows_v)
            for k in range(TOP_K):
                pltpu.sync_copy(dest_hbm.at[pl.ds(k * T + t0, window)], idx_v)
                pltpu.sync_copy(rows_v, dst_hbm.at[idx_v])

    return _sc_kernel(body, TOP_K * T, h2, window, "scatter_rows")(dest.reshape(-1), h2)


def _gather_rows(dest, yg):
    T = dest.shape[1]
    n_workers = V7X_SC_CORES * V7X_SC_SUBCORES
    per_worker = TOP_K * T // n_workers
    window = min(SC_ROWS, per_worker)

    def body(dest_hbm, src_hbm, dst_hbm, idx_v, rows_v):
        base = _sc_worker_id() * per_worker

        @pl.loop(0, per_worker // window)
        def _(c):
            r0 = base + c * window
            pltpu.sync_copy(dest_hbm.at[pl.ds(r0, window)], idx_v)
            pltpu.sync_copy(src_hbm.at[idx_v], rows_v)
            pltpu.sync_copy(rows_v, dst_hbm.at[pl.ds(r0, window)])

    return _sc_kernel(body, TOP_K * T, yg, window, "gather_rows")(dest.reshape(-1), yg)


def _prep_gate_up_kernel(w_ref, perm_ref, wg_ref, wu_ref):
    half = V7X_MXU_DIM // 2
    for g in range(w_ref.shape[2] // V7X_MXU_DIM):
        wb = w_ref[0, :, g * V7X_MXU_DIM:(g + 1) * V7X_MXU_DIM].astype(BF16)
        d = jnp.dot(wb, perm_ref[...], preferred_element_type=F32).astype(BF16)
        wg_ref[0, :, g * half:(g + 1) * half] = d[:, :half]
        wu_ref[0, :, g * half:(g + 1) * half] = d[:, half:]


def _prep_gate_up(w_gate_up):
    E, D, F2 = w_gate_up.shape
    cols = min(PREP_COLS, F2)
    i = np.arange(V7X_MXU_DIM)
    src = np.where(i < V7X_MXU_DIM // 2, 2 * i, 2 * (i - V7X_MXU_DIM // 2) + 1)
    perm = np.zeros((V7X_MXU_DIM, V7X_MXU_DIM), np.float32)
    perm[src, i] = 1.0
    out = pl.BlockSpec((1, D, cols // 2), lambda e, c: (e, 0, c))
    return pl.pallas_call(
        _prep_gate_up_kernel,
        grid=(E, F2 // cols),
        in_specs=[pl.BlockSpec((1, D, cols), lambda e, c: (e, 0, c)),
                  pl.BlockSpec((V7X_MXU_DIM, V7X_MXU_DIM), lambda e, c: (0, 0))],
        out_specs=[out, out],
        out_shape=[jax.ShapeDtypeStruct((E, D, F2 // 2), BF16)] * 2,
        compiler_params=_cparams("parallel", "parallel"),
        name="prep_gate_up",
    )(w_gate_up, jnp.asarray(perm, BF16))


def _experts_kernel(blk_ref, exp_ref, lo_ref, hi_ref, first_ref,
                    x_ref, wg_ref, wu_ref, bg_ref, bu_ref, wd_ref, bd_ref, y_ref):
    w = pl.program_id(0)
    lo = lo_ref[w]
    hi = hi_ref[w]

    @pl.when(lo < hi)
    def _():
        x = x_ref[...].astype(BF16)
        g = jnp.dot(x, wg_ref[0], preferred_element_type=F32) + bg_ref[0]
        u = jnp.dot(x, wu_ref[0], preferred_element_type=F32) + bu_ref[0]
        g = jnp.minimum(g, SWIGLU_LIMIT)
        u = jnp.clip(u, -SWIGLU_LIMIT, SWIGLU_LIMIT)
        act = (u + 1.0) * (g * jax.nn.sigmoid(g * SWIGLU_ALPHA))
        y = jnp.dot(act.astype(BF16), wd_ref[0], preferred_element_type=F32) + bd_ref[0]
        whole = hi - lo == y.shape[0]

        @pl.when(whole)
        def _():
            y_ref[...] = y

        r = lax.broadcasted_iota(jnp.int32, y.shape, 0)
        mine = (r >= lo) & (r < hi)

        @pl.when(jnp.logical_not(whole) & (first_ref[w] == 1))
        def _():
            y_ref[...] = jnp.where(mine, y, 0.0)

        @pl.when(jnp.logical_not(whole) & (first_ref[w] == 0))
        def _():
            y_ref[...] = jnp.where(mine, y, y_ref[...])


def _work_items(counts, n_rows, tm):
    nblk = n_rows // tm
    n_items = nblk + N_EXPERTS - 1
    end = jnp.cumsum(counts)
    start = end - counts
    fb = start // tm
    nitems = jnp.where(counts > 0, (end - 1) // tm - fb + 1, 0)
    item_end = jnp.cumsum(nitems)
    item_start = item_end - nitems
    w = jnp.arange(n_items, dtype=jnp.int32)
    valid = w < item_end[-1]
    wc = jnp.minimum(w, item_end[-1] - 1)
    e = jnp.sum(wc[:, None] >= item_end[None, :], axis=1).astype(jnp.int32)
    e = jnp.minimum(e, N_EXPERTS - 1)
    blk = (fb[e] + (wc - item_start[e])).astype(jnp.int32)
    lo = jnp.maximum(start[e], blk * tm) - blk * tm
    hi = jnp.minimum(end[e], (blk + 1) * tm) - blk * tm
    lo = jnp.where(valid, lo, 0).astype(jnp.int32)
    hi = jnp.where(valid, hi, 0).astype(jnp.int32)
    first = jnp.concatenate([jnp.ones((1,), jnp.int32),
                             (blk[1:] != blk[:-1]).astype(jnp.int32)])
    return blk, e, lo, hi, first


def _experts(xg, counts, wg, wu, bg, bu, wd, bd):
    A, D = xg.shape
    tm = min(EXPERT_ROWS, A)
    F = wg.shape[2]
    items = _work_items(counts, A, tm)
    n_items = A // tm + N_EXPERTS - 1
    xs = pl.BlockSpec((tm, D), lambda w, blk, e, lo, hi, fi: (blk[w], 0))
    wsp = lambda r, c: pl.BlockSpec((1, r, c), lambda w, blk, e, lo, hi, fi: (e[w], 0, 0))
    return pl.pallas_call(
        _experts_kernel,
        grid_spec=pltpu.PrefetchScalarGridSpec(
            num_scalar_prefetch=5,
            grid=(n_items,),
            in_specs=[xs, wsp(D, F), wsp(D, F), wsp(1, F), wsp(1, F), wsp(F, D), wsp(1, D)],
            out_specs=xs,
        ),
        out_shape=jax.ShapeDtypeStruct(xg.shape, F32),
        compiler_params=_cparams("arbitrary"),
        name="experts",
    )(*items, xg, wg, wu, bg, bu, wd, bd)


def _combine_kernel(x1_ref, yk_ref, w_ref, nw_ref, o_ref):
    w = w_ref[...]
    x = x1_ref[...]
    for k in range(TOP_K):
        x = x + yk_ref[k] * w[:, k:k + 1]
    o_ref[...] = x * lax.rsqrt(jnp.mean(x * x, axis=-1, keepdims=True) + RMS_EPS) * nw_ref[...]


def _combine(x1, yk, w_tk, norm_w):
    T, D = x1.shape
    tm = min(COMBINE_ROWS, T)
    row = pl.BlockSpec((tm, D), lambda i: (i, 0))
    return pl.pallas_call(
        _combine_kernel,
        grid=(T // tm,),
        in_specs=[row, pl.BlockSpec((TOP_K, tm, D), lambda i: (0, i, 0)),
                  pl.BlockSpec((tm, TOP_K), lambda i: (i, 0)),
                  pl.BlockSpec((1, D), lambda i: (0, 0))],
        out_specs=row,
        out_shape=jax.ShapeDtypeStruct((T, D), F32),
        compiler_params=_cparams("parallel"),
        name="combine",
    )(x1, yk, w_tk, norm_w)


def _qk_column_order():
    half = ATT_HEAD_DIM // 2
    order = []
    for p in range(ATT_HEADS // 2):
        for part in range(2):
            for h in (2 * p, 2 * p + 1):
                order.extend(range(h * ATT_HEAD_DIM + part * half, h * ATT_HEAD_DIM + (part + 1) * half))
    return np.asarray(order, np.int32)


def _rope_tables(seq):
    half = ATT_HEAD_DIM // 2
    inv = ROPE_THETA ** (-(jnp.arange(half, dtype=F32) * 2.0 / ATT_HEAD_DIM))
    ang = jnp.arange(seq, dtype=F32)[:, None] * inv[None, :]
    cos, sin = jnp.cos(ang), jnp.sin(ang)
    return (jnp.concatenate([cos, cos, cos, cos], axis=1),
            jnp.concatenate([-sin, -sin, sin, sin], axis=1))


def kernel(x, norm1_w, w_in, moba_up, hgrn_lb_logits, hgrn_norm_w, hgrn_up, w_out, norm2_w,
           router_w, router_b, w_gate_up, b_gate_up, w_down, b_down, final_norm_w):
    B, S, D = x.shape
    T = B * S
    assert S % MOBA_BLOCK == 0 and w_in.shape[0] == 1
    x2 = x.reshape(T, D)

    perm = _qk_column_order()
    w0 = w_in[0]
    w_in_p = jnp.concatenate([w0[:, :ATT_WIDTH][:, perm], w0[:, ATT_WIDTH:2 * ATT_WIDTH][:, perm],
                              w0[:, 2 * ATT_WIDTH:]], axis=1).astype(BF16)
    cos_t, sin_t = _rope_tables(S)
    lb = jnp.cumsum(jax.nn.softmax(hgrn_lb_logits.astype(F32), axis=0), axis=0)[0:1]

    qa, ka, va, qb, fb, ib, gb, ga, gtb = _in_proj(x2, norm1_w, w_in_p, cos_t, sin_t, S)
    r3 = lambda a: a.reshape(B, S, a.shape[1])
    ya = _moba(r3(qa), r3(ka), r3(va)).reshape(T, ATT_WIDTH)
    yb = _hgrn(r3(qb), r3(fb), r3(ib), r3(gb), lb, hgrn_norm_w).reshape(T, HGRN_WIDTH)

    x1, h2, top_e, top_w, rank, cnt = _mix_route(
        ya, yb, ga, gtb, x2, moba_up[0].astype(BF16), hgrn_up[0].astype(BF16),
        w_out[0].astype(BF16), norm2_w, router_w[0].T, router_b[0][:, None])

    counts = cnt[:, 0].astype(jnp.int32)
    start = jnp.cumsum(counts) - counts
    dest = rank + jnp.sum(jnp.where(top_e[:, :, None] == jnp.arange(N_EXPERTS)[None, None, :],
                                    start[None, None, :], 0), axis=-1)

    xg = _scatter_rows(dest, h2)
    wg, wu = _prep_gate_up(w_gate_up[0])
    yg = _experts(xg, counts, wg, wu,
                  b_gate_up[0][:, None, 0::2], b_gate_up[0][:, None, 1::2],
                  w_down[0].astype(BF16), b_down[0][:, None, :])
    yk = _gather_rows(dest, yg).reshape(TOP_K, T, D)
    out = _combine(x1, yk, top_w.T, final_norm_w[None, :])
    return out.reshape(B, S, D)
```

```python
import functools
import math

import numpy as np
import jax
import jax.numpy as jnp
from jax import lax
from jax.experimental import pallas as pl
from jax.experimental.pallas import tpu as pltpu
from jax.experimental.pallas import tpu_sc as plsc

ATT_HEADS = 8
ATT_HEAD_DIM = 64
ATT_WIDTH = ATT_HEADS * ATT_HEAD_DIM
MOBA_BLOCK = 256
MOBA_TOPK = 3
ROPE_THETA = 10000.0
HGRN_HEADS = 4
HGRN_DIM = 128
HGRN_WIDTH = HGRN_HEADS * HGRN_DIM
HGRN_CHUNK = 64
N_EXPERTS = 32
TOP_K = 4
SWIGLU_LIMIT = 7.0
SWIGLU_ALPHA = 1.702
RMS_EPS = 1e-6
NEG = -1e30

V7X_LANES = 128
V7X_SUBLANES = 8
V7X_MXU_DIM = 256
V7X_VMEM_LIMIT_BYTES = 56 * 1024 * 1024
V7X_SC_CORES = 2
V7X_SC_SUBCORES = 16

MOBA_GROUP = 4
PROJ_ROWS = 512
PROJ_COLS = 512
HGRN_ROWS = 256
MIX_ROWS = 512
EXPERT_ROWS = 256
MOE_TOKEN_GROUPS = 2
SC_ROWS = 32
COMBINE_ROWS = 256
PREP_COLS = 512

F32 = jnp.float32
BF16 = jnp.bfloat16


def _nt_dot(a, b, precision=None):
    return lax.dot_general(a, b, (((1,), (1,)), ((), ())), precision=precision,
                           preferred_element_type=F32)


def _tn_dot(a, b, precision=None):
    return lax.dot_general(a, b, (((0,), (0,)), ((), ())), precision=precision,
                           preferred_element_type=F32)


def _cparams(*sem):
    return pltpu.CompilerParams(dimension_semantics=sem, vmem_limit_bytes=V7X_VMEM_LIMIT_BYTES)


def _in_proj_kernel(x_ref, nw_ref, w_ref, cos_ref, sin_ref,
                    qa_ref, ka_ref, va_ref, qb_ref, fb_ref, ib_ref, gb_ref, ga_ref, gtb_ref):
    x = x_ref[...]
    h = x * lax.rsqrt(jnp.mean(x * x, axis=-1, keepdims=True) + RMS_EPS) * nw_ref[...]
    h = h.astype(BF16)
    cos = cos_ref[...]
    sin = sin_ref[...]

    def proj(c):
        return jnp.dot(h, w_ref[:, c * PROJ_COLS:(c + 1) * PROJ_COLS], preferred_element_type=F32)

    def rope(t):
        out = []
        for j in range(PROJ_COLS // V7X_LANES):
            tj = t[:, j * V7X_LANES:(j + 1) * V7X_LANES]
            out.append(tj * cos + pltpu.roll(tj, V7X_LANES // 2, 1) * sin)
        return jnp.concatenate(out, axis=1)

    qa_ref[...] = (rope(proj(0)) * (ATT_HEAD_DIM ** -0.5)).astype(BF16)
    ka_ref[...] = rope(proj(1)).astype(BF16)
    va_ref[...] = proj(2).astype(BF16)
    qb_ref[...] = proj(3).astype(BF16)
    fb_ref[...] = proj(4)
    ib_ref[...] = proj(5).astype(BF16)
    gb_ref[...] = proj(6).astype(BF16)
    ga_ref[:, :PROJ_COLS] = proj(7).astype(BF16)
    ga_ref[:, PROJ_COLS:] = proj(8).astype(BF16)
    gtb_ref[:, :PROJ_COLS] = proj(9).astype(BF16)
    gtb_ref[:, PROJ_COLS:] = proj(10).astype(BF16)


def _in_proj(x2, norm_w, w_in_bf16, cos_t, sin_t, seq):
    T, D = x2.shape
    tm = min(PROJ_ROWS, seq)
    n_seq_tiles = seq // tm
    row = lambda w: pl.BlockSpec((tm, w), lambda i: (i, 0))
    tab = pl.BlockSpec((tm, V7X_LANES), lambda i: (i % n_seq_tiles, 0))
    widths = [ATT_WIDTH] * 3 + [HGRN_WIDTH] * 4 + [D, D]
    dtypes = [BF16, BF16, BF16, BF16, F32, BF16, BF16, BF16, BF16]
    return pl.pallas_call(
        _in_proj_kernel,
        grid=(T // tm,),
        in_specs=[row(D), pl.BlockSpec((1, D), lambda i: (0, 0)),
                  pl.BlockSpec(memory_space=pltpu.VMEM), tab, tab],
        out_specs=[row(w) for w in widths],
        out_shape=[jax.ShapeDtypeStruct((T, w), dt) for w, dt in zip(widths, dtypes)],
        compiler_params=_cparams("parallel"),
        name="in_proj",
    )(x2, norm_w, w_in_bf16, cos_t, sin_t)


def _moba_kernel(q_ref, k_ref, v_ref, o_ref, kaug_ref, kmean_ref, vt_ref, acc_ref, s_ref, *, nb):
    qi = pl.program_id(2)
    blk = MOBA_BLOCK
    lanes = V7X_LANES

    @pl.when(qi == 0)
    def _():
        kaug_ref[:, :lanes] = k_ref[0]
        rowb = lax.broadcasted_iota(jnp.int32, (nb * blk, lanes), 0) // blk
        col = lax.broadcasted_iota(jnp.int32, (nb * blk, lanes), 1)
        kaug_ref[:, lanes:] = jnp.where(rowb == col, 1.0, 0.0).astype(BF16)
        kmean_ref[...] = jnp.zeros_like(kmean_ref)
        for n in range(nb):
            kb = k_ref[0, n * blk:(n + 1) * blk, :].astype(F32)
            kmean_ref[n:n + 1, :] = jnp.sum(kb, axis=0, keepdims=True) * (1.0 / blk)

        for c in range(nb):
            vt = v_ref[0, c * blk:(c + 1) * blk, :].astype(F32).T
            vt_ref[:lanes, c * blk:(c + 1) * blk] = vt.astype(BF16)
        vt_ref[lanes:, :] = jnp.ones((vt_ref.shape[0] - lanes, nb * blk), BF16)

    qt = q_ref[0].astype(F32).T
    feat = lax.broadcasted_iota(jnp.int32, (lanes, blk), 0)
    key_i = lax.broadcasted_iota(jnp.int32, (blk, blk), 0)
    qry_i = lax.broadcasted_iota(jnp.int32, (blk, blk), 1)
    own = pl.multiple_of(qi * blk, blk)
    k_own = kaug_ref[pl.ds(own, blk), :lanes]
    nbp = kmean_ref.shape[0]
    blk_id = lax.broadcasted_iota(jnp.int32, (nbp, blk), 0)
    slab = 2 * V7X_SUBLANES

    def slab_max(s):
        return jnp.max(s.reshape(s.shape[0] // slab, slab, blk), axis=0)

    q_aug, m_init = [], []
    for hh in range(2):
        qh = jnp.where((feat // (ATT_HEAD_DIM // 2)) % 2 == hh, qt, 0.0)
        gate = jnp.dot(kmean_ref[...], qh, precision=lax.Precision.HIGHEST,
                       preferred_element_type=F32)
        gate = jnp.where(blk_id < qi, gate, NEG)
        beaten = jnp.zeros((nbp, blk), F32)
        for n in range(nb):
            gn = gate[n:n + 1, :]
            wins = (gn > gate) | ((gn == gate) & (blk_id > n))
            beaten = beaten + jnp.where(wins, 1.0, 0.0)
        sel = (beaten < MOBA_TOPK) & (blk_id < qi)
        bias = jnp.where(sel, 0.0, NEG)
        bias = jnp.concatenate([bias, jnp.zeros((lanes - nbp, blk), F32)], axis=0)
        qh = qh.astype(BF16)
        q_aug.append(jnp.concatenate([qh, bias.astype(BF16)], axis=0))

        m_init.append(jnp.full((slab, blk), NEG, F32))
        acc_ref[hh] = jnp.zeros(acc_ref.shape[1:], F32)

    group = MOBA_GROUP * blk
    n_groups = (qi + MOBA_GROUP) // MOBA_GROUP

    def score_group(g, ms):
        off = pl.multiple_of(g * group, group)
        kb = kaug_ref[pl.ds(off, group), :]
        out = []
        for hh in range(2):
            s = jnp.dot(kb, q_aug[hh], preferred_element_type=F32)
            s_ref[hh, pl.ds(off, group), :] = s
            out.append(jnp.maximum(ms[hh], slab_max(s)))
        return tuple(out)

    ms = lax.fori_loop(0, n_groups, score_group, tuple(m_init))

    m_fin = []
    for hh in range(2):
        s = jnp.dot(k_own, q_aug[hh][:lanes], preferred_element_type=F32)
        s = jnp.where(key_i <= qry_i, s, NEG)
        s_ref[hh, pl.ds(own, blk), :] = s
        m = jnp.maximum(ms[hh], slab_max(s))
        m_fin.append(jnp.max(m, axis=0, keepdims=True))

    def value_group(g, c):
        off = pl.multiple_of(g * group, group)
        vb = vt_ref[:, pl.ds(off, group)]
        for hh in range(2):
            p = jnp.exp(s_ref[hh, pl.ds(off, group), :] - m_fin[hh]).astype(BF16)
            acc_ref[hh] += jnp.dot(vb, p, preferred_element_type=F32)
        return c

    lax.fori_loop(0, n_groups, value_group, 0)

    half = lanes // 2
    o0 = acc_ref[0, :half, :] / acc_ref[0, lanes:lanes + 1, :]
    o1 = acc_ref[1, half:lanes, :] / acc_ref[1, lanes:lanes + 1, :]
    o_ref[0] = jnp.concatenate([o0, o1], axis=0).T.astype(BF16)


def _moba(q, k, v):
    B, S, _ = q.shape
    nb = S // MOBA_BLOCK
    assert nb % MOBA_GROUP == 0
    nbp = -(-nb // V7X_SUBLANES) * V7X_SUBLANES
    n_pairs = ATT_WIDTH // V7X_LANES
    vt_rows = V7X_LANES + 2 * V7X_SUBLANES
    qspec = pl.BlockSpec((1, MOBA_BLOCK, V7X_LANES), lambda b, p, i: (b, i, p))
    kvspec = pl.BlockSpec((1, S, V7X_LANES), lambda b, p, i: (b, 0, p))
    return pl.pallas_call(
        functools.partial(_moba_kernel, nb=nb),
        grid=(B, n_pairs, nb),
        in_specs=[qspec, kvspec, kvspec],
        out_specs=qspec,
        out_shape=jax.ShapeDtypeStruct((B, S, ATT_WIDTH), BF16),
        scratch_shapes=[pltpu.VMEM((S, 2 * V7X_LANES), BF16), pltpu.VMEM((nbp, V7X_LANES), F32),
                        pltpu.VMEM((vt_rows, S), BF16), pltpu.VMEM((2, vt_rows, MOBA_BLOCK), F32),
                        pltpu.VMEM((2, S, MOBA_BLOCK), F32)],
        compiler_params=_cparams("parallel", "parallel", "arbitrary"),
        name="moba",
    )(q, k, v)


def _hgrn_level_sizes(chunk):
    return [chunk >> (i + 1) for i in range(int(math.log2(chunk)))]


def _hgrn_constants(chunk):
    t = np.arange(chunk)
    mats = [(t[None, :] <= t[:, None]),
            (t[None, :] > t[:, None])]
    qrows, pmasks = [], []
    for bs in _hgrn_level_sizes(chunk):
        blk = t // bs
        odd = (blk % 2) == 1
        lo, hi = blk * bs, (blk + 1) * bs
        u = t[None, :]
        m_odd = (u >= lo[:, None]) & (u <= t[:, None])
        m_even = (u > t[:, None]) & (u < hi[:, None])
        mats.append(np.where(odd[:, None], m_odd, m_even))
        qrows.append(odd)
        pmasks.append(odd[:, None] & (blk[None, :] == blk[:, None] - 1))
    pmasks.append(t[None, :] == t[:, None])
    summat = np.concatenate(mats, axis=0).astype(np.float32)
    qrow = np.stack(qrows, axis=0).astype(np.float32)
    pmask = np.stack(pmasks, axis=0).astype(np.float32)
    return summat, qrow, pmask


def _hgrn_kernel(q_ref, f_ref, i_ref, g_ref, lb_ref, nw_ref, sm_ref, qrow_ref, pm_ref,
                 o_ref, state_ref, *, rows):
    C = HGRN_CHUNK
    n_levels = qrow_ref.shape[0]

    @pl.when(pl.program_id(1) == 0)
    def _():
        state_ref[...] = jnp.zeros_like(state_ref)

    lb = lb_ref[...]
    summat = sm_ref[...]
    for c in range(rows // C):
        r0 = c * C
        fg = lb + (1.0 - lb) * jax.nn.sigmoid(f_ref[0, r0:r0 + C, :])
        logf = jnp.log(fg)
        hi = logf.astype(BF16)
        rem = logf - hi.astype(F32)
        mid = rem.astype(BF16)
        lo = (rem - mid.astype(F32)).astype(BF16)
        sums = jnp.dot(summat, jnp.concatenate([hi, mid, lo], axis=0),
                       preferred_element_type=F32)
        for h in range(HGRN_HEADS):
            ls = slice(h * HGRN_DIM, (h + 1) * HGRN_DIM)
            qf = jax.nn.silu(q_ref[0, r0:r0 + C, ls].astype(F32))
            kf = 1.0 - fg[:, ls]
            iv = i_ref[0, r0:r0 + C, ls]
            bcum = sums[0:C, ls]
            bsuf = sums[C:2 * C, ls]
            att = _nt_dot(qf.astype(BF16), kf.astype(BF16)) * pm_ref[n_levels]
            for lv in range(n_levels):
                w = jnp.exp(sums[(2 + lv) * C:(3 + lv) * C, ls])
                qrow = qrow_ref[lv]
                z = (jnp.where(qrow > 0.5, qf, kf) * w).astype(BF16)
                att = att + _nt_dot(z, z) * pm_ref[lv]
            o = jnp.dot(att.astype(BF16), iv, preferred_element_type=F32)
            st = state_ref[h]
            o = o + _nt_dot((qf * jnp.exp(bcum)).astype(BF16), st.astype(BF16))
            kdec = (kf * jnp.exp(bsuf)).astype(BF16)
            state_ref[h] = st * jnp.exp(bcum[C - 1:C, :]) + _tn_dot(iv, kdec)
            o = o * lax.rsqrt(jnp.mean(o * o, axis=-1, keepdims=True) + RMS_EPS)
            o = o * nw_ref[:, ls] * jax.nn.silu(g_ref[0, r0:r0 + C, ls].astype(F32))
            o_ref[0, r0:r0 + C, ls] = o.astype(BF16)


def _hgrn(qb, fb, ib, gb, lb, norm_w):
    B, S, W = qb.shape
    rows = min(HGRN_ROWS, S)
    summat, qrow, pmask = _hgrn_constants(HGRN_CHUNK)
    summat = jnp.asarray(np.concatenate([summat] * 3, axis=1), BF16)
    n_levels = qrow.shape[0]
    blk = pl.BlockSpec((1, rows, W), lambda b, s: (b, s, 0))
    vec = pl.BlockSpec((1, W), lambda b, s: (0, 0))
    const = lambda a: pl.BlockSpec(a.shape, lambda b, s: (0,) * a.ndim)
    qrow3 = qrow.reshape(n_levels, HGRN_CHUNK, 1)
    return pl.pallas_call(
        functools.partial(_hgrn_kernel, rows=rows),
        grid=(B, S // rows),
        in_specs=[blk, blk, blk, blk, vec, vec, const(summat), const(qrow3), const(pmask)],
        out_specs=blk,
        out_shape=jax.ShapeDtypeStruct((B, S, W), BF16),
        scratch_shapes=[pltpu.VMEM((HGRN_HEADS, HGRN_DIM, HGRN_DIM), F32)],
        compiler_params=_cparams("parallel", "arbitrary"),
        name="hgrn",
    )(qb, fb, ib, gb, lb, norm_w, jnp.asarray(summat), jnp.asarray(qrow3), jnp.asarray(pmask))


def _mix_route_kernel(ya_ref, yb_ref, ga_ref, gb_ref, x_ref, wa_ref, wb_ref, wo_ref, nw_ref,
                      rw_ref, rb_ref, tri_ref,
                      x1_ref, h2_ref, e_ref, w_ref, rank_ref, cnt_ref, carry_ref):
    @pl.when(pl.program_id(0) == 0)
    def _():
        carry_ref[...] = jnp.zeros_like(carry_ref)

    ua = jnp.dot(ya_ref[...], wa_ref[...], preferred_element_type=F32)
    ub = jnp.dot(yb_ref[...], wb_ref[...], preferred_element_type=F32)
    mixed = (jax.nn.sigmoid(ga_ref[...].astype(F32)) * ua
             + jax.nn.sigmoid(gb_ref[...].astype(F32)) * ub)
    x1 = x_ref[...] + jnp.dot(mixed.astype(BF16), wo_ref[...], preferred_element_type=F32)
    x1_ref[...] = x1
    h2 = x1 * lax.rsqrt(jnp.mean(x1 * x1, axis=-1, keepdims=True) + RMS_EPS) * nw_ref[...]
    h2_ref[...] = h2

    tm = x1.shape[0]
    logits = _nt_dot(rw_ref[...], h2, precision=lax.Precision.HIGHEST) + rb_ref[...]
    eid = lax.broadcasted_iota(jnp.int32, (N_EXPERTS, tm), 0)
    work = logits
    es, vs = [], []
    for _ in range(TOP_K):
        mx = jnp.max(work, axis=0, keepdims=True)
        idx = jnp.min(jnp.where(work == mx, eid, N_EXPERTS), axis=0, keepdims=True)
        es.append(idx)
        vs.append(mx)
        work = jnp.where(eid == idx, -jnp.inf, work)
    ex = [jnp.exp(v - vs[0]) for v in vs]
    den = ex[0] + ex[1] + ex[2] + ex[3]
    multi = jnp.zeros((N_EXPERTS, tm), F32)
    for k in range(TOP_K):
        multi = multi + jnp.where(eid == es[k], 1.0, 0.0)
    before = jnp.dot(multi.astype(BF16), tri_ref[...], preferred_element_type=F32) + carry_ref[...]
    for k in range(TOP_K):
        e_ref[k:k + 1, :] = es[k]
        w_ref[k:k + 1, :] = ex[k] / den
        rank_ref[k:k + 1, :] = jnp.sum(jnp.where(eid == es[k], before, 0.0), axis=0,
                                       keepdims=True).astype(jnp.int32)
    carry_ref[...] = carry_ref[...] + jnp.sum(multi, axis=1, keepdims=True)
    cnt_ref[...] = jnp.broadcast_to(carry_ref[...], cnt_ref.shape)


def _mix_route(ya, yb, ga, gb, x2, wa, wb, wo, norm_w, rw_t, rb, row0, rows):
    D = x2.shape[1]
    T = rows
    tm = min(MIX_ROWS, T)
    row = lambda w: pl.BlockSpec((tm, w), lambda i: (i + row0 // tm, 0))
    out_row = pl.BlockSpec((tm, D), lambda i: (i, 0))
    whole = pl.BlockSpec(memory_space=pltpu.VMEM)
    kt = pl.BlockSpec((TOP_K, tm), lambda i: (0, i))
    tri = jnp.asarray(np.triu(np.ones((tm, tm), np.float32), 1), BF16)
    return pl.pallas_call(
        _mix_route_kernel,
        grid=(T // tm,),
        in_specs=[row(ATT_WIDTH), row(HGRN_WIDTH), row(D), row(D), row(D),
                  whole, whole, whole, pl.BlockSpec((1, D), lambda i: (0, 0)),
                  whole, whole, whole],
        out_specs=[out_row, out_row, kt, kt, kt,
                   pl.BlockSpec((N_EXPERTS, V7X_LANES), lambda i: (0, 0))],
        out_shape=[jax.ShapeDtypeStruct((T, D), F32), jax.ShapeDtypeStruct((T, D), F32),
                   jax.ShapeDtypeStruct((TOP_K, T), jnp.int32),
                   jax.ShapeDtypeStruct((TOP_K, T), F32),
                   jax.ShapeDtypeStruct((TOP_K, T), jnp.int32),
                   jax.ShapeDtypeStruct((N_EXPERTS, V7X_LANES), F32)],
        scratch_shapes=[pltpu.VMEM((N_EXPERTS, 1), F32)],
        compiler_params=_cparams("arbitrary"),
        name="mix_route",
    )(ya, yb, ga, gb, x2, wa, wb, wo, norm_w, rw_t, rb, tri)


def _sc_worker_id():
    return lax.axis_index("s") * V7X_SC_CORES + lax.axis_index("c")


def _sc_kernel(body, out_rows, like, window, name):
    mesh = plsc.VectorSubcoreMesh(core_axis_name="c", subcore_axis_name="s")
    return pl.kernel(
        body, mesh=mesh,
        out_type=jax.ShapeDtypeStruct((out_rows,) + like.shape[1:], like.dtype),
        scratch_types=[pltpu.VMEM((window,), jnp.int32),
                       pltpu.VMEM((window,) + like.shape[1:], like.dtype)],
        name=name)


def _scatter_rows(dest, h2):
    T = dest.shape[1]
    n_workers = V7X_SC_CORES * V7X_SC_SUBCORES
    per_worker = T // n_workers
    window = min(SC_ROWS, per_worker)

    def body(dest_hbm, src_hbm, dst_hbm, idx_v, rows_v):
        base = _sc_worker_id() * per_worker

        @pl.loop(0, per_worker // window)
        def _(c):
            t0 = base + c * window
            pltpu.sync_copy(src_hbm.at[pl.ds(t0, window)], rows_v)
            for k in range(TOP_K):
                pltpu.sync_copy(dest_hbm.at[pl.ds(k * T + t0, window)], idx_v)
                pltpu.sync_copy(rows_v, dst_hbm.at[idx_v])

    return _sc_kernel(body, TOP_K * T, h2, window, "scatter_rows")(dest.reshape(-1), h2)


def _gather_rows(dest, yg):
    T = dest.shape[1]
    n_workers = V7X_SC_CORES * V7X_SC_SUBCORES
    per_worker = TOP_K * T // n_workers
    window = min(SC_ROWS, per_worker)

    def body(dest_hbm, src_hbm, dst_hbm, idx_v, rows_v):
        base = _sc_worker_id() * per_worker

        @pl.loop(0, per_worker // window)
        def _(c):
            r0 = base + c * window
            pltpu.sync_copy(dest_hbm.at[pl.ds(r0, window)], idx_v)
            pltpu.sync_copy(src_hbm.at[idx_v], rows_v)
            pltpu.sync_copy(rows_v, dst_hbm.at[pl.ds(r0, window)])

    return _sc_kernel(body, TOP_K * T, yg, window, "gather_rows")(dest.reshape(-1), yg)


def _prep_gate_up_kernel(w_ref, perm_ref, wg_ref, wu_ref):
    half = V7X_MXU_DIM // 2
    for g in range(w_ref.shape[2] // V7X_MXU_DIM):
        wb = w_ref[0, :, g * V7X_MXU_DIM:(g + 1) * V7X_MXU_DIM].astype(BF16)
        d = jnp.dot(wb, perm_ref[...], preferred_element_type=F32).astype(BF16)
        wg_ref[0, :, g * half:(g + 1) * half] = d[:, :half]
        wu_ref[0, :, g * half:(g + 1) * half] = d[:, half:]


def _prep_gate_up(w_gate_up):
    E, D, F2 = w_gate_up.shape
    cols = min(PREP_COLS, F2)
    i = np.arange(V7X_MXU_DIM)
    src = np.where(i < V7X_MXU_DIM // 2, 2 * i, 2 * (i - V7X_MXU_DIM // 2) + 1)
    perm = np.zeros((V7X_MXU_DIM, V7X_MXU_DIM), np.float32)
    perm[src, i] = 1.0
    out = pl.BlockSpec((1, D, cols // 2), lambda e, c: (e, 0, c))
    return pl.pallas_call(
        _prep_gate_up_kernel,
        grid=(E, F2 // cols),
        in_specs=[pl.BlockSpec((1, D, cols), lambda e, c: (e, 0, c)),
                  pl.BlockSpec((V7X_MXU_DIM, V7X_MXU_DIM), lambda e, c: (0, 0))],
        out_specs=[out, out],
        out_shape=[jax.ShapeDtypeStruct((E, D, F2 // 2), BF16)] * 2,
        compiler_params=_cparams("parallel", "parallel"),
        name="prep_gate_up",
    )(w_gate_up, jnp.asarray(perm, BF16))


def _experts_kernel(blk_ref, exp_ref, lo_ref, hi_ref, first_ref,
                    x_ref, wg_ref, wu_ref, bg_ref, bu_ref, wd_ref, bd_ref, y_ref):
    w = pl.program_id(0)
    lo = lo_ref[w]
    hi = hi_ref[w]

    @pl.when(lo < hi)
    def _():
        x = x_ref[...].astype(BF16)
        g = jnp.dot(x, wg_ref[0], preferred_element_type=F32) + bg_ref[0]
        u = jnp.dot(x, wu_ref[0], preferred_element_type=F32) + bu_ref[0]
        g = jnp.minimum(g, SWIGLU_LIMIT)
        u = jnp.clip(u, -SWIGLU_LIMIT, SWIGLU_LIMIT)
        act = (u + 1.0) * (g * jax.nn.sigmoid(g * SWIGLU_ALPHA))
        y = jnp.dot(act.astype(BF16), wd_ref[0], preferred_element_type=F32) + bd_ref[0]
        whole = hi - lo == y.shape[0]

        @pl.when(whole)
        def _():
            y_ref[...] = y

        r = lax.broadcasted_iota(jnp.int32, y.shape, 0)
        mine = (r >= lo) & (r < hi)

        @pl.when(jnp.logical_not(whole) & (first_ref[w] == 1))
        def _():
            y_ref[...] = jnp.where(mine, y, 0.0)

        @pl.when(jnp.logical_not(whole) & (first_ref[w] == 0))
        def _():
            y_ref[...] = jnp.where(mine, y, y_ref[...])


def _work_items(counts, n_rows, tm):
    nblk = n_rows // tm
    n_items = nblk + N_EXPERTS - 1
    end = jnp.cumsum(counts)
    start = end - counts
    fb = start // tm
    nitems = jnp.where(counts > 0, (end - 1) // tm - fb + 1, 0)
    item_end = jnp.cumsum(nitems)
    item_start = item_end - nitems
    w = jnp.arange(n_items, dtype=jnp.int32)
    valid = w < item_end[-1]
    wc = jnp.minimum(w, item_end[-1] - 1)
    e = jnp.sum(wc[:, None] >= item_end[None, :], axis=1).astype(jnp.int32)
    e = jnp.minimum(e, N_EXPERTS - 1)
    blk = (fb[e] + (wc - item_start[e])).astype(jnp.int32)
    lo = jnp.maximum(start[e], blk * tm) - blk * tm
    hi = jnp.minimum(end[e], (blk + 1) * tm) - blk * tm
    lo = jnp.where(valid, lo, 0).astype(jnp.int32)
    hi = jnp.where(valid, hi, 0).astype(jnp.int32)
    first = jnp.concatenate([jnp.ones((1,), jnp.int32),
                             (blk[1:] != blk[:-1]).astype(jnp.int32)])
    return blk, e, lo, hi, first


def _experts(xg, counts, wg, wu, bg, bu, wd, bd):
    A, D = xg.shape
    tm = min(EXPERT_ROWS, A)
    F = wg.shape[2]
    items = _work_items(counts, A, tm)
    n_items = A // tm + N_EXPERTS - 1
    xs = pl.BlockSpec((tm, D), lambda w, blk, e, lo, hi, fi: (blk[w], 0))
    wsp = lambda r, c: pl.BlockSpec((1, r, c), lambda w, blk, e, lo, hi, fi: (e[w], 0, 0))
    return pl.pallas_call(
        _experts_kernel,
        grid_spec=pltpu.PrefetchScalarGridSpec(
            num_scalar_prefetch=5,
            grid=(n_items,),
            in_specs=[xs, wsp(D, F), wsp(D, F), wsp(1, F), wsp(1, F), wsp(F, D), wsp(1, D)],
            out_specs=xs,
        ),
        out_shape=jax.ShapeDtypeStruct(xg.shape, F32),
        compiler_params=_cparams("arbitrary"),
        name="experts",
    )(*items, xg, wg, wu, bg, bu, wd, bd)


def _combine_kernel(x1_ref, yk_ref, w_ref, nw_ref, *rest):
    o_ref = rest[-1]
    w = w_ref[...]
    x = x1_ref[...]
    for k in range(TOP_K):
        x = x + yk_ref[k] * w[:, k:k + 1]
    o_ref[...] = x * lax.rsqrt(jnp.mean(x * x, axis=-1, keepdims=True) + RMS_EPS) * nw_ref[...]


def _combine(x1, yk, w_tk, norm_w, out_rows, row0, prev_out):
    Tg, D = x1.shape
    tm = min(COMBINE_ROWS, Tg)
    row = pl.BlockSpec((tm, D), lambda i: (i, 0))
    in_specs = [row, pl.BlockSpec((TOP_K, tm, D), lambda i: (0, i, 0)),
                pl.BlockSpec((tm, TOP_K), lambda i: (i, 0)),
                pl.BlockSpec((1, D), lambda i: (0, 0))]
    args = [x1, yk, w_tk, norm_w]
    aliases = {}
    if prev_out is not None:
        in_specs.append(pl.BlockSpec(memory_space=pl.ANY))
        args.append(prev_out)
        aliases = {len(args) - 1: 0}
    return pl.pallas_call(
        _combine_kernel,
        grid=(Tg // tm,),
        in_specs=in_specs,
        out_specs=pl.BlockSpec((tm, D), lambda i: (i + row0 // tm, 0)),
        out_shape=jax.ShapeDtypeStruct((out_rows, D), F32),
        input_output_aliases=aliases,
        compiler_params=_cparams("parallel"),
        name="combine",
    )(*args)


def _qk_column_order():
    half = ATT_HEAD_DIM // 2
    order = []
    for p in range(ATT_HEADS // 2):
        for part in range(2):
            for h in (2 * p, 2 * p + 1):
                order.extend(range(h * ATT_HEAD_DIM + part * half, h * ATT_HEAD_DIM + (part + 1) * half))
    return np.asarray(order, np.int32)


def _rope_tables(seq):
    half = ATT_HEAD_DIM // 2
    inv = ROPE_THETA ** (-(jnp.arange(half, dtype=F32) * 2.0 / ATT_HEAD_DIM))
    ang = jnp.arange(seq, dtype=F32)[:, None] * inv[None, :]
    cos, sin = jnp.cos(ang), jnp.sin(ang)
    return (jnp.concatenate([cos, cos, cos, cos], axis=1),
            jnp.concatenate([-sin, -sin, sin, sin], axis=1))


def kernel(x, norm1_w, w_in, moba_up, hgrn_lb_logits, hgrn_norm_w, hgrn_up, w_out, norm2_w,
           router_w, router_b, w_gate_up, b_gate_up, w_down, b_down, final_norm_w):
    B, S, D = x.shape
    T = B * S
    assert S % MOBA_BLOCK == 0 and w_in.shape[0] == 1
    x2 = x.reshape(T, D)

    perm = _qk_column_order()
    w0 = w_in[0]
    w_in_p = jnp.concatenate([w0[:, :ATT_WIDTH][:, perm], w0[:, ATT_WIDTH:2 * ATT_WIDTH][:, perm],
                              w0[:, 2 * ATT_WIDTH:]], axis=1).astype(BF16)
    cos_t, sin_t = _rope_tables(S)
    lb = jnp.cumsum(jax.nn.softmax(hgrn_lb_logits.astype(F32), axis=0), axis=0)[0:1]

    qa, ka, va, qb, fb, ib, gb, ga, gtb = _in_proj(x2, norm1_w, w_in_p, cos_t, sin_t, S)
    r3 = lambda a: a.reshape(B, S, a.shape[1])
    ya = _moba(r3(qa), r3(ka), r3(va)).reshape(T, ATT_WIDTH)
    yb = _hgrn(r3(qb), r3(fb), r3(ib), r3(gb), lb, hgrn_norm_w).reshape(T, HGRN_WIDTH)

    mix_w = (moba_up[0].astype(BF16), hgrn_up[0].astype(BF16), w_out[0].astype(BF16), norm2_w,
             router_w[0].T, router_b[0][:, None])
    wg, wu = _prep_gate_up(w_gate_up[0])
    expert_w = (wg, wu, b_gate_up[0][:, None, 0::2], b_gate_up[0][:, None, 1::2],
                w_down[0].astype(BF16), b_down[0][:, None, :])

    Tg = T // MOE_TOKEN_GROUPS
    out = None
    for g in range(MOE_TOKEN_GROUPS):
        x1, h2, top_e, top_w, rank, cnt = _mix_route(ya, yb, ga, gtb, x2, *mix_w, g * Tg, Tg)
        counts = cnt[:, 0].astype(jnp.int32)
        start = jnp.cumsum(counts) - counts
        dest = rank + jnp.sum(jnp.where(top_e[:, :, None] == jnp.arange(N_EXPERTS)[None, None, :],
                                        start[None, None, :], 0), axis=-1)
        xg = _scatter_rows(dest, h2)
        yg = _experts(xg, counts, *expert_w)
        yk = _gather_rows(dest, yg).reshape(TOP_K, Tg, D)
        out = _combine(x1, yk, top_w.T, final_norm_w[None, :], T, g * Tg, out)
    return out.reshape(B, S, D)
```

```python
import functools
import math

import numpy as np
import jax
import jax.numpy as jnp
from jax import lax
from jax.experimental import pallas as pl
from jax.experimental.pallas import tpu as pltpu
from jax.experimental.pallas import tpu_sc as plsc

ATT_HEADS = 8
ATT_HEAD_DIM = 64
ATT_WIDTH = ATT_HEADS * ATT_HEAD_DIM
MOBA_BLOCK = 256
MOBA_TOPK = 3
ROPE_THETA = 10000.0
HGRN_HEADS = 4
HGRN_DIM = 128
HGRN_WIDTH = HGRN_HEADS * HGRN_DIM
HGRN_CHUNK = 64
N_EXPERTS = 32
TOP_K = 4
SWIGLU_LIMIT = 7.0
SWIGLU_ALPHA = 1.702
RMS_EPS = 1e-6
NEG = -1e30

V7X_LANES = 128
V7X_SUBLANES = 8
V7X_MXU_DIM = 256
V7X_VMEM_LIMIT_BYTES = 56 * 1024 * 1024
V7X_SC_CORES = 2
V7X_SC_SUBCORES = 16

MOBA_GROUP = 4
PROJ_ROWS = 512
PROJ_COLS = 512
HGRN_ROWS = 256
MIX_ROWS = 512
EXPERT_ROWS = 256
MOE_TOKEN_GROUPS = 2
SC_ROWS = 32
COMBINE_ROWS = 256
PREP_ROWS = 256

F32 = jnp.float32
BF16 = jnp.bfloat16


def _nt_dot(a, b, precision=None):
    return lax.dot_general(a, b, (((1,), (1,)), ((), ())), precision=precision,
                           preferred_element_type=F32)


def _tn_dot(a, b, precision=None):
    return lax.dot_general(a, b, (((0,), (0,)), ((), ())), precision=precision,
                           preferred_element_type=F32)


def _cparams(*sem):
    return pltpu.CompilerParams(dimension_semantics=sem, vmem_limit_bytes=V7X_VMEM_LIMIT_BYTES)


def _in_proj_kernel(x_ref, nw_ref, w_ref, cos_ref, sin_ref,
                    qa_ref, ka_ref, va_ref, qb_ref, fb_ref, ib_ref, gb_ref, ga_ref, gtb_ref):
    x = x_ref[...]
    h = x * lax.rsqrt(jnp.mean(x * x, axis=-1, keepdims=True) + RMS_EPS) * nw_ref[...]
    h = h.astype(BF16)
    cos = cos_ref[...]
    sin = sin_ref[...]

    def proj(c):
        return jnp.dot(h, w_ref[:, c * PROJ_COLS:(c + 1) * PROJ_COLS], preferred_element_type=F32)

    def rope(t):
        out = []
        for j in range(PROJ_COLS // V7X_LANES):
            tj = t[:, j * V7X_LANES:(j + 1) * V7X_LANES]
            out.append(tj * cos + pltpu.roll(tj, V7X_LANES // 2, 1) * sin)
        return jnp.concatenate(out, axis=1)

    qa_ref[...] = (rope(proj(0)) * (ATT_HEAD_DIM ** -0.5)).astype(BF16)
    ka_ref[...] = rope(proj(1)).astype(BF16)
    va_ref[...] = proj(2).astype(BF16)
    qb_ref[...] = proj(3).astype(BF16)
    fb_ref[...] = proj(4)
    ib_ref[...] = proj(5).astype(BF16)
    gb_ref[...] = proj(6).astype(BF16)
    ga_ref[:, :PROJ_COLS] = proj(7).astype(BF16)
    ga_ref[:, PROJ_COLS:] = proj(8).astype(BF16)
    gtb_ref[:, :PROJ_COLS] = proj(9).astype(BF16)
    gtb_ref[:, PROJ_COLS:] = proj(10).astype(BF16)


def _in_proj(x2, norm_w, w_in_bf16, cos_t, sin_t, seq):
    T, D = x2.shape
    tm = min(PROJ_ROWS, seq)
    n_seq_tiles = seq // tm
    row = lambda w: pl.BlockSpec((tm, w), lambda i: (i, 0))
    tab = pl.BlockSpec((tm, V7X_LANES), lambda i: (i % n_seq_tiles, 0))
    widths = [ATT_WIDTH] * 3 + [HGRN_WIDTH] * 4 + [D, D]
    dtypes = [BF16, BF16, BF16, BF16, F32, BF16, BF16, BF16, BF16]
    return pl.pallas_call(
        _in_proj_kernel,
        grid=(T // tm,),
        in_specs=[row(D), pl.BlockSpec((1, D), lambda i: (0, 0)),
                  pl.BlockSpec(memory_space=pltpu.VMEM), tab, tab],
        out_specs=[row(w) for w in widths],
        out_shape=[jax.ShapeDtypeStruct((T, w), dt) for w, dt in zip(widths, dtypes)],
        compiler_params=_cparams("parallel"),
        name="in_proj",
    )(x2, norm_w, w_in_bf16, cos_t, sin_t)


def _moba_kernel(q_ref, k_ref, v_ref, o_ref, kaug_ref, kmean_ref, vt_ref, acc_ref, s_ref, *, nb):
    qi = pl.program_id(2)
    blk = MOBA_BLOCK
    lanes = V7X_LANES

    @pl.when(qi == 0)
    def _():
        kaug_ref[:, :lanes] = k_ref[0]
        rowb = lax.broadcasted_iota(jnp.int32, (nb * blk, lanes), 0) // blk
        col = lax.broadcasted_iota(jnp.int32, (nb * blk, lanes), 1)
        kaug_ref[:, lanes:] = jnp.where(rowb == col, 1.0, 0.0).astype(BF16)
        means = [jnp.sum(k_ref[0, n * blk:(n + 1) * blk, :].astype(F32), axis=0, keepdims=True)
                 * (1.0 / blk) for n in range(nb)]
        nbp = kmean_ref.shape[0]
        km = jnp.concatenate(means + [jnp.zeros((nbp - nb, lanes), F32)] * (nbp > nb), axis=0)
        hi = km.astype(BF16)
        rem = km - hi.astype(F32)
        mid = rem.astype(BF16)
        lo = (rem - mid.astype(F32)).astype(BF16)
        kmean_ref[...] = jnp.concatenate([hi, mid, lo], axis=1)

        for c in range(nb):
            vt = v_ref[0, c * blk:(c + 1) * blk, :].astype(F32).T
            vt_ref[:lanes, c * blk:(c + 1) * blk] = vt.astype(BF16)
        vt_ref[lanes:, :] = jnp.ones((vt_ref.shape[0] - lanes, nb * blk), BF16)

    qt = q_ref[0].astype(F32).T
    feat = lax.broadcasted_iota(jnp.int32, (lanes, blk), 0)
    key_i = lax.broadcasted_iota(jnp.int32, (blk, blk), 0)
    qry_i = lax.broadcasted_iota(jnp.int32, (blk, blk), 1)
    own = pl.multiple_of(qi * blk, blk)
    k_own = kaug_ref[pl.ds(own, blk), :lanes]
    nbp = kmean_ref.shape[0]
    blk_id = lax.broadcasted_iota(jnp.int32, (nbp, 2 * blk), 0)
    slab = 2 * V7X_SUBLANES

    def slab_max(s):
        return jnp.max(s.reshape(s.shape[0] // slab, slab, blk), axis=0)

    qhs = [jnp.where((feat // (ATT_HEAD_DIM // 2)) % 2 == hh, qt, 0.0).astype(BF16)
           for hh in range(2)]
    q2 = jnp.concatenate(qhs, axis=1)
    gate = jnp.dot(kmean_ref[...], jnp.concatenate([q2, q2, q2], axis=0),
                   preferred_element_type=F32)
    gate = jnp.where(blk_id < qi, gate, NEG)
    beaten = jnp.zeros((nbp, 2 * blk), F32)
    for n in range(nb):
        gn = gate[n:n + 1, :]
        wins = (gn > gate) | ((gn == gate) & (blk_id > n))
        beaten = beaten + jnp.where(wins, 1.0, 0.0)
    sel = (beaten < MOBA_TOPK) & (blk_id < qi)
    bias = jnp.where(sel, 0.0, NEG).astype(BF16)
    bias = jnp.concatenate([bias, jnp.zeros((lanes - nbp, 2 * blk), BF16)], axis=0)

    q_aug, m_init = [], []
    for hh in range(2):
        q_aug.append(jnp.concatenate([qhs[hh], bias[:, hh * blk:(hh + 1) * blk]], axis=0))
        m_init.append(jnp.full((slab, blk), NEG, F32))
        acc_ref[hh] = jnp.zeros(acc_ref.shape[1:], F32)

    group = MOBA_GROUP * blk
    n_groups = (qi + MOBA_GROUP) // MOBA_GROUP

    def score_group(g, ms):
        off = pl.multiple_of(g * group, group)
        kb = kaug_ref[pl.ds(off, group), :]
        out = []
        for hh in range(2):
            s = jnp.dot(kb, q_aug[hh], preferred_element_type=F32)
            s_ref[hh, pl.ds(off, group), :] = s
            out.append(jnp.maximum(ms[hh], slab_max(s)))
        return tuple(out)

    ms = lax.fori_loop(0, n_groups, score_group, tuple(m_init))

    m_fin = []
    for hh in range(2):
        s = jnp.dot(k_own, q_aug[hh][:lanes], preferred_element_type=F32)
        s = jnp.where(key_i <= qry_i, s, NEG)
        s_ref[hh, pl.ds(own, blk), :] = s
        m = jnp.maximum(ms[hh], slab_max(s))
        m_fin.append(jnp.max(m, axis=0, keepdims=True))

    def value_group(g, c):
        off = pl.multiple_of(g * group, group)
        vb = vt_ref[:, pl.ds(off, group)]
        for hh in range(2):
            p = jnp.exp(s_ref[hh, pl.ds(off, group), :] - m_fin[hh]).astype(BF16)
            acc_ref[hh] += jnp.dot(vb, p, preferred_element_type=F32)
        return c

    lax.fori_loop(0, n_groups, value_group, 0)

    half = lanes // 2
    o0 = acc_ref[0, :half, :] / acc_ref[0, lanes:lanes + 1, :]
    o1 = acc_ref[1, half:lanes, :] / acc_ref[1, lanes:lanes + 1, :]
    o_ref[0] = jnp.concatenate([o0, o1], axis=0).T.astype(BF16)


def _moba(q, k, v):
    B, S, _ = q.shape
    nb = S // MOBA_BLOCK
    assert nb % MOBA_GROUP == 0
    nbp = -(-nb // (2 * V7X_SUBLANES)) * (2 * V7X_SUBLANES)
    n_pairs = ATT_WIDTH // V7X_LANES
    vt_rows = V7X_LANES + 2 * V7X_SUBLANES
    qspec = pl.BlockSpec((1, MOBA_BLOCK, V7X_LANES), lambda b, p, i: (b, i, p))
    kvspec = pl.BlockSpec((1, S, V7X_LANES), lambda b, p, i: (b, 0, p))
    return pl.pallas_call(
        functools.partial(_moba_kernel, nb=nb),
        grid=(B, n_pairs, nb),
        in_specs=[qspec, kvspec, kvspec],
        out_specs=qspec,
        out_shape=jax.ShapeDtypeStruct((B, S, ATT_WIDTH), BF16),
        scratch_shapes=[pltpu.VMEM((S, 2 * V7X_LANES), BF16), pltpu.VMEM((nbp, 3 * V7X_LANES), BF16),
                        pltpu.VMEM((vt_rows, S), BF16), pltpu.VMEM((2, vt_rows, MOBA_BLOCK), F32),
                        pltpu.VMEM((2, S, MOBA_BLOCK), F32)],
        compiler_params=_cparams("parallel", "parallel", "arbitrary"),
        name="moba",
    )(q, k, v)


def _hgrn_level_sizes(chunk):
    return [chunk >> (i + 1) for i in range(int(math.log2(chunk)))]


def _hgrn_constants(chunk):
    t = np.arange(chunk)
    mats = [(t[None, :] <= t[:, None]),
            (t[None, :] > t[:, None])]
    qrows, pmasks = [], []
    for bs in _hgrn_level_sizes(chunk):
        blk = t // bs
        odd = (blk % 2) == 1
        lo, hi = blk * bs, (blk + 1) * bs
        u = t[None, :]
        m_odd = (u >= lo[:, None]) & (u <= t[:, None])
        m_even = (u > t[:, None]) & (u < hi[:, None])
        mats.append(np.where(odd[:, None], m_odd, m_even))
        qrows.append(odd)
        pmasks.append(odd[:, None] & (blk[None, :] == blk[:, None] - 1))
    pmasks.append(t[None, :] == t[:, None])
    summat = np.concatenate(mats, axis=0).astype(np.float32)
    qrow = np.stack(qrows, axis=0).astype(np.float32)
    pmask = np.stack(pmasks, axis=0).astype(np.float32)
    return summat, qrow, pmask


def _hgrn_kernel(q_ref, f_ref, i_ref, g_ref, lb_ref, nw_ref, sm_ref, qrow_ref, pm_ref,
                 o_ref, state_ref, *, rows):
    C = HGRN_CHUNK
    n_levels = qrow_ref.shape[0]

    @pl.when(pl.program_id(1) == 0)
    def _():
        state_ref[...] = jnp.zeros_like(state_ref)

    lb = lb_ref[...]
    summat = sm_ref[...]
    for c in range(rows // C):
        r0 = c * C
        fg = lb + (1.0 - lb) * jax.nn.sigmoid(f_ref[0, r0:r0 + C, :])
        logf = jnp.log(fg)
        hi = logf.astype(BF16)
        rem = logf - hi.astype(F32)
        mid = rem.astype(BF16)
        lo = (rem - mid.astype(F32)).astype(BF16)
        sums = jnp.dot(summat, jnp.concatenate([hi, mid, lo], axis=0),
                       preferred_element_type=F32)
        for h in range(HGRN_HEADS):
            ls = slice(h * HGRN_DIM, (h + 1) * HGRN_DIM)
            qf = jax.nn.silu(q_ref[0, r0:r0 + C, ls].astype(F32))
            kf = 1.0 - fg[:, ls]
            iv = i_ref[0, r0:r0 + C, ls]
            bcum = sums[0:C, ls]
            bsuf = sums[C:2 * C, ls]
            att = _nt_dot(qf.astype(BF16), kf.astype(BF16)) * pm_ref[n_levels]
            for lv in range(n_levels):
                w = jnp.exp(sums[(2 + lv) * C:(3 + lv) * C, ls])
                qrow = qrow_ref[lv]
                z = (jnp.where(qrow > 0.5, qf, kf) * w).astype(BF16)
                att = att + _nt_dot(z, z) * pm_ref[lv]
            o = jnp.dot(att.astype(BF16), iv, preferred_element_type=F32)
            st = state_ref[h]
            o = o + _nt_dot((qf * jnp.exp(bcum)).astype(BF16), st.astype(BF16))
            kdec = (kf * jnp.exp(bsuf)).astype(BF16)
            state_ref[h] = st * jnp.exp(bcum[C - 1:C, :]) + _tn_dot(iv, kdec)
            o = o * lax.rsqrt(jnp.mean(o * o, axis=-1, keepdims=True) + RMS_EPS)
            o = o * nw_ref[:, ls] * jax.nn.silu(g_ref[0, r0:r0 + C, ls].astype(F32))
            o_ref[0, r0:r0 + C, ls] = o.astype(BF16)


def _hgrn(qb, fb, ib, gb, lb, norm_w):
    B, S, W = qb.shape
    rows = min(HGRN_ROWS, S)
    summat, qrow, pmask = _hgrn_constants(HGRN_CHUNK)
    summat = jnp.asarray(np.concatenate([summat] * 3, axis=1), BF16)
    n_levels = qrow.shape[0]
    blk = pl.BlockSpec((1, rows, W), lambda b, s: (b, s, 0))
    vec = pl.BlockSpec((1, W), lambda b, s: (0, 0))
    const = lambda a: pl.BlockSpec(a.shape, lambda b, s: (0,) * a.ndim)
    qrow3 = qrow.reshape(n_levels, HGRN_CHUNK, 1)
    return pl.pallas_call(
        functools.partial(_hgrn_kernel, rows=rows),
        grid=(B, S // rows),
        in_specs=[blk, blk, blk, blk, vec, vec, const(summat), const(qrow3), const(pmask)],
        out_specs=blk,
        out_shape=jax.ShapeDtypeStruct((B, S, W), BF16),
        scratch_shapes=[pltpu.VMEM((HGRN_HEADS, HGRN_DIM, HGRN_DIM), F32)],
        compiler_params=_cparams("parallel", "arbitrary"),
        name="hgrn",
    )(qb, fb, ib, gb, lb, norm_w, jnp.asarray(summat), jnp.asarray(qrow3), jnp.asarray(pmask))


def _mix_route_kernel(ya_ref, yb_ref, ga_ref, gb_ref, x_ref, wa_ref, wb_ref, wo_ref, nw_ref,
                      rw_ref, rb_ref, tri_ref,
                      x1_ref, h2_ref, e_ref, w_ref, rank_ref, cnt_ref, carry_ref):
    @pl.when(pl.program_id(0) == 0)
    def _():
        carry_ref[...] = jnp.zeros_like(carry_ref)

    ua = jnp.dot(ya_ref[...], wa_ref[...], preferred_element_type=F32)
    ub = jnp.dot(yb_ref[...], wb_ref[...], preferred_element_type=F32)
    mixed = (jax.nn.sigmoid(ga_ref[...].astype(F32)) * ua
             + jax.nn.sigmoid(gb_ref[...].astype(F32)) * ub)
    x1 = x_ref[...] + jnp.dot(mixed.astype(BF16), wo_ref[...], preferred_element_type=F32)
    x1_ref[...] = x1
    h2 = x1 * lax.rsqrt(jnp.mean(x1 * x1, axis=-1, keepdims=True) + RMS_EPS) * nw_ref[...]
    h2_ref[...] = h2

    tm = x1.shape[0]
    logits = _nt_dot(rw_ref[...], h2, precision=lax.Precision.HIGHEST) + rb_ref[...]
    eid = lax.broadcasted_iota(jnp.int32, (N_EXPERTS, tm), 0)
    work = logits
    es, vs = [], []
    for _ in range(TOP_K):
        mx = jnp.max(work, axis=0, keepdims=True)
        idx = jnp.min(jnp.where(work == mx, eid, N_EXPERTS), axis=0, keepdims=True)
        es.append(idx)
        vs.append(mx)
        work = jnp.where(eid == idx, -jnp.inf, work)
    ex = [jnp.exp(v - vs[0]) for v in vs]
    den = ex[0] + ex[1] + ex[2] + ex[3]
    multi = jnp.zeros((N_EXPERTS, tm), F32)
    for k in range(TOP_K):
        multi = multi + jnp.where(eid == es[k], 1.0, 0.0)
    before = jnp.dot(multi.astype(BF16), tri_ref[...], preferred_element_type=F32) + carry_ref[...]
    for k in range(TOP_K):
        e_ref[k:k + 1, :] = es[k]
        w_ref[k:k + 1, :] = ex[k] / den
        rank_ref[k:k + 1, :] = jnp.sum(jnp.where(eid == es[k], before, 0.0), axis=0,
                                       keepdims=True).astype(jnp.int32)
    carry_ref[...] = carry_ref[...] + jnp.sum(multi, axis=1, keepdims=True)
    cnt_ref[...] = jnp.broadcast_to(carry_ref[...], cnt_ref.shape)


def _mix_route(ya, yb, ga, gb, x2, wa, wb, wo, norm_w, rw_t, rb, row0, rows):
    D = x2.shape[1]
    T = rows
    tm = min(MIX_ROWS, T)
    row = lambda w: pl.BlockSpec((tm, w), lambda i: (i + row0 // tm, 0))
    out_row = pl.BlockSpec((tm, D), lambda i: (i, 0))
    whole = pl.BlockSpec(memory_space=pltpu.VMEM)
    kt = pl.BlockSpec((TOP_K, tm), lambda i: (0, i))
    tri = jnp.asarray(np.triu(np.ones((tm, tm), np.float32), 1), BF16)
    return pl.pallas_call(
        _mix_route_kernel,
        grid=(T // tm,),
        in_specs=[row(ATT_WIDTH), row(HGRN_WIDTH), row(D), row(D), row(D),
                  whole, whole, whole, pl.BlockSpec((1, D), lambda i: (0, 0)),
                  whole, whole, whole],
        out_specs=[out_row, out_row, kt, kt, kt,
                   pl.BlockSpec((N_EXPERTS, V7X_LANES), lambda i: (0, 0))],
        out_shape=[jax.ShapeDtypeStruct((T, D), F32), jax.ShapeDtypeStruct((T, D), F32),
                   jax.ShapeDtypeStruct((TOP_K, T), jnp.int32),
                   jax.ShapeDtypeStruct((TOP_K, T), F32),
                   jax.ShapeDtypeStruct((TOP_K, T), jnp.int32),
                   jax.ShapeDtypeStruct((N_EXPERTS, V7X_LANES), F32)],
        scratch_shapes=[pltpu.VMEM((N_EXPERTS, 1), F32)],
        compiler_params=_cparams("arbitrary"),
        name="mix_route",
    )(ya, yb, ga, gb, x2, wa, wb, wo, norm_w, rw_t, rb, tri)


def _sc_worker_id():
    return lax.axis_index("s") * V7X_SC_CORES + lax.axis_index("c")


def _sc_kernel(body, out_rows, like, window, name):
    mesh = plsc.VectorSubcoreMesh(core_axis_name="c", subcore_axis_name="s")
    return pl.kernel(
        body, mesh=mesh,
        out_type=jax.ShapeDtypeStruct((out_rows,) + like.shape[1:], like.dtype),
        scratch_types=[pltpu.VMEM((window,), jnp.int32),
                       pltpu.VMEM((window,) + like.shape[1:], like.dtype)],
        name=name)


def _scatter_rows(dest, h2):
    T = dest.shape[1]
    n_workers = V7X_SC_CORES * V7X_SC_SUBCORES
    per_worker = T // n_workers
    window = min(SC_ROWS, per_worker)

    def body(dest_hbm, src_hbm, dst_hbm, idx_v, rows_v):
        base = _sc_worker_id() * per_worker

        @pl.loop(0, per_worker // window)
        def _(c):
            t0 = base + c * window
            pltpu.sync_copy(src_hbm.at[pl.ds(t0, window)], rows_v)
            for k in range(TOP_K):
                pltpu.sync_copy(dest_hbm.at[pl.ds(k * T + t0, window)], idx_v)
                pltpu.sync_copy(rows_v, dst_hbm.at[idx_v])

    return _sc_kernel(body, TOP_K * T, h2, window, "scatter_rows")(dest.reshape(-1), h2)


def _gather_rows(dest, yg):
    T = dest.shape[1]
    n_workers = V7X_SC_CORES * V7X_SC_SUBCORES
    per_worker = TOP_K * T // n_workers
    window = min(SC_ROWS, per_worker)

    def body(dest_hbm, src_hbm, dst_hbm, idx_v, rows_v):
        base = _sc_worker_id() * per_worker

        @pl.loop(0, per_worker // window)
        def _(c):
            r0 = base + c * window
            pltpu.sync_copy(dest_hbm.at[pl.ds(r0, window)], idx_v)
            pltpu.sync_copy(src_hbm.at[idx_v], rows_v)
            pltpu.sync_copy(rows_v, dst_hbm.at[pl.ds(r0, window)])

    return _sc_kernel(body, TOP_K * T, yg, window, "gather_rows")(dest.reshape(-1), yg)


def _prep_gate_up_kernel(w_ref, perm_ref, wg_ref, wu_ref):
    half = V7X_MXU_DIM // 2
    for g in range(w_ref.shape[2] // V7X_MXU_DIM):
        wb = w_ref[0, :, g * V7X_MXU_DIM:(g + 1) * V7X_MXU_DIM].astype(BF16)
        d = jnp.dot(wb, perm_ref[...], preferred_element_type=F32).astype(BF16)
        wg_ref[0, :, g * half:(g + 1) * half] = d[:, :half]
        wu_ref[0, :, g * half:(g + 1) * half] = d[:, half:]


def _prep_gate_up(w_gate_up):
    E, D, F2 = w_gate_up.shape
    rows = min(PREP_ROWS, D)
    i = np.arange(V7X_MXU_DIM)
    src = np.where(i < V7X_MXU_DIM // 2, 2 * i, 2 * (i - V7X_MXU_DIM // 2) + 1)
    perm = np.zeros((V7X_MXU_DIM, V7X_MXU_DIM), np.float32)
    perm[src, i] = 1.0
    out = pl.BlockSpec((1, rows, F2 // 2), lambda e, r: (e, r, 0))
    return pl.pallas_call(
        _prep_gate_up_kernel,
        grid=(E, D // rows),
        in_specs=[pl.BlockSpec((1, rows, F2), lambda e, r: (e, r, 0)),
                  pl.BlockSpec((V7X_MXU_DIM, V7X_MXU_DIM), lambda e, r: (0, 0))],
        out_specs=[out, out],
        out_shape=[jax.ShapeDtypeStruct((E, D, F2 // 2), BF16)] * 2,
        compiler_params=_cparams("parallel", "parallel"),
        name="prep_gate_up",
    )(w_gate_up, jnp.asarray(perm, BF16))


def _experts_kernel(blk_ref, exp_ref, lo_ref, hi_ref, first_ref,
                    x_ref, wg_ref, wu_ref, bg_ref, bu_ref, wd_ref, bd_ref, y_ref):
    w = pl.program_id(0)
    lo = lo_ref[w]
    hi = hi_ref[w]

    @pl.when(lo < hi)
    def _():
        x = x_ref[...].astype(BF16)
        g = jnp.dot(x, wg_ref[0], preferred_element_type=F32) + bg_ref[0]
        u = jnp.dot(x, wu_ref[0], preferred_element_type=F32) + bu_ref[0]
        g = jnp.minimum(g, SWIGLU_LIMIT)
        u = jnp.clip(u, -SWIGLU_LIMIT, SWIGLU_LIMIT)
        act = (u + 1.0) * (g * jax.nn.sigmoid(g * SWIGLU_ALPHA))
        y = jnp.dot(act.astype(BF16), wd_ref[0], preferred_element_type=F32) + bd_ref[0]
        whole = hi - lo == y.shape[0]

        @pl.when(whole)
        def _():
            y_ref[...] = y

        r = lax.broadcasted_iota(jnp.int32, y.shape, 0)
        mine = (r >= lo) & (r < hi)

        @pl.when(jnp.logical_not(whole) & (first_ref[w] == 1))
        def _():
            y_ref[...] = jnp.where(mine, y, 0.0)

        @pl.when(jnp.logical_not(whole) & (first_ref[w] == 0))
        def _():
            y_ref[...] = jnp.where(mine, y, y_ref[...])


def _work_items(counts, n_rows, tm):
    nblk = n_rows // tm
    n_items = nblk + N_EXPERTS - 1
    end = jnp.cumsum(counts)
    start = end - counts
    fb = start // tm
    nitems = jnp.where(counts > 0, (end - 1) // tm - fb + 1, 0)
    item_end = jnp.cumsum(nitems)
    item_start = item_end - nitems
    w = jnp.arange(n_items, dtype=jnp.int32)
    valid = w < item_end[-1]
    wc = jnp.minimum(w, item_end[-1] - 1)
    e = jnp.sum(wc[:, None] >= item_end[None, :], axis=1).astype(jnp.int32)
    e = jnp.minimum(e, N_EXPERTS - 1)
    blk = (fb[e] + (wc - item_start[e])).astype(jnp.int32)
    lo = jnp.maximum(start[e], blk * tm) - blk * tm
    hi = jnp.minimum(end[e], (blk + 1) * tm) - blk * tm
    lo = jnp.where(valid, lo, 0).astype(jnp.int32)
    hi = jnp.where(valid, hi, 0).astype(jnp.int32)
    first = jnp.concatenate([jnp.ones((1,), jnp.int32),
                             (blk[1:] != blk[:-1]).astype(jnp.int32)])
    return blk, e, lo, hi, first


def _experts(xg, counts, wg, wu, bg, bu, wd, bd):
    A, D = xg.shape
    tm = min(EXPERT_ROWS, A)
    F = wg.shape[2]
    items = _work_items(counts, A, tm)
    n_items = A // tm + N_EXPERTS - 1
    xs = pl.BlockSpec((tm, D), lambda w, blk, e, lo, hi, fi: (blk[w], 0))
    wsp = lambda r, c: pl.BlockSpec((1, r, c), lambda w, blk, e, lo, hi, fi: (e[w], 0, 0))
    return pl.pallas_call(
        _experts_kernel,
        grid_spec=pltpu.PrefetchScalarGridSpec(
            num_scalar_prefetch=5,
            grid=(n_items,),
            in_specs=[xs, wsp(D, F), wsp(D, F), wsp(1, F), wsp(1, F), wsp(F, D), wsp(1, D)],
            out_specs=xs,
        ),
        out_shape=jax.ShapeDtypeStruct(xg.shape, F32),
        compiler_params=_cparams("arbitrary"),
        name="experts",
    )(*items, xg, wg, wu, bg, bu, wd, bd)


def _combine_kernel(x1_ref, yk_ref, w_ref, nw_ref, *rest):
    o_ref = rest[-1]
    w = w_ref[...]
    x = x1_ref[...]
    for k in range(TOP_K):
        x = x + yk_ref[k] * w[:, k:k + 1]
    o_ref[...] = x * lax.rsqrt(jnp.mean(x * x, axis=-1, keepdims=True) + RMS_EPS) * nw_ref[...]


def _combine(x1, yk, w_tk, norm_w, out_rows, row0, prev_out):
    Tg, D = x1.shape
    tm = min(COMBINE_ROWS, Tg)
    row = pl.BlockSpec((tm, D), lambda i: (i, 0))
    in_specs = [row, pl.BlockSpec((TOP_K, tm, D), lambda i: (0, i, 0)),
                pl.BlockSpec((tm, TOP_K), lambda i: (i, 0)),
                pl.BlockSpec((1, D), lambda i: (0, 0))]
    args = [x1, yk, w_tk, norm_w]
    aliases = {}
    if prev_out is not None:
        in_specs.append(pl.BlockSpec(memory_space=pl.ANY))
        args.append(prev_out)
        aliases = {len(args) - 1: 0}
    return pl.pallas_call(
        _combine_kernel,
        grid=(Tg // tm,),
        in_specs=in_specs,
        out_specs=pl.BlockSpec((tm, D), lambda i: (i + row0 // tm, 0)),
        out_shape=jax.ShapeDtypeStruct((out_rows, D), F32),
        input_output_aliases=aliases,
        compiler_params=_cparams("parallel"),
        name="combine",
    )(*args)


def _qk_column_order():
    half = ATT_HEAD_DIM // 2
    order = []
    for p in range(ATT_HEADS // 2):
        for part in range(2):
            for h in (2 * p, 2 * p + 1):
                order.extend(range(h * ATT_HEAD_DIM + part * half, h * ATT_HEAD_DIM + (part + 1) * half))
    return np.asarray(order, np.int32)


def _rope_tables(seq):
    half = ATT_HEAD_DIM // 2
    inv = ROPE_THETA ** (-(jnp.arange(half, dtype=F32) * 2.0 / ATT_HEAD_DIM))
    ang = jnp.arange(seq, dtype=F32)[:, None] * inv[None, :]
    cos, sin = jnp.cos(ang), jnp.sin(ang)
    return (jnp.concatenate([cos, cos, cos, cos], axis=1),
            jnp.concatenate([-sin, -sin, sin, sin], axis=1))


def kernel(x, norm1_w, w_in, moba_up, hgrn_lb_logits, hgrn_norm_w, hgrn_up, w_out, norm2_w,
           router_w, router_b, w_gate_up, b_gate_up, w_down, b_down, final_norm_w):
    B, S, D = x.shape
    T = B * S
    assert S % MOBA_BLOCK == 0 and w_in.shape[0] == 1
    x2 = x.reshape(T, D)

    perm = _qk_column_order()
    w0 = w_in[0]
    w_in_p = jnp.concatenate([w0[:, :ATT_WIDTH][:, perm], w0[:, ATT_WIDTH:2 * ATT_WIDTH][:, perm],
                              w0[:, 2 * ATT_WIDTH:]], axis=1).astype(BF16)
    cos_t, sin_t = _rope_tables(S)
    lb = jnp.cumsum(jax.nn.softmax(hgrn_lb_logits.astype(F32), axis=0), axis=0)[0:1]

    qa, ka, va, qb, fb, ib, gb, ga, gtb = _in_proj(x2, norm1_w, w_in_p, cos_t, sin_t, S)
    r3 = lambda a: a.reshape(B, S, a.shape[1])
    ya = _moba(r3(qa), r3(ka), r3(va)).reshape(T, ATT_WIDTH)
    yb = _hgrn(r3(qb), r3(fb), r3(ib), r3(gb), lb, hgrn_norm_w).reshape(T, HGRN_WIDTH)

    mix_w = (moba_up[0].astype(BF16), hgrn_up[0].astype(BF16), w_out[0].astype(BF16), norm2_w,
             router_w[0].T, router_b[0][:, None])
    wg, wu = _prep_gate_up(w_gate_up[0])
    expert_w = (wg, wu, b_gate_up[0][:, None, 0::2], b_gate_up[0][:, None, 1::2],
                w_down[0].astype(BF16), b_down[0][:, None, :])

    Tg = T // MOE_TOKEN_GROUPS
    out = None
    for g in range(MOE_TOKEN_GROUPS):
        x1, h2, top_e, top_w, rank, cnt = _mix_route(ya, yb, ga, gtb, x2, *mix_w, g * Tg, Tg)
        counts = cnt[:, 0].astype(jnp.int32)
        start = jnp.cumsum(counts) - counts
        dest = rank + jnp.take(start, top_e, axis=0)
        xg = _scatter_rows(dest, h2)
        yg = _experts(xg, counts, *expert_w)
        yk = _gather_rows(dest, yg).reshape(TOP_K, Tg, D)
        out = _combine(x1, yk, top_w.T, final_norm_w[None, :], T, g * Tg, out)
    return out.reshape(B, S, D)
```

```python
import functools
import math

import numpy as np
import jax
import jax.numpy as jnp
from jax import lax
from jax.experimental import pallas as pl
from jax.experimental.pallas import tpu as pltpu
from jax.experimental.pallas import tpu_sc as plsc

ATT_HEADS = 8
ATT_HEAD_DIM = 64
ATT_WIDTH = ATT_HEADS * ATT_HEAD_DIM
MOBA_BLOCK = 256
MOBA_TOPK = 3
ROPE_THETA = 10000.0
HGRN_HEADS = 4
HGRN_DIM = 128
HGRN_WIDTH = HGRN_HEADS * HGRN_DIM
HGRN_CHUNK = 64
N_EXPERTS = 32
TOP_K = 4
SWIGLU_LIMIT = 7.0
SWIGLU_ALPHA = 1.702
RMS_EPS = 1e-6
NEG = -1e30

V7X_LANES = 128
V7X_SUBLANES = 8
V7X_MXU_DIM = 256
V7X_VMEM_LIMIT_BYTES = 56 * 1024 * 1024
V7X_SC_CORES = 2
V7X_SC_SUBCORES = 16

MOBA_GROUP = 4
PROJ_ROWS = 512
PROJ_COLS = 512
HGRN_ROWS = 256
MIX_ROWS = 512
EXPERT_ROWS = 256
MOE_TOKEN_GROUPS = 2
SC_ROWS = 32
COMBINE_ROWS = 256
PREP_ROWS = 256

F32 = jnp.float32
BF16 = jnp.bfloat16


def _nt_dot(a, b, precision=None):
    return lax.dot_general(a, b, (((1,), (1,)), ((), ())), precision=precision,
                           preferred_element_type=F32)


def _tn_dot(a, b, precision=None):
    return lax.dot_general(a, b, (((0,), (0,)), ((), ())), precision=precision,
                           preferred_element_type=F32)


def _cparams(*sem):
    return pltpu.CompilerParams(dimension_semantics=sem, vmem_limit_bytes=V7X_VMEM_LIMIT_BYTES)


def _in_proj_kernel(x_ref, nw_ref, w_ref, cos_ref, sin_ref,
                    qa_ref, ka_ref, va_ref, qb_ref, fb_ref, ib_ref, gb_ref, ga_ref, gtb_ref):
    x = x_ref[...]
    h = x * lax.rsqrt(jnp.mean(x * x, axis=-1, keepdims=True) + RMS_EPS) * nw_ref[...]
    h = h.astype(BF16)
    cos = cos_ref[...]
    sin = sin_ref[...]

    def proj(c):
        return jnp.dot(h, w_ref[:, c * PROJ_COLS:(c + 1) * PROJ_COLS], preferred_element_type=F32)

    def rope(t):
        out = []
        for j in range(PROJ_COLS // V7X_LANES):
            tj = t[:, j * V7X_LANES:(j + 1) * V7X_LANES]
            out.append(tj * cos + pltpu.roll(tj, V7X_LANES // 2, 1) * sin)
        return jnp.concatenate(out, axis=1)

    qa_ref[...] = (rope(proj(0)) * (ATT_HEAD_DIM ** -0.5)).astype(BF16)
    ka_ref[...] = rope(proj(1)).astype(BF16)
    va_ref[...] = proj(2).astype(BF16)
    qb_ref[...] = proj(3).astype(BF16)
    fb_ref[...] = proj(4)
    ib_ref[...] = proj(5).astype(BF16)
    gb_ref[...] = proj(6).astype(BF16)
    ga_ref[:, :PROJ_COLS] = proj(7).astype(BF16)
    ga_ref[:, PROJ_COLS:] = proj(8).astype(BF16)
    gtb_ref[:, :PROJ_COLS] = proj(9).astype(BF16)
    gtb_ref[:, PROJ_COLS:] = proj(10).astype(BF16)


def _in_proj(x2, norm_w, w_in_bf16, cos_t, sin_t, seq):
    T, D = x2.shape
    tm = min(PROJ_ROWS, seq)
    n_seq_tiles = seq // tm
    row = lambda w: pl.BlockSpec((tm, w), lambda i: (i, 0))
    tab = pl.BlockSpec((tm, V7X_LANES), lambda i: (i % n_seq_tiles, 0))
    widths = [ATT_WIDTH] * 3 + [HGRN_WIDTH] * 4 + [D, D]
    dtypes = [BF16, BF16, BF16, BF16, F32, BF16, BF16, BF16, BF16]
    return pl.pallas_call(
        _in_proj_kernel,
        grid=(T // tm,),
        in_specs=[row(D), pl.BlockSpec((1, D), lambda i: (0, 0)),
                  pl.BlockSpec(memory_space=pltpu.VMEM), tab, tab],
        out_specs=[row(w) for w in widths],
        out_shape=[jax.ShapeDtypeStruct((T, w), dt) for w, dt in zip(widths, dtypes)],
        compiler_params=_cparams("parallel"),
        name="in_proj",
    )(x2, norm_w, w_in_bf16, cos_t, sin_t)


def _moba_kernel(q_ref, k_ref, v_ref, o_ref, kaug_ref, kmean_ref, vt_ref, acc_ref, s_ref,
                 sdiag_ref, *, nb):
    qi = pl.program_id(2)
    blk = MOBA_BLOCK
    lanes = V7X_LANES

    @pl.when(qi == 0)
    def _():
        kaug_ref[:, :lanes] = k_ref[0]
        rowb = lax.broadcasted_iota(jnp.int32, (nb * blk, lanes), 0) // blk
        col = lax.broadcasted_iota(jnp.int32, (nb * blk, lanes), 1)
        kaug_ref[:, lanes:] = jnp.where(rowb == col, 1.0, 0.0).astype(BF16)
        means = [jnp.sum(k_ref[0, n * blk:(n + 1) * blk, :].astype(F32), axis=0, keepdims=True)
                 * (1.0 / blk) for n in range(nb)]
        nbp = kmean_ref.shape[0]
        km = jnp.concatenate(means + [jnp.zeros((nbp - nb, lanes), F32)] * (nbp > nb), axis=0)
        hi = km.astype(BF16)
        rem = km - hi.astype(F32)
        mid = rem.astype(BF16)
        lo = (rem - mid.astype(F32)).astype(BF16)
        kmean_ref[...] = jnp.concatenate([hi, mid, lo], axis=1)

        for c in range(nb):
            vt = v_ref[0, c * blk:(c + 1) * blk, :].astype(F32).T
            vt_ref[:lanes, c * blk:(c + 1) * blk] = vt.astype(BF16)
        vt_ref[lanes:, :] = jnp.ones((vt_ref.shape[0] - lanes, nb * blk), BF16)

    qt = q_ref[0].astype(F32).T
    feat = lax.broadcasted_iota(jnp.int32, (lanes, blk), 0)
    key_i = lax.broadcasted_iota(jnp.int32, (blk, blk), 0)
    qry_i = lax.broadcasted_iota(jnp.int32, (blk, blk), 1)
    own = pl.multiple_of(qi * blk, blk)
    k_own = kaug_ref[pl.ds(own, blk), :lanes]
    nbp = kmean_ref.shape[0]
    blk_id = lax.broadcasted_iota(jnp.int32, (nbp, 2 * blk), 0)
    slab = 2 * V7X_SUBLANES

    def slab_max(s):
        return jnp.max(s.reshape(s.shape[0] // slab, slab, blk), axis=0)

    qhs = [jnp.where((feat // (ATT_HEAD_DIM // 2)) % 2 == hh, qt, 0.0).astype(BF16)
           for hh in range(2)]
    q2 = jnp.concatenate(qhs, axis=1)
    gate = jnp.dot(kmean_ref[...], jnp.concatenate([q2, q2, q2], axis=0),
                   preferred_element_type=F32)
    gate = jnp.where(blk_id < qi, gate, NEG)
    beaten = jnp.zeros((nbp, 2 * blk), F32)
    for n in range(nb):
        gn = gate[n:n + 1, :]
        wins = (gn > gate) | ((gn == gate) & (blk_id > n))
        beaten = beaten + jnp.where(wins, 1.0, 0.0)
    sel = (beaten < MOBA_TOPK) & (blk_id < qi)
    bias = jnp.where(sel, 0.0, NEG).astype(BF16)
    bias = jnp.concatenate([bias, jnp.zeros((lanes - nbp, 2 * blk), BF16)], axis=0)

    q_aug, m_init = [], []
    for hh in range(2):
        q_aug.append(jnp.concatenate([qhs[hh], bias[:, hh * blk:(hh + 1) * blk]], axis=0))
        s = jnp.dot(k_own, qhs[hh], preferred_element_type=F32)
        s = jnp.where(key_i <= qry_i, s, NEG)
        sdiag_ref[hh] = s
        m_init.append(slab_max(s))
        acc_ref[hh] = jnp.zeros(acc_ref.shape[1:], F32)

    group = MOBA_GROUP * blk
    n_groups = (qi + MOBA_GROUP) // MOBA_GROUP

    def score_group(g, ms):
        off = pl.multiple_of(g * group, group)
        kb = kaug_ref[pl.ds(off, group), :]
        out = []
        for hh in range(2):
            s = jnp.dot(kb, q_aug[hh], preferred_element_type=F32)
            s_ref[hh, pl.ds(off, group), :] = s
            out.append(jnp.maximum(ms[hh], slab_max(s)))
        return tuple(out)

    ms = lax.fori_loop(0, n_groups, score_group, tuple(m_init))

    m_fin = []
    for hh in range(2):
        s_ref[hh, pl.ds(own, blk), :] = sdiag_ref[hh]
        m_fin.append(jnp.max(ms[hh], axis=0, keepdims=True))

    def value_group(g, c):
        off = pl.multiple_of(g * group, group)
        vb = vt_ref[:, pl.ds(off, group)]
        for hh in range(2):
            p = jnp.exp(s_ref[hh, pl.ds(off, group), :] - m_fin[hh]).astype(BF16)
            acc_ref[hh] += jnp.dot(vb, p, preferred_element_type=F32)
        return c

    lax.fori_loop(0, n_groups, value_group, 0)

    half = lanes // 2
    o0 = acc_ref[0, :half, :] / acc_ref[0, lanes:lanes + 1, :]
    o1 = acc_ref[1, half:lanes, :] / acc_ref[1, lanes:lanes + 1, :]
    o_ref[0] = jnp.concatenate([o0, o1], axis=0).T.astype(BF16)


def _moba(q, k, v):
    B, S, _ = q.shape
    nb = S // MOBA_BLOCK
    assert nb % MOBA_GROUP == 0
    nbp = -(-nb // (2 * V7X_SUBLANES)) * (2 * V7X_SUBLANES)
    n_pairs = ATT_WIDTH // V7X_LANES
    vt_rows = V7X_LANES + 2 * V7X_SUBLANES
    qspec = pl.BlockSpec((1, MOBA_BLOCK, V7X_LANES), lambda b, p, i: (b, i, p))
    kvspec = pl.BlockSpec((1, S, V7X_LANES), lambda b, p, i: (b, 0, p))
    return pl.pallas_call(
        functools.partial(_moba_kernel, nb=nb),
        grid=(B, n_pairs, nb),
        in_specs=[qspec, kvspec, kvspec],
        out_specs=qspec,
        out_shape=jax.ShapeDtypeStruct((B, S, ATT_WIDTH), BF16),
        scratch_shapes=[pltpu.VMEM((S, 2 * V7X_LANES), BF16), pltpu.VMEM((nbp, 3 * V7X_LANES), BF16),
                        pltpu.VMEM((vt_rows, S), BF16), pltpu.VMEM((2, vt_rows, MOBA_BLOCK), F32),
                        pltpu.VMEM((2, S, MOBA_BLOCK), F32),
                        pltpu.VMEM((2, MOBA_BLOCK, MOBA_BLOCK), F32)],
        compiler_params=_cparams("parallel", "parallel", "arbitrary"),
        name="moba",
    )(q, k, v)


def _hgrn_level_sizes(chunk):
    return [chunk >> (i + 1) for i in range(int(math.log2(chunk)))]


def _hgrn_constants(chunk):
    t = np.arange(chunk)
    mats = [(t[None, :] <= t[:, None]),
            (t[None, :] > t[:, None])]
    qrows, pmasks = [], []
    for bs in _hgrn_level_sizes(chunk):
        blk = t // bs
        odd = (blk % 2) == 1
        lo, hi = blk * bs, (blk + 1) * bs
        u = t[None, :]
        m_odd = (u >= lo[:, None]) & (u <= t[:, None])
        m_even = (u > t[:, None]) & (u < hi[:, None])
        mats.append(np.where(odd[:, None], m_odd, m_even))
        qrows.append(odd)
        pmasks.append(odd[:, None] & (blk[None, :] == blk[:, None] - 1))
    pmasks.append(t[None, :] == t[:, None])
    summat = np.concatenate(mats, axis=0).astype(np.float32)
    qrow = np.stack(qrows, axis=0).astype(np.float32)
    pmask = np.stack(pmasks, axis=0).astype(np.float32)
    return summat, qrow, pmask


def _hgrn_kernel(q_ref, f_ref, i_ref, g_ref, lb_ref, nw_ref, sm_ref, qrow_ref, pm_ref,
                 o_ref, state_ref, *, rows):
    C = HGRN_CHUNK
    n_levels = qrow_ref.shape[0]

    @pl.when(pl.program_id(1) == 0)
    def _():
        state_ref[...] = jnp.zeros_like(state_ref)

    lb = lb_ref[...]
    summat = sm_ref[...]
    for c in range(rows // C):
        r0 = c * C
        fg = lb + (1.0 - lb) * jax.nn.sigmoid(f_ref[0, r0:r0 + C, :])
        logf = jnp.log(fg)
        hi = logf.astype(BF16)
        rem = logf - hi.astype(F32)
        mid = rem.astype(BF16)
        lo = (rem - mid.astype(F32)).astype(BF16)
        sums = jnp.dot(summat, jnp.concatenate([hi, mid, lo], axis=0),
                       preferred_element_type=F32)
        for h in range(HGRN_HEADS):
            ls = slice(h * HGRN_DIM, (h + 1) * HGRN_DIM)
            qf = jax.nn.silu(q_ref[0, r0:r0 + C, ls].astype(F32))
            kf = 1.0 - fg[:, ls]
            iv = i_ref[0, r0:r0 + C, ls]
            bcum = sums[0:C, ls]
            bsuf = sums[C:2 * C, ls]
            att = _nt_dot(qf.astype(BF16), kf.astype(BF16)) * pm_ref[n_levels]
            for lv in range(n_levels):
                w = jnp.exp(sums[(2 + lv) * C:(3 + lv) * C, ls])
                qrow = qrow_ref[lv]
                z = (jnp.where(qrow > 0.5, qf, kf) * w).astype(BF16)
                att = att + _nt_dot(z, z) * pm_ref[lv]
            o = jnp.dot(att.astype(BF16), iv, preferred_element_type=F32)
            st = state_ref[h]
            o = o + _nt_dot((qf * jnp.exp(bcum)).astype(BF16), st.astype(BF16))
            kdec = (kf * jnp.exp(bsuf)).astype(BF16)
            state_ref[h] = st * jnp.exp(bcum[C - 1:C, :]) + _tn_dot(iv, kdec)
            o = o * lax.rsqrt(jnp.mean(o * o, axis=-1, keepdims=True) + RMS_EPS)
            o = o * nw_ref[:, ls] * jax.nn.silu(g_ref[0, r0:r0 + C, ls].astype(F32))
            o_ref[0, r0:r0 + C, ls] = o.astype(BF16)


def _hgrn(qb, fb, ib, gb, lb, norm_w):
    B, S, W = qb.shape
    rows = min(HGRN_ROWS, S)
    summat, qrow, pmask = _hgrn_constants(HGRN_CHUNK)
    summat = jnp.asarray(np.concatenate([summat] * 3, axis=1), BF16)
    n_levels = qrow.shape[0]
    blk = pl.BlockSpec((1, rows, W), lambda b, s: (b, s, 0))
    vec = pl.BlockSpec((1, W), lambda b, s: (0, 0))
    const = lambda a: pl.BlockSpec(a.shape, lambda b, s: (0,) * a.ndim)
    qrow3 = qrow.reshape(n_levels, HGRN_CHUNK, 1)
    return pl.pallas_call(
        functools.partial(_hgrn_kernel, rows=rows),
        grid=(B, S // rows),
        in_specs=[blk, blk, blk, blk, vec, vec, const(summat), const(qrow3), const(pmask)],
        out_specs=blk,
        out_shape=jax.ShapeDtypeStruct((B, S, W), BF16),
        scratch_shapes=[pltpu.VMEM((HGRN_HEADS, HGRN_DIM, HGRN_DIM), F32)],
        compiler_params=_cparams("parallel", "arbitrary"),
        name="hgrn",
    )(qb, fb, ib, gb, lb, norm_w, jnp.asarray(summat), jnp.asarray(qrow3), jnp.asarray(pmask))


def _mix_route_kernel(ya_ref, yb_ref, ga_ref, gb_ref, x_ref, wa_ref, wb_ref, wo_ref, nw_ref,
                      rw_ref, rb_ref, tri_ref,
                      x1_ref, h2_ref, e_ref, w_ref, rank_ref, cnt_ref, carry_ref):
    @pl.when(pl.program_id(0) == 0)
    def _():
        carry_ref[...] = jnp.zeros_like(carry_ref)

    ua = jnp.dot(ya_ref[...], wa_ref[...], preferred_element_type=F32)
    ub = jnp.dot(yb_ref[...], wb_ref[...], preferred_element_type=F32)
    mixed = (jax.nn.sigmoid(ga_ref[...].astype(F32)) * ua
             + jax.nn.sigmoid(gb_ref[...].astype(F32)) * ub)
    x1 = x_ref[...] + jnp.dot(mixed.astype(BF16), wo_ref[...], preferred_element_type=F32)
    x1_ref[...] = x1
    h2 = x1 * lax.rsqrt(jnp.mean(x1 * x1, axis=-1, keepdims=True) + RMS_EPS) * nw_ref[...]
    h2_ref[...] = h2

    tm = x1.shape[0]
    logits = _nt_dot(rw_ref[...], h2, precision=lax.Precision.HIGHEST) + rb_ref[...]
    eid = lax.broadcasted_iota(jnp.int32, (N_EXPERTS, tm), 0)
    work = logits
    es, vs = [], []
    for _ in range(TOP_K):
        mx = jnp.max(work, axis=0, keepdims=True)
        idx = jnp.min(jnp.where(work == mx, eid, N_EXPERTS), axis=0, keepdims=True)
        es.append(idx)
        vs.append(mx)
        work = jnp.where(eid == idx, -jnp.inf, work)
    ex = [jnp.exp(v - vs[0]) for v in vs]
    den = ex[0] + ex[1] + ex[2] + ex[3]
    multi = jnp.zeros((N_EXPERTS, tm), F32)
    for k in range(TOP_K):
        multi = multi + jnp.where(eid == es[k], 1.0, 0.0)
    before = jnp.dot(multi.astype(BF16), tri_ref[...], preferred_element_type=F32) + carry_ref[...]
    for k in range(TOP_K):
        e_ref[k:k + 1, :] = es[k]
        w_ref[k:k + 1, :] = ex[k] / den
        rank_ref[k:k + 1, :] = jnp.sum(jnp.where(eid == es[k], before, 0.0), axis=0,
                                       keepdims=True).astype(jnp.int32)
    carry_ref[...] = carry_ref[...] + jnp.sum(multi, axis=1, keepdims=True)
    cnt_ref[...] = jnp.broadcast_to(carry_ref[...], cnt_ref.shape)


def _mix_route(ya, yb, ga, gb, x2, wa, wb, wo, norm_w, rw_t, rb, row0, rows):
    D = x2.shape[1]
    T = rows
    tm = min(MIX_ROWS, T)
    row = lambda w: pl.BlockSpec((tm, w), lambda i: (i + row0 // tm, 0))
    out_row = pl.BlockSpec((tm, D), lambda i: (i, 0))
    whole = pl.BlockSpec(memory_space=pltpu.VMEM)
    kt = pl.BlockSpec((TOP_K, tm), lambda i: (0, i))
    tri = jnp.asarray(np.triu(np.ones((tm, tm), np.float32), 1), BF16)
    return pl.pallas_call(
        _mix_route_kernel,
        grid=(T // tm,),
        in_specs=[row(ATT_WIDTH), row(HGRN_WIDTH), row(D), row(D), row(D),
                  whole, whole, whole, pl.BlockSpec((1, D), lambda i: (0, 0)),
                  whole, whole, whole],
        out_specs=[out_row, out_row, kt, kt, kt,
                   pl.BlockSpec((N_EXPERTS, V7X_LANES), lambda i: (0, 0))],
        out_shape=[jax.ShapeDtypeStruct((T, D), F32), jax.ShapeDtypeStruct((T, D), F32),
                   jax.ShapeDtypeStruct((TOP_K, T), jnp.int32),
                   jax.ShapeDtypeStruct((TOP_K, T), F32),
                   jax.ShapeDtypeStruct((TOP_K, T), jnp.int32),
                   jax.ShapeDtypeStruct((N_EXPERTS, V7X_LANES), F32)],
        scratch_shapes=[pltpu.VMEM((N_EXPERTS, 1), F32)],
        compiler_params=_cparams("arbitrary"),
        name="mix_route",
    )(ya, yb, ga, gb, x2, wa, wb, wo, norm_w, rw_t, rb, tri)


def _sc_worker_id():
    return lax.axis_index("s") * V7X_SC_CORES + lax.axis_index("c")


def _sc_kernel(body, out_rows, like, window, name):
    mesh = plsc.VectorSubcoreMesh(core_axis_name="c", subcore_axis_name="s")
    return pl.kernel(
        body, mesh=mesh,
        out_type=jax.ShapeDtypeStruct((out_rows,) + like.shape[1:], like.dtype),
        scratch_types=[pltpu.VMEM((window,), jnp.int32),
                       pltpu.VMEM((window,) + like.shape[1:], like.dtype)],
        name=name)


def _scatter_rows(dest, h2):
    T = dest.shape[1]
    n_workers = V7X_SC_CORES * V7X_SC_SUBCORES
    per_worker = T // n_workers
    window = min(SC_ROWS, per_worker)

    def body(dest_hbm, src_hbm, dst_hbm, idx_v, rows_v):
        base = _sc_worker_id() * per_worker

        @pl.loop(0, per_worker // window)
        def _(c):
            t0 = base + c * window
            pltpu.sync_copy(src_hbm.at[pl.ds(t0, window)], rows_v)
            for k in range(TOP_K):
                pltpu.sync_copy(dest_hbm.at[pl.ds(k * T + t0, window)], idx_v)
                pltpu.sync_copy(rows_v, dst_hbm.at[idx_v])

    return _sc_kernel(body, TOP_K * T, h2, window, "scatter_rows")(dest.reshape(-1), h2)


def _gather_rows(dest, yg):
    T = dest.shape[1]
    n_workers = V7X_SC_CORES * V7X_SC_SUBCORES
    per_worker = TOP_K * T // n_workers
    window = min(SC_ROWS, per_worker)

    def body(dest_hbm, src_hbm, dst_hbm, idx_v, rows_v):
        base = _sc_worker_id() * per_worker

        @pl.loop(0, per_worker // window)
        def _(c):
            r0 = base + c * window
            pltpu.sync_copy(dest_hbm.at[pl.ds(r0, window)], idx_v)
            pltpu.sync_copy(src_hbm.at[idx_v], rows_v)
            pltpu.sync_copy(rows_v, dst_hbm.at[pl.ds(r0, window)])

    return _sc_kernel(body, TOP_K * T, yg, window, "gather_rows")(dest.reshape(-1), yg)


def _prep_gate_up_kernel(w_ref, perm_ref, wg_ref, wu_ref):
    half = V7X_MXU_DIM // 2
    for g in range(w_ref.shape[2] // V7X_MXU_DIM):
        wb = w_ref[0, :, g * V7X_MXU_DIM:(g + 1) * V7X_MXU_DIM].astype(BF16)
        d = jnp.dot(wb, perm_ref[...], preferred_element_type=F32).astype(BF16)
        wg_ref[0, :, g * half:(g + 1) * half] = d[:, :half]
        wu_ref[0, :, g * half:(g + 1) * half] = d[:, half:]


def _prep_gate_up(w_gate_up):
    E, D, F2 = w_gate_up.shape
    rows = min(PREP_ROWS, D)
    i = np.arange(V7X_MXU_DIM)
    src = np.where(i < V7X_MXU_DIM // 2, 2 * i, 2 * (i - V7X_MXU_DIM // 2) + 1)
    perm = np.zeros((V7X_MXU_DIM, V7X_MXU_DIM), np.float32)
    perm[src, i] = 1.0
    out = pl.BlockSpec((1, rows, F2 // 2), lambda e, r: (e, r, 0))
    return pl.pallas_call(
        _prep_gate_up_kernel,
        grid=(E, D // rows),
        in_specs=[pl.BlockSpec((1, rows, F2), lambda e, r: (e, r, 0)),
                  pl.BlockSpec((V7X_MXU_DIM, V7X_MXU_DIM), lambda e, r: (0, 0))],
        out_specs=[out, out],
        out_shape=[jax.ShapeDtypeStruct((E, D, F2 // 2), BF16)] * 2,
        compiler_params=_cparams("parallel", "parallel"),
        name="prep_gate_up",
    )(w_gate_up, jnp.asarray(perm, BF16))


def _experts_kernel(blk_ref, exp_ref, lo_ref, hi_ref, first_ref,
                    x_ref, wg_ref, wu_ref, bg_ref, bu_ref, wd_ref, bd_ref, y_ref):
    w = pl.program_id(0)
    lo = lo_ref[w]
    hi = hi_ref[w]

    @pl.when(lo < hi)
    def _():
        x = x_ref[...].astype(BF16)
        g = jnp.dot(x, wg_ref[0], preferred_element_type=F32) + bg_ref[0]
        u = jnp.dot(x, wu_ref[0], preferred_element_type=F32) + bu_ref[0]
        g = jnp.minimum(g, SWIGLU_LIMIT)
        u = jnp.clip(u, -SWIGLU_LIMIT, SWIGLU_LIMIT)
        act = (u + 1.0) * (g * jax.nn.sigmoid(g * SWIGLU_ALPHA))
        y = jnp.dot(act.astype(BF16), wd_ref[0], preferred_element_type=F32) + bd_ref[0]
        whole = hi - lo == y.shape[0]

        @pl.when(whole)
        def _():
            y_ref[...] = y

        r = lax.broadcasted_iota(jnp.int32, y.shape, 0)
        mine = (r >= lo) & (r < hi)

        @pl.when(jnp.logical_not(whole) & (first_ref[w] == 1))
        def _():
            y_ref[...] = jnp.where(mine, y, 0.0)

        @pl.when(jnp.logical_not(whole) & (first_ref[w] == 0))
        def _():
            y_ref[...] = jnp.where(mine, y, y_ref[...])


def _work_items(counts, n_rows, tm):
    nblk = n_rows // tm
    n_items = nblk + N_EXPERTS - 1
    end = jnp.cumsum(counts)
    start = end - counts
    fb = start // tm
    nitems = jnp.where(counts > 0, (end - 1) // tm - fb + 1, 0)
    item_end = jnp.cumsum(nitems)
    item_start = item_end - nitems
    w = jnp.arange(n_items, dtype=jnp.int32)
    valid = w < item_end[-1]
    wc = jnp.minimum(w, item_end[-1] - 1)
    e = jnp.sum(wc[:, None] >= item_end[None, :], axis=1).astype(jnp.int32)
    e = jnp.minimum(e, N_EXPERTS - 1)
    onehot = e[:, None] == jnp.arange(N_EXPERTS, dtype=jnp.int32)[None, :]
    pick = lambda table: jnp.sum(jnp.where(onehot, table[None, :], 0), axis=1)
    blk = (pick(fb) + (wc - pick(item_start))).astype(jnp.int32)
    lo = jnp.maximum(pick(start), blk * tm) - blk * tm
    hi = jnp.minimum(pick(end), (blk + 1) * tm) - blk * tm
    lo = jnp.where(valid, lo, 0).astype(jnp.int32)
    hi = jnp.where(valid, hi, 0).astype(jnp.int32)
    first = jnp.concatenate([jnp.ones((1,), jnp.int32),
                             (blk[1:] != blk[:-1]).astype(jnp.int32)])
    return blk, e, lo, hi, first


def _experts(xg, counts, wg, wu, bg, bu, wd, bd):
    A, D = xg.shape
    tm = min(EXPERT_ROWS, A)
    F = wg.shape[2]
    items = _work_items(counts, A, tm)
    n_items = A // tm + N_EXPERTS - 1
    xs = pl.BlockSpec((tm, D), lambda w, blk, e, lo, hi, fi: (blk[w], 0))
    wsp = lambda r, c: pl.BlockSpec((1, r, c), lambda w, blk, e, lo, hi, fi: (e[w], 0, 0))
    return pl.pallas_call(
        _experts_kernel,
        grid_spec=pltpu.PrefetchScalarGridSpec(
            num_scalar_prefetch=5,
            grid=(n_items,),
            in_specs=[xs, wsp(D, F), wsp(D, F), wsp(1, F), wsp(1, F), wsp(F, D), wsp(1, D)],
            out_specs=xs,
        ),
        out_shape=jax.ShapeDtypeStruct(xg.shape, F32),
        compiler_params=_cparams("arbitrary"),
        name="experts",
    )(*items, xg, wg, wu, bg, bu, wd, bd)


def _combine_kernel(x1_ref, yk_ref, w_ref, nw_ref, *rest):
    o_ref = rest[-1]
    w = w_ref[...]
    x = x1_ref[...]
    for k in range(TOP_K):
        x = x + yk_ref[k] * w[:, k:k + 1]
    o_ref[...] = x * lax.rsqrt(jnp.mean(x * x, axis=-1, keepdims=True) + RMS_EPS) * nw_ref[...]


def _combine(x1, yk, w_tk, norm_w, out_rows, row0, prev_out):
    Tg, D = x1.shape
    tm = min(COMBINE_ROWS, Tg)
    row = pl.BlockSpec((tm, D), lambda i: (i, 0))
    in_specs = [row, pl.BlockSpec((TOP_K, tm, D), lambda i: (0, i, 0)),
                pl.BlockSpec((tm, TOP_K), lambda i: (i, 0)),
                pl.BlockSpec((1, D), lambda i: (0, 0))]
    args = [x1, yk, w_tk, norm_w]
    aliases = {}
    if prev_out is not None:
        in_specs.append(pl.BlockSpec(memory_space=pl.ANY))
        args.append(prev_out)
        aliases = {len(args) - 1: 0}
    return pl.pallas_call(
        _combine_kernel,
        grid=(Tg // tm,),
        in_specs=in_specs,
        out_specs=pl.BlockSpec((tm, D), lambda i: (i + row0 // tm, 0)),
        out_shape=jax.ShapeDtypeStruct((out_rows, D), F32),
        input_output_aliases=aliases,
        compiler_params=_cparams("parallel"),
        name="combine",
    )(*args)


def _qk_column_order():
    half = ATT_HEAD_DIM // 2
    order = []
    for p in range(ATT_HEADS // 2):
        for part in range(2):
            for h in (2 * p, 2 * p + 1):
                order.extend(range(h * ATT_HEAD_DIM + part * half, h * ATT_HEAD_DIM + (part + 1) * half))
    return np.asarray(order, np.int32)


def _rope_tables(seq):
    half = ATT_HEAD_DIM // 2
    inv = ROPE_THETA ** (-(jnp.arange(half, dtype=F32) * 2.0 / ATT_HEAD_DIM))
    ang = jnp.arange(seq, dtype=F32)[:, None] * inv[None, :]
    cos, sin = jnp.cos(ang), jnp.sin(ang)
    return (jnp.concatenate([cos, cos, cos, cos], axis=1),
            jnp.concatenate([-sin, -sin, sin, sin], axis=1))


def kernel(x, norm1_w, w_in, moba_up, hgrn_lb_logits, hgrn_norm_w, hgrn_up, w_out, norm2_w,
           router_w, router_b, w_gate_up, b_gate_up, w_down, b_down, final_norm_w):
    B, S, D = x.shape
    T = B * S
    assert S % MOBA_BLOCK == 0 and w_in.shape[0] == 1
    x2 = x.reshape(T, D)

    perm = _qk_column_order()
    w0 = w_in[0]
    w_in_p = jnp.concatenate([w0[:, :ATT_WIDTH][:, perm], w0[:, ATT_WIDTH:2 * ATT_WIDTH][:, perm],
                              w0[:, 2 * ATT_WIDTH:]], axis=1).astype(BF16)
    cos_t, sin_t = _rope_tables(S)
    lb = jnp.cumsum(jax.nn.softmax(hgrn_lb_logits.astype(F32), axis=0), axis=0)[0:1]

    qa, ka, va, qb, fb, ib, gb, ga, gtb = _in_proj(x2, norm1_w, w_in_p, cos_t, sin_t, S)
    r3 = lambda a: a.reshape(B, S, a.shape[1])
    ya = _moba(r3(qa), r3(ka), r3(va)).reshape(T, ATT_WIDTH)
    yb = _hgrn(r3(qb), r3(fb), r3(ib), r3(gb), lb, hgrn_norm_w).reshape(T, HGRN_WIDTH)

    mix_w = (moba_up[0].astype(BF16), hgrn_up[0].astype(BF16), w_out[0].astype(BF16), norm2_w,
             router_w[0].T, router_b[0][:, None])
    wg, wu = _prep_gate_up(w_gate_up[0])
    expert_w = (wg, wu, b_gate_up[0][:, None, 0::2], b_gate_up[0][:, None, 1::2],
                w_down[0].astype(BF16), b_down[0][:, None, :])

    Tg = T // MOE_TOKEN_GROUPS
    out = None
    for g in range(MOE_TOKEN_GROUPS):
        x1, h2, top_e, top_w, rank, cnt = _mix_route(ya, yb, ga, gtb, x2, *mix_w, g * Tg, Tg)
        counts = cnt[:, 0].astype(jnp.int32)
        start = jnp.cumsum(counts) - counts
        dest = rank + jnp.sum(jnp.where(top_e[:, :, None] == jnp.arange(N_EXPERTS)[None, None, :],
                                        start[None, None, :], 0), axis=-1)
        xg = _scatter_rows(dest, h2)
        yg = _experts(xg, counts, *expert_w)
        yk = _gather_rows(dest, yg).reshape(TOP_K, Tg, D)
        out = _combine(x1, yk, top_w.T, final_norm_w[None, :], T, g * Tg, out)
    return out.reshape(B, S, D)
```

```python
import functools
import math

import numpy as np
import jax
import jax.numpy as jnp
from jax import lax
from jax.experimental import pallas as pl
from jax.experimental.pallas import tpu as pltpu
from jax.experimental.pallas import tpu_sc as plsc

ATT_HEADS = 8
ATT_HEAD_DIM = 64
ATT_WIDTH = ATT_HEADS * ATT_HEAD_DIM
MOBA_BLOCK = 256
MOBA_TOPK = 3
ROPE_THETA = 10000.0
HGRN_HEADS = 4
HGRN_DIM = 128
HGRN_WIDTH = HGRN_HEADS * HGRN_DIM
HGRN_CHUNK = 64
N_EXPERTS = 32
TOP_K = 4
SWIGLU_LIMIT = 7.0
SWIGLU_ALPHA = 1.702
RMS_EPS = 1e-6
NEG = -1e30

V7X_LANES = 128
V7X_SUBLANES = 8
V7X_MXU_DIM = 256
V7X_VMEM_LIMIT_BYTES = 56 * 1024 * 1024
V7X_SC_CORES = 2
V7X_SC_SUBCORES = 16

MOBA_PAIRS = 2
MOBA_GROUP = 4
PROJ_ROWS = 512
PROJ_COLS = 512
HGRN_ROWS = 256
MIX_ROWS = 512
EXPERT_ROWS = 256
MOE_TOKEN_GROUPS = 2
SC_ROWS = 32
COMBINE_ROWS = 256
PREP_ROWS = 256

F32 = jnp.float32
BF16 = jnp.bfloat16


def _nt_dot(a, b, precision=None):
    return lax.dot_general(a, b, (((1,), (1,)), ((), ())), precision=precision,
                           preferred_element_type=F32)


def _tn_dot(a, b, precision=None):
    return lax.dot_general(a, b, (((0,), (0,)), ((), ())), precision=precision,
                           preferred_element_type=F32)


def _cparams(*sem):
    return pltpu.CompilerParams(dimension_semantics=sem, vmem_limit_bytes=V7X_VMEM_LIMIT_BYTES)


def _in_proj_kernel(x_ref, nw_ref, w_ref, cos_ref, sin_ref,
                    qa_ref, ka_ref, va_ref, qb_ref, fb_ref, ib_ref, gb_ref, ga_ref, gtb_ref):
    x = x_ref[...]
    h = x * lax.rsqrt(jnp.mean(x * x, axis=-1, keepdims=True) + RMS_EPS) * nw_ref[...]
    h = h.astype(BF16)
    cos = cos_ref[...]
    sin = sin_ref[...]

    def proj(c):
        return jnp.dot(h, w_ref[:, c * PROJ_COLS:(c + 1) * PROJ_COLS], preferred_element_type=F32)

    def rope(t):
        out = []
        for j in range(PROJ_COLS // V7X_LANES):
            tj = t[:, j * V7X_LANES:(j + 1) * V7X_LANES]
            out.append(tj * cos + pltpu.roll(tj, V7X_LANES // 2, 1) * sin)
        return jnp.concatenate(out, axis=1)

    qa_ref[...] = (rope(proj(0)) * (ATT_HEAD_DIM ** -0.5)).astype(BF16)
    ka_ref[...] = rope(proj(1)).astype(BF16)
    va_ref[...] = proj(2).astype(BF16)
    qb_ref[...] = proj(3).astype(BF16)
    fb_ref[...] = proj(4)
    ib_ref[...] = proj(5).astype(BF16)
    gb_ref[...] = proj(6).astype(BF16)
    ga_ref[:, :PROJ_COLS] = proj(7).astype(BF16)
    ga_ref[:, PROJ_COLS:] = proj(8).astype(BF16)
    gtb_ref[:, :PROJ_COLS] = proj(9).astype(BF16)
    gtb_ref[:, PROJ_COLS:] = proj(10).astype(BF16)


def _in_proj(x2, norm_w, w_in_bf16, cos_t, sin_t, seq):
    T, D = x2.shape
    tm = min(PROJ_ROWS, seq)
    n_seq_tiles = seq // tm
    row = lambda w: pl.BlockSpec((tm, w), lambda i: (i, 0))
    tab = pl.BlockSpec((tm, V7X_LANES), lambda i: (i % n_seq_tiles, 0))
    widths = [ATT_WIDTH] * 3 + [HGRN_WIDTH] * 4 + [D, D]
    dtypes = [BF16, BF16, BF16, BF16, F32, BF16, BF16, BF16, BF16]
    return pl.pallas_call(
        _in_proj_kernel,
        grid=(T // tm,),
        in_specs=[row(D), pl.BlockSpec((1, D), lambda i: (0, 0)),
                  pl.BlockSpec(memory_space=pltpu.VMEM), tab, tab],
        out_specs=[row(w) for w in widths],
        out_shape=[jax.ShapeDtypeStruct((T, w), dt) for w, dt in zip(widths, dtypes)],
        compiler_params=_cparams("parallel"),
        name="in_proj",
    )(x2, norm_w, w_in_bf16, cos_t, sin_t)


def _moba_kernel(q_ref, k_ref, v_ref, o_ref, kaug_ref, kmean_ref, vt_ref, acc_ref, s_ref,
                 sdiag_ref, p_ref, *, nb):
    qi = pl.program_id(2)
    blk = MOBA_BLOCK
    lanes = V7X_LANES
    n_pairs = kaug_ref.shape[0]
    heads = [(pp, hh) for pp in range(n_pairs) for hh in range(2)]
    nbp = kmean_ref.shape[1]

    @pl.when(qi == 0)
    def _():
        rowb = lax.broadcasted_iota(jnp.int32, (nb * blk, lanes), 0) // blk
        col = lax.broadcasted_iota(jnp.int32, (nb * blk, lanes), 1)
        onehot = jnp.where(rowb == col, 1.0, 0.0).astype(BF16)
        for pp in range(n_pairs):
            pl_ = slice(pp * lanes, (pp + 1) * lanes)
            kaug_ref[pp, :, :lanes] = k_ref[0, :, pl_]
            kaug_ref[pp, :, lanes:] = onehot
            means = [jnp.sum(k_ref[0, n * blk:(n + 1) * blk, pl_].astype(F32), axis=0,
                             keepdims=True) * (1.0 / blk) for n in range(nb)]
            km = jnp.concatenate(means + [jnp.zeros((nbp - nb, lanes), F32)] * (nbp > nb), axis=0)
            hi = km.astype(BF16)
            rem = km - hi.astype(F32)
            mid = rem.astype(BF16)
            lo = (rem - mid.astype(F32)).astype(BF16)
            kmean_ref[pp] = jnp.concatenate([hi, mid, lo], axis=1)
            for c in range(nb):
                vt = v_ref[0, c * blk:(c + 1) * blk, pl_].astype(F32).T
                vt_ref[pp, :lanes, c * blk:(c + 1) * blk] = vt.astype(BF16)
            vt_ref[pp, lanes:, :] = jnp.ones((vt_ref.shape[1] - lanes, nb * blk), BF16)

    feat = lax.broadcasted_iota(jnp.int32, (lanes, blk), 0)
    key_i = lax.broadcasted_iota(jnp.int32, (blk, blk), 0)
    qry_i = lax.broadcasted_iota(jnp.int32, (blk, blk), 1)
    own = pl.multiple_of(qi * blk, blk)
    blk_id = lax.broadcasted_iota(jnp.int32, (nbp, 2 * blk), 0)
    slab = 2 * V7X_SUBLANES

    def slab_max(s):
        return jnp.max(s.reshape(s.shape[0] // slab, slab, blk), axis=0)

    q_aug, m_init = [], []
    for pp in range(n_pairs):
        qt = q_ref[0, :, pp * lanes:(pp + 1) * lanes].astype(F32).T
        k_own = kaug_ref[pp, pl.ds(own, blk), :lanes]
        qhs = [jnp.where((feat // (ATT_HEAD_DIM // 2)) % 2 == hh, qt, 0.0).astype(BF16)
               for hh in range(2)]
        q2 = jnp.concatenate(qhs, axis=1)
        gate = jnp.dot(kmean_ref[pp], jnp.concatenate([q2, q2, q2], axis=0),
                       preferred_element_type=F32)
        gate = jnp.where(blk_id < qi, gate, NEG)
        beaten = jnp.zeros((nbp, 2 * blk), F32)
        for n in range(nb):
            gn = gate[n:n + 1, :]
            wins = (gn > gate) | ((gn == gate) & (blk_id > n))
            beaten = beaten + jnp.where(wins, 1.0, 0.0)
        sel = (beaten < MOBA_TOPK) & (blk_id < qi)
        bias = jnp.where(sel, 0.0, NEG).astype(BF16)
        bias = jnp.concatenate([bias, jnp.zeros((lanes - nbp, 2 * blk), BF16)], axis=0)
        for hh in range(2):
            h = 2 * pp + hh
            q_aug.append(jnp.concatenate([qhs[hh], bias[:, hh * blk:(hh + 1) * blk]], axis=0))
            s = jnp.dot(k_own, qhs[hh], preferred_element_type=F32)
            s = jnp.where(key_i <= qry_i, s, NEG)
            sdiag_ref[h] = s
            m_init.append(slab_max(s))
            acc_ref[h] = jnp.zeros(acc_ref.shape[1:], F32)

    group = MOBA_GROUP * blk
    n_groups = (qi + MOBA_GROUP) // MOBA_GROUP

    def score_group(g, ms):
        off = pl.multiple_of(g * group, group)
        out = []
        for pp in range(n_pairs):
            kb = kaug_ref[pp, pl.ds(off, group), :]
            for hh in range(2):
                h = 2 * pp + hh
                s = jnp.dot(kb, q_aug[h], preferred_element_type=F32)
                s_ref[h, pl.ds(off, group), :] = s
                out.append(jnp.maximum(ms[h], slab_max(s)))
        return tuple(out)

    ms = lax.fori_loop(0, n_groups, score_group, tuple(m_init))

    m_fin = []
    for h in range(len(heads)):
        s_ref[h, pl.ds(own, blk), :] = sdiag_ref[h]
        m_fin.append(jnp.max(ms[h], axis=0, keepdims=True))

    def exp_group(g):
        off = pl.multiple_of(g * group, group)
        for h in range(len(heads)):
            p_ref[h] = jnp.exp(s_ref[h, pl.ds(off, group), :] - m_fin[h]).astype(BF16)

    def value_group(g):
        off = pl.multiple_of(g * group, group)
        for pp in range(n_pairs):
            vb = vt_ref[pp, :, pl.ds(off, group)]
            for hh in range(2):
                h = 2 * pp + hh
                acc_ref[h] += jnp.dot(vb, p_ref[h], preferred_element_type=F32)

    def pipelined(g, c):
        value_group(g - 1)
        exp_group(g)
        return c

    exp_group(0)
    lax.fori_loop(1, n_groups, pipelined, 0)
    value_group(n_groups - 1)

    half = lanes // 2
    outs = []
    for pp in range(n_pairs):
        outs.append(acc_ref[2 * pp, :half, :] / acc_ref[2 * pp, lanes:lanes + 1, :])
        outs.append(acc_ref[2 * pp + 1, half:lanes, :] / acc_ref[2 * pp + 1, lanes:lanes + 1, :])
    o_ref[0] = jnp.concatenate(outs, axis=0).T.astype(BF16)


def _moba(q, k, v):
    B, S, _ = q.shape
    nb = S // MOBA_BLOCK
    assert nb % MOBA_GROUP == 0
    nbp = -(-nb // (2 * V7X_SUBLANES)) * (2 * V7X_SUBLANES)
    n_pairs = ATT_WIDTH // V7X_LANES
    pp = MOBA_PAIRS
    assert n_pairs % pp == 0
    vt_rows = V7X_LANES + 2 * V7X_SUBLANES
    qspec = pl.BlockSpec((1, MOBA_BLOCK, pp * V7X_LANES), lambda b, p, i: (b, i, p))
    kvspec = pl.BlockSpec((1, S, pp * V7X_LANES), lambda b, p, i: (b, 0, p))
    return pl.pallas_call(
        functools.partial(_moba_kernel, nb=nb),
        grid=(B, n_pairs // pp, nb),
        in_specs=[qspec, kvspec, kvspec],
        out_specs=qspec,
        out_shape=jax.ShapeDtypeStruct((B, S, ATT_WIDTH), BF16),
        scratch_shapes=[pltpu.VMEM((pp, S, 2 * V7X_LANES), BF16),
                        pltpu.VMEM((pp, nbp, 3 * V7X_LANES), BF16),
                        pltpu.VMEM((pp, vt_rows, S), BF16),
                        pltpu.VMEM((2 * pp, vt_rows, MOBA_BLOCK), F32),
                        pltpu.VMEM((2 * pp, S, MOBA_BLOCK), F32),
                        pltpu.VMEM((2 * pp, MOBA_BLOCK, MOBA_BLOCK), F32),
                        pltpu.VMEM((2 * pp, MOBA_GROUP * MOBA_BLOCK, MOBA_BLOCK), BF16)],
        compiler_params=_cparams("parallel", "parallel", "arbitrary"),
        name="moba",
    )(q, k, v)


def _hgrn_level_sizes(chunk):
    return [chunk >> (i + 1) for i in range(int(math.log2(chunk)))]


def _hgrn_constants(chunk):
    t = np.arange(chunk)
    mats = [(t[None, :] <= t[:, None]),
            (t[None, :] > t[:, None])]
    qrows, pmasks = [], []
    for bs in _hgrn_level_sizes(chunk):
        blk = t // bs
        odd = (blk % 2) == 1
        lo, hi = blk * bs, (blk + 1) * bs
        u = t[None, :]
        m_odd = (u >= lo[:, None]) & (u <= t[:, None])
        m_even = (u > t[:, None]) & (u < hi[:, None])
        mats.append(np.where(odd[:, None], m_odd, m_even))
        qrows.append(odd)
        pmasks.append(odd[:, None] & (blk[None, :] == blk[:, None] - 1))
    pmasks.append(t[None, :] == t[:, None])
    summat = np.concatenate(mats, axis=0).astype(np.float32)
    qrow = np.stack(qrows, axis=0).astype(np.float32)
    pmask = np.stack(pmasks, axis=0).astype(np.float32)
    return summat, qrow, pmask


def _hgrn_kernel(q_ref, f_ref, i_ref, g_ref, lb_ref, nw_ref, sm_ref, qrow_ref, pm_ref,
                 o_ref, state_ref, *, rows):
    C = HGRN_CHUNK
    n_levels = qrow_ref.shape[0]

    @pl.when(pl.program_id(1) == 0)
    def _():
        state_ref[...] = jnp.zeros_like(state_ref)

    lb = lb_ref[...]
    summat = sm_ref[...]
    for c in range(rows // C):
        r0 = c * C
        fg = lb + (1.0 - lb) * jax.nn.sigmoid(f_ref[0, r0:r0 + C, :])
        logf = jnp.log(fg)
        hi = logf.astype(BF16)
        rem = logf - hi.astype(F32)
        mid = rem.astype(BF16)
        lo = (rem - mid.astype(F32)).astype(BF16)
        sums = jnp.dot(summat, jnp.concatenate([hi, mid, lo], axis=0),
                       preferred_element_type=F32)
        for h in range(HGRN_HEADS):
            ls = slice(h * HGRN_DIM, (h + 1) * HGRN_DIM)
            qf = jax.nn.silu(q_ref[0, r0:r0 + C, ls].astype(F32))
            kf = 1.0 - fg[:, ls]
            iv = i_ref[0, r0:r0 + C, ls]
            bcum = sums[0:C, ls]
            bsuf = sums[C:2 * C, ls]
            att = _nt_dot(qf.astype(BF16), kf.astype(BF16)) * pm_ref[n_levels]
            for lv in range(n_levels):
                w = jnp.exp(sums[(2 + lv) * C:(3 + lv) * C, ls])
                qrow = qrow_ref[lv]
                z = (jnp.where(qrow > 0.5, qf, kf) * w).astype(BF16)
                att = att + _nt_dot(z, z) * pm_ref[lv]
            o = jnp.dot(att.astype(BF16), iv, preferred_element_type=F32)
            st = state_ref[h]
            o = o + _nt_dot((qf * jnp.exp(bcum)).astype(BF16), st.astype(BF16))
            kdec = (kf * jnp.exp(bsuf)).astype(BF16)
            state_ref[h] = st * jnp.exp(bcum[C - 1:C, :]) + _tn_dot(iv, kdec)
            o = o * lax.rsqrt(jnp.mean(o * o, axis=-1, keepdims=True) + RMS_EPS)
            o = o * nw_ref[:, ls] * jax.nn.silu(g_ref[0, r0:r0 + C, ls].astype(F32))
            o_ref[0, r0:r0 + C, ls] = o.astype(BF16)


def _hgrn(qb, fb, ib, gb, lb, norm_w):
    B, S, W = qb.shape
    rows = min(HGRN_ROWS, S)
    summat, qrow, pmask = _hgrn_constants(HGRN_CHUNK)
    summat = jnp.asarray(np.concatenate([summat] * 3, axis=1), BF16)
    n_levels = qrow.shape[0]
    blk = pl.BlockSpec((1, rows, W), lambda b, s: (b, s, 0))
    vec = pl.BlockSpec((1, W), lambda b, s: (0, 0))
    const = lambda a: pl.BlockSpec(a.shape, lambda b, s: (0,) * a.ndim)
    qrow3 = qrow.reshape(n_levels, HGRN_CHUNK, 1)
    return pl.pallas_call(
        functools.partial(_hgrn_kernel, rows=rows),
        grid=(B, S // rows),
        in_specs=[blk, blk, blk, blk, vec, vec, const(summat), const(qrow3), const(pmask)],
        out_specs=blk,
        out_shape=jax.ShapeDtypeStruct((B, S, W), BF16),
        scratch_shapes=[pltpu.VMEM((HGRN_HEADS, HGRN_DIM, HGRN_DIM), F32)],
        compiler_params=_cparams("parallel", "arbitrary"),
        name="hgrn",
    )(qb, fb, ib, gb, lb, norm_w, jnp.asarray(summat), jnp.asarray(qrow3), jnp.asarray(pmask))


def _mix_route_kernel(ya_ref, yb_ref, ga_ref, gb_ref, x_ref, wa_ref, wb_ref, wo_ref, nw_ref,
                      rw_ref, rb_ref, tri_ref,
                      x1_ref, h2_ref, e_ref, w_ref, rank_ref, cnt_ref, carry_ref):
    @pl.when(pl.program_id(0) == 0)
    def _():
        carry_ref[...] = jnp.zeros_like(carry_ref)

    ua = jnp.dot(ya_ref[...], wa_ref[...], preferred_element_type=F32)
    ub = jnp.dot(yb_ref[...], wb_ref[...], preferred_element_type=F32)
    mixed = (jax.nn.sigmoid(ga_ref[...].astype(F32)) * ua
             + jax.nn.sigmoid(gb_ref[...].astype(F32)) * ub)
    x1 = x_ref[...] + jnp.dot(mixed.astype(BF16), wo_ref[...], preferred_element_type=F32)
    x1_ref[...] = x1
    h2 = x1 * lax.rsqrt(jnp.mean(x1 * x1, axis=-1, keepdims=True) + RMS_EPS) * nw_ref[...]
    h2_ref[...] = h2

    tm = x1.shape[0]
    logits = _nt_dot(rw_ref[...], h2, precision=lax.Precision.HIGHEST) + rb_ref[...]
    eid = lax.broadcasted_iota(jnp.int32, (N_EXPERTS, tm), 0)
    work = logits
    es, vs = [], []
    for _ in range(TOP_K):
        mx = jnp.max(work, axis=0, keepdims=True)
        idx = jnp.min(jnp.where(work == mx, eid, N_EXPERTS), axis=0, keepdims=True)
        es.append(idx)
        vs.append(mx)
        work = jnp.where(eid == idx, -jnp.inf, work)
    ex = [jnp.exp(v - vs[0]) for v in vs]
    den = ex[0] + ex[1] + ex[2] + ex[3]
    multi = jnp.zeros((N_EXPERTS, tm), F32)
    for k in range(TOP_K):
        multi = multi + jnp.where(eid == es[k], 1.0, 0.0)
    before = jnp.dot(multi.astype(BF16), tri_ref[...], preferred_element_type=F32) + carry_ref[...]
    for k in range(TOP_K):
        e_ref[k:k + 1, :] = es[k]
        w_ref[k:k + 1, :] = ex[k] / den
        rank_ref[k:k + 1, :] = jnp.sum(jnp.where(eid == es[k], before, 0.0), axis=0,
                                       keepdims=True).astype(jnp.int32)
    carry_ref[...] = carry_ref[...] + jnp.sum(multi, axis=1, keepdims=True)
    cnt_ref[...] = jnp.broadcast_to(carry_ref[...], cnt_ref.shape)


def _mix_route(ya, yb, ga, gb, x2, wa, wb, wo, norm_w, rw_t, rb, row0, rows):
    D = x2.shape[1]
    T = rows
    tm = min(MIX_ROWS, T)
    row = lambda w: pl.BlockSpec((tm, w), lambda i: (i + row0 // tm, 0))
    out_row = pl.BlockSpec((tm, D), lambda i: (i, 0))
    whole = pl.BlockSpec(memory_space=pltpu.VMEM)
    kt = pl.BlockSpec((TOP_K, tm), lambda i: (0, i))
    tri = jnp.asarray(np.triu(np.ones((tm, tm), np.float32), 1), BF16)
    return pl.pallas_call(
        _mix_route_kernel,
        grid=(T // tm,),
        in_specs=[row(ATT_WIDTH), row(HGRN_WIDTH), row(D), row(D), row(D),
                  whole, whole, whole, pl.BlockSpec((1, D), lambda i: (0, 0)),
                  whole, whole, whole],
        out_specs=[out_row, out_row, kt, kt, kt,
                   pl.BlockSpec((N_EXPERTS, V7X_LANES), lambda i: (0, 0))],
        out_shape=[jax.ShapeDtypeStruct((T, D), F32), jax.ShapeDtypeStruct((T, D), F32),
                   jax.ShapeDtypeStruct((TOP_K, T), jnp.int32),
                   jax.ShapeDtypeStruct((TOP_K, T), F32),
                   jax.ShapeDtypeStruct((TOP_K, T), jnp.int32),
                   jax.ShapeDtypeStruct((N_EXPERTS, V7X_LANES), F32)],
        scratch_shapes=[pltpu.VMEM((N_EXPERTS, 1), F32)],
        compiler_params=_cparams("arbitrary"),
        name="mix_route",
    )(ya, yb, ga, gb, x2, wa, wb, wo, norm_w, rw_t, rb, tri)


def _sc_worker_id():
    return lax.axis_index("s") * V7X_SC_CORES + lax.axis_index("c")


def _sc_kernel(body, out_rows, like, window, name):
    mesh = plsc.VectorSubcoreMesh(core_axis_name="c", subcore_axis_name="s")
    return pl.kernel(
        body, mesh=mesh,
        out_type=jax.ShapeDtypeStruct((out_rows,) + like.shape[1:], like.dtype),
        scratch_types=[pltpu.VMEM((window,), jnp.int32),
                       pltpu.VMEM((window,) + like.shape[1:], like.dtype)],
        name=name)


def _scatter_rows(dest, h2):
    T = dest.shape[1]
    n_workers = V7X_SC_CORES * V7X_SC_SUBCORES
    per_worker = T // n_workers
    window = min(SC_ROWS, per_worker)

    def body(dest_hbm, src_hbm, dst_hbm, idx_v, rows_v):
        base = _sc_worker_id() * per_worker

        @pl.loop(0, per_worker // window)
        def _(c):
            t0 = base + c * window
            pltpu.sync_copy(src_hbm.at[pl.ds(t0, window)], rows_v)
            for k in range(TOP_K):
                pltpu.sync_copy(dest_hbm.at[pl.ds(k * T + t0, window)], idx_v)
                pltpu.sync_copy(rows_v, dst_hbm.at[idx_v])

    return _sc_kernel(body, TOP_K * T, h2, window, "scatter_rows")(dest.reshape(-1), h2)


def _gather_rows(dest, yg):
    T = dest.shape[1]
    n_workers = V7X_SC_CORES * V7X_SC_SUBCORES
    per_worker = TOP_K * T // n_workers
    window = min(SC_ROWS, per_worker)

    def body(dest_hbm, src_hbm, dst_hbm, idx_v, rows_v):
        base = _sc_worker_id() * per_worker

        @pl.loop(0, per_worker // window)
        def _(c):
            r0 = base + c * window
            pltpu.sync_copy(dest_hbm.at[pl.ds(r0, window)], idx_v)
            pltpu.sync_copy(src_hbm.at[idx_v], rows_v)
            pltpu.sync_copy(rows_v, dst_hbm.at[pl.ds(r0, window)])

    return _sc_kernel(body, TOP_K * T, yg, window, "gather_rows")(dest.reshape(-1), yg)


def _prep_gate_up_kernel(w_ref, perm_ref, wg_ref, wu_ref):
    half = V7X_MXU_DIM // 2
    for g in range(w_ref.shape[2] // V7X_MXU_DIM):
        wb = w_ref[0, :, g * V7X_MXU_DIM:(g + 1) * V7X_MXU_DIM].astype(BF16)
        d = jnp.dot(wb, perm_ref[...], preferred_element_type=F32).astype(BF16)
        wg_ref[0, :, g * half:(g + 1) * half] = d[:, :half]
        wu_ref[0, :, g * half:(g + 1) * half] = d[:, half:]


def _prep_gate_up(w_gate_up):
    E, D, F2 = w_gate_up.shape
    rows = min(PREP_ROWS, D)
    i = np.arange(V7X_MXU_DIM)
    src = np.where(i < V7X_MXU_DIM // 2, 2 * i, 2 * (i - V7X_MXU_DIM // 2) + 1)
    perm = np.zeros((V7X_MXU_DIM, V7X_MXU_DIM), np.float32)
    perm[src, i] = 1.0
    out = pl.BlockSpec((1, rows, F2 // 2), lambda e, r: (e, r, 0))
    return pl.pallas_call(
        _prep_gate_up_kernel,
        grid=(E, D // rows),
        in_specs=[pl.BlockSpec((1, rows, F2), lambda e, r: (e, r, 0)),
                  pl.BlockSpec((V7X_MXU_DIM, V7X_MXU_DIM), lambda e, r: (0, 0))],
        out_specs=[out, out],
        out_shape=[jax.ShapeDtypeStruct((E, D, F2 // 2), BF16)] * 2,
        compiler_params=_cparams("parallel", "parallel"),
        name="prep_gate_up",
    )(w_gate_up, jnp.asarray(perm, BF16))


def _experts_kernel(blk_ref, exp_ref, lo_ref, hi_ref, first_ref,
                    x_ref, wg_ref, wu_ref, bg_ref, bu_ref, wd_ref, bd_ref, y_ref):
    w = pl.program_id(0)
    lo = lo_ref[w]
    hi = hi_ref[w]

    @pl.when(lo < hi)
    def _():
        x = x_ref[...].astype(BF16)
        g = jnp.dot(x, wg_ref[0], preferred_element_type=F32) + bg_ref[0]
        u = jnp.dot(x, wu_ref[0], preferred_element_type=F32) + bu_ref[0]
        g = jnp.minimum(g, SWIGLU_LIMIT)
        u = jnp.clip(u, -SWIGLU_LIMIT, SWIGLU_LIMIT)
        act = (u + 1.0) * (g * jax.nn.sigmoid(g * SWIGLU_ALPHA))
        y = jnp.dot(act.astype(BF16), wd_ref[0], preferred_element_type=F32) + bd_ref[0]
        whole = hi - lo == y.shape[0]

        @pl.when(whole)
        def _():
            y_ref[...] = y

        r = lax.broadcasted_iota(jnp.int32, y.shape, 0)
        mine = (r >= lo) & (r < hi)

        @pl.when(jnp.logical_not(whole) & (first_ref[w] == 1))
        def _():
            y_ref[...] = jnp.where(mine, y, 0.0)

        @pl.when(jnp.logical_not(whole) & (first_ref[w] == 0))
        def _():
            y_ref[...] = jnp.where(mine, y, y_ref[...])


def _work_items(counts, n_rows, tm):
    nblk = n_rows // tm
    n_items = nblk + N_EXPERTS - 1
    end = jnp.cumsum(counts)
    start = end - counts
    fb = start // tm
    nitems = jnp.where(counts > 0, (end - 1) // tm - fb + 1, 0)
    item_end = jnp.cumsum(nitems)
    item_start = item_end - nitems
    w = jnp.arange(n_items, dtype=jnp.int32)
    valid = w < item_end[-1]
    wc = jnp.minimum(w, item_end[-1] - 1)
    e = jnp.sum(wc[:, None] >= item_end[None, :], axis=1).astype(jnp.int32)
    e = jnp.minimum(e, N_EXPERTS - 1)
    onehot = e[:, None] == jnp.arange(N_EXPERTS, dtype=jnp.int32)[None, :]
    pick = lambda table: jnp.sum(jnp.where(onehot, table[None, :], 0), axis=1)
    blk = (pick(fb) + (wc - pick(item_start))).astype(jnp.int32)
    lo = jnp.maximum(pick(start), blk * tm) - blk * tm
    hi = jnp.minimum(pick(end), (blk + 1) * tm) - blk * tm
    lo = jnp.where(valid, lo, 0).astype(jnp.int32)
    hi = jnp.where(valid, hi, 0).astype(jnp.int32)
    first = jnp.concatenate([jnp.ones((1,), jnp.int32),
                             (blk[1:] != blk[:-1]).astype(jnp.int32)])
    return blk, e, lo, hi, first


def _experts(xg, counts, wg, wu, bg, bu, wd, bd):
    A, D = xg.shape
    tm = min(EXPERT_ROWS, A)
    F = wg.shape[2]
    items = _work_items(counts, A, tm)
    n_items = A // tm + N_EXPERTS - 1
    xs = pl.BlockSpec((tm, D), lambda w, blk, e, lo, hi, fi: (blk[w], 0))
    wsp = lambda r, c: pl.BlockSpec((1, r, c), lambda w, blk, e, lo, hi, fi: (e[w], 0, 0))
    return pl.pallas_call(
        _experts_kernel,
        grid_spec=pltpu.PrefetchScalarGridSpec(
            num_scalar_prefetch=5,
            grid=(n_items,),
            in_specs=[xs, wsp(D, F), wsp(D, F), wsp(1, F), wsp(1, F), wsp(F, D), wsp(1, D)],
            out_specs=xs,
        ),
        out_shape=jax.ShapeDtypeStruct(xg.shape, F32),
        compiler_params=_cparams("arbitrary"),
        name="experts",
    )(*items, xg, wg, wu, bg, bu, wd, bd)


def _combine_kernel(x1_ref, yk_ref, w_ref, nw_ref, *rest):
    o_ref = rest[-1]
    w = w_ref[...]
    x = x1_ref[...]
    for k in range(TOP_K):
        x = x + yk_ref[k] * w[:, k:k + 1]
    o_ref[...] = x * lax.rsqrt(jnp.mean(x * x, axis=-1, keepdims=True) + RMS_EPS) * nw_ref[...]


def _combine(x1, yk, w_tk, norm_w, out_rows, row0, prev_out):
    Tg, D = x1.shape
    tm = min(COMBINE_ROWS, Tg)
    row = pl.BlockSpec((tm, D), lambda i: (i, 0))
    in_specs = [row, pl.BlockSpec((TOP_K, tm, D), lambda i: (0, i, 0)),
                pl.BlockSpec((tm, TOP_K), lambda i: (i, 0)),
                pl.BlockSpec((1, D), lambda i: (0, 0))]
    args = [x1, yk, w_tk, norm_w]
    aliases = {}
    if prev_out is not None:
        in_specs.append(pl.BlockSpec(memory_space=pl.ANY))
        args.append(prev_out)
        aliases = {len(args) - 1: 0}
    return pl.pallas_call(
        _combine_kernel,
        grid=(Tg // tm,),
        in_specs=in_specs,
        out_specs=pl.BlockSpec((tm, D), lambda i: (i + row0 // tm, 0)),
        out_shape=jax.ShapeDtypeStruct((out_rows, D), F32),
        input_output_aliases=aliases,
        compiler_params=_cparams("parallel"),
        name="combine",
    )(*args)


def _qk_column_order():
    half = ATT_HEAD_DIM // 2
    order = []
    for p in range(ATT_HEADS // 2):
        for part in range(2):
            for h in (2 * p, 2 * p + 1):
                order.extend(range(h * ATT_HEAD_DIM + part * half, h * ATT_HEAD_DIM + (part + 1) * half))
    return np.asarray(order, np.int32)


def _rope_tables(seq):
    half = ATT_HEAD_DIM // 2
    inv = ROPE_THETA ** (-(jnp.arange(half, dtype=F32) * 2.0 / ATT_HEAD_DIM))
    ang = jnp.arange(seq, dtype=F32)[:, None] * inv[None, :]
    cos, sin = jnp.cos(ang), jnp.sin(ang)
    return (jnp.concatenate([cos, cos, cos, cos], axis=1),
            jnp.concatenate([-sin, -sin, sin, sin], axis=1))


def kernel(x, norm1_w, w_in, moba_up, hgrn_lb_logits, hgrn_norm_w, hgrn_up, w_out, norm2_w,
           router_w, router_b, w_gate_up, b_gate_up, w_down, b_down, final_norm_w):
    B, S, D = x.shape
    T = B * S
    assert S % MOBA_BLOCK == 0 and w_in.shape[0] == 1
    x2 = x.reshape(T, D)

    perm = _qk_column_order()
    w0 = w_in[0]
    w_in_p = jnp.concatenate([w0[:, :ATT_WIDTH][:, perm], w0[:, ATT_WIDTH:2 * ATT_WIDTH][:, perm],
                              w0[:, 2 * ATT_WIDTH:]], axis=1).astype(BF16)
    cos_t, sin_t = _rope_tables(S)
    lb = jnp.cumsum(jax.nn.softmax(hgrn_lb_logits.astype(F32), axis=0), axis=0)[0:1]

    qa, ka, va, qb, fb, ib, gb, ga, gtb = _in_proj(x2, norm1_w, w_in_p, cos_t, sin_t, S)
    r3 = lambda a: a.reshape(B, S, a.shape[1])
    ya = _moba(r3(qa), r3(ka), r3(va)).reshape(T, ATT_WIDTH)
    yb = _hgrn(r3(qb), r3(fb), r3(ib), r3(gb), lb, hgrn_norm_w).reshape(T, HGRN_WIDTH)

    mix_w = (moba_up[0].astype(BF16), hgrn_up[0].astype(BF16), w_out[0].astype(BF16), norm2_w,
             router_w[0].T, router_b[0][:, None])
    wg, wu = _prep_gate_up(w_gate_up[0])
    expert_w = (wg, wu, b_gate_up[0][:, None, 0::2], b_gate_up[0][:, None, 1::2],
                w_down[0].astype(BF16), b_down[0][:, None, :])

    Tg = T // MOE_TOKEN_GROUPS
    out = None
    for g in range(MOE_TOKEN_GROUPS):
        x1, h2, top_e, top_w, rank, cnt = _mix_route(ya, yb, ga, gtb, x2, *mix_w, g * Tg, Tg)
        counts = cnt[:, 0].astype(jnp.int32)
        start = jnp.cumsum(counts) - counts
        dest = rank + jnp.sum(jnp.where(top_e[:, :, None] == jnp.arange(N_EXPERTS)[None, None, :],
                                        start[None, None, :], 0), axis=-1)
        xg = _scatter_rows(dest, h2)
        yg = _experts(xg, counts, *expert_w)
        yk = _gather_rows(dest, yg).reshape(TOP_K, Tg, D)
        out = _combine(x1, yk, top_w.T, final_norm_w[None, :], T, g * Tg, out)
    return out.reshape(B, S, D)
```

```python
import functools
import math

import numpy as np
import jax
import jax.numpy as jnp
from jax import lax
from jax.experimental import pallas as pl
from jax.experimental.pallas import tpu as pltpu
from jax.experimental.pallas import tpu_sc as plsc

ATT_HEADS = 8
ATT_HEAD_DIM = 64
ATT_WIDTH = ATT_HEADS * ATT_HEAD_DIM
MOBA_BLOCK = 256
MOBA_TOPK = 3
ROPE_THETA = 10000.0
HGRN_HEADS = 4
HGRN_DIM = 128
HGRN_WIDTH = HGRN_HEADS * HGRN_DIM
HGRN_CHUNK = 64
N_EXPERTS = 32
TOP_K = 4
SWIGLU_LIMIT = 7.0
SWIGLU_ALPHA = 1.702
RMS_EPS = 1e-6
NEG = -1e30

V7X_LANES = 128
V7X_SUBLANES = 8
V7X_MXU_DIM = 256
V7X_VMEM_LIMIT_BYTES = 56 * 1024 * 1024
V7X_SC_CORES = 2
V7X_SC_SUBCORES = 16

MOBA_PAIRS = 2
MOBA_GROUP = 4
PROJ_ROWS = 512
PROJ_COLS = 512
HGRN_ROWS = 256
MIX_ROWS = 512
EXPERT_ROWS = 256
MOE_TOKEN_GROUPS = 2
SC_ROWS = 32
COMBINE_ROWS = 256
PREP_ROWS = 256

F32 = jnp.float32
BF16 = jnp.bfloat16


def _nt_dot(a, b, precision=None):
    return lax.dot_general(a, b, (((1,), (1,)), ((), ())), precision=precision,
                           preferred_element_type=F32)


def _tn_dot(a, b, precision=None):
    return lax.dot_general(a, b, (((0,), (0,)), ((), ())), precision=precision,
                           preferred_element_type=F32)


def _cparams(*sem):
    return pltpu.CompilerParams(dimension_semantics=sem, vmem_limit_bytes=V7X_VMEM_LIMIT_BYTES)


def _in_proj_kernel(x_ref, nw_ref, w_ref, cos_ref, sin_ref,
                    qa_ref, ka_ref, va_ref, qb_ref, fb_ref, ib_ref, gb_ref, ga_ref, gtb_ref):
    x = x_ref[...]
    h = x * lax.rsqrt(jnp.mean(x * x, axis=-1, keepdims=True) + RMS_EPS) * nw_ref[...]
    h = h.astype(BF16)
    cos = cos_ref[...]
    sin = sin_ref[...]

    def proj(c):
        return jnp.dot(h, w_ref[:, c * PROJ_COLS:(c + 1) * PROJ_COLS], preferred_element_type=F32)

    def rope(t):
        out = []
        for j in range(PROJ_COLS // V7X_LANES):
            tj = t[:, j * V7X_LANES:(j + 1) * V7X_LANES]
            out.append(tj * cos + pltpu.roll(tj, V7X_LANES // 2, 1) * sin)
        return jnp.concatenate(out, axis=1)

    qa_ref[...] = (rope(proj(0)) * (ATT_HEAD_DIM ** -0.5)).astype(BF16)
    ka_ref[...] = rope(proj(1)).astype(BF16)
    va_ref[...] = proj(2).astype(BF16)
    qb_ref[...] = proj(3).astype(BF16)
    fb_ref[...] = proj(4)
    ib_ref[...] = proj(5).astype(BF16)
    gb_ref[...] = proj(6).astype(BF16)
    ga_ref[:, :PROJ_COLS] = proj(7).astype(BF16)
    ga_ref[:, PROJ_COLS:] = proj(8).astype(BF16)
    gtb_ref[:, :PROJ_COLS] = proj(9).astype(BF16)
    gtb_ref[:, PROJ_COLS:] = proj(10).astype(BF16)


def _in_proj(x2, norm_w, w_in_bf16, cos_t, sin_t, seq):
    T, D = x2.shape
    tm = min(PROJ_ROWS, seq)
    n_seq_tiles = seq // tm
    row = lambda w: pl.BlockSpec((tm, w), lambda i: (i, 0))
    tab = pl.BlockSpec((tm, V7X_LANES), lambda i: (i % n_seq_tiles, 0))
    widths = [ATT_WIDTH] * 3 + [HGRN_WIDTH] * 4 + [D, D]
    dtypes = [BF16, BF16, BF16, BF16, F32, BF16, BF16, BF16, BF16]
    return pl.pallas_call(
        _in_proj_kernel,
        grid=(T // tm,),
        in_specs=[row(D), pl.BlockSpec((1, D), lambda i: (0, 0)),
                  pl.BlockSpec(memory_space=pltpu.VMEM), tab, tab],
        out_specs=[row(w) for w in widths],
        out_shape=[jax.ShapeDtypeStruct((T, w), dt) for w, dt in zip(widths, dtypes)],
        compiler_params=_cparams("parallel"),
        name="in_proj",
    )(x2, norm_w, w_in_bf16, cos_t, sin_t)


def _moba_kernel(q_ref, k_ref, v_ref, o_ref, kaug_ref, kmean_ref, vt_ref, acc_ref, s_ref,
                 sdiag_ref, p_ref, *, nb):
    qi = pl.program_id(2)
    blk = MOBA_BLOCK
    lanes = V7X_LANES
    n_pairs = kaug_ref.shape[0]
    heads = [(pp, hh) for pp in range(n_pairs) for hh in range(2)]
    nbp = kmean_ref.shape[1]

    @pl.when(qi == 0)
    def _():
        rowb = lax.broadcasted_iota(jnp.int32, (nb * blk, lanes), 0) // blk
        col = lax.broadcasted_iota(jnp.int32, (nb * blk, lanes), 1)
        onehot = jnp.where(rowb == col, 1.0, 0.0).astype(BF16)
        for pp in range(n_pairs):
            pl_ = slice(pp * lanes, (pp + 1) * lanes)
            kaug_ref[pp, :, :lanes] = k_ref[0, :, pl_]
            kaug_ref[pp, :, lanes:] = onehot
            means = [jnp.sum(k_ref[0, n * blk:(n + 1) * blk, pl_].astype(F32), axis=0,
                             keepdims=True) * (1.0 / blk) for n in range(nb)]
            km = jnp.concatenate(means + [jnp.zeros((nbp - nb, lanes), F32)] * (nbp > nb), axis=0)
            hi = km.astype(BF16)
            rem = km - hi.astype(F32)
            mid = rem.astype(BF16)
            lo = (rem - mid.astype(F32)).astype(BF16)
            kmean_ref[pp] = jnp.concatenate([hi, mid, lo], axis=1)
            for c in range(nb):
                vt = v_ref[0, c * blk:(c + 1) * blk, pl_].astype(F32).T
                vt_ref[pp, :lanes, c * blk:(c + 1) * blk] = vt.astype(BF16)
            vt_ref[pp, lanes:, :] = jnp.ones((vt_ref.shape[1] - lanes, nb * blk), BF16)

    feat = lax.broadcasted_iota(jnp.int32, (lanes, blk), 0)
    key_i = lax.broadcasted_iota(jnp.int32, (blk, blk), 0)
    qry_i = lax.broadcasted_iota(jnp.int32, (blk, blk), 1)
    own = pl.multiple_of(qi * blk, blk)
    blk_id = lax.broadcasted_iota(jnp.int32, (nbp, 2 * blk), 0)
    slab = 2 * V7X_SUBLANES

    def slab_max(s):
        return jnp.max(s.reshape(s.shape[0] // slab, slab, blk), axis=0)

    q_aug, m_init = [], []
    for pp in range(n_pairs):
        qt = q_ref[0, :, pp * lanes:(pp + 1) * lanes].astype(F32).T
        k_own = kaug_ref[pp, pl.ds(own, blk), :lanes]
        qhs = [jnp.where((feat // (ATT_HEAD_DIM // 2)) % 2 == hh, qt, 0.0).astype(BF16)
               for hh in range(2)]
        q2 = jnp.concatenate(qhs, axis=1)
        gate = jnp.dot(kmean_ref[pp], jnp.concatenate([q2, q2, q2], axis=0),
                       preferred_element_type=F32)
        gate = jnp.where(blk_id < qi, gate, NEG)
        beaten = jnp.zeros((nbp, 2 * blk), F32)
        for n in range(nb):
            gn = gate[n:n + 1, :]
            wins = (gn > gate) | ((gn == gate) & (blk_id > n))
            beaten = beaten + jnp.where(wins, 1.0, 0.0)
        sel = (beaten < MOBA_TOPK) & (blk_id < qi)
        bias = jnp.where(sel, 0.0, NEG).astype(BF16)
        bias = jnp.concatenate([bias, jnp.zeros((lanes - nbp, 2 * blk), BF16)], axis=0)
        for hh in range(2):
            h = 2 * pp + hh
            q_aug.append(jnp.concatenate([qhs[hh], bias[:, hh * blk:(hh + 1) * blk]], axis=0))
            s = jnp.dot(k_own, qhs[hh], preferred_element_type=F32)
            s = jnp.where(key_i <= qry_i, s, NEG)
            sdiag_ref[h] = s
            m_init.append(slab_max(s))
            acc_ref[h] = jnp.zeros(acc_ref.shape[1:], F32)

    group = MOBA_GROUP * blk
    n_groups = (qi + MOBA_GROUP) // MOBA_GROUP

    def score_group(g, ms):
        off = pl.multiple_of(g * group, group)
        out = []
        for pp in range(n_pairs):
            kb = kaug_ref[pp, pl.ds(off, group), :]
            for hh in range(2):
                h = 2 * pp + hh
                s = jnp.dot(kb, q_aug[h], preferred_element_type=F32)
                s_ref[h, pl.ds(off, group), :] = s
                out.append(jnp.maximum(ms[h], slab_max(s)))
        return tuple(out)

    ms = lax.fori_loop(0, n_groups, score_group, tuple(m_init))

    m_fin = []
    for h in range(len(heads)):
        s_ref[h, pl.ds(own, blk), :] = sdiag_ref[h]
        m_fin.append(jnp.max(ms[h], axis=0, keepdims=True))

    def exp_group(g):
        off = pl.multiple_of(g * group, group)
        for h in range(len(heads)):
            p_ref[h] = jnp.exp(s_ref[h, pl.ds(off, group), :] - m_fin[h]).astype(BF16)

    def value_group(g):
        off = pl.multiple_of(g * group, group)
        for pp in range(n_pairs):
            vb = vt_ref[pp, :, pl.ds(off, group)]
            for hh in range(2):
                h = 2 * pp + hh
                acc_ref[h] += jnp.dot(vb, p_ref[h], preferred_element_type=F32)

    def pipelined(g, c):
        value_group(g - 1)
        exp_group(g)
        return c

    exp_group(0)
    lax.fori_loop(1, n_groups, pipelined, 0)
    value_group(n_groups - 1)

    half = lanes // 2
    outs = []
    for pp in range(n_pairs):
        outs.append(acc_ref[2 * pp, :half, :] / acc_ref[2 * pp, lanes:lanes + 1, :])
        outs.append(acc_ref[2 * pp + 1, half:lanes, :] / acc_ref[2 * pp + 1, lanes:lanes + 1, :])
    o_ref[0] = jnp.concatenate(outs, axis=0).T.astype(BF16)


def _moba(q, k, v):
    B, S, _ = q.shape
    nb = S // MOBA_BLOCK
    assert nb % MOBA_GROUP == 0
    nbp = -(-nb // (2 * V7X_SUBLANES)) * (2 * V7X_SUBLANES)
    n_pairs = ATT_WIDTH // V7X_LANES
    pp = MOBA_PAIRS
    assert n_pairs % pp == 0
    vt_rows = V7X_LANES + 2 * V7X_SUBLANES
    qspec = pl.BlockSpec((1, MOBA_BLOCK, pp * V7X_LANES), lambda b, p, i: (b, i, p))
    kvspec = pl.BlockSpec((1, S, pp * V7X_LANES), lambda b, p, i: (b, 0, p))
    return pl.pallas_call(
        functools.partial(_moba_kernel, nb=nb),
        grid=(B, n_pairs // pp, nb),
        in_specs=[qspec, kvspec, kvspec],
        out_specs=qspec,
        out_shape=jax.ShapeDtypeStruct((B, S, ATT_WIDTH), BF16),
        scratch_shapes=[pltpu.VMEM((pp, S, 2 * V7X_LANES), BF16),
                        pltpu.VMEM((pp, nbp, 3 * V7X_LANES), BF16),
                        pltpu.VMEM((pp, vt_rows, S), BF16),
                        pltpu.VMEM((2 * pp, vt_rows, MOBA_BLOCK), F32),
                        pltpu.VMEM((2 * pp, S, MOBA_BLOCK), F32),
                        pltpu.VMEM((2 * pp, MOBA_BLOCK, MOBA_BLOCK), F32),
                        pltpu.VMEM((2 * pp, MOBA_GROUP * MOBA_BLOCK, MOBA_BLOCK), BF16)],
        compiler_params=_cparams("parallel", "parallel", "arbitrary"),
        name="moba",
    )(q, k, v)


def _hgrn_level_sizes(chunk):
    return [chunk >> (i + 1) for i in range(int(math.log2(chunk)))]


def _hgrn_constants(chunk):
    t = np.arange(chunk)
    mats = [(t[None, :] <= t[:, None]),
            (t[None, :] > t[:, None])]
    qrows, pmasks = [], []
    for bs in _hgrn_level_sizes(chunk):
        blk = t // bs
        odd = (blk % 2) == 1
        lo, hi = blk * bs, (blk + 1) * bs
        u = t[None, :]
        m_odd = (u >= lo[:, None]) & (u <= t[:, None])
        m_even = (u > t[:, None]) & (u < hi[:, None])
        mats.append(np.where(odd[:, None], m_odd, m_even))
        qrows.append(odd)
        pmasks.append(odd[:, None] & (blk[None, :] == blk[:, None] - 1))
    pmasks.append(t[None, :] == t[:, None])
    summat = np.concatenate(mats, axis=0).astype(np.float32)
    qrow = np.stack(qrows, axis=0).astype(np.float32)
    pmask = np.stack(pmasks, axis=0).astype(np.float32)
    return summat, qrow, pmask


def _hgrn_kernel(q_ref, f_ref, i_ref, g_ref, lb_ref, nw_ref, sm_ref, qrow_ref, pm_ref,
                 o_ref, state_ref, *, rows):
    C = HGRN_CHUNK
    n_levels = qrow_ref.shape[0]

    @pl.when(pl.program_id(1) == 0)
    def _():
        state_ref[...] = jnp.zeros_like(state_ref)

    lb = lb_ref[...]
    summat = sm_ref[...]
    for c in range(rows // C):
        r0 = c * C
        fg = lb + (1.0 - lb) * jax.nn.sigmoid(f_ref[0, r0:r0 + C, :])
        logf = jnp.log(fg)
        hi = logf.astype(BF16)
        rem = logf - hi.astype(F32)
        mid = rem.astype(BF16)
        lo = (rem - mid.astype(F32)).astype(BF16)
        sums = jnp.dot(summat, jnp.concatenate([hi, mid, lo], axis=0),
                       preferred_element_type=F32)
        for h in range(HGRN_HEADS):
            ls = slice(h * HGRN_DIM, (h + 1) * HGRN_DIM)
            qf = jax.nn.silu(q_ref[0, r0:r0 + C, ls].astype(F32))
            kf = 1.0 - fg[:, ls]
            iv = i_ref[0, r0:r0 + C, ls]
            bcum = sums[0:C, ls]
            bsuf = sums[C:2 * C, ls]
            att = _nt_dot(qf.astype(BF16), kf.astype(BF16)) * pm_ref[n_levels]
            for lv in range(n_levels):
                w = jnp.exp(sums[(2 + lv) * C:(3 + lv) * C, ls])
                qrow = qrow_ref[lv]
                z = (jnp.where(qrow > 0.5, qf, kf) * w).astype(BF16)
                att = att + _nt_dot(z, z) * pm_ref[lv]
            o = jnp.dot(att.astype(BF16), iv, preferred_element_type=F32)
            st = state_ref[h]
            o = o + _nt_dot((qf * jnp.exp(bcum)).astype(BF16), st.astype(BF16))
            kdec = (kf * jnp.exp(bsuf)).astype(BF16)
            state_ref[h] = st * jnp.exp(bcum[C - 1:C, :]) + _tn_dot(iv, kdec)
            o = o * lax.rsqrt(jnp.mean(o * o, axis=-1, keepdims=True) + RMS_EPS)
            o = o * nw_ref[:, ls] * jax.nn.silu(g_ref[0, r0:r0 + C, ls].astype(F32))
            o_ref[0, r0:r0 + C, ls] = o.astype(BF16)


def _hgrn(qb, fb, ib, gb, lb, norm_w):
    B, S, W = qb.shape
    rows = min(HGRN_ROWS, S)
    summat, qrow, pmask = _hgrn_constants(HGRN_CHUNK)
    summat = jnp.asarray(np.concatenate([summat] * 3, axis=1), BF16)
    n_levels = qrow.shape[0]
    blk = pl.BlockSpec((1, rows, W), lambda b, s: (b, s, 0))
    vec = pl.BlockSpec((1, W), lambda b, s: (0, 0))
    const = lambda a: pl.BlockSpec(a.shape, lambda b, s: (0,) * a.ndim)
    qrow3 = qrow.reshape(n_levels, HGRN_CHUNK, 1)
    return pl.pallas_call(
        functools.partial(_hgrn_kernel, rows=rows),
        grid=(B, S // rows),
        in_specs=[blk, blk, blk, blk, vec, vec, const(summat), const(qrow3), const(pmask)],
        out_specs=blk,
        out_shape=jax.ShapeDtypeStruct((B, S, W), BF16),
        scratch_shapes=[pltpu.VMEM((HGRN_HEADS, HGRN_DIM, HGRN_DIM), F32)],
        compiler_params=_cparams("parallel", "arbitrary"),
        name="hgrn",
    )(qb, fb, ib, gb, lb, norm_w, jnp.asarray(summat), jnp.asarray(qrow3), jnp.asarray(pmask))


def _mix_route_kernel(ya_ref, yb_ref, ga_ref, gb_ref, x_ref, wa_ref, wb_ref, wo_ref, nw_ref,
                      rw_ref, rb_ref, tri_ref,
                      x1_ref, h2_ref, e_ref, w_ref, rank_ref, cnt_ref, carry_ref):
    @pl.when(pl.program_id(0) == 0)
    def _():
        carry_ref[...] = jnp.zeros_like(carry_ref)

    ua = jnp.dot(ya_ref[...], wa_ref[...], preferred_element_type=F32)
    ub = jnp.dot(yb_ref[...], wb_ref[...], preferred_element_type=F32)
    mixed = (jax.nn.sigmoid(ga_ref[...].astype(F32)) * ua
             + jax.nn.sigmoid(gb_ref[...].astype(F32)) * ub)
    x1 = x_ref[...] + jnp.dot(mixed.astype(BF16), wo_ref[...], preferred_element_type=F32)
    x1_ref[...] = x1
    h2 = x1 * lax.rsqrt(jnp.mean(x1 * x1, axis=-1, keepdims=True) + RMS_EPS) * nw_ref[...]
    h2_ref[...] = h2

    tm = x1.shape[0]
    logits = _nt_dot(rw_ref[...], h2, precision=lax.Precision.HIGHEST) + rb_ref[...]
    eid = lax.broadcasted_iota(jnp.int32, (N_EXPERTS, tm), 0)
    work = logits
    es, vs = [], []
    for _ in range(TOP_K):
        mx = jnp.max(work, axis=0, keepdims=True)
        idx = jnp.min(jnp.where(work == mx, eid, N_EXPERTS), axis=0, keepdims=True)
        es.append(idx)
        vs.append(mx)
        work = jnp.where(eid == idx, -jnp.inf, work)
    ex = [jnp.exp(v - vs[0]) for v in vs]
    den = ex[0] + ex[1] + ex[2] + ex[3]
    multi = jnp.zeros((N_EXPERTS, tm), F32)
    for k in range(TOP_K):
        multi = multi + jnp.where(eid == es[k], 1.0, 0.0)
    before = jnp.dot(multi.astype(BF16), tri_ref[...], preferred_element_type=F32) + carry_ref[...]
    for k in range(TOP_K):
        e_ref[k:k + 1, :] = es[k]
        w_ref[k:k + 1, :] = ex[k] / den
        rank_ref[k:k + 1, :] = jnp.sum(jnp.where(eid == es[k], before, 0.0), axis=0,
                                       keepdims=True).astype(jnp.int32)
    carry_ref[...] = carry_ref[...] + jnp.sum(multi, axis=1, keepdims=True)
    cnt_ref[...] = jnp.broadcast_to(carry_ref[...], cnt_ref.shape)


def _mix_route(ya, yb, ga, gb, x2, wa, wb, wo, norm_w, rw_t, rb, row0, rows):
    D = x2.shape[1]
    T = rows
    tm = min(MIX_ROWS, T)
    row = lambda w: pl.BlockSpec((tm, w), lambda i: (i + row0 // tm, 0))
    out_row = pl.BlockSpec((tm, D), lambda i: (i, 0))
    whole = pl.BlockSpec(memory_space=pltpu.VMEM)
    kt = pl.BlockSpec((TOP_K, tm), lambda i: (0, i))
    tri = jnp.asarray(np.triu(np.ones((tm, tm), np.float32), 1), BF16)
    return pl.pallas_call(
        _mix_route_kernel,
        grid=(T // tm,),
        in_specs=[row(ATT_WIDTH), row(HGRN_WIDTH), row(D), row(D), row(D),
                  whole, whole, whole, pl.BlockSpec((1, D), lambda i: (0, 0)),
                  whole, whole, whole],
        out_specs=[out_row, out_row, kt, kt, kt,
                   pl.BlockSpec((N_EXPERTS, V7X_LANES), lambda i: (0, 0))],
        out_shape=[jax.ShapeDtypeStruct((T, D), F32), jax.ShapeDtypeStruct((T, D), F32),
                   jax.ShapeDtypeStruct((TOP_K, T), jnp.int32),
                   jax.ShapeDtypeStruct((TOP_K, T), F32),
                   jax.ShapeDtypeStruct((TOP_K, T), jnp.int32),
                   jax.ShapeDtypeStruct((N_EXPERTS, V7X_LANES), F32)],
        scratch_shapes=[pltpu.VMEM((N_EXPERTS, 1), F32)],
        compiler_params=_cparams("arbitrary"),
        name="mix_route",
    )(ya, yb, ga, gb, x2, wa, wb, wo, norm_w, rw_t, rb, tri)


def _sc_worker_id():
    return lax.axis_index("s") * V7X_SC_CORES + lax.axis_index("c")


def _sc_kernel(body, out_rows, like, window, name):
    mesh = plsc.VectorSubcoreMesh(core_axis_name="c", subcore_axis_name="s")
    return pl.kernel(
        body, mesh=mesh,
        out_type=jax.ShapeDtypeStruct((out_rows,) + like.shape[1:], like.dtype),
        scratch_types=[pltpu.VMEM((window,), jnp.int32),
                       pltpu.VMEM((window,) + like.shape[1:], like.dtype)],
        name=name)


def _scatter_rows(dest, h2):
    T = dest.shape[1]
    n_workers = V7X_SC_CORES * V7X_SC_SUBCORES
    per_worker = T // n_workers
    window = min(SC_ROWS, per_worker)

    def body(dest_hbm, src_hbm, dst_hbm, idx_v, rows_v):
        base = _sc_worker_id() * per_worker

        @pl.loop(0, per_worker // window)
        def _(c):
            t0 = base + c * window
            pltpu.sync_copy(src_hbm.at[pl.ds(t0, window)], rows_v)
            for k in range(TOP_K):
                pltpu.sync_copy(dest_hbm.at[pl.ds(k * T + t0, window)], idx_v)
                pltpu.sync_copy(rows_v, dst_hbm.at[idx_v])

    return _sc_kernel(body, TOP_K * T, h2, window, "scatter_rows")(dest.reshape(-1), h2)


def _gather_rows(dest, yg):
    T = dest.shape[1]
    n_workers = V7X_SC_CORES * V7X_SC_SUBCORES
    per_worker = TOP_K * T // n_workers
    window = min(SC_ROWS, per_worker)

    def body(dest_hbm, src_hbm, dst_hbm, idx_v, rows_v):
        base = _sc_worker_id() * per_worker

        @pl.loop(0, per_worker // window)
        def _(c):
            r0 = base + c * window
            pltpu.sync_copy(dest_hbm.at[pl.ds(r0, window)], idx_v)
            pltpu.sync_copy(src_hbm.at[idx_v], rows_v)
            pltpu.sync_copy(rows_v, dst_hbm.at[pl.ds(r0, window)])

    return _sc_kernel(body, TOP_K * T, yg, window, "gather_rows")(dest.reshape(-1), yg)


def _prep_gate_up_kernel(w_ref, perm_ref, wg_ref, wu_ref):
    half = V7X_MXU_DIM // 2
    for g in range(w_ref.shape[2] // V7X_MXU_DIM):
        wb = w_ref[0, :, g * V7X_MXU_DIM:(g + 1) * V7X_MXU_DIM].astype(BF16)
        d = jnp.dot(wb, perm_ref[...], preferred_element_type=F32).astype(BF16)
        wg_ref[0, :, g * half:(g + 1) * half] = d[:, :half]
        wu_ref[0, :, g * half:(g + 1) * half] = d[:, half:]


def _prep_gate_up(w_gate_up):
    E, D, F2 = w_gate_up.shape
    rows = min(PREP_ROWS, D)
    i = np.arange(V7X_MXU_DIM)
    src = np.where(i < V7X_MXU_DIM // 2, 2 * i, 2 * (i - V7X_MXU_DIM // 2) + 1)
    perm = np.zeros((V7X_MXU_DIM, V7X_MXU_DIM), np.float32)
    perm[src, i] = 1.0
    out = pl.BlockSpec((1, rows, F2 // 2), lambda e, r: (e, r, 0))
    return pl.pallas_call(
        _prep_gate_up_kernel,
        grid=(E, D // rows),
        in_specs=[pl.BlockSpec((1, rows, F2), lambda e, r: (e, r, 0)),
                  pl.BlockSpec((V7X_MXU_DIM, V7X_MXU_DIM), lambda e, r: (0, 0))],
        out_specs=[out, out],
        out_shape=[jax.ShapeDtypeStruct((E, D, F2 // 2), BF16)] * 2,
        compiler_params=_cparams("parallel", "parallel"),
        name="prep_gate_up",
    )(w_gate_up, jnp.asarray(perm, BF16))


def _experts_kernel(blk_ref, exp_ref, lo_ref, hi_ref, first_ref,
                    x_ref, wg_ref, wu_ref, bg_ref, bu_ref, wd_ref, bd_ref, y_ref):
    w = pl.program_id(0)
    lo = lo_ref[w]
    hi = hi_ref[w]

    tm = x_ref.shape[0]
    n_out = V7X_MXU_DIM

    def ffn(merge):
        x = x_ref[...].astype(BF16)
        g = jnp.dot(x, wg_ref[0], preferred_element_type=F32) + bg_ref[0]
        u = jnp.dot(x, wu_ref[0], preferred_element_type=F32) + bu_ref[0]
        g = jnp.minimum(g, SWIGLU_LIMIT)
        u = jnp.clip(u, -SWIGLU_LIMIT, SWIGLU_LIMIT)
        act = ((u + 1.0) * (g * jax.nn.sigmoid(g * SWIGLU_ALPHA))).astype(BF16)
        for c in range(y_ref.shape[1] // n_out):
            cs = slice(c * n_out, (c + 1) * n_out)
            y = jnp.dot(act, wd_ref[0, :, cs], preferred_element_type=F32) + bd_ref[0, :, cs]
            y_ref[:, cs] = merge(y, cs)

    whole = hi - lo == tm
    r = lax.broadcasted_iota(jnp.int32, (tm, n_out), 0)
    mine = (r >= lo) & (r < hi)

    @pl.when(whole)
    def _():
        ffn(lambda y, cs: y)

    @pl.when((lo < hi) & jnp.logical_not(whole) & (first_ref[w] == 1))
    def _():
        ffn(lambda y, cs: jnp.where(mine, y, 0.0))

    @pl.when((lo < hi) & jnp.logical_not(whole) & (first_ref[w] == 0))
    def _():
        ffn(lambda y, cs: jnp.where(mine, y, y_ref[:, cs]))


def _work_items(counts, n_rows, tm):
    nblk = n_rows // tm
    n_items = nblk + N_EXPERTS - 1
    end = jnp.cumsum(counts)
    start = end - counts
    fb = start // tm
    nitems = jnp.where(counts > 0, (end - 1) // tm - fb + 1, 0)
    item_end = jnp.cumsum(nitems)
    item_start = item_end - nitems
    w = jnp.arange(n_items, dtype=jnp.int32)
    valid = w < item_end[-1]
    wc = jnp.minimum(w, item_end[-1] - 1)
    e = jnp.sum(wc[:, None] >= item_end[None, :], axis=1).astype(jnp.int32)
    e = jnp.minimum(e, N_EXPERTS - 1)
    onehot = e[:, None] == jnp.arange(N_EXPERTS, dtype=jnp.int32)[None, :]
    pick = lambda table: jnp.sum(jnp.where(onehot, table[None, :], 0), axis=1)
    blk = (pick(fb) + (wc - pick(item_start))).astype(jnp.int32)
    lo = jnp.maximum(pick(start), blk * tm) - blk * tm
    hi = jnp.minimum(pick(end), (blk + 1) * tm) - blk * tm
    lo = jnp.where(valid, lo, 0).astype(jnp.int32)
    hi = jnp.where(valid, hi, 0).astype(jnp.int32)
    first = jnp.concatenate([jnp.ones((1,), jnp.int32),
                             (blk[1:] != blk[:-1]).astype(jnp.int32)])
    return blk, e, lo, hi, first


def _experts(xg, counts, wg, wu, bg, bu, wd, bd):
    A, D = xg.shape
    tm = min(EXPERT_ROWS, A)
    F = wg.shape[2]
    items = _work_items(counts, A, tm)
    n_items = A // tm + N_EXPERTS - 1
    xs = pl.BlockSpec((tm, D), lambda w, blk, e, lo, hi, fi: (blk[w], 0))
    wsp = lambda r, c: pl.BlockSpec((1, r, c), lambda w, blk, e, lo, hi, fi: (e[w], 0, 0))
    return pl.pallas_call(
        _experts_kernel,
        grid_spec=pltpu.PrefetchScalarGridSpec(
            num_scalar_prefetch=5,
            grid=(n_items,),
            in_specs=[xs, wsp(D, F), wsp(D, F), wsp(1, F), wsp(1, F), wsp(F, D), wsp(1, D)],
            out_specs=xs,
        ),
        out_shape=jax.ShapeDtypeStruct(xg.shape, F32),
        compiler_params=_cparams("arbitrary"),
        name="experts",
    )(*items, xg, wg, wu, bg, bu, wd, bd)


def _combine_kernel(x1_ref, yk_ref, w_ref, nw_ref, *rest):
    o_ref = rest[-1]
    w = w_ref[...]
    x = x1_ref[...]
    for k in range(TOP_K):
        x = x + yk_ref[k] * w[:, k:k + 1]
    o_ref[...] = x * lax.rsqrt(jnp.mean(x * x, axis=-1, keepdims=True) + RMS_EPS) * nw_ref[...]


def _combine(x1, yk, w_tk, norm_w, out_rows, row0, prev_out):
    Tg, D = x1.shape
    tm = min(COMBINE_ROWS, Tg)
    row = pl.BlockSpec((tm, D), lambda i: (i, 0))
    in_specs = [row, pl.BlockSpec((TOP_K, tm, D), lambda i: (0, i, 0)),
                pl.BlockSpec((tm, TOP_K), lambda i: (i, 0)),
                pl.BlockSpec((1, D), lambda i: (0, 0))]
    args = [x1, yk, w_tk, norm_w]
    aliases = {}
    if prev_out is not None:
        in_specs.append(pl.BlockSpec(memory_space=pl.ANY))
        args.append(prev_out)
        aliases = {len(args) - 1: 0}
    return pl.pallas_call(
        _combine_kernel,
        grid=(Tg // tm,),
        in_specs=in_specs,
        out_specs=pl.BlockSpec((tm, D), lambda i: (i + row0 // tm, 0)),
        out_shape=jax.ShapeDtypeStruct((out_rows, D), F32),
        input_output_aliases=aliases,
        compiler_params=_cparams("parallel"),
        name="combine",
    )(*args)


def _qk_column_order():
    half = ATT_HEAD_DIM // 2
    order = []
    for p in range(ATT_HEADS // 2):
        for part in range(2):
            for h in (2 * p, 2 * p + 1):
                order.extend(range(h * ATT_HEAD_DIM + part * half, h * ATT_HEAD_DIM + (part + 1) * half))
    return np.asarray(order, np.int32)


def _rope_tables(seq):
    half = ATT_HEAD_DIM // 2
    inv = ROPE_THETA ** (-(jnp.arange(half, dtype=F32) * 2.0 / ATT_HEAD_DIM))
    ang = jnp.arange(seq, dtype=F32)[:, None] * inv[None, :]
    cos, sin = jnp.cos(ang), jnp.sin(ang)
    return (jnp.concatenate([cos, cos, cos, cos], axis=1),
            jnp.concatenate([-sin, -sin, sin, sin], axis=1))


def kernel(x, norm1_w, w_in, moba_up, hgrn_lb_logits, hgrn_norm_w, hgrn_up, w_out, norm2_w,
           router_w, router_b, w_gate_up, b_gate_up, w_down, b_down, final_norm_w):
    B, S, D = x.shape
    T = B * S
    assert S % MOBA_BLOCK == 0 and w_in.shape[0] == 1
    x2 = x.reshape(T, D)

    perm = _qk_column_order()
    w0 = w_in[0]
    w_in_p = jnp.concatenate([w0[:, :ATT_WIDTH][:, perm], w0[:, ATT_WIDTH:2 * ATT_WIDTH][:, perm],
                              w0[:, 2 * ATT_WIDTH:]], axis=1).astype(BF16)
    cos_t, sin_t = _rope_tables(S)
    lb = jnp.cumsum(jax.nn.softmax(hgrn_lb_logits.astype(F32), axis=0), axis=0)[0:1]

    qa, ka, va, qb, fb, ib, gb, ga, gtb = _in_proj(x2, norm1_w, w_in_p, cos_t, sin_t, S)
    r3 = lambda a: a.reshape(B, S, a.shape[1])
    ya = _moba(r3(qa), r3(ka), r3(va)).reshape(T, ATT_WIDTH)
    yb = _hgrn(r3(qb), r3(fb), r3(ib), r3(gb), lb, hgrn_norm_w).reshape(T, HGRN_WIDTH)

    mix_w = (moba_up[0].astype(BF16), hgrn_up[0].astype(BF16), w_out[0].astype(BF16), norm2_w,
             router_w[0].T, router_b[0][:, None])
    wg, wu = _prep_gate_up(w_gate_up[0])
    expert_w = (wg, wu, b_gate_up[0][:, None, 0::2], b_gate_up[0][:, None, 1::2],
                w_down[0].astype(BF16), b_down[0][:, None, :])

    Tg = T // MOE_TOKEN_GROUPS
    out = None
    for g in range(MOE_TOKEN_GROUPS):
        x1, h2, top_e, top_w, rank, cnt = _mix_route(ya, yb, ga, gtb, x2, *mix_w, g * Tg, Tg)
        counts = cnt[:, 0].astype(jnp.int32)
        start = jnp.cumsum(counts) - counts
        dest = rank + jnp.sum(jnp.where(top_e[:, :, None] == jnp.arange(N_EXPERTS)[None, None, :],
                                        start[None, None, :], 0), axis=-1)
        xg = _scatter_rows(dest, h2)
        yg = _experts(xg, counts, *expert_w)
        yk = _gather_rows(dest, yg).reshape(TOP_K, Tg, D)
        out = _combine(x1, yk, top_w.T, final_norm_w[None, :], T, g * Tg, out)
    return out.reshape(B, S, D)
```

```python
import functools
import math

import numpy as np
import jax
import jax.numpy as jnp
from jax import lax
from jax.experimental import pallas as pl
from jax.experimental.pallas import tpu as pltpu
from jax.experimental.pallas import tpu_sc as plsc

ATT_HEADS = 8
ATT_HEAD_DIM = 64
ATT_WIDTH = ATT_HEADS * ATT_HEAD_DIM
MOBA_BLOCK = 256
MOBA_TOPK = 3
ROPE_THETA = 10000.0
HGRN_HEADS = 4
HGRN_DIM = 128
HGRN_WIDTH = HGRN_HEADS * HGRN_DIM
HGRN_CHUNK = 64
N_EXPERTS = 32
TOP_K = 4
SWIGLU_LIMIT = 7.0
SWIGLU_ALPHA = 1.702
RMS_EPS = 1e-6
NEG = -1e30

V7X_LANES = 128
V7X_SUBLANES = 8
V7X_MXU_DIM = 256
V7X_VMEM_LIMIT_BYTES = 56 * 1024 * 1024
V7X_SC_CORES = 2
V7X_SC_SUBCORES = 16

MOBA_PAIRS = 2
MOBA_GROUP = 2
PROJ_ROWS = 512
PROJ_COLS = 512
HGRN_ROWS = 256
MIX_ROWS = 512
EXPERT_ROWS = 256
MOE_TOKEN_GROUPS = 2
SC_ROWS = 32
COMBINE_ROWS = 256
PREP_ROWS = 256

F32 = jnp.float32
BF16 = jnp.bfloat16


def _nt_dot(a, b, precision=None):
    return lax.dot_general(a, b, (((1,), (1,)), ((), ())), precision=precision,
                           preferred_element_type=F32)


def _tn_dot(a, b, precision=None):
    return lax.dot_general(a, b, (((0,), (0,)), ((), ())), precision=precision,
                           preferred_element_type=F32)


def _cparams(*sem):
    return pltpu.CompilerParams(dimension_semantics=sem, vmem_limit_bytes=V7X_VMEM_LIMIT_BYTES)


def _in_proj_kernel(x_ref, nw_ref, w_ref, cos_ref, sin_ref,
                    qa_ref, ka_ref, va_ref, qb_ref, fb_ref, ib_ref, gb_ref, ga_ref, gtb_ref):
    x = x_ref[...]
    h = x * lax.rsqrt(jnp.mean(x * x, axis=-1, keepdims=True) + RMS_EPS) * nw_ref[...]
    h = h.astype(BF16)
    cos = cos_ref[...]
    sin = sin_ref[...]

    def proj(c):
        return jnp.dot(h, w_ref[:, c * PROJ_COLS:(c + 1) * PROJ_COLS], preferred_element_type=F32)

    def rope(t):
        out = []
        for j in range(PROJ_COLS // V7X_LANES):
            tj = t[:, j * V7X_LANES:(j + 1) * V7X_LANES]
            out.append(tj * cos + pltpu.roll(tj, V7X_LANES // 2, 1) * sin)
        return jnp.concatenate(out, axis=1)

    qa_ref[...] = (rope(proj(0)) * (ATT_HEAD_DIM ** -0.5)).astype(BF16)
    ka_ref[...] = rope(proj(1)).astype(BF16)
    va_ref[...] = proj(2).astype(BF16)
    qb_ref[...] = proj(3).astype(BF16)
    fb_ref[...] = proj(4)
    ib_ref[...] = proj(5).astype(BF16)
    gb_ref[...] = proj(6).astype(BF16)
    ga_ref[:, :PROJ_COLS] = proj(7).astype(BF16)
    ga_ref[:, PROJ_COLS:] = proj(8).astype(BF16)
    gtb_ref[:, :PROJ_COLS] = proj(9).astype(BF16)
    gtb_ref[:, PROJ_COLS:] = proj(10).astype(BF16)


def _in_proj(x2, norm_w, w_in_bf16, cos_t, sin_t, seq):
    T, D = x2.shape
    tm = min(PROJ_ROWS, seq)
    n_seq_tiles = seq // tm
    row = lambda w: pl.BlockSpec((tm, w), lambda i: (i, 0))
    tab = pl.BlockSpec((tm, V7X_LANES), lambda i: (i % n_seq_tiles, 0))
    widths = [ATT_WIDTH] * 3 + [HGRN_WIDTH] * 4 + [D, D]
    dtypes = [BF16, BF16, BF16, BF16, F32, BF16, BF16, BF16, BF16]
    return pl.pallas_call(
        _in_proj_kernel,
        grid=(T // tm,),
        in_specs=[row(D), pl.BlockSpec((1, D), lambda i: (0, 0)),
                  pl.BlockSpec(memory_space=pltpu.VMEM), tab, tab],
        out_specs=[row(w) for w in widths],
        out_shape=[jax.ShapeDtypeStruct((T, w), dt) for w, dt in zip(widths, dtypes)],
        compiler_params=_cparams("parallel"),
        name="in_proj",
    )(x2, norm_w, w_in_bf16, cos_t, sin_t)


def _moba_kernel(q_ref, k_ref, v_ref, o_ref, kaug_ref, kmean_ref, vt_ref, acc_ref, s_ref,
                 sdiag_ref, p_ref, *, nb):
    qi = pl.program_id(2)
    blk = MOBA_BLOCK
    lanes = V7X_LANES
    n_pairs = kaug_ref.shape[0]
    heads = [(pp, hh) for pp in range(n_pairs) for hh in range(2)]
    nbp = kmean_ref.shape[1]

    @pl.when(qi == 0)
    def _():
        rowb = lax.broadcasted_iota(jnp.int32, (nb * blk, lanes), 0) // blk
        col = lax.broadcasted_iota(jnp.int32, (nb * blk, lanes), 1)
        onehot = jnp.where(rowb == col, 1.0, 0.0).astype(BF16)
        for pp in range(n_pairs):
            pl_ = slice(pp * lanes, (pp + 1) * lanes)
            kaug_ref[pp, :, :lanes] = k_ref[0, :, pl_]
            kaug_ref[pp, :, lanes:] = onehot
            means = [jnp.sum(k_ref[0, n * blk:(n + 1) * blk, pl_].astype(F32), axis=0,
                             keepdims=True) * (1.0 / blk) for n in range(nb)]
            km = jnp.concatenate(means + [jnp.zeros((nbp - nb, lanes), F32)] * (nbp > nb), axis=0)
            hi = km.astype(BF16)
            rem = km - hi.astype(F32)
            mid = rem.astype(BF16)
            lo = (rem - mid.astype(F32)).astype(BF16)
            kmean_ref[pp] = jnp.concatenate([hi, mid, lo], axis=1)
            for c in range(nb):
                vt = v_ref[0, c * blk:(c + 1) * blk, pl_].astype(F32).T
                vt_ref[pp, :lanes, c * blk:(c + 1) * blk] = vt.astype(BF16)
            vt_ref[pp, lanes:, :] = jnp.ones((vt_ref.shape[1] - lanes, nb * blk), BF16)

    feat = lax.broadcasted_iota(jnp.int32, (lanes, blk), 0)
    key_i = lax.broadcasted_iota(jnp.int32, (blk, blk), 0)
    qry_i = lax.broadcasted_iota(jnp.int32, (blk, blk), 1)
    own = pl.multiple_of(qi * blk, blk)
    blk_id = lax.broadcasted_iota(jnp.int32, (nbp, 2 * blk), 0)
    slab = 2 * V7X_SUBLANES

    def slab_max(s):
        return jnp.max(s.reshape(s.shape[0] // slab, slab, blk), axis=0)

    q_aug, m_init = [], []
    for pp in range(n_pairs):
        qt = q_ref[0, :, pp * lanes:(pp + 1) * lanes].astype(F32).T
        k_own = kaug_ref[pp, pl.ds(own, blk), :lanes]
        qhs = [jnp.where((feat // (ATT_HEAD_DIM // 2)) % 2 == hh, qt, 0.0).astype(BF16)
               for hh in range(2)]
        q2 = jnp.concatenate(qhs, axis=1)
        gate = jnp.dot(kmean_ref[pp], jnp.concatenate([q2, q2, q2], axis=0),
                       preferred_element_type=F32)
        gate = jnp.where(blk_id < qi, gate, NEG)
        beaten = jnp.zeros((nbp, 2 * blk), F32)
        for n in range(nb):
            gn = gate[n:n + 1, :]
            wins = (gn > gate) | ((gn == gate) & (blk_id > n))
            beaten = beaten + jnp.where(wins, 1.0, 0.0)
        sel = (beaten < MOBA_TOPK) & (blk_id < qi)
        bias = jnp.where(sel, 0.0, NEG).astype(BF16)
        bias = jnp.concatenate([bias, jnp.zeros((lanes - nbp, 2 * blk), BF16)], axis=0)
        for hh in range(2):
            h = 2 * pp + hh
            q_aug.append(jnp.concatenate([qhs[hh], bias[:, hh * blk:(hh + 1) * blk]], axis=0))
            s = jnp.dot(k_own, qhs[hh], preferred_element_type=F32)
            s = jnp.where(key_i <= qry_i, s, NEG)
            sdiag_ref[h] = s
            m_init.append(slab_max(s))
            acc_ref[h] = jnp.zeros(acc_ref.shape[1:], F32)

    group = MOBA_GROUP * blk
    n_groups = (qi + MOBA_GROUP) // MOBA_GROUP

    def score_group(g, ms):
        off = pl.multiple_of(g * group, group)
        out = []
        for pp in range(n_pairs):
            kb = kaug_ref[pp, pl.ds(off, group), :]
            for hh in range(2):
                h = 2 * pp + hh
                s = jnp.dot(kb, q_aug[h], preferred_element_type=F32)
                s_ref[h, pl.ds(off, group), :] = s
                out.append(jnp.maximum(ms[h], slab_max(s)))
        return tuple(out)

    ms = lax.fori_loop(0, n_groups, score_group, tuple(m_init))

    m_fin = []
    for h in range(len(heads)):
        s_ref[h, pl.ds(own, blk), :] = sdiag_ref[h]
        m_fin.append(jnp.max(ms[h], axis=0, keepdims=True))

    def exp_group(g):
        off = pl.multiple_of(g * group, group)
        for h in range(len(heads)):
            p_ref[h] = jnp.exp(s_ref[h, pl.ds(off, group), :] - m_fin[h]).astype(BF16)

    def value_group(g):
        off = pl.multiple_of(g * group, group)
        for pp in range(n_pairs):
            vb = vt_ref[pp, :, pl.ds(off, group)]
            for hh in range(2):
                h = 2 * pp + hh
                acc_ref[h] += jnp.dot(vb, p_ref[h], preferred_element_type=F32)

    def pipelined(g, c):
        value_group(g - 1)
        exp_group(g)
        return c

    exp_group(0)
    lax.fori_loop(1, n_groups, pipelined, 0)
    value_group(n_groups - 1)

    half = lanes // 2
    outs = []
    for pp in range(n_pairs):
        outs.append(acc_ref[2 * pp, :half, :] / acc_ref[2 * pp, lanes:lanes + 1, :])
        outs.append(acc_ref[2 * pp + 1, half:lanes, :] / acc_ref[2 * pp + 1, lanes:lanes + 1, :])
    o_ref[0] = jnp.concatenate(outs, axis=0).T.astype(BF16)


def _moba(q, k, v):
    B, S, _ = q.shape
    nb = S // MOBA_BLOCK
    assert nb % MOBA_GROUP == 0
    nbp = -(-nb // (2 * V7X_SUBLANES)) * (2 * V7X_SUBLANES)
    n_pairs = ATT_WIDTH // V7X_LANES
    pp = MOBA_PAIRS
    assert n_pairs % pp == 0
    vt_rows = V7X_LANES + 2 * V7X_SUBLANES
    qspec = pl.BlockSpec((1, MOBA_BLOCK, pp * V7X_LANES), lambda b, p, i: (b, i, p))
    kvspec = pl.BlockSpec((1, S, pp * V7X_LANES), lambda b, p, i: (b, 0, p))
    return pl.pallas_call(
        functools.partial(_moba_kernel, nb=nb),
        grid=(B, n_pairs // pp, nb),
        in_specs=[qspec, kvspec, kvspec],
        out_specs=qspec,
        out_shape=jax.ShapeDtypeStruct((B, S, ATT_WIDTH), BF16),
        scratch_shapes=[pltpu.VMEM((pp, S, 2 * V7X_LANES), BF16),
                        pltpu.VMEM((pp, nbp, 3 * V7X_LANES), BF16),
                        pltpu.VMEM((pp, vt_rows, S), BF16),
                        pltpu.VMEM((2 * pp, vt_rows, MOBA_BLOCK), F32),
                        pltpu.VMEM((2 * pp, S, MOBA_BLOCK), F32),
                        pltpu.VMEM((2 * pp, MOBA_BLOCK, MOBA_BLOCK), F32),
                        pltpu.VMEM((2 * pp, MOBA_GROUP * MOBA_BLOCK, MOBA_BLOCK), BF16)],
        compiler_params=_cparams("parallel", "parallel", "arbitrary"),
        name="moba",
    )(q, k, v)


def _hgrn_level_sizes(chunk):
    return [chunk >> (i + 1) for i in range(int(math.log2(chunk)))]


def _hgrn_constants(chunk):
    t = np.arange(chunk)
    mats = [(t[None, :] <= t[:, None]),
            (t[None, :] > t[:, None])]
    qrows, pmasks = [], []
    for bs in _hgrn_level_sizes(chunk):
        blk = t // bs
        odd = (blk % 2) == 1
        lo, hi = blk * bs, (blk + 1) * bs
        u = t[None, :]
        m_odd = (u >= lo[:, None]) & (u <= t[:, None])
        m_even = (u > t[:, None]) & (u < hi[:, None])
        mats.append(np.where(odd[:, None], m_odd, m_even))
        qrows.append(odd)
        pmasks.append(odd[:, None] & (blk[None, :] == blk[:, None] - 1))
    pmasks.append(t[None, :] == t[:, None])
    summat = np.concatenate(mats, axis=0).astype(np.float32)
    qrow = np.stack(qrows, axis=0).astype(np.float32)
    pmask = np.stack(pmasks, axis=0).astype(np.float32)
    return summat, qrow, pmask


def _hgrn_kernel(q_ref, f_ref, i_ref, g_ref, lb_ref, nw_ref, sm_ref, qrow_ref, pm_ref,
                 o_ref, state_ref, *, rows):
    C = HGRN_CHUNK
    n_levels = qrow_ref.shape[0]

    @pl.when(pl.program_id(1) == 0)
    def _():
        state_ref[...] = jnp.zeros_like(state_ref)

    lb = lb_ref[...]
    summat = sm_ref[...]
    for c in range(rows // C):
        r0 = c * C
        fg = lb + (1.0 - lb) * jax.nn.sigmoid(f_ref[0, r0:r0 + C, :])
        logf = jnp.log(fg)
        hi = logf.astype(BF16)
        rem = logf - hi.astype(F32)
        mid = rem.astype(BF16)
        lo = (rem - mid.astype(F32)).astype(BF16)
        sums = jnp.dot(summat, jnp.concatenate([hi, mid, lo], axis=0),
                       preferred_element_type=F32)
        for h in range(HGRN_HEADS):
            ls = slice(h * HGRN_DIM, (h + 1) * HGRN_DIM)
            qf = jax.nn.silu(q_ref[0, r0:r0 + C, ls].astype(F32))
            kf = 1.0 - fg[:, ls]
            iv = i_ref[0, r0:r0 + C, ls]
            bcum = sums[0:C, ls]
            bsuf = sums[C:2 * C, ls]
            att = _nt_dot(qf.astype(BF16), kf.astype(BF16)) * pm_ref[n_levels]
            for lv in range(n_levels):
                w = jnp.exp(sums[(2 + lv) * C:(3 + lv) * C, ls])
                qrow = qrow_ref[lv]
                z = (jnp.where(qrow > 0.5, qf, kf) * w).astype(BF16)
                att = att + _nt_dot(z, z) * pm_ref[lv]
            o = jnp.dot(att.astype(BF16), iv, preferred_element_type=F32)
            st = state_ref[h]
            o = o + _nt_dot((qf * jnp.exp(bcum)).astype(BF16), st.astype(BF16))
            kdec = (kf * jnp.exp(bsuf)).astype(BF16)
            state_ref[h] = st * jnp.exp(bcum[C - 1:C, :]) + _tn_dot(iv, kdec)
            o = o * lax.rsqrt(jnp.mean(o * o, axis=-1, keepdims=True) + RMS_EPS)
            o = o * nw_ref[:, ls] * jax.nn.silu(g_ref[0, r0:r0 + C, ls].astype(F32))
            o_ref[0, r0:r0 + C, ls] = o.astype(BF16)


def _hgrn(qb, fb, ib, gb, lb, norm_w):
    B, S, W = qb.shape
    rows = min(HGRN_ROWS, S)
    summat, qrow, pmask = _hgrn_constants(HGRN_CHUNK)
    summat = jnp.asarray(np.concatenate([summat] * 3, axis=1), BF16)
    n_levels = qrow.shape[0]
    blk = pl.BlockSpec((1, rows, W), lambda b, s: (b, s, 0))
    vec = pl.BlockSpec((1, W), lambda b, s: (0, 0))
    const = lambda a: pl.BlockSpec(a.shape, lambda b, s: (0,) * a.ndim)
    qrow3 = qrow.reshape(n_levels, HGRN_CHUNK, 1)
    return pl.pallas_call(
        functools.partial(_hgrn_kernel, rows=rows),
        grid=(B, S // rows),
        in_specs=[blk, blk, blk, blk, vec, vec, const(summat), const(qrow3), const(pmask)],
        out_specs=blk,
        out_shape=jax.ShapeDtypeStruct((B, S, W), BF16),
        scratch_shapes=[pltpu.VMEM((HGRN_HEADS, HGRN_DIM, HGRN_DIM), F32)],
        compiler_params=_cparams("parallel", "arbitrary"),
        name="hgrn",
    )(qb, fb, ib, gb, lb, norm_w, jnp.asarray(summat), jnp.asarray(qrow3), jnp.asarray(pmask))


def _mix_route_kernel(ya_ref, yb_ref, ga_ref, gb_ref, x_ref, wa_ref, wb_ref, wo_ref, nw_ref,
                      rw_ref, rb_ref, tri_ref,
                      x1_ref, h2_ref, e_ref, w_ref, rank_ref, cnt_ref, carry_ref):
    @pl.when(pl.program_id(0) == 0)
    def _():
        carry_ref[...] = jnp.zeros_like(carry_ref)

    ua = jnp.dot(ya_ref[...], wa_ref[...], preferred_element_type=F32)
    ub = jnp.dot(yb_ref[...], wb_ref[...], preferred_element_type=F32)
    mixed = (jax.nn.sigmoid(ga_ref[...].astype(F32)) * ua
             + jax.nn.sigmoid(gb_ref[...].astype(F32)) * ub)
    x1 = x_ref[...] + jnp.dot(mixed.astype(BF16), wo_ref[...], preferred_element_type=F32)
    x1_ref[...] = x1
    h2 = x1 * lax.rsqrt(jnp.mean(x1 * x1, axis=-1, keepdims=True) + RMS_EPS) * nw_ref[...]
    h2_ref[...] = h2

    tm = x1.shape[0]
    logits = _nt_dot(rw_ref[...], h2, precision=lax.Precision.HIGHEST) + rb_ref[...]
    eid = lax.broadcasted_iota(jnp.int32, (N_EXPERTS, tm), 0)
    work = logits
    es, vs = [], []
    for _ in range(TOP_K):
        mx = jnp.max(work, axis=0, keepdims=True)
        idx = jnp.min(jnp.where(work == mx, eid, N_EXPERTS), axis=0, keepdims=True)
        es.append(idx)
        vs.append(mx)
        work = jnp.where(eid == idx, -jnp.inf, work)
    ex = [jnp.exp(v - vs[0]) for v in vs]
    den = ex[0] + ex[1] + ex[2] + ex[3]
    multi = jnp.zeros((N_EXPERTS, tm), F32)
    for k in range(TOP_K):
        multi = multi + jnp.where(eid == es[k], 1.0, 0.0)
    before = jnp.dot(multi.astype(BF16), tri_ref[...], preferred_element_type=F32) + carry_ref[...]
    for k in range(TOP_K):
        e_ref[k:k + 1, :] = es[k]
        w_ref[k:k + 1, :] = ex[k] / den
        rank_ref[k:k + 1, :] = jnp.sum(jnp.where(eid == es[k], before, 0.0), axis=0,
                                       keepdims=True).astype(jnp.int32)
    carry_ref[...] = carry_ref[...] + jnp.sum(multi, axis=1, keepdims=True)
    cnt_ref[...] = jnp.broadcast_to(carry_ref[...], cnt_ref.shape)


def _mix_route(ya, yb, ga, gb, x2, wa, wb, wo, norm_w, rw_t, rb, row0, rows):
    D = x2.shape[1]
    T = rows
    tm = min(MIX_ROWS, T)
    row = lambda w: pl.BlockSpec((tm, w), lambda i: (i + row0 // tm, 0))
    out_row = pl.BlockSpec((tm, D), lambda i: (i, 0))
    whole = pl.BlockSpec(memory_space=pltpu.VMEM)
    kt = pl.BlockSpec((TOP_K, tm), lambda i: (0, i))
    tri = jnp.asarray(np.triu(np.ones((tm, tm), np.float32), 1), BF16)
    return pl.pallas_call(
        _mix_route_kernel,
        grid=(T // tm,),
        in_specs=[row(ATT_WIDTH), row(HGRN_WIDTH), row(D), row(D), row(D),
                  whole, whole, whole, pl.BlockSpec((1, D), lambda i: (0, 0)),
                  whole, whole, whole],
        out_specs=[out_row, out_row, kt, kt, kt,
                   pl.BlockSpec((N_EXPERTS, V7X_LANES), lambda i: (0, 0))],
        out_shape=[jax.ShapeDtypeStruct((T, D), F32), jax.ShapeDtypeStruct((T, D), F32),
                   jax.ShapeDtypeStruct((TOP_K, T), jnp.int32),
                   jax.ShapeDtypeStruct((TOP_K, T), F32),
                   jax.ShapeDtypeStruct((TOP_K, T), jnp.int32),
                   jax.ShapeDtypeStruct((N_EXPERTS, V7X_LANES), F32)],
        scratch_shapes=[pltpu.VMEM((N_EXPERTS, 1), F32)],
        compiler_params=_cparams("arbitrary"),
        name="mix_route",
    )(ya, yb, ga, gb, x2, wa, wb, wo, norm_w, rw_t, rb, tri)


def _sc_worker_id():
    return lax.axis_index("s") * V7X_SC_CORES + lax.axis_index("c")


def _sc_kernel(body, out_rows, like, window, name):
    mesh = plsc.VectorSubcoreMesh(core_axis_name="c", subcore_axis_name="s")
    return pl.kernel(
        body, mesh=mesh,
        out_type=jax.ShapeDtypeStruct((out_rows,) + like.shape[1:], like.dtype),
        scratch_types=[pltpu.VMEM((window,), jnp.int32),
                       pltpu.VMEM((window,) + like.shape[1:], like.dtype)],
        name=name)


def _scatter_rows(dest, h2):
    T = dest.shape[1]
    n_workers = V7X_SC_CORES * V7X_SC_SUBCORES
    per_worker = T // n_workers
    window = min(SC_ROWS, per_worker)

    def body(dest_hbm, src_hbm, dst_hbm, idx_v, rows_v):
        base = _sc_worker_id() * per_worker

        @pl.loop(0, per_worker // window)
        def _(c):
            t0 = base + c * window
            pltpu.sync_copy(src_hbm.at[pl.ds(t0, window)], rows_v)
            for k in range(TOP_K):
                pltpu.sync_copy(dest_hbm.at[pl.ds(k * T + t0, window)], idx_v)
                pltpu.sync_copy(rows_v, dst_hbm.at[idx_v])

    return _sc_kernel(body, TOP_K * T, h2, window, "scatter_rows")(dest.reshape(-1), h2)


def _gather_rows(dest, yg):
    T = dest.shape[1]
    n_workers = V7X_SC_CORES * V7X_SC_SUBCORES
    per_worker = TOP_K * T // n_workers
    window = min(SC_ROWS, per_worker)

    def body(dest_hbm, src_hbm, dst_hbm, idx_v, rows_v):
        base = _sc_worker_id() * per_worker

        @pl.loop(0, per_worker // window)
        def _(c):
            r0 = base + c * window
            pltpu.sync_copy(dest_hbm.at[pl.ds(r0, window)], idx_v)
            pltpu.sync_copy(src_hbm.at[idx_v], rows_v)
            pltpu.sync_copy(rows_v, dst_hbm.at[pl.ds(r0, window)])

    return _sc_kernel(body, TOP_K * T, yg, window, "gather_rows")(dest.reshape(-1), yg)


def _prep_gate_up_kernel(w_ref, perm_ref, wg_ref, wu_ref):
    half = V7X_MXU_DIM // 2
    for g in range(w_ref.shape[2] // V7X_MXU_DIM):
        wb = w_ref[0, :, g * V7X_MXU_DIM:(g + 1) * V7X_MXU_DIM].astype(BF16)
        d = jnp.dot(wb, perm_ref[...], preferred_element_type=F32).astype(BF16)
        wg_ref[0, :, g * half:(g + 1) * half] = d[:, :half]
        wu_ref[0, :, g * half:(g + 1) * half] = d[:, half:]


def _prep_gate_up(w_gate_up):
    E, D, F2 = w_gate_up.shape
    rows = min(PREP_ROWS, D)
    i = np.arange(V7X_MXU_DIM)
    src = np.where(i < V7X_MXU_DIM // 2, 2 * i, 2 * (i - V7X_MXU_DIM // 2) + 1)
    perm = np.zeros((V7X_MXU_DIM, V7X_MXU_DIM), np.float32)
    perm[src, i] = 1.0
    out = pl.BlockSpec((1, rows, F2 // 2), lambda e, r: (e, r, 0))
    return pl.pallas_call(
        _prep_gate_up_kernel,
        grid=(E, D // rows),
        in_specs=[pl.BlockSpec((1, rows, F2), lambda e, r: (e, r, 0)),
                  pl.BlockSpec((V7X_MXU_DIM, V7X_MXU_DIM), lambda e, r: (0, 0))],
        out_specs=[out, out],
        out_shape=[jax.ShapeDtypeStruct((E, D, F2 // 2), BF16)] * 2,
        compiler_params=_cparams("parallel", "parallel"),
        name="prep_gate_up",
    )(w_gate_up, jnp.asarray(perm, BF16))


def _experts_kernel(blk_ref, exp_ref, lo_ref, hi_ref, first_ref,
                    x_ref, wg_ref, wu_ref, bg_ref, bu_ref, wd_ref, bd_ref, y_ref):
    w = pl.program_id(0)
    lo = lo_ref[w]
    hi = hi_ref[w]

    tm = x_ref.shape[0]
    n_out = V7X_MXU_DIM

    def ffn(merge):
        x = x_ref[...].astype(BF16)
        g = jnp.dot(x, wg_ref[0], preferred_element_type=F32) + bg_ref[0]
        u = jnp.dot(x, wu_ref[0], preferred_element_type=F32) + bu_ref[0]
        g = jnp.minimum(g, SWIGLU_LIMIT)
        u = jnp.clip(u, -SWIGLU_LIMIT, SWIGLU_LIMIT)
        act = ((u + 1.0) * (g * jax.nn.sigmoid(g * SWIGLU_ALPHA))).astype(BF16)
        for c in range(y_ref.shape[1] // n_out):
            cs = slice(c * n_out, (c + 1) * n_out)
            y = jnp.dot(act, wd_ref[0, :, cs], preferred_element_type=F32) + bd_ref[0, :, cs]
            y_ref[:, cs] = merge(y, cs)

    whole = hi - lo == tm
    r = lax.broadcasted_iota(jnp.int32, (tm, n_out), 0)
    mine = (r >= lo) & (r < hi)

    @pl.when(whole)
    def _():
        ffn(lambda y, cs: y)

    @pl.when((lo < hi) & jnp.logical_not(whole) & (first_ref[w] == 1))
    def _():
        ffn(lambda y, cs: jnp.where(mine, y, 0.0))

    @pl.when((lo < hi) & jnp.logical_not(whole) & (first_ref[w] == 0))
    def _():
        ffn(lambda y, cs: jnp.where(mine, y, y_ref[:, cs]))


def _work_items(counts, n_rows, tm):
    nblk = n_rows // tm
    n_items = nblk + N_EXPERTS - 1
    end = jnp.cumsum(counts)
    start = end - counts
    fb = start // tm
    nitems = jnp.where(counts > 0, (end - 1) // tm - fb + 1, 0)
    item_end = jnp.cumsum(nitems)
    item_start = item_end - nitems
    w = jnp.arange(n_items, dtype=jnp.int32)
    valid = w < item_end[-1]
    wc = jnp.minimum(w, item_end[-1] - 1)
    e = jnp.sum(wc[:, None] >= item_end[None, :], axis=1).astype(jnp.int32)
    e = jnp.minimum(e, N_EXPERTS - 1)
    onehot = e[:, None] == jnp.arange(N_EXPERTS, dtype=jnp.int32)[None, :]
    pick = lambda table: jnp.sum(jnp.where(onehot, table[None, :], 0), axis=1)
    blk = (pick(fb) + (wc - pick(item_start))).astype(jnp.int32)
    lo = jnp.maximum(pick(start), blk * tm) - blk * tm
    hi = jnp.minimum(pick(end), (blk + 1) * tm) - blk * tm
    lo = jnp.where(valid, lo, 0).astype(jnp.int32)
    hi = jnp.where(valid, hi, 0).astype(jnp.int32)
    first = jnp.concatenate([jnp.ones((1,), jnp.int32),
                             (blk[1:] != blk[:-1]).astype(jnp.int32)])
    return blk, e, lo, hi, first


def _experts(xg, counts, wg, wu, bg, bu, wd, bd):
    A, D = xg.shape
    tm = min(EXPERT_ROWS, A)
    F = wg.shape[2]
    items = _work_items(counts, A, tm)
    n_items = A // tm + N_EXPERTS - 1
    xs = pl.BlockSpec((tm, D), lambda w, blk, e, lo, hi, fi: (blk[w], 0))
    wsp = lambda r, c: pl.BlockSpec((1, r, c), lambda w, blk, e, lo, hi, fi: (e[w], 0, 0))
    return pl.pallas_call(
        _experts_kernel,
        grid_spec=pltpu.PrefetchScalarGridSpec(
            num_scalar_prefetch=5,
            grid=(n_items,),
            in_specs=[xs, wsp(D, F), wsp(D, F), wsp(1, F), wsp(1, F), wsp(F, D), wsp(1, D)],
            out_specs=xs,
        ),
        out_shape=jax.ShapeDtypeStruct(xg.shape, F32),
        compiler_params=_cparams("arbitrary"),
        name="experts",
    )(*items, xg, wg, wu, bg, bu, wd, bd)


def _combine_kernel(x1_ref, yk_ref, w_ref, nw_ref, *rest):
    o_ref = rest[-1]
    w = w_ref[...]
    x = x1_ref[...]
    for k in range(TOP_K):
        x = x + yk_ref[k] * w[:, k:k + 1]
    o_ref[...] = x * lax.rsqrt(jnp.mean(x * x, axis=-1, keepdims=True) + RMS_EPS) * nw_ref[...]


def _combine(x1, yk, w_tk, norm_w, out_rows, row0, prev_out):
    Tg, D = x1.shape
    tm = min(COMBINE_ROWS, Tg)
    row = pl.BlockSpec((tm, D), lambda i: (i, 0))
    in_specs = [row, pl.BlockSpec((TOP_K, tm, D), lambda i: (0, i, 0)),
                pl.BlockSpec((tm, TOP_K), lambda i: (i, 0)),
                pl.BlockSpec((1, D), lambda i: (0, 0))]
    args = [x1, yk, w_tk, norm_w]
    aliases = {}
    if prev_out is not None:
        in_specs.append(pl.BlockSpec(memory_space=pl.ANY))
        args.append(prev_out)
        aliases = {len(args) - 1: 0}
    return pl.pallas_call(
        _combine_kernel,
        grid=(Tg // tm,),
        in_specs=in_specs,
        out_specs=pl.BlockSpec((tm, D), lambda i: (i + row0 // tm, 0)),
        out_shape=jax.ShapeDtypeStruct((out_rows, D), F32),
        input_output_aliases=aliases,
        compiler_params=_cparams("parallel"),
        name="combine",
    )(*args)


def _qk_column_order():
    half = ATT_HEAD_DIM // 2
    order = []
    for p in range(ATT_HEADS // 2):
        for part in range(2):
            for h in (2 * p, 2 * p + 1):
                order.extend(range(h * ATT_HEAD_DIM + part * half, h * ATT_HEAD_DIM + (part + 1) * half))
    return np.asarray(order, np.int32)


def _rope_tables(seq):
    half = ATT_HEAD_DIM // 2
    inv = ROPE_THETA ** (-(jnp.arange(half, dtype=F32) * 2.0 / ATT_HEAD_DIM))
    ang = jnp.arange(seq, dtype=F32)[:, None] * inv[None, :]
    cos, sin = jnp.cos(ang), jnp.sin(ang)
    return (jnp.concatenate([cos, cos, cos, cos], axis=1),
            jnp.concatenate([-sin, -sin, sin, sin], axis=1))


def kernel(x, norm1_w, w_in, moba_up, hgrn_lb_logits, hgrn_norm_w, hgrn_up, w_out, norm2_w,
           router_w, router_b, w_gate_up, b_gate_up, w_down, b_down, final_norm_w):
    B, S, D = x.shape
    T = B * S
    assert S % MOBA_BLOCK == 0 and w_in.shape[0] == 1
    x2 = x.reshape(T, D)

    perm = _qk_column_order()
    w0 = w_in[0]
    w_in_p = jnp.concatenate([w0[:, :ATT_WIDTH][:, perm], w0[:, ATT_WIDTH:2 * ATT_WIDTH][:, perm],
                              w0[:, 2 * ATT_WIDTH:]], axis=1).astype(BF16)
    cos_t, sin_t = _rope_tables(S)
    lb = jnp.cumsum(jax.nn.softmax(hgrn_lb_logits.astype(F32), axis=0), axis=0)[0:1]

    qa, ka, va, qb, fb, ib, gb, ga, gtb = _in_proj(x2, norm1_w, w_in_p, cos_t, sin_t, S)
    r3 = lambda a: a.reshape(B, S, a.shape[1])
    ya = _moba(r3(qa), r3(ka), r3(va)).reshape(T, ATT_WIDTH)
    yb = _hgrn(r3(qb), r3(fb), r3(ib), r3(gb), lb, hgrn_norm_w).reshape(T, HGRN_WIDTH)

    mix_w = (moba_up[0].astype(BF16), hgrn_up[0].astype(BF16), w_out[0].astype(BF16), norm2_w,
             router_w[0].T, router_b[0][:, None])
    wg, wu = _prep_gate_up(w_gate_up[0])
    expert_w = (wg, wu, b_gate_up[0][:, None, 0::2], b_gate_up[0][:, None, 1::2],
                w_down[0].astype(BF16), b_down[0][:, None, :])

    Tg = T // MOE_TOKEN_GROUPS
    out = None
    for g in range(MOE_TOKEN_GROUPS):
        x1, h2, top_e, top_w, rank, cnt = _mix_route(ya, yb, ga, gtb, x2, *mix_w, g * Tg, Tg)
        counts = cnt[:, 0].astype(jnp.int32)
        start = jnp.cumsum(counts) - counts
        dest = rank + jnp.sum(jnp.where(top_e[:, :, None] == jnp.arange(N_EXPERTS)[None, None, :],
                                        start[None, None, :], 0), axis=-1)
        xg = _scatter_rows(dest, h2)
        yg = _experts(xg, counts, *expert_w)
        yk = _gather_rows(dest, yg).reshape(TOP_K, Tg, D)
        out = _combine(x1, yk, top_w.T, final_norm_w[None, :], T, g * Tg, out)
    return out.reshape(B, S, D)
```

```python
import functools
import math

import numpy as np
import jax
import jax.numpy as jnp
from jax import lax
from jax.experimental import pallas as pl
from jax.experimental.pallas import tpu as pltpu
from jax.experimental.pallas import tpu_sc as plsc

ATT_HEADS = 8
ATT_HEAD_DIM = 64
ATT_WIDTH = ATT_HEADS * ATT_HEAD_DIM
MOBA_BLOCK = 256
MOBA_TOPK = 3
ROPE_THETA = 10000.0
HGRN_HEADS = 4
HGRN_DIM = 128
HGRN_WIDTH = HGRN_HEADS * HGRN_DIM
HGRN_CHUNK = 64
N_EXPERTS = 32
TOP_K = 4
SWIGLU_LIMIT = 7.0
SWIGLU_ALPHA = 1.702
RMS_EPS = 1e-6
NEG = -1e30

V7X_LANES = 128
V7X_SUBLANES = 8
V7X_MXU_DIM = 256
V7X_VMEM_LIMIT_BYTES = 56 * 1024 * 1024
V7X_SC_CORES = 2
V7X_SC_SUBCORES = 16

MOBA_PAIRS = 2
MOBA_GROUP = 4
PROJ_ROWS = 512
PROJ_COLS = 512
HGRN_ROWS = 256
MIX_ROWS = 512
EXPERT_ROWS = 256
MOE_TOKEN_GROUPS = 2
SC_ROWS = 64
COMBINE_ROWS = 256
PREP_ROWS = 256

F32 = jnp.float32
BF16 = jnp.bfloat16


def _nt_dot(a, b, precision=None):
    return lax.dot_general(a, b, (((1,), (1,)), ((), ())), precision=precision,
                           preferred_element_type=F32)


def _tn_dot(a, b, precision=None):
    return lax.dot_general(a, b, (((0,), (0,)), ((), ())), precision=precision,
                           preferred_element_type=F32)


def _cparams(*sem):
    return pltpu.CompilerParams(dimension_semantics=sem, vmem_limit_bytes=V7X_VMEM_LIMIT_BYTES)


def _pack_bf16_pair(lo, hi):
    lo_bits = lax.bitcast_convert_type(lo.astype(BF16).astype(F32), jnp.uint32)
    hi_bits = lax.bitcast_convert_type(hi.astype(BF16).astype(F32), jnp.uint32)
    return (lo_bits >> 16) | hi_bits


def _unpack_bf16_pair(word):
    lo = lax.bitcast_convert_type(word << 16, F32)
    hi = lax.bitcast_convert_type(word & jnp.uint32(0xFFFF0000), F32)
    return lo, hi


def _in_proj_kernel(x_ref, nw_ref, w_ref, cos_ref, sin_ref,
                    qa_ref, ka_ref, va_ref, qb_ref, fb_ref, ib_ref, gb_ref, ga_ref, gtb_ref):
    x = x_ref[...]
    h = x * lax.rsqrt(jnp.mean(x * x, axis=-1, keepdims=True) + RMS_EPS) * nw_ref[...]
    h = h.astype(BF16)
    cos = cos_ref[...]
    sin = sin_ref[...]

    def proj(c):
        return jnp.dot(h, w_ref[:, c * PROJ_COLS:(c + 1) * PROJ_COLS], preferred_element_type=F32)

    def rope(t):
        out = []
        for j in range(PROJ_COLS // V7X_LANES):
            tj = t[:, j * V7X_LANES:(j + 1) * V7X_LANES]
            out.append(tj * cos + pltpu.roll(tj, V7X_LANES // 2, 1) * sin)
        return jnp.concatenate(out, axis=1)

    qa_ref[...] = (rope(proj(0)) * (ATT_HEAD_DIM ** -0.5)).astype(BF16)
    ka_ref[...] = rope(proj(1)).astype(BF16)
    va_ref[...] = proj(2).astype(BF16)
    qb_ref[...] = proj(3).astype(BF16)
    fb_ref[...] = proj(4)
    ib_ref[...] = proj(5).astype(BF16)
    gb_ref[...] = proj(6).astype(BF16)
    ga_ref[:, :PROJ_COLS] = proj(7).astype(BF16)
    ga_ref[:, PROJ_COLS:] = proj(8).astype(BF16)
    gtb_ref[:, :PROJ_COLS] = proj(9).astype(BF16)
    gtb_ref[:, PROJ_COLS:] = proj(10).astype(BF16)


def _in_proj(x2, norm_w, w_in_bf16, cos_t, sin_t, seq):
    T, D = x2.shape
    tm = min(PROJ_ROWS, seq)
    n_seq_tiles = seq // tm
    row = lambda w: pl.BlockSpec((tm, w), lambda i: (i, 0))
    tab = pl.BlockSpec((tm, V7X_LANES), lambda i: (i % n_seq_tiles, 0))
    widths = [ATT_WIDTH] * 3 + [HGRN_WIDTH] * 4 + [D, D]
    dtypes = [BF16, BF16, BF16, BF16, F32, BF16, BF16, BF16, BF16]
    return pl.pallas_call(
        _in_proj_kernel,
        grid=(T // tm,),
        in_specs=[row(D), pl.BlockSpec((1, D), lambda i: (0, 0)),
                  pl.BlockSpec(memory_space=pltpu.VMEM), tab, tab],
        out_specs=[row(w) for w in widths],
        out_shape=[jax.ShapeDtypeStruct((T, w), dt) for w, dt in zip(widths, dtypes)],
        compiler_params=_cparams("parallel"),
        name="in_proj",
    )(x2, norm_w, w_in_bf16, cos_t, sin_t)


def _moba_kernel(q_ref, k_ref, v_ref, o_ref, kaug_ref, kmean_ref, vt_ref, acc_ref, s_ref,
                 sdiag_ref, p_ref, *, nb):
    qi = pl.program_id(2)
    blk = MOBA_BLOCK
    lanes = V7X_LANES
    n_pairs = kaug_ref.shape[0]
    heads = [(pp, hh) for pp in range(n_pairs) for hh in range(2)]
    nbp = kmean_ref.shape[1]

    @pl.when(qi == 0)
    def _():
        rowb = lax.broadcasted_iota(jnp.int32, (nb * blk, lanes), 0) // blk
        col = lax.broadcasted_iota(jnp.int32, (nb * blk, lanes), 1)
        onehot = jnp.where(rowb == col, 1.0, 0.0).astype(BF16)
        for pp in range(n_pairs):
            pl_ = slice(pp * lanes, (pp + 1) * lanes)
            kaug_ref[pp, :, :lanes] = k_ref[0, :, pl_]
            kaug_ref[pp, :, lanes:] = onehot
            means = [jnp.sum(k_ref[0, n * blk:(n + 1) * blk, pl_].astype(F32), axis=0,
                             keepdims=True) * (1.0 / blk) for n in range(nb)]
            km = jnp.concatenate(means + [jnp.zeros((nbp - nb, lanes), F32)] * (nbp > nb), axis=0)
            hi = km.astype(BF16)
            rem = km - hi.astype(F32)
            mid = rem.astype(BF16)
            lo = (rem - mid.astype(F32)).astype(BF16)
            kmean_ref[pp] = jnp.concatenate([hi, mid, lo], axis=1)
            for c in range(nb):
                vt = v_ref[0, c * blk:(c + 1) * blk, pl_].astype(F32).T
                vt_ref[pp, :lanes, c * blk:(c + 1) * blk] = vt.astype(BF16)
            vt_ref[pp, lanes:, :] = jnp.ones((vt_ref.shape[1] - lanes, nb * blk), BF16)

    feat = lax.broadcasted_iota(jnp.int32, (lanes, blk), 0)
    key_i = lax.broadcasted_iota(jnp.int32, (blk, blk), 0)
    qry_i = lax.broadcasted_iota(jnp.int32, (blk, blk), 1)
    own = pl.multiple_of(qi * blk, blk)
    blk_id = lax.broadcasted_iota(jnp.int32, (nbp, 2 * blk), 0)
    slab = 2 * V7X_SUBLANES

    def slab_max(s):
        return jnp.max(s.reshape(s.shape[0] // slab, slab, blk), axis=0)

    q_aug, m_init = [], []
    for pp in range(n_pairs):
        qt = q_ref[0, :, pp * lanes:(pp + 1) * lanes].astype(F32).T
        k_own = kaug_ref[pp, pl.ds(own, blk), :lanes]
        qhs = [jnp.where((feat // (ATT_HEAD_DIM // 2)) % 2 == hh, qt, 0.0).astype(BF16)
               for hh in range(2)]
        q2 = jnp.concatenate(qhs, axis=1)
        gate = jnp.dot(kmean_ref[pp], jnp.concatenate([q2, q2, q2], axis=0),
                       preferred_element_type=F32)
        gate = jnp.where(blk_id < qi, gate, NEG)
        beaten = jnp.zeros((nbp, 2 * blk), F32)
        for n in range(nb):
            gn = gate[n:n + 1, :]
            wins = (gn > gate) | ((gn == gate) & (blk_id > n))
            beaten = beaten + jnp.where(wins, 1.0, 0.0)
        sel = (beaten < MOBA_TOPK) & (blk_id < qi)
        bias = jnp.where(sel, 0.0, NEG).astype(BF16)
        bias = jnp.concatenate([bias, jnp.zeros((lanes - nbp, 2 * blk), BF16)], axis=0)
        for hh in range(2):
            h = 2 * pp + hh
            q_aug.append(jnp.concatenate([qhs[hh], bias[:, hh * blk:(hh + 1) * blk]], axis=0))
            s = jnp.dot(k_own, qhs[hh], preferred_element_type=F32)
            s = jnp.where(key_i <= qry_i, s, NEG)
            sdiag_ref[h] = s
            m_init.append(slab_max(s))
            acc_ref[h] = jnp.zeros(acc_ref.shape[1:], F32)

    group = MOBA_GROUP * blk
    n_groups = (qi + MOBA_GROUP) // MOBA_GROUP

    def score_group(g, ms):
        off = pl.multiple_of(g * group, group)
        out = []
        for pp in range(n_pairs):
            kb = kaug_ref[pp, pl.ds(off, group), :]
            for hh in range(2):
                h = 2 * pp + hh
                s = jnp.dot(kb, q_aug[h], preferred_element_type=F32)
                s_ref[h, pl.ds(off, group), :] = s
                out.append(jnp.maximum(ms[h], slab_max(s)))
        return tuple(out)

    ms = lax.fori_loop(0, n_groups, score_group, tuple(m_init))

    m_fin = []
    for h in range(len(heads)):
        s_ref[h, pl.ds(own, blk), :] = sdiag_ref[h]
        m_fin.append(jnp.max(ms[h], axis=0, keepdims=True))

    def exp_group(g):
        off = pl.multiple_of(g * group, group)
        for h in range(len(heads)):
            p_ref[h] = jnp.exp(s_ref[h, pl.ds(off, group), :] - m_fin[h]).astype(BF16)

    def value_group(g):
        off = pl.multiple_of(g * group, group)
        for pp in range(n_pairs):
            vb = vt_ref[pp, :, pl.ds(off, group)]
            for hh in range(2):
                h = 2 * pp + hh
                acc_ref[h] += jnp.dot(vb, p_ref[h], preferred_element_type=F32)

    def pipelined(g, c):
        value_group(g - 1)
        exp_group(g)
        return c

    exp_group(0)
    lax.fori_loop(1, n_groups, pipelined, 0)
    value_group(n_groups - 1)

    half = lanes // 2
    outs = []
    for pp in range(n_pairs):
        outs.append(acc_ref[2 * pp, :half, :] / acc_ref[2 * pp, lanes:lanes + 1, :])
        outs.append(acc_ref[2 * pp + 1, half:lanes, :] / acc_ref[2 * pp + 1, lanes:lanes + 1, :])
    o_ref[0] = jnp.concatenate(outs, axis=0).T.astype(BF16)


def _moba(q, k, v):
    B, S, _ = q.shape
    nb = S // MOBA_BLOCK
    assert nb % MOBA_GROUP == 0
    nbp = -(-nb // (2 * V7X_SUBLANES)) * (2 * V7X_SUBLANES)
    n_pairs = ATT_WIDTH // V7X_LANES
    pp = MOBA_PAIRS
    assert n_pairs % pp == 0
    vt_rows = V7X_LANES + 2 * V7X_SUBLANES
    qspec = pl.BlockSpec((1, MOBA_BLOCK, pp * V7X_LANES), lambda b, p, i: (b, i, p))
    kvspec = pl.BlockSpec((1, S, pp * V7X_LANES), lambda b, p, i: (b, 0, p))
    return pl.pallas_call(
        functools.partial(_moba_kernel, nb=nb),
        grid=(B, n_pairs // pp, nb),
        in_specs=[qspec, kvspec, kvspec],
        out_specs=qspec,
        out_shape=jax.ShapeDtypeStruct((B, S, ATT_WIDTH), BF16),
        scratch_shapes=[pltpu.VMEM((pp, S, 2 * V7X_LANES), BF16),
                        pltpu.VMEM((pp, nbp, 3 * V7X_LANES), BF16),
                        pltpu.VMEM((pp, vt_rows, S), BF16),
                        pltpu.VMEM((2 * pp, vt_rows, MOBA_BLOCK), F32),
                        pltpu.VMEM((2 * pp, S, MOBA_BLOCK), F32),
                        pltpu.VMEM((2 * pp, MOBA_BLOCK, MOBA_BLOCK), F32),
                        pltpu.VMEM((2 * pp, MOBA_GROUP * MOBA_BLOCK, MOBA_BLOCK), BF16)],
        compiler_params=_cparams("parallel", "parallel", "arbitrary"),
        name="moba",
    )(q, k, v)


def _hgrn_level_sizes(chunk):
    return [chunk >> (i + 1) for i in range(int(math.log2(chunk)))]


def _hgrn_constants(chunk):
    t = np.arange(chunk)
    mats = [(t[None, :] <= t[:, None]),
            (t[None, :] > t[:, None])]
    qrows, pmasks = [], []
    for bs in _hgrn_level_sizes(chunk):
        blk = t // bs
        odd = (blk % 2) == 1
        lo, hi = blk * bs, (blk + 1) * bs
        u = t[None, :]
        m_odd = (u >= lo[:, None]) & (u <= t[:, None])
        m_even = (u > t[:, None]) & (u < hi[:, None])
        mats.append(np.where(odd[:, None], m_odd, m_even))
        qrows.append(odd)
        pmasks.append(odd[:, None] & (blk[None, :] == blk[:, None] - 1))
    pmasks.append(t[None, :] == t[:, None])
    summat = np.concatenate(mats, axis=0).astype(np.float32)
    qrow = np.stack(qrows, axis=0).astype(np.float32)
    pmask = np.stack(pmasks, axis=0).astype(np.float32)
    return summat, qrow, pmask


def _hgrn_kernel(q_ref, f_ref, i_ref, g_ref, lb_ref, nw_ref, sm_ref, qrow_ref, pm_ref,
                 o_ref, state_ref, *, rows):
    C = HGRN_CHUNK
    n_levels = qrow_ref.shape[0]

    @pl.when(pl.program_id(1) == 0)
    def _():
        state_ref[...] = jnp.zeros_like(state_ref)

    lb = lb_ref[...]
    summat = sm_ref[...]
    for c in range(rows // C):
        r0 = c * C
        fg = lb + (1.0 - lb) * jax.nn.sigmoid(f_ref[0, r0:r0 + C, :])
        logf = jnp.log(fg)
        hi = logf.astype(BF16)
        rem = logf - hi.astype(F32)
        mid = rem.astype(BF16)
        lo = (rem - mid.astype(F32)).astype(BF16)
        sums = jnp.dot(summat, jnp.concatenate([hi, mid, lo], axis=0),
                       preferred_element_type=F32)
        for h in range(HGRN_HEADS):
            ls = slice(h * HGRN_DIM, (h + 1) * HGRN_DIM)
            qf = jax.nn.silu(q_ref[0, r0:r0 + C, ls].astype(F32))
            kf = 1.0 - fg[:, ls]
            iv = i_ref[0, r0:r0 + C, ls]
            bcum = sums[0:C, ls]
            bsuf = sums[C:2 * C, ls]
            att = _nt_dot(qf.astype(BF16), kf.astype(BF16)) * pm_ref[n_levels]
            for lv in range(n_levels):
                w = jnp.exp(sums[(2 + lv) * C:(3 + lv) * C, ls])
                qrow = qrow_ref[lv]
                z = (jnp.where(qrow > 0.5, qf, kf) * w).astype(BF16)
                att = att + _nt_dot(z, z) * pm_ref[lv]
            o = jnp.dot(att.astype(BF16), iv, preferred_element_type=F32)
            st = state_ref[h]
            o = o + _nt_dot((qf * jnp.exp(bcum)).astype(BF16), st.astype(BF16))
            kdec = (kf * jnp.exp(bsuf)).astype(BF16)
            state_ref[h] = st * jnp.exp(bcum[C - 1:C, :]) + _tn_dot(iv, kdec)
            o = o * lax.rsqrt(jnp.mean(o * o, axis=-1, keepdims=True) + RMS_EPS)
            o = o * nw_ref[:, ls] * jax.nn.silu(g_ref[0, r0:r0 + C, ls].astype(F32))
            o_ref[0, r0:r0 + C, ls] = o.astype(BF16)


def _hgrn(qb, fb, ib, gb, lb, norm_w):
    B, S, W = qb.shape
    rows = min(HGRN_ROWS, S)
    summat, qrow, pmask = _hgrn_constants(HGRN_CHUNK)
    summat = jnp.asarray(np.concatenate([summat] * 3, axis=1), BF16)
    n_levels = qrow.shape[0]
    blk = pl.BlockSpec((1, rows, W), lambda b, s: (b, s, 0))
    vec = pl.BlockSpec((1, W), lambda b, s: (0, 0))
    const = lambda a: pl.BlockSpec(a.shape, lambda b, s: (0,) * a.ndim)
    qrow3 = qrow.reshape(n_levels, HGRN_CHUNK, 1)
    return pl.pallas_call(
        functools.partial(_hgrn_kernel, rows=rows),
        grid=(B, S // rows),
        in_specs=[blk, blk, blk, blk, vec, vec, const(summat), const(qrow3), const(pmask)],
        out_specs=blk,
        out_shape=jax.ShapeDtypeStruct((B, S, W), BF16),
        scratch_shapes=[pltpu.VMEM((HGRN_HEADS, HGRN_DIM, HGRN_DIM), F32)],
        compiler_params=_cparams("parallel", "arbitrary"),
        name="hgrn",
    )(qb, fb, ib, gb, lb, norm_w, jnp.asarray(summat), jnp.asarray(qrow3), jnp.asarray(pmask))


def _mix_route_kernel(ya_ref, yb_ref, ga_ref, gb_ref, x_ref, wa_ref, wb_ref, wo_ref, nw_ref,
                      rw_ref, rb_ref, tri_ref,
                      x1_ref, h2_ref, e_ref, w_ref, rank_ref, cnt_ref, carry_ref):
    @pl.when(pl.program_id(0) == 0)
    def _():
        carry_ref[...] = jnp.zeros_like(carry_ref)

    ua = jnp.dot(ya_ref[...], wa_ref[...], preferred_element_type=F32)
    ub = jnp.dot(yb_ref[...], wb_ref[...], preferred_element_type=F32)
    mixed = (jax.nn.sigmoid(ga_ref[...].astype(F32)) * ua
             + jax.nn.sigmoid(gb_ref[...].astype(F32)) * ub)
    x1 = x_ref[...] + jnp.dot(mixed.astype(BF16), wo_ref[...], preferred_element_type=F32)
    x1_ref[...] = x1
    h2 = x1 * lax.rsqrt(jnp.mean(x1 * x1, axis=-1, keepdims=True) + RMS_EPS) * nw_ref[...]
    half = h2.shape[1] // 2
    h2_ref[...] = _pack_bf16_pair(h2[:, :half], h2[:, half:])

    tm = x1.shape[0]
    logits = _nt_dot(rw_ref[...], h2, precision=lax.Precision.HIGHEST) + rb_ref[...]
    eid = lax.broadcasted_iota(jnp.int32, (N_EXPERTS, tm), 0)
    work = logits
    es, vs = [], []
    for _ in range(TOP_K):
        mx = jnp.max(work, axis=0, keepdims=True)
        idx = jnp.min(jnp.where(work == mx, eid, N_EXPERTS), axis=0, keepdims=True)
        es.append(idx)
        vs.append(mx)
        work = jnp.where(eid == idx, -jnp.inf, work)
    ex = [jnp.exp(v - vs[0]) for v in vs]
    den = ex[0] + ex[1] + ex[2] + ex[3]
    multi = jnp.zeros((N_EXPERTS, tm), F32)
    for k in range(TOP_K):
        multi = multi + jnp.where(eid == es[k], 1.0, 0.0)
    before = jnp.dot(multi.astype(BF16), tri_ref[...], preferred_element_type=F32) + carry_ref[...]
    for k in range(TOP_K):
        e_ref[k:k + 1, :] = es[k]
        w_ref[k:k + 1, :] = ex[k] / den
        rank_ref[k:k + 1, :] = jnp.sum(jnp.where(eid == es[k], before, 0.0), axis=0,
                                       keepdims=True).astype(jnp.int32)
    carry_ref[...] = carry_ref[...] + jnp.sum(multi, axis=1, keepdims=True)
    cnt_ref[...] = jnp.broadcast_to(carry_ref[...], cnt_ref.shape)


def _mix_route(ya, yb, ga, gb, x2, wa, wb, wo, norm_w, rw_t, rb, row0, rows):
    D = x2.shape[1]
    T = rows
    tm = min(MIX_ROWS, T)
    row = lambda w: pl.BlockSpec((tm, w), lambda i: (i + row0 // tm, 0))
    out_row = pl.BlockSpec((tm, D), lambda i: (i, 0))
    whole = pl.BlockSpec(memory_space=pltpu.VMEM)
    kt = pl.BlockSpec((TOP_K, tm), lambda i: (0, i))
    tri = jnp.asarray(np.triu(np.ones((tm, tm), np.float32), 1), BF16)
    return pl.pallas_call(
        _mix_route_kernel,
        grid=(T // tm,),
        in_specs=[row(ATT_WIDTH), row(HGRN_WIDTH), row(D), row(D), row(D),
                  whole, whole, whole, pl.BlockSpec((1, D), lambda i: (0, 0)),
                  whole, whole, whole],
        out_specs=[out_row, pl.BlockSpec((tm, D // 2), lambda i: (i, 0)), kt, kt, kt,
                   pl.BlockSpec((N_EXPERTS, V7X_LANES), lambda i: (0, 0))],
        out_shape=[jax.ShapeDtypeStruct((T, D), F32), jax.ShapeDtypeStruct((T, D // 2), jnp.uint32),
                   jax.ShapeDtypeStruct((TOP_K, T), jnp.int32),
                   jax.ShapeDtypeStruct((TOP_K, T), F32),
                   jax.ShapeDtypeStruct((TOP_K, T), jnp.int32),
                   jax.ShapeDtypeStruct((N_EXPERTS, V7X_LANES), F32)],
        scratch_shapes=[pltpu.VMEM((N_EXPERTS, 1), F32)],
        compiler_params=_cparams("arbitrary"),
        name="mix_route",
    )(ya, yb, ga, gb, x2, wa, wb, wo, norm_w, rw_t, rb, tri)


def _sc_worker_id():
    return lax.axis_index("s") * V7X_SC_CORES + lax.axis_index("c")


def _sc_kernel(body, out_rows, like, window, name):
    mesh = plsc.VectorSubcoreMesh(core_axis_name="c", subcore_axis_name="s")
    return pl.kernel(
        body, mesh=mesh,
        out_type=jax.ShapeDtypeStruct((out_rows,) + like.shape[1:], like.dtype),
        scratch_types=[pltpu.VMEM((window,), jnp.int32),
                       pltpu.VMEM((window,) + like.shape[1:], like.dtype)],
        name=name)


def _scatter_rows(dest, h2):
    T = dest.shape[1]
    n_workers = V7X_SC_CORES * V7X_SC_SUBCORES
    per_worker = T // n_workers
    window = min(SC_ROWS, per_worker)

    def body(dest_hbm, src_hbm, dst_hbm, idx_v, rows_v):
        base = _sc_worker_id() * per_worker

        @pl.loop(0, per_worker // window)
        def _(c):
            t0 = base + c * window
            pltpu.sync_copy(src_hbm.at[pl.ds(t0, window)], rows_v)
            for k in range(TOP_K):
                pltpu.sync_copy(dest_hbm.at[pl.ds(k * T + t0, window)], idx_v)
                pltpu.sync_copy(rows_v, dst_hbm.at[idx_v])

    return _sc_kernel(body, TOP_K * T, h2, window, "scatter_rows")(dest.reshape(-1), h2)


def _gather_rows(dest, yg):
    T = dest.shape[1]
    n_workers = V7X_SC_CORES * V7X_SC_SUBCORES
    per_worker = TOP_K * T // n_workers
    window = min(SC_ROWS, per_worker)

    def body(dest_hbm, src_hbm, dst_hbm, idx_v, rows_v):
        base = _sc_worker_id() * per_worker

        @pl.loop(0, per_worker // window)
        def _(c):
            r0 = base + c * window
            pltpu.sync_copy(dest_hbm.at[pl.ds(r0, window)], idx_v)
            pltpu.sync_copy(src_hbm.at[idx_v], rows_v)
            pltpu.sync_copy(rows_v, dst_hbm.at[pl.ds(r0, window)])

    return _sc_kernel(body, TOP_K * T, yg, window, "gather_rows")(dest.reshape(-1), yg)


def _prep_gate_up_kernel(w_ref, perm_ref, wg_ref, wu_ref):
    half = V7X_MXU_DIM // 2
    for g in range(w_ref.shape[2] // V7X_MXU_DIM):
        wb = w_ref[0, :, g * V7X_MXU_DIM:(g + 1) * V7X_MXU_DIM].astype(BF16)
        d = jnp.dot(wb, perm_ref[...], preferred_element_type=F32).astype(BF16)
        wg_ref[0, :, g * half:(g + 1) * half] = d[:, :half]
        wu_ref[0, :, g * half:(g + 1) * half] = d[:, half:]


def _prep_gate_up(w_gate_up):
    E, D, F2 = w_gate_up.shape
    rows = min(PREP_ROWS, D)
    i = np.arange(V7X_MXU_DIM)
    src = np.where(i < V7X_MXU_DIM // 2, 2 * i, 2 * (i - V7X_MXU_DIM // 2) + 1)
    perm = np.zeros((V7X_MXU_DIM, V7X_MXU_DIM), np.float32)
    perm[src, i] = 1.0
    out = pl.BlockSpec((1, rows, F2 // 2), lambda e, r: (e, r, 0))
    return pl.pallas_call(
        _prep_gate_up_kernel,
        grid=(E, D // rows),
        in_specs=[pl.BlockSpec((1, rows, F2), lambda e, r: (e, r, 0)),
                  pl.BlockSpec((V7X_MXU_DIM, V7X_MXU_DIM), lambda e, r: (0, 0))],
        out_specs=[out, out],
        out_shape=[jax.ShapeDtypeStruct((E, D, F2 // 2), BF16)] * 2,
        compiler_params=_cparams("parallel", "parallel"),
        name="prep_gate_up",
    )(w_gate_up, jnp.asarray(perm, BF16))


def _experts_kernel(blk_ref, exp_ref, lo_ref, hi_ref, first_ref,
                    x_ref, wg_ref, wu_ref, bg_ref, bu_ref, wd_ref, bd_ref, y_ref):
    w = pl.program_id(0)
    lo = lo_ref[w]
    hi = hi_ref[w]

    tm = x_ref.shape[0]
    n_out = V7X_MXU_DIM

    def ffn(merge):
        x = jnp.concatenate(_unpack_bf16_pair(x_ref[...]), axis=1).astype(BF16)
        g = jnp.dot(x, wg_ref[0], preferred_element_type=F32) + bg_ref[0]
        u = jnp.dot(x, wu_ref[0], preferred_element_type=F32) + bu_ref[0]
        g = jnp.minimum(g, SWIGLU_LIMIT)
        u = jnp.clip(u, -SWIGLU_LIMIT, SWIGLU_LIMIT)
        act = ((u + 1.0) * (g * jax.nn.sigmoid(g * SWIGLU_ALPHA))).astype(BF16)
        half = y_ref.shape[1]

        def down(c0):
            cs = slice(c0, c0 + n_out)
            return jnp.dot(act, wd_ref[0, :, cs], preferred_element_type=F32) + bd_ref[0, :, cs]

        for c in range(half // n_out):
            cs = slice(c * n_out, (c + 1) * n_out)
            y_ref[:, cs] = merge(_pack_bf16_pair(down(c * n_out), down(half + c * n_out)), cs)

    whole = hi - lo == tm
    r = lax.broadcasted_iota(jnp.int32, (tm, n_out), 0)
    mine = (r >= lo) & (r < hi)

    @pl.when(whole)
    def _():
        ffn(lambda y, cs: y)

    @pl.when((lo < hi) & jnp.logical_not(whole) & (first_ref[w] == 1))
    def _():
        ffn(lambda y, cs: jnp.where(mine, y, jnp.uint32(0)))

    @pl.when((lo < hi) & jnp.logical_not(whole) & (first_ref[w] == 0))
    def _():
        ffn(lambda y, cs: jnp.where(mine, y, y_ref[:, cs]))


def _work_items(counts, n_rows, tm):
    nblk = n_rows // tm
    n_items = nblk + N_EXPERTS - 1
    end = jnp.cumsum(counts)
    start = end - counts
    fb = start // tm
    nitems = jnp.where(counts > 0, (end - 1) // tm - fb + 1, 0)
    item_end = jnp.cumsum(nitems)
    item_start = item_end - nitems
    w = jnp.arange(n_items, dtype=jnp.int32)
    valid = w < item_end[-1]
    wc = jnp.minimum(w, item_end[-1] - 1)
    e = jnp.sum(wc[:, None] >= item_end[None, :], axis=1).astype(jnp.int32)
    e = jnp.minimum(e, N_EXPERTS - 1)
    onehot = e[:, None] == jnp.arange(N_EXPERTS, dtype=jnp.int32)[None, :]
    pick = lambda table: jnp.sum(jnp.where(onehot, table[None, :], 0), axis=1)
    blk = (pick(fb) + (wc - pick(item_start))).astype(jnp.int32)
    lo = jnp.maximum(pick(start), blk * tm) - blk * tm
    hi = jnp.minimum(pick(end), (blk + 1) * tm) - blk * tm
    lo = jnp.where(valid, lo, 0).astype(jnp.int32)
    hi = jnp.where(valid, hi, 0).astype(jnp.int32)
    first = jnp.concatenate([jnp.ones((1,), jnp.int32),
                             (blk[1:] != blk[:-1]).astype(jnp.int32)])
    return blk, e, lo, hi, first


def _experts(xg, counts, wg, wu, bg, bu, wd, bd):
    A = xg.shape[0]
    D = 2 * xg.shape[1]
    tm = min(EXPERT_ROWS, A)
    F = wg.shape[2]
    items = _work_items(counts, A, tm)
    n_items = A // tm + N_EXPERTS - 1
    xs = pl.BlockSpec((tm, D // 2), lambda w, blk, e, lo, hi, fi: (blk[w], 0))
    wsp = lambda r, c: pl.BlockSpec((1, r, c), lambda w, blk, e, lo, hi, fi: (e[w], 0, 0))
    return pl.pallas_call(
        _experts_kernel,
        grid_spec=pltpu.PrefetchScalarGridSpec(
            num_scalar_prefetch=5,
            grid=(n_items,),
            in_specs=[xs, wsp(D, F), wsp(D, F), wsp(1, F), wsp(1, F), wsp(F, D), wsp(1, D)],
            out_specs=xs,
        ),
        out_shape=jax.ShapeDtypeStruct(xg.shape, jnp.uint32),
        compiler_params=_cparams("arbitrary"),
        name="experts",
    )(*items, xg, wg, wu, bg, bu, wd, bd)


def _combine_kernel(x1_ref, yk_ref, w_ref, nw_ref, *rest):
    o_ref = rest[-1]
    w = w_ref[...]
    x = x1_ref[...]
    for k in range(TOP_K):
        x = x + jnp.concatenate(_unpack_bf16_pair(yk_ref[k]), axis=1) * w[:, k:k + 1]
    o_ref[...] = x * lax.rsqrt(jnp.mean(x * x, axis=-1, keepdims=True) + RMS_EPS) * nw_ref[...]


def _combine(x1, yk, w_tk, norm_w, out_rows, row0, prev_out):
    Tg, D = x1.shape
    tm = min(COMBINE_ROWS, Tg)
    row = pl.BlockSpec((tm, D), lambda i: (i, 0))
    in_specs = [row, pl.BlockSpec((TOP_K, tm, D // 2), lambda i: (0, i, 0)),
                pl.BlockSpec((tm, TOP_K), lambda i: (i, 0)),
                pl.BlockSpec((1, D), lambda i: (0, 0))]
    args = [x1, yk, w_tk, norm_w]
    aliases = {}
    if prev_out is not None:
        in_specs.append(pl.BlockSpec(memory_space=pl.ANY))
        args.append(prev_out)
        aliases = {len(args) - 1: 0}
    return pl.pallas_call(
        _combine_kernel,
        grid=(Tg // tm,),
        in_specs=in_specs,
        out_specs=pl.BlockSpec((tm, D), lambda i: (i + row0 // tm, 0)),
        out_shape=jax.ShapeDtypeStruct((out_rows, D), F32),
        input_output_aliases=aliases,
        compiler_params=_cparams("parallel"),
        name="combine",
    )(*args)


def _qk_column_order():
    half = ATT_HEAD_DIM // 2
    order = []
    for p in range(ATT_HEADS // 2):
        for part in range(2):
            for h in (2 * p, 2 * p + 1):
                order.extend(range(h * ATT_HEAD_DIM + part * half, h * ATT_HEAD_DIM + (part + 1) * half))
    return np.asarray(order, np.int32)


def _rope_tables(seq):
    half = ATT_HEAD_DIM // 2
    inv = ROPE_THETA ** (-(jnp.arange(half, dtype=F32) * 2.0 / ATT_HEAD_DIM))
    ang = jnp.arange(seq, dtype=F32)[:, None] * inv[None, :]
    cos, sin = jnp.cos(ang), jnp.sin(ang)
    return (jnp.concatenate([cos, cos, cos, cos], axis=1),
            jnp.concatenate([-sin, -sin, sin, sin], axis=1))


def kernel(x, norm1_w, w_in, moba_up, hgrn_lb_logits, hgrn_norm_w, hgrn_up, w_out, norm2_w,
           router_w, router_b, w_gate_up, b_gate_up, w_down, b_down, final_norm_w):
    B, S, D = x.shape
    T = B * S
    assert S % MOBA_BLOCK == 0 and w_in.shape[0] == 1
    x2 = x.reshape(T, D)

    perm = _qk_column_order()
    w0 = w_in[0]
    w_in_p = jnp.concatenate([w0[:, :ATT_WIDTH][:, perm], w0[:, ATT_WIDTH:2 * ATT_WIDTH][:, perm],
                              w0[:, 2 * ATT_WIDTH:]], axis=1).astype(BF16)
    cos_t, sin_t = _rope_tables(S)
    lb = jnp.cumsum(jax.nn.softmax(hgrn_lb_logits.astype(F32), axis=0), axis=0)[0:1]

    qa, ka, va, qb, fb, ib, gb, ga, gtb = _in_proj(x2, norm1_w, w_in_p, cos_t, sin_t, S)
    r3 = lambda a: a.reshape(B, S, a.shape[1])
    ya = _moba(r3(qa), r3(ka), r3(va)).reshape(T, ATT_WIDTH)
    yb = _hgrn(r3(qb), r3(fb), r3(ib), r3(gb), lb, hgrn_norm_w).reshape(T, HGRN_WIDTH)

    mix_w = (moba_up[0].astype(BF16), hgrn_up[0].astype(BF16), w_out[0].astype(BF16), norm2_w,
             router_w[0].T, router_b[0][:, None])
    wg, wu = _prep_gate_up(w_gate_up[0])
    expert_w = (wg, wu, b_gate_up[0][:, None, 0::2], b_gate_up[0][:, None, 1::2],
                w_down[0].astype(BF16), b_down[0][:, None, :])

    Tg = T // MOE_TOKEN_GROUPS
    out = None
    for g in range(MOE_TOKEN_GROUPS):
        x1, h2, top_e, top_w, rank, cnt = _mix_route(ya, yb, ga, gtb, x2, *mix_w, g * Tg, Tg)
        counts = cnt[:, 0].astype(jnp.int32)
        start = jnp.cumsum(counts) - counts
        dest = rank + jnp.sum(jnp.where(top_e[:, :, None] == jnp.arange(N_EXPERTS)[None, None, :],
                                        start[None, None, :], 0), axis=-1)
        xg = _scatter_rows(dest, h2)
        yg = _experts(xg, counts, *expert_w)
        yk = _gather_rows(dest, yg).reshape(TOP_K, Tg, D // 2)
        out = _combine(x1, yk, top_w.T, final_norm_w[None, :], T, g * Tg, out)
    return out.reshape(B, S, D)
```

```python
import functools
import math

import numpy as np
import jax
import jax.numpy as jnp
from jax import lax
from jax.experimental import pallas as pl
from jax.experimental.pallas import tpu as pltpu
from jax.experimental.pallas import tpu_sc as plsc

ATT_HEADS = 8
ATT_HEAD_DIM = 64
ATT_WIDTH = ATT_HEADS * ATT_HEAD_DIM
MOBA_BLOCK = 256
MOBA_TOPK = 3
ROPE_THETA = 10000.0
HGRN_HEADS = 4
HGRN_DIM = 128
HGRN_WIDTH = HGRN_HEADS * HGRN_DIM
HGRN_CHUNK = 64
N_EXPERTS = 32
TOP_K = 4
SWIGLU_LIMIT = 7.0
SWIGLU_ALPHA = 1.702
RMS_EPS = 1e-6
NEG = -1e30

V7X_LANES = 128
V7X_SUBLANES = 8
V7X_MXU_DIM = 256
V7X_VMEM_LIMIT_BYTES = 56 * 1024 * 1024
V7X_SC_CORES = 2
V7X_SC_SUBCORES = 16

MOBA_PAIRS = 2
MOBA_GROUP = 4
PROJ_ROWS = 512
PROJ_COLS = 512
HGRN_ROWS = 256
MIX_ROWS = 512
EXPERT_ROWS = 256
MOE_TOKEN_GROUPS = 2
SC_ROWS = 64
COMBINE_ROWS = 256
PREP_ROWS = 256

F32 = jnp.float32
BF16 = jnp.bfloat16


def _nt_dot(a, b, precision=None):
    return lax.dot_general(a, b, (((1,), (1,)), ((), ())), precision=precision,
                           preferred_element_type=F32)


def _tn_dot(a, b, precision=None):
    return lax.dot_general(a, b, (((0,), (0,)), ((), ())), precision=precision,
                           preferred_element_type=F32)


def _cparams(*sem):
    return pltpu.CompilerParams(dimension_semantics=sem, vmem_limit_bytes=V7X_VMEM_LIMIT_BYTES)


def _pack_bf16_pair(lo, hi):
    lo_bits = lax.bitcast_convert_type(lo.astype(BF16).astype(F32), jnp.uint32)
    hi_bits = lax.bitcast_convert_type(hi.astype(BF16).astype(F32), jnp.uint32)
    return (lo_bits >> 16) | hi_bits


def _unpack_bf16_pair(word):
    lo = lax.bitcast_convert_type(word << 16, F32)
    hi = lax.bitcast_convert_type(word & jnp.uint32(0xFFFF0000), F32)
    return lo, hi


def _in_proj_kernel(x_ref, nw_ref, w_ref, cos_ref, sin_ref,
                    qa_ref, ka_ref, va_ref, qb_ref, fb_ref, ib_ref, gb_ref, ga_ref, gtb_ref):
    x = x_ref[...]
    h = x * lax.rsqrt(jnp.mean(x * x, axis=-1, keepdims=True) + RMS_EPS) * nw_ref[...]
    h = h.astype(BF16)
    cos = cos_ref[...]
    sin = sin_ref[...]

    def proj(c):
        return jnp.dot(h, w_ref[:, c * PROJ_COLS:(c + 1) * PROJ_COLS], preferred_element_type=F32)

    def rope(t):
        out = []
        for j in range(PROJ_COLS // V7X_LANES):
            tj = t[:, j * V7X_LANES:(j + 1) * V7X_LANES]
            out.append(tj * cos + pltpu.roll(tj, V7X_LANES // 2, 1) * sin)
        return jnp.concatenate(out, axis=1)

    qa_ref[...] = (rope(proj(0)) * (ATT_HEAD_DIM ** -0.5)).astype(BF16)
    ka_ref[...] = rope(proj(1)).astype(BF16)
    va_ref[...] = proj(2).astype(BF16)
    qb_ref[...] = proj(3).astype(BF16)
    fb_ref[...] = proj(4)
    ib_ref[...] = proj(5).astype(BF16)
    gb_ref[...] = proj(6).astype(BF16)
    ga_ref[:, :PROJ_COLS] = proj(7).astype(BF16)
    ga_ref[:, PROJ_COLS:] = proj(8).astype(BF16)
    gtb_ref[:, :PROJ_COLS] = proj(9).astype(BF16)
    gtb_ref[:, PROJ_COLS:] = proj(10).astype(BF16)


def _in_proj(x2, norm_w, w_in_bf16, cos_t, sin_t, seq):
    T, D = x2.shape
    tm = min(PROJ_ROWS, seq)
    n_seq_tiles = seq // tm
    row = lambda w: pl.BlockSpec((tm, w), lambda i: (i, 0))
    tab = pl.BlockSpec((tm, V7X_LANES), lambda i: (i % n_seq_tiles, 0))
    widths = [ATT_WIDTH] * 3 + [HGRN_WIDTH] * 4 + [D, D]
    dtypes = [BF16, BF16, BF16, BF16, F32, BF16, BF16, BF16, BF16]
    return pl.pallas_call(
        _in_proj_kernel,
        grid=(T // tm,),
        in_specs=[row(D), pl.BlockSpec((1, D), lambda i: (0, 0)),
                  pl.BlockSpec(memory_space=pltpu.VMEM), tab, tab],
        out_specs=[row(w) for w in widths],
        out_shape=[jax.ShapeDtypeStruct((T, w), dt) for w, dt in zip(widths, dtypes)],
        compiler_params=_cparams("parallel"),
        name="in_proj",
    )(x2, norm_w, w_in_bf16, cos_t, sin_t)


def _moba_kernel(q_ref, k_ref, v_ref, o_ref, kaug_ref, kmean_ref, vt_ref, acc_ref, s_ref,
                 sdiag_ref, p_ref, *, nb):
    qi = pl.program_id(2)
    blk = MOBA_BLOCK
    lanes = V7X_LANES
    n_pairs = kaug_ref.shape[0]
    heads = [(pp, hh) for pp in range(n_pairs) for hh in range(2)]
    nbp = kmean_ref.shape[1]

    @pl.when(qi == 0)
    def _():
        rowb = lax.broadcasted_iota(jnp.int32, (nb * blk, lanes), 0) // blk
        col = lax.broadcasted_iota(jnp.int32, (nb * blk, lanes), 1)
        onehot = jnp.where(rowb == col, 1.0, 0.0).astype(BF16)
        for pp in range(n_pairs):
            pl_ = slice(pp * lanes, (pp + 1) * lanes)
            kaug_ref[pp, :, :lanes] = k_ref[0, :, pl_]
            kaug_ref[pp, :, lanes:] = onehot
            means = [jnp.sum(k_ref[0, n * blk:(n + 1) * blk, pl_].astype(F32), axis=0,
                             keepdims=True) * (1.0 / blk) for n in range(nb)]
            km = jnp.concatenate(means + [jnp.zeros((nbp - nb, lanes), F32)] * (nbp > nb), axis=0)
            hi = km.astype(BF16)
            rem = km - hi.astype(F32)
            mid = rem.astype(BF16)
            lo = (rem - mid.astype(F32)).astype(BF16)
            kmean_ref[pp] = jnp.concatenate([hi, mid, lo], axis=1)
            for c in range(nb):
                vt = v_ref[0, c * blk:(c + 1) * blk, pl_].astype(F32).T
                vt_ref[pp, :lanes, c * blk:(c + 1) * blk] = vt.astype(BF16)
            vt_ref[pp, lanes:, :] = jnp.ones((vt_ref.shape[1] - lanes, nb * blk), BF16)

    feat = lax.broadcasted_iota(jnp.int32, (lanes, blk), 0)
    key_i = lax.broadcasted_iota(jnp.int32, (blk, blk), 0)
    qry_i = lax.broadcasted_iota(jnp.int32, (blk, blk), 1)
    own = pl.multiple_of(qi * blk, blk)
    blk_id = lax.broadcasted_iota(jnp.int32, (nbp, 2 * blk), 0)
    slab = 2 * V7X_SUBLANES

    def slab_max(s):
        return jnp.max(s.reshape(s.shape[0] // slab, slab, blk), axis=0)

    q_aug, m_init = [], []
    for pp in range(n_pairs):
        qt = q_ref[0, :, pp * lanes:(pp + 1) * lanes].astype(F32).T
        k_own = kaug_ref[pp, pl.ds(own, blk), :lanes]
        qhs = [jnp.where((feat // (ATT_HEAD_DIM // 2)) % 2 == hh, qt, 0.0).astype(BF16)
               for hh in range(2)]
        q2 = jnp.concatenate(qhs, axis=1)
        gate = jnp.dot(kmean_ref[pp], jnp.concatenate([q2, q2, q2], axis=0),
                       preferred_element_type=F32)
        gate = jnp.where(blk_id < qi, gate, NEG)
        beaten = jnp.zeros((nbp, 2 * blk), F32)
        for n in range(nb):
            gn = gate[n:n + 1, :]
            wins = (gn > gate) | ((gn == gate) & (blk_id > n))
            beaten = beaten + jnp.where(wins, 1.0, 0.0)
        sel = (beaten < MOBA_TOPK) & (blk_id < qi)
        bias = jnp.where(sel, 0.0, NEG).astype(BF16)
        bias = jnp.concatenate([bias, jnp.zeros((lanes - nbp, 2 * blk), BF16)], axis=0)
        for hh in range(2):
            h = 2 * pp + hh
            q_aug.append(jnp.concatenate([qhs[hh], bias[:, hh * blk:(hh + 1) * blk]], axis=0))
            s = jnp.dot(k_own, qhs[hh], preferred_element_type=F32)
            s = jnp.where(key_i <= qry_i, s, NEG)
            sdiag_ref[h] = s
            m_init.append(slab_max(s))
            acc_ref[h] = jnp.zeros(acc_ref.shape[1:], F32)

    group = MOBA_GROUP * blk
    n_groups = (qi + MOBA_GROUP) // MOBA_GROUP

    def score_group(g, ms):
        off = pl.multiple_of(g * group, group)
        out = []
        for pp in range(n_pairs):
            kb = kaug_ref[pp, pl.ds(off, group), :]
            for hh in range(2):
                h = 2 * pp + hh
                s = jnp.dot(kb, q_aug[h], preferred_element_type=F32)
                s_ref[h, pl.ds(off, group), :] = s
                out.append(jnp.maximum(ms[h], slab_max(s)))
        return tuple(out)

    ms = lax.fori_loop(0, n_groups, score_group, tuple(m_init))

    m_fin = []
    for h in range(len(heads)):
        s_ref[h, pl.ds(own, blk), :] = sdiag_ref[h]
        m_fin.append(jnp.max(ms[h], axis=0, keepdims=True))

    def exp_group(g):
        off = pl.multiple_of(g * group, group)
        for h in range(len(heads)):
            p_ref[h] = jnp.exp(s_ref[h, pl.ds(off, group), :] - m_fin[h]).astype(BF16)

    def value_group(g):
        off = pl.multiple_of(g * group, group)
        for pp in range(n_pairs):
            vb = vt_ref[pp, :, pl.ds(off, group)]
            for hh in range(2):
                h = 2 * pp + hh
                acc_ref[h] += jnp.dot(vb, p_ref[h], preferred_element_type=F32)

    def pipelined(g, c):
        value_group(g - 1)
        exp_group(g)
        return c

    exp_group(0)
    lax.fori_loop(1, n_groups, pipelined, 0)
    value_group(n_groups - 1)

    half = lanes // 2
    outs = []
    for pp in range(n_pairs):
        outs.append(acc_ref[2 * pp, :half, :] / acc_ref[2 * pp, lanes:lanes + 1, :])
        outs.append(acc_ref[2 * pp + 1, half:lanes, :] / acc_ref[2 * pp + 1, lanes:lanes + 1, :])
    o_ref[0] = jnp.concatenate(outs, axis=0).T.astype(BF16)


def _moba(q, k, v):
    B, S, _ = q.shape
    nb = S // MOBA_BLOCK
    assert nb % MOBA_GROUP == 0
    nbp = -(-nb // (2 * V7X_SUBLANES)) * (2 * V7X_SUBLANES)
    n_pairs = ATT_WIDTH // V7X_LANES
    pp = MOBA_PAIRS
    assert n_pairs % pp == 0
    vt_rows = V7X_LANES + 2 * V7X_SUBLANES
    qspec = pl.BlockSpec((1, MOBA_BLOCK, pp * V7X_LANES), lambda b, p, i: (b, i, p))
    kvspec = pl.BlockSpec((1, S, pp * V7X_LANES), lambda b, p, i: (b, 0, p))
    return pl.pallas_call(
        functools.partial(_moba_kernel, nb=nb),
        grid=(B, n_pairs // pp, nb),
        in_specs=[qspec, kvspec, kvspec],
        out_specs=qspec,
        out_shape=jax.ShapeDtypeStruct((B, S, ATT_WIDTH), BF16),
        scratch_shapes=[pltpu.VMEM((pp, S, 2 * V7X_LANES), BF16),
                        pltpu.VMEM((pp, nbp, 3 * V7X_LANES), BF16),
                        pltpu.VMEM((pp, vt_rows, S), BF16),
                        pltpu.VMEM((2 * pp, vt_rows, MOBA_BLOCK), F32),
                        pltpu.VMEM((2 * pp, S, MOBA_BLOCK), F32),
                        pltpu.VMEM((2 * pp, MOBA_BLOCK, MOBA_BLOCK), F32),
                        pltpu.VMEM((2 * pp, MOBA_GROUP * MOBA_BLOCK, MOBA_BLOCK), BF16)],
        compiler_params=_cparams("parallel", "parallel", "arbitrary"),
        name="moba",
    )(q, k, v)


def _hgrn_level_sizes(chunk):
    return [chunk >> (i + 1) for i in range(int(math.log2(chunk)))]


def _hgrn_constants(chunk):
    t = np.arange(chunk)
    mats = [(t[None, :] <= t[:, None]),
            (t[None, :] > t[:, None])]
    qrows, pmasks = [], []
    for bs in _hgrn_level_sizes(chunk):
        blk = t // bs
        odd = (blk % 2) == 1
        lo, hi = blk * bs, (blk + 1) * bs
        u = t[None, :]
        m_odd = (u >= lo[:, None]) & (u <= t[:, None])
        m_even = (u > t[:, None]) & (u < hi[:, None])
        mats.append(np.where(odd[:, None], m_odd, m_even))
        qrows.append(odd)
        pmasks.append(odd[:, None] & (blk[None, :] == blk[:, None] - 1))
    pmasks.append(t[None, :] == t[:, None])
    summat = np.concatenate(mats, axis=0).astype(np.float32)
    qrow = np.stack(qrows, axis=0).astype(np.float32)
    pmask = np.stack(pmasks, axis=0).astype(np.float32)
    return summat, qrow, pmask


def _hgrn_kernel(q_ref, f_ref, i_ref, g_ref, lb_ref, nw_ref, sm_ref, qrow_ref, pm_ref,
                 o_ref, state_ref, *, rows):
    C = HGRN_CHUNK
    n_levels = qrow_ref.shape[0]

    @pl.when(pl.program_id(1) == 0)
    def _():
        state_ref[...] = jnp.zeros_like(state_ref)

    lb = lb_ref[...]
    summat = sm_ref[...]
    for c in range(rows // C):
        r0 = c * C
        fg = lb + (1.0 - lb) * jax.nn.sigmoid(f_ref[0, r0:r0 + C, :])
        logf = jnp.log(fg)
        hi = logf.astype(BF16)
        rem = logf - hi.astype(F32)
        mid = rem.astype(BF16)
        lo = (rem - mid.astype(F32)).astype(BF16)
        sums = jnp.dot(summat, jnp.concatenate([hi, mid, lo], axis=0),
                       preferred_element_type=F32)
        for h in range(HGRN_HEADS):
            ls = slice(h * HGRN_DIM, (h + 1) * HGRN_DIM)
            qf = jax.nn.silu(q_ref[0, r0:r0 + C, ls].astype(F32))
            kf = 1.0 - fg[:, ls]
            iv = i_ref[0, r0:r0 + C, ls]
            bcum = sums[0:C, ls]
            bsuf = sums[C:2 * C, ls]
            att = _nt_dot(qf.astype(BF16), kf.astype(BF16)) * pm_ref[n_levels]
            for lv in range(n_levels):
                w = jnp.exp(sums[(2 + lv) * C:(3 + lv) * C, ls])
                qrow = qrow_ref[lv]
                z = (jnp.where(qrow > 0.5, qf, kf) * w).astype(BF16)
                att = att + _nt_dot(z, z) * pm_ref[lv]
            o = jnp.dot(att.astype(BF16), iv, preferred_element_type=F32)
            st = state_ref[h]
            o = o + _nt_dot((qf * jnp.exp(bcum)).astype(BF16), st.astype(BF16))
            kdec = (kf * jnp.exp(bsuf)).astype(BF16)
            state_ref[h] = st * jnp.exp(bcum[C - 1:C, :]) + _tn_dot(iv, kdec)
            o = o * lax.rsqrt(jnp.mean(o * o, axis=-1, keepdims=True) + RMS_EPS)
            o = o * nw_ref[:, ls] * jax.nn.silu(g_ref[0, r0:r0 + C, ls].astype(F32))
            o_ref[0, r0:r0 + C, ls] = o.astype(BF16)


def _hgrn(qb, fb, ib, gb, lb, norm_w):
    B, S, W = qb.shape
    rows = min(HGRN_ROWS, S)
    summat, qrow, pmask = _hgrn_constants(HGRN_CHUNK)
    summat = jnp.asarray(np.concatenate([summat] * 3, axis=1), BF16)
    n_levels = qrow.shape[0]
    blk = pl.BlockSpec((1, rows, W), lambda b, s: (b, s, 0))
    vec = pl.BlockSpec((1, W), lambda b, s: (0, 0))
    const = lambda a: pl.BlockSpec(a.shape, lambda b, s: (0,) * a.ndim)
    qrow3 = qrow.reshape(n_levels, HGRN_CHUNK, 1)
    return pl.pallas_call(
        functools.partial(_hgrn_kernel, rows=rows),
        grid=(B, S // rows),
        in_specs=[blk, blk, blk, blk, vec, vec, const(summat), const(qrow3), const(pmask)],
        out_specs=blk,
        out_shape=jax.ShapeDtypeStruct((B, S, W), BF16),
        scratch_shapes=[pltpu.VMEM((HGRN_HEADS, HGRN_DIM, HGRN_DIM), F32)],
        compiler_params=_cparams("parallel", "arbitrary"),
        name="hgrn",
    )(qb, fb, ib, gb, lb, norm_w, jnp.asarray(summat), jnp.asarray(qrow3), jnp.asarray(pmask))


def _mix_route_kernel(ya_ref, yb_ref, ga_ref, gb_ref, x_ref, wa_ref, wb_ref, wo_ref, nw_ref,
                      rw_ref, rb_ref, tri_ref,
                      x1_ref, h2_ref, e_ref, w_ref, rank_ref, cnt_ref, carry_ref):
    @pl.when(pl.program_id(0) == 0)
    def _():
        carry_ref[...] = jnp.zeros_like(carry_ref)

    ua = jnp.dot(ya_ref[...], wa_ref[...], preferred_element_type=F32)
    ub = jnp.dot(yb_ref[...], wb_ref[...], preferred_element_type=F32)
    mixed = (jax.nn.sigmoid(ga_ref[...].astype(F32)) * ua
             + jax.nn.sigmoid(gb_ref[...].astype(F32)) * ub)
    x1 = x_ref[...] + jnp.dot(mixed.astype(BF16), wo_ref[...], preferred_element_type=F32)
    x1_ref[...] = x1
    h2 = x1 * lax.rsqrt(jnp.mean(x1 * x1, axis=-1, keepdims=True) + RMS_EPS) * nw_ref[...]
    half = h2.shape[1] // 2
    h2_ref[...] = _pack_bf16_pair(h2[:, :half], h2[:, half:])

    tm = x1.shape[0]
    logits = _nt_dot(rw_ref[...], h2, precision=lax.Precision.HIGHEST) + rb_ref[...]
    eid = lax.broadcasted_iota(jnp.int32, (N_EXPERTS, tm), 0)
    work = logits
    es, vs = [], []
    for _ in range(TOP_K):
        mx = jnp.max(work, axis=0, keepdims=True)
        idx = jnp.min(jnp.where(work == mx, eid, N_EXPERTS), axis=0, keepdims=True)
        es.append(idx)
        vs.append(mx)
        work = jnp.where(eid == idx, -jnp.inf, work)
    ex = [jnp.exp(v - vs[0]) for v in vs]
    den = ex[0] + ex[1] + ex[2] + ex[3]
    multi = jnp.zeros((N_EXPERTS, tm), F32)
    for k in range(TOP_K):
        multi = multi + jnp.where(eid == es[k], 1.0, 0.0)
    before = jnp.dot(multi.astype(BF16), tri_ref[...], preferred_element_type=F32) + carry_ref[...]
    for k in range(TOP_K):
        e_ref[k:k + 1, :] = es[k]
        w_ref[k:k + 1, :] = ex[k] / den
        rank_ref[k:k + 1, :] = jnp.sum(jnp.where(eid == es[k], before, 0.0), axis=0,
                                       keepdims=True).astype(jnp.int32)
    carry_ref[...] = carry_ref[...] + jnp.sum(multi, axis=1, keepdims=True)
    cnt_ref[...] = jnp.broadcast_to(carry_ref[...], cnt_ref.shape)


def _mix_route(ya, yb, ga, gb, x2, wa, wb, wo, norm_w, rw_t, rb, row0, rows):
    D = x2.shape[1]
    T = rows
    tm = min(MIX_ROWS, T)
    row = lambda w: pl.BlockSpec((tm, w), lambda i: (i + row0 // tm, 0))
    out_row = pl.BlockSpec((tm, D), lambda i: (i, 0))
    whole = pl.BlockSpec(memory_space=pltpu.VMEM)
    kt = pl.BlockSpec((TOP_K, tm), lambda i: (0, i))
    tri = jnp.asarray(np.triu(np.ones((tm, tm), np.float32), 1), BF16)
    return pl.pallas_call(
        _mix_route_kernel,
        grid=(T // tm,),
        in_specs=[row(ATT_WIDTH), row(HGRN_WIDTH), row(D), row(D), row(D),
                  whole, whole, whole, pl.BlockSpec((1, D), lambda i: (0, 0)),
                  whole, whole, whole],
        out_specs=[out_row, pl.BlockSpec((tm, D // 2), lambda i: (i, 0)), kt, kt, kt,
                   pl.BlockSpec((N_EXPERTS, V7X_LANES), lambda i: (0, 0))],
        out_shape=[jax.ShapeDtypeStruct((T, D), F32), jax.ShapeDtypeStruct((T, D // 2), jnp.uint32),
                   jax.ShapeDtypeStruct((TOP_K, T), jnp.int32),
                   jax.ShapeDtypeStruct((TOP_K, T), F32),
                   jax.ShapeDtypeStruct((TOP_K, T), jnp.int32),
                   jax.ShapeDtypeStruct((N_EXPERTS, V7X_LANES), F32)],
        scratch_shapes=[pltpu.VMEM((N_EXPERTS, 1), F32)],
        compiler_params=_cparams("arbitrary"),
        name="mix_route",
    )(ya, yb, ga, gb, x2, wa, wb, wo, norm_w, rw_t, rb, tri)


def _sc_worker_id():
    return lax.axis_index("s") * V7X_SC_CORES + lax.axis_index("c")


def _sc_kernel(body, out_rows, like, window, name):
    mesh = plsc.VectorSubcoreMesh(core_axis_name="c", subcore_axis_name="s")
    return pl.kernel(
        body, mesh=mesh,
        out_type=jax.ShapeDtypeStruct((out_rows,) + like.shape[1:], like.dtype),
        scratch_types=[pltpu.VMEM((window,), jnp.int32),
                       pltpu.VMEM((window,) + like.shape[1:], like.dtype)],
        name=name)


def _scatter_rows(dest, h2):
    T = dest.shape[1]
    n_workers = V7X_SC_CORES * V7X_SC_SUBCORES
    per_worker = T // n_workers
    window = min(SC_ROWS, per_worker)

    def body(dest_hbm, src_hbm, dst_hbm, idx_v, rows_v):
        base = _sc_worker_id() * per_worker

        @pl.loop(0, per_worker // window)
        def _(c):
            t0 = base + c * window
            pltpu.sync_copy(src_hbm.at[pl.ds(t0, window)], rows_v)
            for k in range(TOP_K):
                pltpu.sync_copy(dest_hbm.at[pl.ds(k * T + t0, window)], idx_v)
                pltpu.sync_copy(rows_v, dst_hbm.at[idx_v])

    return _sc_kernel(body, TOP_K * T, h2, window, "scatter_rows")(dest.reshape(-1), h2)


def _gather_rows(dest, yg):
    T = dest.shape[1]
    n_workers = V7X_SC_CORES * V7X_SC_SUBCORES
    per_worker = TOP_K * T // n_workers
    window = min(SC_ROWS, per_worker)

    def body(dest_hbm, src_hbm, dst_hbm, idx_v, rows_v):
        base = _sc_worker_id() * per_worker

        @pl.loop(0, per_worker // window)
        def _(c):
            r0 = base + c * window
            pltpu.sync_copy(dest_hbm.at[pl.ds(r0, window)], idx_v)
            pltpu.sync_copy(src_hbm.at[idx_v], rows_v)
            pltpu.sync_copy(rows_v, dst_hbm.at[pl.ds(r0, window)])

    return _sc_kernel(body, TOP_K * T, yg, window, "gather_rows")(dest.reshape(-1), yg)


def _prep_gate_up_kernel(w_ref, perm_ref, wg_ref, wu_ref):
    half = V7X_MXU_DIM // 2
    for g in range(w_ref.shape[2] // V7X_MXU_DIM):
        wb = w_ref[0, :, g * V7X_MXU_DIM:(g + 1) * V7X_MXU_DIM].astype(BF16)
        d = jnp.dot(wb, perm_ref[...], preferred_element_type=F32).astype(BF16)
        wg_ref[0, :, g * half:(g + 1) * half] = d[:, :half]
        wu_ref[0, :, g * half:(g + 1) * half] = d[:, half:]


def _prep_gate_up(w_gate_up):
    E, D, F2 = w_gate_up.shape
    rows = min(PREP_ROWS, D)
    i = np.arange(V7X_MXU_DIM)
    src = np.where(i < V7X_MXU_DIM // 2, 2 * i, 2 * (i - V7X_MXU_DIM // 2) + 1)
    perm = np.zeros((V7X_MXU_DIM, V7X_MXU_DIM), np.float32)
    perm[src, i] = 1.0
    out = pl.BlockSpec((1, rows, F2 // 2), lambda e, r: (e, r, 0))
    return pl.pallas_call(
        _prep_gate_up_kernel,
        grid=(E, D // rows),
        in_specs=[pl.BlockSpec((1, rows, F2), lambda e, r: (e, r, 0)),
                  pl.BlockSpec((V7X_MXU_DIM, V7X_MXU_DIM), lambda e, r: (0, 0))],
        out_specs=[out, out],
        out_shape=[jax.ShapeDtypeStruct((E, D, F2 // 2), BF16)] * 2,
        compiler_params=_cparams("parallel", "parallel"),
        name="prep_gate_up",
    )(w_gate_up, jnp.asarray(perm, BF16))


def _experts_kernel(blk_ref, exp_ref, lo_ref, hi_ref, first_ref, slot_ref, efirst_ref, enext_ref,
                    x_ref, wg_hbm, wu_hbm, bg_ref, bu_ref, wd_hbm, bd_ref, y_ref,
                    wg_buf, wu_buf, wd_buf, sems):
    w = pl.program_id(0)
    lo = lo_ref[w]
    hi = hi_ref[w]
    slot = slot_ref[w]

    tm = x_ref.shape[0]
    n_out = V7X_MXU_DIM

    def weight_copies(e, s):
        return [pltpu.make_async_copy(hbm.at[e], buf.at[s], sems.at[s, i])
                for i, (hbm, buf) in enumerate(((wg_hbm, wg_buf), (wu_hbm, wu_buf), (wd_hbm, wd_buf)))]

    @pl.when(w == 0)
    def _():
        for c in weight_copies(exp_ref[0], slot):
            c.start()

    @pl.when(efirst_ref[w] == 1)
    def _():
        for c in weight_copies(exp_ref[w], slot):
            c.wait()

        @pl.when(enext_ref[w] >= 0)
        def _():
            for c in weight_copies(enext_ref[w], 1 - slot):
                c.start()

    def ffn(merge):
        x = jnp.concatenate(_unpack_bf16_pair(x_ref[...]), axis=1).astype(BF16)
        g = jnp.dot(x, wg_buf[slot], preferred_element_type=F32) + bg_ref[0]
        u = jnp.dot(x, wu_buf[slot], preferred_element_type=F32) + bu_ref[0]
        g = jnp.minimum(g, SWIGLU_LIMIT)
        u = jnp.clip(u, -SWIGLU_LIMIT, SWIGLU_LIMIT)
        act = ((u + 1.0) * (g * jax.nn.sigmoid(g * SWIGLU_ALPHA))).astype(BF16)
        half = y_ref.shape[1]

        def down(c0):
            cs = slice(c0, c0 + n_out)
            return (jnp.dot(act, wd_buf[slot, :, cs], preferred_element_type=F32)
                    + bd_ref[0, :, cs])

        for c in range(half // n_out):
            cs = slice(c * n_out, (c + 1) * n_out)
            y_ref[:, cs] = merge(_pack_bf16_pair(down(c * n_out), down(half + c * n_out)), cs)

    whole = hi - lo == tm
    r = lax.broadcasted_iota(jnp.int32, (tm, n_out), 0)
    mine = (r >= lo) & (r < hi)

    @pl.when(whole)
    def _():
        ffn(lambda y, cs: y)

    @pl.when((lo < hi) & jnp.logical_not(whole) & (first_ref[w] == 1))
    def _():
        ffn(lambda y, cs: jnp.where(mine, y, jnp.uint32(0)))

    @pl.when((lo < hi) & jnp.logical_not(whole) & (first_ref[w] == 0))
    def _():
        ffn(lambda y, cs: jnp.where(mine, y, y_ref[:, cs]))


def _work_items(counts, n_rows, tm):
    nblk = n_rows // tm
    n_items = nblk + N_EXPERTS - 1
    end = jnp.cumsum(counts)
    start = end - counts
    fb = start // tm
    nitems = jnp.where(counts > 0, (end - 1) // tm - fb + 1, 0)
    item_end = jnp.cumsum(nitems)
    item_start = item_end - nitems
    w = jnp.arange(n_items, dtype=jnp.int32)
    valid = w < item_end[-1]
    wc = jnp.minimum(w, item_end[-1] - 1)
    e = jnp.sum(wc[:, None] >= item_end[None, :], axis=1).astype(jnp.int32)
    e = jnp.minimum(e, N_EXPERTS - 1)
    onehot = e[:, None] == jnp.arange(N_EXPERTS, dtype=jnp.int32)[None, :]
    pick = lambda table: jnp.sum(jnp.where(onehot, table[None, :], 0), axis=1)
    blk = (pick(fb) + (wc - pick(item_start))).astype(jnp.int32)
    lo = jnp.maximum(pick(start), blk * tm) - blk * tm
    hi = jnp.minimum(pick(end), (blk + 1) * tm) - blk * tm
    lo = jnp.where(valid, lo, 0).astype(jnp.int32)
    hi = jnp.where(valid, hi, 0).astype(jnp.int32)
    first = jnp.concatenate([jnp.ones((1,), jnp.int32),
                             (blk[1:] != blk[:-1]).astype(jnp.int32)])
    used = counts > 0
    ids = jnp.arange(N_EXPERTS, dtype=jnp.int32)
    slot = pick((jnp.cumsum(used) - 1) % 2).astype(jnp.int32)
    efirst = (valid & (w == pick(item_start))).astype(jnp.int32)
    later = jnp.where(used[None, :] & (ids[None, :] > ids[:, None]), ids[None, :], N_EXPERTS)
    next_used = jnp.min(later, axis=1)
    enext = pick(jnp.where(next_used < N_EXPERTS, next_used, -1)).astype(jnp.int32)
    return blk, e, lo, hi, first, slot, efirst, enext


def _experts(xg, counts, wg, wu, bg, bu, wd, bd):
    A = xg.shape[0]
    D = 2 * xg.shape[1]
    tm = min(EXPERT_ROWS, A)
    F = wg.shape[2]
    items = _work_items(counts, A, tm)
    n_items = A // tm + N_EXPERTS - 1
    xs = pl.BlockSpec((tm, D // 2), lambda w, blk, e, *_: (blk[w], 0))
    bias = lambda c: pl.BlockSpec((1, 1, c), lambda w, blk, e, *_: (e[w], 0, 0))
    hbm = pl.BlockSpec(memory_space=pl.ANY)
    return pl.pallas_call(
        _experts_kernel,
        grid_spec=pltpu.PrefetchScalarGridSpec(
            num_scalar_prefetch=len(items),
            grid=(n_items,),
            in_specs=[xs, hbm, hbm, bias(F), bias(F), hbm, bias(D)],
            out_specs=xs,
            scratch_shapes=[pltpu.VMEM((2, D, F), BF16), pltpu.VMEM((2, D, F), BF16),
                            pltpu.VMEM((2, F, D), BF16), pltpu.SemaphoreType.DMA((2, 3))],
        ),
        out_shape=jax.ShapeDtypeStruct(xg.shape, jnp.uint32),
        compiler_params=_cparams("arbitrary"),
        name="experts",
    )(*items, xg, wg, wu, bg, bu, wd, bd)


def _combine_kernel(x1_ref, yk_ref, w_ref, nw_ref, *rest):
    o_ref = rest[-1]
    w = w_ref[...]
    x = x1_ref[...]
    for k in range(TOP_K):
        x = x + jnp.concatenate(_unpack_bf16_pair(yk_ref[k]), axis=1) * w[:, k:k + 1]
    o_ref[...] = x * lax.rsqrt(jnp.mean(x * x, axis=-1, keepdims=True) + RMS_EPS) * nw_ref[...]


def _combine(x1, yk, w_tk, norm_w, out_rows, row0, prev_out):
    Tg, D = x1.shape
    tm = min(COMBINE_ROWS, Tg)
    row = pl.BlockSpec((tm, D), lambda i: (i, 0))
    in_specs = [row, pl.BlockSpec((TOP_K, tm, D // 2), lambda i: (0, i, 0)),
                pl.BlockSpec((tm, TOP_K), lambda i: (i, 0)),
                pl.BlockSpec((1, D), lambda i: (0, 0))]
    args = [x1, yk, w_tk, norm_w]
    aliases = {}
    if prev_out is not None:
        in_specs.append(pl.BlockSpec(memory_space=pl.ANY))
        args.append(prev_out)
        aliases = {len(args) - 1: 0}
    return pl.pallas_call(
        _combine_kernel,
        grid=(Tg // tm,),
        in_specs=in_specs,
        out_specs=pl.BlockSpec((tm, D), lambda i: (i + row0 // tm, 0)),
        out_shape=jax.ShapeDtypeStruct((out_rows, D), F32),
        input_output_aliases=aliases,
        compiler_params=_cparams("parallel"),
        name="combine",
    )(*args)


def _qk_column_order():
    half = ATT_HEAD_DIM // 2
    order = []
    for p in range(ATT_HEADS // 2):
        for part in range(2):
            for h in (2 * p, 2 * p + 1):
                order.extend(range(h * ATT_HEAD_DIM + part * half, h * ATT_HEAD_DIM + (part + 1) * half))
    return np.asarray(order, np.int32)


def _rope_tables(seq):
    half = ATT_HEAD_DIM // 2
    inv = ROPE_THETA ** (-(jnp.arange(half, dtype=F32) * 2.0 / ATT_HEAD_DIM))
    ang = jnp.arange(seq, dtype=F32)[:, None] * inv[None, :]
    cos, sin = jnp.cos(ang), jnp.sin(ang)
    return (jnp.concatenate([cos, cos, cos, cos], axis=1),
            jnp.concatenate([-sin, -sin, sin, sin], axis=1))


def kernel(x, norm1_w, w_in, moba_up, hgrn_lb_logits, hgrn_norm_w, hgrn_up, w_out, norm2_w,
           router_w, router_b, w_gate_up, b_gate_up, w_down, b_down, final_norm_w):
    B, S, D = x.shape
    T = B * S
    assert S % MOBA_BLOCK == 0 and w_in.shape[0] == 1
    x2 = x.reshape(T, D)

    perm = _qk_column_order()
    w0 = w_in[0]
    w_in_p = jnp.concatenate([w0[:, :ATT_WIDTH][:, perm], w0[:, ATT_WIDTH:2 * ATT_WIDTH][:, perm],
                              w0[:, 2 * ATT_WIDTH:]], axis=1).astype(BF16)
    cos_t, sin_t = _rope_tables(S)
    lb = jnp.cumsum(jax.nn.softmax(hgrn_lb_logits.astype(F32), axis=0), axis=0)[0:1]

    qa, ka, va, qb, fb, ib, gb, ga, gtb = _in_proj(x2, norm1_w, w_in_p, cos_t, sin_t, S)
    r3 = lambda a: a.reshape(B, S, a.shape[1])
    ya = _moba(r3(qa), r3(ka), r3(va)).reshape(T, ATT_WIDTH)
    yb = _hgrn(r3(qb), r3(fb), r3(ib), r3(gb), lb, hgrn_norm_w).reshape(T, HGRN_WIDTH)

    mix_w = (moba_up[0].astype(BF16), hgrn_up[0].astype(BF16), w_out[0].astype(BF16), norm2_w,
             router_w[0].T, router_b[0][:, None])
    wg, wu = _prep_gate_up(w_gate_up[0])
    expert_w = (wg, wu, b_gate_up[0][:, None, 0::2], b_gate_up[0][:, None, 1::2],
                w_down[0].astype(BF16), b_down[0][:, None, :])

    Tg = T // MOE_TOKEN_GROUPS
    out = None
    for g in range(MOE_TOKEN_GROUPS):
        x1, h2, top_e, top_w, rank, cnt = _mix_route(ya, yb, ga, gtb, x2, *mix_w, g * Tg, Tg)
        counts = cnt[:, 0].astype(jnp.int32)
        start = jnp.cumsum(counts) - counts
        dest = rank + jnp.sum(jnp.where(top_e[:, :, None] == jnp.arange(N_EXPERTS)[None, None, :],
                                        start[None, None, :], 0), axis=-1)
        xg = _scatter_rows(dest, h2)
        yg = _experts(xg, counts, *expert_w)
        yk = _gather_rows(dest, yg).reshape(TOP_K, Tg, D // 2)
        out = _combine(x1, yk, top_w.T, final_norm_w[None, :], T, g * Tg, out)
    return out.reshape(B, S, D)
```

```python
import functools
import math

import numpy as np
import jax
import jax.numpy as jnp
from jax import lax
from jax.experimental import pallas as pl
from jax.experimental.pallas import tpu as pltpu
from jax.experimental.pallas import tpu_sc as plsc

ATT_HEADS = 8
ATT_HEAD_DIM = 64
ATT_WIDTH = ATT_HEADS * ATT_HEAD_DIM
MOBA_BLOCK = 256
MOBA_TOPK = 3
ROPE_THETA = 10000.0
HGRN_HEADS = 4
HGRN_DIM = 128
HGRN_WIDTH = HGRN_HEADS * HGRN_DIM
HGRN_CHUNK = 64
N_EXPERTS = 32
TOP_K = 4
SWIGLU_LIMIT = 7.0
SWIGLU_ALPHA = 1.702
RMS_EPS = 1e-6
NEG = -1e30

V7X_LANES = 128
V7X_SUBLANES = 8
V7X_MXU_DIM = 256
V7X_VMEM_LIMIT_BYTES = 56 * 1024 * 1024
V7X_SC_CORES = 2
V7X_SC_SUBCORES = 16

MOBA_PAIRS = 2
MOBA_GROUP = 4
PROJ_ROWS = 512
PROJ_COLS = 512
HGRN_ROWS = 256
MIX_ROWS = 512
EXPERT_ROWS = 256
MOE_TOKEN_GROUPS = 2
SC_ROWS = 64
COMBINE_ROWS = 256
PREP_ROWS = 256

F32 = jnp.float32
BF16 = jnp.bfloat16


def _nt_dot(a, b, precision=None):
    return lax.dot_general(a, b, (((1,), (1,)), ((), ())), precision=precision,
                           preferred_element_type=F32)


def _tn_dot(a, b, precision=None):
    return lax.dot_general(a, b, (((0,), (0,)), ((), ())), precision=precision,
                           preferred_element_type=F32)


def _cparams(*sem):
    return pltpu.CompilerParams(dimension_semantics=sem, vmem_limit_bytes=V7X_VMEM_LIMIT_BYTES)


def _pack_bf16_pair(lo, hi):
    lo_bits = lax.bitcast_convert_type(lo.astype(BF16).astype(F32), jnp.uint32)
    hi_bits = lax.bitcast_convert_type(hi.astype(BF16).astype(F32), jnp.uint32)
    return (lo_bits >> 16) | hi_bits


def _unpack_bf16_pair(word):
    lo = lax.bitcast_convert_type(word << 16, F32)
    hi = lax.bitcast_convert_type(word & jnp.uint32(0xFFFF0000), F32)
    return lo, hi


def _in_proj_kernel(x_ref, nw_ref, w_ref, cos_ref, sin_ref,
                    qa_ref, ka_ref, va_ref, qb_ref, fb_ref, ib_ref, gb_ref, ga_ref, gtb_ref):
    x = x_ref[...]
    h = x * lax.rsqrt(jnp.mean(x * x, axis=-1, keepdims=True) + RMS_EPS) * nw_ref[...]
    h = h.astype(BF16)
    cos = cos_ref[...]
    sin = sin_ref[...]

    def proj(c):
        return jnp.dot(h, w_ref[:, c * PROJ_COLS:(c + 1) * PROJ_COLS], preferred_element_type=F32)

    def rope(t):
        out = []
        for j in range(PROJ_COLS // V7X_LANES):
            tj = t[:, j * V7X_LANES:(j + 1) * V7X_LANES]
            out.append(tj * cos + pltpu.roll(tj, V7X_LANES // 2, 1) * sin)
        return jnp.concatenate(out, axis=1)

    qa_ref[...] = (rope(proj(0)) * (ATT_HEAD_DIM ** -0.5)).astype(BF16)
    ka_ref[...] = rope(proj(1)).astype(BF16)
    va_ref[...] = proj(2).astype(BF16)
    qb_ref[...] = proj(3).astype(BF16)
    fb_ref[...] = proj(4)
    ib_ref[...] = proj(5).astype(BF16)
    gb_ref[...] = proj(6).astype(BF16)
    ga_ref[:, :PROJ_COLS] = proj(7).astype(BF16)
    ga_ref[:, PROJ_COLS:] = proj(8).astype(BF16)
    gtb_ref[:, :PROJ_COLS] = proj(9).astype(BF16)
    gtb_ref[:, PROJ_COLS:] = proj(10).astype(BF16)


def _in_proj(x2, norm_w, w_in_bf16, cos_t, sin_t, seq):
    T, D = x2.shape
    tm = min(PROJ_ROWS, seq)
    n_seq_tiles = seq // tm
    row = lambda w: pl.BlockSpec((tm, w), lambda i: (i, 0))
    tab = pl.BlockSpec((tm, V7X_LANES), lambda i: (i % n_seq_tiles, 0))
    widths = [ATT_WIDTH] * 3 + [HGRN_WIDTH] * 4 + [D, D]
    dtypes = [BF16, BF16, BF16, BF16, F32, BF16, BF16, BF16, BF16]
    return pl.pallas_call(
        _in_proj_kernel,
        grid=(T // tm,),
        in_specs=[row(D), pl.BlockSpec((1, D), lambda i: (0, 0)),
                  pl.BlockSpec(memory_space=pltpu.VMEM), tab, tab],
        out_specs=[row(w) for w in widths],
        out_shape=[jax.ShapeDtypeStruct((T, w), dt) for w, dt in zip(widths, dtypes)],
        compiler_params=_cparams("parallel"),
        name="in_proj",
    )(x2, norm_w, w_in_bf16, cos_t, sin_t)


def _moba_kernel(q_ref, k_ref, v_ref, o_ref, kaug_ref, kmean_ref, vt_ref, acc_ref, s_ref,
                 sdiag_ref, p_ref, *, nb):
    qi = pl.program_id(2)
    blk = MOBA_BLOCK
    lanes = V7X_LANES
    n_pairs = kaug_ref.shape[0]
    heads = [(pp, hh) for pp in range(n_pairs) for hh in range(2)]
    nbp = kmean_ref.shape[1]

    @pl.when(qi == 0)
    def _():
        rowb = lax.broadcasted_iota(jnp.int32, (nb * blk, lanes), 0) // blk
        col = lax.broadcasted_iota(jnp.int32, (nb * blk, lanes), 1)
        onehot = jnp.where(rowb == col, 1.0, 0.0).astype(BF16)
        for pp in range(n_pairs):
            pl_ = slice(pp * lanes, (pp + 1) * lanes)
            kaug_ref[pp, :, :lanes] = k_ref[0, :, pl_]
            kaug_ref[pp, :, lanes:] = onehot
            means = [jnp.sum(k_ref[0, n * blk:(n + 1) * blk, pl_].astype(F32), axis=0,
                             keepdims=True) * (1.0 / blk) for n in range(nb)]
            km = jnp.concatenate(means + [jnp.zeros((nbp - nb, lanes), F32)] * (nbp > nb), axis=0)
            hi = km.astype(BF16)
            rem = km - hi.astype(F32)
            mid = rem.astype(BF16)
            lo = (rem - mid.astype(F32)).astype(BF16)
            kmean_ref[pp] = jnp.concatenate([hi, mid, lo], axis=1)
            for c in range(nb):
                vt = v_ref[0, c * blk:(c + 1) * blk, pl_].astype(F32).T
                vt_ref[pp, :lanes, c * blk:(c + 1) * blk] = vt.astype(BF16)
            vt_ref[pp, lanes:, :] = jnp.ones((vt_ref.shape[1] - lanes, nb * blk), BF16)

    feat = lax.broadcasted_iota(jnp.int32, (lanes, blk), 0)
    key_i = lax.broadcasted_iota(jnp.int32, (blk, blk), 0)
    qry_i = lax.broadcasted_iota(jnp.int32, (blk, blk), 1)
    own = pl.multiple_of(qi * blk, blk)
    blk_id = lax.broadcasted_iota(jnp.int32, (nbp, 2 * blk), 0)
    slab = 2 * V7X_SUBLANES

    def slab_max(s):
        return jnp.max(s.reshape(s.shape[0] // slab, slab, blk), axis=0)

    q_aug, m_init = [], []
    for pp in range(n_pairs):
        qt = q_ref[0, :, pp * lanes:(pp + 1) * lanes].astype(F32).T
        k_own = kaug_ref[pp, pl.ds(own, blk), :lanes]
        qhs = [jnp.where((feat // (ATT_HEAD_DIM // 2)) % 2 == hh, qt, 0.0).astype(BF16)
               for hh in range(2)]
        q2 = jnp.concatenate(qhs, axis=1)
        gate = jnp.dot(kmean_ref[pp], jnp.concatenate([q2, q2, q2], axis=0),
                       preferred_element_type=F32)
        gate = jnp.where(blk_id < qi, gate, NEG)
        beaten = jnp.zeros((nbp, 2 * blk), F32)
        for n in range(nb):
            gn = gate[n:n + 1, :]
            wins = (gn > gate) | ((gn == gate) & (blk_id > n))
            beaten = beaten + jnp.where(wins, 1.0, 0.0)
        sel = (beaten < MOBA_TOPK) & (blk_id < qi)
        bias = jnp.where(sel, 0.0, NEG).astype(BF16)
        bias = jnp.concatenate([bias, jnp.zeros((lanes - nbp, 2 * blk), BF16)], axis=0)
        for hh in range(2):
            h = 2 * pp + hh
            q_aug.append(jnp.concatenate([qhs[hh], bias[:, hh * blk:(hh + 1) * blk]], axis=0))
            s = jnp.dot(k_own, qhs[hh], preferred_element_type=F32)
            s = jnp.where(key_i <= qry_i, s, NEG)
            sdiag_ref[h] = s
            m_init.append(slab_max(s))
            acc_ref[h] = jnp.zeros(acc_ref.shape[1:], F32)

    group = MOBA_GROUP * blk
    n_groups = (qi + MOBA_GROUP) // MOBA_GROUP

    def score_group(g, ms):
        off = pl.multiple_of(g * group, group)
        out = []
        for pp in range(n_pairs):
            kb = kaug_ref[pp, pl.ds(off, group), :]
            for hh in range(2):
                h = 2 * pp + hh
                s = jnp.dot(kb, q_aug[h], preferred_element_type=F32)
                s_ref[h, pl.ds(off, group), :] = s
                out.append(jnp.maximum(ms[h], slab_max(s)))
        return tuple(out)

    ms = lax.fori_loop(0, n_groups, score_group, tuple(m_init))

    m_fin = []
    for h in range(len(heads)):
        s_ref[h, pl.ds(own, blk), :] = sdiag_ref[h]
        m_fin.append(jnp.max(ms[h], axis=0, keepdims=True))

    def exp_group(g):
        off = pl.multiple_of(g * group, group)
        for h in range(len(heads)):
            p_ref[h] = jnp.exp(s_ref[h, pl.ds(off, group), :] - m_fin[h]).astype(BF16)

    def value_group(g):
        off = pl.multiple_of(g * group, group)
        for pp in range(n_pairs):
            vb = vt_ref[pp, :, pl.ds(off, group)]
            for hh in range(2):
                h = 2 * pp + hh
                acc_ref[h] += jnp.dot(vb, p_ref[h], preferred_element_type=F32)

    def pipelined(g, c):
        value_group(g - 1)
        exp_group(g)
        return c

    exp_group(0)
    lax.fori_loop(1, n_groups, pipelined, 0)
    value_group(n_groups - 1)

    half = lanes // 2
    outs = []
    for pp in range(n_pairs):
        outs.append(acc_ref[2 * pp, :half, :] / acc_ref[2 * pp, lanes:lanes + 1, :])
        outs.append(acc_ref[2 * pp + 1, half:lanes, :] / acc_ref[2 * pp + 1, lanes:lanes + 1, :])
    o_ref[0] = jnp.concatenate(outs, axis=0).T.astype(BF16)


def _moba(q, k, v):
    B, S, _ = q.shape
    nb = S // MOBA_BLOCK
    assert nb % MOBA_GROUP == 0
    nbp = -(-nb // (2 * V7X_SUBLANES)) * (2 * V7X_SUBLANES)
    n_pairs = ATT_WIDTH // V7X_LANES
    pp = MOBA_PAIRS
    assert n_pairs % pp == 0
    vt_rows = V7X_LANES + 2 * V7X_SUBLANES
    qspec = pl.BlockSpec((1, MOBA_BLOCK, pp * V7X_LANES), lambda b, p, i: (b, i, p))
    kvspec = pl.BlockSpec((1, S, pp * V7X_LANES), lambda b, p, i: (b, 0, p))
    return pl.pallas_call(
        functools.partial(_moba_kernel, nb=nb),
        grid=(B, n_pairs // pp, nb),
        in_specs=[qspec, kvspec, kvspec],
        out_specs=qspec,
        out_shape=jax.ShapeDtypeStruct((B, S, ATT_WIDTH), BF16),
        scratch_shapes=[pltpu.VMEM((pp, S, 2 * V7X_LANES), BF16),
                        pltpu.VMEM((pp, nbp, 3 * V7X_LANES), BF16),
                        pltpu.VMEM((pp, vt_rows, S), BF16),
                        pltpu.VMEM((2 * pp, vt_rows, MOBA_BLOCK), F32),
                        pltpu.VMEM((2 * pp, S, MOBA_BLOCK), F32),
                        pltpu.VMEM((2 * pp, MOBA_BLOCK, MOBA_BLOCK), F32),
                        pltpu.VMEM((2 * pp, MOBA_GROUP * MOBA_BLOCK, MOBA_BLOCK), BF16)],
        compiler_params=_cparams("parallel", "parallel", "arbitrary"),
        name="moba",
    )(q, k, v)


def _hgrn_level_sizes(chunk):
    return [chunk >> (i + 1) for i in range(int(math.log2(chunk)))]


def _hgrn_constants(chunk):
    t = np.arange(chunk)
    mats = [(t[None, :] <= t[:, None]),
            (t[None, :] > t[:, None])]
    qrows, pmasks = [], []
    for bs in _hgrn_level_sizes(chunk):
        blk = t // bs
        odd = (blk % 2) == 1
        lo, hi = blk * bs, (blk + 1) * bs
        u = t[None, :]
        m_odd = (u >= lo[:, None]) & (u <= t[:, None])
        m_even = (u > t[:, None]) & (u < hi[:, None])
        mats.append(np.where(odd[:, None], m_odd, m_even))
        qrows.append(odd)
        pmasks.append(odd[:, None] & (blk[None, :] == blk[:, None] - 1))
    pmasks.append(t[None, :] == t[:, None])
    summat = np.concatenate(mats, axis=0).astype(np.float32)
    qrow = np.stack(qrows, axis=0).astype(np.float32)
    pmask = np.stack(pmasks, axis=0).astype(np.float32)
    return summat, qrow, pmask


def _hgrn_kernel(q_ref, f_ref, i_ref, g_ref, lb_ref, nw_ref, sm_ref, qrow_ref, pm_ref,
                 o_ref, state_ref, *, rows):
    C = HGRN_CHUNK
    n_levels = qrow_ref.shape[0]

    @pl.when(pl.program_id(1) == 0)
    def _():
        state_ref[...] = jnp.zeros_like(state_ref)

    lb = lb_ref[...]
    summat = sm_ref[...]
    for c in range(rows // C):
        r0 = c * C
        fg = lb + (1.0 - lb) * jax.nn.sigmoid(f_ref[0, r0:r0 + C, :])
        logf = jnp.log(fg)
        hi = logf.astype(BF16)
        rem = logf - hi.astype(F32)
        mid = rem.astype(BF16)
        lo = (rem - mid.astype(F32)).astype(BF16)
        sums = jnp.dot(summat, jnp.concatenate([hi, mid, lo], axis=0),
                       preferred_element_type=F32)
        for h in range(HGRN_HEADS):
            ls = slice(h * HGRN_DIM, (h + 1) * HGRN_DIM)
            qf = jax.nn.silu(q_ref[0, r0:r0 + C, ls].astype(F32))
            kf = 1.0 - fg[:, ls]
            iv = i_ref[0, r0:r0 + C, ls]
            bcum = sums[0:C, ls]
            bsuf = sums[C:2 * C, ls]
            att = _nt_dot(qf.astype(BF16), kf.astype(BF16)) * pm_ref[n_levels]
            for lv in range(n_levels):
                w = jnp.exp(sums[(2 + lv) * C:(3 + lv) * C, ls])
                qrow = qrow_ref[lv]
                z = (jnp.where(qrow > 0.5, qf, kf) * w).astype(BF16)
                att = att + _nt_dot(z, z) * pm_ref[lv]
            o = jnp.dot(att.astype(BF16), iv, preferred_element_type=F32)
            st = state_ref[h]
            o = o + _nt_dot((qf * jnp.exp(bcum)).astype(BF16), st.astype(BF16))
            kdec = (kf * jnp.exp(bsuf)).astype(BF16)
            state_ref[h] = st * jnp.exp(bcum[C - 1:C, :]) + _tn_dot(iv, kdec)
            o = o * lax.rsqrt(jnp.mean(o * o, axis=-1, keepdims=True) + RMS_EPS)
            o = o * nw_ref[:, ls] * jax.nn.silu(g_ref[0, r0:r0 + C, ls].astype(F32))
            o_ref[0, r0:r0 + C, ls] = o.astype(BF16)


def _hgrn(qb, fb, ib, gb, lb, norm_w):
    B, S, W = qb.shape
    rows = min(HGRN_ROWS, S)
    summat, qrow, pmask = _hgrn_constants(HGRN_CHUNK)
    summat = jnp.asarray(np.concatenate([summat] * 3, axis=1), BF16)
    n_levels = qrow.shape[0]
    blk = pl.BlockSpec((1, rows, W), lambda b, s: (b, s, 0))
    vec = pl.BlockSpec((1, W), lambda b, s: (0, 0))
    const = lambda a: pl.BlockSpec(a.shape, lambda b, s: (0,) * a.ndim)
    qrow3 = qrow.reshape(n_levels, HGRN_CHUNK, 1)
    return pl.pallas_call(
        functools.partial(_hgrn_kernel, rows=rows),
        grid=(B, S // rows),
        in_specs=[blk, blk, blk, blk, vec, vec, const(summat), const(qrow3), const(pmask)],
        out_specs=blk,
        out_shape=jax.ShapeDtypeStruct((B, S, W), BF16),
        scratch_shapes=[pltpu.VMEM((HGRN_HEADS, HGRN_DIM, HGRN_DIM), F32)],
        compiler_params=_cparams("parallel", "arbitrary"),
        name="hgrn",
    )(qb, fb, ib, gb, lb, norm_w, jnp.asarray(summat), jnp.asarray(qrow3), jnp.asarray(pmask))


def _mix_route_kernel(ya_ref, yb_ref, ga_ref, gb_ref, x_ref, wa_ref, wb_ref, wo_ref, nw_ref,
                      rw_ref, rb_ref, tri_ref,
                      x1_ref, h2_ref, e_ref, w_ref, rank_ref, cnt_ref, carry_ref):
    @pl.when(pl.program_id(0) == 0)
    def _():
        carry_ref[...] = jnp.zeros_like(carry_ref)

    ua = jnp.dot(ya_ref[...], wa_ref[...], preferred_element_type=F32)
    ub = jnp.dot(yb_ref[...], wb_ref[...], preferred_element_type=F32)
    mixed = (jax.nn.sigmoid(ga_ref[...].astype(F32)) * ua
             + jax.nn.sigmoid(gb_ref[...].astype(F32)) * ub)
    x1 = x_ref[...] + jnp.dot(mixed.astype(BF16), wo_ref[...], preferred_element_type=F32)
    x1_ref[...] = x1
    h2 = x1 * lax.rsqrt(jnp.mean(x1 * x1, axis=-1, keepdims=True) + RMS_EPS) * nw_ref[...]
    half = h2.shape[1] // 2
    h2_ref[...] = _pack_bf16_pair(h2[:, :half], h2[:, half:])

    tm = x1.shape[0]
    h2_hi = h2.astype(BF16)
    h2_lo = (h2 - h2_hi.astype(F32)).astype(BF16)
    logits = _nt_dot(rw_ref[...], jnp.concatenate([h2_hi, h2_lo, h2_hi], axis=1)) + rb_ref[...]
    eid = lax.broadcasted_iota(jnp.int32, (N_EXPERTS, tm), 0)
    work = logits
    es, vs = [], []
    for _ in range(TOP_K):
        mx = jnp.max(work, axis=0, keepdims=True)
        idx = jnp.min(jnp.where(work == mx, eid, N_EXPERTS), axis=0, keepdims=True)
        es.append(idx)
        vs.append(mx)
        work = jnp.where(eid == idx, -jnp.inf, work)
    ex = [jnp.exp(v - vs[0]) for v in vs]
    den = ex[0] + ex[1] + ex[2] + ex[3]
    multi = jnp.zeros((N_EXPERTS, tm), F32)
    for k in range(TOP_K):
        multi = multi + jnp.where(eid == es[k], 1.0, 0.0)
    before = jnp.dot(multi.astype(BF16), tri_ref[...], preferred_element_type=F32) + carry_ref[...]
    for k in range(TOP_K):
        e_ref[k:k + 1, :] = es[k]
        w_ref[k:k + 1, :] = ex[k] / den
        rank_ref[k:k + 1, :] = jnp.sum(jnp.where(eid == es[k], before, 0.0), axis=0,
                                       keepdims=True).astype(jnp.int32)
    carry_ref[...] = carry_ref[...] + jnp.sum(multi, axis=1, keepdims=True)
    cnt_ref[...] = jnp.broadcast_to(carry_ref[...], cnt_ref.shape)


def _mix_route(ya, yb, ga, gb, x2, wa, wb, wo, norm_w, rw_t, rb, row0, rows):
    D = x2.shape[1]
    T = rows
    tm = min(MIX_ROWS, T)
    row = lambda w: pl.BlockSpec((tm, w), lambda i: (i + row0 // tm, 0))
    out_row = pl.BlockSpec((tm, D), lambda i: (i, 0))
    whole = pl.BlockSpec(memory_space=pltpu.VMEM)
    kt = pl.BlockSpec((TOP_K, tm), lambda i: (0, i))
    tri = jnp.asarray(np.triu(np.ones((tm, tm), np.float32), 1), BF16)
    return pl.pallas_call(
        _mix_route_kernel,
        grid=(T // tm,),
        in_specs=[row(ATT_WIDTH), row(HGRN_WIDTH), row(D), row(D), row(D),
                  whole, whole, whole, pl.BlockSpec((1, D), lambda i: (0, 0)),
                  whole, whole, whole],
        out_specs=[out_row, pl.BlockSpec((tm, D // 2), lambda i: (i, 0)), kt, kt, kt,
                   pl.BlockSpec((N_EXPERTS, V7X_LANES), lambda i: (0, 0))],
        out_shape=[jax.ShapeDtypeStruct((T, D), F32), jax.ShapeDtypeStruct((T, D // 2), jnp.uint32),
                   jax.ShapeDtypeStruct((TOP_K, T), jnp.int32),
                   jax.ShapeDtypeStruct((TOP_K, T), F32),
                   jax.ShapeDtypeStruct((TOP_K, T), jnp.int32),
                   jax.ShapeDtypeStruct((N_EXPERTS, V7X_LANES), F32)],
        scratch_shapes=[pltpu.VMEM((N_EXPERTS, 1), F32)],
        compiler_params=_cparams("arbitrary"),
        name="mix_route",
    )(ya, yb, ga, gb, x2, wa, wb, wo, norm_w, rw_t, rb, tri)


def _sc_worker_id():
    return lax.axis_index("s") * V7X_SC_CORES + lax.axis_index("c")


def _sc_kernel(body, out_rows, like, window, name):
    mesh = plsc.VectorSubcoreMesh(core_axis_name="c", subcore_axis_name="s")
    return pl.kernel(
        body, mesh=mesh,
        out_type=jax.ShapeDtypeStruct((out_rows,) + like.shape[1:], like.dtype),
        scratch_types=[pltpu.VMEM((window,), jnp.int32),
                       pltpu.VMEM((window,) + like.shape[1:], like.dtype)],
        name=name)


def _scatter_rows(dest, h2):
    T = dest.shape[1]
    n_workers = V7X_SC_CORES * V7X_SC_SUBCORES
    per_worker = T // n_workers
    window = min(SC_ROWS, per_worker)

    def body(dest_hbm, src_hbm, dst_hbm, idx_v, rows_v):
        base = _sc_worker_id() * per_worker

        @pl.loop(0, per_worker // window)
        def _(c):
            t0 = base + c * window
            pltpu.sync_copy(src_hbm.at[pl.ds(t0, window)], rows_v)
            for k in range(TOP_K):
                pltpu.sync_copy(dest_hbm.at[pl.ds(k * T + t0, window)], idx_v)
                pltpu.sync_copy(rows_v, dst_hbm.at[idx_v])

    return _sc_kernel(body, TOP_K * T, h2, window, "scatter_rows")(dest.reshape(-1), h2)


def _gather_rows(dest, yg):
    T = dest.shape[1]
    n_workers = V7X_SC_CORES * V7X_SC_SUBCORES
    per_worker = TOP_K * T // n_workers
    window = min(SC_ROWS, per_worker)

    def body(dest_hbm, src_hbm, dst_hbm, idx_v, rows_v):
        base = _sc_worker_id() * per_worker

        @pl.loop(0, per_worker // window)
        def _(c):
            r0 = base + c * window
            pltpu.sync_copy(dest_hbm.at[pl.ds(r0, window)], idx_v)
            pltpu.sync_copy(src_hbm.at[idx_v], rows_v)
            pltpu.sync_copy(rows_v, dst_hbm.at[pl.ds(r0, window)])

    return _sc_kernel(body, TOP_K * T, yg, window, "gather_rows")(dest.reshape(-1), yg)


def _prep_gate_up_kernel(w_ref, perm_ref, wg_ref, wu_ref):
    half = V7X_MXU_DIM // 2
    for g in range(w_ref.shape[2] // V7X_MXU_DIM):
        wb = w_ref[0, :, g * V7X_MXU_DIM:(g + 1) * V7X_MXU_DIM].astype(BF16)
        d = jnp.dot(wb, perm_ref[...], preferred_element_type=F32).astype(BF16)
        wg_ref[0, :, g * half:(g + 1) * half] = d[:, :half]
        wu_ref[0, :, g * half:(g + 1) * half] = d[:, half:]


def _prep_gate_up(w_gate_up):
    E, D, F2 = w_gate_up.shape
    rows = min(PREP_ROWS, D)
    i = np.arange(V7X_MXU_DIM)
    src = np.where(i < V7X_MXU_DIM // 2, 2 * i, 2 * (i - V7X_MXU_DIM // 2) + 1)
    perm = np.zeros((V7X_MXU_DIM, V7X_MXU_DIM), np.float32)
    perm[src, i] = 1.0
    out = pl.BlockSpec((1, rows, F2 // 2), lambda e, r: (e, r, 0))
    return pl.pallas_call(
        _prep_gate_up_kernel,
        grid=(E, D // rows),
        in_specs=[pl.BlockSpec((1, rows, F2), lambda e, r: (e, r, 0)),
                  pl.BlockSpec((V7X_MXU_DIM, V7X_MXU_DIM), lambda e, r: (0, 0))],
        out_specs=[out, out],
        out_shape=[jax.ShapeDtypeStruct((E, D, F2 // 2), BF16)] * 2,
        compiler_params=_cparams("parallel", "parallel"),
        name="prep_gate_up",
    )(w_gate_up, jnp.asarray(perm, BF16))


def _experts_kernel(blk_ref, exp_ref, lo_ref, hi_ref, first_ref, slot_ref, efirst_ref, enext_ref,
                    x_ref, wg_hbm, wu_hbm, bg_ref, bu_ref, wd_hbm, bd_ref, y_ref,
                    wg_buf, wu_buf, wd_buf, sems):
    w = pl.program_id(0)
    lo = lo_ref[w]
    hi = hi_ref[w]
    slot = slot_ref[w]

    tm = x_ref.shape[0]
    n_out = V7X_MXU_DIM

    def weight_copies(e, s):
        return [pltpu.make_async_copy(hbm.at[e], buf.at[s], sems.at[s, i])
                for i, (hbm, buf) in enumerate(((wg_hbm, wg_buf), (wu_hbm, wu_buf), (wd_hbm, wd_buf)))]

    @pl.when(w == 0)
    def _():
        for c in weight_copies(exp_ref[0], slot):
            c.start()

    @pl.when(efirst_ref[w] == 1)
    def _():
        for c in weight_copies(exp_ref[w], slot):
            c.wait()

        @pl.when(enext_ref[w] >= 0)
        def _():
            for c in weight_copies(enext_ref[w], 1 - slot):
                c.start()

    def ffn(merge):
        x = jnp.concatenate(_unpack_bf16_pair(x_ref[...]), axis=1).astype(BF16)
        g = jnp.dot(x, wg_buf[slot], preferred_element_type=F32) + bg_ref[0]
        u = jnp.dot(x, wu_buf[slot], preferred_element_type=F32) + bu_ref[0]
        g = jnp.minimum(g, SWIGLU_LIMIT)
        u = jnp.clip(u, -SWIGLU_LIMIT, SWIGLU_LIMIT)
        act = ((u + 1.0) * (g * jax.nn.sigmoid(g * SWIGLU_ALPHA))).astype(BF16)
        half = y_ref.shape[1]

        def down(c0):
            cs = slice(c0, c0 + n_out)
            return (jnp.dot(act, wd_buf[slot, :, cs], preferred_element_type=F32)
                    + bd_ref[0, :, cs])

        for c in range(half // n_out):
            cs = slice(c * n_out, (c + 1) * n_out)
            y_ref[:, cs] = merge(_pack_bf16_pair(down(c * n_out), down(half + c * n_out)), cs)

    whole = hi - lo == tm
    r = lax.broadcasted_iota(jnp.int32, (tm, n_out), 0)
    mine = (r >= lo) & (r < hi)

    @pl.when(whole)
    def _():
        ffn(lambda y, cs: y)

    @pl.when((lo < hi) & jnp.logical_not(whole) & (first_ref[w] == 1))
    def _():
        ffn(lambda y, cs: jnp.where(mine, y, jnp.uint32(0)))

    @pl.when((lo < hi) & jnp.logical_not(whole) & (first_ref[w] == 0))
    def _():
        ffn(lambda y, cs: jnp.where(mine, y, y_ref[:, cs]))


def _work_items(counts, n_rows, tm):
    nblk = n_rows // tm
    n_items = nblk + N_EXPERTS - 1
    end = jnp.cumsum(counts)
    start = end - counts
    fb = start // tm
    nitems = jnp.where(counts > 0, (end - 1) // tm - fb + 1, 0)
    item_end = jnp.cumsum(nitems)
    item_start = item_end - nitems
    w = jnp.arange(n_items, dtype=jnp.int32)
    valid = w < item_end[-1]
    wc = jnp.minimum(w, item_end[-1] - 1)
    e = jnp.sum(wc[:, None] >= item_end[None, :], axis=1).astype(jnp.int32)
    e = jnp.minimum(e, N_EXPERTS - 1)
    onehot = e[:, None] == jnp.arange(N_EXPERTS, dtype=jnp.int32)[None, :]
    pick = lambda table: jnp.sum(jnp.where(onehot, table[None, :], 0), axis=1)
    blk = (pick(fb) + (wc - pick(item_start))).astype(jnp.int32)
    lo = jnp.maximum(pick(start), blk * tm) - blk * tm
    hi = jnp.minimum(pick(end), (blk + 1) * tm) - blk * tm
    lo = jnp.where(valid, lo, 0).astype(jnp.int32)
    hi = jnp.where(valid, hi, 0).astype(jnp.int32)
    first = jnp.concatenate([jnp.ones((1,), jnp.int32),
                             (blk[1:] != blk[:-1]).astype(jnp.int32)])
    used = counts > 0
    ids = jnp.arange(N_EXPERTS, dtype=jnp.int32)
    slot = pick((jnp.cumsum(used) - 1) % 2).astype(jnp.int32)
    efirst = (valid & (w == pick(item_start))).astype(jnp.int32)
    later = jnp.where(used[None, :] & (ids[None, :] > ids[:, None]), ids[None, :], N_EXPERTS)
    next_used = jnp.min(later, axis=1)
    enext = pick(jnp.where(next_used < N_EXPERTS, next_used, -1)).astype(jnp.int32)
    return blk, e, lo, hi, first, slot, efirst, enext


def _experts(xg, counts, wg, wu, bg, bu, wd, bd):
    A = xg.shape[0]
    D = 2 * xg.shape[1]
    tm = min(EXPERT_ROWS, A)
    F = wg.shape[2]
    items = _work_items(counts, A, tm)
    n_items = A // tm + N_EXPERTS - 1
    xs = pl.BlockSpec((tm, D // 2), lambda w, blk, e, *_: (blk[w], 0))
    bias = lambda c: pl.BlockSpec((1, 1, c), lambda w, blk, e, *_: (e[w], 0, 0))
    hbm = pl.BlockSpec(memory_space=pl.ANY)
    return pl.pallas_call(
        _experts_kernel,
        grid_spec=pltpu.PrefetchScalarGridSpec(
            num_scalar_prefetch=len(items),
            grid=(n_items,),
            in_specs=[xs, hbm, hbm, bias(F), bias(F), hbm, bias(D)],
            out_specs=xs,
            scratch_shapes=[pltpu.VMEM((2, D, F), BF16), pltpu.VMEM((2, D, F), BF16),
                            pltpu.VMEM((2, F, D), BF16), pltpu.SemaphoreType.DMA((2, 3))],
        ),
        out_shape=jax.ShapeDtypeStruct(xg.shape, jnp.uint32),
        compiler_params=_cparams("arbitrary"),
        name="experts",
    )(*items, xg, wg, wu, bg, bu, wd, bd)


def _combine_kernel(x1_ref, yk_ref, w_ref, nw_ref, *rest):
    o_ref = rest[-1]
    w = w_ref[...]
    x = x1_ref[...]
    for k in range(TOP_K):
        x = x + jnp.concatenate(_unpack_bf16_pair(yk_ref[k]), axis=1) * w[:, k:k + 1]
    o_ref[...] = x * lax.rsqrt(jnp.mean(x * x, axis=-1, keepdims=True) + RMS_EPS) * nw_ref[...]


def _combine(x1, yk, w_tk, norm_w, out_rows, row0, prev_out):
    Tg, D = x1.shape
    tm = min(COMBINE_ROWS, Tg)
    row = pl.BlockSpec((tm, D), lambda i: (i, 0))
    in_specs = [row, pl.BlockSpec((TOP_K, tm, D // 2), lambda i: (0, i, 0)),
                pl.BlockSpec((tm, TOP_K), lambda i: (i, 0)),
                pl.BlockSpec((1, D), lambda i: (0, 0))]
    args = [x1, yk, w_tk, norm_w]
    aliases = {}
    if prev_out is not None:
        in_specs.append(pl.BlockSpec(memory_space=pl.ANY))
        args.append(prev_out)
        aliases = {len(args) - 1: 0}
    return pl.pallas_call(
        _combine_kernel,
        grid=(Tg // tm,),
        in_specs=in_specs,
        out_specs=pl.BlockSpec((tm, D), lambda i: (i + row0 // tm, 0)),
        out_shape=jax.ShapeDtypeStruct((out_rows, D), F32),
        input_output_aliases=aliases,
        compiler_params=_cparams("parallel"),
        name="combine",
    )(*args)


def _qk_column_order():
    half = ATT_HEAD_DIM // 2
    order = []
    for p in range(ATT_HEADS // 2):
        for part in range(2):
            for h in (2 * p, 2 * p + 1):
                order.extend(range(h * ATT_HEAD_DIM + part * half, h * ATT_HEAD_DIM + (part + 1) * half))
    return np.asarray(order, np.int32)


def _rope_tables(seq):
    half = ATT_HEAD_DIM // 2
    inv = ROPE_THETA ** (-(jnp.arange(half, dtype=F32) * 2.0 / ATT_HEAD_DIM))
    ang = jnp.arange(seq, dtype=F32)[:, None] * inv[None, :]
    cos, sin = jnp.cos(ang), jnp.sin(ang)
    return (jnp.concatenate([cos, cos, cos, cos], axis=1),
            jnp.concatenate([-sin, -sin, sin, sin], axis=1))


def kernel(x, norm1_w, w_in, moba_up, hgrn_lb_logits, hgrn_norm_w, hgrn_up, w_out, norm2_w,
           router_w, router_b, w_gate_up, b_gate_up, w_down, b_down, final_norm_w):
    B, S, D = x.shape
    T = B * S
    assert S % MOBA_BLOCK == 0 and w_in.shape[0] == 1
    x2 = x.reshape(T, D)

    perm = _qk_column_order()
    w0 = w_in[0]
    w_in_p = jnp.concatenate([w0[:, :ATT_WIDTH][:, perm], w0[:, ATT_WIDTH:2 * ATT_WIDTH][:, perm],
                              w0[:, 2 * ATT_WIDTH:]], axis=1).astype(BF16)
    cos_t, sin_t = _rope_tables(S)
    lb = jnp.cumsum(jax.nn.softmax(hgrn_lb_logits.astype(F32), axis=0), axis=0)[0:1]

    qa, ka, va, qb, fb, ib, gb, ga, gtb = _in_proj(x2, norm1_w, w_in_p, cos_t, sin_t, S)
    r3 = lambda a: a.reshape(B, S, a.shape[1])
    ya = _moba(r3(qa), r3(ka), r3(va)).reshape(T, ATT_WIDTH)
    yb = _hgrn(r3(qb), r3(fb), r3(ib), r3(gb), lb, hgrn_norm_w).reshape(T, HGRN_WIDTH)

    rw_t = router_w[0].T
    rw_hi = rw_t.astype(BF16)
    rw_lo = (rw_t - rw_hi.astype(F32)).astype(BF16)
    mix_w = (moba_up[0].astype(BF16), hgrn_up[0].astype(BF16), w_out[0].astype(BF16), norm2_w,
             jnp.concatenate([rw_hi, rw_hi, rw_lo], axis=1), router_b[0][:, None])
    wg, wu = _prep_gate_up(w_gate_up[0])
    expert_w = (wg, wu, b_gate_up[0][:, None, 0::2], b_gate_up[0][:, None, 1::2],
                w_down[0].astype(BF16), b_down[0][:, None, :])

    Tg = T // MOE_TOKEN_GROUPS
    out = None
    for g in range(MOE_TOKEN_GROUPS):
        x1, h2, top_e, top_w, rank, cnt = _mix_route(ya, yb, ga, gtb, x2, *mix_w, g * Tg, Tg)
        counts = cnt[:, 0].astype(jnp.int32)
        start = jnp.cumsum(counts) - counts
        dest = rank + jnp.sum(jnp.where(top_e[:, :, None] == jnp.arange(N_EXPERTS)[None, None, :],
                                        start[None, None, :], 0), axis=-1)
        xg = _scatter_rows(dest, h2)
        yg = _experts(xg, counts, *expert_w)
        yk = _gather_rows(dest, yg).reshape(TOP_K, Tg, D // 2)
        out = _combine(x1, yk, top_w.T, final_norm_w[None, :], T, g * Tg, out)
    return out.reshape(B, S, D)
```

```python
import functools
import math

import numpy as np
import jax
import jax.numpy as jnp
from jax import lax
from jax.experimental import pallas as pl
from jax.experimental.pallas import tpu as pltpu
from jax.experimental.pallas import tpu_sc as plsc

ATT_HEADS = 8
ATT_HEAD_DIM = 64
ATT_WIDTH = ATT_HEADS * ATT_HEAD_DIM
MOBA_BLOCK = 256
MOBA_TOPK = 3
ROPE_THETA = 10000.0
HGRN_HEADS = 4
HGRN_DIM = 128
HGRN_WIDTH = HGRN_HEADS * HGRN_DIM
HGRN_CHUNK = 64
N_EXPERTS = 32
TOP_K = 4
SWIGLU_LIMIT = 7.0
SWIGLU_ALPHA = 1.702
RMS_EPS = 1e-6
NEG = -1e30

V7X_LANES = 128
V7X_SUBLANES = 8
V7X_MXU_DIM = 256
V7X_VMEM_LIMIT_BYTES = 56 * 1024 * 1024
V7X_SC_CORES = 2
V7X_SC_SUBCORES = 16

MOBA_PAIRS = 2
MOBA_GROUP = 4
PROJ_ROWS = 512
PROJ_COLS = 512
HGRN_ROWS = 512
MIX_ROWS = 512
EXPERT_ROWS = 256
MOE_TOKEN_GROUPS = 2
SC_ROWS = 64
COMBINE_ROWS = 256

F32 = jnp.float32
BF16 = jnp.bfloat16


def _nt_dot(a, b, precision=None):
    return lax.dot_general(a, b, (((1,), (1,)), ((), ())), precision=precision,
                           preferred_element_type=F32)


def _tn_dot(a, b, precision=None):
    return lax.dot_general(a, b, (((0,), (0,)), ((), ())), precision=precision,
                           preferred_element_type=F32)


def _cparams(*sem):
    return pltpu.CompilerParams(dimension_semantics=sem, vmem_limit_bytes=V7X_VMEM_LIMIT_BYTES)


def _pack_bf16_pair(lo, hi):
    lo_bits = lax.bitcast_convert_type(lo.astype(BF16).astype(F32), jnp.uint32)
    hi_bits = lax.bitcast_convert_type(hi.astype(BF16).astype(F32), jnp.uint32)
    return (lo_bits >> 16) | hi_bits


def _unpack_bf16_pair(word):
    lo = lax.bitcast_convert_type(word << 16, F32)
    hi = lax.bitcast_convert_type(word & jnp.uint32(0xFFFF0000), F32)
    return lo, hi


def _in_proj_kernel(x_ref, nw_ref, w_ref, cos_ref, sin_ref,
                    qa_ref, ka_ref, va_ref, qb_ref, fb_ref, ib_ref, gb_ref, ga_ref, gtb_ref):
    x = x_ref[...]
    h = x * lax.rsqrt(jnp.mean(x * x, axis=-1, keepdims=True) + RMS_EPS) * nw_ref[...]
    h = h.astype(BF16)
    cos = cos_ref[...]
    sin = sin_ref[...]

    def proj(c):
        return jnp.dot(h, w_ref[:, c * PROJ_COLS:(c + 1) * PROJ_COLS], preferred_element_type=F32)

    def rope(t):
        out = []
        for j in range(PROJ_COLS // V7X_LANES):
            tj = t[:, j * V7X_LANES:(j + 1) * V7X_LANES]
            out.append(tj * cos + pltpu.roll(tj, V7X_LANES // 2, 1) * sin)
        return jnp.concatenate(out, axis=1)

    qa_ref[...] = (rope(proj(0)) * (ATT_HEAD_DIM ** -0.5)).astype(BF16)
    ka_ref[...] = rope(proj(1)).astype(BF16)
    va_ref[...] = proj(2).astype(BF16)
    qb_ref[...] = proj(3).astype(BF16)
    fb_ref[...] = proj(4)
    ib_ref[...] = proj(5).astype(BF16)
    gb_ref[...] = proj(6).astype(BF16)
    ga_ref[:, :PROJ_COLS] = proj(7).astype(BF16)
    ga_ref[:, PROJ_COLS:] = proj(8).astype(BF16)
    gtb_ref[:, :PROJ_COLS] = proj(9).astype(BF16)
    gtb_ref[:, PROJ_COLS:] = proj(10).astype(BF16)


def _in_proj(x2, norm_w, w_in_bf16, cos_t, sin_t, seq):
    T, D = x2.shape
    tm = min(PROJ_ROWS, seq)
    n_seq_tiles = seq // tm
    row = lambda w: pl.BlockSpec((tm, w), lambda i: (i, 0))
    tab = pl.BlockSpec((tm, V7X_LANES), lambda i: (i % n_seq_tiles, 0))
    widths = [ATT_WIDTH] * 3 + [HGRN_WIDTH] * 4 + [D, D]
    dtypes = [BF16, BF16, BF16, BF16, F32, BF16, BF16, BF16, BF16]
    return pl.pallas_call(
        _in_proj_kernel,
        grid=(T // tm,),
        in_specs=[row(D), pl.BlockSpec((1, D), lambda i: (0, 0)),
                  pl.BlockSpec(memory_space=pltpu.VMEM), tab, tab],
        out_specs=[row(w) for w in widths],
        out_shape=[jax.ShapeDtypeStruct((T, w), dt) for w, dt in zip(widths, dtypes)],
        compiler_params=_cparams("parallel"),
        name="in_proj",
    )(x2, norm_w, w_in_bf16, cos_t, sin_t)


def _moba_kernel(q_ref, k_ref, v_ref, o_ref, kaug_ref, kmean_ref, vt_ref, acc_ref, s_ref,
                 sdiag_ref, p_ref, *, nb):
    qi = pl.program_id(2)
    blk = MOBA_BLOCK
    lanes = V7X_LANES
    n_pairs = kaug_ref.shape[0]
    heads = [(pp, hh) for pp in range(n_pairs) for hh in range(2)]
    nbp = kmean_ref.shape[1]

    @pl.when(qi == 0)
    def _():
        rowb = lax.broadcasted_iota(jnp.int32, (nb * blk, lanes), 0) // blk
        col = lax.broadcasted_iota(jnp.int32, (nb * blk, lanes), 1)
        onehot = jnp.where(rowb == col, 1.0, 0.0).astype(BF16)
        for pp in range(n_pairs):
            pl_ = slice(pp * lanes, (pp + 1) * lanes)
            kaug_ref[pp, :, :lanes] = k_ref[0, :, pl_]
            kaug_ref[pp, :, lanes:] = onehot
            means = [jnp.sum(k_ref[0, n * blk:(n + 1) * blk, pl_].astype(F32), axis=0,
                             keepdims=True) * (1.0 / blk) for n in range(nb)]
            km = jnp.concatenate(means + [jnp.zeros((nbp - nb, lanes), F32)] * (nbp > nb), axis=0)
            hi = km.astype(BF16)
            rem = km - hi.astype(F32)
            mid = rem.astype(BF16)
            lo = (rem - mid.astype(F32)).astype(BF16)
            kmean_ref[pp] = jnp.concatenate([hi, mid, lo], axis=1)
            for c in range(nb):
                vt = v_ref[0, c * blk:(c + 1) * blk, pl_].astype(F32).T
                vt_ref[pp, :lanes, c * blk:(c + 1) * blk] = vt.astype(BF16)
            vt_ref[pp, lanes:, :] = jnp.ones((vt_ref.shape[1] - lanes, nb * blk), BF16)

    feat = lax.broadcasted_iota(jnp.int32, (lanes, blk), 0)
    key_i = lax.broadcasted_iota(jnp.int32, (blk, blk), 0)
    qry_i = lax.broadcasted_iota(jnp.int32, (blk, blk), 1)
    own = pl.multiple_of(qi * blk, blk)
    blk_id = lax.broadcasted_iota(jnp.int32, (nbp, 2 * blk), 0)
    slab = 2 * V7X_SUBLANES

    def slab_max(s):
        return jnp.max(s.reshape(s.shape[0] // slab, slab, blk), axis=0)

    q_aug, m_init = [], []
    for pp in range(n_pairs):
        qt = q_ref[0, :, pp * lanes:(pp + 1) * lanes].astype(F32).T
        k_own = kaug_ref[pp, pl.ds(own, blk), :lanes]
        qhs = [jnp.where((feat // (ATT_HEAD_DIM // 2)) % 2 == hh, qt, 0.0).astype(BF16)
               for hh in range(2)]
        q2 = jnp.concatenate(qhs, axis=1)
        gate = jnp.dot(kmean_ref[pp], jnp.concatenate([q2, q2, q2], axis=0),
                       preferred_element_type=F32)
        gate = jnp.where(blk_id < qi, gate, NEG)
        beaten = jnp.zeros((nbp, 2 * blk), F32)
        for n in range(nb):
            gn = gate[n:n + 1, :]
            wins = (gn > gate) | ((gn == gate) & (blk_id > n))
            beaten = beaten + jnp.where(wins, 1.0, 0.0)
        sel = (beaten < MOBA_TOPK) & (blk_id < qi)
        bias = jnp.where(sel, 0.0, NEG).astype(BF16)
        bias = jnp.concatenate([bias, jnp.zeros((lanes - nbp, 2 * blk), BF16)], axis=0)
        for hh in range(2):
            h = 2 * pp + hh
            q_aug.append(jnp.concatenate([qhs[hh], bias[:, hh * blk:(hh + 1) * blk]], axis=0))
            s = jnp.dot(k_own, qhs[hh], preferred_element_type=F32)
            s = jnp.where(key_i <= qry_i, s, NEG)
            sdiag_ref[h] = s
            m_init.append(slab_max(s))
            acc_ref[h] = jnp.zeros(acc_ref.shape[1:], F32)

    group = MOBA_GROUP * blk
    n_groups = (qi + MOBA_GROUP) // MOBA_GROUP

    def score_group(g, ms):
        off = pl.multiple_of(g * group, group)
        out = []
        for pp in range(n_pairs):
            kb = kaug_ref[pp, pl.ds(off, group), :]
            for hh in range(2):
                h = 2 * pp + hh
                s = jnp.dot(kb, q_aug[h], preferred_element_type=F32)
                s_ref[h, pl.ds(off, group), :] = s
                out.append(jnp.maximum(ms[h], slab_max(s)))
        return tuple(out)

    ms = lax.fori_loop(0, n_groups, score_group, tuple(m_init))

    m_fin = []
    for h in range(len(heads)):
        s_ref[h, pl.ds(own, blk), :] = sdiag_ref[h]
        m_fin.append(jnp.max(ms[h], axis=0, keepdims=True))

    def exp_group(g):
        off = pl.multiple_of(g * group, group)
        for h in range(len(heads)):
            p_ref[h] = jnp.exp(s_ref[h, pl.ds(off, group), :] - m_fin[h]).astype(BF16)

    def value_group(g):
        off = pl.multiple_of(g * group, group)
        for pp in range(n_pairs):
            vb = vt_ref[pp, :, pl.ds(off, group)]
            for hh in range(2):
                h = 2 * pp + hh
                acc_ref[h] += jnp.dot(vb, p_ref[h], preferred_element_type=F32)

    def pipelined(g, c):
        value_group(g - 1)
        exp_group(g)
        return c

    exp_group(0)
    lax.fori_loop(1, n_groups, pipelined, 0)
    value_group(n_groups - 1)

    half = lanes // 2
    outs = []
    for pp in range(n_pairs):
        outs.append(acc_ref[2 * pp, :half, :] / acc_ref[2 * pp, lanes:lanes + 1, :])
        outs.append(acc_ref[2 * pp + 1, half:lanes, :] / acc_ref[2 * pp + 1, lanes:lanes + 1, :])
    o_ref[0] = jnp.concatenate(outs, axis=0).T.astype(BF16)


def _moba(q, k, v):
    B, S, _ = q.shape
    nb = S // MOBA_BLOCK
    assert nb % MOBA_GROUP == 0
    nbp = -(-nb // (2 * V7X_SUBLANES)) * (2 * V7X_SUBLANES)
    n_pairs = ATT_WIDTH // V7X_LANES
    pp = MOBA_PAIRS
    assert n_pairs % pp == 0
    vt_rows = V7X_LANES + 2 * V7X_SUBLANES
    qspec = pl.BlockSpec((1, MOBA_BLOCK, pp * V7X_LANES), lambda b, p, i: (b, i, p))
    kvspec = pl.BlockSpec((1, S, pp * V7X_LANES), lambda b, p, i: (b, 0, p))
    return pl.pallas_call(
        functools.partial(_moba_kernel, nb=nb),
        grid=(B, n_pairs // pp, nb),
        in_specs=[qspec, kvspec, kvspec],
        out_specs=qspec,
        out_shape=jax.ShapeDtypeStruct((B, S, ATT_WIDTH), BF16),
        scratch_shapes=[pltpu.VMEM((pp, S, 2 * V7X_LANES), BF16),
                        pltpu.VMEM((pp, nbp, 3 * V7X_LANES), BF16),
                        pltpu.VMEM((pp, vt_rows, S), BF16),
                        pltpu.VMEM((2 * pp, vt_rows, MOBA_BLOCK), F32),
                        pltpu.VMEM((2 * pp, S, MOBA_BLOCK), F32),
                        pltpu.VMEM((2 * pp, MOBA_BLOCK, MOBA_BLOCK), F32),
                        pltpu.VMEM((2 * pp, MOBA_GROUP * MOBA_BLOCK, MOBA_BLOCK), BF16)],
        compiler_params=_cparams("parallel", "parallel", "arbitrary"),
        name="moba",
    )(q, k, v)


def _hgrn_level_sizes(chunk):
    return [chunk >> (i + 1) for i in range(int(math.log2(chunk)))]


def _hgrn_constants(chunk):
    t = np.arange(chunk)
    mats = [(t[None, :] <= t[:, None]),
            (t[None, :] > t[:, None])]
    qrows, pmasks = [], []
    for bs in _hgrn_level_sizes(chunk):
        blk = t // bs
        odd = (blk % 2) == 1
        lo, hi = blk * bs, (blk + 1) * bs
        u = t[None, :]
        m_odd = (u >= lo[:, None]) & (u <= t[:, None])
        m_even = (u > t[:, None]) & (u < hi[:, None])
        mats.append(np.where(odd[:, None], m_odd, m_even))
        qrows.append(odd)
        pmasks.append(odd[:, None] & (blk[None, :] == blk[:, None] - 1))
    pmasks.append(t[None, :] == t[:, None])
    summat = np.concatenate(mats, axis=0).astype(np.float32)
    qrow = np.stack(qrows, axis=0).astype(np.float32)
    pmask = np.stack(pmasks, axis=0).astype(np.float32)
    return summat, qrow, pmask


def _hgrn_kernel(q_ref, f_ref, i_ref, g_ref, lb_ref, nw_ref, sm_ref, qrow_ref, pm_ref,
                 o_ref, state_ref, *, rows):
    C = HGRN_CHUNK
    n_levels = qrow_ref.shape[0]

    @pl.when(pl.program_id(1) == 0)
    def _():
        state_ref[...] = jnp.zeros_like(state_ref)

    lb = lb_ref[...]
    summat = sm_ref[...]
    for c in range(rows // C):
        rs = slice(c * C, (c + 1) * C)
        fg = lb + (1.0 - lb) * jax.nn.sigmoid(f_ref[0, rs, :])
        logf = jnp.log(fg)
        hi = logf.astype(BF16)
        rem = logf - hi.astype(F32)
        mid = rem.astype(BF16)
        lo = (rem - mid.astype(F32)).astype(BF16)
        sums = jnp.dot(summat, jnp.concatenate([hi, mid, lo], axis=0),
                       preferred_element_type=F32)
        for h in range(HGRN_HEADS):
            ls = slice(h * HGRN_DIM, (h + 1) * HGRN_DIM)
            qf = jax.nn.silu(q_ref[0, rs, ls].astype(F32))
            kf = 1.0 - fg[:, ls]
            iv = i_ref[0, rs, ls]
            bcum = sums[0:C, ls]
            bsuf = sums[C:2 * C, ls]
            att = _nt_dot(qf.astype(BF16), kf.astype(BF16)) * pm_ref[n_levels]
            for lv in range(n_levels):
                w = jnp.exp(sums[(2 + lv) * C:(3 + lv) * C, ls])
                qrow = qrow_ref[lv]
                z = (jnp.where(qrow > 0.5, qf, kf) * w).astype(BF16)
                att = att + _nt_dot(z, z) * pm_ref[lv]
            o = jnp.dot(att.astype(BF16), iv, preferred_element_type=F32)
            st = state_ref[h]
            o = o + _nt_dot((qf * jnp.exp(bcum)).astype(BF16), st.astype(BF16))
            kdec = (kf * jnp.exp(bsuf)).astype(BF16)
            state_ref[h] = st * jnp.exp(bcum[C - 1:C, :]) + _tn_dot(iv, kdec)
            o = o * lax.rsqrt(jnp.mean(o * o, axis=-1, keepdims=True) + RMS_EPS)
            o = o * nw_ref[:, ls] * jax.nn.silu(g_ref[0, rs, ls].astype(F32))
            o_ref[0, rs, ls] = o.astype(BF16)


def _hgrn(qb, fb, ib, gb, lb, norm_w):
    B, S, W = qb.shape
    rows = min(HGRN_ROWS, S)
    summat, qrow, pmask = _hgrn_constants(HGRN_CHUNK)
    summat = jnp.asarray(np.concatenate([summat] * 3, axis=1), BF16)
    n_levels = qrow.shape[0]
    blk = pl.BlockSpec((1, rows, W), lambda b, s: (b, s, 0))
    vec = pl.BlockSpec((1, W), lambda b, s: (0, 0))
    const = lambda a: pl.BlockSpec(a.shape, lambda b, s: (0,) * a.ndim)
    qrow3 = qrow.reshape(n_levels, HGRN_CHUNK, 1)
    return pl.pallas_call(
        functools.partial(_hgrn_kernel, rows=rows),
        grid=(B, S // rows),
        in_specs=[blk, blk, blk, blk, vec, vec, const(summat), const(qrow3), const(pmask)],
        out_specs=blk,
        out_shape=jax.ShapeDtypeStruct((B, S, W), BF16),
        scratch_shapes=[pltpu.VMEM((HGRN_HEADS, HGRN_DIM, HGRN_DIM), F32)],
        compiler_params=_cparams("parallel", "arbitrary"),
        name="hgrn",
    )(qb, fb, ib, gb, lb, norm_w, jnp.asarray(summat), jnp.asarray(qrow3), jnp.asarray(pmask))


def _mix_route_kernel(ya_ref, yb_ref, ga_ref, gb_ref, x_ref, wa_ref, wb_ref, wo_ref, nw_ref,
                      rw_ref, rb_ref, tri_ref,
                      x1_ref, h2_ref, e_ref, w_ref, rank_ref, cnt_ref, carry_ref):
    @pl.when(pl.program_id(0) == 0)
    def _():
        carry_ref[...] = jnp.zeros_like(carry_ref)

    ua = jnp.dot(ya_ref[...], wa_ref[...], preferred_element_type=F32)
    ub = jnp.dot(yb_ref[...], wb_ref[...], preferred_element_type=F32)
    mixed = (jax.nn.sigmoid(ga_ref[...].astype(F32)) * ua
             + jax.nn.sigmoid(gb_ref[...].astype(F32)) * ub)
    x1 = x_ref[...] + jnp.dot(mixed.astype(BF16), wo_ref[...], preferred_element_type=F32)
    x1_ref[...] = x1
    h2 = x1 * lax.rsqrt(jnp.mean(x1 * x1, axis=-1, keepdims=True) + RMS_EPS) * nw_ref[...]
    half = h2.shape[1] // 2
    h2_ref[...] = _pack_bf16_pair(h2[:, :half], h2[:, half:])

    tm = x1.shape[0]
    h2_hi = h2.astype(BF16)
    h2_lo = (h2 - h2_hi.astype(F32)).astype(BF16)
    logits = _nt_dot(rw_ref[...], jnp.concatenate([h2_hi, h2_lo, h2_hi], axis=1)) + rb_ref[...]
    eid = lax.broadcasted_iota(jnp.int32, (N_EXPERTS, tm), 0)
    work = logits
    es, vs = [], []
    for _ in range(TOP_K):
        mx = jnp.max(work, axis=0, keepdims=True)
        idx = jnp.min(jnp.where(work == mx, eid, N_EXPERTS), axis=0, keepdims=True)
        es.append(idx)
        vs.append(mx)
        work = jnp.where(eid == idx, -jnp.inf, work)
    ex = [jnp.exp(v - vs[0]) for v in vs]
    den = ex[0] + ex[1] + ex[2] + ex[3]
    multi = jnp.zeros((N_EXPERTS, tm), F32)
    for k in range(TOP_K):
        multi = multi + jnp.where(eid == es[k], 1.0, 0.0)
    before = jnp.dot(multi.astype(BF16), tri_ref[...], preferred_element_type=F32) + carry_ref[...]
    for k in range(TOP_K):
        e_ref[k:k + 1, :] = es[k]
        w_ref[k:k + 1, :] = ex[k] / den
        rank_ref[k:k + 1, :] = jnp.sum(jnp.where(eid == es[k], before, 0.0), axis=0,
                                       keepdims=True).astype(jnp.int32)
    carry_ref[...] = carry_ref[...] + jnp.sum(multi, axis=1, keepdims=True)
    cnt_ref[...] = jnp.broadcast_to(carry_ref[...], cnt_ref.shape)


def _mix_route(ya, yb, ga, gb, x2, wa, wb, wo, norm_w, rw_t, rb, row0, rows):
    D = x2.shape[1]
    T = rows
    tm = min(MIX_ROWS, T)
    row = lambda w: pl.BlockSpec((tm, w), lambda i: (i + row0 // tm, 0))
    out_row = pl.BlockSpec((tm, D), lambda i: (i, 0))
    whole = pl.BlockSpec(memory_space=pltpu.VMEM)
    kt = pl.BlockSpec((TOP_K, tm), lambda i: (0, i))
    tri = jnp.asarray(np.triu(np.ones((tm, tm), np.float32), 1), BF16)
    return pl.pallas_call(
        _mix_route_kernel,
        grid=(T // tm,),
        in_specs=[row(ATT_WIDTH), row(HGRN_WIDTH), row(D), row(D), row(D),
                  whole, whole, whole, pl.BlockSpec((1, D), lambda i: (0, 0)),
                  whole, whole, whole],
        out_specs=[out_row, pl.BlockSpec((tm, D // 2), lambda i: (i, 0)), kt, kt, kt,
                   pl.BlockSpec((N_EXPERTS, V7X_LANES), lambda i: (0, 0))],
        out_shape=[jax.ShapeDtypeStruct((T, D), F32), jax.ShapeDtypeStruct((T, D // 2), jnp.uint32),
                   jax.ShapeDtypeStruct((TOP_K, T), jnp.int32),
                   jax.ShapeDtypeStruct((TOP_K, T), F32),
                   jax.ShapeDtypeStruct((TOP_K, T), jnp.int32),
                   jax.ShapeDtypeStruct((N_EXPERTS, V7X_LANES), F32)],
        scratch_shapes=[pltpu.VMEM((N_EXPERTS, 1), F32)],
        compiler_params=_cparams("arbitrary"),
        name="mix_route",
    )(ya, yb, ga, gb, x2, wa, wb, wo, norm_w, rw_t, rb, tri)


def _sc_worker_id():
    return lax.axis_index("s") * V7X_SC_CORES + lax.axis_index("c")


def _sc_kernel(body, out_rows, like, window, name):
    mesh = plsc.VectorSubcoreMesh(core_axis_name="c", subcore_axis_name="s")
    return pl.kernel(
        body, mesh=mesh,
        out_type=jax.ShapeDtypeStruct((out_rows,) + like.shape[1:], like.dtype),
        scratch_types=[pltpu.VMEM((window,), jnp.int32),
                       pltpu.VMEM((window,) + like.shape[1:], like.dtype)],
        name=name)


def _scatter_rows(dest, h2):
    T = dest.shape[1]
    n_workers = V7X_SC_CORES * V7X_SC_SUBCORES
    per_worker = T // n_workers
    window = min(SC_ROWS, per_worker)

    def body(dest_hbm, src_hbm, dst_hbm, idx_v, rows_v):
        base = _sc_worker_id() * per_worker

        @pl.loop(0, per_worker // window)
        def _(c):
            t0 = base + c * window
            pltpu.sync_copy(src_hbm.at[pl.ds(t0, window)], rows_v)
            for k in range(TOP_K):
                pltpu.sync_copy(dest_hbm.at[pl.ds(k * T + t0, window)], idx_v)
                pltpu.sync_copy(rows_v, dst_hbm.at[idx_v])

    return _sc_kernel(body, TOP_K * T, h2, window, "scatter_rows")(dest.reshape(-1), h2)


def _gather_rows(dest, yg):
    T = dest.shape[1]
    n_workers = V7X_SC_CORES * V7X_SC_SUBCORES
    per_worker = TOP_K * T // n_workers
    window = min(SC_ROWS, per_worker)

    def body(dest_hbm, src_hbm, dst_hbm, idx_v, rows_v):
        base = _sc_worker_id() * per_worker

        @pl.loop(0, per_worker // window)
        def _(c):
            r0 = base + c * window
            pltpu.sync_copy(dest_hbm.at[pl.ds(r0, window)], idx_v)
            pltpu.sync_copy(src_hbm.at[idx_v], rows_v)
            pltpu.sync_copy(rows_v, dst_hbm.at[pl.ds(r0, window)])

    return _sc_kernel(body, TOP_K * T, yg, window, "gather_rows")(dest.reshape(-1), yg)


def _experts_kernel(blk_ref, exp_ref, lo_ref, hi_ref, first_ref, slot_ref, efirst_ref, enext_ref,
                    x_ref, wgu_hbm, bg_ref, bu_ref, wd_hbm, bd_ref, perm_ref, y_ref,
                    wgu_stage, wd_stage, wg_buf, wu_buf, wd_buf, sems):
    w = pl.program_id(0)
    lo = lo_ref[w]
    hi = hi_ref[w]
    slot = slot_ref[w]

    tm = x_ref.shape[0]
    n_out = V7X_MXU_DIM

    def weight_copies(e, s):
        return [pltpu.make_async_copy(wgu_hbm.at[e], wgu_stage.at[s], sems.at[s, 0]),
                pltpu.make_async_copy(wd_hbm.at[e], wd_stage.at[s], sems.at[s, 1])]

    @pl.when(w == 0)
    def _():
        for c in weight_copies(exp_ref[0], slot):
            c.start()

    @pl.when(efirst_ref[w] == 1)
    def _():
        for c in weight_copies(exp_ref[w], slot):
            c.wait()

        @pl.when(enext_ref[w] >= 0)
        def _():
            for c in weight_copies(enext_ref[w], 1 - slot):
                c.start()

        half = V7X_MXU_DIM // 2
        for g in range(wgu_stage.shape[2] // V7X_MXU_DIM):
            wb = wgu_stage[slot, :, g * V7X_MXU_DIM:(g + 1) * V7X_MXU_DIM].astype(BF16)
            d = jnp.dot(wb, perm_ref[...], preferred_element_type=F32).astype(BF16)
            wg_buf[:, g * half:(g + 1) * half] = d[:, :half]
            wu_buf[:, g * half:(g + 1) * half] = d[:, half:]
        wd_buf[...] = wd_stage[slot].astype(BF16)

    def ffn(merge):
        x = jnp.concatenate(_unpack_bf16_pair(x_ref[...]), axis=1).astype(BF16)
        g = jnp.dot(x, wg_buf[...], preferred_element_type=F32) + bg_ref[0]
        u = jnp.dot(x, wu_buf[...], preferred_element_type=F32) + bu_ref[0]
        g = jnp.minimum(g, SWIGLU_LIMIT)
        u = jnp.clip(u, -SWIGLU_LIMIT, SWIGLU_LIMIT)
        act = ((u + 1.0) * (g * jax.nn.sigmoid(g * SWIGLU_ALPHA))).astype(BF16)
        half = y_ref.shape[1]

        def down(c0):
            cs = slice(c0, c0 + n_out)
            return (jnp.dot(act, wd_buf[:, cs], preferred_element_type=F32)
                    + bd_ref[0, :, cs])

        for c in range(half // n_out):
            cs = slice(c * n_out, (c + 1) * n_out)
            y_ref[:, cs] = merge(_pack_bf16_pair(down(c * n_out), down(half + c * n_out)), cs)

    whole = hi - lo == tm
    r = lax.broadcasted_iota(jnp.int32, (tm, n_out), 0)
    mine = (r >= lo) & (r < hi)

    @pl.when(whole)
    def _():
        ffn(lambda y, cs: y)

    @pl.when((lo < hi) & jnp.logical_not(whole) & (first_ref[w] == 1))
    def _():
        ffn(lambda y, cs: jnp.where(mine, y, jnp.uint32(0)))

    @pl.when((lo < hi) & jnp.logical_not(whole) & (first_ref[w] == 0))
    def _():
        ffn(lambda y, cs: jnp.where(mine, y, y_ref[:, cs]))


def _work_items(counts, n_rows, tm):
    nblk = n_rows // tm
    n_items = nblk + N_EXPERTS - 1
    end = jnp.cumsum(counts)
    start = end - counts
    fb = start // tm
    nitems = jnp.where(counts > 0, (end - 1) // tm - fb + 1, 0)
    item_end = jnp.cumsum(nitems)
    item_start = item_end - nitems
    w = jnp.arange(n_items, dtype=jnp.int32)
    valid = w < item_end[-1]
    wc = jnp.minimum(w, item_end[-1] - 1)
    e = jnp.sum(wc[:, None] >= item_end[None, :], axis=1).astype(jnp.int32)
    e = jnp.minimum(e, N_EXPERTS - 1)
    onehot = e[:, None] == jnp.arange(N_EXPERTS, dtype=jnp.int32)[None, :]
    pick = lambda table: jnp.sum(jnp.where(onehot, table[None, :], 0), axis=1)
    blk = (pick(fb) + (wc - pick(item_start))).astype(jnp.int32)
    lo = jnp.maximum(pick(start), blk * tm) - blk * tm
    hi = jnp.minimum(pick(end), (blk + 1) * tm) - blk * tm
    lo = jnp.where(valid, lo, 0).astype(jnp.int32)
    hi = jnp.where(valid, hi, 0).astype(jnp.int32)
    first = jnp.concatenate([jnp.ones((1,), jnp.int32),
                             (blk[1:] != blk[:-1]).astype(jnp.int32)])
    used = counts > 0
    ids = jnp.arange(N_EXPERTS, dtype=jnp.int32)
    slot = pick((jnp.cumsum(used) - 1) % 2).astype(jnp.int32)
    efirst = (valid & (w == pick(item_start))).astype(jnp.int32)
    later = jnp.where(used[None, :] & (ids[None, :] > ids[:, None]), ids[None, :], N_EXPERTS)
    next_used = jnp.min(later, axis=1)
    enext = pick(jnp.where(next_used < N_EXPERTS, next_used, -1)).astype(jnp.int32)
    return blk, e, lo, hi, first, slot, efirst, enext


def _gate_up_split_matrix():
    i = np.arange(V7X_MXU_DIM)
    src = np.where(i < V7X_MXU_DIM // 2, 2 * i, 2 * (i - V7X_MXU_DIM // 2) + 1)
    perm = np.zeros((V7X_MXU_DIM, V7X_MXU_DIM), np.float32)
    perm[src, i] = 1.0
    return jnp.asarray(perm, BF16)


def _experts(xg, counts, w_gate_up, bg, bu, w_down, bd):
    A = xg.shape[0]
    D = 2 * xg.shape[1]
    tm = min(EXPERT_ROWS, A)
    F = w_down.shape[1]
    items = _work_items(counts, A, tm)
    n_items = A // tm + N_EXPERTS - 1
    xs = pl.BlockSpec((tm, D // 2), lambda w, blk, e, *_: (blk[w], 0))
    bias = lambda c: pl.BlockSpec((1, 1, c), lambda w, blk, e, *_: (e[w], 0, 0))
    hbm = pl.BlockSpec(memory_space=pl.ANY)
    return pl.pallas_call(
        _experts_kernel,
        grid_spec=pltpu.PrefetchScalarGridSpec(
            num_scalar_prefetch=len(items),
            grid=(n_items,),
            in_specs=[xs, hbm, bias(F), bias(F), hbm, bias(D), pl.BlockSpec(memory_space=pltpu.VMEM)],
            out_specs=xs,
            scratch_shapes=[pltpu.VMEM((2, D, 2 * F), F32), pltpu.VMEM((2, F, D), F32),
                            pltpu.VMEM((D, F), BF16), pltpu.VMEM((D, F), BF16),
                            pltpu.VMEM((F, D), BF16), pltpu.SemaphoreType.DMA((2, 2))],
        ),
        out_shape=jax.ShapeDtypeStruct(xg.shape, jnp.uint32),
        compiler_params=_cparams("arbitrary"),
        name="experts",
    )(*items, xg, w_gate_up, bg, bu, w_down, bd, _gate_up_split_matrix())


def _combine_kernel(x1_ref, yk_ref, w_ref, nw_ref, *rest):
    o_ref = rest[-1]
    w = w_ref[...]
    x = x1_ref[...]
    for k in range(TOP_K):
        x = x + jnp.concatenate(_unpack_bf16_pair(yk_ref[k]), axis=1) * w[:, k:k + 1]
    o_ref[...] = x * lax.rsqrt(jnp.mean(x * x, axis=-1, keepdims=True) + RMS_EPS) * nw_ref[...]


def _combine(x1, yk, w_tk, norm_w, out_rows, row0, prev_out):
    Tg, D = x1.shape
    tm = min(COMBINE_ROWS, Tg)
    row = pl.BlockSpec((tm, D), lambda i: (i, 0))
    in_specs = [row, pl.BlockSpec((TOP_K, tm, D // 2), lambda i: (0, i, 0)),
                pl.BlockSpec((tm, TOP_K), lambda i: (i, 0)),
                pl.BlockSpec((1, D), lambda i: (0, 0))]
    args = [x1, yk, w_tk, norm_w]
    aliases = {}
    if prev_out is not None:
        in_specs.append(pl.BlockSpec(memory_space=pl.ANY))
        args.append(prev_out)
        aliases = {len(args) - 1: 0}
    return pl.pallas_call(
        _combine_kernel,
        grid=(Tg // tm,),
        in_specs=in_specs,
        out_specs=pl.BlockSpec((tm, D), lambda i: (i + row0 // tm, 0)),
        out_shape=jax.ShapeDtypeStruct((out_rows, D), F32),
        input_output_aliases=aliases,
        compiler_params=_cparams("parallel"),
        name="combine",
    )(*args)


def _qk_column_order():
    half = ATT_HEAD_DIM // 2
    order = []
    for p in range(ATT_HEADS // 2):
        for part in range(2):
            for h in (2 * p, 2 * p + 1):
                order.extend(range(h * ATT_HEAD_DIM + part * half, h * ATT_HEAD_DIM + (part + 1) * half))
    return np.asarray(order, np.int32)


def _rope_tables(seq):
    half = ATT_HEAD_DIM // 2
    inv = ROPE_THETA ** (-(jnp.arange(half, dtype=F32) * 2.0 / ATT_HEAD_DIM))
    ang = jnp.arange(seq, dtype=F32)[:, None] * inv[None, :]
    cos, sin = jnp.cos(ang), jnp.sin(ang)
    return (jnp.concatenate([cos, cos, cos, cos], axis=1),
            jnp.concatenate([-sin, -sin, sin, sin], axis=1))


def kernel(x, norm1_w, w_in, moba_up, hgrn_lb_logits, hgrn_norm_w, hgrn_up, w_out, norm2_w,
           router_w, router_b, w_gate_up, b_gate_up, w_down, b_down, final_norm_w):
    B, S, D = x.shape
    T = B * S
    assert S % MOBA_BLOCK == 0 and w_in.shape[0] == 1
    x2 = x.reshape(T, D)

    perm = _qk_column_order()
    w0 = w_in[0]
    w_in_p = jnp.concatenate([w0[:, :ATT_WIDTH][:, perm], w0[:, ATT_WIDTH:2 * ATT_WIDTH][:, perm],
                              w0[:, 2 * ATT_WIDTH:]], axis=1).astype(BF16)
    cos_t, sin_t = _rope_tables(S)
    lb = jnp.cumsum(jax.nn.softmax(hgrn_lb_logits.astype(F32), axis=0), axis=0)[0:1]

    qa, ka, va, qb, fb, ib, gb, ga, gtb = _in_proj(x2, norm1_w, w_in_p, cos_t, sin_t, S)
    r3 = lambda a: a.reshape(B, S, a.shape[1])
    ya = _moba(r3(qa), r3(ka), r3(va)).reshape(T, ATT_WIDTH)
    yb = _hgrn(r3(qb), r3(fb), r3(ib), r3(gb), lb, hgrn_norm_w).reshape(T, HGRN_WIDTH)

    rw_t = router_w[0].T
    rw_hi = rw_t.astype(BF16)
    rw_lo = (rw_t - rw_hi.astype(F32)).astype(BF16)
    mix_w = (moba_up[0].astype(BF16), hgrn_up[0].astype(BF16), w_out[0].astype(BF16), norm2_w,
             jnp.concatenate([rw_hi, rw_hi, rw_lo], axis=1), router_b[0][:, None])
    expert_w = (w_gate_up[0], b_gate_up[0][:, None, 0::2], b_gate_up[0][:, None, 1::2],
                w_down[0], b_down[0][:, None, :])

    Tg = T // MOE_TOKEN_GROUPS
    out = None
    for g in range(MOE_TOKEN_GROUPS):
        x1, h2, top_e, top_w, rank, cnt = _mix_route(ya, yb, ga, gtb, x2, *mix_w, g * Tg, Tg)
        counts = cnt[:, 0].astype(jnp.int32)
        start = jnp.cumsum(counts) - counts
        dest = rank + jnp.sum(jnp.where(top_e[:, :, None] == jnp.arange(N_EXPERTS)[None, None, :],
                                        start[None, None, :], 0), axis=-1)
        xg = _scatter_rows(dest, h2)
        yg = _experts(xg, counts, *expert_w)
        yk = _gather_rows(dest, yg).reshape(TOP_K, Tg, D // 2)
        out = _combine(x1, yk, top_w.T, final_norm_w[None, :], T, g * Tg, out)
    return out.reshape(B, S, D)
```

```python
import functools
import math

import numpy as np
import jax
import jax.numpy as jnp
from jax import lax
from jax.experimental import pallas as pl
from jax.experimental.pallas import tpu as pltpu
from jax.experimental.pallas import tpu_sc as plsc

ATT_HEADS = 8
ATT_HEAD_DIM = 64
ATT_WIDTH = ATT_HEADS * ATT_HEAD_DIM
MOBA_BLOCK = 256
MOBA_TOPK = 3
ROPE_THETA = 10000.0
HGRN_HEADS = 4
HGRN_DIM = 128
HGRN_WIDTH = HGRN_HEADS * HGRN_DIM
HGRN_CHUNK = 64
N_EXPERTS = 32
TOP_K = 4
SWIGLU_LIMIT = 7.0
SWIGLU_ALPHA = 1.702
RMS_EPS = 1e-6
NEG = -1e30

V7X_LANES = 128
V7X_SUBLANES = 8
V7X_MXU_DIM = 256
V7X_VMEM_LIMIT_BYTES = 56 * 1024 * 1024
V7X_SC_CORES = 2
V7X_SC_SUBCORES = 16

MOBA_PAIRS = 2
MOBA_GROUP = 4
PROJ_ROWS = 512
PROJ_COLS = 512
HGRN_ROWS = 512
MIX_ROWS = 512
EXPERT_ROWS = 512
EXPERT_SUB_ROWS = 256
MOE_TOKEN_GROUPS = 2
SC_ROWS = 64
COMBINE_ROWS = 256

F32 = jnp.float32
BF16 = jnp.bfloat16


def _nt_dot(a, b, precision=None):
    return lax.dot_general(a, b, (((1,), (1,)), ((), ())), precision=precision,
                           preferred_element_type=F32)


def _tn_dot(a, b, precision=None):
    return lax.dot_general(a, b, (((0,), (0,)), ((), ())), precision=precision,
                           preferred_element_type=F32)


def _cparams(*sem):
    return pltpu.CompilerParams(dimension_semantics=sem, vmem_limit_bytes=V7X_VMEM_LIMIT_BYTES)


def _pack_bf16_pair(lo, hi):
    lo_bits = lax.bitcast_convert_type(lo.astype(BF16).astype(F32), jnp.uint32)
    hi_bits = lax.bitcast_convert_type(hi.astype(BF16).astype(F32), jnp.uint32)
    return (lo_bits >> 16) | hi_bits


def _unpack_bf16_pair(word):
    lo = lax.bitcast_convert_type(word << 16, F32)
    hi = lax.bitcast_convert_type(word & jnp.uint32(0xFFFF0000), F32)
    return lo, hi


def _in_proj_kernel(x_ref, nw_ref, w_ref, cos_ref, sin_ref,
                    qa_ref, ka_ref, va_ref, qb_ref, fb_ref, ib_ref, gb_ref, ga_ref, gtb_ref):
    x = x_ref[...]
    h = x * lax.rsqrt(jnp.mean(x * x, axis=-1, keepdims=True) + RMS_EPS) * nw_ref[...]
    h = h.astype(BF16)
    cos = cos_ref[...]
    sin = sin_ref[...]

    def proj(c):
        return jnp.dot(h, w_ref[:, c * PROJ_COLS:(c + 1) * PROJ_COLS], preferred_element_type=F32)

    def rope(t):
        out = []
        for j in range(PROJ_COLS // V7X_LANES):
            tj = t[:, j * V7X_LANES:(j + 1) * V7X_LANES]
            out.append(tj * cos + pltpu.roll(tj, V7X_LANES // 2, 1) * sin)
        return jnp.concatenate(out, axis=1)

    qa_ref[...] = (rope(proj(0)) * (ATT_HEAD_DIM ** -0.5)).astype(BF16)
    ka_ref[...] = rope(proj(1)).astype(BF16)
    va_ref[...] = proj(2).astype(BF16)
    qb_ref[...] = proj(3).astype(BF16)
    fb_ref[...] = proj(4)
    ib_ref[...] = proj(5).astype(BF16)
    gb_ref[...] = proj(6).astype(BF16)
    ga_ref[:, :PROJ_COLS] = proj(7).astype(BF16)
    ga_ref[:, PROJ_COLS:] = proj(8).astype(BF16)
    gtb_ref[:, :PROJ_COLS] = proj(9).astype(BF16)
    gtb_ref[:, PROJ_COLS:] = proj(10).astype(BF16)


def _in_proj(x2, norm_w, w_in_bf16, cos_t, sin_t, seq):
    T, D = x2.shape
    tm = min(PROJ_ROWS, seq)
    n_seq_tiles = seq // tm
    row = lambda w: pl.BlockSpec((tm, w), lambda i: (i, 0))
    tab = pl.BlockSpec((tm, V7X_LANES), lambda i: (i % n_seq_tiles, 0))
    widths = [ATT_WIDTH] * 3 + [HGRN_WIDTH] * 4 + [D, D]
    dtypes = [BF16, BF16, BF16, BF16, F32, BF16, BF16, BF16, BF16]
    return pl.pallas_call(
        _in_proj_kernel,
        grid=(T // tm,),
        in_specs=[row(D), pl.BlockSpec((1, D), lambda i: (0, 0)),
                  pl.BlockSpec(memory_space=pltpu.VMEM), tab, tab],
        out_specs=[row(w) for w in widths],
        out_shape=[jax.ShapeDtypeStruct((T, w), dt) for w, dt in zip(widths, dtypes)],
        compiler_params=_cparams("parallel"),
        name="in_proj",
    )(x2, norm_w, w_in_bf16, cos_t, sin_t)


def _moba_kernel(q_ref, k_ref, v_ref, o_ref, kaug_ref, kmean_ref, vt_ref, acc_ref, s_ref,
                 sdiag_ref, p_ref, *, nb):
    qi = pl.program_id(2)
    blk = MOBA_BLOCK
    lanes = V7X_LANES
    n_pairs = kaug_ref.shape[0]
    heads = [(pp, hh) for pp in range(n_pairs) for hh in range(2)]
    nbp = kmean_ref.shape[1]

    @pl.when(qi == 0)
    def _():
        rowb = lax.broadcasted_iota(jnp.int32, (nb * blk, lanes), 0) // blk
        col = lax.broadcasted_iota(jnp.int32, (nb * blk, lanes), 1)
        onehot = jnp.where(rowb == col, 1.0, 0.0).astype(BF16)
        for pp in range(n_pairs):
            pl_ = slice(pp * lanes, (pp + 1) * lanes)
            kaug_ref[pp, :, :lanes] = k_ref[0, :, pl_]
            kaug_ref[pp, :, lanes:] = onehot
            means = [jnp.sum(k_ref[0, n * blk:(n + 1) * blk, pl_].astype(F32), axis=0,
                             keepdims=True) * (1.0 / blk) for n in range(nb)]
            km = jnp.concatenate(means + [jnp.zeros((nbp - nb, lanes), F32)] * (nbp > nb), axis=0)
            hi = km.astype(BF16)
            rem = km - hi.astype(F32)
            mid = rem.astype(BF16)
            lo = (rem - mid.astype(F32)).astype(BF16)
            kmean_ref[pp] = jnp.concatenate([hi, mid, lo], axis=1)
            for c in range(nb):
                vt = v_ref[0, c * blk:(c + 1) * blk, pl_].astype(F32).T
                vt_ref[pp, :lanes, c * blk:(c + 1) * blk] = vt.astype(BF16)
            vt_ref[pp, lanes:, :] = jnp.ones((vt_ref.shape[1] - lanes, nb * blk), BF16)

    feat = lax.broadcasted_iota(jnp.int32, (lanes, blk), 0)
    key_i = lax.broadcasted_iota(jnp.int32, (blk, blk), 0)
    qry_i = lax.broadcasted_iota(jnp.int32, (blk, blk), 1)
    own = pl.multiple_of(qi * blk, blk)
    blk_id = lax.broadcasted_iota(jnp.int32, (nbp, 2 * blk), 0)
    slab = 2 * V7X_SUBLANES

    def slab_max(s):
        return jnp.max(s.reshape(s.shape[0] // slab, slab, blk), axis=0)

    q_aug, m_init = [], []
    for pp in range(n_pairs):
        qt = q_ref[0, :, pp * lanes:(pp + 1) * lanes].astype(F32).T
        k_own = kaug_ref[pp, pl.ds(own, blk), :lanes]
        qhs = [jnp.where((feat // (ATT_HEAD_DIM // 2)) % 2 == hh, qt, 0.0).astype(BF16)
               for hh in range(2)]
        q2 = jnp.concatenate(qhs, axis=1)
        gate = jnp.dot(kmean_ref[pp], jnp.concatenate([q2, q2, q2], axis=0),
                       preferred_element_type=F32)
        gate = jnp.where(blk_id < qi, gate, NEG)
        beaten = jnp.zeros((nbp, 2 * blk), F32)
        for n in range(nb):
            gn = gate[n:n + 1, :]
            wins = (gn > gate) | ((gn == gate) & (blk_id > n))
            beaten = beaten + jnp.where(wins, 1.0, 0.0)
        sel = (beaten < MOBA_TOPK) & (blk_id < qi)
        bias = jnp.where(sel, 0.0, NEG).astype(BF16)
        bias = jnp.concatenate([bias, jnp.zeros((lanes - nbp, 2 * blk), BF16)], axis=0)
        for hh in range(2):
            h = 2 * pp + hh
            q_aug.append(jnp.concatenate([qhs[hh], bias[:, hh * blk:(hh + 1) * blk]], axis=0))
            s = jnp.dot(k_own, qhs[hh], preferred_element_type=F32)
            s = jnp.where(key_i <= qry_i, s, NEG)
            sdiag_ref[h] = s
            m_init.append(slab_max(s))
            acc_ref[h] = jnp.zeros(acc_ref.shape[1:], F32)

    group = MOBA_GROUP * blk
    n_groups = (qi + MOBA_GROUP) // MOBA_GROUP

    def score_group(g, ms):
        off = pl.multiple_of(g * group, group)
        out = []
        for pp in range(n_pairs):
            kb = kaug_ref[pp, pl.ds(off, group), :]
            for hh in range(2):
                h = 2 * pp + hh
                s = jnp.dot(kb, q_aug[h], preferred_element_type=F32)
                s_ref[h, pl.ds(off, group), :] = s
                out.append(jnp.maximum(ms[h], slab_max(s)))
        return tuple(out)

    ms = lax.fori_loop(0, n_groups, score_group, tuple(m_init))

    m_fin = []
    for h in range(len(heads)):
        s_ref[h, pl.ds(own, blk), :] = sdiag_ref[h]
        m_fin.append(jnp.max(ms[h], axis=0, keepdims=True))

    def exp_group(g):
        off = pl.multiple_of(g * group, group)
        for h in range(len(heads)):
            p_ref[h] = jnp.exp(s_ref[h, pl.ds(off, group), :] - m_fin[h]).astype(BF16)

    def value_group(g):
        off = pl.multiple_of(g * group, group)
        for pp in range(n_pairs):
            vb = vt_ref[pp, :, pl.ds(off, group)]
            for hh in range(2):
                h = 2 * pp + hh
                acc_ref[h] += jnp.dot(vb, p_ref[h], preferred_element_type=F32)

    def pipelined(g, c):
        value_group(g - 1)
        exp_group(g)
        return c

    exp_group(0)
    lax.fori_loop(1, n_groups, pipelined, 0)
    value_group(n_groups - 1)

    half = lanes // 2
    outs = []
    for pp in range(n_pairs):
        outs.append(acc_ref[2 * pp, :half, :] / acc_ref[2 * pp, lanes:lanes + 1, :])
        outs.append(acc_ref[2 * pp + 1, half:lanes, :] / acc_ref[2 * pp + 1, lanes:lanes + 1, :])
    o_ref[0] = jnp.concatenate(outs, axis=0).T.astype(BF16)


def _moba(q, k, v):
    B, S, _ = q.shape
    nb = S // MOBA_BLOCK
    assert nb % MOBA_GROUP == 0
    nbp = -(-nb // (2 * V7X_SUBLANES)) * (2 * V7X_SUBLANES)
    n_pairs = ATT_WIDTH // V7X_LANES
    pp = MOBA_PAIRS
    assert n_pairs % pp == 0
    vt_rows = V7X_LANES + 2 * V7X_SUBLANES
    qspec = pl.BlockSpec((1, MOBA_BLOCK, pp * V7X_LANES), lambda b, p, i: (b, i, p))
    kvspec = pl.BlockSpec((1, S, pp * V7X_LANES), lambda b, p, i: (b, 0, p))
    return pl.pallas_call(
        functools.partial(_moba_kernel, nb=nb),
        grid=(B, n_pairs // pp, nb),
        in_specs=[qspec, kvspec, kvspec],
        out_specs=qspec,
        out_shape=jax.ShapeDtypeStruct((B, S, ATT_WIDTH), BF16),
        scratch_shapes=[pltpu.VMEM((pp, S, 2 * V7X_LANES), BF16),
                        pltpu.VMEM((pp, nbp, 3 * V7X_LANES), BF16),
                        pltpu.VMEM((pp, vt_rows, S), BF16),
                        pltpu.VMEM((2 * pp, vt_rows, MOBA_BLOCK), F32),
                        pltpu.VMEM((2 * pp, S, MOBA_BLOCK), F32),
                        pltpu.VMEM((2 * pp, MOBA_BLOCK, MOBA_BLOCK), F32),
                        pltpu.VMEM((2 * pp, MOBA_GROUP * MOBA_BLOCK, MOBA_BLOCK), BF16)],
        compiler_params=_cparams("parallel", "parallel", "arbitrary"),
        name="moba",
    )(q, k, v)


def _hgrn_level_sizes(chunk):
    return [chunk >> (i + 1) for i in range(int(math.log2(chunk)))]


def _hgrn_constants(chunk):
    t = np.arange(chunk)
    mats = [(t[None, :] <= t[:, None]),
            (t[None, :] > t[:, None])]
    qrows, pmasks = [], []
    for bs in _hgrn_level_sizes(chunk):
        blk = t // bs
        odd = (blk % 2) == 1
        lo, hi = blk * bs, (blk + 1) * bs
        u = t[None, :]
        m_odd = (u >= lo[:, None]) & (u <= t[:, None])
        m_even = (u > t[:, None]) & (u < hi[:, None])
        mats.append(np.where(odd[:, None], m_odd, m_even))
        qrows.append(odd)
        pmasks.append(odd[:, None] & (blk[None, :] == blk[:, None] - 1))
    pmasks.append(t[None, :] == t[:, None])
    summat = np.concatenate(mats, axis=0).astype(np.float32)
    qrow = np.stack(qrows, axis=0).astype(np.float32)
    pmask = np.stack(pmasks, axis=0).astype(np.float32)
    return summat, qrow, pmask


def _hgrn_kernel(q_ref, f_ref, i_ref, g_ref, lb_ref, nw_ref, sm_ref, qrow_ref, pm_ref,
                 o_ref, state_ref, *, rows):
    C = HGRN_CHUNK
    n_levels = qrow_ref.shape[0]

    @pl.when(pl.program_id(1) == 0)
    def _():
        state_ref[...] = jnp.zeros_like(state_ref)

    lb = lb_ref[...]
    summat = sm_ref[...]
    for c in range(rows // C):
        rs = slice(c * C, (c + 1) * C)
        fg = lb + (1.0 - lb) * jax.nn.sigmoid(f_ref[0, rs, :])
        logf = jnp.log(fg)
        hi = logf.astype(BF16)
        rem = logf - hi.astype(F32)
        mid = rem.astype(BF16)
        lo = (rem - mid.astype(F32)).astype(BF16)
        sums = jnp.dot(summat, jnp.concatenate([hi, mid, lo], axis=0),
                       preferred_element_type=F32)
        for h in range(HGRN_HEADS):
            ls = slice(h * HGRN_DIM, (h + 1) * HGRN_DIM)
            qf = jax.nn.silu(q_ref[0, rs, ls].astype(F32))
            kf = 1.0 - fg[:, ls]
            iv = i_ref[0, rs, ls]
            bcum = sums[0:C, ls]
            bsuf = sums[C:2 * C, ls]
            att = _nt_dot(qf.astype(BF16), kf.astype(BF16)) * pm_ref[n_levels]
            for lv in range(n_levels):
                w = jnp.exp(sums[(2 + lv) * C:(3 + lv) * C, ls])
                qrow = qrow_ref[lv]
                z = (jnp.where(qrow > 0.5, qf, kf) * w).astype(BF16)
                att = att + _nt_dot(z, z) * pm_ref[lv]
            o = jnp.dot(att.astype(BF16), iv, preferred_element_type=F32)
            st = state_ref[h]
            o = o + _nt_dot((qf * jnp.exp(bcum)).astype(BF16), st.astype(BF16))
            kdec = (kf * jnp.exp(bsuf)).astype(BF16)
            state_ref[h] = st * jnp.exp(bcum[C - 1:C, :]) + _tn_dot(iv, kdec)
            o = o * lax.rsqrt(jnp.mean(o * o, axis=-1, keepdims=True) + RMS_EPS)
            o = o * nw_ref[:, ls] * jax.nn.silu(g_ref[0, rs, ls].astype(F32))
            o_ref[0, rs, ls] = o.astype(BF16)


def _hgrn(qb, fb, ib, gb, lb, norm_w):
    B, S, W = qb.shape
    rows = min(HGRN_ROWS, S)
    summat, qrow, pmask = _hgrn_constants(HGRN_CHUNK)
    summat = jnp.asarray(np.concatenate([summat] * 3, axis=1), BF16)
    n_levels = qrow.shape[0]
    blk = pl.BlockSpec((1, rows, W), lambda b, s: (b, s, 0))
    vec = pl.BlockSpec((1, W), lambda b, s: (0, 0))
    const = lambda a: pl.BlockSpec(a.shape, lambda b, s: (0,) * a.ndim)
    qrow3 = qrow.reshape(n_levels, HGRN_CHUNK, 1)
    return pl.pallas_call(
        functools.partial(_hgrn_kernel, rows=rows),
        grid=(B, S // rows),
        in_specs=[blk, blk, blk, blk, vec, vec, const(summat), const(qrow3), const(pmask)],
        out_specs=blk,
        out_shape=jax.ShapeDtypeStruct((B, S, W), BF16),
        scratch_shapes=[pltpu.VMEM((HGRN_HEADS, HGRN_DIM, HGRN_DIM), F32)],
        compiler_params=_cparams("parallel", "arbitrary"),
        name="hgrn",
    )(qb, fb, ib, gb, lb, norm_w, jnp.asarray(summat), jnp.asarray(qrow3), jnp.asarray(pmask))


def _mix_route_kernel(ya_ref, yb_ref, ga_ref, gb_ref, x_ref, wa_ref, wb_ref, wo_ref, nw_ref,
                      rw_ref, rb_ref, tri_ref,
                      x1_ref, h2_ref, e_ref, w_ref, rank_ref, cnt_ref, carry_ref):
    @pl.when(pl.program_id(0) == 0)
    def _():
        carry_ref[...] = jnp.zeros_like(carry_ref)

    ua = jnp.dot(ya_ref[...], wa_ref[...], preferred_element_type=F32)
    ub = jnp.dot(yb_ref[...], wb_ref[...], preferred_element_type=F32)
    mixed = (jax.nn.sigmoid(ga_ref[...].astype(F32)) * ua
             + jax.nn.sigmoid(gb_ref[...].astype(F32)) * ub)
    x1 = x_ref[...] + jnp.dot(mixed.astype(BF16), wo_ref[...], preferred_element_type=F32)
    x1_ref[...] = x1
    h2 = x1 * lax.rsqrt(jnp.mean(x1 * x1, axis=-1, keepdims=True) + RMS_EPS) * nw_ref[...]
    half = h2.shape[1] // 2
    h2_ref[...] = _pack_bf16_pair(h2[:, :half], h2[:, half:])

    tm = x1.shape[0]
    h2_hi = h2.astype(BF16)
    h2_lo = (h2 - h2_hi.astype(F32)).astype(BF16)
    logits = _nt_dot(rw_ref[...], jnp.concatenate([h2_hi, h2_lo, h2_hi], axis=1)) + rb_ref[...]
    eid = lax.broadcasted_iota(jnp.int32, (N_EXPERTS, tm), 0)
    work = logits
    es, vs = [], []
    for _ in range(TOP_K):
        mx = jnp.max(work, axis=0, keepdims=True)
        idx = jnp.min(jnp.where(work == mx, eid, N_EXPERTS), axis=0, keepdims=True)
        es.append(idx)
        vs.append(mx)
        work = jnp.where(eid == idx, -jnp.inf, work)
    ex = [jnp.exp(v - vs[0]) for v in vs]
    den = ex[0] + ex[1] + ex[2] + ex[3]
    multi = jnp.zeros((N_EXPERTS, tm), F32)
    for k in range(TOP_K):
        multi = multi + jnp.where(eid == es[k], 1.0, 0.0)
    before = jnp.dot(multi.astype(BF16), tri_ref[...], preferred_element_type=F32) + carry_ref[...]
    for k in range(TOP_K):
        e_ref[k:k + 1, :] = es[k]
        w_ref[k:k + 1, :] = ex[k] / den
        rank_ref[k:k + 1, :] = jnp.sum(jnp.where(eid == es[k], before, 0.0), axis=0,
                                       keepdims=True).astype(jnp.int32)
    carry_ref[...] = carry_ref[...] + jnp.sum(multi, axis=1, keepdims=True)
    cnt_ref[...] = jnp.broadcast_to(carry_ref[...], cnt_ref.shape)


def _mix_route(ya, yb, ga, gb, x2, wa, wb, wo, norm_w, rw_t, rb, row0, rows):
    D = x2.shape[1]
    T = rows
    tm = min(MIX_ROWS, T)
    row = lambda w: pl.BlockSpec((tm, w), lambda i: (i + row0 // tm, 0))
    out_row = pl.BlockSpec((tm, D), lambda i: (i, 0))
    whole = pl.BlockSpec(memory_space=pltpu.VMEM)
    kt = pl.BlockSpec((TOP_K, tm), lambda i: (0, i))
    tri = jnp.asarray(np.triu(np.ones((tm, tm), np.float32), 1), BF16)
    return pl.pallas_call(
        _mix_route_kernel,
        grid=(T // tm,),
        in_specs=[row(ATT_WIDTH), row(HGRN_WIDTH), row(D), row(D), row(D),
                  whole, whole, whole, pl.BlockSpec((1, D), lambda i: (0, 0)),
                  whole, whole, whole],
        out_specs=[out_row, pl.BlockSpec((tm, D // 2), lambda i: (i, 0)), kt, kt, kt,
                   pl.BlockSpec((N_EXPERTS, V7X_LANES), lambda i: (0, 0))],
        out_shape=[jax.ShapeDtypeStruct((T, D), F32), jax.ShapeDtypeStruct((T, D // 2), jnp.uint32),
                   jax.ShapeDtypeStruct((TOP_K, T), jnp.int32),
                   jax.ShapeDtypeStruct((TOP_K, T), F32),
                   jax.ShapeDtypeStruct((TOP_K, T), jnp.int32),
                   jax.ShapeDtypeStruct((N_EXPERTS, V7X_LANES), F32)],
        scratch_shapes=[pltpu.VMEM((N_EXPERTS, 1), F32)],
        compiler_params=_cparams("arbitrary"),
        name="mix_route",
    )(ya, yb, ga, gb, x2, wa, wb, wo, norm_w, rw_t, rb, tri)


def _sc_worker_id():
    return lax.axis_index("s") * V7X_SC_CORES + lax.axis_index("c")


def _sc_kernel(body, out_rows, like, window, name):
    mesh = plsc.VectorSubcoreMesh(core_axis_name="c", subcore_axis_name="s")
    return pl.kernel(
        body, mesh=mesh,
        out_type=jax.ShapeDtypeStruct((out_rows,) + like.shape[1:], like.dtype),
        scratch_types=[pltpu.VMEM((window,), jnp.int32),
                       pltpu.VMEM((window,) + like.shape[1:], like.dtype)],
        name=name)


def _scatter_rows(dest, h2):
    T = dest.shape[1]
    n_workers = V7X_SC_CORES * V7X_SC_SUBCORES
    per_worker = T // n_workers
    window = min(SC_ROWS, per_worker)

    def body(dest_hbm, src_hbm, dst_hbm, idx_v, rows_v):
        base = _sc_worker_id() * per_worker

        @pl.loop(0, per_worker // window)
        def _(c):
            t0 = base + c * window
            pltpu.sync_copy(src_hbm.at[pl.ds(t0, window)], rows_v)
            for k in range(TOP_K):
                pltpu.sync_copy(dest_hbm.at[pl.ds(k * T + t0, window)], idx_v)
                pltpu.sync_copy(rows_v, dst_hbm.at[idx_v])

    return _sc_kernel(body, TOP_K * T, h2, window, "scatter_rows")(dest.reshape(-1), h2)


def _gather_rows(dest, yg):
    T = dest.shape[1]
    n_workers = V7X_SC_CORES * V7X_SC_SUBCORES
    per_worker = TOP_K * T // n_workers
    window = min(SC_ROWS, per_worker)

    def body(dest_hbm, src_hbm, dst_hbm, idx_v, rows_v):
        base = _sc_worker_id() * per_worker

        @pl.loop(0, per_worker // window)
        def _(c):
            r0 = base + c * window
            pltpu.sync_copy(dest_hbm.at[pl.ds(r0, window)], idx_v)
            pltpu.sync_copy(src_hbm.at[idx_v], rows_v)
            pltpu.sync_copy(rows_v, dst_hbm.at[pl.ds(r0, window)])

    return _sc_kernel(body, TOP_K * T, yg, window, "gather_rows")(dest.reshape(-1), yg)


def _experts_kernel(blk_ref, exp_ref, lo_ref, hi_ref, slot_ref, efirst_ref, enext_ref,
                    x_ref, wgu_hbm, bg_ref, bu_ref, wd_hbm, bd_ref, perm_ref, y_ref,
                    wgu_stage, wd_stage, wg_buf, wu_buf, wd_buf, sems):
    w = pl.program_id(0)
    lo = lo_ref[w]
    hi = hi_ref[w]
    slot = slot_ref[w]

    tm = x_ref.shape[0]
    n_out = V7X_MXU_DIM

    def weight_copies(e, s):
        return [pltpu.make_async_copy(wgu_hbm.at[e], wgu_stage.at[s], sems.at[s, 0]),
                pltpu.make_async_copy(wd_hbm.at[e], wd_stage.at[s], sems.at[s, 1])]

    @pl.when(w == 0)
    def _():
        for c in weight_copies(exp_ref[0], slot):
            c.start()

    @pl.when(efirst_ref[w] == 1)
    def _():
        for c in weight_copies(exp_ref[w], slot):
            c.wait()

        @pl.when(enext_ref[w] >= 0)
        def _():
            for c in weight_copies(enext_ref[w], 1 - slot):
                c.start()

        half = V7X_MXU_DIM // 2
        for g in range(wgu_stage.shape[2] // V7X_MXU_DIM):
            wb = wgu_stage[slot, :, g * V7X_MXU_DIM:(g + 1) * V7X_MXU_DIM].astype(BF16)
            d = jnp.dot(wb, perm_ref[...], preferred_element_type=F32).astype(BF16)
            wg_buf[:, g * half:(g + 1) * half] = d[:, :half]
            wu_buf[:, g * half:(g + 1) * half] = d[:, half:]
        wd_buf[...] = wd_stage[slot].astype(BF16)

    sb = EXPERT_SUB_ROWS

    def ffn(r0, keep_below):
        rs = slice(r0, r0 + sb)
        x = jnp.concatenate(_unpack_bf16_pair(x_ref[rs, :]), axis=1).astype(BF16)
        g = jnp.dot(x, wg_buf[...], preferred_element_type=F32) + bg_ref[0]
        u = jnp.dot(x, wu_buf[...], preferred_element_type=F32) + bu_ref[0]
        g = jnp.minimum(g, SWIGLU_LIMIT)
        u = jnp.clip(u, -SWIGLU_LIMIT, SWIGLU_LIMIT)
        act = ((u + 1.0) * (g * jax.nn.sigmoid(g * SWIGLU_ALPHA))).astype(BF16)
        half = y_ref.shape[1]

        def down(c0):
            cs = slice(c0, c0 + n_out)
            return (jnp.dot(act, wd_buf[:, cs], preferred_element_type=F32)
                    + bd_ref[0, :, cs])

        for c in range(half // n_out):
            cs = slice(c * n_out, (c + 1) * n_out)
            word = _pack_bf16_pair(down(c * n_out), down(half + c * n_out))
            if keep_below is not None:
                r = lax.broadcasted_iota(jnp.int32, (sb, n_out), 0)
                word = jnp.where(r >= keep_below, word, y_ref[rs, cs])
            y_ref[rs, cs] = word

    n_sub = tm // sb
    fused = (lo == 0) & (hi > (n_sub - 1) * sb)

    @pl.when(fused)
    def _():
        for j in range(n_sub):
            ffn(j * sb, None)

    for j in range(n_sub):
        lo_j = lo - j * sb
        touched = jnp.logical_not(fused) & (lo < hi) & (lo_j < sb) & (hi > j * sb)

        @pl.when(touched & (lo_j <= 0))
        def _():
            ffn(j * sb, None)

        @pl.when(touched & (lo_j > 0))
        def _():
            ffn(j * sb, lo_j)


def _work_items(counts, n_rows, tm):
    nblk = n_rows // tm
    n_items = nblk + N_EXPERTS - 1
    end = jnp.cumsum(counts)
    start = end - counts
    fb = start // tm
    nitems = jnp.where(counts > 0, (end - 1) // tm - fb + 1, 0)
    item_end = jnp.cumsum(nitems)
    item_start = item_end - nitems
    w = jnp.arange(n_items, dtype=jnp.int32)
    valid = w < item_end[-1]
    wc = jnp.minimum(w, item_end[-1] - 1)
    e = jnp.sum(wc[:, None] >= item_end[None, :], axis=1).astype(jnp.int32)
    e = jnp.minimum(e, N_EXPERTS - 1)
    onehot = e[:, None] == jnp.arange(N_EXPERTS, dtype=jnp.int32)[None, :]
    pick = lambda table: jnp.sum(jnp.where(onehot, table[None, :], 0), axis=1)
    blk = (pick(fb) + (wc - pick(item_start))).astype(jnp.int32)
    lo = jnp.maximum(pick(start), blk * tm) - blk * tm
    hi = jnp.minimum(pick(end), (blk + 1) * tm) - blk * tm
    lo = jnp.where(valid, lo, 0).astype(jnp.int32)
    hi = jnp.where(valid, hi, 0).astype(jnp.int32)
    used = counts > 0
    ids = jnp.arange(N_EXPERTS, dtype=jnp.int32)
    slot = pick((jnp.cumsum(used) - 1) % 2).astype(jnp.int32)
    efirst = (valid & (w == pick(item_start))).astype(jnp.int32)
    later = jnp.where(used[None, :] & (ids[None, :] > ids[:, None]), ids[None, :], N_EXPERTS)
    next_used = jnp.min(later, axis=1)
    enext = pick(jnp.where(next_used < N_EXPERTS, next_used, -1)).astype(jnp.int32)
    return blk, e, lo, hi, slot, efirst, enext


def _gate_up_split_matrix():
    i = np.arange(V7X_MXU_DIM)
    src = np.where(i < V7X_MXU_DIM // 2, 2 * i, 2 * (i - V7X_MXU_DIM // 2) + 1)
    perm = np.zeros((V7X_MXU_DIM, V7X_MXU_DIM), np.float32)
    perm[src, i] = 1.0
    return jnp.asarray(perm, BF16)


def _experts(xg, counts, w_gate_up, bg, bu, w_down, bd):
    A = xg.shape[0]
    D = 2 * xg.shape[1]
    tm = min(EXPERT_ROWS, A)
    F = w_down.shape[1]
    items = _work_items(counts, A, tm)
    n_items = A // tm + N_EXPERTS - 1
    xs = pl.BlockSpec((tm, D // 2), lambda w, blk, e, *_: (blk[w], 0))
    bias = lambda c: pl.BlockSpec((1, 1, c), lambda w, blk, e, *_: (e[w], 0, 0))
    hbm = pl.BlockSpec(memory_space=pl.ANY)
    return pl.pallas_call(
        _experts_kernel,
        grid_spec=pltpu.PrefetchScalarGridSpec(
            num_scalar_prefetch=len(items),
            grid=(n_items,),
            in_specs=[xs, hbm, bias(F), bias(F), hbm, bias(D), pl.BlockSpec(memory_space=pltpu.VMEM)],
            out_specs=xs,
            scratch_shapes=[pltpu.VMEM((2, D, 2 * F), F32), pltpu.VMEM((2, F, D), F32),
                            pltpu.VMEM((D, F), BF16), pltpu.VMEM((D, F), BF16),
                            pltpu.VMEM((F, D), BF16), pltpu.SemaphoreType.DMA((2, 2))],
        ),
        out_shape=jax.ShapeDtypeStruct(xg.shape, jnp.uint32),
        compiler_params=_cparams("arbitrary"),
        name="experts",
    )(*items, xg, w_gate_up, bg, bu, w_down, bd, _gate_up_split_matrix())


def _combine_kernel(x1_ref, yk_ref, w_ref, nw_ref, *rest):
    o_ref = rest[-1]
    w = w_ref[...]
    x = x1_ref[...]
    for k in range(TOP_K):
        x = x + jnp.concatenate(_unpack_bf16_pair(yk_ref[k]), axis=1) * w[:, k:k + 1]
    o_ref[...] = x * lax.rsqrt(jnp.mean(x * x, axis=-1, keepdims=True) + RMS_EPS) * nw_ref[...]


def _combine(x1, yk, w_tk, norm_w, out_rows, row0, prev_out):
    Tg, D = x1.shape
    tm = min(COMBINE_ROWS, Tg)
    row = pl.BlockSpec((tm, D), lambda i: (i, 0))
    in_specs = [row, pl.BlockSpec((TOP_K, tm, D // 2), lambda i: (0, i, 0)),
                pl.BlockSpec((tm, TOP_K), lambda i: (i, 0)),
                pl.BlockSpec((1, D), lambda i: (0, 0))]
    args = [x1, yk, w_tk, norm_w]
    aliases = {}
    if prev_out is not None:
        in_specs.append(pl.BlockSpec(memory_space=pl.ANY))
        args.append(prev_out)
        aliases = {len(args) - 1: 0}
    return pl.pallas_call(
        _combine_kernel,
        grid=(Tg // tm,),
        in_specs=in_specs,
        out_specs=pl.BlockSpec((tm, D), lambda i: (i + row0 // tm, 0)),
        out_shape=jax.ShapeDtypeStruct((out_rows, D), F32),
        input_output_aliases=aliases,
        compiler_params=_cparams("parallel"),
        name="combine",
    )(*args)


def _qk_column_order():
    half = ATT_HEAD_DIM // 2
    order = []
    for p in range(ATT_HEADS // 2):
        for part in range(2):
            for h in (2 * p, 2 * p + 1):
                order.extend(range(h * ATT_HEAD_DIM + part * half, h * ATT_HEAD_DIM + (part + 1) * half))
    return np.asarray(order, np.int32)


def _rope_tables(seq):
    half = ATT_HEAD_DIM // 2
    inv = ROPE_THETA ** (-(jnp.arange(half, dtype=F32) * 2.0 / ATT_HEAD_DIM))
    ang = jnp.arange(seq, dtype=F32)[:, None] * inv[None, :]
    cos, sin = jnp.cos(ang), jnp.sin(ang)
    return (jnp.concatenate([cos, cos, cos, cos], axis=1),
            jnp.concatenate([-sin, -sin, sin, sin], axis=1))


def kernel(x, norm1_w, w_in, moba_up, hgrn_lb_logits, hgrn_norm_w, hgrn_up, w_out, norm2_w,
           router_w, router_b, w_gate_up, b_gate_up, w_down, b_down, final_norm_w):
    B, S, D = x.shape
    T = B * S
    assert S % MOBA_BLOCK == 0 and w_in.shape[0] == 1
    x2 = x.reshape(T, D)

    perm = _qk_column_order()
    w0 = w_in[0]
    w_in_p = jnp.concatenate([w0[:, :ATT_WIDTH][:, perm], w0[:, ATT_WIDTH:2 * ATT_WIDTH][:, perm],
                              w0[:, 2 * ATT_WIDTH:]], axis=1).astype(BF16)
    cos_t, sin_t = _rope_tables(S)
    lb = jnp.cumsum(jax.nn.softmax(hgrn_lb_logits.astype(F32), axis=0), axis=0)[0:1]

    qa, ka, va, qb, fb, ib, gb, ga, gtb = _in_proj(x2, norm1_w, w_in_p, cos_t, sin_t, S)
    r3 = lambda a: a.reshape(B, S, a.shape[1])
    ya = _moba(r3(qa), r3(ka), r3(va)).reshape(T, ATT_WIDTH)
    yb = _hgrn(r3(qb), r3(fb), r3(ib), r3(gb), lb, hgrn_norm_w).reshape(T, HGRN_WIDTH)

    rw_t = router_w[0].T
    rw_hi = rw_t.astype(BF16)
    rw_lo = (rw_t - rw_hi.astype(F32)).astype(BF16)
    mix_w = (moba_up[0].astype(BF16), hgrn_up[0].astype(BF16), w_out[0].astype(BF16), norm2_w,
             jnp.concatenate([rw_hi, rw_hi, rw_lo], axis=1), router_b[0][:, None])
    expert_w = (w_gate_up[0], b_gate_up[0][:, None, 0::2], b_gate_up[0][:, None, 1::2],
                w_down[0], b_down[0][:, None, :])

    Tg = T // MOE_TOKEN_GROUPS
    out = None
    for g in range(MOE_TOKEN_GROUPS):
        x1, h2, top_e, top_w, rank, cnt = _mix_route(ya, yb, ga, gtb, x2, *mix_w, g * Tg, Tg)
        counts = cnt[:, 0].astype(jnp.int32)
        start = jnp.cumsum(counts) - counts
        dest = rank + jnp.sum(jnp.where(top_e[:, :, None] == jnp.arange(N_EXPERTS)[None, None, :],
                                        start[None, None, :], 0), axis=-1)
        xg = _scatter_rows(dest, h2)
        yg = _experts(xg, counts, *expert_w)
        yk = _gather_rows(dest, yg).reshape(TOP_K, Tg, D // 2)
        out = _combine(x1, yk, top_w.T, final_norm_w[None, :], T, g * Tg, out)
    return out.reshape(B, S, D)
```

```python
import functools
import math

import numpy as np
import jax
import jax.numpy as jnp
from jax import lax
from jax.experimental import pallas as pl
from jax.experimental.pallas import tpu as pltpu
from jax.experimental.pallas import tpu_sc as plsc

ATT_HEADS = 8
ATT_HEAD_DIM = 64
ATT_WIDTH = ATT_HEADS * ATT_HEAD_DIM
MOBA_BLOCK = 256
MOBA_TOPK = 3
ROPE_THETA = 10000.0
HGRN_HEADS = 4
HGRN_DIM = 128
HGRN_WIDTH = HGRN_HEADS * HGRN_DIM
HGRN_CHUNK = 64
N_EXPERTS = 32
TOP_K = 4
SWIGLU_LIMIT = 7.0
SWIGLU_ALPHA = 1.702
RMS_EPS = 1e-6
NEG = -1e30

V7X_LANES = 128
V7X_SUBLANES = 8
V7X_MXU_DIM = 256
V7X_VMEM_LIMIT_BYTES = 56 * 1024 * 1024
V7X_SC_CORES = 2
V7X_SC_SUBCORES = 16

MOBA_PAIRS = 2
MOBA_GROUP = 4
MOBA_VALUE_GROUP = 2
PROJ_ROWS = 512
PROJ_COLS = 512
HGRN_ROWS = 512
MIX_ROWS = 512
EXPERT_ROWS = 512
EXPERT_SUB_ROWS = 256
MOE_TOKEN_GROUPS = 2
SC_ROWS = 64
COMBINE_ROWS = 256

F32 = jnp.float32
BF16 = jnp.bfloat16


def _nt_dot(a, b, precision=None):
    return lax.dot_general(a, b, (((1,), (1,)), ((), ())), precision=precision,
                           preferred_element_type=F32)


def _tn_dot(a, b, precision=None):
    return lax.dot_general(a, b, (((0,), (0,)), ((), ())), precision=precision,
                           preferred_element_type=F32)


def _cparams(*sem):
    return pltpu.CompilerParams(dimension_semantics=sem, vmem_limit_bytes=V7X_VMEM_LIMIT_BYTES)


def _pack_bf16_pair(lo, hi):
    lo_bits = lax.bitcast_convert_type(lo.astype(BF16).astype(F32), jnp.uint32)
    hi_bits = lax.bitcast_convert_type(hi.astype(BF16).astype(F32), jnp.uint32)
    return (lo_bits >> 16) | hi_bits


def _unpack_bf16_pair(word):
    lo = lax.bitcast_convert_type(word << 16, F32)
    hi = lax.bitcast_convert_type(word & jnp.uint32(0xFFFF0000), F32)
    return lo, hi


def _in_proj_kernel(x_ref, nw_ref, w_ref, cos_ref, sin_ref,
                    qa_ref, ka_ref, va_ref, qb_ref, fb_ref, ib_ref, gb_ref, ga_ref, gtb_ref):
    x = x_ref[...]
    h = x * lax.rsqrt(jnp.mean(x * x, axis=-1, keepdims=True) + RMS_EPS) * nw_ref[...]
    h = h.astype(BF16)
    cos = cos_ref[...]
    sin = sin_ref[...]

    def proj(c):
        return jnp.dot(h, w_ref[:, c * PROJ_COLS:(c + 1) * PROJ_COLS], preferred_element_type=F32)

    def rope(t):
        out = []
        for j in range(PROJ_COLS // V7X_LANES):
            tj = t[:, j * V7X_LANES:(j + 1) * V7X_LANES]
            out.append(tj * cos + pltpu.roll(tj, V7X_LANES // 2, 1) * sin)
        return jnp.concatenate(out, axis=1)

    qa_ref[...] = (rope(proj(0)) * (ATT_HEAD_DIM ** -0.5)).astype(BF16)
    ka_ref[...] = rope(proj(1)).astype(BF16)
    va_ref[...] = proj(2).astype(BF16)
    qb_ref[...] = proj(3).astype(BF16)
    fb_ref[...] = proj(4)
    ib_ref[...] = proj(5).astype(BF16)
    gb_ref[...] = proj(6).astype(BF16)
    ga_ref[:, :PROJ_COLS] = proj(7).astype(BF16)
    ga_ref[:, PROJ_COLS:] = proj(8).astype(BF16)
    gtb_ref[:, :PROJ_COLS] = proj(9).astype(BF16)
    gtb_ref[:, PROJ_COLS:] = proj(10).astype(BF16)


def _in_proj(x2, norm_w, w_in_bf16, cos_t, sin_t, seq):
    T, D = x2.shape
    tm = min(PROJ_ROWS, seq)
    n_seq_tiles = seq // tm
    row = lambda w: pl.BlockSpec((tm, w), lambda i: (i, 0))
    tab = pl.BlockSpec((tm, V7X_LANES), lambda i: (i % n_seq_tiles, 0))
    widths = [ATT_WIDTH] * 3 + [HGRN_WIDTH] * 4 + [D, D]
    dtypes = [BF16, BF16, BF16, BF16, F32, BF16, BF16, BF16, BF16]
    return pl.pallas_call(
        _in_proj_kernel,
        grid=(T // tm,),
        in_specs=[row(D), pl.BlockSpec((1, D), lambda i: (0, 0)),
                  pl.BlockSpec(memory_space=pltpu.VMEM), tab, tab],
        out_specs=[row(w) for w in widths],
        out_shape=[jax.ShapeDtypeStruct((T, w), dt) for w, dt in zip(widths, dtypes)],
        compiler_params=_cparams("parallel"),
        name="in_proj",
    )(x2, norm_w, w_in_bf16, cos_t, sin_t)


def _moba_kernel(q_ref, k_ref, v_ref, o_ref, kaug_ref, kmean_ref, vt_ref, acc_ref, s_ref,
                 sdiag_ref, p_ref, *, nb):
    qi = pl.program_id(2)
    blk = MOBA_BLOCK
    lanes = V7X_LANES
    n_pairs = kaug_ref.shape[0]
    heads = [(pp, hh) for pp in range(n_pairs) for hh in range(2)]
    nbp = kmean_ref.shape[1]

    @pl.when(qi == 0)
    def _():
        rowb = lax.broadcasted_iota(jnp.int32, (nb * blk, lanes), 0) // blk
        col = lax.broadcasted_iota(jnp.int32, (nb * blk, lanes), 1)
        onehot = jnp.where(rowb == col, 1.0, 0.0).astype(BF16)
        for pp in range(n_pairs):
            pl_ = slice(pp * lanes, (pp + 1) * lanes)
            kaug_ref[pp, :, :lanes] = k_ref[0, :, pl_]
            kaug_ref[pp, :, lanes:] = onehot
            means = [jnp.sum(k_ref[0, n * blk:(n + 1) * blk, pl_].astype(F32), axis=0,
                             keepdims=True) * (1.0 / blk) for n in range(nb)]
            km = jnp.concatenate(means + [jnp.zeros((nbp - nb, lanes), F32)] * (nbp > nb), axis=0)
            hi = km.astype(BF16)
            rem = km - hi.astype(F32)
            mid = rem.astype(BF16)
            lo = (rem - mid.astype(F32)).astype(BF16)
            kmean_ref[pp] = jnp.concatenate([hi, mid, lo], axis=1)
            for c in range(nb):
                vt = v_ref[0, c * blk:(c + 1) * blk, pl_].astype(F32).T
                vt_ref[pp, :lanes, c * blk:(c + 1) * blk] = vt.astype(BF16)
            vt_ref[pp, lanes:, :] = jnp.ones((vt_ref.shape[1] - lanes, nb * blk), BF16)

    feat = lax.broadcasted_iota(jnp.int32, (lanes, blk), 0)
    key_i = lax.broadcasted_iota(jnp.int32, (blk, blk), 0)
    qry_i = lax.broadcasted_iota(jnp.int32, (blk, blk), 1)
    own = pl.multiple_of(qi * blk, blk)
    blk_id = lax.broadcasted_iota(jnp.int32, (nbp, 2 * blk), 0)
    slab = 2 * V7X_SUBLANES

    def slab_max(s):
        return jnp.max(s.reshape(s.shape[0] // slab, slab, blk), axis=0)

    q_aug, m_init = [], []
    for pp in range(n_pairs):
        qt = q_ref[0, :, pp * lanes:(pp + 1) * lanes].astype(F32).T
        k_own = kaug_ref[pp, pl.ds(own, blk), :lanes]
        qhs = [jnp.where((feat // (ATT_HEAD_DIM // 2)) % 2 == hh, qt, 0.0).astype(BF16)
               for hh in range(2)]
        q2 = jnp.concatenate(qhs, axis=1)
        gate = jnp.dot(kmean_ref[pp], jnp.concatenate([q2, q2, q2], axis=0),
                       preferred_element_type=F32)
        gate = jnp.where(blk_id < qi, gate, NEG)
        beaten = jnp.zeros((nbp, 2 * blk), F32)
        for n in range(nb):
            gn = gate[n:n + 1, :]
            wins = (gn > gate) | ((gn == gate) & (blk_id > n))
            beaten = beaten + jnp.where(wins, 1.0, 0.0)
        sel = (beaten < MOBA_TOPK) & (blk_id < qi)
        bias = jnp.where(sel, 0.0, NEG).astype(BF16)
        bias = jnp.concatenate([bias, jnp.zeros((lanes - nbp, 2 * blk), BF16)], axis=0)
        for hh in range(2):
            h = 2 * pp + hh
            q_aug.append(jnp.concatenate([qhs[hh], bias[:, hh * blk:(hh + 1) * blk]], axis=0))
            s = jnp.dot(k_own, qhs[hh], preferred_element_type=F32)
            s = jnp.where(key_i <= qry_i, s, NEG)
            sdiag_ref[h] = s
            m_init.append(slab_max(s))
            acc_ref[h] = jnp.zeros(acc_ref.shape[1:], F32)

    group = MOBA_GROUP * blk
    n_groups = (qi + MOBA_GROUP) // MOBA_GROUP

    def score_group(g, ms):
        off = pl.multiple_of(g * group, group)
        out = []
        for pp in range(n_pairs):
            kb = kaug_ref[pp, pl.ds(off, group), :]
            for hh in range(2):
                h = 2 * pp + hh
                s = jnp.dot(kb, q_aug[h], preferred_element_type=F32)
                s_ref[h, pl.ds(off, group), :] = s
                out.append(jnp.maximum(ms[h], slab_max(s)))
        return tuple(out)

    ms = lax.fori_loop(0, n_groups, score_group, tuple(m_init))

    m_fin = []
    for h in range(len(heads)):
        s_ref[h, pl.ds(own, blk), :] = sdiag_ref[h]
        m_fin.append(jnp.max(ms[h], axis=0, keepdims=True))

    vgroup = p_ref.shape[1]
    n_vgroups = (qi * blk + vgroup) // vgroup

    def exp_group(g):
        off = pl.multiple_of(g * vgroup, vgroup)
        for h in range(len(heads)):
            p_ref[h] = jnp.exp(s_ref[h, pl.ds(off, vgroup), :] - m_fin[h]).astype(BF16)

    def value_group(g):
        off = pl.multiple_of(g * vgroup, vgroup)
        for pp in range(n_pairs):
            vb = vt_ref[pp, :, pl.ds(off, vgroup)]
            for hh in range(2):
                h = 2 * pp + hh
                acc_ref[h] += jnp.dot(vb, p_ref[h], preferred_element_type=F32)

    def pipelined(g, c):
        value_group(g - 1)
        exp_group(g)
        return c

    exp_group(0)
    lax.fori_loop(1, n_vgroups, pipelined, 0)
    value_group(n_vgroups - 1)

    half = lanes // 2
    outs = []
    for pp in range(n_pairs):
        outs.append(acc_ref[2 * pp, :half, :] / acc_ref[2 * pp, lanes:lanes + 1, :])
        outs.append(acc_ref[2 * pp + 1, half:lanes, :] / acc_ref[2 * pp + 1, lanes:lanes + 1, :])
    o_ref[0] = jnp.concatenate(outs, axis=0).T.astype(BF16)


def _moba(q, k, v):
    B, S, _ = q.shape
    nb = S // MOBA_BLOCK
    assert nb % MOBA_GROUP == 0
    nbp = -(-nb // (2 * V7X_SUBLANES)) * (2 * V7X_SUBLANES)
    n_pairs = ATT_WIDTH // V7X_LANES
    pp = MOBA_PAIRS
    assert n_pairs % pp == 0
    vt_rows = V7X_LANES + 2 * V7X_SUBLANES
    qspec = pl.BlockSpec((1, MOBA_BLOCK, pp * V7X_LANES), lambda b, p, i: (b, i, p))
    kvspec = pl.BlockSpec((1, S, pp * V7X_LANES), lambda b, p, i: (b, 0, p))
    return pl.pallas_call(
        functools.partial(_moba_kernel, nb=nb),
        grid=(B, n_pairs // pp, nb),
        in_specs=[qspec, kvspec, kvspec],
        out_specs=qspec,
        out_shape=jax.ShapeDtypeStruct((B, S, ATT_WIDTH), BF16),
        scratch_shapes=[pltpu.VMEM((pp, S, 2 * V7X_LANES), BF16),
                        pltpu.VMEM((pp, nbp, 3 * V7X_LANES), BF16),
                        pltpu.VMEM((pp, vt_rows, S), BF16),
                        pltpu.VMEM((2 * pp, vt_rows, MOBA_BLOCK), F32),
                        pltpu.VMEM((2 * pp, S, MOBA_BLOCK), F32),
                        pltpu.VMEM((2 * pp, MOBA_BLOCK, MOBA_BLOCK), F32),
                        pltpu.VMEM((2 * pp, MOBA_VALUE_GROUP * MOBA_BLOCK, MOBA_BLOCK), BF16)],
        compiler_params=_cparams("parallel", "parallel", "arbitrary"),
        name="moba",
    )(q, k, v)


def _hgrn_level_sizes(chunk):
    return [chunk >> (i + 1) for i in range(int(math.log2(chunk)))]


def _hgrn_constants(chunk):
    t = np.arange(chunk)
    mats = [(t[None, :] <= t[:, None]),
            (t[None, :] > t[:, None])]
    qrows, pmasks = [], []
    for bs in _hgrn_level_sizes(chunk):
        blk = t // bs
        odd = (blk % 2) == 1
        lo, hi = blk * bs, (blk + 1) * bs
        u = t[None, :]
        m_odd = (u >= lo[:, None]) & (u <= t[:, None])
        m_even = (u > t[:, None]) & (u < hi[:, None])
        mats.append(np.where(odd[:, None], m_odd, m_even))
        qrows.append(odd)
        pmasks.append(odd[:, None] & (blk[None, :] == blk[:, None] - 1))
    pmasks.append(t[None, :] == t[:, None])
    summat = np.concatenate(mats, axis=0).astype(np.float32)
    qrow = np.stack(qrows, axis=0).astype(np.float32)
    pmask = np.stack(pmasks, axis=0).astype(np.float32)
    return summat, qrow, pmask


def _hgrn_kernel(q_ref, f_ref, i_ref, g_ref, lb_ref, nw_ref, sm_ref, qrow_ref, pm_ref,
                 o_ref, state_ref, *, rows):
    C = HGRN_CHUNK
    n_levels = qrow_ref.shape[0]

    @pl.when(pl.program_id(1) == 0)
    def _():
        state_ref[...] = jnp.zeros_like(state_ref)

    lb = lb_ref[...]
    summat = sm_ref[...]
    for c in range(rows // C):
        rs = slice(c * C, (c + 1) * C)
        fg = lb + (1.0 - lb) * jax.nn.sigmoid(f_ref[0, rs, :])
        logf = jnp.log(fg)
        hi = logf.astype(BF16)
        rem = logf - hi.astype(F32)
        mid = rem.astype(BF16)
        lo = (rem - mid.astype(F32)).astype(BF16)
        sums = jnp.dot(summat, jnp.concatenate([hi, mid, lo], axis=0),
                       preferred_element_type=F32)
        for h in range(HGRN_HEADS):
            ls = slice(h * HGRN_DIM, (h + 1) * HGRN_DIM)
            qf = jax.nn.silu(q_ref[0, rs, ls].astype(F32))
            kf = 1.0 - fg[:, ls]
            iv = i_ref[0, rs, ls]
            bcum = sums[0:C, ls]
            bsuf = sums[C:2 * C, ls]
            att = _nt_dot(qf.astype(BF16), kf.astype(BF16)) * pm_ref[n_levels]
            for lv in range(n_levels):
                w = jnp.exp(sums[(2 + lv) * C:(3 + lv) * C, ls])
                qrow = qrow_ref[lv]
                z = (jnp.where(qrow > 0.5, qf, kf) * w).astype(BF16)
                att = att + _nt_dot(z, z) * pm_ref[lv]
            o = jnp.dot(att.astype(BF16), iv, preferred_element_type=F32)
            st = state_ref[h]
            o = o + _nt_dot((qf * jnp.exp(bcum)).astype(BF16), st.astype(BF16))
            kdec = (kf * jnp.exp(bsuf)).astype(BF16)
            state_ref[h] = st * jnp.exp(bcum[C - 1:C, :]) + _tn_dot(iv, kdec)
            o = o * lax.rsqrt(jnp.mean(o * o, axis=-1, keepdims=True) + RMS_EPS)
            o = o * nw_ref[:, ls] * jax.nn.silu(g_ref[0, rs, ls].astype(F32))
            o_ref[0, rs, ls] = o.astype(BF16)


def _hgrn(qb, fb, ib, gb, lb, norm_w):
    B, S, W = qb.shape
    rows = min(HGRN_ROWS, S)
    summat, qrow, pmask = _hgrn_constants(HGRN_CHUNK)
    summat = jnp.asarray(np.concatenate([summat] * 3, axis=1), BF16)
    n_levels = qrow.shape[0]
    blk = pl.BlockSpec((1, rows, W), lambda b, s: (b, s, 0))
    vec = pl.BlockSpec((1, W), lambda b, s: (0, 0))
    const = lambda a: pl.BlockSpec(a.shape, lambda b, s: (0,) * a.ndim)
    qrow3 = qrow.reshape(n_levels, HGRN_CHUNK, 1)
    return pl.pallas_call(
        functools.partial(_hgrn_kernel, rows=rows),
        grid=(B, S // rows),
        in_specs=[blk, blk, blk, blk, vec, vec, const(summat), const(qrow3), const(pmask)],
        out_specs=blk,
        out_shape=jax.ShapeDtypeStruct((B, S, W), BF16),
        scratch_shapes=[pltpu.VMEM((HGRN_HEADS, HGRN_DIM, HGRN_DIM), F32)],
        compiler_params=_cparams("parallel", "arbitrary"),
        name="hgrn",
    )(qb, fb, ib, gb, lb, norm_w, jnp.asarray(summat), jnp.asarray(qrow3), jnp.asarray(pmask))


def _mix_route_kernel(ya_ref, yb_ref, ga_ref, gb_ref, x_ref, wa_ref, wb_ref, wo_ref, nw_ref,
                      rw_ref, rb_ref, tri_ref,
                      x1_ref, h2_ref, e_ref, w_ref, rank_ref, cnt_ref, carry_ref):
    @pl.when(pl.program_id(0) == 0)
    def _():
        carry_ref[...] = jnp.zeros_like(carry_ref)

    ua = jnp.dot(ya_ref[...], wa_ref[...], preferred_element_type=F32)
    ub = jnp.dot(yb_ref[...], wb_ref[...], preferred_element_type=F32)
    mixed = (jax.nn.sigmoid(ga_ref[...].astype(F32)) * ua
             + jax.nn.sigmoid(gb_ref[...].astype(F32)) * ub)
    x1 = x_ref[...] + jnp.dot(mixed.astype(BF16), wo_ref[...], preferred_element_type=F32)
    x1_ref[...] = x1
    h2 = x1 * lax.rsqrt(jnp.mean(x1 * x1, axis=-1, keepdims=True) + RMS_EPS) * nw_ref[...]
    half = h2.shape[1] // 2
    h2_ref[...] = _pack_bf16_pair(h2[:, :half], h2[:, half:])

    tm = x1.shape[0]
    h2_hi = h2.astype(BF16)
    h2_lo = (h2 - h2_hi.astype(F32)).astype(BF16)
    logits = _nt_dot(rw_ref[...], jnp.concatenate([h2_hi, h2_lo, h2_hi], axis=1)) + rb_ref[...]
    eid = lax.broadcasted_iota(jnp.int32, (N_EXPERTS, tm), 0)
    work = logits
    es, vs = [], []
    for _ in range(TOP_K):
        mx = jnp.max(work, axis=0, keepdims=True)
        idx = jnp.min(jnp.where(work == mx, eid, N_EXPERTS), axis=0, keepdims=True)
        es.append(idx)
        vs.append(mx)
        work = jnp.where(eid == idx, -jnp.inf, work)
    ex = [jnp.exp(v - vs[0]) for v in vs]
    den = ex[0] + ex[1] + ex[2] + ex[3]
    multi = jnp.zeros((N_EXPERTS, tm), F32)
    for k in range(TOP_K):
        multi = multi + jnp.where(eid == es[k], 1.0, 0.0)
    before = jnp.dot(multi.astype(BF16), tri_ref[...], preferred_element_type=F32) + carry_ref[...]
    for k in range(TOP_K):
        e_ref[k:k + 1, :] = es[k]
        w_ref[k:k + 1, :] = ex[k] / den
        rank_ref[k:k + 1, :] = jnp.sum(jnp.where(eid == es[k], before, 0.0), axis=0,
                                       keepdims=True).astype(jnp.int32)
    carry_ref[...] = carry_ref[...] + jnp.sum(multi, axis=1, keepdims=True)
    cnt_ref[...] = jnp.broadcast_to(carry_ref[...], cnt_ref.shape)


def _mix_route(ya, yb, ga, gb, x2, wa, wb, wo, norm_w, rw_t, rb, row0, rows):
    D = x2.shape[1]
    T = rows
    tm = min(MIX_ROWS, T)
    row = lambda w: pl.BlockSpec((tm, w), lambda i: (i + row0 // tm, 0))
    out_row = pl.BlockSpec((tm, D), lambda i: (i, 0))
    whole = pl.BlockSpec(memory_space=pltpu.VMEM)
    kt = pl.BlockSpec((TOP_K, tm), lambda i: (0, i))
    tri = jnp.asarray(np.triu(np.ones((tm, tm), np.float32), 1), BF16)
    return pl.pallas_call(
        _mix_route_kernel,
        grid=(T // tm,),
        in_specs=[row(ATT_WIDTH), row(HGRN_WIDTH), row(D), row(D), row(D),
                  whole, whole, whole, pl.BlockSpec((1, D), lambda i: (0, 0)),
                  whole, whole, whole],
        out_specs=[out_row, pl.BlockSpec((tm, D // 2), lambda i: (i, 0)), kt, kt, kt,
                   pl.BlockSpec((N_EXPERTS, V7X_LANES), lambda i: (0, 0))],
        out_shape=[jax.ShapeDtypeStruct((T, D), F32), jax.ShapeDtypeStruct((T, D // 2), jnp.uint32),
                   jax.ShapeDtypeStruct((TOP_K, T), jnp.int32),
                   jax.ShapeDtypeStruct((TOP_K, T), F32),
                   jax.ShapeDtypeStruct((TOP_K, T), jnp.int32),
                   jax.ShapeDtypeStruct((N_EXPERTS, V7X_LANES), F32)],
        scratch_shapes=[pltpu.VMEM((N_EXPERTS, 1), F32)],
        compiler_params=_cparams("arbitrary"),
        name="mix_route",
    )(ya, yb, ga, gb, x2, wa, wb, wo, norm_w, rw_t, rb, tri)


def _sc_worker_id():
    return lax.axis_index("s") * V7X_SC_CORES + lax.axis_index("c")


def _sc_kernel(body, out_rows, like, window, name):
    mesh = plsc.VectorSubcoreMesh(core_axis_name="c", subcore_axis_name="s")
    return pl.kernel(
        body, mesh=mesh,
        out_type=jax.ShapeDtypeStruct((out_rows,) + like.shape[1:], like.dtype),
        scratch_types=[pltpu.VMEM((window,), jnp.int32),
                       pltpu.VMEM((window,) + like.shape[1:], like.dtype)],
        name=name)


def _scatter_rows(dest, h2):
    T = dest.shape[1]
    n_workers = V7X_SC_CORES * V7X_SC_SUBCORES
    per_worker = T // n_workers
    window = min(SC_ROWS, per_worker)

    def body(dest_hbm, src_hbm, dst_hbm, idx_v, rows_v):
        base = _sc_worker_id() * per_worker

        @pl.loop(0, per_worker // window)
        def _(c):
            t0 = base + c * window
            pltpu.sync_copy(src_hbm.at[pl.ds(t0, window)], rows_v)
            for k in range(TOP_K):
                pltpu.sync_copy(dest_hbm.at[pl.ds(k * T + t0, window)], idx_v)
                pltpu.sync_copy(rows_v, dst_hbm.at[idx_v])

    return _sc_kernel(body, TOP_K * T, h2, window, "scatter_rows")(dest.reshape(-1), h2)


def _gather_rows(dest, yg):
    T = dest.shape[1]
    n_workers = V7X_SC_CORES * V7X_SC_SUBCORES
    per_worker = TOP_K * T // n_workers
    window = min(SC_ROWS, per_worker)

    def body(dest_hbm, src_hbm, dst_hbm, idx_v, rows_v):
        base = _sc_worker_id() * per_worker

        @pl.loop(0, per_worker // window)
        def _(c):
            r0 = base + c * window
            pltpu.sync_copy(dest_hbm.at[pl.ds(r0, window)], idx_v)
            pltpu.sync_copy(src_hbm.at[idx_v], rows_v)
            pltpu.sync_copy(rows_v, dst_hbm.at[pl.ds(r0, window)])

    return _sc_kernel(body, TOP_K * T, yg, window, "gather_rows")(dest.reshape(-1), yg)


def _experts_kernel(blk_ref, exp_ref, lo_ref, hi_ref, slot_ref, efirst_ref, enext_ref,
                    x_ref, wgu_hbm, bg_ref, bu_ref, wd_hbm, bd_ref, perm_ref, y_ref,
                    wgu_stage, wd_stage, wg_buf, wu_buf, wd_buf, sems):
    w = pl.program_id(0)
    lo = lo_ref[w]
    hi = hi_ref[w]
    slot = slot_ref[w]

    tm = x_ref.shape[0]
    n_out = V7X_MXU_DIM

    def weight_copies(e, s):
        return [pltpu.make_async_copy(wgu_hbm.at[e], wgu_stage.at[s], sems.at[s, 0]),
                pltpu.make_async_copy(wd_hbm.at[e], wd_stage.at[s], sems.at[s, 1])]

    @pl.when(w == 0)
    def _():
        for c in weight_copies(exp_ref[0], slot):
            c.start()

    @pl.when(efirst_ref[w] == 1)
    def _():
        for c in weight_copies(exp_ref[w], slot):
            c.wait()

        @pl.when(enext_ref[w] >= 0)
        def _():
            for c in weight_copies(enext_ref[w], 1 - slot):
                c.start()

        half = V7X_MXU_DIM // 2
        for g in range(wgu_stage.shape[2] // V7X_MXU_DIM):
            wb = wgu_stage[slot, :, g * V7X_MXU_DIM:(g + 1) * V7X_MXU_DIM].astype(BF16)
            d = jnp.dot(wb, perm_ref[...], preferred_element_type=F32).astype(BF16)
            wg_buf[:, g * half:(g + 1) * half] = d[:, :half]
            wu_buf[:, g * half:(g + 1) * half] = d[:, half:]
        wd_buf[...] = wd_stage[slot].astype(BF16)

    sb = EXPERT_SUB_ROWS

    def ffn(r0, keep_below):
        rs = slice(r0, r0 + sb)
        x = jnp.concatenate(_unpack_bf16_pair(x_ref[rs, :]), axis=1).astype(BF16)
        g = jnp.dot(x, wg_buf[...], preferred_element_type=F32) + bg_ref[0]
        u = jnp.dot(x, wu_buf[...], preferred_element_type=F32) + bu_ref[0]
        g = jnp.minimum(g, SWIGLU_LIMIT)
        u = jnp.clip(u, -SWIGLU_LIMIT, SWIGLU_LIMIT)
        act = ((u + 1.0) * (g * jax.nn.sigmoid(g * SWIGLU_ALPHA))).astype(BF16)
        half = y_ref.shape[1]

        def down(c0):
            cs = slice(c0, c0 + n_out)
            return (jnp.dot(act, wd_buf[:, cs], preferred_element_type=F32)
                    + bd_ref[0, :, cs])

        for c in range(half // n_out):
            cs = slice(c * n_out, (c + 1) * n_out)
            word = _pack_bf16_pair(down(c * n_out), down(half + c * n_out))
            if keep_below is not None:
                r = lax.broadcasted_iota(jnp.int32, (sb, n_out), 0)
                word = jnp.where(r >= keep_below, word, y_ref[rs, cs])
            y_ref[rs, cs] = word

    n_sub = tm // sb
    fused = (lo == 0) & (hi > (n_sub - 1) * sb)

    @pl.when(fused)
    def _():
        for j in range(n_sub):
            ffn(j * sb, None)

    for j in range(n_sub):
        lo_j = lo - j * sb
        touched = jnp.logical_not(fused) & (lo < hi) & (lo_j < sb) & (hi > j * sb)

        @pl.when(touched & (lo_j <= 0))
        def _():
            ffn(j * sb, None)

        @pl.when(touched & (lo_j > 0))
        def _():
            ffn(j * sb, lo_j)


def _work_items(counts, n_rows, tm):
    nblk = n_rows // tm
    n_items = nblk + N_EXPERTS - 1
    end = jnp.cumsum(counts)
    start = end - counts
    fb = start // tm
    nitems = jnp.where(counts > 0, (end - 1) // tm - fb + 1, 0)
    item_end = jnp.cumsum(nitems)
    item_start = item_end - nitems
    w = jnp.arange(n_items, dtype=jnp.int32)
    valid = w < item_end[-1]
    wc = jnp.minimum(w, item_end[-1] - 1)
    e = jnp.sum(wc[:, None] >= item_end[None, :], axis=1).astype(jnp.int32)
    e = jnp.minimum(e, N_EXPERTS - 1)
    onehot = e[:, None] == jnp.arange(N_EXPERTS, dtype=jnp.int32)[None, :]
    pick = lambda table: jnp.sum(jnp.where(onehot, table[None, :], 0), axis=1)
    blk = (pick(fb) + (wc - pick(item_start))).astype(jnp.int32)
    lo = jnp.maximum(pick(start), blk * tm) - blk * tm
    hi = jnp.minimum(pick(end), (blk + 1) * tm) - blk * tm
    lo = jnp.where(valid, lo, 0).astype(jnp.int32)
    hi = jnp.where(valid, hi, 0).astype(jnp.int32)
    used = counts > 0
    ids = jnp.arange(N_EXPERTS, dtype=jnp.int32)
    slot = pick((jnp.cumsum(used) - 1) % 2).astype(jnp.int32)
    efirst = (valid & (w == pick(item_start))).astype(jnp.int32)
    later = jnp.where(used[None, :] & (ids[None, :] > ids[:, None]), ids[None, :], N_EXPERTS)
    next_used = jnp.min(later, axis=1)
    enext = pick(jnp.where(next_used < N_EXPERTS, next_used, -1)).astype(jnp.int32)
    return blk, e, lo, hi, slot, efirst, enext


def _gate_up_split_matrix():
    i = np.arange(V7X_MXU_DIM)
    src = np.where(i < V7X_MXU_DIM // 2, 2 * i, 2 * (i - V7X_MXU_DIM // 2) + 1)
    perm = np.zeros((V7X_MXU_DIM, V7X_MXU_DIM), np.float32)
    perm[src, i] = 1.0
    return jnp.asarray(perm, BF16)


def _experts(xg, counts, w_gate_up, bg, bu, w_down, bd):
    A = xg.shape[0]
    D = 2 * xg.shape[1]
    tm = min(EXPERT_ROWS, A)
    F = w_down.shape[1]
    items = _work_items(counts, A, tm)
    n_items = A // tm + N_EXPERTS - 1
    xs = pl.BlockSpec((tm, D // 2), lambda w, blk, e, *_: (blk[w], 0))
    bias = lambda c: pl.BlockSpec((1, 1, c), lambda w, blk, e, *_: (e[w], 0, 0))
    hbm = pl.BlockSpec(memory_space=pl.ANY)
    return pl.pallas_call(
        _experts_kernel,
        grid_spec=pltpu.PrefetchScalarGridSpec(
            num_scalar_prefetch=len(items),
            grid=(n_items,),
            in_specs=[xs, hbm, bias(F), bias(F), hbm, bias(D), pl.BlockSpec(memory_space=pltpu.VMEM)],
            out_specs=xs,
            scratch_shapes=[pltpu.VMEM((2, D, 2 * F), F32), pltpu.VMEM((2, F, D), F32),
                            pltpu.VMEM((D, F), BF16), pltpu.VMEM((D, F), BF16),
                            pltpu.VMEM((F, D), BF16), pltpu.SemaphoreType.DMA((2, 2))],
        ),
        out_shape=jax.ShapeDtypeStruct(xg.shape, jnp.uint32),
        compiler_params=_cparams("arbitrary"),
        name="experts",
    )(*items, xg, w_gate_up, bg, bu, w_down, bd, _gate_up_split_matrix())


def _combine_kernel(x1_ref, yk_ref, w_ref, nw_ref, *rest):
    o_ref = rest[-1]
    w = w_ref[...]
    x = x1_ref[...]
    for k in range(TOP_K):
        x = x + jnp.concatenate(_unpack_bf16_pair(yk_ref[k]), axis=1) * w[:, k:k + 1]
    o_ref[...] = x * lax.rsqrt(jnp.mean(x * x, axis=-1, keepdims=True) + RMS_EPS) * nw_ref[...]


def _combine(x1, yk, w_tk, norm_w, out_rows, row0, prev_out):
    Tg, D = x1.shape
    tm = min(COMBINE_ROWS, Tg)
    row = pl.BlockSpec((tm, D), lambda i: (i, 0))
    in_specs = [row, pl.BlockSpec((TOP_K, tm, D // 2), lambda i: (0, i, 0)),
                pl.BlockSpec((tm, TOP_K), lambda i: (i, 0)),
                pl.BlockSpec((1, D), lambda i: (0, 0))]
    args = [x1, yk, w_tk, norm_w]
    aliases = {}
    if prev_out is not None:
        in_specs.append(pl.BlockSpec(memory_space=pl.ANY))
        args.append(prev_out)
        aliases = {len(args) - 1: 0}
    return pl.pallas_call(
        _combine_kernel,
        grid=(Tg // tm,),
        in_specs=in_specs,
        out_specs=pl.BlockSpec((tm, D), lambda i: (i + row0 // tm, 0)),
        out_shape=jax.ShapeDtypeStruct((out_rows, D), F32),
        input_output_aliases=aliases,
        compiler_params=_cparams("parallel"),
        name="combine",
    )(*args)


def _qk_column_order():
    half = ATT_HEAD_DIM // 2
    order = []
    for p in range(ATT_HEADS // 2):
        for part in range(2):
            for h in (2 * p, 2 * p + 1):
                order.extend(range(h * ATT_HEAD_DIM + part * half, h * ATT_HEAD_DIM + (part + 1) * half))
    return np.asarray(order, np.int32)


def _rope_tables(seq):
    half = ATT_HEAD_DIM // 2
    inv = ROPE_THETA ** (-(jnp.arange(half, dtype=F32) * 2.0 / ATT_HEAD_DIM))
    ang = jnp.arange(seq, dtype=F32)[:, None] * inv[None, :]
    cos, sin = jnp.cos(ang), jnp.sin(ang)
    return (jnp.concatenate([cos, cos, cos, cos], axis=1),
            jnp.concatenate([-sin, -sin, sin, sin], axis=1))


def kernel(x, norm1_w, w_in, moba_up, hgrn_lb_logits, hgrn_norm_w, hgrn_up, w_out, norm2_w,
           router_w, router_b, w_gate_up, b_gate_up, w_down, b_down, final_norm_w):
    B, S, D = x.shape
    T = B * S
    assert S % MOBA_BLOCK == 0 and w_in.shape[0] == 1
    x2 = x.reshape(T, D)

    perm = _qk_column_order()
    w0 = w_in[0]
    w_in_p = jnp.concatenate([w0[:, :ATT_WIDTH][:, perm], w0[:, ATT_WIDTH:2 * ATT_WIDTH][:, perm],
                              w0[:, 2 * ATT_WIDTH:]], axis=1).astype(BF16)
    cos_t, sin_t = _rope_tables(S)
    lb = jnp.cumsum(jax.nn.softmax(hgrn_lb_logits.astype(F32), axis=0), axis=0)[0:1]

    qa, ka, va, qb, fb, ib, gb, ga, gtb = _in_proj(x2, norm1_w, w_in_p, cos_t, sin_t, S)
    r3 = lambda a: a.reshape(B, S, a.shape[1])
    ya = _moba(r3(qa), r3(ka), r3(va)).reshape(T, ATT_WIDTH)
    yb = _hgrn(r3(qb), r3(fb), r3(ib), r3(gb), lb, hgrn_norm_w).reshape(T, HGRN_WIDTH)

    rw_t = router_w[0].T
    rw_hi = rw_t.astype(BF16)
    rw_lo = (rw_t - rw_hi.astype(F32)).astype(BF16)
    mix_w = (moba_up[0].astype(BF16), hgrn_up[0].astype(BF16), w_out[0].astype(BF16), norm2_w,
             jnp.concatenate([rw_hi, rw_hi, rw_lo], axis=1), router_b[0][:, None])
    expert_w = (w_gate_up[0], b_gate_up[0][:, None, 0::2], b_gate_up[0][:, None, 1::2],
                w_down[0], b_down[0][:, None, :])

    Tg = T // MOE_TOKEN_GROUPS
    out = None
    for g in range(MOE_TOKEN_GROUPS):
        x1, h2, top_e, top_w, rank, cnt = _mix_route(ya, yb, ga, gtb, x2, *mix_w, g * Tg, Tg)
        counts = cnt[:, 0].astype(jnp.int32)
        start = jnp.cumsum(counts) - counts
        dest = rank + jnp.sum(jnp.where(top_e[:, :, None] == jnp.arange(N_EXPERTS)[None, None, :],
                                        start[None, None, :], 0), axis=-1)
        xg = _scatter_rows(dest, h2)
        yg = _experts(xg, counts, *expert_w)
        yk = _gather_rows(dest, yg).reshape(TOP_K, Tg, D // 2)
        out = _combine(x1, yk, top_w.T, final_norm_w[None, :], T, g * Tg, out)
    return out.reshape(B, S, D)
```

```python
import functools
import math

import numpy as np
import jax
import jax.numpy as jnp
from jax import lax
from jax.experimental import pallas as pl
from jax.experimental.pallas import tpu as pltpu
from jax.experimental.pallas import tpu_sc as plsc

ATT_HEADS = 8
ATT_HEAD_DIM = 64
ATT_WIDTH = ATT_HEADS * ATT_HEAD_DIM
MOBA_BLOCK = 256
MOBA_TOPK = 3
ROPE_THETA = 10000.0
HGRN_HEADS = 4
HGRN_DIM = 128
HGRN_WIDTH = HGRN_HEADS * HGRN_DIM
HGRN_CHUNK = 64
N_EXPERTS = 32
TOP_K = 4
SWIGLU_LIMIT = 7.0
SWIGLU_ALPHA = 1.702
RMS_EPS = 1e-6
NEG = -1e30

V7X_LANES = 128
V7X_SUBLANES = 8
V7X_MXU_DIM = 256
V7X_VMEM_LIMIT_BYTES = 56 * 1024 * 1024
V7X_SC_CORES = 2
V7X_SC_SUBCORES = 16

MOBA_PAIRS = 2
MOBA_GROUP = 4
MOBA_VALUE_GROUP = 4
PROJ_ROWS = 1024
PROJ_COLS = 512
HGRN_ROWS = 512
MIX_ROWS = 512
EXPERT_ROWS = 512
EXPERT_SUB_ROWS = 256
MOE_TOKEN_GROUPS = 2
SC_ROWS = 64
COMBINE_ROWS = 512

F32 = jnp.float32
BF16 = jnp.bfloat16


def _nt_dot(a, b, precision=None):
    return lax.dot_general(a, b, (((1,), (1,)), ((), ())), precision=precision,
                           preferred_element_type=F32)


def _tn_dot(a, b, precision=None):
    return lax.dot_general(a, b, (((0,), (0,)), ((), ())), precision=precision,
                           preferred_element_type=F32)


def _cparams(*sem):
    return pltpu.CompilerParams(dimension_semantics=sem, vmem_limit_bytes=V7X_VMEM_LIMIT_BYTES)


def _pack_bf16_pair(lo, hi):
    lo_bits = lax.bitcast_convert_type(lo.astype(BF16).astype(F32), jnp.uint32)
    hi_bits = lax.bitcast_convert_type(hi.astype(BF16).astype(F32), jnp.uint32)
    return (lo_bits >> 16) | hi_bits


def _unpack_bf16_pair(word):
    lo = lax.bitcast_convert_type(word << 16, F32)
    hi = lax.bitcast_convert_type(word & jnp.uint32(0xFFFF0000), F32)
    return lo, hi


def _in_proj_kernel(x_ref, nw_ref, w_ref, cos_ref, sin_ref,
                    qa_ref, ka_ref, va_ref, qb_ref, fb_ref, ib_ref, gb_ref, ga_ref, gtb_ref):
    x = x_ref[...]
    h = x * lax.rsqrt(jnp.mean(x * x, axis=-1, keepdims=True) + RMS_EPS) * nw_ref[...]
    h = h.astype(BF16)
    cos = cos_ref[...]
    sin = sin_ref[...]

    def proj(c):
        return jnp.dot(h, w_ref[:, c * PROJ_COLS:(c + 1) * PROJ_COLS], preferred_element_type=F32)

    def rope(t):
        out = []
        for j in range(PROJ_COLS // V7X_LANES):
            tj = t[:, j * V7X_LANES:(j + 1) * V7X_LANES]
            out.append(tj * cos + pltpu.roll(tj, V7X_LANES // 2, 1) * sin)
        return jnp.concatenate(out, axis=1)

    qa_ref[...] = (rope(proj(0)) * (ATT_HEAD_DIM ** -0.5)).astype(BF16)
    ka_ref[...] = rope(proj(1)).astype(BF16)
    va_ref[...] = proj(2).astype(BF16)
    qb_ref[...] = proj(3).astype(BF16)
    fb_ref[...] = proj(4)
    ib_ref[...] = proj(5).astype(BF16)
    gb_ref[...] = proj(6).astype(BF16)
    ga_ref[:, :PROJ_COLS] = proj(7).astype(BF16)
    ga_ref[:, PROJ_COLS:] = proj(8).astype(BF16)
    gtb_ref[:, :PROJ_COLS] = proj(9).astype(BF16)
    gtb_ref[:, PROJ_COLS:] = proj(10).astype(BF16)


def _in_proj(x2, norm_w, w_in_bf16, cos_t, sin_t, seq):
    T, D = x2.shape
    tm = min(PROJ_ROWS, seq)
    n_seq_tiles = seq // tm
    row = lambda w: pl.BlockSpec((tm, w), lambda i: (i, 0))
    tab = pl.BlockSpec((tm, V7X_LANES), lambda i: (i % n_seq_tiles, 0))
    widths = [ATT_WIDTH] * 3 + [HGRN_WIDTH] * 4 + [D, D]
    dtypes = [BF16, BF16, BF16, BF16, F32, BF16, BF16, BF16, BF16]
    return pl.pallas_call(
        _in_proj_kernel,
        grid=(T // tm,),
        in_specs=[row(D), pl.BlockSpec((1, D), lambda i: (0, 0)),
                  pl.BlockSpec(memory_space=pltpu.VMEM), tab, tab],
        out_specs=[row(w) for w in widths],
        out_shape=[jax.ShapeDtypeStruct((T, w), dt) for w, dt in zip(widths, dtypes)],
        compiler_params=_cparams("parallel"),
        name="in_proj",
    )(x2, norm_w, w_in_bf16, cos_t, sin_t)


def _moba_kernel(q_ref, k_ref, v_ref, o_ref, kaug_ref, kmean_ref, vt_ref, acc_ref, s_ref,
                 sdiag_ref, p_ref, *, nb):
    qi = pl.program_id(2)
    blk = MOBA_BLOCK
    lanes = V7X_LANES
    n_pairs = kaug_ref.shape[0]
    heads = [(pp, hh) for pp in range(n_pairs) for hh in range(2)]
    nbp = kmean_ref.shape[1]

    @pl.when(qi == 0)
    def _():
        rowb = lax.broadcasted_iota(jnp.int32, (nb * blk, lanes), 0) // blk
        col = lax.broadcasted_iota(jnp.int32, (nb * blk, lanes), 1)
        onehot = jnp.where(rowb == col, 1.0, 0.0).astype(BF16)
        for pp in range(n_pairs):
            pl_ = slice(pp * lanes, (pp + 1) * lanes)
            kaug_ref[pp, :, :lanes] = k_ref[0, :, pl_]
            kaug_ref[pp, :, lanes:] = onehot
            means = [jnp.sum(k_ref[0, n * blk:(n + 1) * blk, pl_].astype(F32), axis=0,
                             keepdims=True) * (1.0 / blk) for n in range(nb)]
            km = jnp.concatenate(means + [jnp.zeros((nbp - nb, lanes), F32)] * (nbp > nb), axis=0)
            hi = km.astype(BF16)
            rem = km - hi.astype(F32)
            mid = rem.astype(BF16)
            lo = (rem - mid.astype(F32)).astype(BF16)
            kmean_ref[pp] = jnp.concatenate([hi, mid, lo], axis=1)
            for c in range(nb):
                vt = v_ref[0, c * blk:(c + 1) * blk, pl_].astype(F32).T
                vt_ref[pp, :lanes, c * blk:(c + 1) * blk] = vt.astype(BF16)
            vt_ref[pp, lanes:, :] = jnp.ones((vt_ref.shape[1] - lanes, nb * blk), BF16)

    feat = lax.broadcasted_iota(jnp.int32, (lanes, blk), 0)
    key_i = lax.broadcasted_iota(jnp.int32, (blk, blk), 0)
    qry_i = lax.broadcasted_iota(jnp.int32, (blk, blk), 1)
    own = pl.multiple_of(qi * blk, blk)
    blk_id = lax.broadcasted_iota(jnp.int32, (nbp, 2 * blk), 0)
    slab = 2 * V7X_SUBLANES

    def slab_max(s):
        return jnp.max(s.reshape(s.shape[0] // slab, slab, blk), axis=0)

    q_aug, m_init = [], []
    for pp in range(n_pairs):
        qt = q_ref[0, :, pp * lanes:(pp + 1) * lanes].astype(F32).T
        k_own = kaug_ref[pp, pl.ds(own, blk), :lanes]
        qhs = [jnp.where((feat // (ATT_HEAD_DIM // 2)) % 2 == hh, qt, 0.0).astype(BF16)
               for hh in range(2)]
        q2 = jnp.concatenate(qhs, axis=1)
        gate = jnp.dot(kmean_ref[pp], jnp.concatenate([q2, q2, q2], axis=0),
                       preferred_element_type=F32)
        gate = jnp.where(blk_id < qi, gate, NEG)
        beaten = jnp.zeros((nbp, 2 * blk), F32)
        for n in range(nb):
            gn = gate[n:n + 1, :]
            wins = (gn > gate) | ((gn == gate) & (blk_id > n))
            beaten = beaten + jnp.where(wins, 1.0, 0.0)
        sel = (beaten < MOBA_TOPK) & (blk_id < qi)
        bias = jnp.where(sel, 0.0, NEG).astype(BF16)
        bias = jnp.concatenate([bias, jnp.zeros((lanes - nbp, 2 * blk), BF16)], axis=0)
        for hh in range(2):
            h = 2 * pp + hh
            q_aug.append(jnp.concatenate([qhs[hh], bias[:, hh * blk:(hh + 1) * blk]], axis=0))
            s = jnp.dot(k_own, qhs[hh], preferred_element_type=F32)
            s = jnp.where(key_i <= qry_i, s, NEG)
            sdiag_ref[h] = s
            m_init.append(slab_max(s))
            acc_ref[h] = jnp.zeros(acc_ref.shape[1:], F32)

    group = MOBA_GROUP * blk
    n_groups = (qi + MOBA_GROUP) // MOBA_GROUP

    def score_group(g, ms):
        off = pl.multiple_of(g * group, group)
        out = []
        for pp in range(n_pairs):
            kb = kaug_ref[pp, pl.ds(off, group), :]
            for hh in range(2):
                h = 2 * pp + hh
                s = jnp.dot(kb, q_aug[h], preferred_element_type=F32)
                s_ref[h, pl.ds(off, group), :] = s
                out.append(jnp.maximum(ms[h], slab_max(s)))
        return tuple(out)

    ms = lax.fori_loop(0, n_groups, score_group, tuple(m_init))

    m_fin = []
    for h in range(len(heads)):
        s_ref[h, pl.ds(own, blk), :] = sdiag_ref[h]
        m_fin.append(jnp.max(ms[h], axis=0, keepdims=True))

    vgroup = p_ref.shape[1]
    n_vgroups = (qi * blk + vgroup) // vgroup

    def exp_group(g):
        off = pl.multiple_of(g * vgroup, vgroup)
        for h in range(len(heads)):
            p_ref[h] = jnp.exp(s_ref[h, pl.ds(off, vgroup), :] - m_fin[h]).astype(BF16)

    def value_group(g):
        off = pl.multiple_of(g * vgroup, vgroup)
        for pp in range(n_pairs):
            vb = vt_ref[pp, :, pl.ds(off, vgroup)]
            for hh in range(2):
                h = 2 * pp + hh
                acc_ref[h] += jnp.dot(vb, p_ref[h], preferred_element_type=F32)

    def pipelined(g, c):
        value_group(g - 1)
        exp_group(g)
        return c

    exp_group(0)
    lax.fori_loop(1, n_vgroups, pipelined, 0)
    value_group(n_vgroups - 1)

    half = lanes // 2
    outs = []
    for pp in range(n_pairs):
        outs.append(acc_ref[2 * pp, :half, :] / acc_ref[2 * pp, lanes:lanes + 1, :])
        outs.append(acc_ref[2 * pp + 1, half:lanes, :] / acc_ref[2 * pp + 1, lanes:lanes + 1, :])
    o_ref[0] = jnp.concatenate(outs, axis=0).T.astype(BF16)


def _moba(q, k, v):
    B, S, _ = q.shape
    nb = S // MOBA_BLOCK
    assert nb % MOBA_GROUP == 0
    nbp = -(-nb // (2 * V7X_SUBLANES)) * (2 * V7X_SUBLANES)
    n_pairs = ATT_WIDTH // V7X_LANES
    pp = MOBA_PAIRS
    assert n_pairs % pp == 0
    vt_rows = V7X_LANES + 2 * V7X_SUBLANES
    qspec = pl.BlockSpec((1, MOBA_BLOCK, pp * V7X_LANES), lambda b, p, i: (b, i, p))
    kvspec = pl.BlockSpec((1, S, pp * V7X_LANES), lambda b, p, i: (b, 0, p))
    return pl.pallas_call(
        functools.partial(_moba_kernel, nb=nb),
        grid=(B, n_pairs // pp, nb),
        in_specs=[qspec, kvspec, kvspec],
        out_specs=qspec,
        out_shape=jax.ShapeDtypeStruct((B, S, ATT_WIDTH), BF16),
        scratch_shapes=[pltpu.VMEM((pp, S, 2 * V7X_LANES), BF16),
                        pltpu.VMEM((pp, nbp, 3 * V7X_LANES), BF16),
                        pltpu.VMEM((pp, vt_rows, S), BF16),
                        pltpu.VMEM((2 * pp, vt_rows, MOBA_BLOCK), F32),
                        pltpu.VMEM((2 * pp, S, MOBA_BLOCK), F32),
                        pltpu.VMEM((2 * pp, MOBA_BLOCK, MOBA_BLOCK), F32),
                        pltpu.VMEM((2 * pp, MOBA_VALUE_GROUP * MOBA_BLOCK, MOBA_BLOCK), BF16)],
        compiler_params=_cparams("parallel", "parallel", "arbitrary"),
        name="moba",
    )(q, k, v)


def _hgrn_level_sizes(chunk):
    return [chunk >> (i + 1) for i in range(int(math.log2(chunk)))]


def _hgrn_constants(chunk):
    t = np.arange(chunk)
    mats = [(t[None, :] <= t[:, None]),
            (t[None, :] > t[:, None])]
    qrows, pmasks = [], []
    for bs in _hgrn_level_sizes(chunk):
        blk = t // bs
        odd = (blk % 2) == 1
        lo, hi = blk * bs, (blk + 1) * bs
        u = t[None, :]
        m_odd = (u >= lo[:, None]) & (u <= t[:, None])
        m_even = (u > t[:, None]) & (u < hi[:, None])
        mats.append(np.where(odd[:, None], m_odd, m_even))
        qrows.append(odd)
        pmasks.append(odd[:, None] & (blk[None, :] == blk[:, None] - 1))
    pmasks.append(t[None, :] == t[:, None])
    summat = np.concatenate(mats, axis=0).astype(np.float32)
    qrow = np.stack(qrows, axis=0).astype(np.float32)
    pmask = np.stack(pmasks, axis=0).astype(np.float32)
    return summat, qrow, pmask


def _hgrn_kernel(q_ref, f_ref, i_ref, g_ref, lb_ref, nw_ref, sm_ref, qrow_ref, pm_ref,
                 o_ref, state_ref, *, rows):
    C = HGRN_CHUNK
    n_levels = qrow_ref.shape[0]

    @pl.when(pl.program_id(1) == 0)
    def _():
        state_ref[...] = jnp.zeros_like(state_ref)

    lb = lb_ref[...]
    summat = sm_ref[...]
    for c in range(rows // C):
        rs = slice(c * C, (c + 1) * C)
        fg = lb + (1.0 - lb) * jax.nn.sigmoid(f_ref[0, rs, :])
        logf = jnp.log(fg)
        hi = logf.astype(BF16)
        rem = logf - hi.astype(F32)
        mid = rem.astype(BF16)
        lo = (rem - mid.astype(F32)).astype(BF16)
        sums = jnp.dot(summat, jnp.concatenate([hi, mid, lo], axis=0),
                       preferred_element_type=F32)
        for h in range(HGRN_HEADS):
            ls = slice(h * HGRN_DIM, (h + 1) * HGRN_DIM)
            qf = jax.nn.silu(q_ref[0, rs, ls].astype(F32))
            kf = 1.0 - fg[:, ls]
            iv = i_ref[0, rs, ls]
            bcum = sums[0:C, ls]
            bsuf = sums[C:2 * C, ls]
            att = _nt_dot(qf.astype(BF16), kf.astype(BF16)) * pm_ref[n_levels]
            for lv in range(n_levels):
                w = jnp.exp(sums[(2 + lv) * C:(3 + lv) * C, ls])
                qrow = qrow_ref[lv]
                z = (jnp.where(qrow > 0.5, qf, kf) * w).astype(BF16)
                att = att + _nt_dot(z, z) * pm_ref[lv]
            o = jnp.dot(att.astype(BF16), iv, preferred_element_type=F32)
            st = state_ref[h]
            o = o + _nt_dot((qf * jnp.exp(bcum)).astype(BF16), st.astype(BF16))
            kdec = (kf * jnp.exp(bsuf)).astype(BF16)
            state_ref[h] = st * jnp.exp(bcum[C - 1:C, :]) + _tn_dot(iv, kdec)
            o = o * lax.rsqrt(jnp.mean(o * o, axis=-1, keepdims=True) + RMS_EPS)
            o = o * nw_ref[:, ls] * jax.nn.silu(g_ref[0, rs, ls].astype(F32))
            o_ref[0, rs, ls] = o.astype(BF16)


def _hgrn(qb, fb, ib, gb, lb, norm_w):
    B, S, W = qb.shape
    rows = min(HGRN_ROWS, S)
    summat, qrow, pmask = _hgrn_constants(HGRN_CHUNK)
    summat = jnp.asarray(np.concatenate([summat] * 3, axis=1), BF16)
    n_levels = qrow.shape[0]
    blk = pl.BlockSpec((1, rows, W), lambda b, s: (b, s, 0))
    vec = pl.BlockSpec((1, W), lambda b, s: (0, 0))
    const = lambda a: pl.BlockSpec(a.shape, lambda b, s: (0,) * a.ndim)
    qrow3 = qrow.reshape(n_levels, HGRN_CHUNK, 1)
    return pl.pallas_call(
        functools.partial(_hgrn_kernel, rows=rows),
        grid=(B, S // rows),
        in_specs=[blk, blk, blk, blk, vec, vec, const(summat), const(qrow3), const(pmask)],
        out_specs=blk,
        out_shape=jax.ShapeDtypeStruct((B, S, W), BF16),
        scratch_shapes=[pltpu.VMEM((HGRN_HEADS, HGRN_DIM, HGRN_DIM), F32)],
        compiler_params=_cparams("parallel", "arbitrary"),
        name="hgrn",
    )(qb, fb, ib, gb, lb, norm_w, jnp.asarray(summat), jnp.asarray(qrow3), jnp.asarray(pmask))


def _mix_route_kernel(ya_ref, yb_ref, ga_ref, gb_ref, x_ref, wa_ref, wb_ref, wo_ref, nw_ref,
                      rw_ref, rb_ref, tri_ref,
                      x1_ref, h2_ref, e_ref, w_ref, rank_ref, cnt_ref, carry_ref):
    @pl.when(pl.program_id(0) == 0)
    def _():
        carry_ref[...] = jnp.zeros_like(carry_ref)

    ua = jnp.dot(ya_ref[...], wa_ref[...], preferred_element_type=F32)
    ub = jnp.dot(yb_ref[...], wb_ref[...], preferred_element_type=F32)
    mixed = (jax.nn.sigmoid(ga_ref[...].astype(F32)) * ua
             + jax.nn.sigmoid(gb_ref[...].astype(F32)) * ub)
    x1 = x_ref[...] + jnp.dot(mixed.astype(BF16), wo_ref[...], preferred_element_type=F32)
    x1_ref[...] = x1
    h2 = x1 * lax.rsqrt(jnp.mean(x1 * x1, axis=-1, keepdims=True) + RMS_EPS) * nw_ref[...]
    half = h2.shape[1] // 2
    h2_ref[...] = _pack_bf16_pair(h2[:, :half], h2[:, half:])

    tm = x1.shape[0]
    h2_hi = h2.astype(BF16)
    h2_lo = (h2 - h2_hi.astype(F32)).astype(BF16)
    logits = _nt_dot(rw_ref[...], jnp.concatenate([h2_hi, h2_lo, h2_hi], axis=1)) + rb_ref[...]
    eid = lax.broadcasted_iota(jnp.int32, (N_EXPERTS, tm), 0)
    work = logits
    es, vs = [], []
    for _ in range(TOP_K):
        mx = jnp.max(work, axis=0, keepdims=True)
        idx = jnp.min(jnp.where(work == mx, eid, N_EXPERTS), axis=0, keepdims=True)
        es.append(idx)
        vs.append(mx)
        work = jnp.where(eid == idx, -jnp.inf, work)
    ex = [jnp.exp(v - vs[0]) for v in vs]
    den = ex[0] + ex[1] + ex[2] + ex[3]
    multi = jnp.zeros((N_EXPERTS, tm), F32)
    for k in range(TOP_K):
        multi = multi + jnp.where(eid == es[k], 1.0, 0.0)
    before = jnp.dot(multi.astype(BF16), tri_ref[...], preferred_element_type=F32) + carry_ref[...]
    for k in range(TOP_K):
        e_ref[k:k + 1, :] = es[k]
        w_ref[k:k + 1, :] = ex[k] / den
        rank_ref[k:k + 1, :] = jnp.sum(jnp.where(eid == es[k], before, 0.0), axis=0,
                                       keepdims=True).astype(jnp.int32)
    carry_ref[...] = carry_ref[...] + jnp.sum(multi, axis=1, keepdims=True)
    cnt_ref[...] = jnp.broadcast_to(carry_ref[...], cnt_ref.shape)


def _mix_route(ya, yb, ga, gb, x2, wa, wb, wo, norm_w, rw_t, rb, row0, rows):
    D = x2.shape[1]
    T = rows
    tm = min(MIX_ROWS, T)
    row = lambda w: pl.BlockSpec((tm, w), lambda i: (i + row0 // tm, 0))
    out_row = pl.BlockSpec((tm, D), lambda i: (i, 0))
    whole = pl.BlockSpec(memory_space=pltpu.VMEM)
    kt = pl.BlockSpec((TOP_K, tm), lambda i: (0, i))
    tri = jnp.asarray(np.triu(np.ones((tm, tm), np.float32), 1), BF16)
    return pl.pallas_call(
        _mix_route_kernel,
        grid=(T // tm,),
        in_specs=[row(ATT_WIDTH), row(HGRN_WIDTH), row(D), row(D), row(D),
                  whole, whole, whole, pl.BlockSpec((1, D), lambda i: (0, 0)),
                  whole, whole, whole],
        out_specs=[out_row, pl.BlockSpec((tm, D // 2), lambda i: (i, 0)), kt, kt, kt,
                   pl.BlockSpec((N_EXPERTS, V7X_LANES), lambda i: (0, 0))],
        out_shape=[jax.ShapeDtypeStruct((T, D), F32), jax.ShapeDtypeStruct((T, D // 2), jnp.uint32),
                   jax.ShapeDtypeStruct((TOP_K, T), jnp.int32),
                   jax.ShapeDtypeStruct((TOP_K, T), F32),
                   jax.ShapeDtypeStruct((TOP_K, T), jnp.int32),
                   jax.ShapeDtypeStruct((N_EXPERTS, V7X_LANES), F32)],
        scratch_shapes=[pltpu.VMEM((N_EXPERTS, 1), F32)],
        compiler_params=_cparams("arbitrary"),
        name="mix_route",
    )(ya, yb, ga, gb, x2, wa, wb, wo, norm_w, rw_t, rb, tri)


def _sc_worker_id():
    return lax.axis_index("s") * V7X_SC_CORES + lax.axis_index("c")


def _sc_kernel(body, out_rows, like, window, name):
    mesh = plsc.VectorSubcoreMesh(core_axis_name="c", subcore_axis_name="s")
    return pl.kernel(
        body, mesh=mesh,
        out_type=jax.ShapeDtypeStruct((out_rows,) + like.shape[1:], like.dtype),
        scratch_types=[pltpu.VMEM((window,), jnp.int32),
                       pltpu.VMEM((window,) + like.shape[1:], like.dtype)],
        name=name)


def _scatter_rows(dest, h2):
    T = dest.shape[1]
    n_workers = V7X_SC_CORES * V7X_SC_SUBCORES
    per_worker = T // n_workers
    window = min(SC_ROWS, per_worker)

    def body(dest_hbm, src_hbm, dst_hbm, idx_v, rows_v):
        base = _sc_worker_id() * per_worker

        @pl.loop(0, per_worker // window)
        def _(c):
            t0 = base + c * window
            pltpu.sync_copy(src_hbm.at[pl.ds(t0, window)], rows_v)
            for k in range(TOP_K):
                pltpu.sync_copy(dest_hbm.at[pl.ds(k * T + t0, window)], idx_v)
                pltpu.sync_copy(rows_v, dst_hbm.at[idx_v])

    return _sc_kernel(body, TOP_K * T, h2, window, "scatter_rows")(dest.reshape(-1), h2)


def _gather_rows(dest, yg):
    T = dest.shape[1]
    n_workers = V7X_SC_CORES * V7X_SC_SUBCORES
    per_worker = TOP_K * T // n_workers
    window = min(SC_ROWS, per_worker)

    def body(dest_hbm, src_hbm, dst_hbm, idx_v, rows_v):
        base = _sc_worker_id() * per_worker

        @pl.loop(0, per_worker // window)
        def _(c):
            r0 = base + c * window
            pltpu.sync_copy(dest_hbm.at[pl.ds(r0, window)], idx_v)
            pltpu.sync_copy(src_hbm.at[idx_v], rows_v)
            pltpu.sync_copy(rows_v, dst_hbm.at[pl.ds(r0, window)])

    return _sc_kernel(body, TOP_K * T, yg, window, "gather_rows")(dest.reshape(-1), yg)


def _experts_kernel(blk_ref, exp_ref, lo_ref, hi_ref, slot_ref, efirst_ref, enext_ref,
                    x_ref, wgu_hbm, bg_ref, bu_ref, wd_hbm, bd_ref, perm_ref, y_ref,
                    wgu_stage, wd_stage, wg_buf, wu_buf, wd_buf, sems):
    w = pl.program_id(0)
    lo = lo_ref[w]
    hi = hi_ref[w]
    slot = slot_ref[w]

    tm = x_ref.shape[0]
    n_out = V7X_MXU_DIM

    def weight_copies(e, s):
        return [pltpu.make_async_copy(wgu_hbm.at[e], wgu_stage.at[s], sems.at[s, 0]),
                pltpu.make_async_copy(wd_hbm.at[e], wd_stage.at[s], sems.at[s, 1])]

    @pl.when(w == 0)
    def _():
        for c in weight_copies(exp_ref[0], slot):
            c.start()

    @pl.when(efirst_ref[w] == 1)
    def _():
        for c in weight_copies(exp_ref[w], slot):
            c.wait()

        @pl.when(enext_ref[w] >= 0)
        def _():
            for c in weight_copies(enext_ref[w], 1 - slot):
                c.start()

        half = V7X_MXU_DIM // 2
        for g in range(wgu_stage.shape[2] // V7X_MXU_DIM):
            wb = wgu_stage[slot, :, g * V7X_MXU_DIM:(g + 1) * V7X_MXU_DIM].astype(BF16)
            d = jnp.dot(wb, perm_ref[...], preferred_element_type=F32).astype(BF16)
            wg_buf[:, g * half:(g + 1) * half] = d[:, :half]
            wu_buf[:, g * half:(g + 1) * half] = d[:, half:]
        wd_buf[...] = wd_stage[slot].astype(BF16)

    sb = EXPERT_SUB_ROWS

    def ffn(r0, keep_below):
        rs = slice(r0, r0 + sb)
        x = jnp.concatenate(_unpack_bf16_pair(x_ref[rs, :]), axis=1).astype(BF16)
        g = jnp.dot(x, wg_buf[...], preferred_element_type=F32) + bg_ref[0]
        u = jnp.dot(x, wu_buf[...], preferred_element_type=F32) + bu_ref[0]
        g = jnp.minimum(g, SWIGLU_LIMIT)
        u = jnp.clip(u, -SWIGLU_LIMIT, SWIGLU_LIMIT)
        act = ((u + 1.0) * (g * jax.nn.sigmoid(g * SWIGLU_ALPHA))).astype(BF16)
        half = y_ref.shape[1]

        def down(c0):
            cs = slice(c0, c0 + n_out)
            return (jnp.dot(act, wd_buf[:, cs], preferred_element_type=F32)
                    + bd_ref[0, :, cs])

        for c in range(half // n_out):
            cs = slice(c * n_out, (c + 1) * n_out)
            word = _pack_bf16_pair(down(c * n_out), down(half + c * n_out))
            if keep_below is not None:
                r = lax.broadcasted_iota(jnp.int32, (sb, n_out), 0)
                word = jnp.where(r >= keep_below, word, y_ref[rs, cs])
            y_ref[rs, cs] = word

    n_sub = tm // sb
    fused = (lo == 0) & (hi > (n_sub - 1) * sb)

    @pl.when(fused)
    def _():
        for j in range(n_sub):
            ffn(j * sb, None)

    for j in range(n_sub):
        lo_j = lo - j * sb
        touched = jnp.logical_not(fused) & (lo < hi) & (lo_j < sb) & (hi > j * sb)

        @pl.when(touched & (lo_j <= 0))
        def _():
            ffn(j * sb, None)

        @pl.when(touched & (lo_j > 0))
        def _():
            ffn(j * sb, lo_j)


def _work_items(counts, n_rows, tm):
    nblk = n_rows // tm
    n_items = nblk + N_EXPERTS - 1
    end = jnp.cumsum(counts)
    start = end - counts
    fb = start // tm
    nitems = jnp.where(counts > 0, (end - 1) // tm - fb + 1, 0)
    item_end = jnp.cumsum(nitems)
    item_start = item_end - nitems
    w = jnp.arange(n_items, dtype=jnp.int32)
    valid = w < item_end[-1]
    wc = jnp.minimum(w, item_end[-1] - 1)
    e = jnp.sum(wc[:, None] >= item_end[None, :], axis=1).astype(jnp.int32)
    e = jnp.minimum(e, N_EXPERTS - 1)
    onehot = e[:, None] == jnp.arange(N_EXPERTS, dtype=jnp.int32)[None, :]
    pick = lambda table: jnp.sum(jnp.where(onehot, table[None, :], 0), axis=1)
    blk = (pick(fb) + (wc - pick(item_start))).astype(jnp.int32)
    lo = jnp.maximum(pick(start), blk * tm) - blk * tm
    hi = jnp.minimum(pick(end), (blk + 1) * tm) - blk * tm
    lo = jnp.where(valid, lo, 0).astype(jnp.int32)
    hi = jnp.where(valid, hi, 0).astype(jnp.int32)
    used = counts > 0
    ids = jnp.arange(N_EXPERTS, dtype=jnp.int32)
    slot = pick((jnp.cumsum(used) - 1) % 2).astype(jnp.int32)
    efirst = (valid & (w == pick(item_start))).astype(jnp.int32)
    later = jnp.where(used[None, :] & (ids[None, :] > ids[:, None]), ids[None, :], N_EXPERTS)
    next_used = jnp.min(later, axis=1)
    enext = pick(jnp.where(next_used < N_EXPERTS, next_used, -1)).astype(jnp.int32)
    return blk, e, lo, hi, slot, efirst, enext


def _gate_up_split_matrix():
    i = np.arange(V7X_MXU_DIM)
    src = np.where(i < V7X_MXU_DIM // 2, 2 * i, 2 * (i - V7X_MXU_DIM // 2) + 1)
    perm = np.zeros((V7X_MXU_DIM, V7X_MXU_DIM), np.float32)
    perm[src, i] = 1.0
    return jnp.asarray(perm, BF16)


def _experts(xg, counts, w_gate_up, bg, bu, w_down, bd):
    A = xg.shape[0]
    D = 2 * xg.shape[1]
    tm = min(EXPERT_ROWS, A)
    F = w_down.shape[1]
    items = _work_items(counts, A, tm)
    n_items = A // tm + N_EXPERTS - 1
    xs = pl.BlockSpec((tm, D // 2), lambda w, blk, e, *_: (blk[w], 0))
    bias = lambda c: pl.BlockSpec((1, 1, c), lambda w, blk, e, *_: (e[w], 0, 0))
    hbm = pl.BlockSpec(memory_space=pl.ANY)
    return pl.pallas_call(
        _experts_kernel,
        grid_spec=pltpu.PrefetchScalarGridSpec(
            num_scalar_prefetch=len(items),
            grid=(n_items,),
            in_specs=[xs, hbm, bias(F), bias(F), hbm, bias(D), pl.BlockSpec(memory_space=pltpu.VMEM)],
            out_specs=xs,
            scratch_shapes=[pltpu.VMEM((2, D, 2 * F), F32), pltpu.VMEM((2, F, D), F32),
                            pltpu.VMEM((D, F), BF16), pltpu.VMEM((D, F), BF16),
                            pltpu.VMEM((F, D), BF16), pltpu.SemaphoreType.DMA((2, 2))],
        ),
        out_shape=jax.ShapeDtypeStruct(xg.shape, jnp.uint32),
        compiler_params=_cparams("arbitrary"),
        name="experts",
    )(*items, xg, w_gate_up, bg, bu, w_down, bd, _gate_up_split_matrix())


def _combine_kernel(x1_ref, yk_ref, w_ref, nw_ref, *rest):
    o_ref = rest[-1]
    w = w_ref[...]
    x = x1_ref[...]
    for k in range(TOP_K):
        x = x + jnp.concatenate(_unpack_bf16_pair(yk_ref[k]), axis=1) * w[:, k:k + 1]
    o_ref[...] = x * lax.rsqrt(jnp.mean(x * x, axis=-1, keepdims=True) + RMS_EPS) * nw_ref[...]


def _combine(x1, yk, w_tk, norm_w, out_rows, row0, prev_out):
    Tg, D = x1.shape
    tm = min(COMBINE_ROWS, Tg)
    row = pl.BlockSpec((tm, D), lambda i: (i, 0))
    in_specs = [row, pl.BlockSpec((TOP_K, tm, D // 2), lambda i: (0, i, 0)),
                pl.BlockSpec((tm, TOP_K), lambda i: (i, 0)),
                pl.BlockSpec((1, D), lambda i: (0, 0))]
    args = [x1, yk, w_tk, norm_w]
    aliases = {}
    if prev_out is not None:
        in_specs.append(pl.BlockSpec(memory_space=pl.ANY))
        args.append(prev_out)
        aliases = {len(args) - 1: 0}
    return pl.pallas_call(
        _combine_kernel,
        grid=(Tg // tm,),
        in_specs=in_specs,
        out_specs=pl.BlockSpec((tm, D), lambda i: (i + row0 // tm, 0)),
        out_shape=jax.ShapeDtypeStruct((out_rows, D), F32),
        input_output_aliases=aliases,
        compiler_params=_cparams("parallel"),
        name="combine",
    )(*args)


def _qk_column_order():
    half = ATT_HEAD_DIM // 2
    order = []
    for p in range(ATT_HEADS // 2):
        for part in range(2):
            for h in (2 * p, 2 * p + 1):
                order.extend(range(h * ATT_HEAD_DIM + part * half, h * ATT_HEAD_DIM + (part + 1) * half))
    return np.asarray(order, np.int32)


def _rope_tables(seq):
    half = ATT_HEAD_DIM // 2
    inv = ROPE_THETA ** (-(jnp.arange(half, dtype=F32) * 2.0 / ATT_HEAD_DIM))
    ang = jnp.arange(seq, dtype=F32)[:, None] * inv[None, :]
    cos, sin = jnp.cos(ang), jnp.sin(ang)
    return (jnp.concatenate([cos, cos, cos, cos], axis=1),
            jnp.concatenate([-sin, -sin, sin, sin], axis=1))


def kernel(x, norm1_w, w_in, moba_up, hgrn_lb_logits, hgrn_norm_w, hgrn_up, w_out, norm2_w,
           router_w, router_b, w_gate_up, b_gate_up, w_down, b_down, final_norm_w):
    B, S, D = x.shape
    T = B * S
    assert S % MOBA_BLOCK == 0 and w_in.shape[0] == 1
    x2 = x.reshape(T, D)

    perm = _qk_column_order()
    w0 = w_in[0]
    w_in_p = jnp.concatenate([w0[:, :ATT_WIDTH][:, perm], w0[:, ATT_WIDTH:2 * ATT_WIDTH][:, perm],
                              w0[:, 2 * ATT_WIDTH:]], axis=1).astype(BF16)
    cos_t, sin_t = _rope_tables(S)
    lb = jnp.cumsum(jax.nn.softmax(hgrn_lb_logits.astype(F32), axis=0), axis=0)[0:1]

    qa, ka, va, qb, fb, ib, gb, ga, gtb = _in_proj(x2, norm1_w, w_in_p, cos_t, sin_t, S)
    r3 = lambda a: a.reshape(B, S, a.shape[1])
    ya = _moba(r3(qa), r3(ka), r3(va)).reshape(T, ATT_WIDTH)
    yb = _hgrn(r3(qb), r3(fb), r3(ib), r3(gb), lb, hgrn_norm_w).reshape(T, HGRN_WIDTH)

    rw_t = router_w[0].T
    rw_hi = rw_t.astype(BF16)
    rw_lo = (rw_t - rw_hi.astype(F32)).astype(BF16)
    mix_w = (moba_up[0].astype(BF16), hgrn_up[0].astype(BF16), w_out[0].astype(BF16), norm2_w,
             jnp.concatenate([rw_hi, rw_hi, rw_lo], axis=1), router_b[0][:, None])
    expert_w = (w_gate_up[0], b_gate_up[0][:, None, 0::2], b_gate_up[0][:, None, 1::2],
                w_down[0], b_down[0][:, None, :])

    Tg = T // MOE_TOKEN_GROUPS
    out = None
    for g in range(MOE_TOKEN_GROUPS):
        x1, h2, top_e, top_w, rank, cnt = _mix_route(ya, yb, ga, gtb, x2, *mix_w, g * Tg, Tg)
        counts = cnt[:, 0].astype(jnp.int32)
        start = jnp.cumsum(counts) - counts
        dest = rank + jnp.sum(jnp.where(top_e[:, :, None] == jnp.arange(N_EXPERTS)[None, None, :],
                                        start[None, None, :], 0), axis=-1)
        xg = _scatter_rows(dest, h2)
        yg = _experts(xg, counts, *expert_w)
        yk = _gather_rows(dest, yg).reshape(TOP_K, Tg, D // 2)
        out = _combine(x1, yk, top_w.T, final_norm_w[None, :], T, g * Tg, out)
    return out.reshape(B, S, D)
```

```python
import functools
import math

import numpy as np
import jax
import jax.numpy as jnp
from jax import lax
from jax.experimental import pallas as pl
from jax.experimental.pallas import tpu as pltpu
from jax.experimental.pallas import tpu_sc as plsc

ATT_HEADS = 8
ATT_HEAD_DIM = 64
ATT_WIDTH = ATT_HEADS * ATT_HEAD_DIM
MOBA_BLOCK = 256
MOBA_TOPK = 3
ROPE_THETA = 10000.0
HGRN_HEADS = 4
HGRN_DIM = 128
HGRN_WIDTH = HGRN_HEADS * HGRN_DIM
HGRN_CHUNK = 64
N_EXPERTS = 32
TOP_K = 4
SWIGLU_LIMIT = 7.0
SWIGLU_ALPHA = 1.702
RMS_EPS = 1e-6
NEG = -1e30

V7X_LANES = 128
V7X_SUBLANES = 8
V7X_MXU_DIM = 256
V7X_VMEM_LIMIT_BYTES = 56 * 1024 * 1024
V7X_SC_CORES = 2
V7X_SC_SUBCORES = 16

MOBA_PAIRS = 2
MOBA_GROUP = 4
PROJ_ROWS = 1024
PROJ_COLS = 512
HGRN_ROWS = 512
MIX_ROWS = 512
EXPERT_ROWS = 512
EXPERT_SUB_ROWS = 256
MOE_TOKEN_GROUPS = 2
SC_ROWS = 64
COMBINE_ROWS = 512

F32 = jnp.float32
BF16 = jnp.bfloat16


def _nt_dot(a, b, precision=None):
    return lax.dot_general(a, b, (((1,), (1,)), ((), ())), precision=precision,
                           preferred_element_type=F32)


def _tn_dot(a, b, precision=None):
    return lax.dot_general(a, b, (((0,), (0,)), ((), ())), precision=precision,
                           preferred_element_type=F32)


def _cparams(*sem):
    return pltpu.CompilerParams(dimension_semantics=sem, vmem_limit_bytes=V7X_VMEM_LIMIT_BYTES)


def _pack_bf16_pair(lo, hi):
    lo_bits = lax.bitcast_convert_type(lo.astype(BF16).astype(F32), jnp.uint32)
    hi_bits = lax.bitcast_convert_type(hi.astype(BF16).astype(F32), jnp.uint32)
    return (lo_bits >> 16) | hi_bits


def _unpack_bf16_pair(word):
    lo = lax.bitcast_convert_type(word << 16, F32)
    hi = lax.bitcast_convert_type(word & jnp.uint32(0xFFFF0000), F32)
    return lo, hi


def _in_proj_kernel(x_ref, nw_ref, w_ref, cos_ref, sin_ref,
                    qa_ref, ka_ref, va_ref, qb_ref, fb_ref, ib_ref, gb_ref, ga_ref, gtb_ref):
    x = x_ref[...]
    h = x * lax.rsqrt(jnp.mean(x * x, axis=-1, keepdims=True) + RMS_EPS) * nw_ref[...]
    h = h.astype(BF16)
    cos = cos_ref[...]
    sin = sin_ref[...]

    def proj(c):
        return jnp.dot(h, w_ref[:, c * PROJ_COLS:(c + 1) * PROJ_COLS], preferred_element_type=F32)

    def rope(t):
        out = []
        for j in range(PROJ_COLS // V7X_LANES):
            tj = t[:, j * V7X_LANES:(j + 1) * V7X_LANES]
            out.append(tj * cos + pltpu.roll(tj, V7X_LANES // 2, 1) * sin)
        return jnp.concatenate(out, axis=1)

    qa_ref[...] = (rope(proj(0)) * (ATT_HEAD_DIM ** -0.5)).astype(BF16)
    ka_ref[...] = rope(proj(1)).astype(BF16)
    va_ref[...] = proj(2).astype(BF16)
    qb_ref[...] = proj(3).astype(BF16)
    fb_ref[...] = proj(4)
    ib_ref[...] = proj(5).astype(BF16)
    gb_ref[...] = proj(6).astype(BF16)
    ga_ref[:, :PROJ_COLS] = proj(7).astype(BF16)
    ga_ref[:, PROJ_COLS:] = proj(8).astype(BF16)
    gtb_ref[:, :PROJ_COLS] = proj(9).astype(BF16)
    gtb_ref[:, PROJ_COLS:] = proj(10).astype(BF16)


def _in_proj(x2, norm_w, w_in_bf16, cos_t, sin_t, seq):
    T, D = x2.shape
    tm = min(PROJ_ROWS, seq)
    n_seq_tiles = seq // tm
    row = lambda w: pl.BlockSpec((tm, w), lambda i: (i, 0))
    tab = pl.BlockSpec((tm, V7X_LANES), lambda i: (i % n_seq_tiles, 0))
    widths = [ATT_WIDTH] * 3 + [HGRN_WIDTH] * 4 + [D, D]
    dtypes = [BF16, BF16, BF16, BF16, F32, BF16, BF16, BF16, BF16]
    return pl.pallas_call(
        _in_proj_kernel,
        grid=(T // tm,),
        in_specs=[row(D), pl.BlockSpec((1, D), lambda i: (0, 0)),
                  pl.BlockSpec(memory_space=pltpu.VMEM), tab, tab],
        out_specs=[row(w) for w in widths],
        out_shape=[jax.ShapeDtypeStruct((T, w), dt) for w, dt in zip(widths, dtypes)],
        compiler_params=_cparams("parallel"),
        name="in_proj",
    )(x2, norm_w, w_in_bf16, cos_t, sin_t)


def _moba_kernel(q_ref, k_ref, v_ref, o_ref, kaug_ref, kmean_ref, vt_ref, acc_ref, *s_refs, nb):
    qi = pl.program_id(2)
    blk = MOBA_BLOCK
    lanes = V7X_LANES
    n_pairs = kaug_ref.shape[0]
    heads = [(pp, hh) for pp in range(n_pairs) for hh in range(2)]
    nbp = kmean_ref.shape[1]

    @pl.when(qi == 0)
    def _():
        rowb = lax.broadcasted_iota(jnp.int32, (nb * blk, lanes), 0) // blk
        col = lax.broadcasted_iota(jnp.int32, (nb * blk, lanes), 1)
        onehot = jnp.where(rowb == col, 1.0, 0.0).astype(BF16)
        for pp in range(n_pairs):
            pl_ = slice(pp * lanes, (pp + 1) * lanes)
            kaug_ref[pp, :, :lanes] = k_ref[0, :, pl_]
            kaug_ref[pp, :, lanes:] = onehot
            means = [jnp.sum(k_ref[0, n * blk:(n + 1) * blk, pl_].astype(F32), axis=0,
                             keepdims=True) * (1.0 / blk) for n in range(nb)]
            km = jnp.concatenate(means + [jnp.zeros((nbp - nb, lanes), F32)] * (nbp > nb), axis=0)
            hi = km.astype(BF16)
            rem = km - hi.astype(F32)
            mid = rem.astype(BF16)
            lo = (rem - mid.astype(F32)).astype(BF16)
            kmean_ref[pp] = jnp.concatenate([hi, mid, lo], axis=1)
            for c in range(nb):
                vt = v_ref[0, c * blk:(c + 1) * blk, pl_].astype(F32).T
                vt_ref[pp, :lanes, c * blk:(c + 1) * blk] = vt.astype(BF16)
            vt_ref[pp, lanes:, :] = jnp.ones((vt_ref.shape[1] - lanes, nb * blk), BF16)

    feat = lax.broadcasted_iota(jnp.int32, (lanes, blk), 0)
    key_i = lax.broadcasted_iota(jnp.int32, (blk, blk), 0)
    qry_i = lax.broadcasted_iota(jnp.int32, (blk, blk), 1)
    own = pl.multiple_of(qi * blk, blk)
    blk_id = lax.broadcasted_iota(jnp.int32, (nbp, 2 * blk), 0)
    slab = 2 * V7X_SUBLANES

    def slab_max(s):
        return jnp.max(s.reshape(s.shape[0] // slab, slab, blk), axis=0)

    q_aug, m_init = [], []
    for pp in range(n_pairs):
        qt = q_ref[0, :, pp * lanes:(pp + 1) * lanes].astype(F32).T
        k_own = kaug_ref[pp, pl.ds(own, blk), :lanes]
        qhs = [jnp.where((feat // (ATT_HEAD_DIM // 2)) % 2 == hh, qt, 0.0).astype(BF16)
               for hh in range(2)]
        q2 = jnp.concatenate(qhs, axis=1)
        gate = jnp.dot(kmean_ref[pp], jnp.concatenate([q2, q2, q2], axis=0),
                       preferred_element_type=F32)
        gate = jnp.where(blk_id < qi, gate, NEG)
        beaten = jnp.zeros((nbp, 2 * blk), F32)
        for n in range(nb):
            gn = gate[n:n + 1, :]
            wins = (gn > gate) | ((gn == gate) & (blk_id > n))
            beaten = beaten + jnp.where(wins, 1.0, 0.0)
        sel = (beaten < MOBA_TOPK) & (blk_id < qi)
        bias = jnp.where(sel, 0.0, NEG).astype(BF16)
        bias = jnp.concatenate([bias, jnp.zeros((lanes - nbp, 2 * blk), BF16)], axis=0)
        for hh in range(2):
            h = 2 * pp + hh
            q_aug.append(jnp.concatenate([qhs[hh], bias[:, hh * blk:(hh + 1) * blk]], axis=0))
            s = jnp.dot(k_own, qhs[hh], preferred_element_type=F32)
            s = jnp.where(key_i <= qry_i, s, NEG)
            m0 = jnp.max(slab_max(s), axis=0, keepdims=True)
            p = jnp.exp(s - m0).astype(BF16)
            acc_ref[h] = jnp.dot(vt_ref[pp, :, pl.ds(own, blk)], p, preferred_element_type=F32)
            m_init.append(m0)

    group = MOBA_GROUP * blk
    n_groups = (qi + MOBA_GROUP) // MOBA_GROUP

    def score_head(g, h):
        off = pl.multiple_of(g * group, group)
        s = jnp.dot(kaug_ref[h // 2, pl.ds(off, group), :], q_aug[h], preferred_element_type=F32)
        s_refs[h][pl.ds(off, group), :] = s
        return jnp.max(slab_max(s), axis=0, keepdims=True)

    def value_head(g, h, gmax, m_old):
        off = pl.multiple_of(g * group, group)
        m_new = jnp.maximum(m_old, gmax)
        alpha = jnp.exp(m_old - m_new)
        p = jnp.exp(s_refs[h][pl.ds(off, group), :] - m_new).astype(BF16)
        acc_ref[h] = alpha * acc_ref[h] + jnp.dot(vt_ref[h // 2, :, pl.ds(off, group)], p,
                                                  preferred_element_type=F32)
        return m_new

    def pipelined(g, carry):
        gmax, ms = carry
        last = len(heads) - 1
        new_gmax, new_ms = [None] * (last + 1), [None] * (last + 1)
        new_gmax[last] = score_head(g + 1, last)
        for h in range(last + 1):
            new_ms[h] = value_head(g, h, gmax[h], ms[h])
            if h < last:
                new_gmax[h] = score_head(g + 1, h)
        return tuple(new_gmax), tuple(new_ms)

    first = tuple(score_head(0, h) for h in range(len(heads)))
    gmax, ms = lax.fori_loop(0, n_groups - 1, pipelined, (first, tuple(m_init)))
    for h in range(len(heads)):
        value_head(n_groups - 1, h, gmax[h], ms[h])

    half = lanes // 2
    outs = []
    for pp in range(n_pairs):
        outs.append(acc_ref[2 * pp, :half, :] / acc_ref[2 * pp, lanes:lanes + 1, :])
        outs.append(acc_ref[2 * pp + 1, half:lanes, :] / acc_ref[2 * pp + 1, lanes:lanes + 1, :])
    o_ref[0] = jnp.concatenate(outs, axis=0).T.astype(BF16)


def _moba(q, k, v):
    B, S, _ = q.shape
    nb = S // MOBA_BLOCK
    assert nb % MOBA_GROUP == 0
    nbp = -(-nb // (2 * V7X_SUBLANES)) * (2 * V7X_SUBLANES)
    n_pairs = ATT_WIDTH // V7X_LANES
    pp = MOBA_PAIRS
    assert n_pairs % pp == 0
    vt_rows = V7X_LANES + 2 * V7X_SUBLANES
    qspec = pl.BlockSpec((1, MOBA_BLOCK, pp * V7X_LANES), lambda b, p, i: (b, i, p))
    kvspec = pl.BlockSpec((1, S, pp * V7X_LANES), lambda b, p, i: (b, 0, p))
    return pl.pallas_call(
        functools.partial(_moba_kernel, nb=nb),
        grid=(B, n_pairs // pp, nb),
        in_specs=[qspec, kvspec, kvspec],
        out_specs=qspec,
        out_shape=jax.ShapeDtypeStruct((B, S, ATT_WIDTH), BF16),
        scratch_shapes=[pltpu.VMEM((pp, S, 2 * V7X_LANES), BF16),
                        pltpu.VMEM((pp, nbp, 3 * V7X_LANES), BF16),
                        pltpu.VMEM((pp, vt_rows, S), BF16),
                        pltpu.VMEM((2 * pp, vt_rows, MOBA_BLOCK), F32)]
                       + [pltpu.VMEM((S, MOBA_BLOCK), F32)] * (2 * pp),
        compiler_params=_cparams("parallel", "parallel", "arbitrary"),
        name="moba",
    )(q, k, v)


def _hgrn_level_sizes(chunk):
    return [chunk >> (i + 1) for i in range(int(math.log2(chunk)))]


def _hgrn_constants(chunk):
    t = np.arange(chunk)
    mats = [(t[None, :] <= t[:, None]),
            (t[None, :] > t[:, None])]
    qrows, pmasks = [], []
    for bs in _hgrn_level_sizes(chunk):
        blk = t // bs
        odd = (blk % 2) == 1
        lo, hi = blk * bs, (blk + 1) * bs
        u = t[None, :]
        m_odd = (u >= lo[:, None]) & (u <= t[:, None])
        m_even = (u > t[:, None]) & (u < hi[:, None])
        mats.append(np.where(odd[:, None], m_odd, m_even))
        qrows.append(odd)
        pmasks.append(odd[:, None] & (blk[None, :] == blk[:, None] - 1))
    pmasks.append(t[None, :] == t[:, None])
    summat = np.concatenate(mats, axis=0).astype(np.float32)
    qrow = np.stack(qrows, axis=0).astype(np.float32)
    pmask = np.stack(pmasks, axis=0).astype(np.float32)
    return summat, qrow, pmask


def _hgrn_kernel(q_ref, f_ref, i_ref, g_ref, lb_ref, nw_ref, sm_ref, qrow_ref, pm_ref,
                 o_ref, state_ref, *, rows):
    C = HGRN_CHUNK
    n_levels = qrow_ref.shape[0]

    @pl.when(pl.program_id(1) == 0)
    def _():
        state_ref[...] = jnp.zeros_like(state_ref)

    lb = lb_ref[...]
    summat = sm_ref[...]
    for c in range(rows // C):
        rs = slice(c * C, (c + 1) * C)
        fg = lb + (1.0 - lb) * jax.nn.sigmoid(f_ref[0, rs, :])
        logf = jnp.log(fg)
        hi = logf.astype(BF16)
        rem = logf - hi.astype(F32)
        mid = rem.astype(BF16)
        lo = (rem - mid.astype(F32)).astype(BF16)
        sums = jnp.dot(summat, jnp.concatenate([hi, mid, lo], axis=0),
                       preferred_element_type=F32)
        for h in range(HGRN_HEADS):
            ls = slice(h * HGRN_DIM, (h + 1) * HGRN_DIM)
            qf = jax.nn.silu(q_ref[0, rs, ls].astype(F32))
            kf = 1.0 - fg[:, ls]
            iv = i_ref[0, rs, ls]
            bcum = sums[0:C, ls]
            bsuf = sums[C:2 * C, ls]
            att = _nt_dot(qf.astype(BF16), kf.astype(BF16)) * pm_ref[n_levels]
            for lv in range(n_levels):
                w = jnp.exp(sums[(2 + lv) * C:(3 + lv) * C, ls])
                qrow = qrow_ref[lv]
                z = (jnp.where(qrow > 0.5, qf, kf) * w).astype(BF16)
                att = att + _nt_dot(z, z) * pm_ref[lv]
            o = jnp.dot(att.astype(BF16), iv, preferred_element_type=F32)
            st = state_ref[h]
            o = o + _nt_dot((qf * jnp.exp(bcum)).astype(BF16), st.astype(BF16))
            kdec = (kf * jnp.exp(bsuf)).astype(BF16)
            state_ref[h] = st * jnp.exp(bcum[C - 1:C, :]) + _tn_dot(iv, kdec)
            o = o * lax.rsqrt(jnp.mean(o * o, axis=-1, keepdims=True) + RMS_EPS)
            o = o * nw_ref[:, ls] * jax.nn.silu(g_ref[0, rs, ls].astype(F32))
            o_ref[0, rs, ls] = o.astype(BF16)


def _hgrn(qb, fb, ib, gb, lb, norm_w):
    B, S, W = qb.shape
    rows = min(HGRN_ROWS, S)
    summat, qrow, pmask = _hgrn_constants(HGRN_CHUNK)
    summat = jnp.asarray(np.concatenate([summat] * 3, axis=1), BF16)
    n_levels = qrow.shape[0]
    blk = pl.BlockSpec((1, rows, W), lambda b, s: (b, s, 0))
    vec = pl.BlockSpec((1, W), lambda b, s: (0, 0))
    const = lambda a: pl.BlockSpec(a.shape, lambda b, s: (0,) * a.ndim)
    qrow3 = qrow.reshape(n_levels, HGRN_CHUNK, 1)
    return pl.pallas_call(
        functools.partial(_hgrn_kernel, rows=rows),
        grid=(B, S // rows),
        in_specs=[blk, blk, blk, blk, vec, vec, const(summat), const(qrow3), const(pmask)],
        out_specs=blk,
        out_shape=jax.ShapeDtypeStruct((B, S, W), BF16),
        scratch_shapes=[pltpu.VMEM((HGRN_HEADS, HGRN_DIM, HGRN_DIM), F32)],
        compiler_params=_cparams("parallel", "arbitrary"),
        name="hgrn",
    )(qb, fb, ib, gb, lb, norm_w, jnp.asarray(summat), jnp.asarray(qrow3), jnp.asarray(pmask))


def _mix_route_kernel(ya_ref, yb_ref, ga_ref, gb_ref, x_ref, wa_ref, wb_ref, wo_ref, nw_ref,
                      rw_ref, rb_ref, tri_ref,
                      x1_ref, h2_ref, e_ref, w_ref, rank_ref, cnt_ref, carry_ref):
    @pl.when(pl.program_id(0) == 0)
    def _():
        carry_ref[...] = jnp.zeros_like(carry_ref)

    ua = jnp.dot(ya_ref[...], wa_ref[...], preferred_element_type=F32)
    ub = jnp.dot(yb_ref[...], wb_ref[...], preferred_element_type=F32)
    mixed = (jax.nn.sigmoid(ga_ref[...].astype(F32)) * ua
             + jax.nn.sigmoid(gb_ref[...].astype(F32)) * ub)
    x1 = x_ref[...] + jnp.dot(mixed.astype(BF16), wo_ref[...], preferred_element_type=F32)
    x1_ref[...] = x1
    h2 = x1 * lax.rsqrt(jnp.mean(x1 * x1, axis=-1, keepdims=True) + RMS_EPS) * nw_ref[...]
    half = h2.shape[1] // 2
    h2_ref[...] = _pack_bf16_pair(h2[:, :half], h2[:, half:])

    tm = x1.shape[0]
    h2_hi = h2.astype(BF16)
    h2_lo = (h2 - h2_hi.astype(F32)).astype(BF16)
    logits = _nt_dot(rw_ref[...], jnp.concatenate([h2_hi, h2_lo, h2_hi], axis=1)) + rb_ref[...]
    eid = lax.broadcasted_iota(jnp.int32, (N_EXPERTS, tm), 0)
    work = logits
    es, vs = [], []
    for _ in range(TOP_K):
        mx = jnp.max(work, axis=0, keepdims=True)
        idx = jnp.min(jnp.where(work == mx, eid, N_EXPERTS), axis=0, keepdims=True)
        es.append(idx)
        vs.append(mx)
        work = jnp.where(eid == idx, -jnp.inf, work)
    ex = [jnp.exp(v - vs[0]) for v in vs]
    den = ex[0] + ex[1] + ex[2] + ex[3]
    multi = jnp.zeros((N_EXPERTS, tm), F32)
    for k in range(TOP_K):
        multi = multi + jnp.where(eid == es[k], 1.0, 0.0)
    before = jnp.dot(multi.astype(BF16), tri_ref[...], preferred_element_type=F32) + carry_ref[...]
    for k in range(TOP_K):
        e_ref[k:k + 1, :] = es[k]
        w_ref[k:k + 1, :] = ex[k] / den
        rank_ref[k:k + 1, :] = jnp.sum(jnp.where(eid == es[k], before, 0.0), axis=0,
                                       keepdims=True).astype(jnp.int32)
    carry_ref[...] = carry_ref[...] + jnp.sum(multi, axis=1, keepdims=True)
    cnt_ref[...] = jnp.broadcast_to(carry_ref[...], cnt_ref.shape)


def _mix_route(ya, yb, ga, gb, x2, wa, wb, wo, norm_w, rw_t, rb, row0, rows):
    D = x2.shape[1]
    T = rows
    tm = min(MIX_ROWS, T)
    row = lambda w: pl.BlockSpec((tm, w), lambda i: (i + row0 // tm, 0))
    out_row = pl.BlockSpec((tm, D), lambda i: (i, 0))
    whole = pl.BlockSpec(memory_space=pltpu.VMEM)
    kt = pl.BlockSpec((TOP_K, tm), lambda i: (0, i))
    tri = jnp.asarray(np.triu(np.ones((tm, tm), np.float32), 1), BF16)
    return pl.pallas_call(
        _mix_route_kernel,
        grid=(T // tm,),
        in_specs=[row(ATT_WIDTH), row(HGRN_WIDTH), row(D), row(D), row(D),
                  whole, whole, whole, pl.BlockSpec((1, D), lambda i: (0, 0)),
                  whole, whole, whole],
        out_specs=[out_row, pl.BlockSpec((tm, D // 2), lambda i: (i, 0)), kt, kt, kt,
                   pl.BlockSpec((N_EXPERTS, V7X_LANES), lambda i: (0, 0))],
        out_shape=[jax.ShapeDtypeStruct((T, D), F32), jax.ShapeDtypeStruct((T, D // 2), jnp.uint32),
                   jax.ShapeDtypeStruct((TOP_K, T), jnp.int32),
                   jax.ShapeDtypeStruct((TOP_K, T), F32),
                   jax.ShapeDtypeStruct((TOP_K, T), jnp.int32),
                   jax.ShapeDtypeStruct((N_EXPERTS, V7X_LANES), F32)],
        scratch_shapes=[pltpu.VMEM((N_EXPERTS, 1), F32)],
        compiler_params=_cparams("arbitrary"),
        name="mix_route",
    )(ya, yb, ga, gb, x2, wa, wb, wo, norm_w, rw_t, rb, tri)


def _sc_worker_id():
    return lax.axis_index("s") * V7X_SC_CORES + lax.axis_index("c")


def _sc_kernel(body, out_rows, like, window, name):
    mesh = plsc.VectorSubcoreMesh(core_axis_name="c", subcore_axis_name="s")
    return pl.kernel(
        body, mesh=mesh,
        out_type=jax.ShapeDtypeStruct((out_rows,) + like.shape[1:], like.dtype),
        scratch_types=[pltpu.VMEM((window,), jnp.int32),
                       pltpu.VMEM((window,) + like.shape[1:], like.dtype)],
        name=name)


def _scatter_rows(dest, h2):
    T = dest.shape[1]
    n_workers = V7X_SC_CORES * V7X_SC_SUBCORES
    per_worker = T // n_workers
    window = min(SC_ROWS, per_worker)

    def body(dest_hbm, src_hbm, dst_hbm, idx_v, rows_v):
        base = _sc_worker_id() * per_worker

        @pl.loop(0, per_worker // window)
        def _(c):
            t0 = base + c * window
            pltpu.sync_copy(src_hbm.at[pl.ds(t0, window)], rows_v)
            for k in range(TOP_K):
                pltpu.sync_copy(dest_hbm.at[pl.ds(k * T + t0, window)], idx_v)
                pltpu.sync_copy(rows_v, dst_hbm.at[idx_v])

    return _sc_kernel(body, TOP_K * T, h2, window, "scatter_rows")(dest.reshape(-1), h2)


def _gather_rows(dest, yg):
    T = dest.shape[1]
    n_workers = V7X_SC_CORES * V7X_SC_SUBCORES
    per_worker = TOP_K * T // n_workers
    window = min(SC_ROWS, per_worker)

    def body(dest_hbm, src_hbm, dst_hbm, idx_v, rows_v):
        base = _sc_worker_id() * per_worker

        @pl.loop(0, per_worker // window)
        def _(c):
            r0 = base + c * window
            pltpu.sync_copy(dest_hbm.at[pl.ds(r0, window)], idx_v)
            pltpu.sync_copy(src_hbm.at[idx_v], rows_v)
            pltpu.sync_copy(rows_v, dst_hbm.at[pl.ds(r0, window)])

    return _sc_kernel(body, TOP_K * T, yg, window, "gather_rows")(dest.reshape(-1), yg)


def _experts_kernel(blk_ref, exp_ref, lo_ref, hi_ref, slot_ref, efirst_ref, enext_ref,
                    x_ref, wgu_hbm, bg_ref, bu_ref, wd_hbm, bd_ref, perm_ref, y_ref,
                    wgu_stage, wd_stage, wg_buf, wu_buf, wd_buf, sems):
    w = pl.program_id(0)
    lo = lo_ref[w]
    hi = hi_ref[w]
    slot = slot_ref[w]

    tm = x_ref.shape[0]
    n_out = V7X_MXU_DIM

    def weight_copies(e, s):
        return [pltpu.make_async_copy(wgu_hbm.at[e], wgu_stage.at[s], sems.at[s, 0]),
                pltpu.make_async_copy(wd_hbm.at[e], wd_stage.at[s], sems.at[s, 1])]

    @pl.when(w == 0)
    def _():
        for c in weight_copies(exp_ref[0], slot):
            c.start()

    @pl.when(efirst_ref[w] == 1)
    def _():
        for c in weight_copies(exp_ref[w], slot):
            c.wait()

        @pl.when(enext_ref[w] >= 0)
        def _():
            for c in weight_copies(enext_ref[w], 1 - slot):
                c.start()

        half = V7X_MXU_DIM // 2
        for g in range(wgu_stage.shape[2] // V7X_MXU_DIM):
            wb = wgu_stage[slot, :, g * V7X_MXU_DIM:(g + 1) * V7X_MXU_DIM].astype(BF16)
            d = jnp.dot(wb, perm_ref[...], preferred_element_type=F32).astype(BF16)
            wg_buf[:, g * half:(g + 1) * half] = d[:, :half]
            wu_buf[:, g * half:(g + 1) * half] = d[:, half:]
        wd_buf[...] = wd_stage[slot].astype(BF16)

    sb = EXPERT_SUB_ROWS

    def ffn(r0, keep_below):
        rs = slice(r0, r0 + sb)
        x = jnp.concatenate(_unpack_bf16_pair(x_ref[rs, :]), axis=1).astype(BF16)
        g = jnp.dot(x, wg_buf[...], preferred_element_type=F32) + bg_ref[0]
        u = jnp.dot(x, wu_buf[...], preferred_element_type=F32) + bu_ref[0]
        g = jnp.minimum(g, SWIGLU_LIMIT)
        u = jnp.clip(u, -SWIGLU_LIMIT, SWIGLU_LIMIT)
        act = ((u + 1.0) * (g * jax.nn.sigmoid(g * SWIGLU_ALPHA))).astype(BF16)
        half = y_ref.shape[1]

        def down(c0):
            cs = slice(c0, c0 + n_out)
            return (jnp.dot(act, wd_buf[:, cs], preferred_element_type=F32)
                    + bd_ref[0, :, cs])

        for c in range(half // n_out):
            cs = slice(c * n_out, (c + 1) * n_out)
            word = _pack_bf16_pair(down(c * n_out), down(half + c * n_out))
            if keep_below is not None:
                r = lax.broadcasted_iota(jnp.int32, (sb, n_out), 0)
                word = jnp.where(r >= keep_below, word, y_ref[rs, cs])
            y_ref[rs, cs] = word

    n_sub = tm // sb
    fused = (lo == 0) & (hi > (n_sub - 1) * sb)

    @pl.when(fused)
    def _():
        for j in range(n_sub):
            ffn(j * sb, None)

    for j in range(n_sub):
        lo_j = lo - j * sb
        touched = jnp.logical_not(fused) & (lo < hi) & (lo_j < sb) & (hi > j * sb)

        @pl.when(touched & (lo_j <= 0))
        def _():
            ffn(j * sb, None)

        @pl.when(touched & (lo_j > 0))
        def _():
            ffn(j * sb, lo_j)


def _work_items(counts, n_rows, tm):
    nblk = n_rows // tm
    n_items = nblk + N_EXPERTS - 1
    end = jnp.cumsum(counts)
    start = end - counts
    fb = start // tm
    nitems = jnp.where(counts > 0, (end - 1) // tm - fb + 1, 0)
    item_end = jnp.cumsum(nitems)
    item_start = item_end - nitems
    w = jnp.arange(n_items, dtype=jnp.int32)
    valid = w < item_end[-1]
    wc = jnp.minimum(w, item_end[-1] - 1)
    e = jnp.sum(wc[:, None] >= item_end[None, :], axis=1).astype(jnp.int32)
    e = jnp.minimum(e, N_EXPERTS - 1)
    onehot = e[:, None] == jnp.arange(N_EXPERTS, dtype=jnp.int32)[None, :]
    pick = lambda table: jnp.sum(jnp.where(onehot, table[None, :], 0), axis=1)
    blk = (pick(fb) + (wc - pick(item_start))).astype(jnp.int32)
    lo = jnp.maximum(pick(start), blk * tm) - blk * tm
    hi = jnp.minimum(pick(end), (blk + 1) * tm) - blk * tm
    lo = jnp.where(valid, lo, 0).astype(jnp.int32)
    hi = jnp.where(valid, hi, 0).astype(jnp.int32)
    used = counts > 0
    ids = jnp.arange(N_EXPERTS, dtype=jnp.int32)
    slot = pick((jnp.cumsum(used) - 1) % 2).astype(jnp.int32)
    efirst = (valid & (w == pick(item_start))).astype(jnp.int32)
    later = jnp.where(used[None, :] & (ids[None, :] > ids[:, None]), ids[None, :], N_EXPERTS)
    next_used = jnp.min(later, axis=1)
    enext = pick(jnp.where(next_used < N_EXPERTS, next_used, -1)).astype(jnp.int32)
    return blk, e, lo, hi, slot, efirst, enext


def _gate_up_split_matrix():
    i = np.arange(V7X_MXU_DIM)
    src = np.where(i < V7X_MXU_DIM // 2, 2 * i, 2 * (i - V7X_MXU_DIM // 2) + 1)
    perm = np.zeros((V7X_MXU_DIM, V7X_MXU_DIM), np.float32)
    perm[src, i] = 1.0
    return jnp.asarray(perm, BF16)


def _experts(xg, counts, w_gate_up, bg, bu, w_down, bd):
    A = xg.shape[0]
    D = 2 * xg.shape[1]
    tm = min(EXPERT_ROWS, A)
    F = w_down.shape[1]
    items = _work_items(counts, A, tm)
    n_items = A // tm + N_EXPERTS - 1
    xs = pl.BlockSpec((tm, D // 2), lambda w, blk, e, *_: (blk[w], 0))
    bias = lambda c: pl.BlockSpec((1, 1, c), lambda w, blk, e, *_: (e[w], 0, 0))
    hbm = pl.BlockSpec(memory_space=pl.ANY)
    return pl.pallas_call(
        _experts_kernel,
        grid_spec=pltpu.PrefetchScalarGridSpec(
            num_scalar_prefetch=len(items),
            grid=(n_items,),
            in_specs=[xs, hbm, bias(F), bias(F), hbm, bias(D), pl.BlockSpec(memory_space=pltpu.VMEM)],
            out_specs=xs,
            scratch_shapes=[pltpu.VMEM((2, D, 2 * F), F32), pltpu.VMEM((2, F, D), F32),
                            pltpu.VMEM((D, F), BF16), pltpu.VMEM((D, F), BF16),
                            pltpu.VMEM((F, D), BF16), pltpu.SemaphoreType.DMA((2, 2))],
        ),
        out_shape=jax.ShapeDtypeStruct(xg.shape, jnp.uint32),
        compiler_params=_cparams("arbitrary"),
        name="experts",
    )(*items, xg, w_gate_up, bg, bu, w_down, bd, _gate_up_split_matrix())


def _combine_kernel(x1_ref, yk_ref, w_ref, nw_ref, *rest):
    o_ref = rest[-1]
    w = w_ref[...]
    x = x1_ref[...]
    for k in range(TOP_K):
        x = x + jnp.concatenate(_unpack_bf16_pair(yk_ref[k]), axis=1) * w[:, k:k + 1]
    o_ref[...] = x * lax.rsqrt(jnp.mean(x * x, axis=-1, keepdims=True) + RMS_EPS) * nw_ref[...]


def _combine(x1, yk, w_tk, norm_w, out_rows, row0, prev_out):
    Tg, D = x1.shape
    tm = min(COMBINE_ROWS, Tg)
    row = pl.BlockSpec((tm, D), lambda i: (i, 0))
    in_specs = [row, pl.BlockSpec((TOP_K, tm, D // 2), lambda i: (0, i, 0)),
                pl.BlockSpec((tm, TOP_K), lambda i: (i, 0)),
                pl.BlockSpec((1, D), lambda i: (0, 0))]
    args = [x1, yk, w_tk, norm_w]
    aliases = {}
    if prev_out is not None:
        in_specs.append(pl.BlockSpec(memory_space=pl.ANY))
        args.append(prev_out)
        aliases = {len(args) - 1: 0}
    return pl.pallas_call(
        _combine_kernel,
        grid=(Tg // tm,),
        in_specs=in_specs,
        out_specs=pl.BlockSpec((tm, D), lambda i: (i + row0 // tm, 0)),
        out_shape=jax.ShapeDtypeStruct((out_rows, D), F32),
        input_output_aliases=aliases,
        compiler_params=_cparams("parallel"),
        name="combine",
    )(*args)


def _qk_column_order():
    half = ATT_HEAD_DIM // 2
    order = []
    for p in range(ATT_HEADS // 2):
        for part in range(2):
            for h in (2 * p, 2 * p + 1):
                order.extend(range(h * ATT_HEAD_DIM + part * half, h * ATT_HEAD_DIM + (part + 1) * half))
    return np.asarray(order, np.int32)


def _rope_tables(seq):
    half = ATT_HEAD_DIM // 2
    inv = ROPE_THETA ** (-(jnp.arange(half, dtype=F32) * 2.0 / ATT_HEAD_DIM))
    ang = jnp.arange(seq, dtype=F32)[:, None] * inv[None, :]
    cos, sin = jnp.cos(ang), jnp.sin(ang)
    return (jnp.concatenate([cos, cos, cos, cos], axis=1),
            jnp.concatenate([-sin, -sin, sin, sin], axis=1))


def kernel(x, norm1_w, w_in, moba_up, hgrn_lb_logits, hgrn_norm_w, hgrn_up, w_out, norm2_w,
           router_w, router_b, w_gate_up, b_gate_up, w_down, b_down, final_norm_w):
    B, S, D = x.shape
    T = B * S
    assert S % MOBA_BLOCK == 0 and w_in.shape[0] == 1
    x2 = x.reshape(T, D)

    perm = _qk_column_order()
    w0 = w_in[0]
    w_in_p = jnp.concatenate([w0[:, :ATT_WIDTH][:, perm], w0[:, ATT_WIDTH:2 * ATT_WIDTH][:, perm],
                              w0[:, 2 * ATT_WIDTH:]], axis=1).astype(BF16)
    cos_t, sin_t = _rope_tables(S)
    lb = jnp.cumsum(jax.nn.softmax(hgrn_lb_logits.astype(F32), axis=0), axis=0)[0:1]

    qa, ka, va, qb, fb, ib, gb, ga, gtb = _in_proj(x2, norm1_w, w_in_p, cos_t, sin_t, S)
    r3 = lambda a: a.reshape(B, S, a.shape[1])
    ya = _moba(r3(qa), r3(ka), r3(va)).reshape(T, ATT_WIDTH)
    yb = _hgrn(r3(qb), r3(fb), r3(ib), r3(gb), lb, hgrn_norm_w).reshape(T, HGRN_WIDTH)

    rw_t = router_w[0].T
    rw_hi = rw_t.astype(BF16)
    rw_lo = (rw_t - rw_hi.astype(F32)).astype(BF16)
    mix_w = (moba_up[0].astype(BF16), hgrn_up[0].astype(BF16), w_out[0].astype(BF16), norm2_w,
             jnp.concatenate([rw_hi, rw_hi, rw_lo], axis=1), router_b[0][:, None])
    expert_w = (w_gate_up[0], b_gate_up[0][:, None, 0::2], b_gate_up[0][:, None, 1::2],
                w_down[0], b_down[0][:, None, :])

    Tg = T // MOE_TOKEN_GROUPS
    out = None
    for g in range(MOE_TOKEN_GROUPS):
        x1, h2, top_e, top_w, rank, cnt = _mix_route(ya, yb, ga, gtb, x2, *mix_w, g * Tg, Tg)
        counts = cnt[:, 0].astype(jnp.int32)
        start = jnp.cumsum(counts) - counts
        dest = rank + jnp.sum(jnp.where(top_e[:, :, None] == jnp.arange(N_EXPERTS)[None, None, :],
                                        start[None, None, :], 0), axis=-1)
        xg = _scatter_rows(dest, h2)
        yg = _experts(xg, counts, *expert_w)
        yk = _gather_rows(dest, yg).reshape(TOP_K, Tg, D // 2)
        out = _combine(x1, yk, top_w.T, final_norm_w[None, :], T, g * Tg, out)
    return out.reshape(B, S, D)
```

```python
import functools
import math

import numpy as np
import jax
import jax.numpy as jnp
from jax import lax
from jax.experimental import pallas as pl
from jax.experimental.pallas import tpu as pltpu
from jax.experimental.pallas import tpu_sc as plsc

ATT_HEADS = 8
ATT_HEAD_DIM = 64
ATT_WIDTH = ATT_HEADS * ATT_HEAD_DIM
MOBA_BLOCK = 256
MOBA_TOPK = 3
ROPE_THETA = 10000.0
HGRN_HEADS = 4
HGRN_DIM = 128
HGRN_WIDTH = HGRN_HEADS * HGRN_DIM
HGRN_CHUNK = 64
N_EXPERTS = 32
TOP_K = 4
SWIGLU_LIMIT = 7.0
SWIGLU_ALPHA = 1.702
RMS_EPS = 1e-6
NEG = -1e30

V7X_LANES = 128
V7X_SUBLANES = 8
V7X_MXU_DIM = 256
V7X_VMEM_LIMIT_BYTES = 56 * 1024 * 1024
V7X_SC_CORES = 2
V7X_SC_SUBCORES = 16

MOBA_PAIRS = 2
MOBA_GROUP = 4
PROJ_ROWS = 1024
PROJ_COLS = 512
HGRN_ROWS = 512
MIX_ROWS = 512
EXPERT_ROWS = 512
EXPERT_SUB_ROWS = 256
MOE_TOKEN_GROUPS = 2
SC_ROWS = 64
COMBINE_ROWS = 512

F32 = jnp.float32
BF16 = jnp.bfloat16


def _nt_dot(a, b, precision=None):
    return lax.dot_general(a, b, (((1,), (1,)), ((), ())), precision=precision,
                           preferred_element_type=F32)


def _tn_dot(a, b, precision=None):
    return lax.dot_general(a, b, (((0,), (0,)), ((), ())), precision=precision,
                           preferred_element_type=F32)


def _cparams(*sem):
    return pltpu.CompilerParams(dimension_semantics=sem, vmem_limit_bytes=V7X_VMEM_LIMIT_BYTES)


def _pack_bf16_pair(lo, hi):
    lo_bits = lax.bitcast_convert_type(lo.astype(BF16).astype(F32), jnp.uint32)
    hi_bits = lax.bitcast_convert_type(hi.astype(BF16).astype(F32), jnp.uint32)
    return (lo_bits >> 16) | hi_bits


def _unpack_bf16_pair(word):
    lo = lax.bitcast_convert_type(word << 16, F32)
    hi = lax.bitcast_convert_type(word & jnp.uint32(0xFFFF0000), F32)
    return lo, hi


def _in_proj_kernel(x_ref, nw_ref, w_ref, cos_ref, sin_ref,
                    qa_ref, ka_ref, va_ref, qb_ref, fb_ref, ib_ref, gb_ref, ga_ref, gtb_ref):
    x = x_ref[...]
    h = x * lax.rsqrt(jnp.mean(x * x, axis=-1, keepdims=True) + RMS_EPS) * nw_ref[...]
    h = h.astype(BF16)
    cos = cos_ref[...]
    sin = sin_ref[...]

    def proj(c):
        return jnp.dot(h, w_ref[:, c * PROJ_COLS:(c + 1) * PROJ_COLS], preferred_element_type=F32)

    def rope(t):
        out = []
        for j in range(PROJ_COLS // V7X_LANES):
            tj = t[:, j * V7X_LANES:(j + 1) * V7X_LANES]
            out.append(tj * cos + pltpu.roll(tj, V7X_LANES // 2, 1) * sin)
        return jnp.concatenate(out, axis=1)

    qa_ref[...] = (rope(proj(0)) * (ATT_HEAD_DIM ** -0.5)).astype(BF16)
    ka_ref[...] = rope(proj(1)).astype(BF16)
    va_ref[...] = proj(2).astype(BF16)
    qb_ref[...] = proj(3).astype(BF16)
    fb_ref[...] = proj(4)
    ib_ref[...] = proj(5).astype(BF16)
    gb_ref[...] = proj(6).astype(BF16)
    ga_ref[:, :PROJ_COLS] = proj(7).astype(BF16)
    ga_ref[:, PROJ_COLS:] = proj(8).astype(BF16)
    gtb_ref[:, :PROJ_COLS] = proj(9).astype(BF16)
    gtb_ref[:, PROJ_COLS:] = proj(10).astype(BF16)


def _in_proj(x2, norm_w, w_in_bf16, cos_t, sin_t, seq):
    T, D = x2.shape
    tm = min(PROJ_ROWS, seq)
    n_seq_tiles = seq // tm
    row = lambda w: pl.BlockSpec((tm, w), lambda i: (i, 0))
    tab = pl.BlockSpec((tm, V7X_LANES), lambda i: (i % n_seq_tiles, 0))
    widths = [ATT_WIDTH] * 3 + [HGRN_WIDTH] * 4 + [D, D]
    dtypes = [BF16, BF16, BF16, BF16, F32, BF16, BF16, BF16, BF16]
    return pl.pallas_call(
        _in_proj_kernel,
        grid=(T // tm,),
        in_specs=[row(D), pl.BlockSpec((1, D), lambda i: (0, 0)),
                  pl.BlockSpec(memory_space=pltpu.VMEM), tab, tab],
        out_specs=[row(w) for w in widths],
        out_shape=[jax.ShapeDtypeStruct((T, w), dt) for w, dt in zip(widths, dtypes)],
        compiler_params=_cparams("parallel"),
        name="in_proj",
    )(x2, norm_w, w_in_bf16, cos_t, sin_t)


def _moba_kernel(q_ref, k_ref, v_ref, o_ref, kaug_ref, kmean_ref, vt_ref, acc_ref, *s_refs, nb):
    qi = pl.program_id(2)
    blk = MOBA_BLOCK
    lanes = V7X_LANES
    n_pairs = kaug_ref.shape[0]
    heads = [(pp, hh) for pp in range(n_pairs) for hh in range(2)]
    nbp = kmean_ref.shape[1]

    @pl.when(qi == 0)
    def _():
        rowb = lax.broadcasted_iota(jnp.int32, (nb * blk, lanes), 0) // blk
        col = lax.broadcasted_iota(jnp.int32, (nb * blk, lanes), 1)
        onehot = jnp.where(rowb == col, 1.0, 0.0).astype(BF16)
        for pp in range(n_pairs):
            pl_ = slice(pp * lanes, (pp + 1) * lanes)
            kaug_ref[pp, :, :lanes] = k_ref[0, :, pl_]
            kaug_ref[pp, :, lanes:] = onehot
            means = [jnp.sum(k_ref[0, n * blk:(n + 1) * blk, pl_].astype(F32), axis=0,
                             keepdims=True) * (1.0 / blk) for n in range(nb)]
            km = jnp.concatenate(means + [jnp.zeros((nbp - nb, lanes), F32)] * (nbp > nb), axis=0)
            hi = km.astype(BF16)
            rem = km - hi.astype(F32)
            mid = rem.astype(BF16)
            lo = (rem - mid.astype(F32)).astype(BF16)
            kmean_ref[pp] = jnp.concatenate([hi, mid, lo], axis=1)
            for c in range(nb):
                vt = v_ref[0, c * blk:(c + 1) * blk, pl_].astype(F32).T
                vt_ref[pp, :lanes, c * blk:(c + 1) * blk] = vt.astype(BF16)
            vt_ref[pp, lanes:, :] = jnp.ones((vt_ref.shape[1] - lanes, nb * blk), BF16)

    feat = lax.broadcasted_iota(jnp.int32, (lanes, blk), 0)
    key_i = lax.broadcasted_iota(jnp.int32, (blk, blk), 0)
    qry_i = lax.broadcasted_iota(jnp.int32, (blk, blk), 1)
    own = pl.multiple_of(qi * blk, blk)
    blk_id = lax.broadcasted_iota(jnp.int32, (nbp, 2 * blk), 0)
    slab = 2 * V7X_SUBLANES

    def slab_max(s):
        return jnp.max(s.reshape(s.shape[0] // slab, slab, blk), axis=0)

    q_aug, m_init = [], []
    for pp in range(n_pairs):
        qt = q_ref[0, :, pp * lanes:(pp + 1) * lanes].astype(F32).T
        k_own = kaug_ref[pp, pl.ds(own, blk), :lanes]
        qhs = [jnp.where((feat // (ATT_HEAD_DIM // 2)) % 2 == hh, qt, 0.0).astype(BF16)
               for hh in range(2)]
        q2 = jnp.concatenate(qhs, axis=1)
        gate = jnp.dot(kmean_ref[pp], jnp.concatenate([q2, q2, q2], axis=0),
                       preferred_element_type=F32)
        gate = jnp.where(blk_id < qi, gate, NEG)
        beaten = jnp.zeros((nbp, 2 * blk), F32)
        for n in range(nb):
            gn = gate[n:n + 1, :]
            wins = (gn > gate) | ((gn == gate) & (blk_id > n))
            beaten = beaten + jnp.where(wins, 1.0, 0.0)
        sel = (beaten < MOBA_TOPK) & (blk_id < qi)
        bias = jnp.where(sel, 0.0, NEG).astype(BF16)
        bias = jnp.concatenate([bias, jnp.zeros((lanes - nbp, 2 * blk), BF16)], axis=0)
        for hh in range(2):
            h = 2 * pp + hh
            q_aug.append(jnp.concatenate([qhs[hh], bias[:, hh * blk:(hh + 1) * blk]], axis=0))
            s = jnp.dot(k_own, qhs[hh], preferred_element_type=F32)
            s = jnp.where(key_i <= qry_i, s, NEG)
            m0 = jnp.max(slab_max(s), axis=0, keepdims=True)
            p = jnp.exp(s - m0).astype(BF16)
            acc_ref[h] = jnp.dot(vt_ref[pp, :, pl.ds(own, blk)], p, preferred_element_type=F32)
            m_init.append(m0)

    group = MOBA_GROUP * blk
    n_groups = (qi + MOBA_GROUP) // MOBA_GROUP

    def score_head(g, h):
        off = pl.multiple_of(g * group, group)
        s = jnp.dot(kaug_ref[h // 2, pl.ds(off, group), :], q_aug[h], preferred_element_type=F32)
        s_refs[h][pl.ds(off, group), :] = s
        return jnp.max(slab_max(s), axis=0, keepdims=True)

    def value_head(g, h, gmax, m_old):
        off = pl.multiple_of(g * group, group)
        m_new = jnp.maximum(m_old, gmax)
        alpha = jnp.exp(m_old - m_new)
        p = jnp.exp(s_refs[h][pl.ds(off, group), :] - m_new).astype(BF16)
        acc_ref[h] = alpha * acc_ref[h] + jnp.dot(vt_ref[h // 2, :, pl.ds(off, group)], p,
                                                  preferred_element_type=F32)
        return m_new

    def pipelined(g, carry):
        gmax, ms = carry
        last = len(heads) - 1
        new_gmax, new_ms = [None] * (last + 1), [None] * (last + 1)
        new_gmax[last] = score_head(g + 1, last)
        for h in range(last + 1):
            new_ms[h] = value_head(g, h, gmax[h], ms[h])
            if h < last:
                new_gmax[h] = score_head(g + 1, h)
        return tuple(new_gmax), tuple(new_ms)

    first = tuple(score_head(0, h) for h in range(len(heads)))
    gmax, ms = lax.fori_loop(0, n_groups - 1, pipelined, (first, tuple(m_init)))
    for h in range(len(heads)):
        value_head(n_groups - 1, h, gmax[h], ms[h])

    half = lanes // 2
    outs = []
    for pp in range(n_pairs):
        outs.append(acc_ref[2 * pp, :half, :] / acc_ref[2 * pp, lanes:lanes + 1, :])
        outs.append(acc_ref[2 * pp + 1, half:lanes, :] / acc_ref[2 * pp + 1, lanes:lanes + 1, :])
    o_ref[0] = jnp.concatenate(outs, axis=0).T.astype(BF16)


def _moba(q, k, v):
    B, S, _ = q.shape
    nb = S // MOBA_BLOCK
    assert nb % MOBA_GROUP == 0
    nbp = -(-nb // (2 * V7X_SUBLANES)) * (2 * V7X_SUBLANES)
    n_pairs = ATT_WIDTH // V7X_LANES
    pp = MOBA_PAIRS
    assert n_pairs % pp == 0
    vt_rows = V7X_LANES + 2 * V7X_SUBLANES
    qspec = pl.BlockSpec((1, MOBA_BLOCK, pp * V7X_LANES), lambda b, p, i: (b, i, p))
    kvspec = pl.BlockSpec((1, S, pp * V7X_LANES), lambda b, p, i: (b, 0, p))
    return pl.pallas_call(
        functools.partial(_moba_kernel, nb=nb),
        grid=(B, n_pairs // pp, nb),
        in_specs=[qspec, kvspec, kvspec],
        out_specs=qspec,
        out_shape=jax.ShapeDtypeStruct((B, S, ATT_WIDTH), BF16),
        scratch_shapes=[pltpu.VMEM((pp, S, 2 * V7X_LANES), BF16),
                        pltpu.VMEM((pp, nbp, 3 * V7X_LANES), BF16),
                        pltpu.VMEM((pp, vt_rows, S), BF16),
                        pltpu.VMEM((2 * pp, vt_rows, MOBA_BLOCK), F32)]
                       + [pltpu.VMEM((S, MOBA_BLOCK), F32)] * (2 * pp),
        compiler_params=_cparams("parallel", "parallel", "arbitrary"),
        name="moba",
    )(q, k, v)


def _hgrn_level_sizes(chunk):
    return [chunk >> (i + 1) for i in range(int(math.log2(chunk)))]


def _hgrn_constants(chunk):
    t = np.arange(chunk)
    mats = [(t[None, :] <= t[:, None]),
            (t[None, :] > t[:, None])]
    qrows, pmasks = [], []
    for bs in _hgrn_level_sizes(chunk):
        blk = t // bs
        odd = (blk % 2) == 1
        lo, hi = blk * bs, (blk + 1) * bs
        u = t[None, :]
        m_odd = (u >= lo[:, None]) & (u <= t[:, None])
        m_even = (u > t[:, None]) & (u < hi[:, None])
        mats.append(np.where(odd[:, None], m_odd, m_even))
        qrows.append(odd)
        pmasks.append(odd[:, None] & (blk[None, :] == blk[:, None] - 1))
    pmasks.append(t[None, :] == t[:, None])
    summat = np.concatenate(mats, axis=0).astype(np.float32)
    qrow = np.stack(qrows, axis=0).astype(np.float32)
    pmask = np.stack(pmasks, axis=0).astype(np.float32)
    return summat, qrow, pmask


def _hgrn_kernel(q_ref, f_ref, i_ref, g_ref, lb_ref, nw_ref, sm_ref, qrow_ref, pm_ref,
                 o_ref, state_ref, *, rows):
    C = HGRN_CHUNK
    n_levels = qrow_ref.shape[0]

    @pl.when(pl.program_id(1) == 0)
    def _():
        state_ref[...] = jnp.zeros_like(state_ref)

    lb = lb_ref[...]
    summat = sm_ref[...]
    for c in range(rows // C):
        rs = slice(c * C, (c + 1) * C)
        fg = lb + (1.0 - lb) * jax.nn.sigmoid(f_ref[0, rs, :])
        logf = jnp.log(fg)
        hi = logf.astype(BF16)
        rem = logf - hi.astype(F32)
        mid = rem.astype(BF16)
        lo = (rem - mid.astype(F32)).astype(BF16)
        sums = jnp.dot(summat, jnp.concatenate([hi, mid, lo], axis=0),
                       preferred_element_type=F32)
        for h in range(HGRN_HEADS):
            ls = slice(h * HGRN_DIM, (h + 1) * HGRN_DIM)
            qf = jax.nn.silu(q_ref[0, rs, ls].astype(F32))
            kf = 1.0 - fg[:, ls]
            iv = i_ref[0, rs, ls]
            bcum = sums[0:C, ls]
            bsuf = sums[C:2 * C, ls]
            att = _nt_dot(qf.astype(BF16), kf.astype(BF16)) * pm_ref[n_levels]
            for lv in range(n_levels):
                w = jnp.exp(sums[(2 + lv) * C:(3 + lv) * C, ls])
                qrow = qrow_ref[lv]
                z = (jnp.where(qrow > 0.5, qf, kf) * w).astype(BF16)
                att = att + _nt_dot(z, z) * pm_ref[lv]
            o = jnp.dot(att.astype(BF16), iv, preferred_element_type=F32)
            st = state_ref[h]
            o = o + _nt_dot((qf * jnp.exp(bcum)).astype(BF16), st.astype(BF16))
            kdec = (kf * jnp.exp(bsuf)).astype(BF16)
            state_ref[h] = st * jnp.exp(bcum[C - 1:C, :]) + _tn_dot(iv, kdec)
            o = o * lax.rsqrt(jnp.mean(o * o, axis=-1, keepdims=True) + RMS_EPS)
            o = o * nw_ref[:, ls] * jax.nn.silu(g_ref[0, rs, ls].astype(F32))
            o_ref[0, rs, ls] = o.astype(BF16)


def _hgrn(qb, fb, ib, gb, lb, norm_w):
    B, S, W = qb.shape
    rows = min(HGRN_ROWS, S)
    summat, qrow, pmask = _hgrn_constants(HGRN_CHUNK)
    summat = jnp.asarray(np.concatenate([summat] * 3, axis=1), BF16)
    n_levels = qrow.shape[0]
    blk = pl.BlockSpec((1, rows, W), lambda b, s: (b, s, 0))
    vec = pl.BlockSpec((1, W), lambda b, s: (0, 0))
    const = lambda a: pl.BlockSpec(a.shape, lambda b, s: (0,) * a.ndim)
    qrow3 = qrow.reshape(n_levels, HGRN_CHUNK, 1)
    return pl.pallas_call(
        functools.partial(_hgrn_kernel, rows=rows),
        grid=(B, S // rows),
        in_specs=[blk, blk, blk, blk, vec, vec, const(summat), const(qrow3), const(pmask)],
        out_specs=blk,
        out_shape=jax.ShapeDtypeStruct((B, S, W), BF16),
        scratch_shapes=[pltpu.VMEM((HGRN_HEADS, HGRN_DIM, HGRN_DIM), F32)],
        compiler_params=_cparams("parallel", "arbitrary"),
        name="hgrn",
    )(qb, fb, ib, gb, lb, norm_w, jnp.asarray(summat), jnp.asarray(qrow3), jnp.asarray(pmask))


def _mix_route_kernel(ya_ref, yb_ref, ga_ref, gb_ref, x_ref, wa_ref, wb_ref, wo_ref, nw_ref,
                      rw_ref, rb_ref, tri_ref,
                      x1_ref, h2_ref, e_ref, w_ref, rank_ref, cnt_ref, carry_ref):
    @pl.when(pl.program_id(0) == 0)
    def _():
        carry_ref[...] = jnp.zeros_like(carry_ref)

    ua = jnp.dot(ya_ref[...], wa_ref[...], preferred_element_type=F32)
    ub = jnp.dot(yb_ref[...], wb_ref[...], preferred_element_type=F32)
    mixed = (jax.nn.sigmoid(ga_ref[...].astype(F32)) * ua
             + jax.nn.sigmoid(gb_ref[...].astype(F32)) * ub)
    x1 = x_ref[...] + jnp.dot(mixed.astype(BF16), wo_ref[...], preferred_element_type=F32)
    x1_ref[...] = x1
    h2 = x1 * lax.rsqrt(jnp.mean(x1 * x1, axis=-1, keepdims=True) + RMS_EPS) * nw_ref[...]
    half = h2.shape[1] // 2
    h2_ref[...] = _pack_bf16_pair(h2[:, :half], h2[:, half:])

    tm = x1.shape[0]
    h2_hi = h2.astype(BF16)
    h2_lo = (h2 - h2_hi.astype(F32)).astype(BF16)
    logits = _nt_dot(rw_ref[...], jnp.concatenate([h2_hi, h2_lo, h2_hi], axis=1)) + rb_ref[...]
    eid = lax.broadcasted_iota(jnp.int32, (N_EXPERTS, tm), 0)
    work = logits
    es, vs = [], []
    for _ in range(TOP_K):
        mx = jnp.max(work, axis=0, keepdims=True)
        idx = jnp.min(jnp.where(work == mx, eid, N_EXPERTS), axis=0, keepdims=True)
        es.append(idx)
        vs.append(mx)
        work = jnp.where(eid == idx, -jnp.inf, work)
    ex = [jnp.exp(v - vs[0]) for v in vs]
    den = ex[0] + ex[1] + ex[2] + ex[3]
    multi = jnp.zeros((N_EXPERTS, tm), F32)
    for k in range(TOP_K):
        multi = multi + jnp.where(eid == es[k], 1.0, 0.0)
    before = jnp.dot(multi.astype(BF16), tri_ref[...], preferred_element_type=F32) + carry_ref[...]
    for k in range(TOP_K):
        e_ref[k:k + 1, :] = es[k]
        w_ref[k:k + 1, :] = ex[k] / den
        rank_ref[k:k + 1, :] = jnp.sum(jnp.where(eid == es[k], before, 0.0), axis=0,
                                       keepdims=True).astype(jnp.int32)
    carry_ref[...] = carry_ref[...] + jnp.sum(multi, axis=1, keepdims=True)
    cnt_ref[...] = jnp.broadcast_to(carry_ref[...], cnt_ref.shape)


def _mix_route(ya, yb, ga, gb, x2, wa, wb, wo, norm_w, rw_t, rb, row0, rows):
    D = x2.shape[1]
    T = rows
    tm = min(MIX_ROWS, T)
    row = lambda w: pl.BlockSpec((tm, w), lambda i: (i + row0 // tm, 0))
    out_row = pl.BlockSpec((tm, D), lambda i: (i, 0))
    whole = pl.BlockSpec(memory_space=pltpu.VMEM)
    kt = pl.BlockSpec((TOP_K, tm), lambda i: (0, i))
    tri = jnp.asarray(np.triu(np.ones((tm, tm), np.float32), 1), BF16)
    return pl.pallas_call(
        _mix_route_kernel,
        grid=(T // tm,),
        in_specs=[row(ATT_WIDTH), row(HGRN_WIDTH), row(D), row(D), row(D),
                  whole, whole, whole, pl.BlockSpec((1, D), lambda i: (0, 0)),
                  whole, whole, whole],
        out_specs=[out_row, pl.BlockSpec((tm, D // 2), lambda i: (i, 0)), kt, kt, kt,
                   pl.BlockSpec((N_EXPERTS, V7X_LANES), lambda i: (0, 0))],
        out_shape=[jax.ShapeDtypeStruct((T, D), F32), jax.ShapeDtypeStruct((T, D // 2), jnp.uint32),
                   jax.ShapeDtypeStruct((TOP_K, T), jnp.int32),
                   jax.ShapeDtypeStruct((TOP_K, T), F32),
                   jax.ShapeDtypeStruct((TOP_K, T), jnp.int32),
                   jax.ShapeDtypeStruct((N_EXPERTS, V7X_LANES), F32)],
        scratch_shapes=[pltpu.VMEM((N_EXPERTS, 1), F32)],
        compiler_params=_cparams("arbitrary"),
        name="mix_route",
    )(ya, yb, ga, gb, x2, wa, wb, wo, norm_w, rw_t, rb, tri)


def _sc_worker_id():
    return lax.axis_index("s") * V7X_SC_CORES + lax.axis_index("c")


def _sc_kernel(body, out_rows, like, window, name):
    mesh = plsc.VectorSubcoreMesh(core_axis_name="c", subcore_axis_name="s")
    return pl.kernel(
        body, mesh=mesh,
        out_type=jax.ShapeDtypeStruct((out_rows,) + like.shape[1:], like.dtype),
        scratch_types=[pltpu.VMEM((window,), jnp.int32),
                       pltpu.VMEM((window,) + like.shape[1:], like.dtype)],
        name=name)


def _scatter_rows(dest, h2, out_rows, pad_rows):
    T = dest.shape[1]
    n_workers = V7X_SC_CORES * V7X_SC_SUBCORES
    per_worker = T // n_workers
    window = min(SC_ROWS, per_worker)
    n_pad = pad_rows.shape[1]
    assert pad_rows.shape[0] == n_workers and n_pad % window == 0

    def body(dest_hbm, src_hbm, pad_hbm, zero_hbm, dst_hbm, idx_v, rows_v):
        wid = _sc_worker_id()
        base = wid * per_worker

        @pl.loop(0, per_worker // window)
        def _(c):
            t0 = base + c * window
            pltpu.sync_copy(src_hbm.at[pl.ds(t0, window)], rows_v)
            for k in range(TOP_K):
                pltpu.sync_copy(dest_hbm.at[pl.ds(k * T + t0, window)], idx_v)
                pltpu.sync_copy(rows_v, dst_hbm.at[idx_v])

        pltpu.sync_copy(zero_hbm, rows_v)
        for c in range(n_pad // window):
            pltpu.sync_copy(pad_hbm.at[pl.ds(wid * n_pad + c * window, window)], idx_v)
            pltpu.sync_copy(rows_v, dst_hbm.at[idx_v])

    zeros = jnp.zeros((window,) + h2.shape[1:], h2.dtype)
    return _sc_kernel(body, out_rows, h2, window, "scatter_rows")(
        dest.reshape(-1), h2, pad_rows.reshape(-1), zeros)


def _gather_rows(dest, yg):
    T = dest.shape[1]
    n_workers = V7X_SC_CORES * V7X_SC_SUBCORES
    per_worker = TOP_K * T // n_workers
    window = min(SC_ROWS, per_worker)

    def body(dest_hbm, src_hbm, dst_hbm, idx_v, rows_v):
        base = _sc_worker_id() * per_worker

        @pl.loop(0, per_worker // window)
        def _(c):
            r0 = base + c * window
            pltpu.sync_copy(dest_hbm.at[pl.ds(r0, window)], idx_v)
            pltpu.sync_copy(src_hbm.at[idx_v], rows_v)
            pltpu.sync_copy(rows_v, dst_hbm.at[pl.ds(r0, window)])

    return _sc_kernel(body, TOP_K * T, yg, window, "gather_rows")(dest.reshape(-1), yg)


def _experts_kernel(blk_ref, exp_ref, lo_ref, hi_ref, slot_ref, efirst_ref, enext_ref,
                    x_ref, wgu_hbm, bg_ref, bu_ref, wd_hbm, bd_ref, perm_ref, y_ref,
                    wgu_stage, wd_stage, wg_buf, wu_buf, wd_buf, sems):
    w = pl.program_id(0)
    lo = lo_ref[w]
    hi = hi_ref[w]
    slot = slot_ref[w]

    tm = x_ref.shape[0]
    n_out = V7X_MXU_DIM

    def weight_copies(e, s):
        return [pltpu.make_async_copy(wgu_hbm.at[e], wgu_stage.at[s], sems.at[s, 0]),
                pltpu.make_async_copy(wd_hbm.at[e], wd_stage.at[s], sems.at[s, 1])]

    @pl.when(w == 0)
    def _():
        for c in weight_copies(exp_ref[0], slot):
            c.start()

    @pl.when(efirst_ref[w] == 1)
    def _():
        for c in weight_copies(exp_ref[w], slot):
            c.wait()

        @pl.when(enext_ref[w] >= 0)
        def _():
            for c in weight_copies(enext_ref[w], 1 - slot):
                c.start()

        half = V7X_MXU_DIM // 2
        for g in range(wgu_stage.shape[2] // V7X_MXU_DIM):
            wb = wgu_stage[slot, :, g * V7X_MXU_DIM:(g + 1) * V7X_MXU_DIM].astype(BF16)
            d = jnp.dot(wb, perm_ref[...], preferred_element_type=F32).astype(BF16)
            wg_buf[:, g * half:(g + 1) * half] = d[:, :half]
            wu_buf[:, g * half:(g + 1) * half] = d[:, half:]
        wd_buf[...] = wd_stage[slot].astype(BF16)

    sb = EXPERT_SUB_ROWS

    def ffn(r0):
        rs = slice(r0, r0 + sb)
        x = jnp.concatenate(_unpack_bf16_pair(x_ref[rs, :]), axis=1).astype(BF16)
        g = jnp.dot(x, wg_buf[...], preferred_element_type=F32) + bg_ref[0]
        u = jnp.dot(x, wu_buf[...], preferred_element_type=F32) + bu_ref[0]
        g = jnp.minimum(g, SWIGLU_LIMIT)
        u = jnp.clip(u, -SWIGLU_LIMIT, SWIGLU_LIMIT)
        act = ((u + 1.0) * (g * jax.nn.sigmoid(g * SWIGLU_ALPHA))).astype(BF16)
        half = y_ref.shape[1]

        def down(c0):
            cs = slice(c0, c0 + n_out)
            return (jnp.dot(act, wd_buf[:, cs], preferred_element_type=F32)
                    + bd_ref[0, :, cs])

        for c in range(half // n_out):
            cs = slice(c * n_out, (c + 1) * n_out)
            y_ref[rs, cs] = _pack_bf16_pair(down(c * n_out), down(half + c * n_out))

    n_sub = tm // sb
    fused = (lo == 0) & (hi == tm)

    @pl.when(fused)
    def _():
        for j in range(n_sub):
            ffn(j * sb)

    for j in range(n_sub):
        @pl.when(jnp.logical_not(fused) & (lo <= j * sb) & (hi > j * sb))
        def _():
            ffn(j * sb)


def _work_items(counts, n_rows, tm):
    nblk = n_rows // tm
    n_items = nblk + N_EXPERTS - 1
    end = jnp.cumsum(counts)
    start = end - counts
    fb = start // tm
    nitems = jnp.where(counts > 0, (end - 1) // tm - fb + 1, 0)
    item_end = jnp.cumsum(nitems)
    item_start = item_end - nitems
    w = jnp.arange(n_items, dtype=jnp.int32)
    valid = w < item_end[-1]
    wc = jnp.minimum(w, item_end[-1] - 1)
    e = jnp.sum(wc[:, None] >= item_end[None, :], axis=1).astype(jnp.int32)
    e = jnp.minimum(e, N_EXPERTS - 1)
    onehot = e[:, None] == jnp.arange(N_EXPERTS, dtype=jnp.int32)[None, :]
    pick = lambda table: jnp.sum(jnp.where(onehot, table[None, :], 0), axis=1)
    blk = (pick(fb) + (wc - pick(item_start))).astype(jnp.int32)
    lo = jnp.maximum(pick(start), blk * tm) - blk * tm
    hi = jnp.minimum(pick(end), (blk + 1) * tm) - blk * tm
    lo = jnp.where(valid, lo, 0).astype(jnp.int32)
    hi = jnp.where(valid, hi, 0).astype(jnp.int32)
    used = counts > 0
    ids = jnp.arange(N_EXPERTS, dtype=jnp.int32)
    slot = pick((jnp.cumsum(used) - 1) % 2).astype(jnp.int32)
    efirst = (valid & (w == pick(item_start))).astype(jnp.int32)
    later = jnp.where(used[None, :] & (ids[None, :] > ids[:, None]), ids[None, :], N_EXPERTS)
    next_used = jnp.min(later, axis=1)
    enext = pick(jnp.where(next_used < N_EXPERTS, next_used, -1)).astype(jnp.int32)
    return blk, e, lo, hi, slot, efirst, enext


def _gate_up_split_matrix():
    i = np.arange(V7X_MXU_DIM)
    src = np.where(i < V7X_MXU_DIM // 2, 2 * i, 2 * (i - V7X_MXU_DIM // 2) + 1)
    perm = np.zeros((V7X_MXU_DIM, V7X_MXU_DIM), np.float32)
    perm[src, i] = 1.0
    return jnp.asarray(perm, BF16)


def _experts(xg, counts, w_gate_up, bg, bu, w_down, bd):
    A = xg.shape[0]
    D = 2 * xg.shape[1]
    tm = min(EXPERT_ROWS, A)
    F = w_down.shape[1]
    items = _work_items(counts, A, tm)
    n_items = A // tm + N_EXPERTS - 1
    xs = pl.BlockSpec((tm, D // 2), lambda w, blk, e, *_: (blk[w], 0))
    bias = lambda c: pl.BlockSpec((1, 1, c), lambda w, blk, e, *_: (e[w], 0, 0))
    hbm = pl.BlockSpec(memory_space=pl.ANY)
    return pl.pallas_call(
        _experts_kernel,
        grid_spec=pltpu.PrefetchScalarGridSpec(
            num_scalar_prefetch=len(items),
            grid=(n_items,),
            in_specs=[xs, hbm, bias(F), bias(F), hbm, bias(D), pl.BlockSpec(memory_space=pltpu.VMEM)],
            out_specs=xs,
            scratch_shapes=[pltpu.VMEM((2, D, 2 * F), F32), pltpu.VMEM((2, F, D), F32),
                            pltpu.VMEM((D, F), BF16), pltpu.VMEM((D, F), BF16),
                            pltpu.VMEM((F, D), BF16), pltpu.SemaphoreType.DMA((2, 2))],
        ),
        out_shape=jax.ShapeDtypeStruct(xg.shape, jnp.uint32),
        compiler_params=_cparams("arbitrary"),
        name="experts",
    )(*items, xg, w_gate_up, bg, bu, w_down, bd, _gate_up_split_matrix())


def _combine_kernel(x1_ref, yk_ref, w_ref, nw_ref, *rest):
    o_ref = rest[-1]
    w = w_ref[...]
    x = x1_ref[...]
    for k in range(TOP_K):
        x = x + jnp.concatenate(_unpack_bf16_pair(yk_ref[k]), axis=1) * w[:, k:k + 1]
    o_ref[...] = x * lax.rsqrt(jnp.mean(x * x, axis=-1, keepdims=True) + RMS_EPS) * nw_ref[...]


def _combine(x1, yk, w_tk, norm_w, out_rows, row0, prev_out):
    Tg, D = x1.shape
    tm = min(COMBINE_ROWS, Tg)
    row = pl.BlockSpec((tm, D), lambda i: (i, 0))
    in_specs = [row, pl.BlockSpec((TOP_K, tm, D // 2), lambda i: (0, i, 0)),
                pl.BlockSpec((tm, TOP_K), lambda i: (i, 0)),
                pl.BlockSpec((1, D), lambda i: (0, 0))]
    args = [x1, yk, w_tk, norm_w]
    aliases = {}
    if prev_out is not None:
        in_specs.append(pl.BlockSpec(memory_space=pl.ANY))
        args.append(prev_out)
        aliases = {len(args) - 1: 0}
    return pl.pallas_call(
        _combine_kernel,
        grid=(Tg // tm,),
        in_specs=in_specs,
        out_specs=pl.BlockSpec((tm, D), lambda i: (i + row0 // tm, 0)),
        out_shape=jax.ShapeDtypeStruct((out_rows, D), F32),
        input_output_aliases=aliases,
        compiler_params=_cparams("parallel"),
        name="combine",
    )(*args)


def _qk_column_order():
    half = ATT_HEAD_DIM // 2
    order = []
    for p in range(ATT_HEADS // 2):
        for part in range(2):
            for h in (2 * p, 2 * p + 1):
                order.extend(range(h * ATT_HEAD_DIM + part * half, h * ATT_HEAD_DIM + (part + 1) * half))
    return np.asarray(order, np.int32)


def _rope_tables(seq):
    half = ATT_HEAD_DIM // 2
    inv = ROPE_THETA ** (-(jnp.arange(half, dtype=F32) * 2.0 / ATT_HEAD_DIM))
    ang = jnp.arange(seq, dtype=F32)[:, None] * inv[None, :]
    cos, sin = jnp.cos(ang), jnp.sin(ang)
    return (jnp.concatenate([cos, cos, cos, cos], axis=1),
            jnp.concatenate([-sin, -sin, sin, sin], axis=1))


def kernel(x, norm1_w, w_in, moba_up, hgrn_lb_logits, hgrn_norm_w, hgrn_up, w_out, norm2_w,
           router_w, router_b, w_gate_up, b_gate_up, w_down, b_down, final_norm_w):
    B, S, D = x.shape
    T = B * S
    assert S % MOBA_BLOCK == 0 and w_in.shape[0] == 1
    x2 = x.reshape(T, D)

    perm = _qk_column_order()
    w0 = w_in[0]
    w_in_p = jnp.concatenate([w0[:, :ATT_WIDTH][:, perm], w0[:, ATT_WIDTH:2 * ATT_WIDTH][:, perm],
                              w0[:, 2 * ATT_WIDTH:]], axis=1).astype(BF16)
    cos_t, sin_t = _rope_tables(S)
    lb = jnp.cumsum(jax.nn.softmax(hgrn_lb_logits.astype(F32), axis=0), axis=0)[0:1]

    qa, ka, va, qb, fb, ib, gb, ga, gtb = _in_proj(x2, norm1_w, w_in_p, cos_t, sin_t, S)
    r3 = lambda a: a.reshape(B, S, a.shape[1])
    ya = _moba(r3(qa), r3(ka), r3(va)).reshape(T, ATT_WIDTH)
    yb = _hgrn(r3(qb), r3(fb), r3(ib), r3(gb), lb, hgrn_norm_w).reshape(T, HGRN_WIDTH)

    rw_t = router_w[0].T
    rw_hi = rw_t.astype(BF16)
    rw_lo = (rw_t - rw_hi.astype(F32)).astype(BF16)
    mix_w = (moba_up[0].astype(BF16), hgrn_up[0].astype(BF16), w_out[0].astype(BF16), norm2_w,
             jnp.concatenate([rw_hi, rw_hi, rw_lo], axis=1), router_b[0][:, None])
    expert_w = (w_gate_up[0], b_gate_up[0][:, None, 0::2], b_gate_up[0][:, None, 1::2],
                w_down[0], b_down[0][:, None, :])

    Tg = T // MOE_TOKEN_GROUPS
    out = None
    for g in range(MOE_TOKEN_GROUPS):
        x1, h2, top_e, top_w, rank, cnt = _mix_route(ya, yb, ga, gtb, x2, *mix_w, g * Tg, Tg)
        counts = cnt[:, 0].astype(jnp.int32)
        sb = EXPERT_SUB_ROWS
        padded = (counts + sb - 1) // sb * sb
        start = jnp.cumsum(padded) - padded
        dest = rank + jnp.sum(jnp.where(top_e[:, :, None] == jnp.arange(N_EXPERTS)[None, None, :],
                                        start[None, None, :], 0), axis=-1)
        body_rows = TOP_K * Tg + N_EXPERTS * sb
        spare_row = body_rows
        i = jnp.arange(sb, dtype=jnp.int32)[None, :]
        pad_rows = jnp.where(i < (padded - counts)[:, None], (start + counts)[:, None] + i, spare_row)
        xg = _scatter_rows(dest, h2, body_rows + EXPERT_ROWS, pad_rows)
        yg = _experts(xg, padded, *expert_w)
        yk = _gather_rows(dest, yg).reshape(TOP_K, Tg, D // 2)
        out = _combine(x1, yk, top_w.T, final_norm_w[None, :], T, g * Tg, out)
    return out.reshape(B, S, D)
```

```python
import functools
import math

import numpy as np
import jax
import jax.numpy as jnp
from jax import lax
from jax.experimental import pallas as pl
from jax.experimental.pallas import tpu as pltpu
from jax.experimental.pallas import tpu_sc as plsc

ATT_HEADS = 8
ATT_HEAD_DIM = 64
ATT_WIDTH = ATT_HEADS * ATT_HEAD_DIM
MOBA_BLOCK = 256
MOBA_TOPK = 3
ROPE_THETA = 10000.0
HGRN_HEADS = 4
HGRN_DIM = 128
HGRN_WIDTH = HGRN_HEADS * HGRN_DIM
HGRN_CHUNK = 64
N_EXPERTS = 32
TOP_K = 4
SWIGLU_LIMIT = 7.0
SWIGLU_ALPHA = 1.702
RMS_EPS = 1e-6
NEG = -1e30

V7X_LANES = 128
V7X_SUBLANES = 8
V7X_MXU_DIM = 256
V7X_VMEM_LIMIT_BYTES = 56 * 1024 * 1024
V7X_SC_CORES = 2
V7X_SC_SUBCORES = 16

MOBA_PAIRS = 2
MOBA_GROUP = 4
PROJ_ROWS = 1024
PROJ_COLS = 512
HGRN_ROWS = 512
MIX_ROWS = 512
EXPERT_ROWS = 512
EXPERT_SUB_ROWS = 256
MOE_TOKEN_GROUPS = 2
SC_ROWS = 64
COMBINE_ROWS = 512

F32 = jnp.float32
BF16 = jnp.bfloat16


def _nt_dot(a, b, precision=None):
    return lax.dot_general(a, b, (((1,), (1,)), ((), ())), precision=precision,
                           preferred_element_type=F32)


def _tn_dot(a, b, precision=None):
    return lax.dot_general(a, b, (((0,), (0,)), ((), ())), precision=precision,
                           preferred_element_type=F32)


def _cparams(*sem):
    return pltpu.CompilerParams(dimension_semantics=sem, vmem_limit_bytes=V7X_VMEM_LIMIT_BYTES)


def _pack_bf16_pair(lo, hi):
    lo_bits = lax.bitcast_convert_type(lo.astype(BF16).astype(F32), jnp.uint32)
    hi_bits = lax.bitcast_convert_type(hi.astype(BF16).astype(F32), jnp.uint32)
    return (lo_bits >> 16) | hi_bits


def _unpack_bf16_pair(word):
    lo = lax.bitcast_convert_type(word << 16, F32)
    hi = lax.bitcast_convert_type(word & jnp.uint32(0xFFFF0000), F32)
    return lo, hi


def _in_proj_kernel(x_ref, nw_ref, w_ref, cos_ref, sin_ref,
                    qa_ref, ka_ref, va_ref, qb_ref, fb_ref, ib_ref, gb_ref, ga_ref, gtb_ref):
    x = x_ref[...]
    h = x * lax.rsqrt(jnp.mean(x * x, axis=-1, keepdims=True) + RMS_EPS) * nw_ref[...]
    h = h.astype(BF16)
    cos = cos_ref[...]
    sin = sin_ref[...]

    def proj(c):
        return jnp.dot(h, w_ref[:, c * PROJ_COLS:(c + 1) * PROJ_COLS], preferred_element_type=F32)

    def rope(t):
        out = []
        for j in range(PROJ_COLS // V7X_LANES):
            tj = t[:, j * V7X_LANES:(j + 1) * V7X_LANES]
            out.append(tj * cos + pltpu.roll(tj, V7X_LANES // 2, 1) * sin)
        return jnp.concatenate(out, axis=1)

    qa_ref[...] = (rope(proj(0)) * (ATT_HEAD_DIM ** -0.5)).astype(BF16)
    ka_ref[...] = rope(proj(1)).astype(BF16)
    va_ref[...] = proj(2).astype(BF16)
    qb_ref[...] = proj(3).astype(BF16)
    fb_ref[...] = proj(4)
    ib_ref[...] = proj(5).astype(BF16)
    gb_ref[...] = proj(6).astype(BF16)
    ga_ref[:, :PROJ_COLS] = proj(7).astype(BF16)
    ga_ref[:, PROJ_COLS:] = proj(8).astype(BF16)
    gtb_ref[:, :PROJ_COLS] = proj(9).astype(BF16)
    gtb_ref[:, PROJ_COLS:] = proj(10).astype(BF16)


def _in_proj(x2, norm_w, w_in_bf16, cos_t, sin_t, seq):
    T, D = x2.shape
    tm = min(PROJ_ROWS, seq)
    n_seq_tiles = seq // tm
    row = lambda w: pl.BlockSpec((tm, w), lambda i: (i, 0))
    tab = pl.BlockSpec((tm, V7X_LANES), lambda i: (i % n_seq_tiles, 0))
    widths = [ATT_WIDTH] * 3 + [HGRN_WIDTH] * 4 + [D, D]
    dtypes = [BF16, BF16, BF16, BF16, F32, BF16, BF16, BF16, BF16]
    return pl.pallas_call(
        _in_proj_kernel,
        grid=(T // tm,),
        in_specs=[row(D), pl.BlockSpec((1, D), lambda i: (0, 0)),
                  pl.BlockSpec(memory_space=pltpu.VMEM), tab, tab],
        out_specs=[row(w) for w in widths],
        out_shape=[jax.ShapeDtypeStruct((T, w), dt) for w, dt in zip(widths, dtypes)],
        compiler_params=_cparams("parallel"),
        name="in_proj",
    )(x2, norm_w, w_in_bf16, cos_t, sin_t)


def _moba_kernel(q_ref, k_ref, v_ref, o_ref, kaug_ref, kmean_ref, vt_ref, acc_ref, *s_refs, nb):
    qi = pl.program_id(2)
    blk = MOBA_BLOCK
    lanes = V7X_LANES
    n_pairs = kaug_ref.shape[0]
    heads = [(pp, hh) for pp in range(n_pairs) for hh in range(2)]
    nbp = kmean_ref.shape[1]

    @pl.when(qi == 0)
    def _():
        rowb = lax.broadcasted_iota(jnp.int32, (nb * blk, lanes), 0) // blk
        col = lax.broadcasted_iota(jnp.int32, (nb * blk, lanes), 1)
        onehot = jnp.where(rowb == col, 1.0, 0.0).astype(BF16)
        for pp in range(n_pairs):
            pl_ = slice(pp * lanes, (pp + 1) * lanes)
            kaug_ref[pp, :, :lanes] = k_ref[0, :, pl_]
            kaug_ref[pp, :, lanes:] = onehot
            means = [jnp.sum(k_ref[0, n * blk:(n + 1) * blk, pl_].astype(F32), axis=0,
                             keepdims=True) * (1.0 / blk) for n in range(nb)]
            km = jnp.concatenate(means + [jnp.zeros((nbp - nb, lanes), F32)] * (nbp > nb), axis=0)
            hi = km.astype(BF16)
            rem = km - hi.astype(F32)
            mid = rem.astype(BF16)
            lo = (rem - mid.astype(F32)).astype(BF16)
            kmean_ref[pp] = jnp.concatenate([hi, mid, lo], axis=1)
            for c in range(nb):
                vt = v_ref[0, c * blk:(c + 1) * blk, pl_].astype(F32).T
                vt_ref[pp, :lanes, c * blk:(c + 1) * blk] = vt.astype(BF16)
            vt_ref[pp, lanes:, :] = jnp.ones((vt_ref.shape[1] - lanes, nb * blk), BF16)

    feat = lax.broadcasted_iota(jnp.int32, (lanes, blk), 0)
    key_i = lax.broadcasted_iota(jnp.int32, (blk, blk), 0)
    qry_i = lax.broadcasted_iota(jnp.int32, (blk, blk), 1)
    own = pl.multiple_of(qi * blk, blk)
    blk_id = lax.broadcasted_iota(jnp.int32, (nbp, 2 * blk), 0)
    slab = 2 * V7X_SUBLANES

    def slab_max(s):
        return jnp.max(s.reshape(s.shape[0] // slab, slab, blk), axis=0)

    q_aug, m_init = [], []
    for pp in range(n_pairs):
        qt = q_ref[0, :, pp * lanes:(pp + 1) * lanes].astype(F32).T
        k_own = kaug_ref[pp, pl.ds(own, blk), :lanes]
        qhs = [jnp.where((feat // (ATT_HEAD_DIM // 2)) % 2 == hh, qt, 0.0).astype(BF16)
               for hh in range(2)]
        q2 = jnp.concatenate(qhs, axis=1)
        gate = jnp.dot(kmean_ref[pp], jnp.concatenate([q2, q2, q2], axis=0),
                       preferred_element_type=F32)
        gate = jnp.where(blk_id < qi, gate, NEG)
        beaten = jnp.zeros((nbp, 2 * blk), F32)
        for n in range(nb):
            gn = gate[n:n + 1, :]
            wins = (gn > gate) | ((gn == gate) & (blk_id > n))
            beaten = beaten + jnp.where(wins, 1.0, 0.0)
        sel = (beaten < MOBA_TOPK) & (blk_id < qi)
        bias = jnp.where(sel, 0.0, NEG).astype(BF16)
        bias = jnp.concatenate([bias, jnp.zeros((lanes - nbp, 2 * blk), BF16)], axis=0)
        for hh in range(2):
            h = 2 * pp + hh
            q_aug.append(jnp.concatenate([qhs[hh], bias[:, hh * blk:(hh + 1) * blk]], axis=0))
            s = jnp.dot(k_own, qhs[hh], preferred_element_type=F32)
            s = jnp.where(key_i <= qry_i, s, NEG)
            m0 = jnp.max(slab_max(s), axis=0, keepdims=True)
            p = jnp.exp(s - m0).astype(BF16)
            acc_ref[h] = jnp.dot(vt_ref[pp, :, pl.ds(own, blk)], p, preferred_element_type=F32)
            m_init.append(m0)

    group = MOBA_GROUP * blk
    n_groups = (qi + MOBA_GROUP) // MOBA_GROUP

    def score_head(g, h):
        off = pl.multiple_of(g * group, group)
        s = jnp.dot(kaug_ref[h // 2, pl.ds(off, group), :], q_aug[h], preferred_element_type=F32)
        s_refs[h][pl.ds(off, group), :] = s
        return jnp.max(slab_max(s), axis=0, keepdims=True)

    def value_head(g, h, gmax, m_old):
        off = pl.multiple_of(g * group, group)
        m_new = jnp.maximum(m_old, gmax)
        alpha = jnp.exp(m_old - m_new)
        p = jnp.exp(s_refs[h][pl.ds(off, group), :] - m_new).astype(BF16)
        acc_ref[h] = alpha * acc_ref[h] + jnp.dot(vt_ref[h // 2, :, pl.ds(off, group)], p,
                                                  preferred_element_type=F32)
        return m_new

    def pipelined(g, carry):
        gmax, ms = carry
        last = len(heads) - 1
        new_gmax, new_ms = [None] * (last + 1), [None] * (last + 1)
        new_gmax[last] = score_head(g + 1, last)
        for h in range(last + 1):
            new_ms[h] = value_head(g, h, gmax[h], ms[h])
            if h < last:
                new_gmax[h] = score_head(g + 1, h)
        return tuple(new_gmax), tuple(new_ms)

    first = tuple(score_head(0, h) for h in range(len(heads)))
    gmax, ms = lax.fori_loop(0, n_groups - 1, pipelined, (first, tuple(m_init)))
    for h in range(len(heads)):
        value_head(n_groups - 1, h, gmax[h], ms[h])

    half = lanes // 2
    outs = []
    for pp in range(n_pairs):
        outs.append(acc_ref[2 * pp, :half, :] / acc_ref[2 * pp, lanes:lanes + 1, :])
        outs.append(acc_ref[2 * pp + 1, half:lanes, :] / acc_ref[2 * pp + 1, lanes:lanes + 1, :])
    o_ref[0] = jnp.concatenate(outs, axis=0).T.astype(BF16)


def _moba(q, k, v):
    B, S, _ = q.shape
    nb = S // MOBA_BLOCK
    assert nb % MOBA_GROUP == 0
    nbp = -(-nb // (2 * V7X_SUBLANES)) * (2 * V7X_SUBLANES)
    n_pairs = ATT_WIDTH // V7X_LANES
    pp = MOBA_PAIRS
    assert n_pairs % pp == 0
    vt_rows = V7X_LANES + 2 * V7X_SUBLANES
    qspec = pl.BlockSpec((1, MOBA_BLOCK, pp * V7X_LANES), lambda b, p, i: (b, i, p))
    kvspec = pl.BlockSpec((1, S, pp * V7X_LANES), lambda b, p, i: (b, 0, p))
    return pl.pallas_call(
        functools.partial(_moba_kernel, nb=nb),
        grid=(B, n_pairs // pp, nb),
        in_specs=[qspec, kvspec, kvspec],
        out_specs=qspec,
        out_shape=jax.ShapeDtypeStruct((B, S, ATT_WIDTH), BF16),
        scratch_shapes=[pltpu.VMEM((pp, S, 2 * V7X_LANES), BF16),
                        pltpu.VMEM((pp, nbp, 3 * V7X_LANES), BF16),
                        pltpu.VMEM((pp, vt_rows, S), BF16),
                        pltpu.VMEM((2 * pp, vt_rows, MOBA_BLOCK), F32)]
                       + [pltpu.VMEM((S, MOBA_BLOCK), F32)] * (2 * pp),
        compiler_params=_cparams("parallel", "parallel", "arbitrary"),
        name="moba",
    )(q, k, v)


def _hgrn_level_sizes(chunk):
    return [chunk >> (i + 1) for i in range(int(math.log2(chunk)))]


def _hgrn_constants(chunk):
    t = np.arange(chunk)
    mats = [(t[None, :] <= t[:, None]),
            (t[None, :] > t[:, None])]
    qrows, pmasks = [], []
    for bs in _hgrn_level_sizes(chunk):
        blk = t // bs
        odd = (blk % 2) == 1
        lo, hi = blk * bs, (blk + 1) * bs
        u = t[None, :]
        m_odd = (u >= lo[:, None]) & (u <= t[:, None])
        m_even = (u > t[:, None]) & (u < hi[:, None])
        mats.append(np.where(odd[:, None], m_odd, m_even))
        qrows.append(odd)
        pmasks.append(odd[:, None] & (blk[None, :] == blk[:, None] - 1))
    pmasks.append(t[None, :] == t[:, None])
    summat = np.concatenate(mats, axis=0).astype(np.float32)
    qrow = np.stack(qrows, axis=0).astype(np.float32)
    pmask = np.stack(pmasks, axis=0).astype(np.float32)
    return summat, qrow, pmask


def _hgrn_kernel(q_ref, f_ref, i_ref, g_ref, lb_ref, nw_ref, sm_ref, qrow_ref, pm_ref,
                 o_ref, state_ref, *, rows):
    C = HGRN_CHUNK
    n_levels = qrow_ref.shape[0]

    @pl.when(pl.program_id(1) == 0)
    def _():
        state_ref[...] = jnp.zeros_like(state_ref)

    lb = lb_ref[...]
    summat = sm_ref[...]
    for c in range(rows // C):
        rs = slice(c * C, (c + 1) * C)
        fg = lb + (1.0 - lb) * jax.nn.sigmoid(f_ref[0, rs, :])
        logf = jnp.log(fg)
        hi = logf.astype(BF16)
        rem = logf - hi.astype(F32)
        mid = rem.astype(BF16)
        lo = (rem - mid.astype(F32)).astype(BF16)
        sums = jnp.dot(summat, jnp.concatenate([hi, mid, lo], axis=0),
                       preferred_element_type=F32)
        for h in range(HGRN_HEADS):
            ls = slice(h * HGRN_DIM, (h + 1) * HGRN_DIM)
            qf = jax.nn.silu(q_ref[0, rs, ls].astype(F32))
            kf = 1.0 - fg[:, ls]
            iv = i_ref[0, rs, ls]
            bcum = sums[0:C, ls]
            bsuf = sums[C:2 * C, ls]
            att = _nt_dot(qf.astype(BF16), kf.astype(BF16)) * pm_ref[n_levels]
            for lv in range(n_levels):
                w = jnp.exp(sums[(2 + lv) * C:(3 + lv) * C, ls])
                qrow = qrow_ref[lv]
                z = (jnp.where(qrow > 0.5, qf, kf) * w).astype(BF16)
                att = att + _nt_dot(z, z) * pm_ref[lv]
            o = jnp.dot(att.astype(BF16), iv, preferred_element_type=F32)
            st = state_ref[h]
            o = o + _nt_dot((qf * jnp.exp(bcum)).astype(BF16), st.astype(BF16))
            kdec = (kf * jnp.exp(bsuf)).astype(BF16)
            state_ref[h] = st * jnp.exp(bcum[C - 1:C, :]) + _tn_dot(iv, kdec)
            o = o * lax.rsqrt(jnp.mean(o * o, axis=-1, keepdims=True) + RMS_EPS)
            o = o * nw_ref[:, ls] * jax.nn.silu(g_ref[0, rs, ls].astype(F32))
            o_ref[0, rs, ls] = o.astype(BF16)


def _hgrn(qb, fb, ib, gb, lb, norm_w):
    B, S, W = qb.shape
    rows = min(HGRN_ROWS, S)
    summat, qrow, pmask = _hgrn_constants(HGRN_CHUNK)
    summat = jnp.asarray(np.concatenate([summat] * 3, axis=1), BF16)
    n_levels = qrow.shape[0]
    blk = pl.BlockSpec((1, rows, W), lambda b, s: (b, s, 0))
    vec = pl.BlockSpec((1, W), lambda b, s: (0, 0))
    const = lambda a: pl.BlockSpec(a.shape, lambda b, s: (0,) * a.ndim)
    qrow3 = qrow.reshape(n_levels, HGRN_CHUNK, 1)
    return pl.pallas_call(
        functools.partial(_hgrn_kernel, rows=rows),
        grid=(B, S // rows),
        in_specs=[blk, blk, blk, blk, vec, vec, const(summat), const(qrow3), const(pmask)],
        out_specs=blk,
        out_shape=jax.ShapeDtypeStruct((B, S, W), BF16),
        scratch_shapes=[pltpu.VMEM((HGRN_HEADS, HGRN_DIM, HGRN_DIM), F32)],
        compiler_params=_cparams("parallel", "arbitrary"),
        name="hgrn",
    )(qb, fb, ib, gb, lb, norm_w, jnp.asarray(summat), jnp.asarray(qrow3), jnp.asarray(pmask))


def _mix_route_kernel(ya_ref, yb_ref, ga_ref, gb_ref, x_ref, wa_ref, wb_ref, wo_ref, nw_ref,
                      rw_ref, rb_ref, tri_ref,
                      x1_ref, h2_ref, e_ref, w_ref, rank_ref, cnt_ref, carry_ref):
    @pl.when(pl.program_id(0) == 0)
    def _():
        carry_ref[...] = jnp.zeros_like(carry_ref)

    ua = jnp.dot(ya_ref[...], wa_ref[...], preferred_element_type=F32)
    ub = jnp.dot(yb_ref[...], wb_ref[...], preferred_element_type=F32)
    mixed = (jax.nn.sigmoid(ga_ref[...].astype(F32)) * ua
             + jax.nn.sigmoid(gb_ref[...].astype(F32)) * ub)
    x1 = x_ref[...] + jnp.dot(mixed.astype(BF16), wo_ref[...], preferred_element_type=F32)
    x1_ref[...] = x1
    h2 = x1 * lax.rsqrt(jnp.mean(x1 * x1, axis=-1, keepdims=True) + RMS_EPS) * nw_ref[...]
    half = h2.shape[1] // 2
    h2_ref[...] = _pack_bf16_pair(h2[:, :half], h2[:, half:])

    tm = x1.shape[0]
    h2_hi = h2.astype(BF16)
    h2_lo = (h2 - h2_hi.astype(F32)).astype(BF16)
    logits = _nt_dot(rw_ref[...], jnp.concatenate([h2_hi, h2_lo, h2_hi], axis=1)) + rb_ref[...]
    eid = lax.broadcasted_iota(jnp.int32, (N_EXPERTS, tm), 0)
    work = logits
    es, vs = [], []
    for _ in range(TOP_K):
        mx = jnp.max(work, axis=0, keepdims=True)
        idx = jnp.min(jnp.where(work == mx, eid, N_EXPERTS), axis=0, keepdims=True)
        es.append(idx)
        vs.append(mx)
        work = jnp.where(eid == idx, -jnp.inf, work)
    ex = [jnp.exp(v - vs[0]) for v in vs]
    den = ex[0] + ex[1] + ex[2] + ex[3]
    multi = jnp.zeros((N_EXPERTS, tm), F32)
    for k in range(TOP_K):
        multi = multi + jnp.where(eid == es[k], 1.0, 0.0)
    before = jnp.dot(multi.astype(BF16), tri_ref[...], preferred_element_type=F32) + carry_ref[...]
    for k in range(TOP_K):
        e_ref[k:k + 1, :] = es[k]
        w_ref[k:k + 1, :] = ex[k] / den
        rank_ref[k:k + 1, :] = jnp.sum(jnp.where(eid == es[k], before, 0.0), axis=0,
                                       keepdims=True).astype(jnp.int32)
    carry_ref[...] = carry_ref[...] + jnp.sum(multi, axis=1, keepdims=True)
    cnt_ref[...] = jnp.broadcast_to(carry_ref[...], cnt_ref.shape)


def _mix_route(ya, yb, ga, gb, x2, wa, wb, wo, norm_w, rw_t, rb, row0, rows):
    D = x2.shape[1]
    T = rows
    tm = min(MIX_ROWS, T)
    row = lambda w: pl.BlockSpec((tm, w), lambda i: (i + row0 // tm, 0))
    out_row = pl.BlockSpec((tm, D), lambda i: (i, 0))
    whole = pl.BlockSpec(memory_space=pltpu.VMEM)
    kt = pl.BlockSpec((TOP_K, tm), lambda i: (0, i))
    tri = jnp.asarray(np.triu(np.ones((tm, tm), np.float32), 1), BF16)
    return pl.pallas_call(
        _mix_route_kernel,
        grid=(T // tm,),
        in_specs=[row(ATT_WIDTH), row(HGRN_WIDTH), row(D), row(D), row(D),
                  whole, whole, whole, pl.BlockSpec((1, D), lambda i: (0, 0)),
                  whole, whole, whole],
        out_specs=[out_row, pl.BlockSpec((tm, D // 2), lambda i: (i, 0)), kt, kt, kt,
                   pl.BlockSpec((N_EXPERTS, V7X_LANES), lambda i: (0, 0))],
        out_shape=[jax.ShapeDtypeStruct((T, D), F32), jax.ShapeDtypeStruct((T, D // 2), jnp.uint32),
                   jax.ShapeDtypeStruct((TOP_K, T), jnp.int32),
                   jax.ShapeDtypeStruct((TOP_K, T), F32),
                   jax.ShapeDtypeStruct((TOP_K, T), jnp.int32),
                   jax.ShapeDtypeStruct((N_EXPERTS, V7X_LANES), F32)],
        scratch_shapes=[pltpu.VMEM((N_EXPERTS, 1), F32)],
        compiler_params=_cparams("arbitrary"),
        name="mix_route",
    )(ya, yb, ga, gb, x2, wa, wb, wo, norm_w, rw_t, rb, tri)


def _sc_worker_id():
    return lax.axis_index("s") * V7X_SC_CORES + lax.axis_index("c")


def _sc_kernel(body, out_rows, like, window, name, extra_scratch=()):
    mesh = plsc.VectorSubcoreMesh(core_axis_name="c", subcore_axis_name="s")
    return pl.kernel(
        body, mesh=mesh,
        out_type=jax.ShapeDtypeStruct((out_rows,) + like.shape[1:], like.dtype),
        scratch_types=[pltpu.VMEM((window,), jnp.int32),
                       pltpu.VMEM((window,) + like.shape[1:], like.dtype), *extra_scratch],
        name=name)


def _scatter_rows(dest, h2, out_rows, pad_rows):
    T = dest.shape[1]
    n_workers = V7X_SC_CORES * V7X_SC_SUBCORES
    per_worker = T // n_workers
    window = min(SC_ROWS, per_worker)
    n_pad = pad_rows.shape[1]
    assert pad_rows.shape[0] == n_workers and n_pad % window == 0

    def body(dest_hbm, src_hbm, pad_hbm, zero_hbm, dst_hbm, idx_v, rows_v, pad_v, zero_v, sem):
        wid = _sc_worker_id()
        base = wid * per_worker

        pltpu.sync_copy(pad_hbm.at[pl.ds(wid * n_pad, n_pad)], pad_v)
        pltpu.sync_copy(zero_hbm, zero_v)
        fills = [pltpu.async_copy(zero_v, dst_hbm.at[pad_v.at[pl.ds(c * window, window)]], sem)
                 for c in range(n_pad // window)]

        @pl.loop(0, per_worker // window)
        def _(c):
            t0 = base + c * window
            pltpu.sync_copy(src_hbm.at[pl.ds(t0, window)], rows_v)
            for k in range(TOP_K):
                pltpu.sync_copy(dest_hbm.at[pl.ds(k * T + t0, window)], idx_v)
                pltpu.sync_copy(rows_v, dst_hbm.at[idx_v])

        for f in fills:
            f.wait()

    zeros = jnp.zeros((window,) + h2.shape[1:], h2.dtype)
    extra = [pltpu.VMEM((n_pad,), jnp.int32), pltpu.VMEM(zeros.shape, zeros.dtype),
             pltpu.SemaphoreType.DMA]
    return _sc_kernel(body, out_rows, h2, window, "scatter_rows", extra)(
        dest.reshape(-1), h2, pad_rows.reshape(-1), zeros)


def _gather_rows(dest, yg):
    T = dest.shape[1]
    n_workers = V7X_SC_CORES * V7X_SC_SUBCORES
    per_worker = TOP_K * T // n_workers
    window = min(SC_ROWS, per_worker)

    def body(dest_hbm, src_hbm, dst_hbm, idx_v, rows_v):
        base = _sc_worker_id() * per_worker

        @pl.loop(0, per_worker // window)
        def _(c):
            r0 = base + c * window
            pltpu.sync_copy(dest_hbm.at[pl.ds(r0, window)], idx_v)
            pltpu.sync_copy(src_hbm.at[idx_v], rows_v)
            pltpu.sync_copy(rows_v, dst_hbm.at[pl.ds(r0, window)])

    return _sc_kernel(body, TOP_K * T, yg, window, "gather_rows")(dest.reshape(-1), yg)


def _experts_kernel(blk_ref, exp_ref, lo_ref, hi_ref, slot_ref, efirst_ref, enext_ref,
                    x_ref, wgu_hbm, bg_ref, bu_ref, wd_hbm, bd_ref, perm_ref, y_ref,
                    wgu_stage, wd_stage, wg_buf, wu_buf, wd_buf, sems):
    w = pl.program_id(0)
    lo = lo_ref[w]
    hi = hi_ref[w]
    slot = slot_ref[w]

    tm = x_ref.shape[0]
    n_out = V7X_MXU_DIM

    def weight_copies(e, s):
        return [pltpu.make_async_copy(wgu_hbm.at[e], wgu_stage.at[s], sems.at[s, 0]),
                pltpu.make_async_copy(wd_hbm.at[e], wd_stage.at[s], sems.at[s, 1])]

    @pl.when(w == 0)
    def _():
        for c in weight_copies(exp_ref[0], slot):
            c.start()

    @pl.when(efirst_ref[w] == 1)
    def _():
        for c in weight_copies(exp_ref[w], slot):
            c.wait()

        @pl.when(enext_ref[w] >= 0)
        def _():
            for c in weight_copies(enext_ref[w], 1 - slot):
                c.start()

        half = V7X_MXU_DIM // 2
        for g in range(wgu_stage.shape[2] // V7X_MXU_DIM):
            wb = wgu_stage[slot, :, g * V7X_MXU_DIM:(g + 1) * V7X_MXU_DIM].astype(BF16)
            d = jnp.dot(wb, perm_ref[...], preferred_element_type=F32).astype(BF16)
            wg_buf[:, g * half:(g + 1) * half] = d[:, :half]
            wu_buf[:, g * half:(g + 1) * half] = d[:, half:]
        wd_buf[...] = wd_stage[slot].astype(BF16)

    sb = EXPERT_SUB_ROWS

    def ffn(r0):
        rs = slice(r0, r0 + sb)
        x = jnp.concatenate(_unpack_bf16_pair(x_ref[rs, :]), axis=1).astype(BF16)
        g = jnp.dot(x, wg_buf[...], preferred_element_type=F32) + bg_ref[0]
        u = jnp.dot(x, wu_buf[...], preferred_element_type=F32) + bu_ref[0]
        g = jnp.minimum(g, SWIGLU_LIMIT)
        u = jnp.clip(u, -SWIGLU_LIMIT, SWIGLU_LIMIT)
        act = ((u + 1.0) * (g * jax.nn.sigmoid(g * SWIGLU_ALPHA))).astype(BF16)
        half = y_ref.shape[1]

        def down(c0):
            cs = slice(c0, c0 + n_out)
            return (jnp.dot(act, wd_buf[:, cs], preferred_element_type=F32)
                    + bd_ref[0, :, cs])

        for c in range(half // n_out):
            cs = slice(c * n_out, (c + 1) * n_out)
            y_ref[rs, cs] = _pack_bf16_pair(down(c * n_out), down(half + c * n_out))

    n_sub = tm // sb
    fused = (lo == 0) & (hi == tm)

    @pl.when(fused)
    def _():
        for j in range(n_sub):
            ffn(j * sb)

    for j in range(n_sub):
        @pl.when(jnp.logical_not(fused) & (lo <= j * sb) & (hi > j * sb))
        def _():
            ffn(j * sb)


def _work_items(counts, n_rows, tm):
    nblk = n_rows // tm
    n_items = nblk + N_EXPERTS - 1
    end = jnp.cumsum(counts)
    start = end - counts
    fb = start // tm
    nitems = jnp.where(counts > 0, (end - 1) // tm - fb + 1, 0)
    item_end = jnp.cumsum(nitems)
    item_start = item_end - nitems
    w = jnp.arange(n_items, dtype=jnp.int32)
    valid = w < item_end[-1]
    wc = jnp.minimum(w, item_end[-1] - 1)
    e = jnp.sum(wc[:, None] >= item_end[None, :], axis=1).astype(jnp.int32)
    e = jnp.minimum(e, N_EXPERTS - 1)
    onehot = e[:, None] == jnp.arange(N_EXPERTS, dtype=jnp.int32)[None, :]
    pick = lambda table: jnp.sum(jnp.where(onehot, table[None, :], 0), axis=1)
    blk = (pick(fb) + (wc - pick(item_start))).astype(jnp.int32)
    lo = jnp.maximum(pick(start), blk * tm) - blk * tm
    hi = jnp.minimum(pick(end), (blk + 1) * tm) - blk * tm
    lo = jnp.where(valid, lo, 0).astype(jnp.int32)
    hi = jnp.where(valid, hi, 0).astype(jnp.int32)
    used = counts > 0
    ids = jnp.arange(N_EXPERTS, dtype=jnp.int32)
    slot = pick((jnp.cumsum(used) - 1) % 2).astype(jnp.int32)
    efirst = (valid & (w == pick(item_start))).astype(jnp.int32)
    later = jnp.where(used[None, :] & (ids[None, :] > ids[:, None]), ids[None, :], N_EXPERTS)
    next_used = jnp.min(later, axis=1)
    enext = pick(jnp.where(next_used < N_EXPERTS, next_used, -1)).astype(jnp.int32)
    return blk, e, lo, hi, slot, efirst, enext


def _gate_up_split_matrix():
    i = np.arange(V7X_MXU_DIM)
    src = np.where(i < V7X_MXU_DIM // 2, 2 * i, 2 * (i - V7X_MXU_DIM // 2) + 1)
    perm = np.zeros((V7X_MXU_DIM, V7X_MXU_DIM), np.float32)
    perm[src, i] = 1.0
    return jnp.asarray(perm, BF16)


def _experts(xg, counts, w_gate_up, bg, bu, w_down, bd):
    A = xg.shape[0]
    D = 2 * xg.shape[1]
    tm = min(EXPERT_ROWS, A)
    F = w_down.shape[1]
    items = _work_items(counts, A, tm)
    n_items = A // tm + N_EXPERTS - 1
    xs = pl.BlockSpec((tm, D // 2), lambda w, blk, e, *_: (blk[w], 0))
    bias = lambda c: pl.BlockSpec((1, 1, c), lambda w, blk, e, *_: (e[w], 0, 0))
    hbm = pl.BlockSpec(memory_space=pl.ANY)
    return pl.pallas_call(
        _experts_kernel,
        grid_spec=pltpu.PrefetchScalarGridSpec(
            num_scalar_prefetch=len(items),
            grid=(n_items,),
            in_specs=[xs, hbm, bias(F), bias(F), hbm, bias(D), pl.BlockSpec(memory_space=pltpu.VMEM)],
            out_specs=xs,
            scratch_shapes=[pltpu.VMEM((2, D, 2 * F), F32), pltpu.VMEM((2, F, D), F32),
                            pltpu.VMEM((D, F), BF16), pltpu.VMEM((D, F), BF16),
                            pltpu.VMEM((F, D), BF16), pltpu.SemaphoreType.DMA((2, 2))],
        ),
        out_shape=jax.ShapeDtypeStruct(xg.shape, jnp.uint32),
        compiler_params=_cparams("arbitrary"),
        name="experts",
    )(*items, xg, w_gate_up, bg, bu, w_down, bd, _gate_up_split_matrix())


def _combine_kernel(x1_ref, yk_ref, w_ref, nw_ref, *rest):
    o_ref = rest[-1]
    w = w_ref[...]
    x = x1_ref[...]
    for k in range(TOP_K):
        x = x + jnp.concatenate(_unpack_bf16_pair(yk_ref[k]), axis=1) * w[:, k:k + 1]
    o_ref[...] = x * lax.rsqrt(jnp.mean(x * x, axis=-1, keepdims=True) + RMS_EPS) * nw_ref[...]


def _combine(x1, yk, w_tk, norm_w, out_rows, row0, prev_out):
    Tg, D = x1.shape
    tm = min(COMBINE_ROWS, Tg)
    row = pl.BlockSpec((tm, D), lambda i: (i, 0))
    in_specs = [row, pl.BlockSpec((TOP_K, tm, D // 2), lambda i: (0, i, 0)),
                pl.BlockSpec((tm, TOP_K), lambda i: (i, 0)),
                pl.BlockSpec((1, D), lambda i: (0, 0))]
    args = [x1, yk, w_tk, norm_w]
    aliases = {}
    if prev_out is not None:
        in_specs.append(pl.BlockSpec(memory_space=pl.ANY))
        args.append(prev_out)
        aliases = {len(args) - 1: 0}
    return pl.pallas_call(
        _combine_kernel,
        grid=(Tg // tm,),
        in_specs=in_specs,
        out_specs=pl.BlockSpec((tm, D), lambda i: (i + row0 // tm, 0)),
        out_shape=jax.ShapeDtypeStruct((out_rows, D), F32),
        input_output_aliases=aliases,
        compiler_params=_cparams("parallel"),
        name="combine",
    )(*args)


def _qk_column_order():
    half = ATT_HEAD_DIM // 2
    order = []
    for p in range(ATT_HEADS // 2):
        for part in range(2):
            for h in (2 * p, 2 * p + 1):
                order.extend(range(h * ATT_HEAD_DIM + part * half, h * ATT_HEAD_DIM + (part + 1) * half))
    return np.asarray(order, np.int32)


def _rope_tables(seq):
    half = ATT_HEAD_DIM // 2
    inv = ROPE_THETA ** (-(jnp.arange(half, dtype=F32) * 2.0 / ATT_HEAD_DIM))
    ang = jnp.arange(seq, dtype=F32)[:, None] * inv[None, :]
    cos, sin = jnp.cos(ang), jnp.sin(ang)
    return (jnp.concatenate([cos, cos, cos, cos], axis=1),
            jnp.concatenate([-sin, -sin, sin, sin], axis=1))


def kernel(x, norm1_w, w_in, moba_up, hgrn_lb_logits, hgrn_norm_w, hgrn_up, w_out, norm2_w,
           router_w, router_b, w_gate_up, b_gate_up, w_down, b_down, final_norm_w):
    B, S, D = x.shape
    T = B * S
    assert S % MOBA_BLOCK == 0 and w_in.shape[0] == 1
    x2 = x.reshape(T, D)

    perm = _qk_column_order()
    w0 = w_in[0]
    w_in_p = jnp.concatenate([w0[:, :ATT_WIDTH][:, perm], w0[:, ATT_WIDTH:2 * ATT_WIDTH][:, perm],
                              w0[:, 2 * ATT_WIDTH:]], axis=1).astype(BF16)
    cos_t, sin_t = _rope_tables(S)
    lb = jnp.cumsum(jax.nn.softmax(hgrn_lb_logits.astype(F32), axis=0), axis=0)[0:1]

    qa, ka, va, qb, fb, ib, gb, ga, gtb = _in_proj(x2, norm1_w, w_in_p, cos_t, sin_t, S)
    r3 = lambda a: a.reshape(B, S, a.shape[1])
    ya = _moba(r3(qa), r3(ka), r3(va)).reshape(T, ATT_WIDTH)
    yb = _hgrn(r3(qb), r3(fb), r3(ib), r3(gb), lb, hgrn_norm_w).reshape(T, HGRN_WIDTH)

    rw_t = router_w[0].T
    rw_hi = rw_t.astype(BF16)
    rw_lo = (rw_t - rw_hi.astype(F32)).astype(BF16)
    mix_w = (moba_up[0].astype(BF16), hgrn_up[0].astype(BF16), w_out[0].astype(BF16), norm2_w,
             jnp.concatenate([rw_hi, rw_hi, rw_lo], axis=1), router_b[0][:, None])
    expert_w = (w_gate_up[0], b_gate_up[0][:, None, 0::2], b_gate_up[0][:, None, 1::2],
                w_down[0], b_down[0][:, None, :])

    Tg = T // MOE_TOKEN_GROUPS
    out = None
    for g in range(MOE_TOKEN_GROUPS):
        x1, h2, top_e, top_w, rank, cnt = _mix_route(ya, yb, ga, gtb, x2, *mix_w, g * Tg, Tg)
        counts = cnt[:, 0].astype(jnp.int32)
        sb = EXPERT_SUB_ROWS
        padded = (counts + sb - 1) // sb * sb
        start = jnp.cumsum(padded) - padded
        dest = rank + jnp.sum(jnp.where(top_e[:, :, None] == jnp.arange(N_EXPERTS)[None, None, :],
                                        start[None, None, :], 0), axis=-1)
        body_rows = TOP_K * Tg + N_EXPERTS * sb
        spare_row = body_rows
        i = jnp.arange(sb, dtype=jnp.int32)[None, :]
        pad_rows = jnp.where(i < (padded - counts)[:, None], (start + counts)[:, None] + i, spare_row)
        xg = _scatter_rows(dest, h2, body_rows + EXPERT_ROWS, pad_rows)
        yg = _experts(xg, padded, *expert_w)
        yk = _gather_rows(dest, yg).reshape(TOP_K, Tg, D // 2)
        out = _combine(x1, yk, top_w.T, final_norm_w[None, :], T, g * Tg, out)
    return out.reshape(B, S, D)
```

```python
import functools
import math

import numpy as np
import jax
import jax.numpy as jnp
from jax import lax
from jax.experimental import pallas as pl
from jax.experimental.pallas import tpu as pltpu
from jax.experimental.pallas import tpu_sc as plsc

ATT_HEADS = 8
ATT_HEAD_DIM = 64
ATT_WIDTH = ATT_HEADS * ATT_HEAD_DIM
MOBA_BLOCK = 256
MOBA_TOPK = 3
ROPE_THETA = 10000.0
HGRN_HEADS = 4
HGRN_DIM = 128
HGRN_WIDTH = HGRN_HEADS * HGRN_DIM
HGRN_CHUNK = 64
N_EXPERTS = 32
TOP_K = 4
SWIGLU_LIMIT = 7.0
SWIGLU_ALPHA = 1.702
RMS_EPS = 1e-6
NEG = -1e30

V7X_LANES = 128
V7X_SUBLANES = 8
V7X_MXU_DIM = 256
V7X_VMEM_LIMIT_BYTES = 56 * 1024 * 1024
V7X_SC_CORES = 2
V7X_SC_SUBCORES = 16

MOBA_PAIRS = 2
MOBA_GROUP = 4
PROJ_ROWS = 1024
PROJ_COLS = 512
HGRN_ROWS = 512
MIX_ROWS = 512
EXPERT_ROWS = 512
EXPERT_SUB_ROWS = 256
MOE_TOKEN_GROUPS = 2
SC_ROWS = 64
COMBINE_ROWS = 512

F32 = jnp.float32
BF16 = jnp.bfloat16


def _nt_dot(a, b, precision=None):
    return lax.dot_general(a, b, (((1,), (1,)), ((), ())), precision=precision,
                           preferred_element_type=F32)


def _tn_dot(a, b, precision=None):
    return lax.dot_general(a, b, (((0,), (0,)), ((), ())), precision=precision,
                           preferred_element_type=F32)


def _cparams(*sem):
    return pltpu.CompilerParams(dimension_semantics=sem, vmem_limit_bytes=V7X_VMEM_LIMIT_BYTES)


def _pack_bf16_pair(lo, hi):
    lo_bits = lax.bitcast_convert_type(lo.astype(BF16).astype(F32), jnp.uint32)
    hi_bits = lax.bitcast_convert_type(hi.astype(BF16).astype(F32), jnp.uint32)
    return (lo_bits >> 16) | hi_bits


def _unpack_bf16_pair(word):
    lo = lax.bitcast_convert_type(word << 16, F32)
    hi = lax.bitcast_convert_type(word & jnp.uint32(0xFFFF0000), F32)
    return lo, hi


def _in_proj_kernel(x_ref, nw_ref, w_ref, cos_ref, sin_ref,
                    qa_ref, ka_ref, va_ref, qb_ref, fb_ref, ib_ref, gb_ref, ga_ref, gtb_ref):
    x = x_ref[...]
    h = x * lax.rsqrt(jnp.mean(x * x, axis=-1, keepdims=True) + RMS_EPS) * nw_ref[...]
    h = h.astype(BF16)
    cos = cos_ref[...]
    sin = sin_ref[...]

    def proj(c):
        return jnp.dot(h, w_ref[:, c * PROJ_COLS:(c + 1) * PROJ_COLS], preferred_element_type=F32)

    def rope(t):
        out = []
        for j in range(PROJ_COLS // V7X_LANES):
            tj = t[:, j * V7X_LANES:(j + 1) * V7X_LANES]
            out.append(tj * cos + pltpu.roll(tj, V7X_LANES // 2, 1) * sin)
        return jnp.concatenate(out, axis=1)

    qa_ref[...] = (rope(proj(0)) * (ATT_HEAD_DIM ** -0.5)).astype(BF16)
    ka_ref[...] = rope(proj(1)).astype(BF16)
    va_ref[...] = proj(2).astype(BF16)
    qb_ref[...] = proj(3).astype(BF16)
    fb_ref[...] = proj(4)
    ib_ref[...] = proj(5).astype(BF16)
    gb_ref[...] = proj(6).astype(BF16)
    ga_ref[:, :PROJ_COLS] = proj(7).astype(BF16)
    ga_ref[:, PROJ_COLS:] = proj(8).astype(BF16)
    gtb_ref[:, :PROJ_COLS] = proj(9).astype(BF16)
    gtb_ref[:, PROJ_COLS:] = proj(10).astype(BF16)


def _in_proj(x2, norm_w, w_in_bf16, cos_t, sin_t, seq):
    T, D = x2.shape
    tm = min(PROJ_ROWS, seq)
    n_seq_tiles = seq // tm
    row = lambda w: pl.BlockSpec((tm, w), lambda i: (i, 0))
    tab = pl.BlockSpec((tm, V7X_LANES), lambda i: (i % n_seq_tiles, 0))
    widths = [ATT_WIDTH] * 3 + [HGRN_WIDTH] * 4 + [D, D]
    dtypes = [BF16, BF16, BF16, BF16, F32, BF16, BF16, BF16, BF16]
    return pl.pallas_call(
        _in_proj_kernel,
        grid=(T // tm,),
        in_specs=[row(D), pl.BlockSpec((1, D), lambda i: (0, 0)),
                  pl.BlockSpec(memory_space=pltpu.VMEM), tab, tab],
        out_specs=[row(w) for w in widths],
        out_shape=[jax.ShapeDtypeStruct((T, w), dt) for w, dt in zip(widths, dtypes)],
        compiler_params=_cparams("parallel"),
        name="in_proj",
    )(x2, norm_w, w_in_bf16, cos_t, sin_t)


def _moba_kernel(q_ref, k_ref, v_ref, o_ref, kaug_ref, kmean_ref, vt_ref, acc_ref, *s_refs, nb):
    qi = pl.program_id(2)
    blk = MOBA_BLOCK
    lanes = V7X_LANES
    n_pairs = kaug_ref.shape[0]
    heads = [(pp, hh) for pp in range(n_pairs) for hh in range(2)]
    nbp = kmean_ref.shape[1]

    @pl.when(qi == 0)
    def _():
        rowb = lax.broadcasted_iota(jnp.int32, (nb * blk, lanes), 0) // blk
        col = lax.broadcasted_iota(jnp.int32, (nb * blk, lanes), 1)
        onehot = jnp.where(rowb == col, 1.0, 0.0).astype(BF16)
        for pp in range(n_pairs):
            pl_ = slice(pp * lanes, (pp + 1) * lanes)
            kaug_ref[pp, :, :lanes] = k_ref[0, :, pl_]
            kaug_ref[pp, :, lanes:] = onehot
            means = [jnp.sum(k_ref[0, n * blk:(n + 1) * blk, pl_].astype(F32), axis=0,
                             keepdims=True) * (1.0 / blk) for n in range(nb)]
            km = jnp.concatenate(means + [jnp.zeros((nbp - nb, lanes), F32)] * (nbp > nb), axis=0)
            hi = km.astype(BF16)
            rem = km - hi.astype(F32)
            mid = rem.astype(BF16)
            lo = (rem - mid.astype(F32)).astype(BF16)
            kmean_ref[pp] = jnp.concatenate([hi, mid, lo], axis=1)
            for c in range(nb):
                vt = v_ref[0, c * blk:(c + 1) * blk, pl_].astype(F32).T
                vt_ref[pp, :lanes, c * blk:(c + 1) * blk] = vt.astype(BF16)
            vt_ref[pp, lanes:, :] = jnp.ones((vt_ref.shape[1] - lanes, nb * blk), BF16)

    feat = lax.broadcasted_iota(jnp.int32, (lanes, blk), 0)
    key_i = lax.broadcasted_iota(jnp.int32, (blk, blk), 0)
    qry_i = lax.broadcasted_iota(jnp.int32, (blk, blk), 1)
    own = pl.multiple_of(qi * blk, blk)
    blk_id = lax.broadcasted_iota(jnp.int32, (nbp, 2 * blk), 0)
    slab = 2 * V7X_SUBLANES

    def slab_max(s):
        return jnp.max(s.reshape(s.shape[0] // slab, slab, blk), axis=0)

    q_aug, m_init = [], []
    for pp in range(n_pairs):
        qt = q_ref[0, :, pp * lanes:(pp + 1) * lanes].astype(F32).T
        k_own = kaug_ref[pp, pl.ds(own, blk), :lanes]
        qhs = [jnp.where((feat // (ATT_HEAD_DIM // 2)) % 2 == hh, qt, 0.0).astype(BF16)
               for hh in range(2)]
        q2 = jnp.concatenate(qhs, axis=1)
        gate = jnp.dot(kmean_ref[pp], jnp.concatenate([q2, q2, q2], axis=0),
                       preferred_element_type=F32)
        gate = jnp.where(blk_id < qi, gate, NEG)
        beaten = jnp.zeros((nbp, 2 * blk), F32)
        for n in range(nb):
            gn = gate[n:n + 1, :]
            wins = (gn > gate) | ((gn == gate) & (blk_id > n))
            beaten = beaten + jnp.where(wins, 1.0, 0.0)
        sel = (beaten < MOBA_TOPK) & (blk_id < qi)
        bias = jnp.where(sel, 0.0, NEG).astype(BF16)
        bias = jnp.concatenate([bias, jnp.zeros((lanes - nbp, 2 * blk), BF16)], axis=0)
        for hh in range(2):
            h = 2 * pp + hh
            q_aug.append(jnp.concatenate([qhs[hh], bias[:, hh * blk:(hh + 1) * blk]], axis=0))
            s = jnp.dot(k_own, qhs[hh], preferred_element_type=F32)
            s = jnp.where(key_i <= qry_i, s, NEG)
            m0 = jnp.max(slab_max(s), axis=0, keepdims=True)
            p = jnp.exp(s - m0).astype(BF16)
            acc_ref[h] = jnp.dot(vt_ref[pp, :, pl.ds(own, blk)], p, preferred_element_type=F32)
            m_init.append(m0)

    group = MOBA_GROUP * blk
    n_groups = (qi + MOBA_GROUP) // MOBA_GROUP

    def score_head(g, h):
        off = pl.multiple_of(g * group, group)
        s = jnp.dot(kaug_ref[h // 2, pl.ds(off, group), :], q_aug[h], preferred_element_type=F32)
        s_refs[h][pl.ds(off, group), :] = s
        return jnp.max(slab_max(s), axis=0, keepdims=True)

    def value_head(g, h, gmax, m_old):
        off = pl.multiple_of(g * group, group)
        m_new = jnp.maximum(m_old, gmax)
        alpha = jnp.exp(m_old - m_new)
        p = jnp.exp(s_refs[h][pl.ds(off, group), :] - m_new).astype(BF16)
        acc_ref[h] = alpha * acc_ref[h] + jnp.dot(vt_ref[h // 2, :, pl.ds(off, group)], p,
                                                  preferred_element_type=F32)
        return m_new

    def pipelined(g, carry):
        gmax, ms = carry
        last = len(heads) - 1
        new_gmax, new_ms = [None] * (last + 1), [None] * (last + 1)
        new_gmax[last] = score_head(g + 1, last)
        for h in range(last + 1):
            new_ms[h] = value_head(g, h, gmax[h], ms[h])
            if h < last:
                new_gmax[h] = score_head(g + 1, h)
        return tuple(new_gmax), tuple(new_ms)

    first = tuple(score_head(0, h) for h in range(len(heads)))
    gmax, ms = lax.fori_loop(0, n_groups - 1, pipelined, (first, tuple(m_init)))
    for h in range(len(heads)):
        value_head(n_groups - 1, h, gmax[h], ms[h])

    half = lanes // 2
    outs = []
    for pp in range(n_pairs):
        outs.append(acc_ref[2 * pp, :half, :] / acc_ref[2 * pp, lanes:lanes + 1, :])
        outs.append(acc_ref[2 * pp + 1, half:lanes, :] / acc_ref[2 * pp + 1, lanes:lanes + 1, :])
    o_ref[0] = jnp.concatenate(outs, axis=0).T.astype(BF16)


def _moba(q, k, v):
    B, S, _ = q.shape
    nb = S // MOBA_BLOCK
    assert nb % MOBA_GROUP == 0
    nbp = -(-nb // (2 * V7X_SUBLANES)) * (2 * V7X_SUBLANES)
    n_pairs = ATT_WIDTH // V7X_LANES
    pp = MOBA_PAIRS
    assert n_pairs % pp == 0
    vt_rows = V7X_LANES + 2 * V7X_SUBLANES
    qspec = pl.BlockSpec((1, MOBA_BLOCK, pp * V7X_LANES), lambda b, p, i: (b, i, p))
    kvspec = pl.BlockSpec((1, S, pp * V7X_LANES), lambda b, p, i: (b, 0, p))
    return pl.pallas_call(
        functools.partial(_moba_kernel, nb=nb),
        grid=(B, n_pairs // pp, nb),
        in_specs=[qspec, kvspec, kvspec],
        out_specs=qspec,
        out_shape=jax.ShapeDtypeStruct((B, S, ATT_WIDTH), BF16),
        scratch_shapes=[pltpu.VMEM((pp, S, 2 * V7X_LANES), BF16),
                        pltpu.VMEM((pp, nbp, 3 * V7X_LANES), BF16),
                        pltpu.VMEM((pp, vt_rows, S), BF16),
                        pltpu.VMEM((2 * pp, vt_rows, MOBA_BLOCK), F32)]
                       + [pltpu.VMEM((S, MOBA_BLOCK), F32)] * (2 * pp),
        compiler_params=_cparams("parallel", "parallel", "arbitrary"),
        name="moba",
    )(q, k, v)


def _hgrn_level_sizes(chunk):
    return [chunk >> (i + 1) for i in range(int(math.log2(chunk)))]


def _hgrn_constants(chunk):
    t = np.arange(chunk)
    mats = [(t[None, :] <= t[:, None]),
            (t[None, :] > t[:, None])]
    qrows, pmasks = [], []
    for bs in _hgrn_level_sizes(chunk):
        blk = t // bs
        odd = (blk % 2) == 1
        lo, hi = blk * bs, (blk + 1) * bs
        u = t[None, :]
        m_odd = (u >= lo[:, None]) & (u <= t[:, None])
        m_even = (u > t[:, None]) & (u < hi[:, None])
        mats.append(np.where(odd[:, None], m_odd, m_even))
        qrows.append(odd)
        pmasks.append(odd[:, None] & (blk[None, :] == blk[:, None] - 1))
    pmasks.append(t[None, :] == t[:, None])
    summat = np.concatenate(mats, axis=0).astype(np.float32)
    qrow = np.stack(qrows, axis=0).astype(np.float32)
    pmask = np.stack(pmasks, axis=0).astype(np.float32)
    return summat, qrow, pmask


def _hgrn_kernel(q_ref, f_ref, i_ref, g_ref, lb_ref, nw_ref, sm_ref, qrow_ref, pm_ref,
                 o_ref, state_ref, *, rows):
    C = HGRN_CHUNK
    n_levels = qrow_ref.shape[0]

    @pl.when(pl.program_id(1) == 0)
    def _():
        state_ref[...] = jnp.zeros_like(state_ref)

    lb = lb_ref[...]
    summat = sm_ref[...]
    for c in range(rows // C):
        rs = slice(c * C, (c + 1) * C)
        fg = lb + (1.0 - lb) * jax.nn.sigmoid(f_ref[0, rs, :])
        logf = jnp.log(fg)
        hi = logf.astype(BF16)
        rem = logf - hi.astype(F32)
        mid = rem.astype(BF16)
        lo = (rem - mid.astype(F32)).astype(BF16)
        sums = jnp.dot(summat, jnp.concatenate([hi, mid, lo], axis=0),
                       preferred_element_type=F32)
        for h in range(HGRN_HEADS):
            ls = slice(h * HGRN_DIM, (h + 1) * HGRN_DIM)
            qf = jax.nn.silu(q_ref[0, rs, ls].astype(F32))
            kf = 1.0 - fg[:, ls]
            iv = i_ref[0, rs, ls]
            bcum = sums[0:C, ls]
            bsuf = sums[C:2 * C, ls]
            att = _nt_dot(qf.astype(BF16), kf.astype(BF16)) * pm_ref[n_levels]
            for lv in range(n_levels):
                w = jnp.exp(sums[(2 + lv) * C:(3 + lv) * C, ls])
                qrow = qrow_ref[lv]
                z = (jnp.where(qrow > 0.5, qf, kf) * w).astype(BF16)
                att = att + _nt_dot(z, z) * pm_ref[lv]
            o = jnp.dot(att.astype(BF16), iv, preferred_element_type=F32)
            st = state_ref[h]
            o = o + _nt_dot((qf * jnp.exp(bcum)).astype(BF16), st.astype(BF16))
            kdec = (kf * jnp.exp(bsuf)).astype(BF16)
            state_ref[h] = st * jnp.exp(bcum[C - 1:C, :]) + _tn_dot(iv, kdec)
            o = o * lax.rsqrt(jnp.mean(o * o, axis=-1, keepdims=True) + RMS_EPS)
            o = o * nw_ref[:, ls] * jax.nn.silu(g_ref[0, rs, ls].astype(F32))
            o_ref[0, rs, ls] = o.astype(BF16)


def _hgrn(qb, fb, ib, gb, lb, norm_w):
    B, S, W = qb.shape
    rows = min(HGRN_ROWS, S)
    summat, qrow, pmask = _hgrn_constants(HGRN_CHUNK)
    summat = jnp.asarray(np.concatenate([summat] * 3, axis=1), BF16)
    n_levels = qrow.shape[0]
    blk = pl.BlockSpec((1, rows, W), lambda b, s: (b, s, 0))
    vec = pl.BlockSpec((1, W), lambda b, s: (0, 0))
    const = lambda a: pl.BlockSpec(a.shape, lambda b, s: (0,) * a.ndim)
    qrow3 = qrow.reshape(n_levels, HGRN_CHUNK, 1)
    return pl.pallas_call(
        functools.partial(_hgrn_kernel, rows=rows),
        grid=(B, S // rows),
        in_specs=[blk, blk, blk, blk, vec, vec, const(summat), const(qrow3), const(pmask)],
        out_specs=blk,
        out_shape=jax.ShapeDtypeStruct((B, S, W), BF16),
        scratch_shapes=[pltpu.VMEM((HGRN_HEADS, HGRN_DIM, HGRN_DIM), F32)],
        compiler_params=_cparams("parallel", "arbitrary"),
        name="hgrn",
    )(qb, fb, ib, gb, lb, norm_w, jnp.asarray(summat), jnp.asarray(qrow3), jnp.asarray(pmask))


def _mix_route_kernel(ya_ref, yb_ref, ga_ref, gb_ref, x_ref, wa_ref, wb_ref, wo_ref, nw_ref,
                      rw_ref, rb_ref, tri_ref,
                      x1_ref, h2_ref, e_ref, w_ref, rank_ref, cnt_ref, carry_ref):
    @pl.when(pl.program_id(0) == 0)
    def _():
        carry_ref[...] = jnp.zeros_like(carry_ref)

    ua = jnp.dot(ya_ref[...], wa_ref[...], preferred_element_type=F32)
    ub = jnp.dot(yb_ref[...], wb_ref[...], preferred_element_type=F32)
    mixed = (jax.nn.sigmoid(ga_ref[...].astype(F32)) * ua
             + jax.nn.sigmoid(gb_ref[...].astype(F32)) * ub)
    x1 = x_ref[...] + jnp.dot(mixed.astype(BF16), wo_ref[...], preferred_element_type=F32)
    x1_ref[...] = x1
    h2 = x1 * lax.rsqrt(jnp.mean(x1 * x1, axis=-1, keepdims=True) + RMS_EPS) * nw_ref[...]
    half = h2.shape[1] // 2
    h2_ref[...] = _pack_bf16_pair(h2[:, :half], h2[:, half:])

    tm = x1.shape[0]
    h2_hi = h2.astype(BF16)
    h2_lo = (h2 - h2_hi.astype(F32)).astype(BF16)
    logits = _nt_dot(rw_ref[...], jnp.concatenate([h2_hi, h2_lo, h2_hi], axis=1)) + rb_ref[...]
    eid = lax.broadcasted_iota(jnp.int32, (N_EXPERTS, tm), 0)
    work = logits
    es, vs = [], []
    for _ in range(TOP_K):
        mx = jnp.max(work, axis=0, keepdims=True)
        idx = jnp.min(jnp.where(work == mx, eid, N_EXPERTS), axis=0, keepdims=True)
        es.append(idx)
        vs.append(mx)
        work = jnp.where(eid == idx, -jnp.inf, work)
    ex = [jnp.exp(v - vs[0]) for v in vs]
    den = ex[0] + ex[1] + ex[2] + ex[3]
    multi = jnp.zeros((N_EXPERTS, tm), F32)
    for k in range(TOP_K):
        multi = multi + jnp.where(eid == es[k], 1.0, 0.0)
    before = jnp.dot(multi.astype(BF16), tri_ref[...], preferred_element_type=F32) + carry_ref[...]
    for k in range(TOP_K):
        e_ref[k:k + 1, :] = es[k]
        w_ref[k:k + 1, :] = ex[k] / den
        rank_ref[k:k + 1, :] = jnp.sum(jnp.where(eid == es[k], before, 0.0), axis=0,
                                       keepdims=True).astype(jnp.int32)
    carry_ref[...] = carry_ref[...] + jnp.sum(multi, axis=1, keepdims=True)
    cnt_ref[...] = jnp.broadcast_to(carry_ref[...], cnt_ref.shape)


def _mix_route(ya, yb, ga, gb, x2, wa, wb, wo, norm_w, rw_t, rb, row0, rows):
    D = x2.shape[1]
    T = rows
    tm = min(MIX_ROWS, T)
    row = lambda w: pl.BlockSpec((tm, w), lambda i: (i + row0 // tm, 0))
    out_row = pl.BlockSpec((tm, D), lambda i: (i, 0))
    whole = pl.BlockSpec(memory_space=pltpu.VMEM)
    kt = pl.BlockSpec((TOP_K, tm), lambda i: (0, i))
    tri = jnp.asarray(np.triu(np.ones((tm, tm), np.float32), 1), BF16)
    return pl.pallas_call(
        _mix_route_kernel,
        grid=(T // tm,),
        in_specs=[row(ATT_WIDTH), row(HGRN_WIDTH), row(D), row(D), row(D),
                  whole, whole, whole, pl.BlockSpec((1, D), lambda i: (0, 0)),
                  whole, whole, whole],
        out_specs=[out_row, pl.BlockSpec((tm, D // 2), lambda i: (i, 0)), kt, kt, kt,
                   pl.BlockSpec((N_EXPERTS, V7X_LANES), lambda i: (0, 0))],
        out_shape=[jax.ShapeDtypeStruct((T, D), F32), jax.ShapeDtypeStruct((T, D // 2), jnp.uint32),
                   jax.ShapeDtypeStruct((TOP_K, T), jnp.int32),
                   jax.ShapeDtypeStruct((TOP_K, T), F32),
                   jax.ShapeDtypeStruct((TOP_K, T), jnp.int32),
                   jax.ShapeDtypeStruct((N_EXPERTS, V7X_LANES), F32)],
        scratch_shapes=[pltpu.VMEM((N_EXPERTS, 1), F32)],
        compiler_params=_cparams("arbitrary"),
        name="mix_route",
    )(ya, yb, ga, gb, x2, wa, wb, wo, norm_w, rw_t, rb, tri)


def _sc_worker_id():
    return lax.axis_index("s") * V7X_SC_CORES + lax.axis_index("c")


def _sc_kernel(body, out_rows, like, window, name, extra_scratch=()):
    mesh = plsc.VectorSubcoreMesh(core_axis_name="c", subcore_axis_name="s")
    return pl.kernel(
        body, mesh=mesh,
        out_type=jax.ShapeDtypeStruct((out_rows,) + like.shape[1:], like.dtype),
        scratch_types=[pltpu.VMEM((window,), jnp.int32),
                       pltpu.VMEM((window,) + like.shape[1:], like.dtype), *extra_scratch],
        name=name)


def _scatter_rows(dest, h2, out_rows, pad_rows):
    T = dest.shape[1]
    n_workers = V7X_SC_CORES * V7X_SC_SUBCORES
    per_worker = T // n_workers
    window = min(SC_ROWS, per_worker)
    n_pad = pad_rows.shape[1]
    assert pad_rows.shape[0] == n_workers and n_pad % window == 0

    def body(dest_hbm, src_hbm, pad_hbm, zero_hbm, dst_hbm, idx_v, rows_v, pad_v, zero_v, sem):
        wid = _sc_worker_id()
        base = wid * per_worker

        pltpu.sync_copy(pad_hbm.at[pl.ds(wid * n_pad, n_pad)], pad_v)
        pltpu.sync_copy(zero_hbm, zero_v)
        fills = [pltpu.async_copy(zero_v, dst_hbm.at[pad_v.at[pl.ds(c * window, window)]], sem)
                 for c in range(n_pad // window)]

        @pl.loop(0, per_worker // window)
        def _(c):
            t0 = base + c * window
            pltpu.sync_copy(src_hbm.at[pl.ds(t0, window)], rows_v)
            for k in range(TOP_K):
                pltpu.sync_copy(dest_hbm.at[pl.ds(k * T + t0, window)], idx_v)
                pltpu.sync_copy(rows_v, dst_hbm.at[idx_v])

        for f in fills:
            f.wait()

    zeros = jnp.zeros((window,) + h2.shape[1:], h2.dtype)
    extra = [pltpu.VMEM((n_pad,), jnp.int32), pltpu.VMEM(zeros.shape, zeros.dtype),
             pltpu.SemaphoreType.DMA]
    return _sc_kernel(body, out_rows, h2, window, "scatter_rows", extra)(
        dest.reshape(-1), h2, pad_rows.reshape(-1), zeros)


def _gather_rows(dest, yg):
    T = dest.shape[1]
    n_workers = V7X_SC_CORES * V7X_SC_SUBCORES
    per_worker = TOP_K * T // n_workers
    window = min(SC_ROWS, per_worker)

    def body(dest_hbm, src_hbm, dst_hbm, idx_v, rows_v):
        base = _sc_worker_id() * per_worker

        @pl.loop(0, per_worker // window)
        def _(c):
            r0 = base + c * window
            pltpu.sync_copy(dest_hbm.at[pl.ds(r0, window)], idx_v)
            pltpu.sync_copy(src_hbm.at[idx_v], rows_v)
            pltpu.sync_copy(rows_v, dst_hbm.at[pl.ds(r0, window)])

    return _sc_kernel(body, TOP_K * T, yg, window, "gather_rows")(dest.reshape(-1), yg)


def _experts_kernel(blk_ref, exp_ref, lo_ref, hi_ref, slot_ref, efirst_ref, enext_ref,
                    x_ref, wgu_hbm, bg_ref, bu_ref, wd_hbm, bd_ref, perm_ref, y_ref,
                    wgu_stage, wd_stage, wg_buf, wu_buf, wd_buf, sems):
    w = pl.program_id(0)
    lo = lo_ref[w]
    hi = hi_ref[w]
    slot = slot_ref[w]

    tm = x_ref.shape[0]
    n_out = V7X_MXU_DIM

    def weight_copies(e, s):
        return [pltpu.make_async_copy(wgu_hbm.at[e], wgu_stage.at[s], sems.at[s, 0]),
                pltpu.make_async_copy(wd_hbm.at[e], wd_stage.at[s], sems.at[s, 1])]

    @pl.when(w == 0)
    def _():
        for c in weight_copies(exp_ref[0], slot):
            c.start()

    @pl.when(efirst_ref[w] == 1)
    def _():
        for c in weight_copies(exp_ref[w], slot):
            c.wait()

        @pl.when(enext_ref[w] >= 0)
        def _():
            for c in weight_copies(enext_ref[w], 1 - slot):
                c.start()

        half = V7X_MXU_DIM // 2
        for g in range(wgu_stage.shape[2] // V7X_MXU_DIM):
            wb = wgu_stage[slot, :, g * V7X_MXU_DIM:(g + 1) * V7X_MXU_DIM].astype(BF16)
            d = jnp.dot(wb, perm_ref[...], preferred_element_type=F32).astype(BF16)
            wg_buf[:, g * half:(g + 1) * half] = d[:, :half]
            wu_buf[:, g * half:(g + 1) * half] = d[:, half:]
        wd_buf[...] = wd_stage[slot].astype(BF16)

    sb = EXPERT_SUB_ROWS

    def ffn(r0):
        rs = slice(r0, r0 + sb)
        x = jnp.concatenate(_unpack_bf16_pair(x_ref[rs, :]), axis=1).astype(BF16)
        g = jnp.dot(x, wg_buf[...], preferred_element_type=F32) + bg_ref[0]
        u = jnp.dot(x, wu_buf[...], preferred_element_type=F32) + bu_ref[0]
        g = jnp.minimum(g, SWIGLU_LIMIT)
        u = jnp.clip(u, -SWIGLU_LIMIT, SWIGLU_LIMIT)
        act = ((u + 1.0) * (g * jax.nn.sigmoid(g * SWIGLU_ALPHA))).astype(BF16)
        half = y_ref.shape[1]

        def down(c0):
            cs = slice(c0, c0 + n_out)
            return (jnp.dot(act, wd_buf[:, cs], preferred_element_type=F32)
                    + bd_ref[0, :, cs])

        for c in range(half // n_out):
            cs = slice(c * n_out, (c + 1) * n_out)
            y_ref[rs, cs] = _pack_bf16_pair(down(c * n_out), down(half + c * n_out))

    n_sub = tm // sb
    fused = (lo == 0) & (hi == tm)

    @pl.when(fused)
    def _():
        for j in range(n_sub):
            ffn(j * sb)

    for j in range(n_sub):
        @pl.when(jnp.logical_not(fused) & (lo <= j * sb) & (hi > j * sb))
        def _():
            ffn(j * sb)


def _work_items(counts, n_rows, tm):
    nblk = n_rows // tm
    n_items = nblk + N_EXPERTS - 1
    end = jnp.cumsum(counts)
    start = end - counts
    fb = start // tm
    nitems = jnp.where(counts > 0, (end - 1) // tm - fb + 1, 0)
    item_end = jnp.cumsum(nitems)
    item_start = item_end - nitems
    w = jnp.arange(n_items, dtype=jnp.int32)
    valid = w < item_end[-1]
    wc = jnp.minimum(w, item_end[-1] - 1)
    e = jnp.sum(wc[:, None] >= item_end[None, :], axis=1).astype(jnp.int32)
    e = jnp.minimum(e, N_EXPERTS - 1)
    onehot = e[:, None] == jnp.arange(N_EXPERTS, dtype=jnp.int32)[None, :]
    pick = lambda table: jnp.sum(jnp.where(onehot, table[None, :], 0), axis=1)
    blk = (pick(fb) + (wc - pick(item_start))).astype(jnp.int32)
    lo = jnp.maximum(pick(start), blk * tm) - blk * tm
    hi = jnp.minimum(pick(end), (blk + 1) * tm) - blk * tm
    lo = jnp.where(valid, lo, 0).astype(jnp.int32)
    hi = jnp.where(valid, hi, 0).astype(jnp.int32)
    used = counts > 0
    ids = jnp.arange(N_EXPERTS, dtype=jnp.int32)
    slot = pick((jnp.cumsum(used) - 1) % 2).astype(jnp.int32)
    efirst = (valid & (w == pick(item_start))).astype(jnp.int32)
    later = jnp.where(used[None, :] & (ids[None, :] > ids[:, None]), ids[None, :], N_EXPERTS)
    next_used = jnp.min(later, axis=1)
    enext = pick(jnp.where(next_used < N_EXPERTS, next_used, -1)).astype(jnp.int32)
    return blk, e, lo, hi, slot, efirst, enext


def _gate_up_split_matrix():
    i = np.arange(V7X_MXU_DIM)
    src = np.where(i < V7X_MXU_DIM // 2, 2 * i, 2 * (i - V7X_MXU_DIM // 2) + 1)
    perm = np.zeros((V7X_MXU_DIM, V7X_MXU_DIM), np.float32)
    perm[src, i] = 1.0
    return jnp.asarray(perm, BF16)


def _experts(xg, counts, w_gate_up, bg, bu, w_down, bd):
    A = xg.shape[0]
    D = 2 * xg.shape[1]
    tm = min(EXPERT_ROWS, A)
    F = w_down.shape[1]
    items = _work_items(counts, A, tm)
    n_items = A // tm + N_EXPERTS - 1
    xs = pl.BlockSpec((tm, D // 2), lambda w, blk, e, *_: (blk[w], 0))
    bias = lambda c: pl.BlockSpec((1, 1, c), lambda w, blk, e, *_: (e[w], 0, 0))
    hbm = pl.BlockSpec(memory_space=pl.ANY)
    return pl.pallas_call(
        _experts_kernel,
        grid_spec=pltpu.PrefetchScalarGridSpec(
            num_scalar_prefetch=len(items),
            grid=(n_items,),
            in_specs=[xs, hbm, bias(F), bias(F), hbm, bias(D), pl.BlockSpec(memory_space=pltpu.VMEM)],
            out_specs=xs,
            scratch_shapes=[pltpu.VMEM((2, D, 2 * F), F32), pltpu.VMEM((2, F, D), F32),
                            pltpu.VMEM((D, F), BF16), pltpu.VMEM((D, F), BF16),
                            pltpu.VMEM((F, D), BF16), pltpu.SemaphoreType.DMA((2, 2))],
        ),
        out_shape=jax.ShapeDtypeStruct(xg.shape, jnp.uint32),
        compiler_params=_cparams("arbitrary"),
        name="experts",
    )(*items, xg, w_gate_up, bg, bu, w_down, bd, _gate_up_split_matrix())


def _combine_kernel(x1_ref, yk_ref, w_ref, nw_ref, *rest):
    o_ref = rest[-1]
    w = w_ref[...]
    x = x1_ref[...]
    for k in range(TOP_K):
        x = x + jnp.concatenate(_unpack_bf16_pair(yk_ref[k]), axis=1) * w[:, k:k + 1]
    o_ref[...] = x * lax.rsqrt(jnp.mean(x * x, axis=-1, keepdims=True) + RMS_EPS) * nw_ref[...]


def _combine(x1, yk, w_tk, norm_w, out_rows, row0, prev_out):
    Tg, D = x1.shape
    tm = min(COMBINE_ROWS, Tg)
    row = pl.BlockSpec((tm, D), lambda i: (i, 0))
    in_specs = [row, pl.BlockSpec((TOP_K, tm, D // 2), lambda i: (0, i, 0)),
                pl.BlockSpec((tm, TOP_K), lambda i: (i, 0)),
                pl.BlockSpec((1, D), lambda i: (0, 0))]
    args = [x1, yk, w_tk, norm_w]
    aliases = {}
    if prev_out is not None:
        in_specs.append(pl.BlockSpec(memory_space=pl.ANY))
        args.append(prev_out)
        aliases = {len(args) - 1: 0}
    return pl.pallas_call(
        _combine_kernel,
        grid=(Tg // tm,),
        in_specs=in_specs,
        out_specs=pl.BlockSpec((tm, D), lambda i: (i + row0 // tm, 0)),
        out_shape=jax.ShapeDtypeStruct((out_rows, D), F32),
        input_output_aliases=aliases,
        compiler_params=_cparams("parallel"),
        name="combine",
    )(*args)


def _qk_column_order():
    half = ATT_HEAD_DIM // 2
    order = []
    for p in range(ATT_HEADS // 2):
        for part in range(2):
            for h in (2 * p, 2 * p + 1):
                order.extend(range(h * ATT_HEAD_DIM + part * half, h * ATT_HEAD_DIM + (part + 1) * half))
    return np.asarray(order, np.int32)


def _rope_tables(seq):
    half = ATT_HEAD_DIM // 2
    inv = ROPE_THETA ** (-(jnp.arange(half, dtype=F32) * 2.0 / ATT_HEAD_DIM))
    ang = jnp.arange(seq, dtype=F32)[:, None] * inv[None, :]
    cos, sin = jnp.cos(ang), jnp.sin(ang)
    return (jnp.concatenate([cos, cos, cos, cos], axis=1),
            jnp.concatenate([-sin, -sin, sin, sin], axis=1))


def kernel(x, norm1_w, w_in, moba_up, hgrn_lb_logits, hgrn_norm_w, hgrn_up, w_out, norm2_w,
           router_w, router_b, w_gate_up, b_gate_up, w_down, b_down, final_norm_w):
    B, S, D = x.shape
    T = B * S
    assert S % MOBA_BLOCK == 0 and w_in.shape[0] == 1
    x2 = x.reshape(T, D)

    perm = _qk_column_order()
    w0 = w_in[0]
    w_in_p = jnp.concatenate([w0[:, :ATT_WIDTH][:, perm], w0[:, ATT_WIDTH:2 * ATT_WIDTH][:, perm],
                              w0[:, 2 * ATT_WIDTH:]], axis=1).astype(BF16)
    cos_t, sin_t = _rope_tables(S)
    lb = jnp.cumsum(jax.nn.softmax(hgrn_lb_logits.astype(F32), axis=0), axis=0)[0:1]

    qa, ka, va, qb, fb, ib, gb, ga, gtb = _in_proj(x2, norm1_w, w_in_p, cos_t, sin_t, S)
    r3 = lambda a: a.reshape(B, S, a.shape[1])
    ya = _moba(r3(qa), r3(ka), r3(va)).reshape(T, ATT_WIDTH)
    yb = _hgrn(r3(qb), r3(fb), r3(ib), r3(gb), lb, hgrn_norm_w).reshape(T, HGRN_WIDTH)

    rw_t = router_w[0].T
    rw_hi = rw_t.astype(BF16)
    rw_lo = (rw_t - rw_hi.astype(F32)).astype(BF16)
    mix_w = (moba_up[0].astype(BF16), hgrn_up[0].astype(BF16), w_out[0].astype(BF16), norm2_w,
             jnp.concatenate([rw_hi, rw_hi, rw_lo], axis=1), router_b[0][:, None])
    expert_w = (w_gate_up[0], b_gate_up[0][:, None, 0::2], b_gate_up[0][:, None, 1::2],
                w_down[0], b_down[0][:, None, :])

    Tg = T // MOE_TOKEN_GROUPS
    out = None
    for g in range(MOE_TOKEN_GROUPS):
        x1, h2, top_e, top_w, rank, cnt = _mix_route(ya, yb, ga, gtb, x2, *mix_w, g * Tg, Tg)
        counts = cnt[:, 0].astype(jnp.int32)
        sb = EXPERT_SUB_ROWS
        padded = (counts + sb - 1) // sb * sb
        start = jnp.cumsum(padded) - padded
        dest = rank + jnp.sum(jnp.where(top_e[:, :, None] == jnp.arange(N_EXPERTS)[None, None, :],
                                        start[None, None, :], 0), axis=-1)
        body_rows = TOP_K * Tg + N_EXPERTS * sb
        spare_row = body_rows
        i = jnp.arange(sb, dtype=jnp.int32)[None, :]
        n_fill = (padded - counts)[:, None]
        pad_rows = jnp.where(n_fill > 0, (start + counts)[:, None] + i % jnp.maximum(n_fill, 1),
                             spare_row + i)
        xg = _scatter_rows(dest, h2, body_rows + EXPERT_ROWS, pad_rows)
        yg = _experts(xg, padded, *expert_w)
        yk = _gather_rows(dest, yg).reshape(TOP_K, Tg, D // 2)
        out = _combine(x1, yk, top_w.T, final_norm_w[None, :], T, g * Tg, out)
    return out.reshape(B, S, D)
```

```python
import functools
import math

import numpy as np
import jax
import jax.numpy as jnp
from jax import lax
from jax.experimental import pallas as pl
from jax.experimental.pallas import tpu as pltpu
from jax.experimental.pallas import tpu_sc as plsc

ATT_HEADS = 8
ATT_HEAD_DIM = 64
ATT_WIDTH = ATT_HEADS * ATT_HEAD_DIM
MOBA_BLOCK = 256
MOBA_TOPK = 3
ROPE_THETA = 10000.0
HGRN_HEADS = 4
HGRN_DIM = 128
HGRN_WIDTH = HGRN_HEADS * HGRN_DIM
HGRN_CHUNK = 64
N_EXPERTS = 32
TOP_K = 4
SWIGLU_LIMIT = 7.0
SWIGLU_ALPHA = 1.702
RMS_EPS = 1e-6
NEG = -1e30

V7X_LANES = 128
V7X_SUBLANES = 8
V7X_MXU_DIM = 256
V7X_VMEM_LIMIT_BYTES = 56 * 1024 * 1024
V7X_SC_CORES = 2
V7X_SC_SUBCORES = 16

MOBA_PAIRS = 2
MOBA_GROUP = 4
PROJ_ROWS = 1024
PROJ_COLS = 512
HGRN_ROWS = 512
MIX_ROWS = 512
EXPERT_ROWS = 512
EXPERT_SUB_ROWS = 256
MOE_TOKEN_GROUPS = 2
SC_ROWS = 64
COMBINE_ROWS = 512

F32 = jnp.float32
BF16 = jnp.bfloat16


def _nt_dot(a, b, precision=None):
    return lax.dot_general(a, b, (((1,), (1,)), ((), ())), precision=precision,
                           preferred_element_type=F32)


def _tn_dot(a, b, precision=None):
    return lax.dot_general(a, b, (((0,), (0,)), ((), ())), precision=precision,
                           preferred_element_type=F32)


def _cparams(*sem):
    return pltpu.CompilerParams(dimension_semantics=sem, vmem_limit_bytes=V7X_VMEM_LIMIT_BYTES)


def _pack_bf16_pair(lo, hi):
    lo_bits = lax.bitcast_convert_type(lo.astype(BF16).astype(F32), jnp.uint32)
    hi_bits = lax.bitcast_convert_type(hi.astype(BF16).astype(F32), jnp.uint32)
    return (lo_bits >> 16) | hi_bits


def _unpack_bf16_pair(word):
    lo = lax.bitcast_convert_type(word << 16, F32)
    hi = lax.bitcast_convert_type(word & jnp.uint32(0xFFFF0000), F32)
    return lo, hi


def _in_proj_kernel(x_ref, nw_ref, w_ref, cos_ref, sin_ref,
                    qa_ref, ka_ref, va_ref, qb_ref, fb_ref, ib_ref, gb_ref, ga_ref, gtb_ref):
    x = x_ref[...]
    h = x * lax.rsqrt(jnp.mean(x * x, axis=-1, keepdims=True) + RMS_EPS) * nw_ref[...]
    h = h.astype(BF16)
    cos = cos_ref[...]
    sin = sin_ref[...]

    def proj(c):
        return jnp.dot(h, w_ref[:, c * PROJ_COLS:(c + 1) * PROJ_COLS], preferred_element_type=F32)

    def rope(t):
        out = []
        for j in range(PROJ_COLS // V7X_LANES):
            tj = t[:, j * V7X_LANES:(j + 1) * V7X_LANES]
            out.append(tj * cos + pltpu.roll(tj, V7X_LANES // 2, 1) * sin)
        return jnp.concatenate(out, axis=1)

    qa_ref[...] = (rope(proj(0)) * (ATT_HEAD_DIM ** -0.5)).astype(BF16)
    ka_ref[...] = rope(proj(1)).astype(BF16)
    va_ref[...] = proj(2).astype(BF16)
    qb_ref[...] = proj(3).astype(BF16)
    fb_ref[...] = proj(4)
    ib_ref[...] = proj(5).astype(BF16)
    gb_ref[...] = proj(6).astype(BF16)
    ga_ref[:, :PROJ_COLS] = proj(7).astype(BF16)
    ga_ref[:, PROJ_COLS:] = proj(8).astype(BF16)
    gtb_ref[:, :PROJ_COLS] = proj(9).astype(BF16)
    gtb_ref[:, PROJ_COLS:] = proj(10).astype(BF16)


def _in_proj(x2, norm_w, w_in_bf16, cos_t, sin_t, seq):
    T, D = x2.shape
    tm = min(PROJ_ROWS, seq)
    n_seq_tiles = seq // tm
    row = lambda w: pl.BlockSpec((tm, w), lambda i: (i, 0))
    tab = pl.BlockSpec((tm, V7X_LANES), lambda i: (i % n_seq_tiles, 0))
    widths = [ATT_WIDTH] * 3 + [HGRN_WIDTH] * 4 + [D, D]
    dtypes = [BF16, BF16, BF16, BF16, F32, BF16, BF16, BF16, BF16]
    return pl.pallas_call(
        _in_proj_kernel,
        grid=(T // tm,),
        in_specs=[row(D), pl.BlockSpec((1, D), lambda i: (0, 0)),
                  pl.BlockSpec(memory_space=pltpu.VMEM), tab, tab],
        out_specs=[row(w) for w in widths],
        out_shape=[jax.ShapeDtypeStruct((T, w), dt) for w, dt in zip(widths, dtypes)],
        compiler_params=_cparams("parallel"),
        name="in_proj",
    )(x2, norm_w, w_in_bf16, cos_t, sin_t)


def _moba_kernel(q_ref, k_ref, v_ref, o_ref, kaug_ref, kmean_ref, vt_ref, acc_ref, *s_refs, nb):
    qi = pl.program_id(2)
    blk = MOBA_BLOCK
    lanes = V7X_LANES
    n_pairs = kaug_ref.shape[0]
    heads = [(pp, hh) for pp in range(n_pairs) for hh in range(2)]
    nbp = kmean_ref.shape[1]

    @pl.when(qi == 0)
    def _():
        rowb = lax.broadcasted_iota(jnp.int32, (nb * blk, lanes), 0) // blk
        col = lax.broadcasted_iota(jnp.int32, (nb * blk, lanes), 1)
        onehot = jnp.where(rowb == col, 1.0, 0.0).astype(BF16)
        for pp in range(n_pairs):
            pl_ = slice(pp * lanes, (pp + 1) * lanes)
            kaug_ref[pp, :, :lanes] = k_ref[0, :, pl_]
            kaug_ref[pp, :, lanes:] = onehot
            means = [jnp.sum(k_ref[0, n * blk:(n + 1) * blk, pl_].astype(F32), axis=0,
                             keepdims=True) * (1.0 / blk) for n in range(nb)]
            km = jnp.concatenate(means + [jnp.zeros((nbp - nb, lanes), F32)] * (nbp > nb), axis=0)
            hi = km.astype(BF16)
            rem = km - hi.astype(F32)
            mid = rem.astype(BF16)
            lo = (rem - mid.astype(F32)).astype(BF16)
            kmean_ref[pp] = jnp.concatenate([hi, mid, lo], axis=1)
            for c in range(nb):
                vt = v_ref[0, c * blk:(c + 1) * blk, pl_].astype(F32).T
                vt_ref[pp, :lanes, c * blk:(c + 1) * blk] = vt.astype(BF16)
            vt_ref[pp, lanes:, :] = jnp.ones((vt_ref.shape[1] - lanes, nb * blk), BF16)

    feat = lax.broadcasted_iota(jnp.int32, (lanes, blk), 0)
    key_i = lax.broadcasted_iota(jnp.int32, (blk, blk), 0)
    qry_i = lax.broadcasted_iota(jnp.int32, (blk, blk), 1)
    own = pl.multiple_of(qi * blk, blk)
    blk_id = lax.broadcasted_iota(jnp.int32, (nbp, 2 * blk), 0)
    slab = 2 * V7X_SUBLANES

    def slab_max(s):
        return jnp.max(s.reshape(s.shape[0] // slab, slab, blk), axis=0)

    q_aug, m_init = [], []
    for pp in range(n_pairs):
        qt = q_ref[0, :, pp * lanes:(pp + 1) * lanes].astype(F32).T
        k_own = kaug_ref[pp, pl.ds(own, blk), :lanes]
        qhs = [jnp.where((feat // (ATT_HEAD_DIM // 2)) % 2 == hh, qt, 0.0).astype(BF16)
               for hh in range(2)]
        q2 = jnp.concatenate(qhs, axis=1)
        gate = jnp.dot(kmean_ref[pp], jnp.concatenate([q2, q2, q2], axis=0),
                       preferred_element_type=F32)
        gate = jnp.where(blk_id < qi, gate, NEG)
        beaten = jnp.zeros((nbp, 2 * blk), F32)
        for n in range(nb):
            gn = gate[n:n + 1, :]
            wins = (gn > gate) | ((gn == gate) & (blk_id > n))
            beaten = beaten + jnp.where(wins, 1.0, 0.0)
        sel = (beaten < MOBA_TOPK) & (blk_id < qi)
        bias = jnp.where(sel, 0.0, NEG).astype(BF16)
        bias = jnp.concatenate([bias, jnp.zeros((lanes - nbp, 2 * blk), BF16)], axis=0)
        for hh in range(2):
            h = 2 * pp + hh
            q_aug.append(jnp.concatenate([qhs[hh], bias[:, hh * blk:(hh + 1) * blk]], axis=0))
            s = jnp.dot(k_own, qhs[hh], preferred_element_type=F32)
            s = jnp.where(key_i <= qry_i, s, NEG)
            m0 = jnp.max(slab_max(s), axis=0, keepdims=True)
            p = jnp.exp(s - m0).astype(BF16)
            acc_ref[h] = jnp.dot(vt_ref[pp, :, pl.ds(own, blk)], p, preferred_element_type=F32)
            m_init.append(m0)

    group = MOBA_GROUP * blk
    n_groups = (qi + MOBA_GROUP) // MOBA_GROUP

    def score_head(g, h):
        off = pl.multiple_of(g * group, group)
        s = jnp.dot(kaug_ref[h // 2, pl.ds(off, group), :], q_aug[h], preferred_element_type=F32)
        s_refs[h][pl.ds(off, group), :] = s
        return jnp.max(slab_max(s), axis=0, keepdims=True)

    def value_head(g, h, gmax, m_old):
        off = pl.multiple_of(g * group, group)
        m_new = jnp.maximum(m_old, gmax)
        alpha = jnp.exp(m_old - m_new)
        p = jnp.exp(s_refs[h][pl.ds(off, group), :] - m_new).astype(BF16)
        acc_ref[h] = alpha * acc_ref[h] + jnp.dot(vt_ref[h // 2, :, pl.ds(off, group)], p,
                                                  preferred_element_type=F32)
        return m_new

    def pipelined(g, carry):
        gmax, ms = carry
        last = len(heads) - 1
        new_gmax, new_ms = [None] * (last + 1), [None] * (last + 1)
        new_gmax[last] = score_head(g + 1, last)
        for h in range(last + 1):
            new_ms[h] = value_head(g, h, gmax[h], ms[h])
            if h < last:
                new_gmax[h] = score_head(g + 1, h)
        return tuple(new_gmax), tuple(new_ms)

    first = tuple(score_head(0, h) for h in range(len(heads)))
    gmax, ms = lax.fori_loop(0, n_groups - 1, pipelined, (first, tuple(m_init)))
    for h in range(len(heads)):
        value_head(n_groups - 1, h, gmax[h], ms[h])

    half = lanes // 2
    outs = []
    for pp in range(n_pairs):
        outs.append(acc_ref[2 * pp, :half, :] / acc_ref[2 * pp, lanes:lanes + 1, :])
        outs.append(acc_ref[2 * pp + 1, half:lanes, :] / acc_ref[2 * pp + 1, lanes:lanes + 1, :])
    o_ref[0] = jnp.concatenate(outs, axis=0).T.astype(BF16)


def _moba(q, k, v):
    B, S, _ = q.shape
    nb = S // MOBA_BLOCK
    assert nb % MOBA_GROUP == 0
    nbp = -(-nb // (2 * V7X_SUBLANES)) * (2 * V7X_SUBLANES)
    n_pairs = ATT_WIDTH // V7X_LANES
    pp = MOBA_PAIRS
    assert n_pairs % pp == 0
    vt_rows = V7X_LANES + 2 * V7X_SUBLANES
    qspec = pl.BlockSpec((1, MOBA_BLOCK, pp * V7X_LANES), lambda b, p, i: (b, i, p))
    kvspec = pl.BlockSpec((1, S, pp * V7X_LANES), lambda b, p, i: (b, 0, p))
    return pl.pallas_call(
        functools.partial(_moba_kernel, nb=nb),
        grid=(B, n_pairs // pp, nb),
        in_specs=[qspec, kvspec, kvspec],
        out_specs=qspec,
        out_shape=jax.ShapeDtypeStruct((B, S, ATT_WIDTH), BF16),
        scratch_shapes=[pltpu.VMEM((pp, S, 2 * V7X_LANES), BF16),
                        pltpu.VMEM((pp, nbp, 3 * V7X_LANES), BF16),
                        pltpu.VMEM((pp, vt_rows, S), BF16),
                        pltpu.VMEM((2 * pp, vt_rows, MOBA_BLOCK), F32)]
                       + [pltpu.VMEM((S, MOBA_BLOCK), F32)] * (2 * pp),
        compiler_params=_cparams("parallel", "parallel", "arbitrary"),
        name="moba",
    )(q, k, v)


def _hgrn_level_sizes(chunk):
    return [chunk >> (i + 1) for i in range(int(math.log2(chunk)))]


def _hgrn_constants(chunk):
    t = np.arange(chunk)
    mats = [(t[None, :] <= t[:, None]),
            (t[None, :] > t[:, None])]
    qrows, pmasks = [], []
    for bs in _hgrn_level_sizes(chunk):
        blk = t // bs
        odd = (blk % 2) == 1
        lo, hi = blk * bs, (blk + 1) * bs
        u = t[None, :]
        m_odd = (u >= lo[:, None]) & (u <= t[:, None])
        m_even = (u > t[:, None]) & (u < hi[:, None])
        mats.append(np.where(odd[:, None], m_odd, m_even))
        qrows.append(odd)
        pmasks.append(odd[:, None] & (blk[None, :] == blk[:, None] - 1))
    pmasks.append(t[None, :] == t[:, None])
    summat = np.concatenate(mats, axis=0).astype(np.float32)
    qrow = np.stack(qrows, axis=0).astype(np.float32)
    pmask = np.stack(pmasks, axis=0).astype(np.float32)
    return summat, qrow, pmask


def _hgrn_kernel(q_ref, f_ref, i_ref, g_ref, lb_ref, nw_ref, sm_ref, qrow_ref, pm_ref,
                 o_ref, state_ref, *, rows):
    C = HGRN_CHUNK
    n_levels = qrow_ref.shape[0]

    @pl.when(pl.program_id(1) == 0)
    def _():
        state_ref[...] = jnp.zeros_like(state_ref)

    lb = lb_ref[...]
    summat = sm_ref[...]
    for c in range(rows // C):
        rs = slice(c * C, (c + 1) * C)
        fg = lb + (1.0 - lb) * jax.nn.sigmoid(f_ref[0, rs, :])
        logf = jnp.log(fg)
        hi = logf.astype(BF16)
        rem = logf - hi.astype(F32)
        mid = rem.astype(BF16)
        lo = (rem - mid.astype(F32)).astype(BF16)
        sums = jnp.dot(summat, jnp.concatenate([hi, mid, lo], axis=0),
                       preferred_element_type=F32)
        for h in range(HGRN_HEADS):
            ls = slice(h * HGRN_DIM, (h + 1) * HGRN_DIM)
            qf = jax.nn.silu(q_ref[0, rs, ls].astype(F32))
            kf = 1.0 - fg[:, ls]
            iv = i_ref[0, rs, ls]
            bcum = sums[0:C, ls]
            bsuf = sums[C:2 * C, ls]
            att = _nt_dot(qf.astype(BF16), kf.astype(BF16)) * pm_ref[n_levels]
            for lv in range(n_levels):
                w = jnp.exp(sums[(2 + lv) * C:(3 + lv) * C, ls])
                qrow = qrow_ref[lv]
                z = (jnp.where(qrow > 0.5, qf, kf) * w).astype(BF16)
                att = att + _nt_dot(z, z) * pm_ref[lv]
            o = jnp.dot(att.astype(BF16), iv, preferred_element_type=F32)
            st = state_ref[h]
            o = o + _nt_dot((qf * jnp.exp(bcum)).astype(BF16), st.astype(BF16))
            kdec = (kf * jnp.exp(bsuf)).astype(BF16)
            state_ref[h] = st * jnp.exp(bcum[C - 1:C, :]) + _tn_dot(iv, kdec)
            o = o * lax.rsqrt(jnp.mean(o * o, axis=-1, keepdims=True) + RMS_EPS)
            o = o * nw_ref[:, ls] * jax.nn.silu(g_ref[0, rs, ls].astype(F32))
            o_ref[0, rs, ls] = o.astype(BF16)


def _hgrn(qb, fb, ib, gb, lb, norm_w):
    B, S, W = qb.shape
    rows = min(HGRN_ROWS, S)
    summat, qrow, pmask = _hgrn_constants(HGRN_CHUNK)
    summat = jnp.asarray(np.concatenate([summat] * 3, axis=1), BF16)
    n_levels = qrow.shape[0]
    blk = pl.BlockSpec((1, rows, W), lambda b, s: (b, s, 0))
    vec = pl.BlockSpec((1, W), lambda b, s: (0, 0))
    const = lambda a: pl.BlockSpec(a.shape, lambda b, s: (0,) * a.ndim)
    qrow3 = qrow.reshape(n_levels, HGRN_CHUNK, 1)
    return pl.pallas_call(
        functools.partial(_hgrn_kernel, rows=rows),
        grid=(B, S // rows),
        in_specs=[blk, blk, blk, blk, vec, vec, const(summat), const(qrow3), const(pmask)],
        out_specs=blk,
        out_shape=jax.ShapeDtypeStruct((B, S, W), BF16),
        scratch_shapes=[pltpu.VMEM((HGRN_HEADS, HGRN_DIM, HGRN_DIM), F32)],
        compiler_params=_cparams("parallel", "arbitrary"),
        name="hgrn",
    )(qb, fb, ib, gb, lb, norm_w, jnp.asarray(summat), jnp.asarray(qrow3), jnp.asarray(pmask))


def _mix_route_kernel(ya_ref, yb_ref, ga_ref, gb_ref, x_ref, wa_ref, wb_ref, wo_ref, nw_ref,
                      rw_ref, rb_ref, tri_ref,
                      x1_ref, h2_ref, e_ref, w_ref, rank_ref, cnt_ref, carry_ref):
    @pl.when(pl.program_id(0) == 0)
    def _():
        carry_ref[...] = jnp.zeros_like(carry_ref)

    ua = jnp.dot(ya_ref[...], wa_ref[...], preferred_element_type=F32)
    ub = jnp.dot(yb_ref[...], wb_ref[...], preferred_element_type=F32)
    mixed = (jax.nn.sigmoid(ga_ref[...].astype(F32)) * ua
             + jax.nn.sigmoid(gb_ref[...].astype(F32)) * ub)
    x1 = x_ref[...] + jnp.dot(mixed.astype(BF16), wo_ref[...], preferred_element_type=F32)
    x1_ref[...] = x1
    h2 = x1 * lax.rsqrt(jnp.mean(x1 * x1, axis=-1, keepdims=True) + RMS_EPS) * nw_ref[...]
    half = h2.shape[1] // 2
    h2_ref[...] = _pack_bf16_pair(h2[:, :half], h2[:, half:])

    tm = x1.shape[0]
    h2_hi = h2.astype(BF16)
    h2_lo = (h2 - h2_hi.astype(F32)).astype(BF16)
    logits = _nt_dot(rw_ref[...], jnp.concatenate([h2_hi, h2_lo, h2_hi], axis=1)) + rb_ref[...]
    eid = lax.broadcasted_iota(jnp.int32, (N_EXPERTS, tm), 0)
    work = logits
    es, vs = [], []
    for _ in range(TOP_K):
        mx = jnp.max(work, axis=0, keepdims=True)
        idx = jnp.min(jnp.where(work == mx, eid, N_EXPERTS), axis=0, keepdims=True)
        es.append(idx)
        vs.append(mx)
        work = jnp.where(eid == idx, -jnp.inf, work)
    ex = [jnp.exp(v - vs[0]) for v in vs]
    den = ex[0] + ex[1] + ex[2] + ex[3]
    multi = jnp.zeros((N_EXPERTS, tm), F32)
    for k in range(TOP_K):
        multi = multi + jnp.where(eid == es[k], 1.0, 0.0)
    before = jnp.dot(multi.astype(BF16), tri_ref[...], preferred_element_type=F32) + carry_ref[...]
    for k in range(TOP_K):
        e_ref[k:k + 1, :] = es[k]
        w_ref[k:k + 1, :] = ex[k] / den
        rank_ref[k:k + 1, :] = jnp.sum(jnp.where(eid == es[k], before, 0.0), axis=0,
                                       keepdims=True).astype(jnp.int32)
    carry_ref[...] = carry_ref[...] + jnp.sum(multi, axis=1, keepdims=True)
    cnt_ref[...] = jnp.broadcast_to(carry_ref[...], cnt_ref.shape)


def _mix_route(ya, yb, ga, gb, x2, wa, wb, wo, norm_w, rw_t, rb, row0, rows):
    D = x2.shape[1]
    T = rows
    tm = min(MIX_ROWS, T)
    row = lambda w: pl.BlockSpec((tm, w), lambda i: (i + row0 // tm, 0))
    out_row = pl.BlockSpec((tm, D), lambda i: (i, 0))
    whole = pl.BlockSpec(memory_space=pltpu.VMEM)
    kt = pl.BlockSpec((TOP_K, tm), lambda i: (0, i))
    tri = jnp.asarray(np.triu(np.ones((tm, tm), np.float32), 1), BF16)
    return pl.pallas_call(
        _mix_route_kernel,
        grid=(T // tm,),
        in_specs=[row(ATT_WIDTH), row(HGRN_WIDTH), row(D), row(D), row(D),
                  whole, whole, whole, pl.BlockSpec((1, D), lambda i: (0, 0)),
                  whole, whole, whole],
        out_specs=[out_row, pl.BlockSpec((tm, D // 2), lambda i: (i, 0)), kt, kt, kt,
                   pl.BlockSpec((N_EXPERTS, V7X_LANES), lambda i: (0, 0))],
        out_shape=[jax.ShapeDtypeStruct((T, D), F32), jax.ShapeDtypeStruct((T, D // 2), jnp.uint32),
                   jax.ShapeDtypeStruct((TOP_K, T), jnp.int32),
                   jax.ShapeDtypeStruct((TOP_K, T), F32),
                   jax.ShapeDtypeStruct((TOP_K, T), jnp.int32),
                   jax.ShapeDtypeStruct((N_EXPERTS, V7X_LANES), F32)],
        scratch_shapes=[pltpu.VMEM((N_EXPERTS, 1), F32)],
        compiler_params=_cparams("arbitrary"),
        name="mix_route",
    )(ya, yb, ga, gb, x2, wa, wb, wo, norm_w, rw_t, rb, tri)


def _sc_worker_id():
    return lax.axis_index("s") * V7X_SC_CORES + lax.axis_index("c")


def _sc_kernel(body, out_rows, like, window, name, extra_scratch=()):
    mesh = plsc.VectorSubcoreMesh(core_axis_name="c", subcore_axis_name="s")
    return pl.kernel(
        body, mesh=mesh,
        out_type=jax.ShapeDtypeStruct((out_rows,) + like.shape[1:], like.dtype),
        scratch_types=[pltpu.VMEM((window,), jnp.int32),
                       pltpu.VMEM((window,) + like.shape[1:], like.dtype), *extra_scratch],
        name=name)


def _scatter_rows(dest, h2, out_rows, pad_rows):
    T = dest.shape[1]
    n_workers = V7X_SC_CORES * V7X_SC_SUBCORES
    per_worker = T // n_workers
    window = min(SC_ROWS, per_worker)
    n_pad = pad_rows.shape[1]
    assert pad_rows.shape[0] == n_workers and n_pad % window == 0

    def body(dest_hbm, src_hbm, pad_hbm, zero_hbm, dst_hbm, idx_v, rows_v, pad_v, zero_v, all_v, sem,
             fill_sem):
        del idx_v
        wid = _sc_worker_id()
        base = wid * per_worker

        pltpu.sync_copy(pad_hbm.at[pl.ds(wid * n_pad, n_pad)], pad_v)
        pltpu.sync_copy(zero_hbm, zero_v)
        fills = [pltpu.async_copy(zero_v, dst_hbm.at[pad_v.at[pl.ds(c * window, window)]],
                                     fill_sem)
                 for c in range(n_pad // window)]

        for k in range(TOP_K):
            pltpu.sync_copy(dest_hbm.at[pl.ds(k * T + base, per_worker)],
                            all_v.at[pl.ds(k * per_worker, per_worker)])

        @pl.loop(0, per_worker // window)
        def _(c):
            pltpu.sync_copy(src_hbm.at[pl.ds(base + c * window, window)], rows_v)
            copies = [pltpu.async_copy(
                rows_v, dst_hbm.at[all_v.at[pl.ds(k * per_worker + c * window, window)]], sem)
                for k in range(TOP_K)]
            for cp in copies:
                cp.wait()

        for f in fills:
            f.wait()

    zeros = jnp.zeros((window,) + h2.shape[1:], h2.dtype)
    extra = [pltpu.VMEM((n_pad,), jnp.int32), pltpu.VMEM(zeros.shape, zeros.dtype),
             pltpu.VMEM((TOP_K * per_worker,), jnp.int32), pltpu.SemaphoreType.DMA,
             pltpu.SemaphoreType.DMA]
    return _sc_kernel(body, out_rows, h2, window, "scatter_rows", extra)(
        dest.reshape(-1), h2, pad_rows.reshape(-1), zeros)


def _gather_rows(dest, yg):
    T = dest.shape[1]
    n_workers = V7X_SC_CORES * V7X_SC_SUBCORES
    per_worker = TOP_K * T // n_workers
    window = min(SC_ROWS, per_worker)

    def body(dest_hbm, src_hbm, dst_hbm, idx_v, rows_v, all_v):
        del idx_v
        base = _sc_worker_id() * per_worker
        pltpu.sync_copy(dest_hbm.at[pl.ds(base, per_worker)], all_v)

        @pl.loop(0, per_worker // window)
        def _(c):
            pltpu.sync_copy(src_hbm.at[all_v.at[pl.ds(c * window, window)]], rows_v)
            pltpu.sync_copy(rows_v, dst_hbm.at[pl.ds(base + c * window, window)])

    extra = [pltpu.VMEM((per_worker,), jnp.int32)]
    return _sc_kernel(body, TOP_K * T, yg, window, "gather_rows", extra)(dest.reshape(-1), yg)


def _experts_kernel(blk_ref, exp_ref, lo_ref, hi_ref, slot_ref, efirst_ref, enext_ref,
                    x_ref, wgu_hbm, bg_ref, bu_ref, wd_hbm, bd_ref, perm_ref, y_ref,
                    wgu_stage, wd_stage, wg_buf, wu_buf, wd_buf, sems):
    w = pl.program_id(0)
    lo = lo_ref[w]
    hi = hi_ref[w]
    slot = slot_ref[w]

    tm = x_ref.shape[0]
    n_out = V7X_MXU_DIM

    def weight_copies(e, s):
        return [pltpu.make_async_copy(wgu_hbm.at[e], wgu_stage.at[s], sems.at[s, 0]),
                pltpu.make_async_copy(wd_hbm.at[e], wd_stage.at[s], sems.at[s, 1])]

    @pl.when(w == 0)
    def _():
        for c in weight_copies(exp_ref[0], slot):
            c.start()

    @pl.when(efirst_ref[w] == 1)
    def _():
        for c in weight_copies(exp_ref[w], slot):
            c.wait()

        @pl.when(enext_ref[w] >= 0)
        def _():
            for c in weight_copies(enext_ref[w], 1 - slot):
                c.start()

        half = V7X_MXU_DIM // 2
        for g in range(wgu_stage.shape[2] // V7X_MXU_DIM):
            wb = wgu_stage[slot, :, g * V7X_MXU_DIM:(g + 1) * V7X_MXU_DIM].astype(BF16)
            d = jnp.dot(wb, perm_ref[...], preferred_element_type=F32).astype(BF16)
            wg_buf[:, g * half:(g + 1) * half] = d[:, :half]
            wu_buf[:, g * half:(g + 1) * half] = d[:, half:]
        wd_buf[...] = wd_stage[slot].astype(BF16)

    sb = EXPERT_SUB_ROWS

    def ffn(r0):
        rs = slice(r0, r0 + sb)
        x = jnp.concatenate(_unpack_bf16_pair(x_ref[rs, :]), axis=1).astype(BF16)
        g = jnp.dot(x, wg_buf[...], preferred_element_type=F32) + bg_ref[0]
        u = jnp.dot(x, wu_buf[...], preferred_element_type=F32) + bu_ref[0]
        g = jnp.minimum(g, SWIGLU_LIMIT)
        u = jnp.clip(u, -SWIGLU_LIMIT, SWIGLU_LIMIT)
        act = ((u + 1.0) * (g * jax.nn.sigmoid(g * SWIGLU_ALPHA))).astype(BF16)
        half = y_ref.shape[1]

        def down(c0):
            cs = slice(c0, c0 + n_out)
            return (jnp.dot(act, wd_buf[:, cs], preferred_element_type=F32)
                    + bd_ref[0, :, cs])

        for c in range(half // n_out):
            cs = slice(c * n_out, (c + 1) * n_out)
            y_ref[rs, cs] = _pack_bf16_pair(down(c * n_out), down(half + c * n_out))

    n_sub = tm // sb
    fused = (lo == 0) & (hi == tm)

    @pl.when(fused)
    def _():
        for j in range(n_sub):
            ffn(j * sb)

    for j in range(n_sub):
        @pl.when(jnp.logical_not(fused) & (lo <= j * sb) & (hi > j * sb))
        def _():
            ffn(j * sb)


def _work_items(counts, n_rows, tm):
    nblk = n_rows // tm
    n_items = nblk + N_EXPERTS - 1
    end = jnp.cumsum(counts)
    start = end - counts
    fb = start // tm
    nitems = jnp.where(counts > 0, (end - 1) // tm - fb + 1, 0)
    item_end = jnp.cumsum(nitems)
    item_start = item_end - nitems
    w = jnp.arange(n_items, dtype=jnp.int32)
    valid = w < item_end[-1]
    wc = jnp.minimum(w, item_end[-1] - 1)
    e = jnp.sum(wc[:, None] >= item_end[None, :], axis=1).astype(jnp.int32)
    e = jnp.minimum(e, N_EXPERTS - 1)
    onehot = e[:, None] == jnp.arange(N_EXPERTS, dtype=jnp.int32)[None, :]
    pick = lambda table: jnp.sum(jnp.where(onehot, table[None, :], 0), axis=1)
    blk = (pick(fb) + (wc - pick(item_start))).astype(jnp.int32)
    lo = jnp.maximum(pick(start), blk * tm) - blk * tm
    hi = jnp.minimum(pick(end), (blk + 1) * tm) - blk * tm
    lo = jnp.where(valid, lo, 0).astype(jnp.int32)
    hi = jnp.where(valid, hi, 0).astype(jnp.int32)
    used = counts > 0
    ids = jnp.arange(N_EXPERTS, dtype=jnp.int32)
    slot = pick((jnp.cumsum(used) - 1) % 2).astype(jnp.int32)
    efirst = (valid & (w == pick(item_start))).astype(jnp.int32)
    later = jnp.where(used[None, :] & (ids[None, :] > ids[:, None]), ids[None, :], N_EXPERTS)
    next_used = jnp.min(later, axis=1)
    enext = pick(jnp.where(next_used < N_EXPERTS, next_used, -1)).astype(jnp.int32)
    return blk, e, lo, hi, slot, efirst, enext


def _gate_up_split_matrix():
    i = np.arange(V7X_MXU_DIM)
    src = np.where(i < V7X_MXU_DIM // 2, 2 * i, 2 * (i - V7X_MXU_DIM // 2) + 1)
    perm = np.zeros((V7X_MXU_DIM, V7X_MXU_DIM), np.float32)
    perm[src, i] = 1.0
    return jnp.asarray(perm, BF16)


def _experts(xg, counts, w_gate_up, bg, bu, w_down, bd):
    A = xg.shape[0]
    D = 2 * xg.shape[1]
    tm = min(EXPERT_ROWS, A)
    F = w_down.shape[1]
    items = _work_items(counts, A, tm)
    n_items = A // tm + N_EXPERTS - 1
    xs = pl.BlockSpec((tm, D // 2), lambda w, blk, e, *_: (blk[w], 0))
    bias = lambda c: pl.BlockSpec((1, 1, c), lambda w, blk, e, *_: (e[w], 0, 0))
    hbm = pl.BlockSpec(memory_space=pl.ANY)
    return pl.pallas_call(
        _experts_kernel,
        grid_spec=pltpu.PrefetchScalarGridSpec(
            num_scalar_prefetch=len(items),
            grid=(n_items,),
            in_specs=[xs, hbm, bias(F), bias(F), hbm, bias(D), pl.BlockSpec(memory_space=pltpu.VMEM)],
            out_specs=xs,
            scratch_shapes=[pltpu.VMEM((2, D, 2 * F), F32), pltpu.VMEM((2, F, D), F32),
                            pltpu.VMEM((D, F), BF16), pltpu.VMEM((D, F), BF16),
                            pltpu.VMEM((F, D), BF16), pltpu.SemaphoreType.DMA((2, 2))],
        ),
        out_shape=jax.ShapeDtypeStruct(xg.shape, jnp.uint32),
        compiler_params=_cparams("arbitrary"),
        name="experts",
    )(*items, xg, w_gate_up, bg, bu, w_down, bd, _gate_up_split_matrix())


def _combine_kernel(x1_ref, yk_ref, w_ref, nw_ref, *rest):
    o_ref = rest[-1]
    w = w_ref[...]
    x = x1_ref[...]
    for k in range(TOP_K):
        x = x + jnp.concatenate(_unpack_bf16_pair(yk_ref[k]), axis=1) * w[:, k:k + 1]
    o_ref[...] = x * lax.rsqrt(jnp.mean(x * x, axis=-1, keepdims=True) + RMS_EPS) * nw_ref[...]


def _combine(x1, yk, w_tk, norm_w, out_rows, row0, prev_out):
    Tg, D = x1.shape
    tm = min(COMBINE_ROWS, Tg)
    row = pl.BlockSpec((tm, D), lambda i: (i, 0))
    in_specs = [row, pl.BlockSpec((TOP_K, tm, D // 2), lambda i: (0, i, 0)),
                pl.BlockSpec((tm, TOP_K), lambda i: (i, 0)),
                pl.BlockSpec((1, D), lambda i: (0, 0))]
    args = [x1, yk, w_tk, norm_w]
    aliases = {}
    if prev_out is not None:
        in_specs.append(pl.BlockSpec(memory_space=pl.ANY))
        args.append(prev_out)
        aliases = {len(args) - 1: 0}
    return pl.pallas_call(
        _combine_kernel,
        grid=(Tg // tm,),
        in_specs=in_specs,
        out_specs=pl.BlockSpec((tm, D), lambda i: (i + row0 // tm, 0)),
        out_shape=jax.ShapeDtypeStruct((out_rows, D), F32),
        input_output_aliases=aliases,
        compiler_params=_cparams("parallel"),
        name="combine",
    )(*args)


def _qk_column_order():
    half = ATT_HEAD_DIM // 2
    order = []
    for p in range(ATT_HEADS // 2):
        for part in range(2):
            for h in (2 * p, 2 * p + 1):
                order.extend(range(h * ATT_HEAD_DIM + part * half, h * ATT_HEAD_DIM + (part + 1) * half))
    return np.asarray(order, np.int32)


def _rope_tables(seq):
    half = ATT_HEAD_DIM // 2
    inv = ROPE_THETA ** (-(jnp.arange(half, dtype=F32) * 2.0 / ATT_HEAD_DIM))
    ang = jnp.arange(seq, dtype=F32)[:, None] * inv[None, :]
    cos, sin = jnp.cos(ang), jnp.sin(ang)
    return (jnp.concatenate([cos, cos, cos, cos], axis=1),
            jnp.concatenate([-sin, -sin, sin, sin], axis=1))


def kernel(x, norm1_w, w_in, moba_up, hgrn_lb_logits, hgrn_norm_w, hgrn_up, w_out, norm2_w,
           router_w, router_b, w_gate_up, b_gate_up, w_down, b_down, final_norm_w):
    B, S, D = x.shape
    T = B * S
    assert S % MOBA_BLOCK == 0 and w_in.shape[0] == 1
    x2 = x.reshape(T, D)

    perm = _qk_column_order()
    w0 = w_in[0]
    w_in_p = jnp.concatenate([w0[:, :ATT_WIDTH][:, perm], w0[:, ATT_WIDTH:2 * ATT_WIDTH][:, perm],
                              w0[:, 2 * ATT_WIDTH:]], axis=1).astype(BF16)
    cos_t, sin_t = _rope_tables(S)
    lb = jnp.cumsum(jax.nn.softmax(hgrn_lb_logits.astype(F32), axis=0), axis=0)[0:1]

    qa, ka, va, qb, fb, ib, gb, ga, gtb = _in_proj(x2, norm1_w, w_in_p, cos_t, sin_t, S)
    r3 = lambda a: a.reshape(B, S, a.shape[1])
    ya = _moba(r3(qa), r3(ka), r3(va)).reshape(T, ATT_WIDTH)
    yb = _hgrn(r3(qb), r3(fb), r3(ib), r3(gb), lb, hgrn_norm_w).reshape(T, HGRN_WIDTH)

    rw_t = router_w[0].T
    rw_hi = rw_t.astype(BF16)
    rw_lo = (rw_t - rw_hi.astype(F32)).astype(BF16)
    mix_w = (moba_up[0].astype(BF16), hgrn_up[0].astype(BF16), w_out[0].astype(BF16), norm2_w,
             jnp.concatenate([rw_hi, rw_hi, rw_lo], axis=1), router_b[0][:, None])
    expert_w = (w_gate_up[0], b_gate_up[0][:, None, 0::2], b_gate_up[0][:, None, 1::2],
                w_down[0], b_down[0][:, None, :])

    Tg = T // MOE_TOKEN_GROUPS
    out = None
    for g in range(MOE_TOKEN_GROUPS):
        x1, h2, top_e, top_w, rank, cnt = _mix_route(ya, yb, ga, gtb, x2, *mix_w, g * Tg, Tg)
        counts = cnt[:, 0].astype(jnp.int32)
        sb = EXPERT_SUB_ROWS
        padded = (counts + sb - 1) // sb * sb
        start = jnp.cumsum(padded) - padded
        dest = rank + jnp.sum(jnp.where(top_e[:, :, None] == jnp.arange(N_EXPERTS)[None, None, :],
                                        start[None, None, :], 0), axis=-1)
        body_rows = TOP_K * Tg + N_EXPERTS * sb
        spare_row = body_rows
        i = jnp.arange(sb, dtype=jnp.int32)[None, :]
        n_fill = (padded - counts)[:, None]
        pad_rows = jnp.where(n_fill > 0, (start + counts)[:, None] + i % jnp.maximum(n_fill, 1),
                             spare_row + i)
        xg = _scatter_rows(dest, h2, body_rows + EXPERT_ROWS, pad_rows)
        yg = _experts(xg, padded, *expert_w)
        yk = _gather_rows(dest, yg).reshape(TOP_K, Tg, D // 2)
        out = _combine(x1, yk, top_w.T, final_norm_w[None, :], T, g * Tg, out)
    return out.reshape(B, S, D)
```

```python
import functools
import math

import numpy as np
import jax
import jax.numpy as jnp
from jax import lax
from jax.experimental import pallas as pl
from jax.experimental.pallas import tpu as pltpu
from jax.experimental.pallas import tpu_sc as plsc

ATT_HEADS = 8
ATT_HEAD_DIM = 64
ATT_WIDTH = ATT_HEADS * ATT_HEAD_DIM
MOBA_BLOCK = 256
MOBA_TOPK = 3
ROPE_THETA = 10000.0
HGRN_HEADS = 4
HGRN_DIM = 128
HGRN_WIDTH = HGRN_HEADS * HGRN_DIM
HGRN_CHUNK = 64
N_EXPERTS = 32
TOP_K = 4
SWIGLU_LIMIT = 7.0
SWIGLU_ALPHA = 1.702
RMS_EPS = 1e-6
NEG = -1e30

V7X_LANES = 128
V7X_SUBLANES = 8
V7X_MXU_DIM = 256
V7X_VMEM_LIMIT_BYTES = 56 * 1024 * 1024
V7X_SC_CORES = 2
V7X_SC_SUBCORES = 16

MOBA_PAIRS = 2
MOBA_GROUP = 4
PROJ_ROWS = 1024
PROJ_COLS = 512
HGRN_ROWS = 512
MIX_ROWS = 512
EXPERT_ROWS = 512
EXPERT_SUB_ROWS = 256
MOE_TOKEN_GROUPS = 2
SC_ROWS = 64
COMBINE_ROWS = 512

F32 = jnp.float32
BF16 = jnp.bfloat16


def _nt_dot(a, b, precision=None):
    return lax.dot_general(a, b, (((1,), (1,)), ((), ())), precision=precision,
                           preferred_element_type=F32)


def _tn_dot(a, b, precision=None):
    return lax.dot_general(a, b, (((0,), (0,)), ((), ())), precision=precision,
                           preferred_element_type=F32)


def _cparams(*sem):
    return pltpu.CompilerParams(dimension_semantics=sem, vmem_limit_bytes=V7X_VMEM_LIMIT_BYTES)


def _pack_bf16_pair(lo, hi):
    lo_bits = lax.bitcast_convert_type(lo.astype(BF16).astype(F32), jnp.uint32)
    hi_bits = lax.bitcast_convert_type(hi.astype(BF16).astype(F32), jnp.uint32)
    return (lo_bits >> 16) | hi_bits


def _unpack_bf16_pair(word):
    lo = lax.bitcast_convert_type(word << 16, F32)
    hi = lax.bitcast_convert_type(word & jnp.uint32(0xFFFF0000), F32)
    return lo, hi


def _in_proj_kernel(x_ref, nw_ref, w_ref, cos_ref, sin_ref,
                    qa_ref, ka_ref, va_ref, qb_ref, fb_ref, ib_ref, gb_ref, ga_ref, gtb_ref):
    x = x_ref[...]
    h = x * lax.rsqrt(jnp.mean(x * x, axis=-1, keepdims=True) + RMS_EPS) * nw_ref[...]
    h = h.astype(BF16)
    cos = cos_ref[...]
    sin = sin_ref[...]

    def proj(c):
        return jnp.dot(h, w_ref[:, c * PROJ_COLS:(c + 1) * PROJ_COLS], preferred_element_type=F32)

    def rope(t):
        out = []
        for j in range(PROJ_COLS // V7X_LANES):
            tj = t[:, j * V7X_LANES:(j + 1) * V7X_LANES]
            out.append(tj * cos + pltpu.roll(tj, V7X_LANES // 2, 1) * sin)
        return jnp.concatenate(out, axis=1)

    qa_ref[...] = (rope(proj(0)) * (ATT_HEAD_DIM ** -0.5)).astype(BF16)
    ka_ref[...] = rope(proj(1)).astype(BF16)
    va_ref[...] = proj(2).astype(BF16)
    qb_ref[...] = proj(3).astype(BF16)
    fb_ref[...] = proj(4)
    ib_ref[...] = proj(5).astype(BF16)
    gb_ref[...] = proj(6).astype(BF16)
    ga_ref[:, :PROJ_COLS] = proj(7).astype(BF16)
    ga_ref[:, PROJ_COLS:] = proj(8).astype(BF16)
    gtb_ref[:, :PROJ_COLS] = proj(9).astype(BF16)
    gtb_ref[:, PROJ_COLS:] = proj(10).astype(BF16)


def _in_proj(x2, norm_w, w_in_bf16, cos_t, sin_t, seq):
    T, D = x2.shape
    tm = min(PROJ_ROWS, seq)
    n_seq_tiles = seq // tm
    row = lambda w: pl.BlockSpec((tm, w), lambda i: (i, 0))
    tab = pl.BlockSpec((tm, V7X_LANES), lambda i: (i % n_seq_tiles, 0))
    widths = [ATT_WIDTH] * 3 + [HGRN_WIDTH] * 4 + [D, D]
    dtypes = [BF16, BF16, BF16, BF16, F32, BF16, BF16, BF16, BF16]
    return pl.pallas_call(
        _in_proj_kernel,
        grid=(T // tm,),
        in_specs=[row(D), pl.BlockSpec((1, D), lambda i: (0, 0)),
                  pl.BlockSpec(memory_space=pltpu.VMEM), tab, tab],
        out_specs=[row(w) for w in widths],
        out_shape=[jax.ShapeDtypeStruct((T, w), dt) for w, dt in zip(widths, dtypes)],
        compiler_params=_cparams("parallel"),
        name="in_proj",
    )(x2, norm_w, w_in_bf16, cos_t, sin_t)


def _moba_kernel(q_ref, k_ref, v_ref, o_ref, kaug_ref, kmean_ref, vt_ref, acc_ref, *s_refs, nb):
    qi = pl.program_id(2)
    blk = MOBA_BLOCK
    lanes = V7X_LANES
    n_pairs = kaug_ref.shape[0]
    heads = [(pp, hh) for pp in range(n_pairs) for hh in range(2)]
    nbp = kmean_ref.shape[1]

    @pl.when(qi == 0)
    def _():
        rowb = lax.broadcasted_iota(jnp.int32, (nb * blk, lanes), 0) // blk
        col = lax.broadcasted_iota(jnp.int32, (nb * blk, lanes), 1)
        onehot = jnp.where(rowb == col, 1.0, 0.0).astype(BF16)
        for pp in range(n_pairs):
            pl_ = slice(pp * lanes, (pp + 1) * lanes)
            kaug_ref[pp, :, :lanes] = k_ref[0, :, pl_]
            kaug_ref[pp, :, lanes:] = onehot
            means = [jnp.sum(k_ref[0, n * blk:(n + 1) * blk, pl_].astype(F32), axis=0,
                             keepdims=True) * (1.0 / blk) for n in range(nb)]
            km = jnp.concatenate(means + [jnp.zeros((nbp - nb, lanes), F32)] * (nbp > nb), axis=0)
            hi = km.astype(BF16)
            rem = km - hi.astype(F32)
            mid = rem.astype(BF16)
            lo = (rem - mid.astype(F32)).astype(BF16)
            kmean_ref[pp] = jnp.concatenate([hi, mid, lo], axis=1)
            for c in range(nb):
                vt = v_ref[0, c * blk:(c + 1) * blk, pl_].astype(F32).T
                vt_ref[pp, :lanes, c * blk:(c + 1) * blk] = vt.astype(BF16)
            vt_ref[pp, lanes:, :] = jnp.ones((vt_ref.shape[1] - lanes, nb * blk), BF16)

    feat = lax.broadcasted_iota(jnp.int32, (lanes, blk), 0)
    key_i = lax.broadcasted_iota(jnp.int32, (blk, blk), 0)
    qry_i = lax.broadcasted_iota(jnp.int32, (blk, blk), 1)
    own = pl.multiple_of(qi * blk, blk)
    blk_id = lax.broadcasted_iota(jnp.int32, (nbp, 2 * blk), 0)
    slab = 2 * V7X_SUBLANES

    def slab_max(s):
        return jnp.max(s.reshape(s.shape[0] // slab, slab, blk), axis=0)

    q_aug, m_init = [], []
    for pp in range(n_pairs):
        qt = q_ref[0, :, pp * lanes:(pp + 1) * lanes].astype(F32).T
        k_own = kaug_ref[pp, pl.ds(own, blk), :lanes]
        qhs = [jnp.where((feat // (ATT_HEAD_DIM // 2)) % 2 == hh, qt, 0.0).astype(BF16)
               for hh in range(2)]
        q2 = jnp.concatenate(qhs, axis=1)
        gate = jnp.dot(kmean_ref[pp], jnp.concatenate([q2, q2, q2], axis=0),
                       preferred_element_type=F32)
        gate = jnp.where(blk_id < qi, gate, NEG)
        beaten = jnp.zeros((nbp, 2 * blk), F32)
        for n in range(nb):
            gn = gate[n:n + 1, :]
            wins = (gn > gate) | ((gn == gate) & (blk_id > n))
            beaten = beaten + jnp.where(wins, 1.0, 0.0)
        sel = (beaten < MOBA_TOPK) & (blk_id < qi)
        bias = jnp.where(sel, 0.0, NEG).astype(BF16)
        bias = jnp.concatenate([bias, jnp.zeros((lanes - nbp, 2 * blk), BF16)], axis=0)
        for hh in range(2):
            h = 2 * pp + hh
            q_aug.append(jnp.concatenate([qhs[hh], bias[:, hh * blk:(hh + 1) * blk]], axis=0))
            s = jnp.dot(k_own, qhs[hh], preferred_element_type=F32)
            s = jnp.where(key_i <= qry_i, s, NEG)
            m0 = jnp.max(slab_max(s), axis=0, keepdims=True)
            p = jnp.exp(s - m0).astype(BF16)
            acc_ref[h] = jnp.dot(vt_ref[pp, :, pl.ds(own, blk)], p, preferred_element_type=F32)
            m_init.append(m0)

    group = MOBA_GROUP * blk
    n_groups = (qi + MOBA_GROUP) // MOBA_GROUP

    def score_head(g, h):
        off = pl.multiple_of(g * group, group)
        s = jnp.dot(kaug_ref[h // 2, pl.ds(off, group), :], q_aug[h], preferred_element_type=F32)
        s_refs[h][pl.ds(off, group), :] = s
        return jnp.max(slab_max(s), axis=0, keepdims=True)

    def value_head(g, h, gmax, m_old):
        off = pl.multiple_of(g * group, group)
        m_new = jnp.maximum(m_old, gmax)
        alpha = jnp.exp(m_old - m_new)
        p = jnp.exp(s_refs[h][pl.ds(off, group), :] - m_new).astype(BF16)
        acc_ref[h] = alpha * acc_ref[h] + jnp.dot(vt_ref[h // 2, :, pl.ds(off, group)], p,
                                                  preferred_element_type=F32)
        return m_new

    def pipelined(g, carry):
        gmax, ms = carry
        last = len(heads) - 1
        new_gmax, new_ms = [None] * (last + 1), [None] * (last + 1)
        new_gmax[last] = score_head(g + 1, last)
        for h in range(last + 1):
            new_ms[h] = value_head(g, h, gmax[h], ms[h])
            if h < last:
                new_gmax[h] = score_head(g + 1, h)
        return tuple(new_gmax), tuple(new_ms)

    first = tuple(score_head(0, h) for h in range(len(heads)))
    gmax, ms = lax.fori_loop(0, n_groups - 1, pipelined, (first, tuple(m_init)))
    for h in range(len(heads)):
        value_head(n_groups - 1, h, gmax[h], ms[h])

    half = lanes // 2
    outs = []
    for pp in range(n_pairs):
        outs.append(acc_ref[2 * pp, :half, :] / acc_ref[2 * pp, lanes:lanes + 1, :])
        outs.append(acc_ref[2 * pp + 1, half:lanes, :] / acc_ref[2 * pp + 1, lanes:lanes + 1, :])
    o_ref[0] = jnp.concatenate(outs, axis=0).T.astype(BF16)


def _moba(q, k, v, b0, n_batch):
    S = q.shape[1]
    nb = S // MOBA_BLOCK
    assert nb % MOBA_GROUP == 0
    nbp = -(-nb // (2 * V7X_SUBLANES)) * (2 * V7X_SUBLANES)
    n_pairs = ATT_WIDTH // V7X_LANES
    pp = MOBA_PAIRS
    assert n_pairs % pp == 0
    vt_rows = V7X_LANES + 2 * V7X_SUBLANES
    qspec = pl.BlockSpec((1, MOBA_BLOCK, pp * V7X_LANES), lambda b, p, i: (b + b0, i, p))
    kvspec = pl.BlockSpec((1, S, pp * V7X_LANES), lambda b, p, i: (b + b0, 0, p))
    return pl.pallas_call(
        functools.partial(_moba_kernel, nb=nb),
        grid=(n_batch, n_pairs // pp, nb),
        in_specs=[qspec, kvspec, kvspec],
        out_specs=pl.BlockSpec((1, MOBA_BLOCK, pp * V7X_LANES), lambda b, p, i: (b, i, p)),
        out_shape=jax.ShapeDtypeStruct((n_batch, S, ATT_WIDTH), BF16),
        scratch_shapes=[pltpu.VMEM((pp, S, 2 * V7X_LANES), BF16),
                        pltpu.VMEM((pp, nbp, 3 * V7X_LANES), BF16),
                        pltpu.VMEM((pp, vt_rows, S), BF16),
                        pltpu.VMEM((2 * pp, vt_rows, MOBA_BLOCK), F32)]
                       + [pltpu.VMEM((S, MOBA_BLOCK), F32)] * (2 * pp),
        compiler_params=_cparams("parallel", "parallel", "arbitrary"),
        name="moba",
    )(q, k, v)


def _hgrn_level_sizes(chunk):
    return [chunk >> (i + 1) for i in range(int(math.log2(chunk)))]


def _hgrn_constants(chunk):
    t = np.arange(chunk)
    mats = [(t[None, :] <= t[:, None]),
            (t[None, :] > t[:, None])]
    qrows, pmasks = [], []
    for bs in _hgrn_level_sizes(chunk):
        blk = t // bs
        odd = (blk % 2) == 1
        lo, hi = blk * bs, (blk + 1) * bs
        u = t[None, :]
        m_odd = (u >= lo[:, None]) & (u <= t[:, None])
        m_even = (u > t[:, None]) & (u < hi[:, None])
        mats.append(np.where(odd[:, None], m_odd, m_even))
        qrows.append(odd)
        pmasks.append(odd[:, None] & (blk[None, :] == blk[:, None] - 1))
    pmasks.append(t[None, :] == t[:, None])
    summat = np.concatenate(mats, axis=0).astype(np.float32)
    qrow = np.stack(qrows, axis=0).astype(np.float32)
    pmask = np.stack(pmasks, axis=0).astype(np.float32)
    return summat, qrow, pmask


def _hgrn_kernel(q_ref, f_ref, i_ref, g_ref, lb_ref, nw_ref, sm_ref, qrow_ref, pm_ref,
                 o_ref, state_ref, *, rows):
    C = HGRN_CHUNK
    n_levels = qrow_ref.shape[0]

    @pl.when(pl.program_id(1) == 0)
    def _():
        state_ref[...] = jnp.zeros_like(state_ref)

    lb = lb_ref[...]
    summat = sm_ref[...]
    for c in range(rows // C):
        rs = slice(c * C, (c + 1) * C)
        fg = lb + (1.0 - lb) * jax.nn.sigmoid(f_ref[0, rs, :])
        logf = jnp.log(fg)
        hi = logf.astype(BF16)
        rem = logf - hi.astype(F32)
        mid = rem.astype(BF16)
        lo = (rem - mid.astype(F32)).astype(BF16)
        sums = jnp.dot(summat, jnp.concatenate([hi, mid, lo], axis=0),
                       preferred_element_type=F32)
        for h in range(HGRN_HEADS):
            ls = slice(h * HGRN_DIM, (h + 1) * HGRN_DIM)
            qf = jax.nn.silu(q_ref[0, rs, ls].astype(F32))
            kf = 1.0 - fg[:, ls]
            iv = i_ref[0, rs, ls]
            bcum = sums[0:C, ls]
            bsuf = sums[C:2 * C, ls]
            att = _nt_dot(qf.astype(BF16), kf.astype(BF16)) * pm_ref[n_levels]
            for lv in range(n_levels):
                w = jnp.exp(sums[(2 + lv) * C:(3 + lv) * C, ls])
                qrow = qrow_ref[lv]
                z = (jnp.where(qrow > 0.5, qf, kf) * w).astype(BF16)
                att = att + _nt_dot(z, z) * pm_ref[lv]
            o = jnp.dot(att.astype(BF16), iv, preferred_element_type=F32)
            st = state_ref[h]
            o = o + _nt_dot((qf * jnp.exp(bcum)).astype(BF16), st.astype(BF16))
            kdec = (kf * jnp.exp(bsuf)).astype(BF16)
            state_ref[h] = st * jnp.exp(bcum[C - 1:C, :]) + _tn_dot(iv, kdec)
            o = o * lax.rsqrt(jnp.mean(o * o, axis=-1, keepdims=True) + RMS_EPS)
            o = o * nw_ref[:, ls] * jax.nn.silu(g_ref[0, rs, ls].astype(F32))
            o_ref[0, rs, ls] = o.astype(BF16)


def _hgrn(qb, fb, ib, gb, lb, norm_w, b0, n_batch):
    _, S, W = qb.shape
    rows = min(HGRN_ROWS, S)
    summat, qrow, pmask = _hgrn_constants(HGRN_CHUNK)
    summat = jnp.asarray(np.concatenate([summat] * 3, axis=1), BF16)
    n_levels = qrow.shape[0]
    blk = pl.BlockSpec((1, rows, W), lambda b, s: (b + b0, s, 0))
    vec = pl.BlockSpec((1, W), lambda b, s: (0, 0))
    const = lambda a: pl.BlockSpec(a.shape, lambda b, s: (0,) * a.ndim)
    qrow3 = qrow.reshape(n_levels, HGRN_CHUNK, 1)
    return pl.pallas_call(
        functools.partial(_hgrn_kernel, rows=rows),
        grid=(n_batch, S // rows),
        in_specs=[blk, blk, blk, blk, vec, vec, const(summat), const(qrow3), const(pmask)],
        out_specs=pl.BlockSpec((1, rows, W), lambda b, s: (b, s, 0)),
        out_shape=jax.ShapeDtypeStruct((n_batch, S, W), BF16),
        scratch_shapes=[pltpu.VMEM((HGRN_HEADS, HGRN_DIM, HGRN_DIM), F32)],
        compiler_params=_cparams("parallel", "arbitrary"),
        name="hgrn",
    )(qb, fb, ib, gb, lb, norm_w, jnp.asarray(summat), jnp.asarray(qrow3), jnp.asarray(pmask))


def _mix_route_kernel(ya_ref, yb_ref, ga_ref, gb_ref, x_ref, wa_ref, wb_ref, wo_ref, nw_ref,
                      rw_ref, rb_ref, tri_ref,
                      x1_ref, h2_ref, e_ref, w_ref, rank_ref, cnt_ref, carry_ref):
    @pl.when(pl.program_id(0) == 0)
    def _():
        carry_ref[...] = jnp.zeros_like(carry_ref)

    ua = jnp.dot(ya_ref[...], wa_ref[...], preferred_element_type=F32)
    ub = jnp.dot(yb_ref[...], wb_ref[...], preferred_element_type=F32)
    mixed = (jax.nn.sigmoid(ga_ref[...].astype(F32)) * ua
             + jax.nn.sigmoid(gb_ref[...].astype(F32)) * ub)
    x1 = x_ref[...] + jnp.dot(mixed.astype(BF16), wo_ref[...], preferred_element_type=F32)
    x1_ref[...] = x1
    h2 = x1 * lax.rsqrt(jnp.mean(x1 * x1, axis=-1, keepdims=True) + RMS_EPS) * nw_ref[...]
    half = h2.shape[1] // 2
    h2_ref[...] = _pack_bf16_pair(h2[:, :half], h2[:, half:])

    tm = x1.shape[0]
    h2_hi = h2.astype(BF16)
    h2_lo = (h2 - h2_hi.astype(F32)).astype(BF16)
    logits = _nt_dot(rw_ref[...], jnp.concatenate([h2_hi, h2_lo, h2_hi], axis=1)) + rb_ref[...]
    eid = lax.broadcasted_iota(jnp.int32, (N_EXPERTS, tm), 0)
    work = logits
    es, vs = [], []
    for _ in range(TOP_K):
        mx = jnp.max(work, axis=0, keepdims=True)
        idx = jnp.min(jnp.where(work == mx, eid, N_EXPERTS), axis=0, keepdims=True)
        es.append(idx)
        vs.append(mx)
        work = jnp.where(eid == idx, -jnp.inf, work)
    ex = [jnp.exp(v - vs[0]) for v in vs]
    den = ex[0] + ex[1] + ex[2] + ex[3]
    multi = jnp.zeros((N_EXPERTS, tm), F32)
    for k in range(TOP_K):
        multi = multi + jnp.where(eid == es[k], 1.0, 0.0)
    before = jnp.dot(multi.astype(BF16), tri_ref[...], preferred_element_type=F32) + carry_ref[...]
    for k in range(TOP_K):
        e_ref[k:k + 1, :] = es[k]
        w_ref[k:k + 1, :] = ex[k] / den
        rank_ref[k:k + 1, :] = jnp.sum(jnp.where(eid == es[k], before, 0.0), axis=0,
                                       keepdims=True).astype(jnp.int32)
    carry_ref[...] = carry_ref[...] + jnp.sum(multi, axis=1, keepdims=True)
    cnt_ref[...] = jnp.broadcast_to(carry_ref[...], cnt_ref.shape)


def _mix_route(ya, yb, ga, gb, x2, wa, wb, wo, norm_w, rw_t, rb, mixer_row0, row0, rows):
    D = x2.shape[1]
    T = rows
    tm = min(MIX_ROWS, T)
    row = lambda w: pl.BlockSpec((tm, w), lambda i: (i + row0 // tm, 0))
    mrow = lambda w: pl.BlockSpec((tm, w), lambda i: (i + mixer_row0 // tm, 0))
    out_row = pl.BlockSpec((tm, D), lambda i: (i, 0))
    whole = pl.BlockSpec(memory_space=pltpu.VMEM)
    kt = pl.BlockSpec((TOP_K, tm), lambda i: (0, i))
    tri = jnp.asarray(np.triu(np.ones((tm, tm), np.float32), 1), BF16)
    return pl.pallas_call(
        _mix_route_kernel,
        grid=(T // tm,),
        in_specs=[mrow(ATT_WIDTH), mrow(HGRN_WIDTH), row(D), row(D), row(D),
                  whole, whole, whole, pl.BlockSpec((1, D), lambda i: (0, 0)),
                  whole, whole, whole],
        out_specs=[out_row, pl.BlockSpec((tm, D // 2), lambda i: (i, 0)), kt, kt, kt,
                   pl.BlockSpec((N_EXPERTS, V7X_LANES), lambda i: (0, 0))],
        out_shape=[jax.ShapeDtypeStruct((T, D), F32), jax.ShapeDtypeStruct((T, D // 2), jnp.uint32),
                   jax.ShapeDtypeStruct((TOP_K, T), jnp.int32),
                   jax.ShapeDtypeStruct((TOP_K, T), F32),
                   jax.ShapeDtypeStruct((TOP_K, T), jnp.int32),
                   jax.ShapeDtypeStruct((N_EXPERTS, V7X_LANES), F32)],
        scratch_shapes=[pltpu.VMEM((N_EXPERTS, 1), F32)],
        compiler_params=_cparams("arbitrary"),
        name="mix_route",
    )(ya, yb, ga, gb, x2, wa, wb, wo, norm_w, rw_t, rb, tri)


def _sc_worker_id():
    return lax.axis_index("s") * V7X_SC_CORES + lax.axis_index("c")


def _sc_kernel(body, out_rows, like, window, name, extra_scratch=()):
    mesh = plsc.VectorSubcoreMesh(core_axis_name="c", subcore_axis_name="s")
    return pl.kernel(
        body, mesh=mesh,
        out_type=jax.ShapeDtypeStruct((out_rows,) + like.shape[1:], like.dtype),
        scratch_types=[pltpu.VMEM((window,), jnp.int32),
                       pltpu.VMEM((window,) + like.shape[1:], like.dtype), *extra_scratch],
        name=name)


def _scatter_rows(dest, h2, out_rows, pad_rows):
    T = dest.shape[1]
    n_workers = V7X_SC_CORES * V7X_SC_SUBCORES
    per_worker = T // n_workers
    window = min(SC_ROWS, per_worker)
    n_pad = pad_rows.shape[1]
    assert pad_rows.shape[0] == n_workers and n_pad % window == 0

    def body(dest_hbm, src_hbm, pad_hbm, zero_hbm, dst_hbm, idx_v, rows_v, pad_v, zero_v, all_v, sem,
             fill_sem):
        del idx_v
        wid = _sc_worker_id()
        base = wid * per_worker

        pltpu.sync_copy(pad_hbm.at[pl.ds(wid * n_pad, n_pad)], pad_v)
        pltpu.sync_copy(zero_hbm, zero_v)
        fills = [pltpu.async_copy(zero_v, dst_hbm.at[pad_v.at[pl.ds(c * window, window)]],
                                     fill_sem)
                 for c in range(n_pad // window)]

        for k in range(TOP_K):
            pltpu.sync_copy(dest_hbm.at[pl.ds(k * T + base, per_worker)],
                            all_v.at[pl.ds(k * per_worker, per_worker)])

        @pl.loop(0, per_worker // window)
        def _(c):
            pltpu.sync_copy(src_hbm.at[pl.ds(base + c * window, window)], rows_v)
            copies = [pltpu.async_copy(
                rows_v, dst_hbm.at[all_v.at[pl.ds(k * per_worker + c * window, window)]], sem)
                for k in range(TOP_K)]
            for cp in copies:
                cp.wait()

        for f in fills:
            f.wait()

    zeros = jnp.zeros((window,) + h2.shape[1:], h2.dtype)
    extra = [pltpu.VMEM((n_pad,), jnp.int32), pltpu.VMEM(zeros.shape, zeros.dtype),
             pltpu.VMEM((TOP_K * per_worker,), jnp.int32), pltpu.SemaphoreType.DMA,
             pltpu.SemaphoreType.DMA]
    return _sc_kernel(body, out_rows, h2, window, "scatter_rows", extra)(
        dest.reshape(-1), h2, pad_rows.reshape(-1), zeros)


def _gather_rows(dest, yg):
    T = dest.shape[1]
    n_workers = V7X_SC_CORES * V7X_SC_SUBCORES
    per_worker = TOP_K * T // n_workers
    window = min(SC_ROWS, per_worker)

    def body(dest_hbm, src_hbm, dst_hbm, idx_v, rows_v, all_v):
        del idx_v
        base = _sc_worker_id() * per_worker
        pltpu.sync_copy(dest_hbm.at[pl.ds(base, per_worker)], all_v)

        @pl.loop(0, per_worker // window)
        def _(c):
            pltpu.sync_copy(src_hbm.at[all_v.at[pl.ds(c * window, window)]], rows_v)
            pltpu.sync_copy(rows_v, dst_hbm.at[pl.ds(base + c * window, window)])

    extra = [pltpu.VMEM((per_worker,), jnp.int32)]
    return _sc_kernel(body, TOP_K * T, yg, window, "gather_rows", extra)(dest.reshape(-1), yg)


def _experts_kernel(blk_ref, exp_ref, lo_ref, hi_ref, slot_ref, efirst_ref, enext_ref,
                    x_ref, wgu_hbm, bg_ref, bu_ref, wd_hbm, bd_ref, perm_ref, y_ref,
                    wgu_stage, wd_stage, wg_buf, wu_buf, wd_buf, sems):
    w = pl.program_id(0)
    lo = lo_ref[w]
    hi = hi_ref[w]
    slot = slot_ref[w]

    tm = x_ref.shape[0]
    n_out = V7X_MXU_DIM

    def weight_copies(e, s):
        return [pltpu.make_async_copy(wgu_hbm.at[e], wgu_stage.at[s], sems.at[s, 0]),
                pltpu.make_async_copy(wd_hbm.at[e], wd_stage.at[s], sems.at[s, 1])]

    @pl.when(w == 0)
    def _():
        for c in weight_copies(exp_ref[0], slot):
            c.start()

    @pl.when(efirst_ref[w] == 1)
    def _():
        for c in weight_copies(exp_ref[w], slot):
            c.wait()

        @pl.when(enext_ref[w] >= 0)
        def _():
            for c in weight_copies(enext_ref[w], 1 - slot):
                c.start()

        half = V7X_MXU_DIM // 2
        for g in range(wgu_stage.shape[2] // V7X_MXU_DIM):
            wb = wgu_stage[slot, :, g * V7X_MXU_DIM:(g + 1) * V7X_MXU_DIM].astype(BF16)
            d = jnp.dot(wb, perm_ref[...], preferred_element_type=F32).astype(BF16)
            wg_buf[:, g * half:(g + 1) * half] = d[:, :half]
            wu_buf[:, g * half:(g + 1) * half] = d[:, half:]
        wd_buf[...] = wd_stage[slot].astype(BF16)

    sb = EXPERT_SUB_ROWS

    def ffn(r0):
        rs = slice(r0, r0 + sb)
        x = jnp.concatenate(_unpack_bf16_pair(x_ref[rs, :]), axis=1).astype(BF16)
        g = jnp.dot(x, wg_buf[...], preferred_element_type=F32) + bg_ref[0]
        u = jnp.dot(x, wu_buf[...], preferred_element_type=F32) + bu_ref[0]
        g = jnp.minimum(g, SWIGLU_LIMIT)
        u = jnp.clip(u, -SWIGLU_LIMIT, SWIGLU_LIMIT)
        act = ((u + 1.0) * (g * jax.nn.sigmoid(g * SWIGLU_ALPHA))).astype(BF16)
        half = y_ref.shape[1]

        def down(c0):
            cs = slice(c0, c0 + n_out)
            return (jnp.dot(act, wd_buf[:, cs], preferred_element_type=F32)
                    + bd_ref[0, :, cs])

        for c in range(half // n_out):
            cs = slice(c * n_out, (c + 1) * n_out)
            y_ref[rs, cs] = _pack_bf16_pair(down(c * n_out), down(half + c * n_out))

    n_sub = tm // sb
    fused = (lo == 0) & (hi == tm)

    @pl.when(fused)
    def _():
        for j in range(n_sub):
            ffn(j * sb)

    for j in range(n_sub):
        @pl.when(jnp.logical_not(fused) & (lo <= j * sb) & (hi > j * sb))
        def _():
            ffn(j * sb)


def _work_items(counts, n_rows, tm):
    nblk = n_rows // tm
    n_items = nblk + N_EXPERTS - 1
    end = jnp.cumsum(counts)
    start = end - counts
    fb = start // tm
    nitems = jnp.where(counts > 0, (end - 1) // tm - fb + 1, 0)
    item_end = jnp.cumsum(nitems)
    item_start = item_end - nitems
    w = jnp.arange(n_items, dtype=jnp.int32)
    valid = w < item_end[-1]
    wc = jnp.minimum(w, item_end[-1] - 1)
    e = jnp.sum(wc[:, None] >= item_end[None, :], axis=1).astype(jnp.int32)
    e = jnp.minimum(e, N_EXPERTS - 1)
    onehot = e[:, None] == jnp.arange(N_EXPERTS, dtype=jnp.int32)[None, :]
    pick = lambda table: jnp.sum(jnp.where(onehot, table[None, :], 0), axis=1)
    blk = (pick(fb) + (wc - pick(item_start))).astype(jnp.int32)
    lo = jnp.maximum(pick(start), blk * tm) - blk * tm
    hi = jnp.minimum(pick(end), (blk + 1) * tm) - blk * tm
    lo = jnp.where(valid, lo, 0).astype(jnp.int32)
    hi = jnp.where(valid, hi, 0).astype(jnp.int32)
    used = counts > 0
    ids = jnp.arange(N_EXPERTS, dtype=jnp.int32)
    slot = pick((jnp.cumsum(used) - 1) % 2).astype(jnp.int32)
    efirst = (valid & (w == pick(item_start))).astype(jnp.int32)
    later = jnp.where(used[None, :] & (ids[None, :] > ids[:, None]), ids[None, :], N_EXPERTS)
    next_used = jnp.min(later, axis=1)
    enext = pick(jnp.where(next_used < N_EXPERTS, next_used, -1)).astype(jnp.int32)
    return blk, e, lo, hi, slot, efirst, enext


def _gate_up_split_matrix():
    i = np.arange(V7X_MXU_DIM)
    src = np.where(i < V7X_MXU_DIM // 2, 2 * i, 2 * (i - V7X_MXU_DIM // 2) + 1)
    perm = np.zeros((V7X_MXU_DIM, V7X_MXU_DIM), np.float32)
    perm[src, i] = 1.0
    return jnp.asarray(perm, BF16)


def _experts(xg, counts, w_gate_up, bg, bu, w_down, bd):
    A = xg.shape[0]
    D = 2 * xg.shape[1]
    tm = min(EXPERT_ROWS, A)
    F = w_down.shape[1]
    items = _work_items(counts, A, tm)
    n_items = A // tm + N_EXPERTS - 1
    xs = pl.BlockSpec((tm, D // 2), lambda w, blk, e, *_: (blk[w], 0))
    bias = lambda c: pl.BlockSpec((1, 1, c), lambda w, blk, e, *_: (e[w], 0, 0))
    hbm = pl.BlockSpec(memory_space=pl.ANY)
    return pl.pallas_call(
        _experts_kernel,
        grid_spec=pltpu.PrefetchScalarGridSpec(
            num_scalar_prefetch=len(items),
            grid=(n_items,),
            in_specs=[xs, hbm, bias(F), bias(F), hbm, bias(D), pl.BlockSpec(memory_space=pltpu.VMEM)],
            out_specs=xs,
            scratch_shapes=[pltpu.VMEM((2, D, 2 * F), F32), pltpu.VMEM((2, F, D), F32),
                            pltpu.VMEM((D, F), BF16), pltpu.VMEM((D, F), BF16),
                            pltpu.VMEM((F, D), BF16), pltpu.SemaphoreType.DMA((2, 2))],
        ),
        out_shape=jax.ShapeDtypeStruct(xg.shape, jnp.uint32),
        compiler_params=_cparams("arbitrary"),
        name="experts",
    )(*items, xg, w_gate_up, bg, bu, w_down, bd, _gate_up_split_matrix())


def _combine_kernel(x1_ref, yk_ref, w_ref, nw_ref, *rest):
    o_ref = rest[-1]
    w = w_ref[...]
    x = x1_ref[...]
    for k in range(TOP_K):
        x = x + jnp.concatenate(_unpack_bf16_pair(yk_ref[k]), axis=1) * w[:, k:k + 1]
    o_ref[...] = x * lax.rsqrt(jnp.mean(x * x, axis=-1, keepdims=True) + RMS_EPS) * nw_ref[...]


def _combine(x1, yk, w_tk, norm_w, out_rows, row0, prev_out):
    Tg, D = x1.shape
    tm = min(COMBINE_ROWS, Tg)
    row = pl.BlockSpec((tm, D), lambda i: (i, 0))
    in_specs = [row, pl.BlockSpec((TOP_K, tm, D // 2), lambda i: (0, i, 0)),
                pl.BlockSpec((tm, TOP_K), lambda i: (i, 0)),
                pl.BlockSpec((1, D), lambda i: (0, 0))]
    args = [x1, yk, w_tk, norm_w]
    aliases = {}
    if prev_out is not None:
        in_specs.append(pl.BlockSpec(memory_space=pl.ANY))
        args.append(prev_out)
        aliases = {len(args) - 1: 0}
    return pl.pallas_call(
        _combine_kernel,
        grid=(Tg // tm,),
        in_specs=in_specs,
        out_specs=pl.BlockSpec((tm, D), lambda i: (i + row0 // tm, 0)),
        out_shape=jax.ShapeDtypeStruct((out_rows, D), F32),
        input_output_aliases=aliases,
        compiler_params=_cparams("parallel"),
        name="combine",
    )(*args)


def _qk_column_order():
    half = ATT_HEAD_DIM // 2
    order = []
    for p in range(ATT_HEADS // 2):
        for part in range(2):
            for h in (2 * p, 2 * p + 1):
                order.extend(range(h * ATT_HEAD_DIM + part * half, h * ATT_HEAD_DIM + (part + 1) * half))
    return np.asarray(order, np.int32)


def _rope_tables(seq):
    half = ATT_HEAD_DIM // 2
    inv = ROPE_THETA ** (-(jnp.arange(half, dtype=F32) * 2.0 / ATT_HEAD_DIM))
    ang = jnp.arange(seq, dtype=F32)[:, None] * inv[None, :]
    cos, sin = jnp.cos(ang), jnp.sin(ang)
    return (jnp.concatenate([cos, cos, cos, cos], axis=1),
            jnp.concatenate([-sin, -sin, sin, sin], axis=1))


def kernel(x, norm1_w, w_in, moba_up, hgrn_lb_logits, hgrn_norm_w, hgrn_up, w_out, norm2_w,
           router_w, router_b, w_gate_up, b_gate_up, w_down, b_down, final_norm_w):
    B, S, D = x.shape
    T = B * S
    assert S % MOBA_BLOCK == 0 and w_in.shape[0] == 1
    x2 = x.reshape(T, D)

    perm = _qk_column_order()
    w0 = w_in[0]
    w_in_p = jnp.concatenate([w0[:, :ATT_WIDTH][:, perm], w0[:, ATT_WIDTH:2 * ATT_WIDTH][:, perm],
                              w0[:, 2 * ATT_WIDTH:]], axis=1).astype(BF16)
    cos_t, sin_t = _rope_tables(S)
    lb = jnp.cumsum(jax.nn.softmax(hgrn_lb_logits.astype(F32), axis=0), axis=0)[0:1]

    qa, ka, va, qb, fb, ib, gb, ga, gtb = _in_proj(x2, norm1_w, w_in_p, cos_t, sin_t, S)
    r3 = lambda a: a.reshape(B, S, a.shape[1])

    rw_t = router_w[0].T
    rw_hi = rw_t.astype(BF16)
    rw_lo = (rw_t - rw_hi.astype(F32)).astype(BF16)
    mix_w = (moba_up[0].astype(BF16), hgrn_up[0].astype(BF16), w_out[0].astype(BF16), norm2_w,
             jnp.concatenate([rw_hi, rw_hi, rw_lo], axis=1), router_b[0][:, None])
    expert_w = (w_gate_up[0], b_gate_up[0][:, None, 0::2], b_gate_up[0][:, None, 1::2],
                w_down[0], b_down[0][:, None, :])

    Tg = T // MOE_TOKEN_GROUPS
    n_parts = MOE_TOKEN_GROUPS if B % MOE_TOKEN_GROUPS == 0 else 1
    bp = B // n_parts
    staged = []
    for g in range(MOE_TOKEN_GROUPS):
        if g * n_parts % MOE_TOKEN_GROUPS == 0:
            part = g * n_parts // MOE_TOKEN_GROUPS
            ya = _moba(r3(qa), r3(ka), r3(va), part * bp, bp).reshape(bp * S, ATT_WIDTH)
            yb = _hgrn(r3(qb), r3(fb), r3(ib), r3(gb), lb, hgrn_norm_w,
                       part * bp, bp).reshape(bp * S, HGRN_WIDTH)
        x1, h2, top_e, top_w, rank, cnt = _mix_route(ya, yb, ga, gtb, x2, *mix_w,
                                                     g * Tg - part * bp * S, g * Tg, Tg)
        counts = cnt[:, 0].astype(jnp.int32)
        sb = EXPERT_SUB_ROWS
        padded = (counts + sb - 1) // sb * sb
        start = jnp.cumsum(padded) - padded
        dest = rank + jnp.sum(jnp.where(top_e[:, :, None] == jnp.arange(N_EXPERTS)[None, None, :],
                                        start[None, None, :], 0), axis=-1)
        body_rows = TOP_K * Tg + N_EXPERTS * sb
        spare_row = body_rows
        i = jnp.arange(sb, dtype=jnp.int32)[None, :]
        n_fill = (padded - counts)[:, None]
        pad_rows = jnp.where(n_fill > 0, (start + counts)[:, None] + i % jnp.maximum(n_fill, 1),
                             spare_row + i)
        xg = _scatter_rows(dest, h2, body_rows + EXPERT_ROWS, pad_rows)
        staged.append((xg, padded, dest, x1, top_w))

    out = None
    for g, (xg, padded, dest, x1, top_w) in enumerate(staged):
        yg = _experts(xg, padded, *expert_w)
        yk = _gather_rows(dest, yg).reshape(TOP_K, Tg, D // 2)
        out = _combine(x1, yk, top_w.T, final_norm_w[None, :], T, g * Tg, out)
    return out.reshape(B, S, D)
```

```python
import functools
import math

import numpy as np
import jax
import jax.numpy as jnp
from jax import lax
from jax.experimental import pallas as pl
from jax.experimental.pallas import tpu as pltpu
from jax.experimental.pallas import tpu_sc as plsc

ATT_HEADS = 8
ATT_HEAD_DIM = 64
ATT_WIDTH = ATT_HEADS * ATT_HEAD_DIM
MOBA_BLOCK = 256
MOBA_TOPK = 3
ROPE_THETA = 10000.0
HGRN_HEADS = 4
HGRN_DIM = 128
HGRN_WIDTH = HGRN_HEADS * HGRN_DIM
HGRN_CHUNK = 64
N_EXPERTS = 32
TOP_K = 4
SWIGLU_LIMIT = 7.0
SWIGLU_ALPHA = 1.702
RMS_EPS = 1e-6
NEG = -1e30

V7X_LANES = 128
V7X_SUBLANES = 8
V7X_MXU_DIM = 256
V7X_VMEM_LIMIT_BYTES = 56 * 1024 * 1024
V7X_SC_CORES = 2
V7X_SC_SUBCORES = 16

MOBA_PAIRS = 2
MOBA_GROUP = 4
PROJ_ROWS = 1024
PROJ_COLS = 512
HGRN_ROWS = 512
MIX_ROWS = 512
EXPERT_ROWS = 512
EXPERT_SUB_ROWS = 256
MOE_TOKEN_GROUPS = 2
SC_ROWS = 64
COMBINE_ROWS = 512

F32 = jnp.float32
BF16 = jnp.bfloat16


def _nt_dot(a, b, precision=None):
    return lax.dot_general(a, b, (((1,), (1,)), ((), ())), precision=precision,
                           preferred_element_type=F32)


def _tn_dot(a, b, precision=None):
    return lax.dot_general(a, b, (((0,), (0,)), ((), ())), precision=precision,
                           preferred_element_type=F32)


def _cparams(*sem):
    return pltpu.CompilerParams(dimension_semantics=sem, vmem_limit_bytes=V7X_VMEM_LIMIT_BYTES)


def _pack_bf16_pair(lo, hi):
    lo_bits = lax.bitcast_convert_type(lo.astype(BF16).astype(F32), jnp.uint32)
    hi_bits = lax.bitcast_convert_type(hi.astype(BF16).astype(F32), jnp.uint32)
    return (lo_bits >> 16) | hi_bits


def _unpack_bf16_pair(word):
    lo = lax.bitcast_convert_type(word << 16, F32)
    hi = lax.bitcast_convert_type(word & jnp.uint32(0xFFFF0000), F32)
    return lo, hi


def _in_proj_kernel(x_ref, nw_ref, w_ref, cos_ref, sin_ref,
                    qa_ref, ka_ref, va_ref, qb_ref, fb_ref, ib_ref, gb_ref, ga_ref, gtb_ref):
    x = x_ref[...]
    h = x * lax.rsqrt(jnp.mean(x * x, axis=-1, keepdims=True) + RMS_EPS) * nw_ref[...]
    h = h.astype(BF16)
    cos = cos_ref[...]
    sin = sin_ref[...]

    def proj(c):
        return jnp.dot(h, w_ref[:, c * PROJ_COLS:(c + 1) * PROJ_COLS], preferred_element_type=F32)

    def rope(t):
        out = []
        for j in range(PROJ_COLS // V7X_LANES):
            tj = t[:, j * V7X_LANES:(j + 1) * V7X_LANES]
            out.append(tj * cos + pltpu.roll(tj, V7X_LANES // 2, 1) * sin)
        return jnp.concatenate(out, axis=1)

    qa_ref[...] = (rope(proj(0)) * (ATT_HEAD_DIM ** -0.5)).astype(BF16)
    ka_ref[...] = rope(proj(1)).astype(BF16)
    va_ref[...] = proj(2).astype(BF16)
    qb_ref[...] = proj(3).astype(BF16)
    fb_ref[...] = proj(4)
    ib_ref[...] = proj(5).astype(BF16)
    gb_ref[...] = proj(6).astype(BF16)
    ga_ref[:, :PROJ_COLS] = proj(7).astype(BF16)
    ga_ref[:, PROJ_COLS:] = proj(8).astype(BF16)
    gtb_ref[:, :PROJ_COLS] = proj(9).astype(BF16)
    gtb_ref[:, PROJ_COLS:] = proj(10).astype(BF16)


def _in_proj(x2, norm_w, w_in_bf16, cos_t, sin_t, seq):
    T, D = x2.shape
    tm = min(PROJ_ROWS, seq)
    n_seq_tiles = seq // tm
    row = lambda w: pl.BlockSpec((tm, w), lambda i: (i, 0))
    tab = pl.BlockSpec((tm, V7X_LANES), lambda i: (i % n_seq_tiles, 0))
    widths = [ATT_WIDTH] * 3 + [HGRN_WIDTH] * 4 + [D, D]
    dtypes = [BF16, BF16, BF16, BF16, F32, BF16, BF16, BF16, BF16]
    return pl.pallas_call(
        _in_proj_kernel,
        grid=(T // tm,),
        in_specs=[row(D), pl.BlockSpec((1, D), lambda i: (0, 0)),
                  pl.BlockSpec(memory_space=pltpu.VMEM), tab, tab],
        out_specs=[row(w) for w in widths],
        out_shape=[jax.ShapeDtypeStruct((T, w), dt) for w, dt in zip(widths, dtypes)],
        compiler_params=_cparams("parallel"),
        name="in_proj",
    )(x2, norm_w, w_in_bf16, cos_t, sin_t)


def _moba_kernel(q_ref, k_ref, v_ref, o_ref, kaug_ref, kmean_ref, vt_ref, acc_ref, *s_refs, nb):
    qi = pl.program_id(2)
    blk = MOBA_BLOCK
    lanes = V7X_LANES
    n_pairs = kaug_ref.shape[0]
    heads = [(pp, hh) for pp in range(n_pairs) for hh in range(2)]
    nbp = kmean_ref.shape[1]

    @pl.when(qi == 0)
    def _():
        rowb = lax.broadcasted_iota(jnp.int32, (nb * blk, lanes), 0) // blk
        col = lax.broadcasted_iota(jnp.int32, (nb * blk, lanes), 1)
        onehot = jnp.where(rowb == col, 1.0, 0.0).astype(BF16)
        for pp in range(n_pairs):
            pl_ = slice(pp * lanes, (pp + 1) * lanes)
            kaug_ref[pp, :, :lanes] = k_ref[0, :, pl_]
            kaug_ref[pp, :, lanes:] = onehot
            means = [jnp.sum(k_ref[0, n * blk:(n + 1) * blk, pl_].astype(F32), axis=0,
                             keepdims=True) * (1.0 / blk) for n in range(nb)]
            km = jnp.concatenate(means + [jnp.zeros((nbp - nb, lanes), F32)] * (nbp > nb), axis=0)
            hi = km.astype(BF16)
            rem = km - hi.astype(F32)
            mid = rem.astype(BF16)
            lo = (rem - mid.astype(F32)).astype(BF16)
            kmean_ref[pp] = jnp.concatenate([hi, mid, lo], axis=1)
            for c in range(nb):
                vt = v_ref[0, c * blk:(c + 1) * blk, pl_].astype(F32).T
                vt_ref[pp, :lanes, c * blk:(c + 1) * blk] = vt.astype(BF16)
            vt_ref[pp, lanes:, :] = jnp.ones((vt_ref.shape[1] - lanes, nb * blk), BF16)

    feat = lax.broadcasted_iota(jnp.int32, (lanes, blk), 0)
    key_i = lax.broadcasted_iota(jnp.int32, (blk, blk), 0)
    qry_i = lax.broadcasted_iota(jnp.int32, (blk, blk), 1)
    own = pl.multiple_of(qi * blk, blk)
    blk_id = lax.broadcasted_iota(jnp.int32, (nbp, 2 * blk), 0)
    slab = 2 * V7X_SUBLANES

    def slab_max(s):
        return jnp.max(s.reshape(s.shape[0] // slab, slab, blk), axis=0)

    q_aug, m_init = [], []
    for pp in range(n_pairs):
        qt = q_ref[0, :, pp * lanes:(pp + 1) * lanes].astype(F32).T
        k_own = kaug_ref[pp, pl.ds(own, blk), :lanes]
        qhs = [jnp.where((feat // (ATT_HEAD_DIM // 2)) % 2 == hh, qt, 0.0).astype(BF16)
               for hh in range(2)]
        q2 = jnp.concatenate(qhs, axis=1)
        gate = jnp.dot(kmean_ref[pp], jnp.concatenate([q2, q2, q2], axis=0),
                       preferred_element_type=F32)
        gate = jnp.where(blk_id < qi, gate, NEG)
        beaten = jnp.zeros((nbp, 2 * blk), F32)
        for n in range(nb):
            gn = gate[n:n + 1, :]
            wins = (gn > gate) | ((gn == gate) & (blk_id > n))
            beaten = beaten + jnp.where(wins, 1.0, 0.0)
        sel = (beaten < MOBA_TOPK) & (blk_id < qi)
        bias = jnp.where(sel, 0.0, NEG).astype(BF16)
        bias = jnp.concatenate([bias, jnp.zeros((lanes - nbp, 2 * blk), BF16)], axis=0)
        for hh in range(2):
            h = 2 * pp + hh
            q_aug.append(jnp.concatenate([qhs[hh], bias[:, hh * blk:(hh + 1) * blk]], axis=0))
            s = jnp.dot(k_own, qhs[hh], preferred_element_type=F32)
            s = jnp.where(key_i <= qry_i, s, NEG)
            m0 = jnp.max(slab_max(s), axis=0, keepdims=True)
            p = jnp.exp(s - m0).astype(BF16)
            acc_ref[h] = jnp.dot(vt_ref[pp, :, pl.ds(own, blk)], p, preferred_element_type=F32)
            m_init.append(m0)

    group = MOBA_GROUP * blk
    n_groups = (qi + MOBA_GROUP) // MOBA_GROUP

    def score_head(g, h):
        off = pl.multiple_of(g * group, group)
        s = jnp.dot(kaug_ref[h // 2, pl.ds(off, group), :], q_aug[h], preferred_element_type=F32)
        s_refs[h][pl.ds(off, group), :] = s
        return jnp.max(slab_max(s), axis=0, keepdims=True)

    def value_head(g, h, gmax, m_old):
        off = pl.multiple_of(g * group, group)
        m_new = jnp.maximum(m_old, gmax)
        alpha = jnp.exp(m_old - m_new)
        p = jnp.exp(s_refs[h][pl.ds(off, group), :] - m_new).astype(BF16)
        acc_ref[h] = alpha * acc_ref[h] + jnp.dot(vt_ref[h // 2, :, pl.ds(off, group)], p,
                                                  preferred_element_type=F32)
        return m_new

    def pipelined(g, carry):
        gmax, ms = carry
        last = len(heads) - 1
        new_gmax, new_ms = [None] * (last + 1), [None] * (last + 1)
        new_gmax[last] = score_head(g + 1, last)
        for h in range(last + 1):
            new_ms[h] = value_head(g, h, gmax[h], ms[h])
            if h < last:
                new_gmax[h] = score_head(g + 1, h)
        return tuple(new_gmax), tuple(new_ms)

    first = tuple(score_head(0, h) for h in range(len(heads)))
    gmax, ms = lax.fori_loop(0, n_groups - 1, pipelined, (first, tuple(m_init)))
    for h in range(len(heads)):
        value_head(n_groups - 1, h, gmax[h], ms[h])

    half = lanes // 2
    outs = []
    for pp in range(n_pairs):
        outs.append(acc_ref[2 * pp, :half, :] / acc_ref[2 * pp, lanes:lanes + 1, :])
        outs.append(acc_ref[2 * pp + 1, half:lanes, :] / acc_ref[2 * pp + 1, lanes:lanes + 1, :])
    o_ref[0] = jnp.concatenate(outs, axis=0).T.astype(BF16)


def _moba(q, k, v, b0, n_batch):
    S = q.shape[1]
    nb = S // MOBA_BLOCK
    assert nb % MOBA_GROUP == 0
    nbp = -(-nb // (2 * V7X_SUBLANES)) * (2 * V7X_SUBLANES)
    n_pairs = ATT_WIDTH // V7X_LANES
    pp = MOBA_PAIRS
    assert n_pairs % pp == 0
    vt_rows = V7X_LANES + 2 * V7X_SUBLANES
    qspec = pl.BlockSpec((1, MOBA_BLOCK, pp * V7X_LANES), lambda b, p, i: (b + b0, i, p))
    kvspec = pl.BlockSpec((1, S, pp * V7X_LANES), lambda b, p, i: (b + b0, 0, p))
    return pl.pallas_call(
        functools.partial(_moba_kernel, nb=nb),
        grid=(n_batch, n_pairs // pp, nb),
        in_specs=[qspec, kvspec, kvspec],
        out_specs=pl.BlockSpec((1, MOBA_BLOCK, pp * V7X_LANES), lambda b, p, i: (b, i, p)),
        out_shape=jax.ShapeDtypeStruct((n_batch, S, ATT_WIDTH), BF16),
        scratch_shapes=[pltpu.VMEM((pp, S, 2 * V7X_LANES), BF16),
                        pltpu.VMEM((pp, nbp, 3 * V7X_LANES), BF16),
                        pltpu.VMEM((pp, vt_rows, S), BF16),
                        pltpu.VMEM((2 * pp, vt_rows, MOBA_BLOCK), F32)]
                       + [pltpu.VMEM((S, MOBA_BLOCK), F32)] * (2 * pp),
        compiler_params=_cparams("parallel", "parallel", "arbitrary"),
        name="moba",
    )(q, k, v)


def _hgrn_level_sizes(chunk):
    return [chunk >> (i + 1) for i in range(int(math.log2(chunk)))]


def _hgrn_constants(chunk):
    t = np.arange(chunk)
    mats = [(t[None, :] <= t[:, None]),
            (t[None, :] > t[:, None])]
    qrows, pmasks = [], []
    for bs in _hgrn_level_sizes(chunk):
        blk = t // bs
        odd = (blk % 2) == 1
        lo, hi = blk * bs, (blk + 1) * bs
        u = t[None, :]
        m_odd = (u >= lo[:, None]) & (u <= t[:, None])
        m_even = (u > t[:, None]) & (u < hi[:, None])
        mats.append(np.where(odd[:, None], m_odd, m_even))
        qrows.append(odd)
        pmasks.append(odd[:, None] & (blk[None, :] == blk[:, None] - 1))
    pmasks.append(t[None, :] == t[:, None])
    summat = np.concatenate(mats, axis=0).astype(np.float32)
    qrow = np.stack(qrows, axis=0).astype(np.float32)
    pmask = np.stack(pmasks, axis=0).astype(np.float32)
    return summat, qrow, pmask


def _hgrn_kernel(q_ref, f_ref, i_ref, g_ref, lb_ref, nw_ref, sm_ref, qrow_ref, pm_ref,
                 o_ref, state_ref, *, rows):
    C = HGRN_CHUNK
    n_levels = qrow_ref.shape[0]

    @pl.when(pl.program_id(1) == 0)
    def _():
        state_ref[...] = jnp.zeros_like(state_ref)

    lb = lb_ref[...]
    summat = sm_ref[...]
    for c in range(rows // C):
        rs = slice(c * C, (c + 1) * C)
        fg = lb + (1.0 - lb) * jax.nn.sigmoid(f_ref[0, rs, :])
        logf = jnp.log(fg)
        hi = logf.astype(BF16)
        rem = logf - hi.astype(F32)
        mid = rem.astype(BF16)
        lo = (rem - mid.astype(F32)).astype(BF16)
        sums = jnp.dot(summat, jnp.concatenate([hi, mid, lo], axis=0),
                       preferred_element_type=F32)
        for h in range(HGRN_HEADS):
            ls = slice(h * HGRN_DIM, (h + 1) * HGRN_DIM)
            qf = jax.nn.silu(q_ref[0, rs, ls].astype(F32))
            kf = 1.0 - fg[:, ls]
            iv = i_ref[0, rs, ls]
            bcum = sums[0:C, ls]
            bsuf = sums[C:2 * C, ls]
            att = _nt_dot(qf.astype(BF16), kf.astype(BF16)) * pm_ref[n_levels]
            for lv in range(n_levels):
                w = jnp.exp(sums[(2 + lv) * C:(3 + lv) * C, ls])
                qrow = qrow_ref[lv]
                z = (jnp.where(qrow > 0.5, qf, kf) * w).astype(BF16)
                att = att + _nt_dot(z, z) * pm_ref[lv]
            o = jnp.dot(att.astype(BF16), iv, preferred_element_type=F32)
            st = state_ref[h]
            o = o + _nt_dot((qf * jnp.exp(bcum)).astype(BF16), st.astype(BF16))
            kdec = (kf * jnp.exp(bsuf)).astype(BF16)
            state_ref[h] = st * jnp.exp(bcum[C - 1:C, :]) + _tn_dot(iv, kdec)
            o = o * lax.rsqrt(jnp.mean(o * o, axis=-1, keepdims=True) + RMS_EPS)
            o = o * nw_ref[:, ls] * jax.nn.silu(g_ref[0, rs, ls].astype(F32))
            o_ref[0, rs, ls] = o.astype(BF16)


def _hgrn(qb, fb, ib, gb, lb, norm_w, b0, n_batch):
    _, S, W = qb.shape
    rows = min(HGRN_ROWS, S)
    summat, qrow, pmask = _hgrn_constants(HGRN_CHUNK)
    summat = jnp.asarray(np.concatenate([summat] * 3, axis=1), BF16)
    n_levels = qrow.shape[0]
    blk = pl.BlockSpec((1, rows, W), lambda b, s: (b + b0, s, 0))
    vec = pl.BlockSpec((1, W), lambda b, s: (0, 0))
    const = lambda a: pl.BlockSpec(a.shape, lambda b, s: (0,) * a.ndim)
    qrow3 = qrow.reshape(n_levels, HGRN_CHUNK, 1)
    return pl.pallas_call(
        functools.partial(_hgrn_kernel, rows=rows),
        grid=(n_batch, S // rows),
        in_specs=[blk, blk, blk, blk, vec, vec, const(summat), const(qrow3), const(pmask)],
        out_specs=pl.BlockSpec((1, rows, W), lambda b, s: (b, s, 0)),
        out_shape=jax.ShapeDtypeStruct((n_batch, S, W), BF16),
        scratch_shapes=[pltpu.VMEM((HGRN_HEADS, HGRN_DIM, HGRN_DIM), F32)],
        compiler_params=_cparams("parallel", "arbitrary"),
        name="hgrn",
    )(qb, fb, ib, gb, lb, norm_w, jnp.asarray(summat), jnp.asarray(qrow3), jnp.asarray(pmask))


def _mix_route_kernel(ya_ref, yb_ref, ga_ref, gb_ref, x_ref, wa_ref, wb_ref, wo_ref, nw_ref,
                      rw_ref, rb_ref, tri_ref, anchor_ref,
                      x1_ref, h2_ref, e_ref, w_ref, rank_ref, cnt_ref, carry_ref):
    @pl.when(pl.program_id(0) == 0)
    def _():
        carry_ref[...] = jnp.zeros_like(carry_ref)

    ua = jnp.dot(ya_ref[...], wa_ref[...], preferred_element_type=F32)
    ub = jnp.dot(yb_ref[...], wb_ref[...], preferred_element_type=F32)
    mixed = (jax.nn.sigmoid(ga_ref[...].astype(F32)) * ua
             + jax.nn.sigmoid(gb_ref[...].astype(F32)) * ub)
    x1 = x_ref[...] + jnp.dot(mixed.astype(BF16), wo_ref[...], preferred_element_type=F32)
    x1_ref[...] = x1
    h2 = x1 * lax.rsqrt(jnp.mean(x1 * x1, axis=-1, keepdims=True) + RMS_EPS) * nw_ref[...]
    half = h2.shape[1] // 2
    h2_ref[...] = _pack_bf16_pair(h2[:, :half], h2[:, half:])

    tm = x1.shape[0]
    h2_hi = h2.astype(BF16)
    h2_lo = (h2 - h2_hi.astype(F32)).astype(BF16)
    logits = _nt_dot(rw_ref[...], jnp.concatenate([h2_hi, h2_lo, h2_hi], axis=1)) + rb_ref[...]
    eid = lax.broadcasted_iota(jnp.int32, (N_EXPERTS, tm), 0)
    work = logits
    es, vs = [], []
    for _ in range(TOP_K):
        mx = jnp.max(work, axis=0, keepdims=True)
        idx = jnp.min(jnp.where(work == mx, eid, N_EXPERTS), axis=0, keepdims=True)
        es.append(idx)
        vs.append(mx)
        work = jnp.where(eid == idx, -jnp.inf, work)
    ex = [jnp.exp(v - vs[0]) for v in vs]
    den = ex[0] + ex[1] + ex[2] + ex[3]
    multi = jnp.zeros((N_EXPERTS, tm), F32)
    for k in range(TOP_K):
        multi = multi + jnp.where(eid == es[k], 1.0, 0.0)
    before = jnp.dot(multi.astype(BF16), tri_ref[...], preferred_element_type=F32) + carry_ref[...]
    for k in range(TOP_K):
        e_ref[k:k + 1, :] = es[k]
        w_ref[k:k + 1, :] = ex[k] / den
        rank_ref[k:k + 1, :] = jnp.sum(jnp.where(eid == es[k], before, 0.0), axis=0,
                                       keepdims=True).astype(jnp.int32)
    carry_ref[...] = carry_ref[...] + jnp.sum(multi, axis=1, keepdims=True)
    cnt_ref[...] = jnp.broadcast_to(carry_ref[...], cnt_ref.shape)


def _mix_route(ya, yb, ga, gb, x2, wa, wb, wo, norm_w, rw_t, rb, mixer_row0, row0, rows, anchor):
    D = x2.shape[1]
    T = rows
    tm = min(MIX_ROWS, T)
    row = lambda w: pl.BlockSpec((tm, w), lambda i: (i + row0 // tm, 0))
    mrow = lambda w: pl.BlockSpec((tm, w), lambda i: (i + mixer_row0 // tm, 0))
    out_row = pl.BlockSpec((tm, D), lambda i: (i, 0))
    whole = pl.BlockSpec(memory_space=pltpu.VMEM)
    kt = pl.BlockSpec((TOP_K, tm), lambda i: (0, i))
    tri = jnp.asarray(np.triu(np.ones((tm, tm), np.float32), 1), BF16)
    return pl.pallas_call(
        _mix_route_kernel,
        grid=(T // tm,),
        in_specs=[mrow(ATT_WIDTH), mrow(HGRN_WIDTH), row(D), row(D), row(D),
                  whole, whole, whole, pl.BlockSpec((1, D), lambda i: (0, 0)),
                  whole, whole, whole, pl.BlockSpec(memory_space=pltpu.SMEM)],
        out_specs=[out_row, pl.BlockSpec((tm, D // 2), lambda i: (i, 0)), kt, kt, kt,
                   pl.BlockSpec((N_EXPERTS, V7X_LANES), lambda i: (0, 0))],
        out_shape=[jax.ShapeDtypeStruct((T, D), F32), jax.ShapeDtypeStruct((T, D // 2), jnp.uint32),
                   jax.ShapeDtypeStruct((TOP_K, T), jnp.int32),
                   jax.ShapeDtypeStruct((TOP_K, T), F32),
                   jax.ShapeDtypeStruct((TOP_K, T), jnp.int32),
                   jax.ShapeDtypeStruct((N_EXPERTS, V7X_LANES), F32)],
        scratch_shapes=[pltpu.VMEM((N_EXPERTS, 1), F32)],
        compiler_params=_cparams("arbitrary"),
        name="mix_route",
    )(ya, yb, ga, gb, x2, wa, wb, wo, norm_w, rw_t, rb, tri, anchor)


def _sc_worker_id():
    return lax.axis_index("s") * V7X_SC_CORES + lax.axis_index("c")


def _sc_kernel(body, out_rows, like, window, name, extra_scratch=()):
    mesh = plsc.VectorSubcoreMesh(core_axis_name="c", subcore_axis_name="s")
    return pl.kernel(
        body, mesh=mesh,
        out_type=jax.ShapeDtypeStruct((out_rows,) + like.shape[1:], like.dtype),
        scratch_types=[pltpu.VMEM((window,), jnp.int32),
                       pltpu.VMEM((window,) + like.shape[1:], like.dtype), *extra_scratch],
        name=name)


def _scatter_rows(dest, h2, out_rows, pad_rows):
    T = dest.shape[1]
    n_workers = V7X_SC_CORES * V7X_SC_SUBCORES
    per_worker = T // n_workers
    window = min(SC_ROWS, per_worker)
    n_pad = pad_rows.shape[1]
    assert pad_rows.shape[0] == n_workers and n_pad % window == 0

    def body(dest_hbm, src_hbm, pad_hbm, zero_hbm, dst_hbm, idx_v, rows_v, pad_v, zero_v, all_v, sem,
             fill_sem):
        del idx_v
        wid = _sc_worker_id()
        base = wid * per_worker

        pltpu.sync_copy(pad_hbm.at[pl.ds(wid * n_pad, n_pad)], pad_v)
        pltpu.sync_copy(zero_hbm, zero_v)
        fills = [pltpu.async_copy(zero_v, dst_hbm.at[pad_v.at[pl.ds(c * window, window)]],
                                     fill_sem)
                 for c in range(n_pad // window)]

        for k in range(TOP_K):
            pltpu.sync_copy(dest_hbm.at[pl.ds(k * T + base, per_worker)],
                            all_v.at[pl.ds(k * per_worker, per_worker)])

        @pl.loop(0, per_worker // window)
        def _(c):
            pltpu.sync_copy(src_hbm.at[pl.ds(base + c * window, window)], rows_v)
            copies = [pltpu.async_copy(
                rows_v, dst_hbm.at[all_v.at[pl.ds(k * per_worker + c * window, window)]], sem)
                for k in range(TOP_K)]
            for cp in copies:
                cp.wait()

        for f in fills:
            f.wait()

    zeros = jnp.zeros((window,) + h2.shape[1:], h2.dtype)
    extra = [pltpu.VMEM((n_pad,), jnp.int32), pltpu.VMEM(zeros.shape, zeros.dtype),
             pltpu.VMEM((TOP_K * per_worker,), jnp.int32), pltpu.SemaphoreType.DMA,
             pltpu.SemaphoreType.DMA]
    return _sc_kernel(body, out_rows, h2, window, "scatter_rows", extra)(
        dest.reshape(-1), h2, pad_rows.reshape(-1), zeros)


def _gather_rows(dest, yg):
    T = dest.shape[1]
    n_workers = V7X_SC_CORES * V7X_SC_SUBCORES
    per_worker = TOP_K * T // n_workers
    window = min(SC_ROWS, per_worker)

    def body(dest_hbm, src_hbm, dst_hbm, idx_v, rows_v, all_v):
        del idx_v
        base = _sc_worker_id() * per_worker
        pltpu.sync_copy(dest_hbm.at[pl.ds(base, per_worker)], all_v)

        @pl.loop(0, per_worker // window)
        def _(c):
            pltpu.sync_copy(src_hbm.at[all_v.at[pl.ds(c * window, window)]], rows_v)
            pltpu.sync_copy(rows_v, dst_hbm.at[pl.ds(base + c * window, window)])

    extra = [pltpu.VMEM((per_worker,), jnp.int32)]
    return _sc_kernel(body, TOP_K * T, yg, window, "gather_rows", extra)(dest.reshape(-1), yg)


def _experts_kernel(blk_ref, exp_ref, lo_ref, hi_ref, slot_ref, efirst_ref, enext_ref,
                    x_ref, wgu_hbm, bg_ref, bu_ref, wd_hbm, bd_ref, perm_ref, y_ref,
                    wgu_stage, wd_stage, wg_buf, wu_buf, wd_buf, sems):
    w = pl.program_id(0)
    lo = lo_ref[w]
    hi = hi_ref[w]
    slot = slot_ref[w]

    tm = x_ref.shape[0]
    n_out = V7X_MXU_DIM

    def weight_copies(e, s):
        return [pltpu.make_async_copy(wgu_hbm.at[e], wgu_stage.at[s], sems.at[s, 0]),
                pltpu.make_async_copy(wd_hbm.at[e], wd_stage.at[s], sems.at[s, 1])]

    @pl.when(w == 0)
    def _():
        for c in weight_copies(exp_ref[0], slot):
            c.start()

    @pl.when(efirst_ref[w] == 1)
    def _():
        for c in weight_copies(exp_ref[w], slot):
            c.wait()

        @pl.when(enext_ref[w] >= 0)
        def _():
            for c in weight_copies(enext_ref[w], 1 - slot):
                c.start()

        half = V7X_MXU_DIM // 2
        for g in range(wgu_stage.shape[2] // V7X_MXU_DIM):
            wb = wgu_stage[slot, :, g * V7X_MXU_DIM:(g + 1) * V7X_MXU_DIM].astype(BF16)
            d = jnp.dot(wb, perm_ref[...], preferred_element_type=F32).astype(BF16)
            wg_buf[:, g * half:(g + 1) * half] = d[:, :half]
            wu_buf[:, g * half:(g + 1) * half] = d[:, half:]
        wd_buf[...] = wd_stage[slot].astype(BF16)

    sb = EXPERT_SUB_ROWS

    def ffn(r0):
        rs = slice(r0, r0 + sb)
        x = jnp.concatenate(_unpack_bf16_pair(x_ref[rs, :]), axis=1).astype(BF16)
        g = jnp.dot(x, wg_buf[...], preferred_element_type=F32) + bg_ref[0]
        u = jnp.dot(x, wu_buf[...], preferred_element_type=F32) + bu_ref[0]
        g = jnp.minimum(g, SWIGLU_LIMIT)
        u = jnp.clip(u, -SWIGLU_LIMIT, SWIGLU_LIMIT)
        act = ((u + 1.0) * (g * jax.nn.sigmoid(g * SWIGLU_ALPHA))).astype(BF16)
        half = y_ref.shape[1]

        def down(c0):
            cs = slice(c0, c0 + n_out)
            return (jnp.dot(act, wd_buf[:, cs], preferred_element_type=F32)
                    + bd_ref[0, :, cs])

        for c in range(half // n_out):
            cs = slice(c * n_out, (c + 1) * n_out)
            y_ref[rs, cs] = _pack_bf16_pair(down(c * n_out), down(half + c * n_out))

    n_sub = tm // sb
    fused = (lo == 0) & (hi == tm)

    @pl.when(fused)
    def _():
        for j in range(n_sub):
            ffn(j * sb)

    for j in range(n_sub):
        @pl.when(jnp.logical_not(fused) & (lo <= j * sb) & (hi > j * sb))
        def _():
            ffn(j * sb)


def _work_items(counts, n_rows, tm):
    nblk = n_rows // tm
    n_items = nblk + N_EXPERTS - 1
    end = jnp.cumsum(counts)
    start = end - counts
    fb = start // tm
    nitems = jnp.where(counts > 0, (end - 1) // tm - fb + 1, 0)
    item_end = jnp.cumsum(nitems)
    item_start = item_end - nitems
    w = jnp.arange(n_items, dtype=jnp.int32)
    valid = w < item_end[-1]
    wc = jnp.minimum(w, item_end[-1] - 1)
    e = jnp.sum(wc[:, None] >= item_end[None, :], axis=1).astype(jnp.int32)
    e = jnp.minimum(e, N_EXPERTS - 1)
    onehot = e[:, None] == jnp.arange(N_EXPERTS, dtype=jnp.int32)[None, :]
    pick = lambda table: jnp.sum(jnp.where(onehot, table[None, :], 0), axis=1)
    blk = (pick(fb) + (wc - pick(item_start))).astype(jnp.int32)
    lo = jnp.maximum(pick(start), blk * tm) - blk * tm
    hi = jnp.minimum(pick(end), (blk + 1) * tm) - blk * tm
    lo = jnp.where(valid, lo, 0).astype(jnp.int32)
    hi = jnp.where(valid, hi, 0).astype(jnp.int32)
    used = counts > 0
    ids = jnp.arange(N_EXPERTS, dtype=jnp.int32)
    slot = pick((jnp.cumsum(used) - 1) % 2).astype(jnp.int32)
    efirst = (valid & (w == pick(item_start))).astype(jnp.int32)
    later = jnp.where(used[None, :] & (ids[None, :] > ids[:, None]), ids[None, :], N_EXPERTS)
    next_used = jnp.min(later, axis=1)
    enext = pick(jnp.where(next_used < N_EXPERTS, next_used, -1)).astype(jnp.int32)
    return blk, e, lo, hi, slot, efirst, enext


def _gate_up_split_matrix():
    i = np.arange(V7X_MXU_DIM)
    src = np.where(i < V7X_MXU_DIM // 2, 2 * i, 2 * (i - V7X_MXU_DIM // 2) + 1)
    perm = np.zeros((V7X_MXU_DIM, V7X_MXU_DIM), np.float32)
    perm[src, i] = 1.0
    return jnp.asarray(perm, BF16)


def _experts(xg, counts, w_gate_up, bg, bu, w_down, bd):
    A = xg.shape[0]
    D = 2 * xg.shape[1]
    tm = min(EXPERT_ROWS, A)
    F = w_down.shape[1]
    items = _work_items(counts, A, tm)
    n_items = A // tm + N_EXPERTS - 1
    xs = pl.BlockSpec((tm, D // 2), lambda w, blk, e, *_: (blk[w], 0))
    bias = lambda c: pl.BlockSpec((1, 1, c), lambda w, blk, e, *_: (e[w], 0, 0))
    hbm = pl.BlockSpec(memory_space=pl.ANY)
    return pl.pallas_call(
        _experts_kernel,
        grid_spec=pltpu.PrefetchScalarGridSpec(
            num_scalar_prefetch=len(items),
            grid=(n_items,),
            in_specs=[xs, hbm, bias(F), bias(F), hbm, bias(D), pl.BlockSpec(memory_space=pltpu.VMEM)],
            out_specs=xs,
            scratch_shapes=[pltpu.VMEM((2, D, 2 * F), F32), pltpu.VMEM((2, F, D), F32),
                            pltpu.VMEM((D, F), BF16), pltpu.VMEM((D, F), BF16),
                            pltpu.VMEM((F, D), BF16), pltpu.SemaphoreType.DMA((2, 2))],
        ),
        out_shape=jax.ShapeDtypeStruct(xg.shape, jnp.uint32),
        compiler_params=_cparams("arbitrary"),
        name="experts",
    )(*items, xg, w_gate_up, bg, bu, w_down, bd, _gate_up_split_matrix())


def _combine_kernel(x1_ref, yk_ref, w_ref, nw_ref, *rest):
    o_ref = rest[-1]
    w = w_ref[...]
    x = x1_ref[...]
    for k in range(TOP_K):
        x = x + jnp.concatenate(_unpack_bf16_pair(yk_ref[k]), axis=1) * w[:, k:k + 1]
    o_ref[...] = x * lax.rsqrt(jnp.mean(x * x, axis=-1, keepdims=True) + RMS_EPS) * nw_ref[...]


def _combine(x1, yk, w_tk, norm_w, out_rows, row0, prev_out):
    Tg, D = x1.shape
    tm = min(COMBINE_ROWS, Tg)
    row = pl.BlockSpec((tm, D), lambda i: (i, 0))
    in_specs = [row, pl.BlockSpec((TOP_K, tm, D // 2), lambda i: (0, i, 0)),
                pl.BlockSpec((tm, TOP_K), lambda i: (i, 0)),
                pl.BlockSpec((1, D), lambda i: (0, 0))]
    args = [x1, yk, w_tk, norm_w]
    aliases = {}
    if prev_out is not None:
        in_specs.append(pl.BlockSpec(memory_space=pl.ANY))
        args.append(prev_out)
        aliases = {len(args) - 1: 0}
    return pl.pallas_call(
        _combine_kernel,
        grid=(Tg // tm,),
        in_specs=in_specs,
        out_specs=pl.BlockSpec((tm, D), lambda i: (i + row0 // tm, 0)),
        out_shape=jax.ShapeDtypeStruct((out_rows, D), F32),
        input_output_aliases=aliases,
        compiler_params=_cparams("parallel"),
        name="combine",
    )(*args)


def _qk_column_order():
    half = ATT_HEAD_DIM // 2
    order = []
    for p in range(ATT_HEADS // 2):
        for part in range(2):
            for h in (2 * p, 2 * p + 1):
                order.extend(range(h * ATT_HEAD_DIM + part * half, h * ATT_HEAD_DIM + (part + 1) * half))
    return np.asarray(order, np.int32)


def _rope_tables(seq):
    half = ATT_HEAD_DIM // 2
    inv = ROPE_THETA ** (-(jnp.arange(half, dtype=F32) * 2.0 / ATT_HEAD_DIM))
    ang = jnp.arange(seq, dtype=F32)[:, None] * inv[None, :]
    cos, sin = jnp.cos(ang), jnp.sin(ang)
    return (jnp.concatenate([cos, cos, cos, cos], axis=1),
            jnp.concatenate([-sin, -sin, sin, sin], axis=1))


def kernel(x, norm1_w, w_in, moba_up, hgrn_lb_logits, hgrn_norm_w, hgrn_up, w_out, norm2_w,
           router_w, router_b, w_gate_up, b_gate_up, w_down, b_down, final_norm_w):
    B, S, D = x.shape
    T = B * S
    assert S % MOBA_BLOCK == 0 and w_in.shape[0] == 1
    x2 = x.reshape(T, D)

    perm = _qk_column_order()
    w0 = w_in[0]
    w_in_p = jnp.concatenate([w0[:, :ATT_WIDTH][:, perm], w0[:, ATT_WIDTH:2 * ATT_WIDTH][:, perm],
                              w0[:, 2 * ATT_WIDTH:]], axis=1).astype(BF16)
    cos_t, sin_t = _rope_tables(S)
    lb = jnp.cumsum(jax.nn.softmax(hgrn_lb_logits.astype(F32), axis=0), axis=0)[0:1]

    qa, ka, va, qb, fb, ib, gb, ga, gtb = _in_proj(x2, norm1_w, w_in_p, cos_t, sin_t, S)
    r3 = lambda a: a.reshape(B, S, a.shape[1])

    rw_t = router_w[0].T
    rw_hi = rw_t.astype(BF16)
    rw_lo = (rw_t - rw_hi.astype(F32)).astype(BF16)
    mix_w = (moba_up[0].astype(BF16), hgrn_up[0].astype(BF16), w_out[0].astype(BF16), norm2_w,
             jnp.concatenate([rw_hi, rw_hi, rw_lo], axis=1), router_b[0][:, None])
    expert_w = (w_gate_up[0], b_gate_up[0][:, None, 0::2], b_gate_up[0][:, None, 1::2],
                w_down[0], b_down[0][:, None, :])

    Tg = T // MOE_TOKEN_GROUPS
    n_parts = 1
    bp = B // n_parts
    staged = []
    anchor = jnp.zeros((1,), jnp.int32)
    for g in range(MOE_TOKEN_GROUPS):
        if g * n_parts % MOE_TOKEN_GROUPS == 0:
            part = g * n_parts // MOE_TOKEN_GROUPS
            ya = _moba(r3(qa), r3(ka), r3(va), part * bp, bp).reshape(bp * S, ATT_WIDTH)
            yb = _hgrn(r3(qb), r3(fb), r3(ib), r3(gb), lb, hgrn_norm_w,
                       part * bp, bp).reshape(bp * S, HGRN_WIDTH)
        x1, h2, top_e, top_w, rank, cnt = _mix_route(ya, yb, ga, gtb, x2, *mix_w,
                                                     g * Tg - part * bp * S, g * Tg, Tg, anchor)
        counts = cnt[:, 0].astype(jnp.int32)
        sb = EXPERT_SUB_ROWS
        padded = (counts + sb - 1) // sb * sb
        start = jnp.cumsum(padded) - padded
        dest = rank + jnp.sum(jnp.where(top_e[:, :, None] == jnp.arange(N_EXPERTS)[None, None, :],
                                        start[None, None, :], 0), axis=-1)
        body_rows = TOP_K * Tg + N_EXPERTS * sb
        spare_row = body_rows
        i = jnp.arange(sb, dtype=jnp.int32)[None, :]
        n_fill = (padded - counts)[:, None]
        pad_rows = jnp.where(n_fill > 0, (start + counts)[:, None] + i % jnp.maximum(n_fill, 1),
                             spare_row + i)
        xg = _scatter_rows(dest, h2, body_rows + EXPERT_ROWS, pad_rows)
        staged.append((xg, padded, dest, x1, top_w))
        anchor = dest[0, :1] + pad_rows[0, :1]

    out = None
    for g, (xg, padded, dest, x1, top_w) in enumerate(staged):
        yg = _experts(xg, padded, *expert_w)
        yk = _gather_rows(dest, yg).reshape(TOP_K, Tg, D // 2)
        out = _combine(x1, yk, top_w.T, final_norm_w[None, :], T, g * Tg, out)
    return out.reshape(B, S, D)
```

```python
import functools
import math

import numpy as np
import jax
import jax.numpy as jnp
from jax import lax
from jax.experimental import pallas as pl
from jax.experimental.pallas import tpu as pltpu
from jax.experimental.pallas import tpu_sc as plsc

ATT_HEADS = 8
ATT_HEAD_DIM = 64
ATT_WIDTH = ATT_HEADS * ATT_HEAD_DIM
MOBA_BLOCK = 256
MOBA_TOPK = 3
ROPE_THETA = 10000.0
HGRN_HEADS = 4
HGRN_DIM = 128
HGRN_WIDTH = HGRN_HEADS * HGRN_DIM
HGRN_CHUNK = 64
N_EXPERTS = 32
TOP_K = 4
SWIGLU_LIMIT = 7.0
SWIGLU_ALPHA = 1.702
RMS_EPS = 1e-6
NEG = -1e30

V7X_LANES = 128
V7X_SUBLANES = 8
V7X_MXU_DIM = 256
V7X_VMEM_LIMIT_BYTES = 56 * 1024 * 1024
V7X_SC_CORES = 2
V7X_SC_SUBCORES = 16

MOBA_PAIRS = 2
MOBA_GROUP = 4
PROJ_ROWS = 1024
PROJ_COLS = 512
HGRN_ROWS = 512
MIX_ROWS = 512
EXPERT_ROWS = 512
EXPERT_SUB_ROWS = 256
MOE_TOKEN_GROUPS = 2
SC_ROWS = 64
COMBINE_ROWS = 512

F32 = jnp.float32
BF16 = jnp.bfloat16


def _nt_dot(a, b, precision=None):
    return lax.dot_general(a, b, (((1,), (1,)), ((), ())), precision=precision,
                           preferred_element_type=F32)


def _tn_dot(a, b, precision=None):
    return lax.dot_general(a, b, (((0,), (0,)), ((), ())), precision=precision,
                           preferred_element_type=F32)


def _cparams(*sem):
    return pltpu.CompilerParams(dimension_semantics=sem, vmem_limit_bytes=V7X_VMEM_LIMIT_BYTES)


def _pack_bf16_pair(lo, hi):
    lo_bits = lax.bitcast_convert_type(lo.astype(BF16).astype(F32), jnp.uint32)
    hi_bits = lax.bitcast_convert_type(hi.astype(BF16).astype(F32), jnp.uint32)
    return (lo_bits >> 16) | hi_bits


def _unpack_bf16_pair(word):
    lo = lax.bitcast_convert_type(word << 16, F32)
    hi = lax.bitcast_convert_type(word & jnp.uint32(0xFFFF0000), F32)
    return lo, hi


def _in_proj_kernel(x_ref, nw_ref, w_ref, cos_ref, sin_ref,
                    qa_ref, ka_ref, va_ref, qb_ref, fb_ref, ib_ref, gb_ref, ga_ref, gtb_ref):
    x = x_ref[...]
    h = x * lax.rsqrt(jnp.mean(x * x, axis=-1, keepdims=True) + RMS_EPS) * nw_ref[...]
    h = h.astype(BF16)
    cos = cos_ref[...]
    sin = sin_ref[...]

    def proj(c):
        return jnp.dot(h, w_ref[:, c * PROJ_COLS:(c + 1) * PROJ_COLS], preferred_element_type=F32)

    def rope(t):
        out = []
        for j in range(PROJ_COLS // V7X_LANES):
            tj = t[:, j * V7X_LANES:(j + 1) * V7X_LANES]
            out.append(tj * cos + pltpu.roll(tj, V7X_LANES // 2, 1) * sin)
        return jnp.concatenate(out, axis=1)

    qa_ref[...] = (rope(proj(0)) * (ATT_HEAD_DIM ** -0.5)).astype(BF16)
    ka_ref[...] = rope(proj(1)).astype(BF16)
    va_ref[...] = proj(2).astype(BF16)
    qb_ref[...] = proj(3).astype(BF16)
    fb_ref[...] = proj(4)
    ib_ref[...] = proj(5).astype(BF16)
    gb_ref[...] = proj(6).astype(BF16)
    ga_ref[:, :PROJ_COLS] = proj(7).astype(BF16)
    ga_ref[:, PROJ_COLS:] = proj(8).astype(BF16)
    gtb_ref[:, :PROJ_COLS] = proj(9).astype(BF16)
    gtb_ref[:, PROJ_COLS:] = proj(10).astype(BF16)


def _in_proj(x2, norm_w, w_in_bf16, cos_t, sin_t, seq):
    T, D = x2.shape
    tm = min(PROJ_ROWS, seq)
    n_seq_tiles = seq // tm
    row = lambda w: pl.BlockSpec((tm, w), lambda i: (i, 0))
    tab = pl.BlockSpec((tm, V7X_LANES), lambda i: (i % n_seq_tiles, 0))
    widths = [ATT_WIDTH] * 3 + [HGRN_WIDTH] * 4 + [D, D]
    dtypes = [BF16, BF16, BF16, BF16, F32, BF16, BF16, BF16, BF16]
    return pl.pallas_call(
        _in_proj_kernel,
        grid=(T // tm,),
        in_specs=[row(D), pl.BlockSpec((1, D), lambda i: (0, 0)),
                  pl.BlockSpec(memory_space=pltpu.VMEM), tab, tab],
        out_specs=[row(w) for w in widths],
        out_shape=[jax.ShapeDtypeStruct((T, w), dt) for w, dt in zip(widths, dtypes)],
        compiler_params=_cparams("parallel"),
        name="in_proj",
    )(x2, norm_w, w_in_bf16, cos_t, sin_t)


def _moba_kernel(q_ref, k_ref, v_ref, o_ref, kaug_ref, kmean_ref, vt_ref, acc_ref, *s_refs, nb):
    qi = pl.program_id(2)
    blk = MOBA_BLOCK
    lanes = V7X_LANES
    n_pairs = kaug_ref.shape[0]
    heads = [(pp, hh) for pp in range(n_pairs) for hh in range(2)]
    nbp = kmean_ref.shape[1]

    @pl.when(qi == 0)
    def _():
        rowb = lax.broadcasted_iota(jnp.int32, (nb * blk, lanes), 0) // blk
        col = lax.broadcasted_iota(jnp.int32, (nb * blk, lanes), 1)
        onehot = jnp.where(rowb == col, 1.0, 0.0).astype(BF16)
        for pp in range(n_pairs):
            pl_ = slice(pp * lanes, (pp + 1) * lanes)
            kaug_ref[pp, :, :lanes] = k_ref[0, :, pl_]
            kaug_ref[pp, :, lanes:] = onehot
            means = [jnp.sum(k_ref[0, n * blk:(n + 1) * blk, pl_].astype(F32), axis=0,
                             keepdims=True) * (1.0 / blk) for n in range(nb)]
            km = jnp.concatenate(means + [jnp.zeros((nbp - nb, lanes), F32)] * (nbp > nb), axis=0)
            hi = km.astype(BF16)
            rem = km - hi.astype(F32)
            mid = rem.astype(BF16)
            lo = (rem - mid.astype(F32)).astype(BF16)
            kmean_ref[pp] = jnp.concatenate([hi, mid, lo], axis=1)
            for c in range(nb):
                vt = v_ref[0, c * blk:(c + 1) * blk, pl_].astype(F32).T
                vt_ref[pp, :lanes, c * blk:(c + 1) * blk] = vt.astype(BF16)
            vt_ref[pp, lanes:, :] = jnp.ones((vt_ref.shape[1] - lanes, nb * blk), BF16)

    feat = lax.broadcasted_iota(jnp.int32, (lanes, blk), 0)
    key_i = lax.broadcasted_iota(jnp.int32, (blk, blk), 0)
    qry_i = lax.broadcasted_iota(jnp.int32, (blk, blk), 1)
    own = pl.multiple_of(qi * blk, blk)
    blk_id = lax.broadcasted_iota(jnp.int32, (nbp, 2 * blk), 0)
    slab = 2 * V7X_SUBLANES

    def slab_max(s):
        return jnp.max(s.reshape(s.shape[0] // slab, slab, blk), axis=0)

    q_aug, m_init = [], []
    for pp in range(n_pairs):
        qt = q_ref[0, :, pp * lanes:(pp + 1) * lanes].astype(F32).T
        k_own = kaug_ref[pp, pl.ds(own, blk), :lanes]
        qhs = [jnp.where((feat // (ATT_HEAD_DIM // 2)) % 2 == hh, qt, 0.0).astype(BF16)
               for hh in range(2)]
        q2 = jnp.concatenate(qhs, axis=1)
        gate = jnp.dot(kmean_ref[pp], jnp.concatenate([q2, q2, q2], axis=0),
                       preferred_element_type=F32)
        gate = jnp.where(blk_id < qi, gate, NEG)
        beaten = jnp.zeros((nbp, 2 * blk), F32)
        for n in range(nb):
            gn = gate[n:n + 1, :]
            wins = (gn > gate) | ((gn == gate) & (blk_id > n))
            beaten = beaten + jnp.where(wins, 1.0, 0.0)
        sel = (beaten < MOBA_TOPK) & (blk_id < qi)
        bias = jnp.where(sel, 0.0, NEG).astype(BF16)
        bias = jnp.concatenate([bias, jnp.zeros((lanes - nbp, 2 * blk), BF16)], axis=0)
        for hh in range(2):
            h = 2 * pp + hh
            q_aug.append(jnp.concatenate([qhs[hh], bias[:, hh * blk:(hh + 1) * blk]], axis=0))
            s = jnp.dot(k_own, qhs[hh], preferred_element_type=F32)
            s = jnp.where(key_i <= qry_i, s, NEG)
            m0 = jnp.max(slab_max(s), axis=0, keepdims=True)
            p = jnp.exp(s - m0).astype(BF16)
            acc_ref[h] = jnp.dot(vt_ref[pp, :, pl.ds(own, blk)], p, preferred_element_type=F32)
            m_init.append(m0)

    group = MOBA_GROUP * blk
    n_groups = (qi + MOBA_GROUP) // MOBA_GROUP

    def score_head(g, h):
        off = pl.multiple_of(g * group, group)
        s = jnp.dot(kaug_ref[h // 2, pl.ds(off, group), :], q_aug[h], preferred_element_type=F32)
        s_refs[h][pl.ds(off, group), :] = s
        return jnp.max(slab_max(s), axis=0, keepdims=True)

    def value_head(g, h, gmax, m_old):
        off = pl.multiple_of(g * group, group)
        m_new = jnp.maximum(m_old, gmax)
        alpha = jnp.exp(m_old - m_new)
        p = jnp.exp(s_refs[h][pl.ds(off, group), :] - m_new).astype(BF16)
        acc_ref[h] = alpha * acc_ref[h] + jnp.dot(vt_ref[h // 2, :, pl.ds(off, group)], p,
                                                  preferred_element_type=F32)
        return m_new

    def pipelined(g, carry):
        gmax, ms = carry
        last = len(heads) - 1
        new_gmax, new_ms = [None] * (last + 1), [None] * (last + 1)
        new_gmax[last] = score_head(g + 1, last)
        for h in range(last + 1):
            new_ms[h] = value_head(g, h, gmax[h], ms[h])
            if h < last:
                new_gmax[h] = score_head(g + 1, h)
        return tuple(new_gmax), tuple(new_ms)

    first = tuple(score_head(0, h) for h in range(len(heads)))
    gmax, ms = lax.fori_loop(0, n_groups - 1, pipelined, (first, tuple(m_init)))
    for h in range(len(heads)):
        value_head(n_groups - 1, h, gmax[h], ms[h])

    half = lanes // 2
    outs = []
    for pp in range(n_pairs):
        outs.append(acc_ref[2 * pp, :half, :] / acc_ref[2 * pp, lanes:lanes + 1, :])
        outs.append(acc_ref[2 * pp + 1, half:lanes, :] / acc_ref[2 * pp + 1, lanes:lanes + 1, :])
    o_ref[0] = jnp.concatenate(outs, axis=0).T.astype(BF16)


def _moba(q, k, v):
    B, S, _ = q.shape
    nb = S // MOBA_BLOCK
    assert nb % MOBA_GROUP == 0
    nbp = -(-nb // (2 * V7X_SUBLANES)) * (2 * V7X_SUBLANES)
    n_pairs = ATT_WIDTH // V7X_LANES
    pp = MOBA_PAIRS
    assert n_pairs % pp == 0
    vt_rows = V7X_LANES + 2 * V7X_SUBLANES
    qspec = pl.BlockSpec((1, MOBA_BLOCK, pp * V7X_LANES), lambda b, p, i: (b, i, p))
    kvspec = pl.BlockSpec((1, S, pp * V7X_LANES), lambda b, p, i: (b, 0, p))
    return pl.pallas_call(
        functools.partial(_moba_kernel, nb=nb),
        grid=(B, n_pairs // pp, nb),
        in_specs=[qspec, kvspec, kvspec],
        out_specs=qspec,
        out_shape=jax.ShapeDtypeStruct((B, S, ATT_WIDTH), BF16),
        scratch_shapes=[pltpu.VMEM((pp, S, 2 * V7X_LANES), BF16),
                        pltpu.VMEM((pp, nbp, 3 * V7X_LANES), BF16),
                        pltpu.VMEM((pp, vt_rows, S), BF16),
                        pltpu.VMEM((2 * pp, vt_rows, MOBA_BLOCK), F32)]
                       + [pltpu.VMEM((S, MOBA_BLOCK), F32)] * (2 * pp),
        compiler_params=_cparams("parallel", "parallel", "arbitrary"),
        name="moba",
    )(q, k, v)


def _hgrn_level_sizes(chunk):
    return [chunk >> (i + 1) for i in range(int(math.log2(chunk)))]


def _hgrn_constants(chunk):
    t = np.arange(chunk)
    mats = [(t[None, :] <= t[:, None]),
            (t[None, :] > t[:, None])]
    qrows, pmasks = [], []
    for bs in _hgrn_level_sizes(chunk):
        blk = t // bs
        odd = (blk % 2) == 1
        lo, hi = blk * bs, (blk + 1) * bs
        u = t[None, :]
        m_odd = (u >= lo[:, None]) & (u <= t[:, None])
        m_even = (u > t[:, None]) & (u < hi[:, None])
        mats.append(np.where(odd[:, None], m_odd, m_even))
        qrows.append(odd)
        pmasks.append(odd[:, None] & (blk[None, :] == blk[:, None] - 1))
    pmasks.append(t[None, :] == t[:, None])
    summat = np.concatenate(mats, axis=0).astype(np.float32)
    qrow = np.stack(qrows, axis=0).astype(np.float32)
    pmask = np.stack(pmasks, axis=0).astype(np.float32)
    return summat, qrow, pmask


def _hgrn_kernel(q_ref, f_ref, i_ref, g_ref, lb_ref, nw_ref, sm_ref, qrow_ref, pm_ref,
                 o_ref, state_ref, *, rows):
    C = HGRN_CHUNK
    n_levels = qrow_ref.shape[0]

    @pl.when(pl.program_id(1) == 0)
    def _():
        state_ref[...] = jnp.zeros_like(state_ref)

    lb = lb_ref[...]
    summat = sm_ref[...]
    for c in range(rows // C):
        rs = slice(c * C, (c + 1) * C)
        fg = lb + (1.0 - lb) * jax.nn.sigmoid(f_ref[0, rs, :])
        logf = jnp.log(fg)
        hi = logf.astype(BF16)
        rem = logf - hi.astype(F32)
        mid = rem.astype(BF16)
        lo = (rem - mid.astype(F32)).astype(BF16)
        sums = jnp.dot(summat, jnp.concatenate([hi, mid, lo], axis=0),
                       preferred_element_type=F32)
        for h in range(HGRN_HEADS):
            ls = slice(h * HGRN_DIM, (h + 1) * HGRN_DIM)
            qf = jax.nn.silu(q_ref[0, rs, ls].astype(F32))
            kf = 1.0 - fg[:, ls]
            iv = i_ref[0, rs, ls]
            bcum = sums[0:C, ls]
            bsuf = sums[C:2 * C, ls]
            att = _nt_dot(qf.astype(BF16), kf.astype(BF16)) * pm_ref[n_levels]
            for lv in range(n_levels):
                w = jnp.exp(sums[(2 + lv) * C:(3 + lv) * C, ls])
                qrow = qrow_ref[lv]
                z = (jnp.where(qrow > 0.5, qf, kf) * w).astype(BF16)
                att = att + _nt_dot(z, z) * pm_ref[lv]
            o = jnp.dot(att.astype(BF16), iv, preferred_element_type=F32)
            st = state_ref[h]
            o = o + _nt_dot((qf * jnp.exp(bcum)).astype(BF16), st.astype(BF16))
            kdec = (kf * jnp.exp(bsuf)).astype(BF16)
            state_ref[h] = st * jnp.exp(bcum[C - 1:C, :]) + _tn_dot(iv, kdec)
            o = o * lax.rsqrt(jnp.mean(o * o, axis=-1, keepdims=True) + RMS_EPS)
            o = o * nw_ref[:, ls] * jax.nn.silu(g_ref[0, rs, ls].astype(F32))
            o_ref[0, rs, ls] = o.astype(BF16)


def _hgrn(qb, fb, ib, gb, lb, norm_w):
    B, S, W = qb.shape
    rows = min(HGRN_ROWS, S)
    summat, qrow, pmask = _hgrn_constants(HGRN_CHUNK)
    summat = jnp.asarray(np.concatenate([summat] * 3, axis=1), BF16)
    n_levels = qrow.shape[0]
    blk = pl.BlockSpec((1, rows, W), lambda b, s: (b, s, 0))
    vec = pl.BlockSpec((1, W), lambda b, s: (0, 0))
    const = lambda a: pl.BlockSpec(a.shape, lambda b, s: (0,) * a.ndim)
    qrow3 = qrow.reshape(n_levels, HGRN_CHUNK, 1)
    return pl.pallas_call(
        functools.partial(_hgrn_kernel, rows=rows),
        grid=(B, S // rows),
        in_specs=[blk, blk, blk, blk, vec, vec, const(summat), const(qrow3), const(pmask)],
        out_specs=blk,
        out_shape=jax.ShapeDtypeStruct((B, S, W), BF16),
        scratch_shapes=[pltpu.VMEM((HGRN_HEADS, HGRN_DIM, HGRN_DIM), F32)],
        compiler_params=_cparams("parallel", "arbitrary"),
        name="hgrn",
    )(qb, fb, ib, gb, lb, norm_w, jnp.asarray(summat), jnp.asarray(qrow3), jnp.asarray(pmask))


def _mix_route_kernel(ya_ref, yb_ref, ga_ref, gb_ref, x_ref, wa_ref, wb_ref, wo_ref, nw_ref,
                      rw_ref, rb_ref, tri_ref, anchor_ref,
                      x1_ref, h2_ref, e_ref, w_ref, rank_ref, cnt_ref, carry_ref):
    @pl.when(pl.program_id(0) == 0)
    def _():
        carry_ref[...] = jnp.zeros_like(carry_ref)

    ua = jnp.dot(ya_ref[...], wa_ref[...], preferred_element_type=F32)
    ub = jnp.dot(yb_ref[...], wb_ref[...], preferred_element_type=F32)
    mixed = (jax.nn.sigmoid(ga_ref[...].astype(F32)) * ua
             + jax.nn.sigmoid(gb_ref[...].astype(F32)) * ub)
    x1 = x_ref[...] + jnp.dot(mixed.astype(BF16), wo_ref[...], preferred_element_type=F32)
    x1_ref[...] = x1
    h2 = x1 * lax.rsqrt(jnp.mean(x1 * x1, axis=-1, keepdims=True) + RMS_EPS) * nw_ref[...]
    half = h2.shape[1] // 2
    h2_ref[...] = _pack_bf16_pair(h2[:, :half], h2[:, half:])

    tm = x1.shape[0]
    h2_hi = h2.astype(BF16)
    h2_lo = (h2 - h2_hi.astype(F32)).astype(BF16)
    logits = _nt_dot(rw_ref[...], jnp.concatenate([h2_hi, h2_lo, h2_hi], axis=1)) + rb_ref[...]
    eid = lax.broadcasted_iota(jnp.int32, (N_EXPERTS, tm), 0)
    work = logits
    es, vs = [], []
    for _ in range(TOP_K):
        mx = jnp.max(work, axis=0, keepdims=True)
        idx = jnp.min(jnp.where(work == mx, eid, N_EXPERTS), axis=0, keepdims=True)
        es.append(idx)
        vs.append(mx)
        work = jnp.where(eid == idx, -jnp.inf, work)
    ex = [jnp.exp(v - vs[0]) for v in vs]
    den = ex[0] + ex[1] + ex[2] + ex[3]
    multi = jnp.zeros((N_EXPERTS, tm), F32)
    for k in range(TOP_K):
        multi = multi + jnp.where(eid == es[k], 1.0, 0.0)
    before = jnp.dot(multi.astype(BF16), tri_ref[...], preferred_element_type=F32) + carry_ref[...]
    for k in range(TOP_K):
        e_ref[k:k + 1, :] = es[k]
        w_ref[k:k + 1, :] = ex[k] / den
        rank_ref[k:k + 1, :] = jnp.sum(jnp.where(eid == es[k], before, 0.0), axis=0,
                                       keepdims=True).astype(jnp.int32)
    carry_ref[...] = carry_ref[...] + jnp.sum(multi, axis=1, keepdims=True)
    cnt_ref[...] = jnp.broadcast_to(carry_ref[...], cnt_ref.shape)


def _mix_route(ya, yb, ga, gb, x2, wa, wb, wo, norm_w, rw_t, rb, row0, rows, anchor):
    D = x2.shape[1]
    T = rows
    tm = min(MIX_ROWS, T)
    row = lambda w: pl.BlockSpec((tm, w), lambda i: (i + row0 // tm, 0))
    out_row = pl.BlockSpec((tm, D), lambda i: (i, 0))
    whole = pl.BlockSpec(memory_space=pltpu.VMEM)
    kt = pl.BlockSpec((TOP_K, tm), lambda i: (0, i))
    tri = jnp.asarray(np.triu(np.ones((tm, tm), np.float32), 1), BF16)
    return pl.pallas_call(
        _mix_route_kernel,
        grid=(T // tm,),
        in_specs=[row(ATT_WIDTH), row(HGRN_WIDTH), row(D), row(D), row(D),
                  whole, whole, whole, pl.BlockSpec((1, D), lambda i: (0, 0)),
                  whole, whole, whole, pl.BlockSpec(memory_space=pltpu.SMEM)],
        out_specs=[out_row, pl.BlockSpec((tm, D // 2), lambda i: (i, 0)), kt, kt, kt,
                   pl.BlockSpec((N_EXPERTS, V7X_LANES), lambda i: (0, 0))],
        out_shape=[jax.ShapeDtypeStruct((T, D), F32), jax.ShapeDtypeStruct((T, D // 2), jnp.uint32),
                   jax.ShapeDtypeStruct((TOP_K, T), jnp.int32),
                   jax.ShapeDtypeStruct((TOP_K, T), F32),
                   jax.ShapeDtypeStruct((TOP_K, T), jnp.int32),
                   jax.ShapeDtypeStruct((N_EXPERTS, V7X_LANES), F32)],
        scratch_shapes=[pltpu.VMEM((N_EXPERTS, 1), F32)],
        compiler_params=_cparams("arbitrary"),
        name="mix_route",
    )(ya, yb, ga, gb, x2, wa, wb, wo, norm_w, rw_t, rb, tri, anchor)


def _sc_worker_id():
    return lax.axis_index("s") * V7X_SC_CORES + lax.axis_index("c")


def _sc_kernel(body, out_rows, like, window, name, extra_scratch=()):
    mesh = plsc.VectorSubcoreMesh(core_axis_name="c", subcore_axis_name="s")
    return pl.kernel(
        body, mesh=mesh,
        out_type=jax.ShapeDtypeStruct((out_rows,) + like.shape[1:], like.dtype),
        scratch_types=[pltpu.VMEM((window,), jnp.int32),
                       pltpu.VMEM((window,) + like.shape[1:], like.dtype), *extra_scratch],
        name=name)


def _scatter_rows(dest, h2, out_rows, pad_rows):
    T = dest.shape[1]
    n_workers = V7X_SC_CORES * V7X_SC_SUBCORES
    per_worker = T // n_workers
    window = min(SC_ROWS, per_worker)
    n_pad = pad_rows.shape[1]
    assert pad_rows.shape[0] == n_workers and n_pad % window == 0

    def body(dest_hbm, src_hbm, pad_hbm, zero_hbm, dst_hbm, idx_v, rows_v, pad_v, zero_v, all_v, sem,
             fill_sem):
        del idx_v
        wid = _sc_worker_id()
        base = wid * per_worker

        pltpu.sync_copy(pad_hbm.at[pl.ds(wid * n_pad, n_pad)], pad_v)
        pltpu.sync_copy(zero_hbm, zero_v)
        fills = [pltpu.async_copy(zero_v, dst_hbm.at[pad_v.at[pl.ds(c * window, window)]],
                                     fill_sem)
                 for c in range(n_pad // window)]

        for k in range(TOP_K):
            pltpu.sync_copy(dest_hbm.at[pl.ds(k * T + base, per_worker)],
                            all_v.at[pl.ds(k * per_worker, per_worker)])

        @pl.loop(0, per_worker // window)
        def _(c):
            pltpu.sync_copy(src_hbm.at[pl.ds(base + c * window, window)], rows_v)
            copies = [pltpu.async_copy(
                rows_v, dst_hbm.at[all_v.at[pl.ds(k * per_worker + c * window, window)]], sem)
                for k in range(TOP_K)]
            for cp in copies:
                cp.wait()

        for f in fills:
            f.wait()

    zeros = jnp.zeros((window,) + h2.shape[1:], h2.dtype)
    extra = [pltpu.VMEM((n_pad,), jnp.int32), pltpu.VMEM(zeros.shape, zeros.dtype),
             pltpu.VMEM((TOP_K * per_worker,), jnp.int32), pltpu.SemaphoreType.DMA,
             pltpu.SemaphoreType.DMA]
    return _sc_kernel(body, out_rows, h2, window, "scatter_rows", extra)(
        dest.reshape(-1), h2, pad_rows.reshape(-1), zeros)


def _gather_rows(dest, yg):
    T = dest.shape[1]
    n_workers = V7X_SC_CORES * V7X_SC_SUBCORES
    per_worker = TOP_K * T // n_workers
    window = min(SC_ROWS, per_worker)

    def body(dest_hbm, src_hbm, dst_hbm, idx_v, rows_v, all_v):
        del idx_v
        base = _sc_worker_id() * per_worker
        pltpu.sync_copy(dest_hbm.at[pl.ds(base, per_worker)], all_v)

        @pl.loop(0, per_worker // window)
        def _(c):
            pltpu.sync_copy(src_hbm.at[all_v.at[pl.ds(c * window, window)]], rows_v)
            pltpu.sync_copy(rows_v, dst_hbm.at[pl.ds(base + c * window, window)])

    extra = [pltpu.VMEM((per_worker,), jnp.int32)]
    return _sc_kernel(body, TOP_K * T, yg, window, "gather_rows", extra)(dest.reshape(-1), yg)


def _experts_kernel(blk_ref, exp_ref, lo_ref, hi_ref, slot_ref, efirst_ref, enext_ref,
                    x_ref, wgu_hbm, bg_ref, bu_ref, wd_hbm, bd_ref, perm_ref, y_ref,
                    wgu_stage, wd_stage, wg_buf, wu_buf, wd_buf, sems):
    w = pl.program_id(0)
    lo = lo_ref[w]
    hi = hi_ref[w]
    slot = slot_ref[w]

    tm = x_ref.shape[0]
    n_out = V7X_MXU_DIM

    def weight_copies(e, s):
        return [pltpu.make_async_copy(wgu_hbm.at[e], wgu_stage.at[s], sems.at[s, 0]),
                pltpu.make_async_copy(wd_hbm.at[e], wd_stage.at[s], sems.at[s, 1])]

    @pl.when(w == 0)
    def _():
        for c in weight_copies(exp_ref[0], slot):
            c.start()

    @pl.when(efirst_ref[w] == 1)
    def _():
        for c in weight_copies(exp_ref[w], slot):
            c.wait()

        @pl.when(enext_ref[w] >= 0)
        def _():
            for c in weight_copies(enext_ref[w], 1 - slot):
                c.start()

        half = V7X_MXU_DIM // 2
        for g in range(wgu_stage.shape[2] // V7X_MXU_DIM):
            wb = wgu_stage[slot, :, g * V7X_MXU_DIM:(g + 1) * V7X_MXU_DIM].astype(BF16)
            d = jnp.dot(wb, perm_ref[...], preferred_element_type=F32).astype(BF16)
            wg_buf[:, g * half:(g + 1) * half] = d[:, :half]
            wu_buf[:, g * half:(g + 1) * half] = d[:, half:]
        wd_buf[...] = wd_stage[slot].astype(BF16)

    sb = EXPERT_SUB_ROWS

    def ffn(r0):
        rs = slice(r0, r0 + sb)
        x = jnp.concatenate(_unpack_bf16_pair(x_ref[rs, :]), axis=1).astype(BF16)
        g = jnp.dot(x, wg_buf[...], preferred_element_type=F32) + bg_ref[0]
        u = jnp.dot(x, wu_buf[...], preferred_element_type=F32) + bu_ref[0]
        g = jnp.minimum(g, SWIGLU_LIMIT)
        u = jnp.clip(u, -SWIGLU_LIMIT, SWIGLU_LIMIT)
        act = ((u + 1.0) * (g * jax.nn.sigmoid(g * SWIGLU_ALPHA))).astype(BF16)
        half = y_ref.shape[1]

        def down(c0):
            cs = slice(c0, c0 + n_out)
            return (jnp.dot(act, wd_buf[:, cs], preferred_element_type=F32)
                    + bd_ref[0, :, cs])

        for c in range(half // n_out):
            cs = slice(c * n_out, (c + 1) * n_out)
            y_ref[rs, cs] = _pack_bf16_pair(down(c * n_out), down(half + c * n_out))

    n_sub = tm // sb
    fused = (lo == 0) & (hi == tm)

    @pl.when(fused)
    def _():
        for j in range(n_sub):
            ffn(j * sb)

    for j in range(n_sub):
        @pl.when(jnp.logical_not(fused) & (lo <= j * sb) & (hi > j * sb))
        def _():
            ffn(j * sb)


def _work_items(counts, n_rows, tm):
    nblk = n_rows // tm
    n_items = nblk + N_EXPERTS - 1
    end = jnp.cumsum(counts)
    start = end - counts
    fb = start // tm
    nitems = jnp.where(counts > 0, (end - 1) // tm - fb + 1, 0)
    item_end = jnp.cumsum(nitems)
    item_start = item_end - nitems
    w = jnp.arange(n_items, dtype=jnp.int32)
    valid = w < item_end[-1]
    wc = jnp.minimum(w, item_end[-1] - 1)
    e = jnp.sum(wc[:, None] >= item_end[None, :], axis=1).astype(jnp.int32)
    e = jnp.minimum(e, N_EXPERTS - 1)
    onehot = e[:, None] == jnp.arange(N_EXPERTS, dtype=jnp.int32)[None, :]
    pick = lambda table: jnp.sum(jnp.where(onehot, table[None, :], 0), axis=1)
    blk = (pick(fb) + (wc - pick(item_start))).astype(jnp.int32)
    lo = jnp.maximum(pick(start), blk * tm) - blk * tm
    hi = jnp.minimum(pick(end), (blk + 1) * tm) - blk * tm
    lo = jnp.where(valid, lo, 0).astype(jnp.int32)
    hi = jnp.where(valid, hi, 0).astype(jnp.int32)
    used = counts > 0
    ids = jnp.arange(N_EXPERTS, dtype=jnp.int32)
    slot = pick((jnp.cumsum(used) - 1) % 2).astype(jnp.int32)
    efirst = (valid & (w == pick(item_start))).astype(jnp.int32)
    later = jnp.where(used[None, :] & (ids[None, :] > ids[:, None]), ids[None, :], N_EXPERTS)
    next_used = jnp.min(later, axis=1)
    enext = pick(jnp.where(next_used < N_EXPERTS, next_used, -1)).astype(jnp.int32)
    return blk, e, lo, hi, slot, efirst, enext


def _gate_up_split_matrix():
    i = np.arange(V7X_MXU_DIM)
    src = np.where(i < V7X_MXU_DIM // 2, 2 * i, 2 * (i - V7X_MXU_DIM // 2) + 1)
    perm = np.zeros((V7X_MXU_DIM, V7X_MXU_DIM), np.float32)
    perm[src, i] = 1.0
    return jnp.asarray(perm, BF16)


def _experts(xg, counts, w_gate_up, bg, bu, w_down, bd):
    A = xg.shape[0]
    D = 2 * xg.shape[1]
    tm = min(EXPERT_ROWS, A)
    F = w_down.shape[1]
    items = _work_items(counts, A, tm)
    n_items = A // tm + N_EXPERTS - 1
    xs = pl.BlockSpec((tm, D // 2), lambda w, blk, e, *_: (blk[w], 0))
    bias = lambda c: pl.BlockSpec((1, 1, c), lambda w, blk, e, *_: (e[w], 0, 0))
    hbm = pl.BlockSpec(memory_space=pl.ANY)
    return pl.pallas_call(
        _experts_kernel,
        grid_spec=pltpu.PrefetchScalarGridSpec(
            num_scalar_prefetch=len(items),
            grid=(n_items,),
            in_specs=[xs, hbm, bias(F), bias(F), hbm, bias(D), pl.BlockSpec(memory_space=pltpu.VMEM)],
            out_specs=xs,
            scratch_shapes=[pltpu.VMEM((2, D, 2 * F), F32), pltpu.VMEM((2, F, D), F32),
                            pltpu.VMEM((D, F), BF16), pltpu.VMEM((D, F), BF16),
                            pltpu.VMEM((F, D), BF16), pltpu.SemaphoreType.DMA((2, 2))],
        ),
        out_shape=jax.ShapeDtypeStruct(xg.shape, jnp.uint32),
        compiler_params=_cparams("arbitrary"),
        name="experts",
    )(*items, xg, w_gate_up, bg, bu, w_down, bd, _gate_up_split_matrix())


def _combine_kernel(x1_ref, yk_ref, w_ref, nw_ref, *rest):
    o_ref = rest[-1]
    w = w_ref[...]
    x = x1_ref[...]
    for k in range(TOP_K):
        x = x + jnp.concatenate(_unpack_bf16_pair(yk_ref[k]), axis=1) * w[:, k:k + 1]
    o_ref[...] = x * lax.rsqrt(jnp.mean(x * x, axis=-1, keepdims=True) + RMS_EPS) * nw_ref[...]


def _combine(x1, yk, w_tk, norm_w, out_rows, row0, prev_out):
    Tg, D = x1.shape
    tm = min(COMBINE_ROWS, Tg)
    row = pl.BlockSpec((tm, D), lambda i: (i, 0))
    in_specs = [row, pl.BlockSpec((TOP_K, tm, D // 2), lambda i: (0, i, 0)),
                pl.BlockSpec((tm, TOP_K), lambda i: (i, 0)),
                pl.BlockSpec((1, D), lambda i: (0, 0))]
    args = [x1, yk, w_tk, norm_w]
    aliases = {}
    if prev_out is not None:
        in_specs.append(pl.BlockSpec(memory_space=pl.ANY))
        args.append(prev_out)
        aliases = {len(args) - 1: 0}
    return pl.pallas_call(
        _combine_kernel,
        grid=(Tg // tm,),
        in_specs=in_specs,
        out_specs=pl.BlockSpec((tm, D), lambda i: (i + row0 // tm, 0)),
        out_shape=jax.ShapeDtypeStruct((out_rows, D), F32),
        input_output_aliases=aliases,
        compiler_params=_cparams("parallel"),
        name="combine",
    )(*args)


def _qk_column_order():
    half = ATT_HEAD_DIM // 2
    order = []
    for p in range(ATT_HEADS // 2):
        for part in range(2):
            for h in (2 * p, 2 * p + 1):
                order.extend(range(h * ATT_HEAD_DIM + part * half, h * ATT_HEAD_DIM + (part + 1) * half))
    return np.asarray(order, np.int32)


def _rope_tables(seq):
    half = ATT_HEAD_DIM // 2
    inv = ROPE_THETA ** (-(jnp.arange(half, dtype=F32) * 2.0 / ATT_HEAD_DIM))
    ang = jnp.arange(seq, dtype=F32)[:, None] * inv[None, :]
    cos, sin = jnp.cos(ang), jnp.sin(ang)
    return (jnp.concatenate([cos, cos, cos, cos], axis=1),
            jnp.concatenate([-sin, -sin, sin, sin], axis=1))


def kernel(x, norm1_w, w_in, moba_up, hgrn_lb_logits, hgrn_norm_w, hgrn_up, w_out, norm2_w,
           router_w, router_b, w_gate_up, b_gate_up, w_down, b_down, final_norm_w):
    B, S, D = x.shape
    T = B * S
    assert S % MOBA_BLOCK == 0 and w_in.shape[0] == 1
    x2 = x.reshape(T, D)

    perm = _qk_column_order()
    w0 = w_in[0]
    w_in_p = jnp.concatenate([w0[:, :ATT_WIDTH][:, perm], w0[:, ATT_WIDTH:2 * ATT_WIDTH][:, perm],
                              w0[:, 2 * ATT_WIDTH:]], axis=1).astype(BF16)
    cos_t, sin_t = _rope_tables(S)
    lb = jnp.cumsum(jax.nn.softmax(hgrn_lb_logits.astype(F32), axis=0), axis=0)[0:1]

    qa, ka, va, qb, fb, ib, gb, ga, gtb = _in_proj(x2, norm1_w, w_in_p, cos_t, sin_t, S)
    r3 = lambda a: a.reshape(B, S, a.shape[1])

    rw_t = router_w[0].T
    rw_hi = rw_t.astype(BF16)
    rw_lo = (rw_t - rw_hi.astype(F32)).astype(BF16)
    mix_w = (moba_up[0].astype(BF16), hgrn_up[0].astype(BF16), w_out[0].astype(BF16), norm2_w,
             jnp.concatenate([rw_hi, rw_hi, rw_lo], axis=1), router_b[0][:, None])
    expert_w = (w_gate_up[0], b_gate_up[0][:, None, 0::2], b_gate_up[0][:, None, 1::2],
                w_down[0], b_down[0][:, None, :])

    ya = _moba(r3(qa), r3(ka), r3(va)).reshape(T, ATT_WIDTH)
    yb = _hgrn(r3(qb), r3(fb), r3(ib), r3(gb), lb, hgrn_norm_w).reshape(T, HGRN_WIDTH)

    Tg = T // MOE_TOKEN_GROUPS
    staged = []
    anchor = jnp.zeros((1,), jnp.int32)
    for g in range(MOE_TOKEN_GROUPS):
        x1, h2, top_e, top_w, rank, cnt = _mix_route(ya, yb, ga, gtb, x2, *mix_w, g * Tg, Tg,
                                                     anchor)
        counts = cnt[:, 0].astype(jnp.int32)
        sb = EXPERT_SUB_ROWS
        padded = (counts + sb - 1) // sb * sb
        start = jnp.cumsum(padded) - padded
        dest = rank + jnp.sum(jnp.where(top_e[:, :, None] == jnp.arange(N_EXPERTS)[None, None, :],
                                        start[None, None, :], 0), axis=-1)
        body_rows = TOP_K * Tg + N_EXPERTS * sb
        spare_row = body_rows
        i = jnp.arange(sb, dtype=jnp.int32)[None, :]
        n_fill = (padded - counts)[:, None]
        pad_rows = jnp.where(n_fill > 0, (start + counts)[:, None] + i % jnp.maximum(n_fill, 1),
                             spare_row + i)
        xg = _scatter_rows(dest, h2, body_rows + EXPERT_ROWS, pad_rows)
        staged.append((xg, padded, dest, x1, top_w))
        anchor = dest[0, :1] + pad_rows[0, :1]

    out = None
    for g, (xg, padded, dest, x1, top_w) in enumerate(staged):
        yg = _experts(xg, padded, *expert_w)
        yk = _gather_rows(dest, yg).reshape(TOP_K, Tg, D // 2)
        out = _combine(x1, yk, top_w.T, final_norm_w[None, :], T, g * Tg, out)
    return out.reshape(B, S, D)
```

```python
import functools
import math

import numpy as np
import jax
import jax.numpy as jnp
from jax import lax
from jax.experimental import pallas as pl
from jax.experimental.pallas import tpu as pltpu
from jax.experimental.pallas import tpu_sc as plsc

ATT_HEADS = 8
ATT_HEAD_DIM = 64
ATT_WIDTH = ATT_HEADS * ATT_HEAD_DIM
MOBA_BLOCK = 256
MOBA_TOPK = 3
ROPE_THETA = 10000.0
HGRN_HEADS = 4
HGRN_DIM = 128
HGRN_WIDTH = HGRN_HEADS * HGRN_DIM
HGRN_CHUNK = 64
N_EXPERTS = 32
TOP_K = 4
SWIGLU_LIMIT = 7.0
SWIGLU_ALPHA = 1.702
RMS_EPS = 1e-6
NEG = -1e30

V7X_LANES = 128
V7X_SUBLANES = 8
V7X_MXU_DIM = 256
V7X_VMEM_LIMIT_BYTES = 56 * 1024 * 1024
V7X_SC_CORES = 2
V7X_SC_SUBCORES = 16

MOBA_PAIRS = 2
MOBA_GROUP = 4
PROJ_ROWS = 1024
PROJ_COLS = 512
HGRN_ROWS = 1024
MIX_ROWS = 512
EXPERT_ROWS = 512
EXPERT_SUB_ROWS = 256
MOE_TOKEN_GROUPS = 2
SC_ROWS = 64
COMBINE_ROWS = 512

F32 = jnp.float32
BF16 = jnp.bfloat16


def _nt_dot(a, b, precision=None):
    return lax.dot_general(a, b, (((1,), (1,)), ((), ())), precision=precision,
                           preferred_element_type=F32)


def _tn_dot(a, b, precision=None):
    return lax.dot_general(a, b, (((0,), (0,)), ((), ())), precision=precision,
                           preferred_element_type=F32)


def _cparams(*sem):
    return pltpu.CompilerParams(dimension_semantics=sem, vmem_limit_bytes=V7X_VMEM_LIMIT_BYTES)


def _pack_bf16_pair(lo, hi):
    lo_bits = lax.bitcast_convert_type(lo.astype(BF16).astype(F32), jnp.uint32)
    hi_bits = lax.bitcast_convert_type(hi.astype(BF16).astype(F32), jnp.uint32)
    return (lo_bits >> 16) | hi_bits


def _unpack_bf16_pair(word):
    lo = lax.bitcast_convert_type(word << 16, F32)
    hi = lax.bitcast_convert_type(word & jnp.uint32(0xFFFF0000), F32)
    return lo, hi


def _in_proj_kernel(x_ref, nw_ref, w_ref, cos_ref, sin_ref,
                    qa_ref, ka_ref, va_ref, qb_ref, fb_ref, ib_ref, gb_ref, ga_ref, gtb_ref):
    x = x_ref[...]
    h = x * lax.rsqrt(jnp.mean(x * x, axis=-1, keepdims=True) + RMS_EPS) * nw_ref[...]
    h = h.astype(BF16)
    cos = cos_ref[...]
    sin = sin_ref[...]

    def proj(c):
        return jnp.dot(h, w_ref[:, c * PROJ_COLS:(c + 1) * PROJ_COLS], preferred_element_type=F32)

    def rope(t):
        out = []
        for j in range(PROJ_COLS // V7X_LANES):
            tj = t[:, j * V7X_LANES:(j + 1) * V7X_LANES]
            out.append(tj * cos + pltpu.roll(tj, V7X_LANES // 2, 1) * sin)
        return jnp.concatenate(out, axis=1)

    qa_ref[...] = (rope(proj(0)) * (ATT_HEAD_DIM ** -0.5)).astype(BF16)
    ka_ref[...] = rope(proj(1)).astype(BF16)
    va_ref[...] = proj(2).astype(BF16)
    qb_ref[...] = proj(3).astype(BF16)
    fb_ref[...] = proj(4)
    ib_ref[...] = proj(5).astype(BF16)
    gb_ref[...] = proj(6).astype(BF16)
    ga_ref[:, :PROJ_COLS] = proj(7).astype(BF16)
    ga_ref[:, PROJ_COLS:] = proj(8).astype(BF16)
    gtb_ref[:, :PROJ_COLS] = proj(9).astype(BF16)
    gtb_ref[:, PROJ_COLS:] = proj(10).astype(BF16)


def _in_proj(x2, norm_w, w_in_bf16, cos_t, sin_t, seq):
    T, D = x2.shape
    tm = min(PROJ_ROWS, seq)
    n_seq_tiles = seq // tm
    row = lambda w: pl.BlockSpec((tm, w), lambda i: (i, 0))
    tab = pl.BlockSpec((tm, V7X_LANES), lambda i: (i % n_seq_tiles, 0))
    widths = [ATT_WIDTH] * 3 + [HGRN_WIDTH] * 4 + [D, D]
    dtypes = [BF16, BF16, BF16, BF16, F32, BF16, BF16, BF16, BF16]
    return pl.pallas_call(
        _in_proj_kernel,
        grid=(T // tm,),
        in_specs=[row(D), pl.BlockSpec((1, D), lambda i: (0, 0)),
                  pl.BlockSpec(memory_space=pltpu.VMEM), tab, tab],
        out_specs=[row(w) for w in widths],
        out_shape=[jax.ShapeDtypeStruct((T, w), dt) for w, dt in zip(widths, dtypes)],
        compiler_params=_cparams("parallel"),
        name="in_proj",
    )(x2, norm_w, w_in_bf16, cos_t, sin_t)


def _moba_kernel(q_ref, k_ref, v_ref, o_ref, kaug_ref, kmean_ref, vt_ref, acc_ref, *s_refs, nb):
    qi = pl.program_id(2)
    blk = MOBA_BLOCK
    lanes = V7X_LANES
    n_pairs = kaug_ref.shape[0]
    heads = [(pp, hh) for pp in range(n_pairs) for hh in range(2)]
    nbp = kmean_ref.shape[1]

    @pl.when(qi == 0)
    def _():
        rowb = lax.broadcasted_iota(jnp.int32, (nb * blk, lanes), 0) // blk
        col = lax.broadcasted_iota(jnp.int32, (nb * blk, lanes), 1)
        onehot = jnp.where(rowb == col, 1.0, 0.0).astype(BF16)
        for pp in range(n_pairs):
            pl_ = slice(pp * lanes, (pp + 1) * lanes)
            kaug_ref[pp, :, :lanes] = k_ref[0, :, pl_]
            kaug_ref[pp, :, lanes:] = onehot
            means = [jnp.sum(k_ref[0, n * blk:(n + 1) * blk, pl_].astype(F32), axis=0,
                             keepdims=True) * (1.0 / blk) for n in range(nb)]
            km = jnp.concatenate(means + [jnp.zeros((nbp - nb, lanes), F32)] * (nbp > nb), axis=0)
            hi = km.astype(BF16)
            rem = km - hi.astype(F32)
            mid = rem.astype(BF16)
            lo = (rem - mid.astype(F32)).astype(BF16)
            kmean_ref[pp] = jnp.concatenate([hi, mid, lo], axis=1)
            for c in range(nb):
                vt = v_ref[0, c * blk:(c + 1) * blk, pl_].astype(F32).T
                vt_ref[pp, :lanes, c * blk:(c + 1) * blk] = vt.astype(BF16)
            vt_ref[pp, lanes:, :] = jnp.ones((vt_ref.shape[1] - lanes, nb * blk), BF16)

    feat = lax.broadcasted_iota(jnp.int32, (lanes, blk), 0)
    key_i = lax.broadcasted_iota(jnp.int32, (blk, blk), 0)
    qry_i = lax.broadcasted_iota(jnp.int32, (blk, blk), 1)
    own = pl.multiple_of(qi * blk, blk)
    blk_id = lax.broadcasted_iota(jnp.int32, (nbp, 2 * blk), 0)
    slab = 2 * V7X_SUBLANES

    def slab_max(s):
        return jnp.max(s.reshape(s.shape[0] // slab, slab, blk), axis=0)

    q_aug, m_init = [], []
    for pp in range(n_pairs):
        qt = q_ref[0, :, pp * lanes:(pp + 1) * lanes].astype(F32).T
        k_own = kaug_ref[pp, pl.ds(own, blk), :lanes]
        qhs = [jnp.where((feat // (ATT_HEAD_DIM // 2)) % 2 == hh, qt, 0.0).astype(BF16)
               for hh in range(2)]
        q2 = jnp.concatenate(qhs, axis=1)
        gate = jnp.dot(kmean_ref[pp], jnp.concatenate([q2, q2, q2], axis=0),
                       preferred_element_type=F32)
        gate = jnp.where(blk_id < qi, gate, NEG)
        beaten = jnp.zeros((nbp, 2 * blk), F32)
        for n in range(nb):
            gn = gate[n:n + 1, :]
            wins = (gn > gate) | ((gn == gate) & (blk_id > n))
            beaten = beaten + jnp.where(wins, 1.0, 0.0)
        sel = (beaten < MOBA_TOPK) & (blk_id < qi)
        bias = jnp.where(sel, 0.0, NEG).astype(BF16)
        bias = jnp.concatenate([bias, jnp.zeros((lanes - nbp, 2 * blk), BF16)], axis=0)
        for hh in range(2):
            h = 2 * pp + hh
            q_aug.append(jnp.concatenate([qhs[hh], bias[:, hh * blk:(hh + 1) * blk]], axis=0))
            s = jnp.dot(k_own, qhs[hh], preferred_element_type=F32)
            s = jnp.where(key_i <= qry_i, s, NEG)
            m0 = jnp.max(slab_max(s), axis=0, keepdims=True)
            p = jnp.exp(s - m0).astype(BF16)
            acc_ref[h] = jnp.dot(vt_ref[pp, :, pl.ds(own, blk)], p, preferred_element_type=F32)
            m_init.append(m0)

    group = MOBA_GROUP * blk
    n_groups = (qi + MOBA_GROUP) // MOBA_GROUP

    def score_head(g, h):
        off = pl.multiple_of(g * group, group)
        s = jnp.dot(kaug_ref[h // 2, pl.ds(off, group), :], q_aug[h], preferred_element_type=F32)
        s_refs[h][pl.ds(off, group), :] = s
        return jnp.max(slab_max(s), axis=0, keepdims=True)

    def value_head(g, h, gmax, m_old):
        off = pl.multiple_of(g * group, group)
        m_new = jnp.maximum(m_old, gmax)
        alpha = jnp.exp(m_old - m_new)
        p = jnp.exp(s_refs[h][pl.ds(off, group), :] - m_new).astype(BF16)
        acc_ref[h] = alpha * acc_ref[h] + jnp.dot(vt_ref[h // 2, :, pl.ds(off, group)], p,
                                                  preferred_element_type=F32)
        return m_new

    def pipelined(g, carry):
        gmax, ms = carry
        last = len(heads) - 1
        new_gmax, new_ms = [None] * (last + 1), [None] * (last + 1)
        new_gmax[last] = score_head(g + 1, last)
        for h in range(last + 1):
            new_ms[h] = value_head(g, h, gmax[h], ms[h])
            if h < last:
                new_gmax[h] = score_head(g + 1, h)
        return tuple(new_gmax), tuple(new_ms)

    first = tuple(score_head(0, h) for h in range(len(heads)))
    gmax, ms = lax.fori_loop(0, n_groups - 1, pipelined, (first, tuple(m_init)))
    for h in range(len(heads)):
        value_head(n_groups - 1, h, gmax[h], ms[h])

    half = lanes // 2
    outs = []
    for pp in range(n_pairs):
        outs.append(acc_ref[2 * pp, :half, :] / acc_ref[2 * pp, lanes:lanes + 1, :])
        outs.append(acc_ref[2 * pp + 1, half:lanes, :] / acc_ref[2 * pp + 1, lanes:lanes + 1, :])
    o_ref[0] = jnp.concatenate(outs, axis=0).T.astype(BF16)


def _moba(q, k, v):
    B, S, _ = q.shape
    nb = S // MOBA_BLOCK
    assert nb % MOBA_GROUP == 0
    nbp = -(-nb // (2 * V7X_SUBLANES)) * (2 * V7X_SUBLANES)
    n_pairs = ATT_WIDTH // V7X_LANES
    pp = MOBA_PAIRS
    assert n_pairs % pp == 0
    vt_rows = V7X_LANES + 2 * V7X_SUBLANES
    qspec = pl.BlockSpec((1, MOBA_BLOCK, pp * V7X_LANES), lambda b, p, i: (b, i, p))
    kvspec = pl.BlockSpec((1, S, pp * V7X_LANES), lambda b, p, i: (b, 0, p))
    return pl.pallas_call(
        functools.partial(_moba_kernel, nb=nb),
        grid=(B, n_pairs // pp, nb),
        in_specs=[qspec, kvspec, kvspec],
        out_specs=qspec,
        out_shape=jax.ShapeDtypeStruct((B, S, ATT_WIDTH), BF16),
        scratch_shapes=[pltpu.VMEM((pp, S, 2 * V7X_LANES), BF16),
                        pltpu.VMEM((pp, nbp, 3 * V7X_LANES), BF16),
                        pltpu.VMEM((pp, vt_rows, S), BF16),
                        pltpu.VMEM((2 * pp, vt_rows, MOBA_BLOCK), F32)]
                       + [pltpu.VMEM((S, MOBA_BLOCK), F32)] * (2 * pp),
        compiler_params=_cparams("parallel", "parallel", "arbitrary"),
        name="moba",
    )(q, k, v)


def _hgrn_level_sizes(chunk):
    return [chunk >> (i + 1) for i in range(int(math.log2(chunk)))]


def _hgrn_constants(chunk):
    t = np.arange(chunk)
    mats = [(t[None, :] <= t[:, None]),
            (t[None, :] > t[:, None])]
    qrows, pmasks = [], []
    for bs in _hgrn_level_sizes(chunk):
        blk = t // bs
        odd = (blk % 2) == 1
        lo, hi = blk * bs, (blk + 1) * bs
        u = t[None, :]
        m_odd = (u >= lo[:, None]) & (u <= t[:, None])
        m_even = (u > t[:, None]) & (u < hi[:, None])
        mats.append(np.where(odd[:, None], m_odd, m_even))
        qrows.append(odd)
        pmasks.append(odd[:, None] & (blk[None, :] == blk[:, None] - 1))
    pmasks.append(t[None, :] == t[:, None])
    summat = np.concatenate(mats, axis=0).astype(np.float32)
    qrow = np.stack(qrows, axis=0).astype(np.float32)
    pmask = np.stack(pmasks, axis=0).astype(np.float32)
    return summat, qrow, pmask


def _hgrn_kernel(q_ref, f_ref, i_ref, g_ref, lb_ref, nw_ref, sm_ref, qrow_ref, pm_ref,
                 o_ref, state_ref, *, rows):
    C = HGRN_CHUNK
    n_levels = qrow_ref.shape[0]

    @pl.when(pl.program_id(1) == 0)
    def _():
        state_ref[...] = jnp.zeros_like(state_ref)

    lb = lb_ref[...]
    summat = sm_ref[...]
    for c in range(rows // C):
        rs = slice(c * C, (c + 1) * C)
        fg = lb + (1.0 - lb) * jax.nn.sigmoid(f_ref[0, rs, :])
        logf = jnp.log(fg)
        hi = logf.astype(BF16)
        rem = logf - hi.astype(F32)
        mid = rem.astype(BF16)
        lo = (rem - mid.astype(F32)).astype(BF16)
        sums = jnp.dot(summat, jnp.concatenate([hi, mid, lo], axis=0),
                       preferred_element_type=F32)
        for h in range(HGRN_HEADS):
            ls = slice(h * HGRN_DIM, (h + 1) * HGRN_DIM)
            qf = jax.nn.silu(q_ref[0, rs, ls].astype(F32))
            kf = 1.0 - fg[:, ls]
            iv = i_ref[0, rs, ls]
            bcum = sums[0:C, ls]
            bsuf = sums[C:2 * C, ls]
            att = _nt_dot(qf.astype(BF16), kf.astype(BF16)) * pm_ref[n_levels]
            for lv in range(n_levels):
                w = jnp.exp(sums[(2 + lv) * C:(3 + lv) * C, ls])
                qrow = qrow_ref[lv]
                z = (jnp.where(qrow > 0.5, qf, kf) * w).astype(BF16)
                att = att + _nt_dot(z, z) * pm_ref[lv]
            o = jnp.dot(att.astype(BF16), iv, preferred_element_type=F32)
            st = state_ref[h]
            o = o + _nt_dot((qf * jnp.exp(bcum)).astype(BF16), st.astype(BF16))
            kdec = (kf * jnp.exp(bsuf)).astype(BF16)
            state_ref[h] = st * jnp.exp(bcum[C - 1:C, :]) + _tn_dot(iv, kdec)
            o = o * lax.rsqrt(jnp.mean(o * o, axis=-1, keepdims=True) + RMS_EPS)
            o = o * nw_ref[:, ls] * jax.nn.silu(g_ref[0, rs, ls].astype(F32))
            o_ref[0, rs, ls] = o.astype(BF16)


def _hgrn(qb, fb, ib, gb, lb, norm_w):
    B, S, W = qb.shape
    rows = min(HGRN_ROWS, S)
    summat, qrow, pmask = _hgrn_constants(HGRN_CHUNK)
    summat = jnp.asarray(np.concatenate([summat] * 3, axis=1), BF16)
    n_levels = qrow.shape[0]
    blk = pl.BlockSpec((1, rows, W), lambda b, s: (b, s, 0))
    vec = pl.BlockSpec((1, W), lambda b, s: (0, 0))
    const = lambda a: pl.BlockSpec(a.shape, lambda b, s: (0,) * a.ndim)
    qrow3 = qrow.reshape(n_levels, HGRN_CHUNK, 1)
    return pl.pallas_call(
        functools.partial(_hgrn_kernel, rows=rows),
        grid=(B, S // rows),
        in_specs=[blk, blk, blk, blk, vec, vec, const(summat), const(qrow3), const(pmask)],
        out_specs=blk,
        out_shape=jax.ShapeDtypeStruct((B, S, W), BF16),
        scratch_shapes=[pltpu.VMEM((HGRN_HEADS, HGRN_DIM, HGRN_DIM), F32)],
        compiler_params=_cparams("parallel", "arbitrary"),
        name="hgrn",
    )(qb, fb, ib, gb, lb, norm_w, jnp.asarray(summat), jnp.asarray(qrow3), jnp.asarray(pmask))


def _mix_route_kernel(ya_ref, yb_ref, ga_ref, gb_ref, x_ref, wa_ref, wb_ref, wo_ref, nw_ref,
                      rw_ref, rb_ref, tri_ref, anchor_ref,
                      x1_ref, h2_ref, e_ref, w_ref, rank_ref, cnt_ref, carry_ref):
    @pl.when(pl.program_id(0) == 0)
    def _():
        carry_ref[...] = jnp.zeros_like(carry_ref)

    ua = jnp.dot(ya_ref[...], wa_ref[...], preferred_element_type=F32)
    ub = jnp.dot(yb_ref[...], wb_ref[...], preferred_element_type=F32)
    mixed = (jax.nn.sigmoid(ga_ref[...].astype(F32)) * ua
             + jax.nn.sigmoid(gb_ref[...].astype(F32)) * ub)
    x1 = x_ref[...] + jnp.dot(mixed.astype(BF16), wo_ref[...], preferred_element_type=F32)
    x1_ref[...] = x1
    h2 = x1 * lax.rsqrt(jnp.mean(x1 * x1, axis=-1, keepdims=True) + RMS_EPS) * nw_ref[...]
    half = h2.shape[1] // 2
    h2_ref[...] = _pack_bf16_pair(h2[:, :half], h2[:, half:])

    tm = x1.shape[0]
    h2_hi = h2.astype(BF16)
    h2_lo = (h2 - h2_hi.astype(F32)).astype(BF16)
    logits = _nt_dot(rw_ref[...], jnp.concatenate([h2_hi, h2_lo, h2_hi], axis=1)) + rb_ref[...]
    eid = lax.broadcasted_iota(jnp.int32, (N_EXPERTS, tm), 0)
    work = logits
    es, vs = [], []
    for _ in range(TOP_K):
        mx = jnp.max(work, axis=0, keepdims=True)
        idx = jnp.min(jnp.where(work == mx, eid, N_EXPERTS), axis=0, keepdims=True)
        es.append(idx)
        vs.append(mx)
        work = jnp.where(eid == idx, -jnp.inf, work)
    ex = [jnp.exp(v - vs[0]) for v in vs]
    den = ex[0] + ex[1] + ex[2] + ex[3]
    multi = jnp.zeros((N_EXPERTS, tm), F32)
    for k in range(TOP_K):
        multi = multi + jnp.where(eid == es[k], 1.0, 0.0)
    before = jnp.dot(multi.astype(BF16), tri_ref[...], preferred_element_type=F32) + carry_ref[...]
    for k in range(TOP_K):
        e_ref[k:k + 1, :] = es[k]
        w_ref[k:k + 1, :] = ex[k] / den
        rank_ref[k:k + 1, :] = jnp.sum(jnp.where(eid == es[k], before, 0.0), axis=0,
                                       keepdims=True).astype(jnp.int32)
    carry_ref[...] = carry_ref[...] + jnp.sum(multi, axis=1, keepdims=True)
    cnt_ref[...] = jnp.broadcast_to(carry_ref[...], cnt_ref.shape)


def _mix_route(ya, yb, ga, gb, x2, wa, wb, wo, norm_w, rw_t, rb, row0, rows, anchor):
    D = x2.shape[1]
    T = rows
    tm = min(MIX_ROWS, T)
    row = lambda w: pl.BlockSpec((tm, w), lambda i: (i + row0 // tm, 0))
    out_row = pl.BlockSpec((tm, D), lambda i: (i, 0))
    whole = pl.BlockSpec(memory_space=pltpu.VMEM)
    kt = pl.BlockSpec((TOP_K, tm), lambda i: (0, i))
    tri = jnp.asarray(np.triu(np.ones((tm, tm), np.float32), 1), BF16)
    return pl.pallas_call(
        _mix_route_kernel,
        grid=(T // tm,),
        in_specs=[row(ATT_WIDTH), row(HGRN_WIDTH), row(D), row(D), row(D),
                  whole, whole, whole, pl.BlockSpec((1, D), lambda i: (0, 0)),
                  whole, whole, whole, pl.BlockSpec(memory_space=pltpu.SMEM)],
        out_specs=[out_row, pl.BlockSpec((tm, D // 2), lambda i: (i, 0)), kt, kt, kt,
                   pl.BlockSpec((N_EXPERTS, V7X_LANES), lambda i: (0, 0))],
        out_shape=[jax.ShapeDtypeStruct((T, D), F32), jax.ShapeDtypeStruct((T, D // 2), jnp.uint32),
                   jax.ShapeDtypeStruct((TOP_K, T), jnp.int32),
                   jax.ShapeDtypeStruct((TOP_K, T), F32),
                   jax.ShapeDtypeStruct((TOP_K, T), jnp.int32),
                   jax.ShapeDtypeStruct((N_EXPERTS, V7X_LANES), F32)],
        scratch_shapes=[pltpu.VMEM((N_EXPERTS, 1), F32)],
        compiler_params=_cparams("arbitrary"),
        name="mix_route",
    )(ya, yb, ga, gb, x2, wa, wb, wo, norm_w, rw_t, rb, tri, anchor)


def _sc_worker_id():
    return lax.axis_index("s") * V7X_SC_CORES + lax.axis_index("c")


def _sc_kernel(body, out_rows, like, window, name, extra_scratch=()):
    mesh = plsc.VectorSubcoreMesh(core_axis_name="c", subcore_axis_name="s")
    return pl.kernel(
        body, mesh=mesh,
        out_type=jax.ShapeDtypeStruct((out_rows,) + like.shape[1:], like.dtype),
        scratch_types=[pltpu.VMEM((window,), jnp.int32),
                       pltpu.VMEM((window,) + like.shape[1:], like.dtype), *extra_scratch],
        name=name)


def _scatter_rows(dest, h2, out_rows, pad_rows):
    T = dest.shape[1]
    n_workers = V7X_SC_CORES * V7X_SC_SUBCORES
    per_worker = T // n_workers
    window = min(SC_ROWS, per_worker)
    n_pad = pad_rows.shape[1]
    assert pad_rows.shape[0] == n_workers and n_pad % window == 0

    def body(dest_hbm, src_hbm, pad_hbm, zero_hbm, dst_hbm, idx_v, rows_v, pad_v, zero_v, all_v, sem,
             fill_sem):
        del idx_v
        wid = _sc_worker_id()
        base = wid * per_worker

        pltpu.sync_copy(pad_hbm.at[pl.ds(wid * n_pad, n_pad)], pad_v)
        pltpu.sync_copy(zero_hbm, zero_v)
        fills = [pltpu.async_copy(zero_v, dst_hbm.at[pad_v.at[pl.ds(c * window, window)]],
                                     fill_sem)
                 for c in range(n_pad // window)]

        for k in range(TOP_K):
            pltpu.sync_copy(dest_hbm.at[pl.ds(k * T + base, per_worker)],
                            all_v.at[pl.ds(k * per_worker, per_worker)])

        @pl.loop(0, per_worker // window)
        def _(c):
            pltpu.sync_copy(src_hbm.at[pl.ds(base + c * window, window)], rows_v)
            copies = [pltpu.async_copy(
                rows_v, dst_hbm.at[all_v.at[pl.ds(k * per_worker + c * window, window)]], sem)
                for k in range(TOP_K)]
            for cp in copies:
                cp.wait()

        for f in fills:
            f.wait()

    zeros = jnp.zeros((window,) + h2.shape[1:], h2.dtype)
    extra = [pltpu.VMEM((n_pad,), jnp.int32), pltpu.VMEM(zeros.shape, zeros.dtype),
             pltpu.VMEM((TOP_K * per_worker,), jnp.int32), pltpu.SemaphoreType.DMA,
             pltpu.SemaphoreType.DMA]
    return _sc_kernel(body, out_rows, h2, window, "scatter_rows", extra)(
        dest.reshape(-1), h2, pad_rows.reshape(-1), zeros)


def _gather_rows(dest, yg):
    T = dest.shape[1]
    n_workers = V7X_SC_CORES * V7X_SC_SUBCORES
    per_worker = TOP_K * T // n_workers
    window = min(SC_ROWS, per_worker)

    def body(dest_hbm, src_hbm, dst_hbm, idx_v, rows_v, all_v):
        del idx_v
        base = _sc_worker_id() * per_worker
        pltpu.sync_copy(dest_hbm.at[pl.ds(base, per_worker)], all_v)

        @pl.loop(0, per_worker // window)
        def _(c):
            pltpu.sync_copy(src_hbm.at[all_v.at[pl.ds(c * window, window)]], rows_v)
            pltpu.sync_copy(rows_v, dst_hbm.at[pl.ds(base + c * window, window)])

    extra = [pltpu.VMEM((per_worker,), jnp.int32)]
    return _sc_kernel(body, TOP_K * T, yg, window, "gather_rows", extra)(dest.reshape(-1), yg)


def _experts_kernel(blk_ref, exp_ref, lo_ref, hi_ref, slot_ref, efirst_ref, enext_ref,
                    x_ref, wgu_hbm, bg_ref, bu_ref, wd_hbm, bd_ref, perm_ref, y_ref,
                    wgu_stage, wd_stage, wg_buf, wu_buf, wd_buf, sems):
    w = pl.program_id(0)
    lo = lo_ref[w]
    hi = hi_ref[w]
    slot = slot_ref[w]

    tm = x_ref.shape[0]
    n_out = V7X_MXU_DIM

    def weight_copies(e, s):
        return [pltpu.make_async_copy(wgu_hbm.at[e], wgu_stage.at[s], sems.at[s, 0]),
                pltpu.make_async_copy(wd_hbm.at[e], wd_stage.at[s], sems.at[s, 1])]

    @pl.when(w == 0)
    def _():
        for c in weight_copies(exp_ref[0], slot):
            c.start()

    @pl.when(efirst_ref[w] == 1)
    def _():
        for c in weight_copies(exp_ref[w], slot):
            c.wait()

        @pl.when(enext_ref[w] >= 0)
        def _():
            for c in weight_copies(enext_ref[w], 1 - slot):
                c.start()

        half = V7X_MXU_DIM // 2
        for g in range(wgu_stage.shape[2] // V7X_MXU_DIM):
            wb = wgu_stage[slot, :, g * V7X_MXU_DIM:(g + 1) * V7X_MXU_DIM].astype(BF16)
            d = jnp.dot(wb, perm_ref[...], preferred_element_type=F32).astype(BF16)
            wg_buf[:, g * half:(g + 1) * half] = d[:, :half]
            wu_buf[:, g * half:(g + 1) * half] = d[:, half:]
        wd_buf[...] = wd_stage[slot].astype(BF16)

    sb = EXPERT_SUB_ROWS

    def ffn(r0):
        rs = slice(r0, r0 + sb)
        x = jnp.concatenate(_unpack_bf16_pair(x_ref[rs, :]), axis=1).astype(BF16)
        g = jnp.dot(x, wg_buf[...], preferred_element_type=F32) + bg_ref[0]
        u = jnp.dot(x, wu_buf[...], preferred_element_type=F32) + bu_ref[0]
        g = jnp.minimum(g, SWIGLU_LIMIT)
        u = jnp.clip(u, -SWIGLU_LIMIT, SWIGLU_LIMIT)
        act = ((u + 1.0) * (g * jax.nn.sigmoid(g * SWIGLU_ALPHA))).astype(BF16)
        half = y_ref.shape[1]

        def down(c0):
            cs = slice(c0, c0 + n_out)
            return (jnp.dot(act, wd_buf[:, cs], preferred_element_type=F32)
                    + bd_ref[0, :, cs])

        for c in range(half // n_out):
            cs = slice(c * n_out, (c + 1) * n_out)
            y_ref[rs, cs] = _pack_bf16_pair(down(c * n_out), down(half + c * n_out))

    n_sub = tm // sb
    fused = (lo == 0) & (hi == tm)

    @pl.when(fused)
    def _():
        for j in range(n_sub):
            ffn(j * sb)

    for j in range(n_sub):
        @pl.when(jnp.logical_not(fused) & (lo <= j * sb) & (hi > j * sb))
        def _():
            ffn(j * sb)


def _work_items(counts, n_rows, tm):
    nblk = n_rows // tm
    n_items = nblk + N_EXPERTS - 1
    end = jnp.cumsum(counts)
    start = end - counts
    fb = start // tm
    nitems = jnp.where(counts > 0, (end - 1) // tm - fb + 1, 0)
    item_end = jnp.cumsum(nitems)
    item_start = item_end - nitems
    w = jnp.arange(n_items, dtype=jnp.int32)
    valid = w < item_end[-1]
    wc = jnp.minimum(w, item_end[-1] - 1)
    e = jnp.sum(wc[:, None] >= item_end[None, :], axis=1).astype(jnp.int32)
    e = jnp.minimum(e, N_EXPERTS - 1)
    onehot = e[:, None] == jnp.arange(N_EXPERTS, dtype=jnp.int32)[None, :]
    pick = lambda table: jnp.sum(jnp.where(onehot, table[None, :], 0), axis=1)
    blk = (pick(fb) + (wc - pick(item_start))).astype(jnp.int32)
    lo = jnp.maximum(pick(start), blk * tm) - blk * tm
    hi = jnp.minimum(pick(end), (blk + 1) * tm) - blk * tm
    lo = jnp.where(valid, lo, 0).astype(jnp.int32)
    hi = jnp.where(valid, hi, 0).astype(jnp.int32)
    used = counts > 0
    ids = jnp.arange(N_EXPERTS, dtype=jnp.int32)
    slot = pick((jnp.cumsum(used) - 1) % 2).astype(jnp.int32)
    efirst = (valid & (w == pick(item_start))).astype(jnp.int32)
    later = jnp.where(used[None, :] & (ids[None, :] > ids[:, None]), ids[None, :], N_EXPERTS)
    next_used = jnp.min(later, axis=1)
    enext = pick(jnp.where(next_used < N_EXPERTS, next_used, -1)).astype(jnp.int32)
    return blk, e, lo, hi, slot, efirst, enext


def _gate_up_split_matrix():
    i = np.arange(V7X_MXU_DIM)
    src = np.where(i < V7X_MXU_DIM // 2, 2 * i, 2 * (i - V7X_MXU_DIM // 2) + 1)
    perm = np.zeros((V7X_MXU_DIM, V7X_MXU_DIM), np.float32)
    perm[src, i] = 1.0
    return jnp.asarray(perm, BF16)


def _experts(xg, counts, w_gate_up, bg, bu, w_down, bd):
    A = xg.shape[0]
    D = 2 * xg.shape[1]
    tm = min(EXPERT_ROWS, A)
    F = w_down.shape[1]
    items = _work_items(counts, A, tm)
    n_items = A // tm + N_EXPERTS - 1
    xs = pl.BlockSpec((tm, D // 2), lambda w, blk, e, *_: (blk[w], 0))
    bias = lambda c: pl.BlockSpec((1, 1, c), lambda w, blk, e, *_: (e[w], 0, 0))
    hbm = pl.BlockSpec(memory_space=pl.ANY)
    return pl.pallas_call(
        _experts_kernel,
        grid_spec=pltpu.PrefetchScalarGridSpec(
            num_scalar_prefetch=len(items),
            grid=(n_items,),
            in_specs=[xs, hbm, bias(F), bias(F), hbm, bias(D), pl.BlockSpec(memory_space=pltpu.VMEM)],
            out_specs=xs,
            scratch_shapes=[pltpu.VMEM((2, D, 2 * F), F32), pltpu.VMEM((2, F, D), F32),
                            pltpu.VMEM((D, F), BF16), pltpu.VMEM((D, F), BF16),
                            pltpu.VMEM((F, D), BF16), pltpu.SemaphoreType.DMA((2, 2))],
        ),
        out_shape=jax.ShapeDtypeStruct(xg.shape, jnp.uint32),
        compiler_params=_cparams("arbitrary"),
        name="experts",
    )(*items, xg, w_gate_up, bg, bu, w_down, bd, _gate_up_split_matrix())


def _combine_kernel(x1_ref, yk_ref, w_ref, nw_ref, *rest):
    o_ref = rest[-1]
    w = w_ref[...]
    x = x1_ref[...]
    for k in range(TOP_K):
        x = x + jnp.concatenate(_unpack_bf16_pair(yk_ref[k]), axis=1) * w[:, k:k + 1]
    o_ref[...] = x * lax.rsqrt(jnp.mean(x * x, axis=-1, keepdims=True) + RMS_EPS) * nw_ref[...]


def _combine(x1, yk, w_tk, norm_w, out_rows, row0, prev_out):
    Tg, D = x1.shape
    tm = min(COMBINE_ROWS, Tg)
    row = pl.BlockSpec((tm, D), lambda i: (i, 0))
    in_specs = [row, pl.BlockSpec((TOP_K, tm, D // 2), lambda i: (0, i, 0)),
                pl.BlockSpec((tm, TOP_K), lambda i: (i, 0)),
                pl.BlockSpec((1, D), lambda i: (0, 0))]
    args = [x1, yk, w_tk, norm_w]
    aliases = {}
    if prev_out is not None:
        in_specs.append(pl.BlockSpec(memory_space=pl.ANY))
        args.append(prev_out)
        aliases = {len(args) - 1: 0}
    return pl.pallas_call(
        _combine_kernel,
        grid=(Tg // tm,),
        in_specs=in_specs,
        out_specs=pl.BlockSpec((tm, D), lambda i: (i + row0 // tm, 0)),
        out_shape=jax.ShapeDtypeStruct((out_rows, D), F32),
        input_output_aliases=aliases,
        compiler_params=_cparams("parallel"),
        name="combine",
    )(*args)


def _qk_column_order():
    half = ATT_HEAD_DIM // 2
    order = []
    for p in range(ATT_HEADS // 2):
        for part in range(2):
            for h in (2 * p, 2 * p + 1):
                order.extend(range(h * ATT_HEAD_DIM + part * half, h * ATT_HEAD_DIM + (part + 1) * half))
    return np.asarray(order, np.int32)


def _rope_tables(seq):
    half = ATT_HEAD_DIM // 2
    inv = ROPE_THETA ** (-(jnp.arange(half, dtype=F32) * 2.0 / ATT_HEAD_DIM))
    ang = jnp.arange(seq, dtype=F32)[:, None] * inv[None, :]
    cos, sin = jnp.cos(ang), jnp.sin(ang)
    return (jnp.concatenate([cos, cos, cos, cos], axis=1),
            jnp.concatenate([-sin, -sin, sin, sin], axis=1))


def kernel(x, norm1_w, w_in, moba_up, hgrn_lb_logits, hgrn_norm_w, hgrn_up, w_out, norm2_w,
           router_w, router_b, w_gate_up, b_gate_up, w_down, b_down, final_norm_w):
    B, S, D = x.shape
    T = B * S
    assert S % MOBA_BLOCK == 0 and w_in.shape[0] == 1
    x2 = x.reshape(T, D)

    perm = _qk_column_order()
    w0 = w_in[0]
    w_in_p = jnp.concatenate([w0[:, :ATT_WIDTH][:, perm], w0[:, ATT_WIDTH:2 * ATT_WIDTH][:, perm],
                              w0[:, 2 * ATT_WIDTH:]], axis=1).astype(BF16)
    cos_t, sin_t = _rope_tables(S)
    lb = jnp.cumsum(jax.nn.softmax(hgrn_lb_logits.astype(F32), axis=0), axis=0)[0:1]

    qa, ka, va, qb, fb, ib, gb, ga, gtb = _in_proj(x2, norm1_w, w_in_p, cos_t, sin_t, S)
    r3 = lambda a: a.reshape(B, S, a.shape[1])

    rw_t = router_w[0].T
    rw_hi = rw_t.astype(BF16)
    rw_lo = (rw_t - rw_hi.astype(F32)).astype(BF16)
    mix_w = (moba_up[0].astype(BF16), hgrn_up[0].astype(BF16), w_out[0].astype(BF16), norm2_w,
             jnp.concatenate([rw_hi, rw_hi, rw_lo], axis=1), router_b[0][:, None])
    expert_w = (w_gate_up[0], b_gate_up[0][:, None, 0::2], b_gate_up[0][:, None, 1::2],
                w_down[0], b_down[0][:, None, :])

    ya = _moba(r3(qa), r3(ka), r3(va)).reshape(T, ATT_WIDTH)
    yb = _hgrn(r3(qb), r3(fb), r3(ib), r3(gb), lb, hgrn_norm_w).reshape(T, HGRN_WIDTH)

    Tg = T // MOE_TOKEN_GROUPS
    staged = []
    anchor = jnp.zeros((1,), jnp.int32)
    for g in range(MOE_TOKEN_GROUPS):
        x1, h2, top_e, top_w, rank, cnt = _mix_route(ya, yb, ga, gtb, x2, *mix_w, g * Tg, Tg,
                                                     anchor)
        counts = cnt[:, 0].astype(jnp.int32)
        sb = EXPERT_SUB_ROWS
        padded = (counts + sb - 1) // sb * sb
        start = jnp.cumsum(padded) - padded
        dest = rank + jnp.sum(jnp.where(top_e[:, :, None] == jnp.arange(N_EXPERTS)[None, None, :],
                                        start[None, None, :], 0), axis=-1)
        body_rows = TOP_K * Tg + N_EXPERTS * sb
        spare_row = body_rows
        i = jnp.arange(sb, dtype=jnp.int32)[None, :]
        n_fill = (padded - counts)[:, None]
        pad_rows = jnp.where(n_fill > 0, (start + counts)[:, None] + i % jnp.maximum(n_fill, 1),
                             spare_row + i)
        xg = _scatter_rows(dest, h2, body_rows + EXPERT_ROWS, pad_rows)
        staged.append((xg, padded, dest, x1, top_w))
        anchor = dest[0, :1] + pad_rows[0, :1]

    out = None
    for g, (xg, padded, dest, x1, top_w) in enumerate(staged):
        yg = _experts(xg, padded, *expert_w)
        yk = _gather_rows(dest, yg).reshape(TOP_K, Tg, D // 2)
        out = _combine(x1, yk, top_w.T, final_norm_w[None, :], T, g * Tg, out)
    return out.reshape(B, S, D)
```

```python
import functools
import math

import numpy as np
import jax
import jax.numpy as jnp
from jax import lax
from jax.experimental import pallas as pl
from jax.experimental.pallas import tpu as pltpu
from jax.experimental.pallas import tpu_sc as plsc

ATT_HEADS = 8
ATT_HEAD_DIM = 64
ATT_WIDTH = ATT_HEADS * ATT_HEAD_DIM
MOBA_BLOCK = 256
MOBA_TOPK = 3
ROPE_THETA = 10000.0
HGRN_HEADS = 4
HGRN_DIM = 128
HGRN_WIDTH = HGRN_HEADS * HGRN_DIM
HGRN_CHUNK = 64
N_EXPERTS = 32
TOP_K = 4
SWIGLU_LIMIT = 7.0
SWIGLU_ALPHA = 1.702
RMS_EPS = 1e-6
NEG = -1e30

V7X_LANES = 128
V7X_SUBLANES = 8
V7X_MXU_DIM = 256
V7X_VMEM_LIMIT_BYTES = 56 * 1024 * 1024
V7X_SC_CORES = 2
V7X_SC_SUBCORES = 16

MOBA_PAIRS = 2
MOBA_GROUP = 4
PROJ_ROWS = 1024
PROJ_COLS = 512
HGRN_ROWS = 1024
MIX_ROWS = 512
EXPERT_ROWS = 1024
EXPERT_SUB_ROWS = 256
MOE_TOKEN_GROUPS = 2
SC_ROWS = 64
COMBINE_ROWS = 512

F32 = jnp.float32
BF16 = jnp.bfloat16


def _nt_dot(a, b, precision=None):
    return lax.dot_general(a, b, (((1,), (1,)), ((), ())), precision=precision,
                           preferred_element_type=F32)


def _tn_dot(a, b, precision=None):
    return lax.dot_general(a, b, (((0,), (0,)), ((), ())), precision=precision,
                           preferred_element_type=F32)


def _cparams(*sem):
    return pltpu.CompilerParams(dimension_semantics=sem, vmem_limit_bytes=V7X_VMEM_LIMIT_BYTES)


def _pack_bf16_pair(lo, hi):
    lo_bits = lax.bitcast_convert_type(lo.astype(BF16).astype(F32), jnp.uint32)
    hi_bits = lax.bitcast_convert_type(hi.astype(BF16).astype(F32), jnp.uint32)
    return (lo_bits >> 16) | hi_bits


def _unpack_bf16_pair(word):
    lo = lax.bitcast_convert_type(word << 16, F32)
    hi = lax.bitcast_convert_type(word & jnp.uint32(0xFFFF0000), F32)
    return lo, hi


def _in_proj_kernel(x_ref, nw_ref, w_ref, cos_ref, sin_ref,
                    qa_ref, ka_ref, va_ref, qb_ref, fb_ref, ib_ref, gb_ref, ga_ref, gtb_ref):
    x = x_ref[...]
    h = x * lax.rsqrt(jnp.mean(x * x, axis=-1, keepdims=True) + RMS_EPS) * nw_ref[...]
    h = h.astype(BF16)
    cos = cos_ref[...]
    sin = sin_ref[...]

    def proj(c):
        return jnp.dot(h, w_ref[:, c * PROJ_COLS:(c + 1) * PROJ_COLS], preferred_element_type=F32)

    def rope(t):
        out = []
        for j in range(PROJ_COLS // V7X_LANES):
            tj = t[:, j * V7X_LANES:(j + 1) * V7X_LANES]
            out.append(tj * cos + pltpu.roll(tj, V7X_LANES // 2, 1) * sin)
        return jnp.concatenate(out, axis=1)

    qa_ref[...] = (rope(proj(0)) * (ATT_HEAD_DIM ** -0.5)).astype(BF16)
    ka_ref[...] = rope(proj(1)).astype(BF16)
    va_ref[...] = proj(2).astype(BF16)
    qb_ref[...] = proj(3).astype(BF16)
    fb_ref[...] = proj(4)
    ib_ref[...] = proj(5).astype(BF16)
    gb_ref[...] = proj(6).astype(BF16)
    ga_ref[:, :PROJ_COLS] = proj(7).astype(BF16)
    ga_ref[:, PROJ_COLS:] = proj(8).astype(BF16)
    gtb_ref[:, :PROJ_COLS] = proj(9).astype(BF16)
    gtb_ref[:, PROJ_COLS:] = proj(10).astype(BF16)


def _in_proj(x2, norm_w, w_in_bf16, cos_t, sin_t, seq):
    T, D = x2.shape
    tm = min(PROJ_ROWS, seq)
    n_seq_tiles = seq // tm
    row = lambda w: pl.BlockSpec((tm, w), lambda i: (i, 0))
    tab = pl.BlockSpec((tm, V7X_LANES), lambda i: (i % n_seq_tiles, 0))
    widths = [ATT_WIDTH] * 3 + [HGRN_WIDTH] * 4 + [D, D]
    dtypes = [BF16, BF16, BF16, BF16, F32, BF16, BF16, BF16, BF16]
    return pl.pallas_call(
        _in_proj_kernel,
        grid=(T // tm,),
        in_specs=[row(D), pl.BlockSpec((1, D), lambda i: (0, 0)),
                  pl.BlockSpec(memory_space=pltpu.VMEM), tab, tab],
        out_specs=[row(w) for w in widths],
        out_shape=[jax.ShapeDtypeStruct((T, w), dt) for w, dt in zip(widths, dtypes)],
        compiler_params=_cparams("parallel"),
        name="in_proj",
    )(x2, norm_w, w_in_bf16, cos_t, sin_t)


def _moba_kernel(q_ref, k_ref, v_ref, o_ref, kaug_ref, kmean_ref, vt_ref, acc_ref, *s_refs, nb):
    qi = pl.program_id(2)
    blk = MOBA_BLOCK
    lanes = V7X_LANES
    n_pairs = kaug_ref.shape[0]
    heads = [(pp, hh) for pp in range(n_pairs) for hh in range(2)]
    nbp = kmean_ref.shape[1]

    @pl.when(qi == 0)
    def _():
        rowb = lax.broadcasted_iota(jnp.int32, (nb * blk, lanes), 0) // blk
        col = lax.broadcasted_iota(jnp.int32, (nb * blk, lanes), 1)
        onehot = jnp.where(rowb == col, 1.0, 0.0).astype(BF16)
        for pp in range(n_pairs):
            pl_ = slice(pp * lanes, (pp + 1) * lanes)
            kaug_ref[pp, :, :lanes] = k_ref[0, :, pl_]
            kaug_ref[pp, :, lanes:] = onehot
            means = [jnp.sum(k_ref[0, n * blk:(n + 1) * blk, pl_].astype(F32), axis=0,
                             keepdims=True) * (1.0 / blk) for n in range(nb)]
            km = jnp.concatenate(means + [jnp.zeros((nbp - nb, lanes), F32)] * (nbp > nb), axis=0)
            hi = km.astype(BF16)
            rem = km - hi.astype(F32)
            mid = rem.astype(BF16)
            lo = (rem - mid.astype(F32)).astype(BF16)
            kmean_ref[pp] = jnp.concatenate([hi, mid, lo], axis=1)
            for c in range(nb):
                vt = v_ref[0, c * blk:(c + 1) * blk, pl_].astype(F32).T
                vt_ref[pp, :lanes, c * blk:(c + 1) * blk] = vt.astype(BF16)
            vt_ref[pp, lanes:, :] = jnp.ones((vt_ref.shape[1] - lanes, nb * blk), BF16)

    feat = lax.broadcasted_iota(jnp.int32, (lanes, blk), 0)
    key_i = lax.broadcasted_iota(jnp.int32, (blk, blk), 0)
    qry_i = lax.broadcasted_iota(jnp.int32, (blk, blk), 1)
    own = pl.multiple_of(qi * blk, blk)
    blk_id = lax.broadcasted_iota(jnp.int32, (nbp, 2 * blk), 0)
    slab = 2 * V7X_SUBLANES

    def slab_max(s):
        return jnp.max(s.reshape(s.shape[0] // slab, slab, blk), axis=0)

    q_aug, m_init = [], []
    for pp in range(n_pairs):
        qt = q_ref[0, :, pp * lanes:(pp + 1) * lanes].astype(F32).T
        k_own = kaug_ref[pp, pl.ds(own, blk), :lanes]
        qhs = [jnp.where((feat // (ATT_HEAD_DIM // 2)) % 2 == hh, qt, 0.0).astype(BF16)
               for hh in range(2)]
        q2 = jnp.concatenate(qhs, axis=1)
        gate = jnp.dot(kmean_ref[pp], jnp.concatenate([q2, q2, q2], axis=0),
                       preferred_element_type=F32)
        gate = jnp.where(blk_id < qi, gate, NEG)
        beaten = jnp.zeros((nbp, 2 * blk), F32)
        for n in range(nb):
            gn = gate[n:n + 1, :]
            wins = (gn > gate) | ((gn == gate) & (blk_id > n))
            beaten = beaten + jnp.where(wins, 1.0, 0.0)
        sel = (beaten < MOBA_TOPK) & (blk_id < qi)
        bias = jnp.where(sel, 0.0, NEG).astype(BF16)
        bias = jnp.concatenate([bias, jnp.zeros((lanes - nbp, 2 * blk), BF16)], axis=0)
        for hh in range(2):
            h = 2 * pp + hh
            q_aug.append(jnp.concatenate([qhs[hh], bias[:, hh * blk:(hh + 1) * blk]], axis=0))
            s = jnp.dot(k_own, qhs[hh], preferred_element_type=F32)
            s = jnp.where(key_i <= qry_i, s, NEG)
            m0 = jnp.max(slab_max(s), axis=0, keepdims=True)
            p = jnp.exp(s - m0).astype(BF16)
            acc_ref[h] = jnp.dot(vt_ref[pp, :, pl.ds(own, blk)], p, preferred_element_type=F32)
            m_init.append(m0)

    group = MOBA_GROUP * blk
    n_groups = (qi + MOBA_GROUP) // MOBA_GROUP

    def score_head(g, h):
        off = pl.multiple_of(g * group, group)
        s = jnp.dot(kaug_ref[h // 2, pl.ds(off, group), :], q_aug[h], preferred_element_type=F32)
        s_refs[h][pl.ds(off, group), :] = s
        return jnp.max(slab_max(s), axis=0, keepdims=True)

    def value_head(g, h, gmax, m_old):
        off = pl.multiple_of(g * group, group)
        m_new = jnp.maximum(m_old, gmax)
        alpha = jnp.exp(m_old - m_new)
        p = jnp.exp(s_refs[h][pl.ds(off, group), :] - m_new).astype(BF16)
        acc_ref[h] = alpha * acc_ref[h] + jnp.dot(vt_ref[h // 2, :, pl.ds(off, group)], p,
                                                  preferred_element_type=F32)
        return m_new

    def pipelined(g, carry):
        gmax, ms = carry
        last = len(heads) - 1
        new_gmax, new_ms = [None] * (last + 1), [None] * (last + 1)
        new_gmax[last] = score_head(g + 1, last)
        for h in range(last + 1):
            new_ms[h] = value_head(g, h, gmax[h], ms[h])
            if h < last:
                new_gmax[h] = score_head(g + 1, h)
        return tuple(new_gmax), tuple(new_ms)

    first = tuple(score_head(0, h) for h in range(len(heads)))
    gmax, ms = lax.fori_loop(0, n_groups - 1, pipelined, (first, tuple(m_init)))
    for h in range(len(heads)):
        value_head(n_groups - 1, h, gmax[h], ms[h])

    half = lanes // 2
    outs = []
    for pp in range(n_pairs):
        outs.append(acc_ref[2 * pp, :half, :] / acc_ref[2 * pp, lanes:lanes + 1, :])
        outs.append(acc_ref[2 * pp + 1, half:lanes, :] / acc_ref[2 * pp + 1, lanes:lanes + 1, :])
    o_ref[0] = jnp.concatenate(outs, axis=0).T.astype(BF16)


def _moba(q, k, v):
    B, S, _ = q.shape
    nb = S // MOBA_BLOCK
    assert nb % MOBA_GROUP == 0
    nbp = -(-nb // (2 * V7X_SUBLANES)) * (2 * V7X_SUBLANES)
    n_pairs = ATT_WIDTH // V7X_LANES
    pp = MOBA_PAIRS
    assert n_pairs % pp == 0
    vt_rows = V7X_LANES + 2 * V7X_SUBLANES
    qspec = pl.BlockSpec((1, MOBA_BLOCK, pp * V7X_LANES), lambda b, p, i: (b, i, p))
    kvspec = pl.BlockSpec((1, S, pp * V7X_LANES), lambda b, p, i: (b, 0, p))
    return pl.pallas_call(
        functools.partial(_moba_kernel, nb=nb),
        grid=(B, n_pairs // pp, nb),
        in_specs=[qspec, kvspec, kvspec],
        out_specs=qspec,
        out_shape=jax.ShapeDtypeStruct((B, S, ATT_WIDTH), BF16),
        scratch_shapes=[pltpu.VMEM((pp, S, 2 * V7X_LANES), BF16),
                        pltpu.VMEM((pp, nbp, 3 * V7X_LANES), BF16),
                        pltpu.VMEM((pp, vt_rows, S), BF16),
                        pltpu.VMEM((2 * pp, vt_rows, MOBA_BLOCK), F32)]
                       + [pltpu.VMEM((S, MOBA_BLOCK), F32)] * (2 * pp),
        compiler_params=_cparams("parallel", "parallel", "arbitrary"),
        name="moba",
    )(q, k, v)


def _hgrn_level_sizes(chunk):
    return [chunk >> (i + 1) for i in range(int(math.log2(chunk)))]


def _hgrn_constants(chunk):
    t = np.arange(chunk)
    mats = [(t[None, :] <= t[:, None]),
            (t[None, :] > t[:, None])]
    qrows, pmasks = [], []
    for bs in _hgrn_level_sizes(chunk):
        blk = t // bs
        odd = (blk % 2) == 1
        lo, hi = blk * bs, (blk + 1) * bs
        u = t[None, :]
        m_odd = (u >= lo[:, None]) & (u <= t[:, None])
        m_even = (u > t[:, None]) & (u < hi[:, None])
        mats.append(np.where(odd[:, None], m_odd, m_even))
        qrows.append(odd)
        pmasks.append(odd[:, None] & (blk[None, :] == blk[:, None] - 1))
    pmasks.append(t[None, :] == t[:, None])
    summat = np.concatenate(mats, axis=0).astype(np.float32)
    qrow = np.stack(qrows, axis=0).astype(np.float32)
    pmask = np.stack(pmasks, axis=0).astype(np.float32)
    return summat, qrow, pmask


def _hgrn_kernel(q_ref, f_ref, i_ref, g_ref, lb_ref, nw_ref, sm_ref, qrow_ref, pm_ref,
                 o_ref, state_ref, *, rows):
    C = HGRN_CHUNK
    n_levels = qrow_ref.shape[0]

    @pl.when(pl.program_id(1) == 0)
    def _():
        state_ref[...] = jnp.zeros_like(state_ref)

    lb = lb_ref[...]
    summat = sm_ref[...]
    for c in range(rows // C):
        rs = slice(c * C, (c + 1) * C)
        fg = lb + (1.0 - lb) * jax.nn.sigmoid(f_ref[0, rs, :])
        logf = jnp.log(fg)
        hi = logf.astype(BF16)
        rem = logf - hi.astype(F32)
        mid = rem.astype(BF16)
        lo = (rem - mid.astype(F32)).astype(BF16)
        sums = jnp.dot(summat, jnp.concatenate([hi, mid, lo], axis=0),
                       preferred_element_type=F32)
        for h in range(HGRN_HEADS):
            ls = slice(h * HGRN_DIM, (h + 1) * HGRN_DIM)
            qf = jax.nn.silu(q_ref[0, rs, ls].astype(F32))
            kf = 1.0 - fg[:, ls]
            iv = i_ref[0, rs, ls]
            bcum = sums[0:C, ls]
            bsuf = sums[C:2 * C, ls]
            att = _nt_dot(qf.astype(BF16), kf.astype(BF16)) * pm_ref[n_levels]
            for lv in range(n_levels):
                w = jnp.exp(sums[(2 + lv) * C:(3 + lv) * C, ls])
                qrow = qrow_ref[lv]
                z = (jnp.where(qrow > 0.5, qf, kf) * w).astype(BF16)
                att = att + _nt_dot(z, z) * pm_ref[lv]
            o = jnp.dot(att.astype(BF16), iv, preferred_element_type=F32)
            st = state_ref[h]
            o = o + _nt_dot((qf * jnp.exp(bcum)).astype(BF16), st.astype(BF16))
            kdec = (kf * jnp.exp(bsuf)).astype(BF16)
            state_ref[h] = st * jnp.exp(bcum[C - 1:C, :]) + _tn_dot(iv, kdec)
            o = o * lax.rsqrt(jnp.mean(o * o, axis=-1, keepdims=True) + RMS_EPS)
            o = o * nw_ref[:, ls] * jax.nn.silu(g_ref[0, rs, ls].astype(F32))
            o_ref[0, rs, ls] = o.astype(BF16)


def _hgrn(qb, fb, ib, gb, lb, norm_w):
    B, S, W = qb.shape
    rows = min(HGRN_ROWS, S)
    summat, qrow, pmask = _hgrn_constants(HGRN_CHUNK)
    summat = jnp.asarray(np.concatenate([summat] * 3, axis=1), BF16)
    n_levels = qrow.shape[0]
    blk = pl.BlockSpec((1, rows, W), lambda b, s: (b, s, 0))
    vec = pl.BlockSpec((1, W), lambda b, s: (0, 0))
    const = lambda a: pl.BlockSpec(a.shape, lambda b, s: (0,) * a.ndim)
    qrow3 = qrow.reshape(n_levels, HGRN_CHUNK, 1)
    return pl.pallas_call(
        functools.partial(_hgrn_kernel, rows=rows),
        grid=(B, S // rows),
        in_specs=[blk, blk, blk, blk, vec, vec, const(summat), const(qrow3), const(pmask)],
        out_specs=blk,
        out_shape=jax.ShapeDtypeStruct((B, S, W), BF16),
        scratch_shapes=[pltpu.VMEM((HGRN_HEADS, HGRN_DIM, HGRN_DIM), F32)],
        compiler_params=_cparams("parallel", "arbitrary"),
        name="hgrn",
    )(qb, fb, ib, gb, lb, norm_w, jnp.asarray(summat), jnp.asarray(qrow3), jnp.asarray(pmask))


def _mix_route_kernel(ya_ref, yb_ref, ga_ref, gb_ref, x_ref, wa_ref, wb_ref, wo_ref, nw_ref,
                      rw_ref, rb_ref, tri_ref, anchor_ref,
                      x1_ref, h2_ref, e_ref, w_ref, rank_ref, cnt_ref, carry_ref):
    @pl.when(pl.program_id(0) == 0)
    def _():
        carry_ref[...] = jnp.zeros_like(carry_ref)

    ua = jnp.dot(ya_ref[...], wa_ref[...], preferred_element_type=F32)
    ub = jnp.dot(yb_ref[...], wb_ref[...], preferred_element_type=F32)
    mixed = (jax.nn.sigmoid(ga_ref[...].astype(F32)) * ua
             + jax.nn.sigmoid(gb_ref[...].astype(F32)) * ub)
    x1 = x_ref[...] + jnp.dot(mixed.astype(BF16), wo_ref[...], preferred_element_type=F32)
    x1_ref[...] = x1
    h2 = x1 * lax.rsqrt(jnp.mean(x1 * x1, axis=-1, keepdims=True) + RMS_EPS) * nw_ref[...]
    half = h2.shape[1] // 2
    h2_ref[...] = _pack_bf16_pair(h2[:, :half], h2[:, half:])

    tm = x1.shape[0]
    h2_hi = h2.astype(BF16)
    h2_lo = (h2 - h2_hi.astype(F32)).astype(BF16)
    logits = _nt_dot(rw_ref[...], jnp.concatenate([h2_hi, h2_lo, h2_hi], axis=1)) + rb_ref[...]
    eid = lax.broadcasted_iota(jnp.int32, (N_EXPERTS, tm), 0)
    work = logits
    es, vs = [], []
    for _ in range(TOP_K):
        mx = jnp.max(work, axis=0, keepdims=True)
        idx = jnp.min(jnp.where(work == mx, eid, N_EXPERTS), axis=0, keepdims=True)
        es.append(idx)
        vs.append(mx)
        work = jnp.where(eid == idx, -jnp.inf, work)
    ex = [jnp.exp(v - vs[0]) for v in vs]
    den = ex[0] + ex[1] + ex[2] + ex[3]
    multi = jnp.zeros((N_EXPERTS, tm), F32)
    for k in range(TOP_K):
        multi = multi + jnp.where(eid == es[k], 1.0, 0.0)
    before = jnp.dot(multi.astype(BF16), tri_ref[...], preferred_element_type=F32) + carry_ref[...]
    for k in range(TOP_K):
        e_ref[k:k + 1, :] = es[k]
        w_ref[k:k + 1, :] = ex[k] / den
        rank_ref[k:k + 1, :] = jnp.sum(jnp.where(eid == es[k], before, 0.0), axis=0,
                                       keepdims=True).astype(jnp.int32)
    carry_ref[...] = carry_ref[...] + jnp.sum(multi, axis=1, keepdims=True)
    cnt_ref[...] = jnp.broadcast_to(carry_ref[...], cnt_ref.shape)


def _mix_route(ya, yb, ga, gb, x2, wa, wb, wo, norm_w, rw_t, rb, row0, rows, anchor):
    D = x2.shape[1]
    T = rows
    tm = min(MIX_ROWS, T)
    row = lambda w: pl.BlockSpec((tm, w), lambda i: (i + row0 // tm, 0))
    out_row = pl.BlockSpec((tm, D), lambda i: (i, 0))
    whole = pl.BlockSpec(memory_space=pltpu.VMEM)
    kt = pl.BlockSpec((TOP_K, tm), lambda i: (0, i))
    tri = jnp.asarray(np.triu(np.ones((tm, tm), np.float32), 1), BF16)
    return pl.pallas_call(
        _mix_route_kernel,
        grid=(T // tm,),
        in_specs=[row(ATT_WIDTH), row(HGRN_WIDTH), row(D), row(D), row(D),
                  whole, whole, whole, pl.BlockSpec((1, D), lambda i: (0, 0)),
                  whole, whole, whole, pl.BlockSpec(memory_space=pltpu.SMEM)],
        out_specs=[out_row, pl.BlockSpec((tm, D // 2), lambda i: (i, 0)), kt, kt, kt,
                   pl.BlockSpec((N_EXPERTS, V7X_LANES), lambda i: (0, 0))],
        out_shape=[jax.ShapeDtypeStruct((T, D), F32), jax.ShapeDtypeStruct((T, D // 2), jnp.uint32),
                   jax.ShapeDtypeStruct((TOP_K, T), jnp.int32),
                   jax.ShapeDtypeStruct((TOP_K, T), F32),
                   jax.ShapeDtypeStruct((TOP_K, T), jnp.int32),
                   jax.ShapeDtypeStruct((N_EXPERTS, V7X_LANES), F32)],
        scratch_shapes=[pltpu.VMEM((N_EXPERTS, 1), F32)],
        compiler_params=_cparams("arbitrary"),
        name="mix_route",
    )(ya, yb, ga, gb, x2, wa, wb, wo, norm_w, rw_t, rb, tri, anchor)


def _sc_worker_id():
    return lax.axis_index("s") * V7X_SC_CORES + lax.axis_index("c")


def _sc_kernel(body, out_rows, like, window, name, extra_scratch=()):
    mesh = plsc.VectorSubcoreMesh(core_axis_name="c", subcore_axis_name="s")
    return pl.kernel(
        body, mesh=mesh,
        out_type=jax.ShapeDtypeStruct((out_rows,) + like.shape[1:], like.dtype),
        scratch_types=[pltpu.VMEM((window,), jnp.int32),
                       pltpu.VMEM((window,) + like.shape[1:], like.dtype), *extra_scratch],
        name=name)


def _scatter_rows(dest, h2, out_rows, pad_rows):
    T = dest.shape[1]
    n_workers = V7X_SC_CORES * V7X_SC_SUBCORES
    per_worker = T // n_workers
    window = min(SC_ROWS, per_worker)
    n_pad = pad_rows.shape[1]
    assert pad_rows.shape[0] == n_workers and n_pad % window == 0

    def body(dest_hbm, src_hbm, pad_hbm, zero_hbm, dst_hbm, idx_v, rows_v, pad_v, zero_v, all_v, sem,
             fill_sem):
        del idx_v
        wid = _sc_worker_id()
        base = wid * per_worker

        pltpu.sync_copy(pad_hbm.at[pl.ds(wid * n_pad, n_pad)], pad_v)
        pltpu.sync_copy(zero_hbm, zero_v)
        fills = [pltpu.async_copy(zero_v, dst_hbm.at[pad_v.at[pl.ds(c * window, window)]],
                                     fill_sem)
                 for c in range(n_pad // window)]

        for k in range(TOP_K):
            pltpu.sync_copy(dest_hbm.at[pl.ds(k * T + base, per_worker)],
                            all_v.at[pl.ds(k * per_worker, per_worker)])

        @pl.loop(0, per_worker // window)
        def _(c):
            pltpu.sync_copy(src_hbm.at[pl.ds(base + c * window, window)], rows_v)
            copies = [pltpu.async_copy(
                rows_v, dst_hbm.at[all_v.at[pl.ds(k * per_worker + c * window, window)]], sem)
                for k in range(TOP_K)]
            for cp in copies:
                cp.wait()

        for f in fills:
            f.wait()

    zeros = jnp.zeros((window,) + h2.shape[1:], h2.dtype)
    extra = [pltpu.VMEM((n_pad,), jnp.int32), pltpu.VMEM(zeros.shape, zeros.dtype),
             pltpu.VMEM((TOP_K * per_worker,), jnp.int32), pltpu.SemaphoreType.DMA,
             pltpu.SemaphoreType.DMA]
    return _sc_kernel(body, out_rows, h2, window, "scatter_rows", extra)(
        dest.reshape(-1), h2, pad_rows.reshape(-1), zeros)


def _gather_rows(dest, yg):
    T = dest.shape[1]
    n_workers = V7X_SC_CORES * V7X_SC_SUBCORES
    per_worker = TOP_K * T // n_workers
    window = min(SC_ROWS, per_worker)

    def body(dest_hbm, src_hbm, dst_hbm, idx_v, rows_v, all_v):
        del idx_v
        base = _sc_worker_id() * per_worker
        pltpu.sync_copy(dest_hbm.at[pl.ds(base, per_worker)], all_v)

        @pl.loop(0, per_worker // window)
        def _(c):
            pltpu.sync_copy(src_hbm.at[all_v.at[pl.ds(c * window, window)]], rows_v)
            pltpu.sync_copy(rows_v, dst_hbm.at[pl.ds(base + c * window, window)])

    extra = [pltpu.VMEM((per_worker,), jnp.int32)]
    return _sc_kernel(body, TOP_K * T, yg, window, "gather_rows", extra)(dest.reshape(-1), yg)


def _experts_kernel(blk_ref, exp_ref, lo_ref, hi_ref, slot_ref, efirst_ref, enext_ref,
                    x_ref, wgu_hbm, bg_ref, bu_ref, wd_hbm, bd_ref, perm_ref, y_ref,
                    wgu_stage, wd_stage, wg_buf, wu_buf, wd_buf, sems):
    w = pl.program_id(0)
    lo = lo_ref[w]
    hi = hi_ref[w]
    slot = slot_ref[w]

    tm = x_ref.shape[0]
    n_out = V7X_MXU_DIM

    def weight_copies(e, s):
        return [pltpu.make_async_copy(wgu_hbm.at[e], wgu_stage.at[s], sems.at[s, 0]),
                pltpu.make_async_copy(wd_hbm.at[e], wd_stage.at[s], sems.at[s, 1])]

    @pl.when(w == 0)
    def _():
        for c in weight_copies(exp_ref[0], slot):
            c.start()

    @pl.when(efirst_ref[w] == 1)
    def _():
        for c in weight_copies(exp_ref[w], slot):
            c.wait()

        @pl.when(enext_ref[w] >= 0)
        def _():
            for c in weight_copies(enext_ref[w], 1 - slot):
                c.start()

        half = V7X_MXU_DIM // 2
        for g in range(wgu_stage.shape[2] // V7X_MXU_DIM):
            wb = wgu_stage[slot, :, g * V7X_MXU_DIM:(g + 1) * V7X_MXU_DIM].astype(BF16)
            d = jnp.dot(wb, perm_ref[...], preferred_element_type=F32).astype(BF16)
            wg_buf[:, g * half:(g + 1) * half] = d[:, :half]
            wu_buf[:, g * half:(g + 1) * half] = d[:, half:]
        wd_buf[...] = wd_stage[slot].astype(BF16)

    sb = EXPERT_SUB_ROWS

    def ffn(r0):
        rs = slice(r0, r0 + sb)
        x = jnp.concatenate(_unpack_bf16_pair(x_ref[rs, :]), axis=1).astype(BF16)
        g = jnp.dot(x, wg_buf[...], preferred_element_type=F32) + bg_ref[0]
        u = jnp.dot(x, wu_buf[...], preferred_element_type=F32) + bu_ref[0]
        g = jnp.minimum(g, SWIGLU_LIMIT)
        u = jnp.clip(u, -SWIGLU_LIMIT, SWIGLU_LIMIT)
        act = ((u + 1.0) * (g * jax.nn.sigmoid(g * SWIGLU_ALPHA))).astype(BF16)
        half = y_ref.shape[1]

        def down(c0):
            cs = slice(c0, c0 + n_out)
            return (jnp.dot(act, wd_buf[:, cs], preferred_element_type=F32)
                    + bd_ref[0, :, cs])

        for c in range(half // n_out):
            cs = slice(c * n_out, (c + 1) * n_out)
            y_ref[rs, cs] = _pack_bf16_pair(down(c * n_out), down(half + c * n_out))

    n_sub = tm // sb
    fused = (lo == 0) & (hi == tm)

    @pl.when(fused)
    def _():
        for j in range(n_sub):
            ffn(j * sb)

    for j in range(n_sub):
        @pl.when(jnp.logical_not(fused) & (lo <= j * sb) & (hi > j * sb))
        def _():
            ffn(j * sb)


def _work_items(counts, n_rows, tm):
    nblk = n_rows // tm
    n_items = nblk + N_EXPERTS - 1
    end = jnp.cumsum(counts)
    start = end - counts
    fb = start // tm
    nitems = jnp.where(counts > 0, (end - 1) // tm - fb + 1, 0)
    item_end = jnp.cumsum(nitems)
    item_start = item_end - nitems
    w = jnp.arange(n_items, dtype=jnp.int32)
    valid = w < item_end[-1]
    wc = jnp.minimum(w, item_end[-1] - 1)
    e = jnp.sum(wc[:, None] >= item_end[None, :], axis=1).astype(jnp.int32)
    e = jnp.minimum(e, N_EXPERTS - 1)
    onehot = e[:, None] == jnp.arange(N_EXPERTS, dtype=jnp.int32)[None, :]
    pick = lambda table: jnp.sum(jnp.where(onehot, table[None, :], 0), axis=1)
    blk = (pick(fb) + (wc - pick(item_start))).astype(jnp.int32)
    lo = jnp.maximum(pick(start), blk * tm) - blk * tm
    hi = jnp.minimum(pick(end), (blk + 1) * tm) - blk * tm
    lo = jnp.where(valid, lo, 0).astype(jnp.int32)
    hi = jnp.where(valid, hi, 0).astype(jnp.int32)
    used = counts > 0
    ids = jnp.arange(N_EXPERTS, dtype=jnp.int32)
    slot = pick((jnp.cumsum(used) - 1) % 2).astype(jnp.int32)
    efirst = (valid & (w == pick(item_start))).astype(jnp.int32)
    later = jnp.where(used[None, :] & (ids[None, :] > ids[:, None]), ids[None, :], N_EXPERTS)
    next_used = jnp.min(later, axis=1)
    enext = pick(jnp.where(next_used < N_EXPERTS, next_used, -1)).astype(jnp.int32)
    return blk, e, lo, hi, slot, efirst, enext


def _gate_up_split_matrix():
    i = np.arange(V7X_MXU_DIM)
    src = np.where(i < V7X_MXU_DIM // 2, 2 * i, 2 * (i - V7X_MXU_DIM // 2) + 1)
    perm = np.zeros((V7X_MXU_DIM, V7X_MXU_DIM), np.float32)
    perm[src, i] = 1.0
    return jnp.asarray(perm, BF16)


def _experts(xg, counts, w_gate_up, bg, bu, w_down, bd):
    A = xg.shape[0]
    D = 2 * xg.shape[1]
    tm = min(EXPERT_ROWS, A)
    F = w_down.shape[1]
    items = _work_items(counts, A, tm)
    n_items = A // tm + N_EXPERTS - 1
    xs = pl.BlockSpec((tm, D // 2), lambda w, blk, e, *_: (blk[w], 0))
    bias = lambda c: pl.BlockSpec((1, 1, c), lambda w, blk, e, *_: (e[w], 0, 0))
    hbm = pl.BlockSpec(memory_space=pl.ANY)
    return pl.pallas_call(
        _experts_kernel,
        grid_spec=pltpu.PrefetchScalarGridSpec(
            num_scalar_prefetch=len(items),
            grid=(n_items,),
            in_specs=[xs, hbm, bias(F), bias(F), hbm, bias(D), pl.BlockSpec(memory_space=pltpu.VMEM)],
            out_specs=xs,
            scratch_shapes=[pltpu.VMEM((2, D, 2 * F), F32), pltpu.VMEM((2, F, D), F32),
                            pltpu.VMEM((D, F), BF16), pltpu.VMEM((D, F), BF16),
                            pltpu.VMEM((F, D), BF16), pltpu.SemaphoreType.DMA((2, 2))],
        ),
        out_shape=jax.ShapeDtypeStruct(xg.shape, jnp.uint32),
        compiler_params=_cparams("arbitrary"),
        name="experts",
    )(*items, xg, w_gate_up, bg, bu, w_down, bd, _gate_up_split_matrix())


def _combine_kernel(x1_ref, yk_ref, w_ref, nw_ref, *rest):
    o_ref = rest[-1]
    w = w_ref[...]
    x = x1_ref[...]
    for k in range(TOP_K):
        x = x + jnp.concatenate(_unpack_bf16_pair(yk_ref[k]), axis=1) * w[:, k:k + 1]
    o_ref[...] = x * lax.rsqrt(jnp.mean(x * x, axis=-1, keepdims=True) + RMS_EPS) * nw_ref[...]


def _combine(x1, yk, w_tk, norm_w, out_rows, row0, prev_out):
    Tg, D = x1.shape
    tm = min(COMBINE_ROWS, Tg)
    row = pl.BlockSpec((tm, D), lambda i: (i, 0))
    in_specs = [row, pl.BlockSpec((TOP_K, tm, D // 2), lambda i: (0, i, 0)),
                pl.BlockSpec((tm, TOP_K), lambda i: (i, 0)),
                pl.BlockSpec((1, D), lambda i: (0, 0))]
    args = [x1, yk, w_tk, norm_w]
    aliases = {}
    if prev_out is not None:
        in_specs.append(pl.BlockSpec(memory_space=pl.ANY))
        args.append(prev_out)
        aliases = {len(args) - 1: 0}
    return pl.pallas_call(
        _combine_kernel,
        grid=(Tg // tm,),
        in_specs=in_specs,
        out_specs=pl.BlockSpec((tm, D), lambda i: (i + row0 // tm, 0)),
        out_shape=jax.ShapeDtypeStruct((out_rows, D), F32),
        input_output_aliases=aliases,
        compiler_params=_cparams("parallel"),
        name="combine",
    )(*args)


def _qk_column_order():
    half = ATT_HEAD_DIM // 2
    order = []
    for p in range(ATT_HEADS // 2):
        for part in range(2):
            for h in (2 * p, 2 * p + 1):
                order.extend(range(h * ATT_HEAD_DIM + part * half, h * ATT_HEAD_DIM + (part + 1) * half))
    return np.asarray(order, np.int32)


def _rope_tables(seq):
    half = ATT_HEAD_DIM // 2
    inv = ROPE_THETA ** (-(jnp.arange(half, dtype=F32) * 2.0 / ATT_HEAD_DIM))
    ang = jnp.arange(seq, dtype=F32)[:, None] * inv[None, :]
    cos, sin = jnp.cos(ang), jnp.sin(ang)
    return (jnp.concatenate([cos, cos, cos, cos], axis=1),
            jnp.concatenate([-sin, -sin, sin, sin], axis=1))


def kernel(x, norm1_w, w_in, moba_up, hgrn_lb_logits, hgrn_norm_w, hgrn_up, w_out, norm2_w,
           router_w, router_b, w_gate_up, b_gate_up, w_down, b_down, final_norm_w):
    B, S, D = x.shape
    T = B * S
    assert S % MOBA_BLOCK == 0 and w_in.shape[0] == 1
    x2 = x.reshape(T, D)

    perm = _qk_column_order()
    w0 = w_in[0]
    w_in_p = jnp.concatenate([w0[:, :ATT_WIDTH][:, perm], w0[:, ATT_WIDTH:2 * ATT_WIDTH][:, perm],
                              w0[:, 2 * ATT_WIDTH:]], axis=1).astype(BF16)
    cos_t, sin_t = _rope_tables(S)
    lb = jnp.cumsum(jax.nn.softmax(hgrn_lb_logits.astype(F32), axis=0), axis=0)[0:1]

    qa, ka, va, qb, fb, ib, gb, ga, gtb = _in_proj(x2, norm1_w, w_in_p, cos_t, sin_t, S)
    r3 = lambda a: a.reshape(B, S, a.shape[1])

    rw_t = router_w[0].T
    rw_hi = rw_t.astype(BF16)
    rw_lo = (rw_t - rw_hi.astype(F32)).astype(BF16)
    mix_w = (moba_up[0].astype(BF16), hgrn_up[0].astype(BF16), w_out[0].astype(BF16), norm2_w,
             jnp.concatenate([rw_hi, rw_hi, rw_lo], axis=1), router_b[0][:, None])
    expert_w = (w_gate_up[0], b_gate_up[0][:, None, 0::2], b_gate_up[0][:, None, 1::2],
                w_down[0], b_down[0][:, None, :])

    ya = _moba(r3(qa), r3(ka), r3(va)).reshape(T, ATT_WIDTH)
    yb = _hgrn(r3(qb), r3(fb), r3(ib), r3(gb), lb, hgrn_norm_w).reshape(T, HGRN_WIDTH)

    Tg = T // MOE_TOKEN_GROUPS
    staged = []
    anchor = jnp.zeros((1,), jnp.int32)
    for g in range(MOE_TOKEN_GROUPS):
        x1, h2, top_e, top_w, rank, cnt = _mix_route(ya, yb, ga, gtb, x2, *mix_w, g * Tg, Tg,
                                                     anchor)
        counts = cnt[:, 0].astype(jnp.int32)
        sb = EXPERT_SUB_ROWS
        padded = (counts + sb - 1) // sb * sb
        start = jnp.cumsum(padded) - padded
        dest = rank + jnp.sum(jnp.where(top_e[:, :, None] == jnp.arange(N_EXPERTS)[None, None, :],
                                        start[None, None, :], 0), axis=-1)
        body_rows = TOP_K * Tg + N_EXPERTS * sb
        spare_row = body_rows
        i = jnp.arange(sb, dtype=jnp.int32)[None, :]
        n_fill = (padded - counts)[:, None]
        pad_rows = jnp.where(n_fill > 0, (start + counts)[:, None] + i % jnp.maximum(n_fill, 1),
                             spare_row + i)
        xg = _scatter_rows(dest, h2, body_rows + EXPERT_ROWS, pad_rows)
        staged.append((xg, padded, dest, x1, top_w))
        anchor = dest[0, :1] + pad_rows[0, :1]

    out = None
    for g, (xg, padded, dest, x1, top_w) in enumerate(staged):
        yg = _experts(xg, padded, *expert_w)
        yk = _gather_rows(dest, yg).reshape(TOP_K, Tg, D // 2)
        out = _combine(x1, yk, top_w.T, final_norm_w[None, :], T, g * Tg, out)
    return out.reshape(B, S, D)
```

```python
import functools
import math

import numpy as np
import jax
import jax.numpy as jnp
from jax import lax
from jax.experimental import pallas as pl
from jax.experimental.pallas import tpu as pltpu
from jax.experimental.pallas import tpu_sc as plsc

ATT_HEADS = 8
ATT_HEAD_DIM = 64
ATT_WIDTH = ATT_HEADS * ATT_HEAD_DIM
MOBA_BLOCK = 256
MOBA_TOPK = 3
ROPE_THETA = 10000.0
HGRN_HEADS = 4
HGRN_DIM = 128
HGRN_WIDTH = HGRN_HEADS * HGRN_DIM
HGRN_CHUNK = 64
N_EXPERTS = 32
TOP_K = 4
SWIGLU_LIMIT = 7.0
SWIGLU_ALPHA = 1.702
RMS_EPS = 1e-6
NEG = -1e30

V7X_LANES = 128
V7X_SUBLANES = 8
V7X_MXU_DIM = 256
V7X_VMEM_LIMIT_BYTES = 56 * 1024 * 1024
V7X_SC_CORES = 2
V7X_SC_SUBCORES = 16

MOBA_PAIRS = 2
MOBA_GROUP = 4
PROJ_ROWS = 1024
PROJ_COLS = 512
HGRN_ROWS = 1024
MIX_ROWS = 512
EXPERT_ROWS = 1024
EXPERT_SUB_ROWS = 256
MOE_TOKEN_GROUPS = 2
SC_ROWS = 64
COMBINE_ROWS = 1024

F32 = jnp.float32
BF16 = jnp.bfloat16


def _nt_dot(a, b, precision=None):
    return lax.dot_general(a, b, (((1,), (1,)), ((), ())), precision=precision,
                           preferred_element_type=F32)


def _tn_dot(a, b, precision=None):
    return lax.dot_general(a, b, (((0,), (0,)), ((), ())), precision=precision,
                           preferred_element_type=F32)


def _cparams(*sem):
    return pltpu.CompilerParams(dimension_semantics=sem, vmem_limit_bytes=V7X_VMEM_LIMIT_BYTES)


def _pack_bf16_pair(lo, hi):
    lo_bits = lax.bitcast_convert_type(lo.astype(BF16).astype(F32), jnp.uint32)
    hi_bits = lax.bitcast_convert_type(hi.astype(BF16).astype(F32), jnp.uint32)
    return (lo_bits >> 16) | hi_bits


def _unpack_bf16_pair(word):
    lo = lax.bitcast_convert_type(word << 16, F32)
    hi = lax.bitcast_convert_type(word & jnp.uint32(0xFFFF0000), F32)
    return lo, hi


def _in_proj_kernel(x_ref, nw_ref, w_ref, cos_ref, sin_ref,
                    qa_ref, ka_ref, va_ref, qb_ref, fb_ref, ib_ref, gb_ref, ga_ref, gtb_ref):
    x = x_ref[...]
    h = x * lax.rsqrt(jnp.mean(x * x, axis=-1, keepdims=True) + RMS_EPS) * nw_ref[...]
    h = h.astype(BF16)
    cos = cos_ref[...]
    sin = sin_ref[...]

    def proj(c):
        return jnp.dot(h, w_ref[:, c * PROJ_COLS:(c + 1) * PROJ_COLS], preferred_element_type=F32)

    def rope(t):
        out = []
        for j in range(PROJ_COLS // V7X_LANES):
            tj = t[:, j * V7X_LANES:(j + 1) * V7X_LANES]
            out.append(tj * cos + pltpu.roll(tj, V7X_LANES // 2, 1) * sin)
        return jnp.concatenate(out, axis=1)

    qa_ref[...] = (rope(proj(0)) * (ATT_HEAD_DIM ** -0.5)).astype(BF16)
    ka_ref[...] = rope(proj(1)).astype(BF16)
    va_ref[...] = proj(2).astype(BF16)
    qb_ref[...] = proj(3).astype(BF16)
    fb_ref[...] = proj(4)
    ib_ref[...] = proj(5).astype(BF16)
    gb_ref[...] = proj(6).astype(BF16)
    ga_ref[:, :PROJ_COLS] = proj(7).astype(BF16)
    ga_ref[:, PROJ_COLS:] = proj(8).astype(BF16)
    gtb_ref[:, :PROJ_COLS] = proj(9).astype(BF16)
    gtb_ref[:, PROJ_COLS:] = proj(10).astype(BF16)


def _in_proj(x2, norm_w, w_in_bf16, cos_t, sin_t, seq):
    T, D = x2.shape
    tm = min(PROJ_ROWS, seq)
    n_seq_tiles = seq // tm
    row = lambda w: pl.BlockSpec((tm, w), lambda i: (i, 0))
    tab = pl.BlockSpec((tm, V7X_LANES), lambda i: (i % n_seq_tiles, 0))
    widths = [ATT_WIDTH] * 3 + [HGRN_WIDTH] * 4 + [D, D]
    dtypes = [BF16, BF16, BF16, BF16, F32, BF16, BF16, BF16, BF16]
    return pl.pallas_call(
        _in_proj_kernel,
        grid=(T // tm,),
        in_specs=[row(D), pl.BlockSpec((1, D), lambda i: (0, 0)),
                  pl.BlockSpec(memory_space=pltpu.VMEM), tab, tab],
        out_specs=[row(w) for w in widths],
        out_shape=[jax.ShapeDtypeStruct((T, w), dt) for w, dt in zip(widths, dtypes)],
        compiler_params=_cparams("parallel"),
        name="in_proj",
    )(x2, norm_w, w_in_bf16, cos_t, sin_t)


def _moba_kernel(q_ref, k_ref, v_ref, o_ref, kaug_ref, kmean_ref, vt_ref, acc_ref, *s_refs, nb):
    qi = pl.program_id(2)
    blk = MOBA_BLOCK
    lanes = V7X_LANES
    n_pairs = kaug_ref.shape[0]
    heads = [(pp, hh) for pp in range(n_pairs) for hh in range(2)]
    nbp = kmean_ref.shape[1]

    @pl.when(qi == 0)
    def _():
        rowb = lax.broadcasted_iota(jnp.int32, (nb * blk, lanes), 0) // blk
        col = lax.broadcasted_iota(jnp.int32, (nb * blk, lanes), 1)
        onehot = jnp.where(rowb == col, 1.0, 0.0).astype(BF16)
        for pp in range(n_pairs):
            pl_ = slice(pp * lanes, (pp + 1) * lanes)
            kaug_ref[pp, :, :lanes] = k_ref[0, :, pl_]
            kaug_ref[pp, :, lanes:] = onehot
            means = [jnp.sum(k_ref[0, n * blk:(n + 1) * blk, pl_].astype(F32), axis=0,
                             keepdims=True) * (1.0 / blk) for n in range(nb)]
            km = jnp.concatenate(means + [jnp.zeros((nbp - nb, lanes), F32)] * (nbp > nb), axis=0)
            hi = km.astype(BF16)
            rem = km - hi.astype(F32)
            mid = rem.astype(BF16)
            lo = (rem - mid.astype(F32)).astype(BF16)
            kmean_ref[pp] = jnp.concatenate([hi, mid, lo], axis=1)
            for c in range(nb):
                vt = v_ref[0, c * blk:(c + 1) * blk, pl_].astype(F32).T
                vt_ref[pp, :lanes, c * blk:(c + 1) * blk] = vt.astype(BF16)
            vt_ref[pp, lanes:, :] = jnp.ones((vt_ref.shape[1] - lanes, nb * blk), BF16)

    feat = lax.broadcasted_iota(jnp.int32, (lanes, blk), 0)
    key_i = lax.broadcasted_iota(jnp.int32, (blk, blk), 0)
    qry_i = lax.broadcasted_iota(jnp.int32, (blk, blk), 1)
    own = pl.multiple_of(qi * blk, blk)
    blk_id = lax.broadcasted_iota(jnp.int32, (nbp, 2 * blk), 0)
    slab = 2 * V7X_SUBLANES

    def slab_max(s):
        return jnp.max(s.reshape(s.shape[0] // slab, slab, blk), axis=0)

    q_aug, m_init = [], []
    for pp in range(n_pairs):
        qt = q_ref[0, :, pp * lanes:(pp + 1) * lanes].astype(F32).T
        k_own = kaug_ref[pp, pl.ds(own, blk), :lanes]
        qhs = [jnp.where((feat // (ATT_HEAD_DIM // 2)) % 2 == hh, qt, 0.0).astype(BF16)
               for hh in range(2)]
        q2 = jnp.concatenate(qhs, axis=1)
        gate = jnp.dot(kmean_ref[pp], jnp.concatenate([q2, q2, q2], axis=0),
                       preferred_element_type=F32)
        gate = jnp.where(blk_id < qi, gate, NEG)
        beaten = jnp.zeros((nbp, 2 * blk), F32)
        for n in range(nb):
            gn = gate[n:n + 1, :]
            wins = (gn > gate) | ((gn == gate) & (blk_id > n))
            beaten = beaten + jnp.where(wins, 1.0, 0.0)
        sel = (beaten < MOBA_TOPK) & (blk_id < qi)
        bias = jnp.where(sel, 0.0, NEG).astype(BF16)
        bias = jnp.concatenate([bias, jnp.zeros((lanes - nbp, 2 * blk), BF16)], axis=0)
        for hh in range(2):
            h = 2 * pp + hh
            q_aug.append(jnp.concatenate([qhs[hh], bias[:, hh * blk:(hh + 1) * blk]], axis=0))
            s = jnp.dot(k_own, qhs[hh], preferred_element_type=F32)
            s = jnp.where(key_i <= qry_i, s, NEG)
            m0 = jnp.max(slab_max(s), axis=0, keepdims=True)
            p = jnp.exp(s - m0).astype(BF16)
            acc_ref[h] = jnp.dot(vt_ref[pp, :, pl.ds(own, blk)], p, preferred_element_type=F32)
            m_init.append(m0)

    group = MOBA_GROUP * blk
    n_groups = (qi + MOBA_GROUP) // MOBA_GROUP

    def score_head(g, h):
        off = pl.multiple_of(g * group, group)
        s = jnp.dot(kaug_ref[h // 2, pl.ds(off, group), :], q_aug[h], preferred_element_type=F32)
        s_refs[h][pl.ds(off, group), :] = s
        return jnp.max(slab_max(s), axis=0, keepdims=True)

    def value_head(g, h, gmax, m_old):
        off = pl.multiple_of(g * group, group)
        m_new = jnp.maximum(m_old, gmax)
        alpha = jnp.exp(m_old - m_new)
        p = jnp.exp(s_refs[h][pl.ds(off, group), :] - m_new).astype(BF16)
        acc_ref[h] = alpha * acc_ref[h] + jnp.dot(vt_ref[h // 2, :, pl.ds(off, group)], p,
                                                  preferred_element_type=F32)
        return m_new

    def pipelined(g, carry):
        gmax, ms = carry
        last = len(heads) - 1
        new_gmax, new_ms = [None] * (last + 1), [None] * (last + 1)
        new_gmax[last] = score_head(g + 1, last)
        for h in range(last + 1):
            new_ms[h] = value_head(g, h, gmax[h], ms[h])
            if h < last:
                new_gmax[h] = score_head(g + 1, h)
        return tuple(new_gmax), tuple(new_ms)

    first = tuple(score_head(0, h) for h in range(len(heads)))
    gmax, ms = lax.fori_loop(0, n_groups - 1, pipelined, (first, tuple(m_init)))
    for h in range(len(heads)):
        value_head(n_groups - 1, h, gmax[h], ms[h])

    half = lanes // 2
    outs = []
    for pp in range(n_pairs):
        outs.append(acc_ref[2 * pp, :half, :] / acc_ref[2 * pp, lanes:lanes + 1, :])
        outs.append(acc_ref[2 * pp + 1, half:lanes, :] / acc_ref[2 * pp + 1, lanes:lanes + 1, :])
    o_ref[0] = jnp.concatenate(outs, axis=0).T.astype(BF16)


def _moba(q, k, v):
    B, S, _ = q.shape
    nb = S // MOBA_BLOCK
    assert nb % MOBA_GROUP == 0
    nbp = -(-nb // (2 * V7X_SUBLANES)) * (2 * V7X_SUBLANES)
    n_pairs = ATT_WIDTH // V7X_LANES
    pp = MOBA_PAIRS
    assert n_pairs % pp == 0
    vt_rows = V7X_LANES + 2 * V7X_SUBLANES
    qspec = pl.BlockSpec((1, MOBA_BLOCK, pp * V7X_LANES), lambda b, p, i: (b, i, p))
    kvspec = pl.BlockSpec((1, S, pp * V7X_LANES), lambda b, p, i: (b, 0, p))
    return pl.pallas_call(
        functools.partial(_moba_kernel, nb=nb),
        grid=(B, n_pairs // pp, nb),
        in_specs=[qspec, kvspec, kvspec],
        out_specs=qspec,
        out_shape=jax.ShapeDtypeStruct((B, S, ATT_WIDTH), BF16),
        scratch_shapes=[pltpu.VMEM((pp, S, 2 * V7X_LANES), BF16),
                        pltpu.VMEM((pp, nbp, 3 * V7X_LANES), BF16),
                        pltpu.VMEM((pp, vt_rows, S), BF16),
                        pltpu.VMEM((2 * pp, vt_rows, MOBA_BLOCK), F32)]
                       + [pltpu.VMEM((S, MOBA_BLOCK), F32)] * (2 * pp),
        compiler_params=_cparams("parallel", "parallel", "arbitrary"),
        name="moba",
    )(q, k, v)


def _hgrn_level_sizes(chunk):
    return [chunk >> (i + 1) for i in range(int(math.log2(chunk)))]


def _hgrn_constants(chunk):
    t = np.arange(chunk)
    mats = [(t[None, :] <= t[:, None]),
            (t[None, :] > t[:, None])]
    qrows, pmasks = [], []
    for bs in _hgrn_level_sizes(chunk):
        blk = t // bs
        odd = (blk % 2) == 1
        lo, hi = blk * bs, (blk + 1) * bs
        u = t[None, :]
        m_odd = (u >= lo[:, None]) & (u <= t[:, None])
        m_even = (u > t[:, None]) & (u < hi[:, None])
        mats.append(np.where(odd[:, None], m_odd, m_even))
        qrows.append(odd)
        pmasks.append(odd[:, None] & (blk[None, :] == blk[:, None] - 1))
    pmasks.append(t[None, :] == t[:, None])
    summat = np.concatenate(mats, axis=0).astype(np.float32)
    qrow = np.stack(qrows, axis=0).astype(np.float32)
    pmask = np.stack(pmasks, axis=0).astype(np.float32)
    return summat, qrow, pmask


def _hgrn_kernel(q_ref, f_ref, i_ref, g_ref, lb_ref, nw_ref, sm_ref, qrow_ref, pm_ref,
                 o_ref, state_ref, *, rows):
    C = HGRN_CHUNK
    n_levels = qrow_ref.shape[0]

    @pl.when(pl.program_id(1) == 0)
    def _():
        state_ref[...] = jnp.zeros_like(state_ref)

    lb = lb_ref[...]
    summat = sm_ref[...]
    for c in range(rows // C):
        rs = slice(c * C, (c + 1) * C)
        fg = lb + (1.0 - lb) * jax.nn.sigmoid(f_ref[0, rs, :])
        logf = jnp.log(fg)
        hi = logf.astype(BF16)
        rem = logf - hi.astype(F32)
        mid = rem.astype(BF16)
        lo = (rem - mid.astype(F32)).astype(BF16)
        sums = jnp.dot(summat, jnp.concatenate([hi, mid, lo], axis=0),
                       preferred_element_type=F32)
        for h in range(HGRN_HEADS):
            ls = slice(h * HGRN_DIM, (h + 1) * HGRN_DIM)
            qf = jax.nn.silu(q_ref[0, rs, ls].astype(F32))
            kf = 1.0 - fg[:, ls]
            iv = i_ref[0, rs, ls]
            bcum = sums[0:C, ls]
            bsuf = sums[C:2 * C, ls]
            att = _nt_dot(qf.astype(BF16), kf.astype(BF16)) * pm_ref[n_levels]
            for lv in range(n_levels):
                w = jnp.exp(sums[(2 + lv) * C:(3 + lv) * C, ls])
                qrow = qrow_ref[lv]
                z = (jnp.where(qrow > 0.5, qf, kf) * w).astype(BF16)
                att = att + _nt_dot(z, z) * pm_ref[lv]
            o = jnp.dot(att.astype(BF16), iv, preferred_element_type=F32)
            st = state_ref[h]
            o = o + _nt_dot((qf * jnp.exp(bcum)).astype(BF16), st.astype(BF16))
            kdec = (kf * jnp.exp(bsuf)).astype(BF16)
            state_ref[h] = st * jnp.exp(bcum[C - 1:C, :]) + _tn_dot(iv, kdec)
            o = o * lax.rsqrt(jnp.mean(o * o, axis=-1, keepdims=True) + RMS_EPS)
            o = o * nw_ref[:, ls] * jax.nn.silu(g_ref[0, rs, ls].astype(F32))
            o_ref[0, rs, ls] = o.astype(BF16)


def _hgrn(qb, fb, ib, gb, lb, norm_w):
    B, S, W = qb.shape
    rows = min(HGRN_ROWS, S)
    summat, qrow, pmask = _hgrn_constants(HGRN_CHUNK)
    summat = jnp.asarray(np.concatenate([summat] * 3, axis=1), BF16)
    n_levels = qrow.shape[0]
    blk = pl.BlockSpec((1, rows, W), lambda b, s: (b, s, 0))
    vec = pl.BlockSpec((1, W), lambda b, s: (0, 0))
    const = lambda a: pl.BlockSpec(a.shape, lambda b, s: (0,) * a.ndim)
    qrow3 = qrow.reshape(n_levels, HGRN_CHUNK, 1)
    return pl.pallas_call(
        functools.partial(_hgrn_kernel, rows=rows),
        grid=(B, S // rows),
        in_specs=[blk, blk, blk, blk, vec, vec, const(summat), const(qrow3), const(pmask)],
        out_specs=blk,
        out_shape=jax.ShapeDtypeStruct((B, S, W), BF16),
        scratch_shapes=[pltpu.VMEM((HGRN_HEADS, HGRN_DIM, HGRN_DIM), F32)],
        compiler_params=_cparams("parallel", "arbitrary"),
        name="hgrn",
    )(qb, fb, ib, gb, lb, norm_w, jnp.asarray(summat), jnp.asarray(qrow3), jnp.asarray(pmask))


def _mix_route_kernel(ya_ref, yb_ref, ga_ref, gb_ref, x_ref, wa_ref, wb_ref, wo_ref, nw_ref,
                      rw_ref, rb_ref, tri_ref, anchor_ref,
                      x1_ref, h2_ref, e_ref, w_ref, rank_ref, cnt_ref, carry_ref):
    @pl.when(pl.program_id(0) == 0)
    def _():
        carry_ref[...] = jnp.zeros_like(carry_ref)

    ua = jnp.dot(ya_ref[...], wa_ref[...], preferred_element_type=F32)
    ub = jnp.dot(yb_ref[...], wb_ref[...], preferred_element_type=F32)
    mixed = (jax.nn.sigmoid(ga_ref[...].astype(F32)) * ua
             + jax.nn.sigmoid(gb_ref[...].astype(F32)) * ub)
    x1 = x_ref[...] + jnp.dot(mixed.astype(BF16), wo_ref[...], preferred_element_type=F32)
    x1_ref[...] = x1
    h2 = x1 * lax.rsqrt(jnp.mean(x1 * x1, axis=-1, keepdims=True) + RMS_EPS) * nw_ref[...]
    half = h2.shape[1] // 2
    h2_ref[...] = _pack_bf16_pair(h2[:, :half], h2[:, half:])

    tm = x1.shape[0]
    h2_hi = h2.astype(BF16)
    h2_lo = (h2 - h2_hi.astype(F32)).astype(BF16)
    logits = _nt_dot(rw_ref[...], jnp.concatenate([h2_hi, h2_lo, h2_hi], axis=1)) + rb_ref[...]
    eid = lax.broadcasted_iota(jnp.int32, (N_EXPERTS, tm), 0)
    work = logits
    es, vs = [], []
    for _ in range(TOP_K):
        mx = jnp.max(work, axis=0, keepdims=True)
        idx = jnp.min(jnp.where(work == mx, eid, N_EXPERTS), axis=0, keepdims=True)
        es.append(idx)
        vs.append(mx)
        work = jnp.where(eid == idx, -jnp.inf, work)
    ex = [jnp.exp(v - vs[0]) for v in vs]
    den = ex[0] + ex[1] + ex[2] + ex[3]
    multi = jnp.zeros((N_EXPERTS, tm), F32)
    for k in range(TOP_K):
        multi = multi + jnp.where(eid == es[k], 1.0, 0.0)
    before = jnp.dot(multi.astype(BF16), tri_ref[...], preferred_element_type=F32) + carry_ref[...]
    for k in range(TOP_K):
        e_ref[k:k + 1, :] = es[k]
        w_ref[k:k + 1, :] = ex[k] / den
        rank_ref[k:k + 1, :] = jnp.sum(jnp.where(eid == es[k], before, 0.0), axis=0,
                                       keepdims=True).astype(jnp.int32)
    carry_ref[...] = carry_ref[...] + jnp.sum(multi, axis=1, keepdims=True)
    cnt_ref[...] = jnp.broadcast_to(carry_ref[...], cnt_ref.shape)


def _mix_route(ya, yb, ga, gb, x2, wa, wb, wo, norm_w, rw_t, rb, row0, rows, anchor):
    D = x2.shape[1]
    T = rows
    tm = min(MIX_ROWS, T)
    row = lambda w: pl.BlockSpec((tm, w), lambda i: (i + row0 // tm, 0))
    out_row = pl.BlockSpec((tm, D), lambda i: (i, 0))
    whole = pl.BlockSpec(memory_space=pltpu.VMEM)
    kt = pl.BlockSpec((TOP_K, tm), lambda i: (0, i))
    tri = jnp.asarray(np.triu(np.ones((tm, tm), np.float32), 1), BF16)
    return pl.pallas_call(
        _mix_route_kernel,
        grid=(T // tm,),
        in_specs=[row(ATT_WIDTH), row(HGRN_WIDTH), row(D), row(D), row(D),
                  whole, whole, whole, pl.BlockSpec((1, D), lambda i: (0, 0)),
                  whole, whole, whole, pl.BlockSpec(memory_space=pltpu.SMEM)],
        out_specs=[out_row, pl.BlockSpec((tm, D // 2), lambda i: (i, 0)), kt, kt, kt,
                   pl.BlockSpec((N_EXPERTS, V7X_LANES), lambda i: (0, 0))],
        out_shape=[jax.ShapeDtypeStruct((T, D), F32), jax.ShapeDtypeStruct((T, D // 2), jnp.uint32),
                   jax.ShapeDtypeStruct((TOP_K, T), jnp.int32),
                   jax.ShapeDtypeStruct((TOP_K, T), F32),
                   jax.ShapeDtypeStruct((TOP_K, T), jnp.int32),
                   jax.ShapeDtypeStruct((N_EXPERTS, V7X_LANES), F32)],
        scratch_shapes=[pltpu.VMEM((N_EXPERTS, 1), F32)],
        compiler_params=_cparams("arbitrary"),
        name="mix_route",
    )(ya, yb, ga, gb, x2, wa, wb, wo, norm_w, rw_t, rb, tri, anchor)


def _sc_worker_id():
    return lax.axis_index("s") * V7X_SC_CORES + lax.axis_index("c")


def _sc_kernel(body, out_rows, like, window, name, extra_scratch=()):
    mesh = plsc.VectorSubcoreMesh(core_axis_name="c", subcore_axis_name="s")
    return pl.kernel(
        body, mesh=mesh,
        out_type=jax.ShapeDtypeStruct((out_rows,) + like.shape[1:], like.dtype),
        scratch_types=[pltpu.VMEM((window,), jnp.int32),
                       pltpu.VMEM((window,) + like.shape[1:], like.dtype), *extra_scratch],
        name=name)


def _scatter_rows(dest, h2, out_rows, pad_rows):
    T = dest.shape[1]
    n_workers = V7X_SC_CORES * V7X_SC_SUBCORES
    per_worker = T // n_workers
    window = min(SC_ROWS, per_worker)
    n_pad = pad_rows.shape[1]
    assert pad_rows.shape[0] == n_workers and n_pad % window == 0

    def body(dest_hbm, src_hbm, pad_hbm, zero_hbm, dst_hbm, idx_v, rows_v, pad_v, zero_v, all_v, sem,
             fill_sem):
        del idx_v
        wid = _sc_worker_id()
        base = wid * per_worker

        pltpu.sync_copy(pad_hbm.at[pl.ds(wid * n_pad, n_pad)], pad_v)
        pltpu.sync_copy(zero_hbm, zero_v)
        fills = [pltpu.async_copy(zero_v, dst_hbm.at[pad_v.at[pl.ds(c * window, window)]],
                                     fill_sem)
                 for c in range(n_pad // window)]

        for k in range(TOP_K):
            pltpu.sync_copy(dest_hbm.at[pl.ds(k * T + base, per_worker)],
                            all_v.at[pl.ds(k * per_worker, per_worker)])

        @pl.loop(0, per_worker // window)
        def _(c):
            pltpu.sync_copy(src_hbm.at[pl.ds(base + c * window, window)], rows_v)
            copies = [pltpu.async_copy(
                rows_v, dst_hbm.at[all_v.at[pl.ds(k * per_worker + c * window, window)]], sem)
                for k in range(TOP_K)]
            for cp in copies:
                cp.wait()

        for f in fills:
            f.wait()

    zeros = jnp.zeros((window,) + h2.shape[1:], h2.dtype)
    extra = [pltpu.VMEM((n_pad,), jnp.int32), pltpu.VMEM(zeros.shape, zeros.dtype),
             pltpu.VMEM((TOP_K * per_worker,), jnp.int32), pltpu.SemaphoreType.DMA,
             pltpu.SemaphoreType.DMA]
    return _sc_kernel(body, out_rows, h2, window, "scatter_rows", extra)(
        dest.reshape(-1), h2, pad_rows.reshape(-1), zeros)


def _gather_rows(dest, yg):
    T = dest.shape[1]
    n_workers = V7X_SC_CORES * V7X_SC_SUBCORES
    per_worker = TOP_K * T // n_workers
    window = min(SC_ROWS, per_worker)

    def body(dest_hbm, src_hbm, dst_hbm, idx_v, rows_v, all_v):
        del idx_v
        base = _sc_worker_id() * per_worker
        pltpu.sync_copy(dest_hbm.at[pl.ds(base, per_worker)], all_v)

        @pl.loop(0, per_worker // window)
        def _(c):
            pltpu.sync_copy(src_hbm.at[all_v.at[pl.ds(c * window, window)]], rows_v)
            pltpu.sync_copy(rows_v, dst_hbm.at[pl.ds(base + c * window, window)])

    extra = [pltpu.VMEM((per_worker,), jnp.int32)]
    return _sc_kernel(body, TOP_K * T, yg, window, "gather_rows", extra)(dest.reshape(-1), yg)


def _experts_kernel(blk_ref, exp_ref, lo_ref, hi_ref, slot_ref, efirst_ref, enext_ref,
                    x_ref, wgu_hbm, bg_ref, bu_ref, wd_hbm, bd_ref, perm_ref, y_ref,
                    wgu_stage, wd_stage, wg_buf, wu_buf, wd_buf, sems):
    w = pl.program_id(0)
    lo = lo_ref[w]
    hi = hi_ref[w]
    slot = slot_ref[w]

    tm = x_ref.shape[0]
    n_out = V7X_MXU_DIM

    def weight_copies(e, s):
        return [pltpu.make_async_copy(wgu_hbm.at[e], wgu_stage.at[s], sems.at[s, 0]),
                pltpu.make_async_copy(wd_hbm.at[e], wd_stage.at[s], sems.at[s, 1])]

    @pl.when(w == 0)
    def _():
        for c in weight_copies(exp_ref[0], slot):
            c.start()

    @pl.when(efirst_ref[w] == 1)
    def _():
        for c in weight_copies(exp_ref[w], slot):
            c.wait()

        @pl.when(enext_ref[w] >= 0)
        def _():
            for c in weight_copies(enext_ref[w], 1 - slot):
                c.start()

        half = V7X_MXU_DIM // 2
        for g in range(wgu_stage.shape[2] // V7X_MXU_DIM):
            wb = wgu_stage[slot, :, g * V7X_MXU_DIM:(g + 1) * V7X_MXU_DIM].astype(BF16)
            d = jnp.dot(wb, perm_ref[...], preferred_element_type=F32).astype(BF16)
            wg_buf[:, g * half:(g + 1) * half] = d[:, :half]
            wu_buf[:, g * half:(g + 1) * half] = d[:, half:]
        wd_buf[...] = wd_stage[slot].astype(BF16)

    sb = EXPERT_SUB_ROWS

    def ffn(r0):
        rs = slice(r0, r0 + sb)
        x = jnp.concatenate(_unpack_bf16_pair(x_ref[rs, :]), axis=1).astype(BF16)
        g = jnp.dot(x, wg_buf[...], preferred_element_type=F32) + bg_ref[0]
        u = jnp.dot(x, wu_buf[...], preferred_element_type=F32) + bu_ref[0]
        g = jnp.minimum(g, SWIGLU_LIMIT)
        u = jnp.clip(u, -SWIGLU_LIMIT, SWIGLU_LIMIT)
        act = ((u + 1.0) * (g * jax.nn.sigmoid(g * SWIGLU_ALPHA))).astype(BF16)
        half = y_ref.shape[1]

        def down(c0):
            cs = slice(c0, c0 + n_out)
            return (jnp.dot(act, wd_buf[:, cs], preferred_element_type=F32)
                    + bd_ref[0, :, cs])

        for c in range(half // n_out):
            cs = slice(c * n_out, (c + 1) * n_out)
            y_ref[rs, cs] = _pack_bf16_pair(down(c * n_out), down(half + c * n_out))

    n_sub = tm // sb
    fused = (lo == 0) & (hi == tm)

    @pl.when(fused)
    def _():
        for j in range(n_sub):
            ffn(j * sb)

    for j in range(n_sub):
        @pl.when(jnp.logical_not(fused) & (lo <= j * sb) & (hi > j * sb))
        def _():
            ffn(j * sb)


def _work_items(counts, n_rows, tm):
    nblk = n_rows // tm
    n_items = nblk + N_EXPERTS - 1
    end = jnp.cumsum(counts)
    start = end - counts
    fb = start // tm
    nitems = jnp.where(counts > 0, (end - 1) // tm - fb + 1, 0)
    item_end = jnp.cumsum(nitems)
    item_start = item_end - nitems
    w = jnp.arange(n_items, dtype=jnp.int32)
    valid = w < item_end[-1]
    wc = jnp.minimum(w, item_end[-1] - 1)
    e = jnp.sum(wc[:, None] >= item_end[None, :], axis=1).astype(jnp.int32)
    e = jnp.minimum(e, N_EXPERTS - 1)
    onehot = e[:, None] == jnp.arange(N_EXPERTS, dtype=jnp.int32)[None, :]
    pick = lambda table: jnp.sum(jnp.where(onehot, table[None, :], 0), axis=1)
    blk = (pick(fb) + (wc - pick(item_start))).astype(jnp.int32)
    lo = jnp.maximum(pick(start), blk * tm) - blk * tm
    hi = jnp.minimum(pick(end), (blk + 1) * tm) - blk * tm
    lo = jnp.where(valid, lo, 0).astype(jnp.int32)
    hi = jnp.where(valid, hi, 0).astype(jnp.int32)
    used = counts > 0
    ids = jnp.arange(N_EXPERTS, dtype=jnp.int32)
    slot = pick((jnp.cumsum(used) - 1) % 2).astype(jnp.int32)
    efirst = (valid & (w == pick(item_start))).astype(jnp.int32)
    later = jnp.where(used[None, :] & (ids[None, :] > ids[:, None]), ids[None, :], N_EXPERTS)
    next_used = jnp.min(later, axis=1)
    enext = pick(jnp.where(next_used < N_EXPERTS, next_used, -1)).astype(jnp.int32)
    return blk, e, lo, hi, slot, efirst, enext


def _gate_up_split_matrix():
    i = np.arange(V7X_MXU_DIM)
    src = np.where(i < V7X_MXU_DIM // 2, 2 * i, 2 * (i - V7X_MXU_DIM // 2) + 1)
    perm = np.zeros((V7X_MXU_DIM, V7X_MXU_DIM), np.float32)
    perm[src, i] = 1.0
    return jnp.asarray(perm, BF16)


def _experts(xg, counts, w_gate_up, bg, bu, w_down, bd):
    A = xg.shape[0]
    D = 2 * xg.shape[1]
    tm = min(EXPERT_ROWS, A)
    F = w_down.shape[1]
    items = _work_items(counts, A, tm)
    n_items = A // tm + N_EXPERTS - 1
    xs = pl.BlockSpec((tm, D // 2), lambda w, blk, e, *_: (blk[w], 0))
    bias = lambda c: pl.BlockSpec((1, 1, c), lambda w, blk, e, *_: (e[w], 0, 0))
    hbm = pl.BlockSpec(memory_space=pl.ANY)
    return pl.pallas_call(
        _experts_kernel,
        grid_spec=pltpu.PrefetchScalarGridSpec(
            num_scalar_prefetch=len(items),
            grid=(n_items,),
            in_specs=[xs, hbm, bias(F), bias(F), hbm, bias(D), pl.BlockSpec(memory_space=pltpu.VMEM)],
            out_specs=xs,
            scratch_shapes=[pltpu.VMEM((2, D, 2 * F), F32), pltpu.VMEM((2, F, D), F32),
                            pltpu.VMEM((D, F), BF16), pltpu.VMEM((D, F), BF16),
                            pltpu.VMEM((F, D), BF16), pltpu.SemaphoreType.DMA((2, 2))],
        ),
        out_shape=jax.ShapeDtypeStruct(xg.shape, jnp.uint32),
        compiler_params=_cparams("arbitrary"),
        name="experts",
    )(*items, xg, w_gate_up, bg, bu, w_down, bd, _gate_up_split_matrix())


def _combine_kernel(x1_ref, yk_ref, w_ref, nw_ref, *rest):
    o_ref = rest[-1]
    w = w_ref[...]
    x = x1_ref[...]
    for k in range(TOP_K):
        x = x + jnp.concatenate(_unpack_bf16_pair(yk_ref[k]), axis=1) * w[:, k:k + 1]
    o_ref[...] = x * lax.rsqrt(jnp.mean(x * x, axis=-1, keepdims=True) + RMS_EPS) * nw_ref[...]


def _combine(x1, yk, w_tk, norm_w, out_rows, row0, prev_out):
    Tg, D = x1.shape
    tm = min(COMBINE_ROWS, Tg)
    row = pl.BlockSpec((tm, D), lambda i: (i, 0))
    in_specs = [row, pl.BlockSpec((TOP_K, tm, D // 2), lambda i: (0, i, 0)),
                pl.BlockSpec((tm, TOP_K), lambda i: (i, 0)),
                pl.BlockSpec((1, D), lambda i: (0, 0))]
    args = [x1, yk, w_tk, norm_w]
    aliases = {}
    if prev_out is not None:
        in_specs.append(pl.BlockSpec(memory_space=pl.ANY))
        args.append(prev_out)
        aliases = {len(args) - 1: 0}
    return pl.pallas_call(
        _combine_kernel,
        grid=(Tg // tm,),
        in_specs=in_specs,
        out_specs=pl.BlockSpec((tm, D), lambda i: (i + row0 // tm, 0)),
        out_shape=jax.ShapeDtypeStruct((out_rows, D), F32),
        input_output_aliases=aliases,
        compiler_params=_cparams("parallel"),
        name="combine",
    )(*args)


def _qk_column_order():
    half = ATT_HEAD_DIM // 2
    order = []
    for p in range(ATT_HEADS // 2):
        for part in range(2):
            for h in (2 * p, 2 * p + 1):
                order.extend(range(h * ATT_HEAD_DIM + part * half, h * ATT_HEAD_DIM + (part + 1) * half))
    return np.asarray(order, np.int32)


def _rope_tables(seq):
    half = ATT_HEAD_DIM // 2
    inv = ROPE_THETA ** (-(jnp.arange(half, dtype=F32) * 2.0 / ATT_HEAD_DIM))
    ang = jnp.arange(seq, dtype=F32)[:, None] * inv[None, :]
    cos, sin = jnp.cos(ang), jnp.sin(ang)
    return (jnp.concatenate([cos, cos, cos, cos], axis=1),
            jnp.concatenate([-sin, -sin, sin, sin], axis=1))


def kernel(x, norm1_w, w_in, moba_up, hgrn_lb_logits, hgrn_norm_w, hgrn_up, w_out, norm2_w,
           router_w, router_b, w_gate_up, b_gate_up, w_down, b_down, final_norm_w):
    B, S, D = x.shape
    T = B * S
    assert S % MOBA_BLOCK == 0 and w_in.shape[0] == 1
    x2 = x.reshape(T, D)

    perm = _qk_column_order()
    w0 = w_in[0]
    w_in_p = jnp.concatenate([w0[:, :ATT_WIDTH][:, perm], w0[:, ATT_WIDTH:2 * ATT_WIDTH][:, perm],
                              w0[:, 2 * ATT_WIDTH:]], axis=1).astype(BF16)
    cos_t, sin_t = _rope_tables(S)
    lb = jnp.cumsum(jax.nn.softmax(hgrn_lb_logits.astype(F32), axis=0), axis=0)[0:1]

    qa, ka, va, qb, fb, ib, gb, ga, gtb = _in_proj(x2, norm1_w, w_in_p, cos_t, sin_t, S)
    r3 = lambda a: a.reshape(B, S, a.shape[1])

    rw_t = router_w[0].T
    rw_hi = rw_t.astype(BF16)
    rw_lo = (rw_t - rw_hi.astype(F32)).astype(BF16)
    mix_w = (moba_up[0].astype(BF16), hgrn_up[0].astype(BF16), w_out[0].astype(BF16), norm2_w,
             jnp.concatenate([rw_hi, rw_hi, rw_lo], axis=1), router_b[0][:, None])
    expert_w = (w_gate_up[0], b_gate_up[0][:, None, 0::2], b_gate_up[0][:, None, 1::2],
                w_down[0], b_down[0][:, None, :])

    ya = _moba(r3(qa), r3(ka), r3(va)).reshape(T, ATT_WIDTH)
    yb = _hgrn(r3(qb), r3(fb), r3(ib), r3(gb), lb, hgrn_norm_w).reshape(T, HGRN_WIDTH)

    Tg = T // MOE_TOKEN_GROUPS
    staged = []
    anchor = jnp.zeros((1,), jnp.int32)
    for g in range(MOE_TOKEN_GROUPS):
        x1, h2, top_e, top_w, rank, cnt = _mix_route(ya, yb, ga, gtb, x2, *mix_w, g * Tg, Tg,
                                                     anchor)
        counts = cnt[:, 0].astype(jnp.int32)
        sb = EXPERT_SUB_ROWS
        padded = (counts + sb - 1) // sb * sb
        start = jnp.cumsum(padded) - padded
        dest = rank + jnp.sum(jnp.where(top_e[:, :, None] == jnp.arange(N_EXPERTS)[None, None, :],
                                        start[None, None, :], 0), axis=-1)
        body_rows = TOP_K * Tg + N_EXPERTS * sb
        spare_row = body_rows
        i = jnp.arange(sb, dtype=jnp.int32)[None, :]
        n_fill = (padded - counts)[:, None]
        pad_rows = jnp.where(n_fill > 0, (start + counts)[:, None] + i % jnp.maximum(n_fill, 1),
                             spare_row + i)
        xg = _scatter_rows(dest, h2, body_rows + EXPERT_ROWS, pad_rows)
        staged.append((xg, padded, dest, x1, top_w))
        anchor = dest[0, :1] + pad_rows[0, :1]

    out = None
    for g, (xg, padded, dest, x1, top_w) in enumerate(staged):
        yg = _experts(xg, padded, *expert_w)
        yk = _gather_rows(dest, yg).reshape(TOP_K, Tg, D // 2)
        out = _combine(x1, yk, top_w.T, final_norm_w[None, :], T, g * Tg, out)
    return out.reshape(B, S, D)
```

```python
import functools
import math

import numpy as np
import jax
import jax.numpy as jnp
from jax import lax
from jax.experimental import pallas as pl
from jax.experimental.pallas import tpu as pltpu
from jax.experimental.pallas import tpu_sc as plsc

ATT_HEADS = 8
ATT_HEAD_DIM = 64
ATT_WIDTH = ATT_HEADS * ATT_HEAD_DIM
MOBA_BLOCK = 256
MOBA_TOPK = 3
ROPE_THETA = 10000.0
HGRN_HEADS = 4
HGRN_DIM = 128
HGRN_WIDTH = HGRN_HEADS * HGRN_DIM
HGRN_CHUNK = 64
N_EXPERTS = 32
TOP_K = 4
SWIGLU_LIMIT = 7.0
SWIGLU_ALPHA = 1.702
RMS_EPS = 1e-6
NEG = -1e30

V7X_LANES = 128
V7X_SUBLANES = 8
V7X_MXU_DIM = 256
V7X_VMEM_LIMIT_BYTES = 56 * 1024 * 1024
V7X_SC_CORES = 2
V7X_SC_SUBCORES = 16

MOBA_PAIRS = 2
MOBA_GROUP = 4
PROJ_ROWS = 1024
PROJ_COLS = 512
HGRN_ROWS = 1024
MIX_ROWS = 1024
EXPERT_ROWS = 1024
EXPERT_SUB_ROWS = 256
MOE_TOKEN_GROUPS = 2
SC_ROWS = 64
COMBINE_ROWS = 1024

F32 = jnp.float32
BF16 = jnp.bfloat16


def _nt_dot(a, b, precision=None):
    return lax.dot_general(a, b, (((1,), (1,)), ((), ())), precision=precision,
                           preferred_element_type=F32)


def _tn_dot(a, b, precision=None):
    return lax.dot_general(a, b, (((0,), (0,)), ((), ())), precision=precision,
                           preferred_element_type=F32)


def _cparams(*sem):
    return pltpu.CompilerParams(dimension_semantics=sem, vmem_limit_bytes=V7X_VMEM_LIMIT_BYTES)


def _pack_bf16_pair(lo, hi):
    lo_bits = lax.bitcast_convert_type(lo.astype(BF16).astype(F32), jnp.uint32)
    hi_bits = lax.bitcast_convert_type(hi.astype(BF16).astype(F32), jnp.uint32)
    return (lo_bits >> 16) | hi_bits


def _unpack_bf16_pair(word):
    lo = lax.bitcast_convert_type(word << 16, F32)
    hi = lax.bitcast_convert_type(word & jnp.uint32(0xFFFF0000), F32)
    return lo, hi


def _in_proj_kernel(x_ref, nw_ref, w_ref, cos_ref, sin_ref,
                    qa_ref, ka_ref, va_ref, qb_ref, fb_ref, ib_ref, gb_ref, ga_ref, gtb_ref):
    x = x_ref[...]
    h = x * lax.rsqrt(jnp.mean(x * x, axis=-1, keepdims=True) + RMS_EPS) * nw_ref[...]
    h = h.astype(BF16)
    cos = cos_ref[...]
    sin = sin_ref[...]

    def proj(c):
        return jnp.dot(h, w_ref[:, c * PROJ_COLS:(c + 1) * PROJ_COLS], preferred_element_type=F32)

    def rope(t):
        out = []
        for j in range(PROJ_COLS // V7X_LANES):
            tj = t[:, j * V7X_LANES:(j + 1) * V7X_LANES]
            out.append(tj * cos + pltpu.roll(tj, V7X_LANES // 2, 1) * sin)
        return jnp.concatenate(out, axis=1)

    qa_ref[...] = (rope(proj(0)) * (ATT_HEAD_DIM ** -0.5)).astype(BF16)
    ka_ref[...] = rope(proj(1)).astype(BF16)
    va_ref[...] = proj(2).astype(BF16)
    qb_ref[...] = proj(3).astype(BF16)
    fb_ref[...] = proj(4)
    ib_ref[...] = proj(5).astype(BF16)
    gb_ref[...] = proj(6).astype(BF16)
    ga_ref[:, :PROJ_COLS] = proj(7).astype(BF16)
    ga_ref[:, PROJ_COLS:] = proj(8).astype(BF16)
    gtb_ref[:, :PROJ_COLS] = proj(9).astype(BF16)
    gtb_ref[:, PROJ_COLS:] = proj(10).astype(BF16)


def _in_proj(x2, norm_w, w_in_bf16, cos_t, sin_t, seq):
    T, D = x2.shape
    tm = min(PROJ_ROWS, seq)
    n_seq_tiles = seq // tm
    row = lambda w: pl.BlockSpec((tm, w), lambda i: (i, 0))
    tab = pl.BlockSpec((tm, V7X_LANES), lambda i: (i % n_seq_tiles, 0))
    widths = [ATT_WIDTH] * 3 + [HGRN_WIDTH] * 4 + [D, D]
    dtypes = [BF16, BF16, BF16, BF16, F32, BF16, BF16, BF16, BF16]
    return pl.pallas_call(
        _in_proj_kernel,
        grid=(T // tm,),
        in_specs=[row(D), pl.BlockSpec((1, D), lambda i: (0, 0)),
                  pl.BlockSpec(memory_space=pltpu.VMEM), tab, tab],
        out_specs=[row(w) for w in widths],
        out_shape=[jax.ShapeDtypeStruct((T, w), dt) for w, dt in zip(widths, dtypes)],
        compiler_params=_cparams("parallel"),
        name="in_proj",
    )(x2, norm_w, w_in_bf16, cos_t, sin_t)


def _moba_kernel(q_ref, k_ref, v_ref, o_ref, kaug_ref, kmean_ref, vt_ref, acc_ref, *s_refs, nb):
    qi = pl.program_id(2)
    blk = MOBA_BLOCK
    lanes = V7X_LANES
    n_pairs = kaug_ref.shape[0]
    heads = [(pp, hh) for pp in range(n_pairs) for hh in range(2)]
    nbp = kmean_ref.shape[1]

    @pl.when(qi == 0)
    def _():
        rowb = lax.broadcasted_iota(jnp.int32, (nb * blk, lanes), 0) // blk
        col = lax.broadcasted_iota(jnp.int32, (nb * blk, lanes), 1)
        onehot = jnp.where(rowb == col, 1.0, 0.0).astype(BF16)
        for pp in range(n_pairs):
            pl_ = slice(pp * lanes, (pp + 1) * lanes)
            kaug_ref[pp, :, :lanes] = k_ref[0, :, pl_]
            kaug_ref[pp, :, lanes:] = onehot
            means = [jnp.sum(k_ref[0, n * blk:(n + 1) * blk, pl_].astype(F32), axis=0,
                             keepdims=True) * (1.0 / blk) for n in range(nb)]
            km = jnp.concatenate(means + [jnp.zeros((nbp - nb, lanes), F32)] * (nbp > nb), axis=0)
            hi = km.astype(BF16)
            rem = km - hi.astype(F32)
            mid = rem.astype(BF16)
            lo = (rem - mid.astype(F32)).astype(BF16)
            kmean_ref[pp] = jnp.concatenate([hi, mid, lo], axis=1)
            for c in range(nb):
                vt = v_ref[0, c * blk:(c + 1) * blk, pl_].astype(F32).T
                vt_ref[pp, :lanes, c * blk:(c + 1) * blk] = vt.astype(BF16)
            vt_ref[pp, lanes:, :] = jnp.ones((vt_ref.shape[1] - lanes, nb * blk), BF16)

    feat = lax.broadcasted_iota(jnp.int32, (lanes, blk), 0)
    key_i = lax.broadcasted_iota(jnp.int32, (blk, blk), 0)
    qry_i = lax.broadcasted_iota(jnp.int32, (blk, blk), 1)
    own = pl.multiple_of(qi * blk, blk)
    blk_id = lax.broadcasted_iota(jnp.int32, (nbp, 2 * blk), 0)
    slab = 2 * V7X_SUBLANES

    def slab_max(s):
        return jnp.max(s.reshape(s.shape[0] // slab, slab, blk), axis=0)

    q_aug, m_init = [], []
    for pp in range(n_pairs):
        qt = q_ref[0, :, pp * lanes:(pp + 1) * lanes].astype(F32).T
        k_own = kaug_ref[pp, pl.ds(own, blk), :lanes]
        qhs = [jnp.where((feat // (ATT_HEAD_DIM // 2)) % 2 == hh, qt, 0.0).astype(BF16)
               for hh in range(2)]
        q2 = jnp.concatenate(qhs, axis=1)
        gate = jnp.dot(kmean_ref[pp], jnp.concatenate([q2, q2, q2], axis=0),
                       preferred_element_type=F32)
        gate = jnp.where(blk_id < qi, gate, NEG)
        beaten = jnp.zeros((nbp, 2 * blk), F32)
        for n in range(nb):
            gn = gate[n:n + 1, :]
            wins = (gn > gate) | ((gn == gate) & (blk_id > n))
            beaten = beaten + jnp.where(wins, 1.0, 0.0)
        sel = (beaten < MOBA_TOPK) & (blk_id < qi)
        bias = jnp.where(sel, 0.0, NEG).astype(BF16)
        bias = jnp.concatenate([bias, jnp.zeros((lanes - nbp, 2 * blk), BF16)], axis=0)
        for hh in range(2):
            h = 2 * pp + hh
            q_aug.append(jnp.concatenate([qhs[hh], bias[:, hh * blk:(hh + 1) * blk]], axis=0))
            s = jnp.dot(k_own, qhs[hh], preferred_element_type=F32)
            s = jnp.where(key_i <= qry_i, s, NEG)
            m0 = jnp.max(slab_max(s), axis=0, keepdims=True)
            p = jnp.exp(s - m0).astype(BF16)
            acc_ref[h] = jnp.dot(vt_ref[pp, :, pl.ds(own, blk)], p, preferred_element_type=F32)
            m_init.append(m0)

    group = MOBA_GROUP * blk
    n_groups = (qi + MOBA_GROUP) // MOBA_GROUP

    def score_head(g, h):
        off = pl.multiple_of(g * group, group)
        s = jnp.dot(kaug_ref[h // 2, pl.ds(off, group), :], q_aug[h], preferred_element_type=F32)
        s_refs[h][pl.ds(off, group), :] = s
        return jnp.max(slab_max(s), axis=0, keepdims=True)

    def value_head(g, h, gmax, m_old):
        off = pl.multiple_of(g * group, group)
        m_new = jnp.maximum(m_old, gmax)
        alpha = jnp.exp(m_old - m_new)
        p = jnp.exp(s_refs[h][pl.ds(off, group), :] - m_new).astype(BF16)
        acc_ref[h] = alpha * acc_ref[h] + jnp.dot(vt_ref[h // 2, :, pl.ds(off, group)], p,
                                                  preferred_element_type=F32)
        return m_new

    def pipelined(g, carry):
        gmax, ms = carry
        last = len(heads) - 1
        new_gmax, new_ms = [None] * (last + 1), [None] * (last + 1)
        new_gmax[last] = score_head(g + 1, last)
        for h in range(last + 1):
            new_ms[h] = value_head(g, h, gmax[h], ms[h])
            if h < last:
                new_gmax[h] = score_head(g + 1, h)
        return tuple(new_gmax), tuple(new_ms)

    first = tuple(score_head(0, h) for h in range(len(heads)))
    gmax, ms = lax.fori_loop(0, n_groups - 1, pipelined, (first, tuple(m_init)))
    for h in range(len(heads)):
        value_head(n_groups - 1, h, gmax[h], ms[h])

    half = lanes // 2
    outs = []
    for pp in range(n_pairs):
        outs.append(acc_ref[2 * pp, :half, :] / acc_ref[2 * pp, lanes:lanes + 1, :])
        outs.append(acc_ref[2 * pp + 1, half:lanes, :] / acc_ref[2 * pp + 1, lanes:lanes + 1, :])
    o_ref[0] = jnp.concatenate(outs, axis=0).T.astype(BF16)


def _moba(q, k, v):
    B, S, _ = q.shape
    nb = S // MOBA_BLOCK
    assert nb % MOBA_GROUP == 0
    nbp = -(-nb // (2 * V7X_SUBLANES)) * (2 * V7X_SUBLANES)
    n_pairs = ATT_WIDTH // V7X_LANES
    pp = MOBA_PAIRS
    assert n_pairs % pp == 0
    vt_rows = V7X_LANES + 2 * V7X_SUBLANES
    qspec = pl.BlockSpec((1, MOBA_BLOCK, pp * V7X_LANES), lambda b, p, i: (b, i, p))
    kvspec = pl.BlockSpec((1, S, pp * V7X_LANES), lambda b, p, i: (b, 0, p))
    return pl.pallas_call(
        functools.partial(_moba_kernel, nb=nb),
        grid=(B, n_pairs // pp, nb),
        in_specs=[qspec, kvspec, kvspec],
        out_specs=qspec,
        out_shape=jax.ShapeDtypeStruct((B, S, ATT_WIDTH), BF16),
        scratch_shapes=[pltpu.VMEM((pp, S, 2 * V7X_LANES), BF16),
                        pltpu.VMEM((pp, nbp, 3 * V7X_LANES), BF16),
                        pltpu.VMEM((pp, vt_rows, S), BF16),
                        pltpu.VMEM((2 * pp, vt_rows, MOBA_BLOCK), F32)]
                       + [pltpu.VMEM((S, MOBA_BLOCK), F32)] * (2 * pp),
        compiler_params=_cparams("parallel", "parallel", "arbitrary"),
        name="moba",
    )(q, k, v)


def _hgrn_level_sizes(chunk):
    return [chunk >> (i + 1) for i in range(int(math.log2(chunk)))]


def _hgrn_constants(chunk):
    t = np.arange(chunk)
    mats = [(t[None, :] <= t[:, None]),
            (t[None, :] > t[:, None])]
    qrows, pmasks = [], []
    for bs in _hgrn_level_sizes(chunk):
        blk = t // bs
        odd = (blk % 2) == 1
        lo, hi = blk * bs, (blk + 1) * bs
        u = t[None, :]
        m_odd = (u >= lo[:, None]) & (u <= t[:, None])
        m_even = (u > t[:, None]) & (u < hi[:, None])
        mats.append(np.where(odd[:, None], m_odd, m_even))
        qrows.append(odd)
        pmasks.append(odd[:, None] & (blk[None, :] == blk[:, None] - 1))
    pmasks.append(t[None, :] == t[:, None])
    summat = np.concatenate(mats, axis=0).astype(np.float32)
    qrow = np.stack(qrows, axis=0).astype(np.float32)
    pmask = np.stack(pmasks, axis=0).astype(np.float32)
    return summat, qrow, pmask


def _hgrn_kernel(q_ref, f_ref, i_ref, g_ref, lb_ref, nw_ref, sm_ref, qrow_ref, pm_ref,
                 o_ref, state_ref, *, rows):
    C = HGRN_CHUNK
    n_levels = qrow_ref.shape[0]

    @pl.when(pl.program_id(1) == 0)
    def _():
        state_ref[...] = jnp.zeros_like(state_ref)

    lb = lb_ref[...]
    summat = sm_ref[...]
    for c in range(rows // C):
        rs = slice(c * C, (c + 1) * C)
        fg = lb + (1.0 - lb) * jax.nn.sigmoid(f_ref[0, rs, :])
        logf = jnp.log(fg)
        hi = logf.astype(BF16)
        rem = logf - hi.astype(F32)
        mid = rem.astype(BF16)
        lo = (rem - mid.astype(F32)).astype(BF16)
        sums = jnp.dot(summat, jnp.concatenate([hi, mid, lo], axis=0),
                       preferred_element_type=F32)
        for h in range(HGRN_HEADS):
            ls = slice(h * HGRN_DIM, (h + 1) * HGRN_DIM)
            qf = jax.nn.silu(q_ref[0, rs, ls].astype(F32))
            kf = 1.0 - fg[:, ls]
            iv = i_ref[0, rs, ls]
            bcum = sums[0:C, ls]
            bsuf = sums[C:2 * C, ls]
            att = _nt_dot(qf.astype(BF16), kf.astype(BF16)) * pm_ref[n_levels]
            for lv in range(n_levels):
                w = jnp.exp(sums[(2 + lv) * C:(3 + lv) * C, ls])
                qrow = qrow_ref[lv]
                z = (jnp.where(qrow > 0.5, qf, kf) * w).astype(BF16)
                att = att + _nt_dot(z, z) * pm_ref[lv]
            o = jnp.dot(att.astype(BF16), iv, preferred_element_type=F32)
            st = state_ref[h]
            o = o + _nt_dot((qf * jnp.exp(bcum)).astype(BF16), st.astype(BF16))
            kdec = (kf * jnp.exp(bsuf)).astype(BF16)
            state_ref[h] = st * jnp.exp(bcum[C - 1:C, :]) + _tn_dot(iv, kdec)
            o = o * lax.rsqrt(jnp.mean(o * o, axis=-1, keepdims=True) + RMS_EPS)
            o = o * nw_ref[:, ls] * jax.nn.silu(g_ref[0, rs, ls].astype(F32))
            o_ref[0, rs, ls] = o.astype(BF16)


def _hgrn(qb, fb, ib, gb, lb, norm_w):
    B, S, W = qb.shape
    rows = min(HGRN_ROWS, S)
    summat, qrow, pmask = _hgrn_constants(HGRN_CHUNK)
    summat = jnp.asarray(np.concatenate([summat] * 3, axis=1), BF16)
    n_levels = qrow.shape[0]
    blk = pl.BlockSpec((1, rows, W), lambda b, s: (b, s, 0))
    vec = pl.BlockSpec((1, W), lambda b, s: (0, 0))
    const = lambda a: pl.BlockSpec(a.shape, lambda b, s: (0,) * a.ndim)
    qrow3 = qrow.reshape(n_levels, HGRN_CHUNK, 1)
    return pl.pallas_call(
        functools.partial(_hgrn_kernel, rows=rows),
        grid=(B, S // rows),
        in_specs=[blk, blk, blk, blk, vec, vec, const(summat), const(qrow3), const(pmask)],
        out_specs=blk,
        out_shape=jax.ShapeDtypeStruct((B, S, W), BF16),
        scratch_shapes=[pltpu.VMEM((HGRN_HEADS, HGRN_DIM, HGRN_DIM), F32)],
        compiler_params=_cparams("parallel", "arbitrary"),
        name="hgrn",
    )(qb, fb, ib, gb, lb, norm_w, jnp.asarray(summat), jnp.asarray(qrow3), jnp.asarray(pmask))


def _mix_route_kernel(ya_ref, yb_ref, ga_ref, gb_ref, x_ref, wa_ref, wb_ref, wo_ref, nw_ref,
                      rw_ref, rb_ref, tri_ref, anchor_ref,
                      x1_ref, h2_ref, e_ref, w_ref, rank_ref, cnt_ref, carry_ref):
    @pl.when(pl.program_id(0) == 0)
    def _():
        carry_ref[...] = jnp.zeros_like(carry_ref)

    ua = jnp.dot(ya_ref[...], wa_ref[...], preferred_element_type=F32)
    ub = jnp.dot(yb_ref[...], wb_ref[...], preferred_element_type=F32)
    mixed = (jax.nn.sigmoid(ga_ref[...].astype(F32)) * ua
             + jax.nn.sigmoid(gb_ref[...].astype(F32)) * ub)
    x1 = x_ref[...] + jnp.dot(mixed.astype(BF16), wo_ref[...], preferred_element_type=F32)
    x1_ref[...] = x1
    h2 = x1 * lax.rsqrt(jnp.mean(x1 * x1, axis=-1, keepdims=True) + RMS_EPS) * nw_ref[...]
    half = h2.shape[1] // 2
    h2_ref[...] = _pack_bf16_pair(h2[:, :half], h2[:, half:])

    tm = x1.shape[0]
    h2_hi = h2.astype(BF16)
    h2_lo = (h2 - h2_hi.astype(F32)).astype(BF16)
    logits = _nt_dot(rw_ref[...], jnp.concatenate([h2_hi, h2_lo, h2_hi], axis=1)) + rb_ref[...]
    eid = lax.broadcasted_iota(jnp.int32, (N_EXPERTS, tm), 0)
    work = logits
    es, vs = [], []
    for _ in range(TOP_K):
        mx = jnp.max(work, axis=0, keepdims=True)
        idx = jnp.min(jnp.where(work == mx, eid, N_EXPERTS), axis=0, keepdims=True)
        es.append(idx)
        vs.append(mx)
        work = jnp.where(eid == idx, -jnp.inf, work)
    ex = [jnp.exp(v - vs[0]) for v in vs]
    den = ex[0] + ex[1] + ex[2] + ex[3]
    multi = jnp.zeros((N_EXPERTS, tm), F32)
    for k in range(TOP_K):
        multi = multi + jnp.where(eid == es[k], 1.0, 0.0)
    before = jnp.dot(multi.astype(BF16), tri_ref[...], preferred_element_type=F32) + carry_ref[...]
    for k in range(TOP_K):
        e_ref[k:k + 1, :] = es[k]
        w_ref[k:k + 1, :] = ex[k] / den
        rank_ref[k:k + 1, :] = jnp.sum(jnp.where(eid == es[k], before, 0.0), axis=0,
                                       keepdims=True).astype(jnp.int32)
    carry_ref[...] = carry_ref[...] + jnp.sum(multi, axis=1, keepdims=True)
    cnt_ref[...] = jnp.broadcast_to(carry_ref[...], cnt_ref.shape)


def _mix_route(ya, yb, ga, gb, x2, wa, wb, wo, norm_w, rw_t, rb, row0, rows, anchor):
    D = x2.shape[1]
    T = rows
    tm = min(MIX_ROWS, T)
    row = lambda w: pl.BlockSpec((tm, w), lambda i: (i + row0 // tm, 0))
    out_row = pl.BlockSpec((tm, D), lambda i: (i, 0))
    whole = pl.BlockSpec(memory_space=pltpu.VMEM)
    kt = pl.BlockSpec((TOP_K, tm), lambda i: (0, i))
    tri = jnp.asarray(np.triu(np.ones((tm, tm), np.float32), 1), BF16)
    return pl.pallas_call(
        _mix_route_kernel,
        grid=(T // tm,),
        in_specs=[row(ATT_WIDTH), row(HGRN_WIDTH), row(D), row(D), row(D),
                  whole, whole, whole, pl.BlockSpec((1, D), lambda i: (0, 0)),
                  whole, whole, whole, pl.BlockSpec(memory_space=pltpu.SMEM)],
        out_specs=[out_row, pl.BlockSpec((tm, D // 2), lambda i: (i, 0)), kt, kt, kt,
                   pl.BlockSpec((N_EXPERTS, V7X_LANES), lambda i: (0, 0))],
        out_shape=[jax.ShapeDtypeStruct((T, D), F32), jax.ShapeDtypeStruct((T, D // 2), jnp.uint32),
                   jax.ShapeDtypeStruct((TOP_K, T), jnp.int32),
                   jax.ShapeDtypeStruct((TOP_K, T), F32),
                   jax.ShapeDtypeStruct((TOP_K, T), jnp.int32),
                   jax.ShapeDtypeStruct((N_EXPERTS, V7X_LANES), F32)],
        scratch_shapes=[pltpu.VMEM((N_EXPERTS, 1), F32)],
        compiler_params=_cparams("arbitrary"),
        name="mix_route",
    )(ya, yb, ga, gb, x2, wa, wb, wo, norm_w, rw_t, rb, tri, anchor)


def _sc_worker_id():
    return lax.axis_index("s") * V7X_SC_CORES + lax.axis_index("c")


def _sc_kernel(body, out_rows, like, window, name, extra_scratch=()):
    mesh = plsc.VectorSubcoreMesh(core_axis_name="c", subcore_axis_name="s")
    return pl.kernel(
        body, mesh=mesh,
        out_type=jax.ShapeDtypeStruct((out_rows,) + like.shape[1:], like.dtype),
        scratch_types=[pltpu.VMEM((window,), jnp.int32),
                       pltpu.VMEM((window,) + like.shape[1:], like.dtype), *extra_scratch],
        name=name)


def _scatter_rows(dest, h2, out_rows, pad_rows):
    T = dest.shape[1]
    n_workers = V7X_SC_CORES * V7X_SC_SUBCORES
    per_worker = T // n_workers
    window = min(SC_ROWS, per_worker)
    n_pad = pad_rows.shape[1]
    assert pad_rows.shape[0] == n_workers and n_pad % window == 0

    def body(dest_hbm, src_hbm, pad_hbm, zero_hbm, dst_hbm, idx_v, rows_v, pad_v, zero_v, all_v, sem,
             fill_sem):
        del idx_v
        wid = _sc_worker_id()
        base = wid * per_worker

        pltpu.sync_copy(pad_hbm.at[pl.ds(wid * n_pad, n_pad)], pad_v)
        pltpu.sync_copy(zero_hbm, zero_v)
        fills = [pltpu.async_copy(zero_v, dst_hbm.at[pad_v.at[pl.ds(c * window, window)]],
                                     fill_sem)
                 for c in range(n_pad // window)]

        for k in range(TOP_K):
            pltpu.sync_copy(dest_hbm.at[pl.ds(k * T + base, per_worker)],
                            all_v.at[pl.ds(k * per_worker, per_worker)])

        @pl.loop(0, per_worker // window)
        def _(c):
            pltpu.sync_copy(src_hbm.at[pl.ds(base + c * window, window)], rows_v)
            copies = [pltpu.async_copy(
                rows_v, dst_hbm.at[all_v.at[pl.ds(k * per_worker + c * window, window)]], sem)
                for k in range(TOP_K)]
            for cp in copies:
                cp.wait()

        for f in fills:
            f.wait()

    zeros = jnp.zeros((window,) + h2.shape[1:], h2.dtype)
    extra = [pltpu.VMEM((n_pad,), jnp.int32), pltpu.VMEM(zeros.shape, zeros.dtype),
             pltpu.VMEM((TOP_K * per_worker,), jnp.int32), pltpu.SemaphoreType.DMA,
             pltpu.SemaphoreType.DMA]
    return _sc_kernel(body, out_rows, h2, window, "scatter_rows", extra)(
        dest.reshape(-1), h2, pad_rows.reshape(-1), zeros)


def _gather_rows(dest, yg):
    T = dest.shape[1]
    n_workers = V7X_SC_CORES * V7X_SC_SUBCORES
    per_worker = TOP_K * T // n_workers
    window = min(SC_ROWS, per_worker)

    def body(dest_hbm, src_hbm, dst_hbm, idx_v, rows_v, all_v):
        del idx_v
        base = _sc_worker_id() * per_worker
        pltpu.sync_copy(dest_hbm.at[pl.ds(base, per_worker)], all_v)

        @pl.loop(0, per_worker // window)
        def _(c):
            pltpu.sync_copy(src_hbm.at[all_v.at[pl.ds(c * window, window)]], rows_v)
            pltpu.sync_copy(rows_v, dst_hbm.at[pl.ds(base + c * window, window)])

    extra = [pltpu.VMEM((per_worker,), jnp.int32)]
    return _sc_kernel(body, TOP_K * T, yg, window, "gather_rows", extra)(dest.reshape(-1), yg)


def _experts_kernel(blk_ref, exp_ref, lo_ref, hi_ref, slot_ref, efirst_ref, enext_ref,
                    x_ref, wgu_hbm, bg_ref, bu_ref, wd_hbm, bd_ref, perm_ref, y_ref,
                    wgu_stage, wd_stage, wg_buf, wu_buf, wd_buf, sems):
    w = pl.program_id(0)
    lo = lo_ref[w]
    hi = hi_ref[w]
    slot = slot_ref[w]

    tm = x_ref.shape[0]
    n_out = V7X_MXU_DIM

    def weight_copies(e, s):
        return [pltpu.make_async_copy(wgu_hbm.at[e], wgu_stage.at[s], sems.at[s, 0]),
                pltpu.make_async_copy(wd_hbm.at[e], wd_stage.at[s], sems.at[s, 1])]

    @pl.when(w == 0)
    def _():
        for c in weight_copies(exp_ref[0], slot):
            c.start()

    @pl.when(efirst_ref[w] == 1)
    def _():
        for c in weight_copies(exp_ref[w], slot):
            c.wait()

        @pl.when(enext_ref[w] >= 0)
        def _():
            for c in weight_copies(enext_ref[w], 1 - slot):
                c.start()

        half = V7X_MXU_DIM // 2
        for g in range(wgu_stage.shape[2] // V7X_MXU_DIM):
            wb = wgu_stage[slot, :, g * V7X_MXU_DIM:(g + 1) * V7X_MXU_DIM].astype(BF16)
            d = jnp.dot(wb, perm_ref[...], preferred_element_type=F32).astype(BF16)
            wg_buf[:, g * half:(g + 1) * half] = d[:, :half]
            wu_buf[:, g * half:(g + 1) * half] = d[:, half:]
        wd_buf[...] = wd_stage[slot].astype(BF16)

    sb = EXPERT_SUB_ROWS

    def ffn(r0):
        rs = slice(r0, r0 + sb)
        x = jnp.concatenate(_unpack_bf16_pair(x_ref[rs, :]), axis=1).astype(BF16)
        g = jnp.dot(x, wg_buf[...], preferred_element_type=F32) + bg_ref[0]
        u = jnp.dot(x, wu_buf[...], preferred_element_type=F32) + bu_ref[0]
        g = jnp.minimum(g, SWIGLU_LIMIT)
        u = jnp.clip(u, -SWIGLU_LIMIT, SWIGLU_LIMIT)
        act = ((u + 1.0) * (g * jax.nn.sigmoid(g * SWIGLU_ALPHA))).astype(BF16)
        half = y_ref.shape[1]

        def down(c0):
            cs = slice(c0, c0 + n_out)
            return (jnp.dot(act, wd_buf[:, cs], preferred_element_type=F32)
                    + bd_ref[0, :, cs])

        for c in range(half // n_out):
            cs = slice(c * n_out, (c + 1) * n_out)
            y_ref[rs, cs] = _pack_bf16_pair(down(c * n_out), down(half + c * n_out))

    n_sub = tm // sb
    fused = (lo == 0) & (hi == tm)

    @pl.when(fused)
    def _():
        for j in range(n_sub):
            ffn(j * sb)

    for j in range(n_sub):
        @pl.when(jnp.logical_not(fused) & (lo <= j * sb) & (hi > j * sb))
        def _():
            ffn(j * sb)


def _work_items(counts, n_rows, tm):
    nblk = n_rows // tm
    n_items = nblk + N_EXPERTS - 1
    end = jnp.cumsum(counts)
    start = end - counts
    fb = start // tm
    nitems = jnp.where(counts > 0, (end - 1) // tm - fb + 1, 0)
    item_end = jnp.cumsum(nitems)
    item_start = item_end - nitems
    w = jnp.arange(n_items, dtype=jnp.int32)
    valid = w < item_end[-1]
    wc = jnp.minimum(w, item_end[-1] - 1)
    e = jnp.sum(wc[:, None] >= item_end[None, :], axis=1).astype(jnp.int32)
    e = jnp.minimum(e, N_EXPERTS - 1)
    onehot = e[:, None] == jnp.arange(N_EXPERTS, dtype=jnp.int32)[None, :]
    pick = lambda table: jnp.sum(jnp.where(onehot, table[None, :], 0), axis=1)
    blk = (pick(fb) + (wc - pick(item_start))).astype(jnp.int32)
    lo = jnp.maximum(pick(start), blk * tm) - blk * tm
    hi = jnp.minimum(pick(end), (blk + 1) * tm) - blk * tm
    lo = jnp.where(valid, lo, 0).astype(jnp.int32)
    hi = jnp.where(valid, hi, 0).astype(jnp.int32)
    used = counts > 0
    ids = jnp.arange(N_EXPERTS, dtype=jnp.int32)
    slot = pick((jnp.cumsum(used) - 1) % 2).astype(jnp.int32)
    efirst = (valid & (w == pick(item_start))).astype(jnp.int32)
    later = jnp.where(used[None, :] & (ids[None, :] > ids[:, None]), ids[None, :], N_EXPERTS)
    next_used = jnp.min(later, axis=1)
    enext = pick(jnp.where(next_used < N_EXPERTS, next_used, -1)).astype(jnp.int32)
    return blk, e, lo, hi, slot, efirst, enext


def _gate_up_split_matrix():
    i = np.arange(V7X_MXU_DIM)
    src = np.where(i < V7X_MXU_DIM // 2, 2 * i, 2 * (i - V7X_MXU_DIM // 2) + 1)
    perm = np.zeros((V7X_MXU_DIM, V7X_MXU_DIM), np.float32)
    perm[src, i] = 1.0
    return jnp.asarray(perm, BF16)


def _experts(xg, counts, w_gate_up, bg, bu, w_down, bd):
    A = xg.shape[0]
    D = 2 * xg.shape[1]
    tm = min(EXPERT_ROWS, A)
    F = w_down.shape[1]
    items = _work_items(counts, A, tm)
    n_items = A // tm + N_EXPERTS - 1
    xs = pl.BlockSpec((tm, D // 2), lambda w, blk, e, *_: (blk[w], 0))
    bias = lambda c: pl.BlockSpec((1, 1, c), lambda w, blk, e, *_: (e[w], 0, 0))
    hbm = pl.BlockSpec(memory_space=pl.ANY)
    return pl.pallas_call(
        _experts_kernel,
        grid_spec=pltpu.PrefetchScalarGridSpec(
            num_scalar_prefetch=len(items),
            grid=(n_items,),
            in_specs=[xs, hbm, bias(F), bias(F), hbm, bias(D), pl.BlockSpec(memory_space=pltpu.VMEM)],
            out_specs=xs,
            scratch_shapes=[pltpu.VMEM((2, D, 2 * F), F32), pltpu.VMEM((2, F, D), F32),
                            pltpu.VMEM((D, F), BF16), pltpu.VMEM((D, F), BF16),
                            pltpu.VMEM((F, D), BF16), pltpu.SemaphoreType.DMA((2, 2))],
        ),
        out_shape=jax.ShapeDtypeStruct(xg.shape, jnp.uint32),
        compiler_params=_cparams("arbitrary"),
        name="experts",
    )(*items, xg, w_gate_up, bg, bu, w_down, bd, _gate_up_split_matrix())


def _combine_kernel(x1_ref, yk_ref, w_ref, nw_ref, *rest):
    o_ref = rest[-1]
    w = w_ref[...]
    x = x1_ref[...]
    for k in range(TOP_K):
        x = x + jnp.concatenate(_unpack_bf16_pair(yk_ref[k]), axis=1) * w[:, k:k + 1]
    o_ref[...] = x * lax.rsqrt(jnp.mean(x * x, axis=-1, keepdims=True) + RMS_EPS) * nw_ref[...]


def _combine(x1, yk, w_tk, norm_w, out_rows, row0, prev_out):
    Tg, D = x1.shape
    tm = min(COMBINE_ROWS, Tg)
    row = pl.BlockSpec((tm, D), lambda i: (i, 0))
    in_specs = [row, pl.BlockSpec((TOP_K, tm, D // 2), lambda i: (0, i, 0)),
                pl.BlockSpec((tm, TOP_K), lambda i: (i, 0)),
                pl.BlockSpec((1, D), lambda i: (0, 0))]
    args = [x1, yk, w_tk, norm_w]
    aliases = {}
    if prev_out is not None:
        in_specs.append(pl.BlockSpec(memory_space=pl.ANY))
        args.append(prev_out)
        aliases = {len(args) - 1: 0}
    return pl.pallas_call(
        _combine_kernel,
        grid=(Tg // tm,),
        in_specs=in_specs,
        out_specs=pl.BlockSpec((tm, D), lambda i: (i + row0 // tm, 0)),
        out_shape=jax.ShapeDtypeStruct((out_rows, D), F32),
        input_output_aliases=aliases,
        compiler_params=_cparams("parallel"),
        name="combine",
    )(*args)


def _qk_column_order():
    half = ATT_HEAD_DIM // 2
    order = []
    for p in range(ATT_HEADS // 2):
        for part in range(2):
            for h in (2 * p, 2 * p + 1):
                order.extend(range(h * ATT_HEAD_DIM + part * half, h * ATT_HEAD_DIM + (part + 1) * half))
    return np.asarray(order, np.int32)


def _rope_tables(seq):
    half = ATT_HEAD_DIM // 2
    inv = ROPE_THETA ** (-(jnp.arange(half, dtype=F32) * 2.0 / ATT_HEAD_DIM))
    ang = jnp.arange(seq, dtype=F32)[:, None] * inv[None, :]
    cos, sin = jnp.cos(ang), jnp.sin(ang)
    return (jnp.concatenate([cos, cos, cos, cos], axis=1),
            jnp.concatenate([-sin, -sin, sin, sin], axis=1))


def kernel(x, norm1_w, w_in, moba_up, hgrn_lb_logits, hgrn_norm_w, hgrn_up, w_out, norm2_w,
           router_w, router_b, w_gate_up, b_gate_up, w_down, b_down, final_norm_w):
    B, S, D = x.shape
    T = B * S
    assert S % MOBA_BLOCK == 0 and w_in.shape[0] == 1
    x2 = x.reshape(T, D)

    perm = _qk_column_order()
    w0 = w_in[0]
    w_in_p = jnp.concatenate([w0[:, :ATT_WIDTH][:, perm], w0[:, ATT_WIDTH:2 * ATT_WIDTH][:, perm],
                              w0[:, 2 * ATT_WIDTH:]], axis=1).astype(BF16)
    cos_t, sin_t = _rope_tables(S)
    lb = jnp.cumsum(jax.nn.softmax(hgrn_lb_logits.astype(F32), axis=0), axis=0)[0:1]

    qa, ka, va, qb, fb, ib, gb, ga, gtb = _in_proj(x2, norm1_w, w_in_p, cos_t, sin_t, S)
    r3 = lambda a: a.reshape(B, S, a.shape[1])

    rw_t = router_w[0].T
    rw_hi = rw_t.astype(BF16)
    rw_lo = (rw_t - rw_hi.astype(F32)).astype(BF16)
    mix_w = (moba_up[0].astype(BF16), hgrn_up[0].astype(BF16), w_out[0].astype(BF16), norm2_w,
             jnp.concatenate([rw_hi, rw_hi, rw_lo], axis=1), router_b[0][:, None])
    expert_w = (w_gate_up[0], b_gate_up[0][:, None, 0::2], b_gate_up[0][:, None, 1::2],
                w_down[0], b_down[0][:, None, :])

    ya = _moba(r3(qa), r3(ka), r3(va)).reshape(T, ATT_WIDTH)
    yb = _hgrn(r3(qb), r3(fb), r3(ib), r3(gb), lb, hgrn_norm_w).reshape(T, HGRN_WIDTH)

    Tg = T // MOE_TOKEN_GROUPS
    staged = []
    anchor = jnp.zeros((1,), jnp.int32)
    for g in range(MOE_TOKEN_GROUPS):
        x1, h2, top_e, top_w, rank, cnt = _mix_route(ya, yb, ga, gtb, x2, *mix_w, g * Tg, Tg,
                                                     anchor)
        counts = cnt[:, 0].astype(jnp.int32)
        sb = EXPERT_SUB_ROWS
        padded = (counts + sb - 1) // sb * sb
        start = jnp.cumsum(padded) - padded
        dest = rank + jnp.sum(jnp.where(top_e[:, :, None] == jnp.arange(N_EXPERTS)[None, None, :],
                                        start[None, None, :], 0), axis=-1)
        body_rows = TOP_K * Tg + N_EXPERTS * sb
        spare_row = body_rows
        i = jnp.arange(sb, dtype=jnp.int32)[None, :]
        n_fill = (padded - counts)[:, None]
        pad_rows = jnp.where(n_fill > 0, (start + counts)[:, None] + i % jnp.maximum(n_fill, 1),
                             spare_row + i)
        xg = _scatter_rows(dest, h2, body_rows + EXPERT_ROWS, pad_rows)
        staged.append((xg, padded, dest, x1, top_w))
        anchor = dest[0, :1] + pad_rows[0, :1]

    out = None
    for g, (xg, padded, dest, x1, top_w) in enumerate(staged):
        yg = _experts(xg, padded, *expert_w)
        yk = _gather_rows(dest, yg).reshape(TOP_K, Tg, D // 2)
        out = _combine(x1, yk, top_w.T, final_norm_w[None, :], T, g * Tg, out)
    return out.reshape(B, S, D)
```
